```python
import math
import jax, jax.numpy as jnp
from jax import lax
import numpy as np

D_MODEL = 2048
BATCH = 8
SEQ = 8192
DEPTH = 1

N_META = 16
CHUNK = 64
EPS = 1e-6
HG_HEADS = 8
HG_DK = 128
HG_DV = 128
HG_KEY_W = HG_HEADS * HG_DK
HG_VAL_W = HG_HEADS * HG_DV
GD_HEADS = 8
GD_DK = 128
GD_DV = 128
GD_KEY_W = GD_HEADS * GD_DK
GD_VAL_W = GD_HEADS * GD_DV
CONV_K = 4
GD_CONV_DIM = 2 * GD_KEY_W + GD_VAL_W
D_FF = ((-(-8 * D_MODEL // 3) + 255) // 256) * 256
SPLIT_SIZES = (HG_KEY_W, HG_KEY_W, HG_VAL_W, HG_VAL_W,
               GD_KEY_W, GD_KEY_W, GD_VAL_W, GD_VAL_W, GD_HEADS, GD_HEADS,
               D_MODEL, D_MODEL)
SPLIT_POINTS = tuple(int(s) for s in np.cumsum(SPLIT_SIZES)[:-1])
IN_DIM = int(sum(SPLIT_SIZES))

kernel_name = 'hybrid_hgrn2_gdn_block'


def rms_norm(x, w):
    xf = x.astype(jnp.float32)
    y = xf * lax.rsqrt(jnp.mean(xf * xf, axis=-1, keepdims=True) + EPS)
    return (y * w.astype(jnp.float32)).astype(x.dtype)


def to_heads(t, heads, d):
    bsz, length, _ = t.shape
    return t.reshape(bsz, length, heads, d).transpose(0, 2, 1, 3)


def l2_normalize(t):
    return t * lax.rsqrt(jnp.sum(t * t, axis=-1, keepdims=True) + EPS)


def gated_head_norm(o, gate, w):
    bsz, heads, length, d = o.shape
    o = jnp.swapaxes(o, 1, 2)
    o = o * lax.rsqrt(jnp.mean(o * o, axis=-1, keepdims=True) + EPS) * w.astype(jnp.float32)
    g = gate.astype(jnp.float32).reshape(bsz, length, heads, d)
    return (o * jax.nn.silu(g)).reshape(bsz, length, heads * d)


def causal_depthwise_conv(x, w):
    channels = x.shape[-1]
    return lax.conv_general_dilated(
        x, w[:, None, :].astype(x.dtype), window_strides=(1,), padding=[(CONV_K - 1, 0)],
        dimension_numbers=('NWC', 'WIO', 'NWC'), feature_group_count=channels)


def run_chunked(step, state, xs):
    meta = tuple(t[:, :, :N_META] for t in xs)
    real = tuple(t[:, :, N_META:] for t in xs)
    state, o_meta = step(state, meta)
    n_chunks = real[0].shape[2] // CHUNK

    def to_chunks(t):
        t = t.reshape(t.shape[:2] + (n_chunks, CHUNK) + t.shape[3:])
        return jnp.moveaxis(t, 2, 0)

    _, o_real = lax.scan(step, state, tuple(to_chunks(t) for t in real))
    o_real = jnp.moveaxis(o_real, 0, 2)
    o_real = o_real.reshape(o_real.shape[:2] + (n_chunks * CHUNK,) + o_real.shape[4:])
    return jnp.concatenate([o_meta, o_real], axis=2)


def hgrn2_chunk(state, inp):
    q, k, v, log_f = inp
    c = q.shape[2]
    b = jnp.cumsum(log_f, axis=2)
    mask = jnp.tril(jnp.ones((c, c), dtype=bool))
    diff = b[:, :, :, None, :] - b[:, :, None, :, :]
    decay = jnp.where(mask[:, :, None], jnp.exp(jnp.minimum(diff, 0.0)), 0.0)
    scores = jnp.einsum('bhtd,bhjd,bhtjd->bhtj', q, k, decay)
    o = (jnp.einsum('bhtd,bhde->bhte', q * jnp.exp(b), state)
         + jnp.einsum('bhtj,bhje->bhte', scores, v))
    b_last = b[:, :, -1:, :]
    new_state = (jnp.exp(b_last[:, :, 0, :])[..., None] * state
                 + jnp.einsum('bhjd,bhje->bhde', k * jnp.exp(b_last - b), v))
    return new_state, o


def gdn_chunk(state, inp):
    q, k, v, g, beta = inp
    c = q.shape[2]
    gc = jnp.cumsum(g, axis=-1)
    diff = gc[..., :, None] - gc[..., None, :]
    rel = jnp.exp(jnp.minimum(diff, 0.0))
    strict = jnp.tril(jnp.ones((c, c), dtype=bool), -1)
    incl = jnp.tril(jnp.ones((c, c), dtype=bool))
    kk = jnp.einsum('bhid,bhjd->bhij', k, k)
    a = jnp.where(strict, beta[..., :, None] * kk * rel, 0.0)
    eye = jnp.eye(c, dtype=q.dtype)
    rhs = jnp.concatenate([(beta * jnp.exp(gc))[..., None] * k, beta[..., None] * v], axis=-1)
    wu = lax.linalg.triangular_solve(eye + a, rhs, left_side=True, lower=True)
    w, u = wu[..., :GD_DK], wu[..., GD_DK:]
    v_new = u - jnp.einsum('bhtd,bhde->bhte', w, state)
    attn = jnp.where(incl, jnp.einsum('bhtd,bhjd->bhtj', q, k) * rel, 0.0)
    o = (jnp.einsum('bhtd,bhde->bhte', q * jnp.exp(gc)[..., None], state)
         + jnp.einsum('bhtj,bhje->bhte', attn, v_new))
    g_last = gc[..., -1:]
    new_state = (jnp.exp(g_last)[..., None] * state
                 + jnp.einsum('bhjd,bhje->bhde', k * jnp.exp(g_last - gc)[..., None], v_new))
    return new_state, o


def hgrn2_mixer(q_raw, f_raw, i_raw, g_raw, lower_bound, norm_w):
    bsz = q_raw.shape[0]
    f32 = jnp.float32
    fs = f_raw.astype(f32)
    lb = lower_bound.astype(f32)
    q = jax.nn.silu(q_raw.astype(f32))
    log_f = jnp.log(lb + (1.0 - lb) * jax.nn.sigmoid(fs))
    k = (1.0 - lb) * jax.nn.sigmoid(-fs)
    v = i_raw.astype(f32)
    q = to_heads(q, HG_HEADS, HG_DK)
    k = to_heads(k, HG_HEADS, HG_DK)
    log_f = to_heads(log_f, HG_HEADS, HG_DK)
    v = to_heads(v, HG_HEADS, HG_DV)
    state0 = jnp.zeros((bsz, HG_HEADS, HG_DK, HG_DV), f32)
    o = run_chunked(hgrn2_chunk, state0, (q, k, v, log_f))
    return gated_head_norm(o, g_raw, norm_w)


def gated_deltanet_mixer(q_raw, k_raw, v_raw, z_raw, a_raw, b_raw, conv_w, a_log, dt_bias, norm_w):
    bsz = q_raw.shape[0]
    f32 = jnp.float32
    qkv = causal_depthwise_conv(jnp.concatenate([q_raw, k_raw, v_raw], axis=-1), conv_w)
    qkv = jax.nn.silu(qkv.astype(f32))
    q = l2_normalize(to_heads(qkv[..., :GD_KEY_W], GD_HEADS, GD_DK)) * (GD_DK ** -0.5)
    k = l2_normalize(to_heads(qkv[..., GD_KEY_W:2 * GD_KEY_W], GD_HEADS, GD_DK))
    v = to_heads(qkv[..., 2 * GD_KEY_W:], GD_HEADS, GD_DV)
    beta = jnp.swapaxes(jax.nn.sigmoid(b_raw.astype(f32)), 1, 2)
    g = -jnp.exp(a_log.astype(f32)) * jax.nn.softplus(a_raw.astype(f32) + dt_bias.astype(f32))
    g = jnp.swapaxes(g, 1, 2)
    state0 = jnp.zeros((bsz, GD_HEADS, GD_DK, GD_DV), f32)
    o = run_chunked(gdn_chunk, state0, (q, k, v, g, beta))
    return gated_head_norm(o, z_raw, norm_w)


def hybrid_mixer(h, norm_w, w_in, lower_bound, hg_norm_w, conv_w, a_log, dt_bias, gd_norm_w,
                 w_branch_a, w_branch_b, w_out):
    xn = rms_norm(h, norm_w)
    proj = jnp.einsum('bld,de->ble', xn, w_in)
    (hq, hf, hi, hg, gq, gk, gv, gz, ga, gb, gate_a, gate_b) = jnp.split(proj, SPLIT_POINTS, axis=-1)
    o_a = hgrn2_mixer(hq, hf, hi, hg, lower_bound, hg_norm_w).astype(h.dtype)
    o_b = gated_deltanet_mixer(gq, gk, gv, gz, ga, gb, conv_w, a_log, dt_bias, gd_norm_w).astype(h.dtype)
    merged = jax.nn.sigmoid(gate_a) * (o_a @ w_branch_a) + jax.nn.sigmoid(gate_b) * (o_b @ w_branch_b)
    return merged @ w_out


def swiglu_ffn(x, w_in, w_out):
    gate, up = jnp.split(x @ w_in, 2, axis=-1)
    return (jax.nn.silu(gate) * up) @ w_out


def _fwd_setup_inputs(seed: int = 0) -> dict:
    key = jax.random.key(seed)
    ks = jax.random.split(key, 17)
    f32 = jnp.float32

    def nrm(k, shape, scale):
        return jax.random.normal(k, shape, f32) * scale

    x = nrm(ks[0], (BATCH, SEQ, D_MODEL), 1.0)
    meta_tokens = nrm(ks[1], (N_META, D_MODEL), 1.0)
    lb_logits = nrm(ks[2], (DEPTH + 1, HG_KEY_W), 0.5)
    mix_norm_w = 1.0 + nrm(ks[3], (DEPTH, D_MODEL), 0.02)
    w_in = nrm(ks[4], (DEPTH, D_MODEL, IN_DIM), D_MODEL ** -0.5)
    hg_norm_w = 1.0 + nrm(ks[5], (DEPTH, HG_DV), 0.02)
    gd_conv_w = nrm(ks[6], (DEPTH, CONV_K, GD_CONV_DIM), CONV_K ** -0.5)
    gd_a_log = jnp.log(jax.random.uniform(ks[7], (DEPTH, GD_HEADS), f32, 1.0, 16.0))
    dt = jnp.exp(jax.random.uniform(ks[8], (DEPTH, GD_HEADS), f32, math.log(1e-3), math.log(1e-1)))
    gd_dt_bias = dt + jnp.log(-jnp.expm1(-dt))
    gd_norm_w = 1.0 + nrm(ks[9], (DEPTH, GD_DV), 0.02)
    w_branch_a = nrm(ks[10], (DEPTH, HG_VAL_W, D_MODEL), HG_VAL_W ** -0.5)
    w_branch_b = nrm(ks[11], (DEPTH, GD_VAL_W, D_MODEL), GD_VAL_W ** -0.5)
    w_out = nrm(ks[12], (DEPTH, D_MODEL, D_MODEL), D_MODEL ** -0.5)
    ffn_norm_w = 1.0 + nrm(ks[13], (DEPTH, D_MODEL), 0.02)
    w_ffn_in = nrm(ks[14], (DEPTH, D_MODEL, 2 * D_FF), D_MODEL ** -0.5)
    w_ffn_out = nrm(ks[15], (DEPTH, D_FF, D_MODEL), D_FF ** -0.5)
    final_norm_w = 1.0 + nrm(ks[16], (D_MODEL,), 0.02)
    return {'x': x, 'meta_tokens': meta_tokens, 'lb_logits': lb_logits, 'mix_norm_w': mix_norm_w,
            'w_in': w_in, 'hg_norm_w': hg_norm_w, 'gd_conv_w': gd_conv_w, 'gd_a_log': gd_a_log,
            'gd_dt_bias': gd_dt_bias, 'gd_norm_w': gd_norm_w, 'w_branch_a': w_branch_a,
            'w_branch_b': w_branch_b, 'w_out': w_out, 'ffn_norm_w': ffn_norm_w,
            'w_ffn_in': w_ffn_in, 'w_ffn_out': w_ffn_out, 'final_norm_w': final_norm_w}


def _fwd_reference(x, meta_tokens, lb_logits, mix_norm_w, w_in, hg_norm_w, gd_conv_w, gd_a_log,
              gd_dt_bias, gd_norm_w, w_branch_a, w_branch_b, w_out, ffn_norm_w, w_ffn_in,
              w_ffn_out, final_norm_w):
    bsz = x.shape[0]
    meta = jnp.broadcast_to(meta_tokens[None].astype(x.dtype), (bsz, N_META, D_MODEL))
    h = jnp.concatenate([meta, x], axis=1)
    lower_bounds = jnp.cumsum(jax.nn.softmax(lb_logits.astype(jnp.float32), axis=0), axis=0)
    for layer in range(DEPTH):
        h = h + hybrid_mixer(h, mix_norm_w[layer], w_in[layer], lower_bounds[layer],
                             hg_norm_w[layer], gd_conv_w[layer], gd_a_log[layer],
                             gd_dt_bias[layer], gd_norm_w[layer], w_branch_a[layer],
                             w_branch_b[layer], w_out[layer])
        if layer == DEPTH - 1:
            h = h[:, N_META:]
        h = h + swiglu_ffn(rms_norm(h, ffn_norm_w[layer]), w_ffn_in[layer], w_ffn_out[layer])
    return rms_norm(h, final_norm_w)


import jax as _jax
import jax.numpy as _jnp

TWIN_FORMAT = 'train_step'
FWD_PARAMS = ['x', 'meta_tokens', 'lb_logits', 'mix_norm_w', 'w_in', 'hg_norm_w', 'gd_conv_w', 'gd_a_log', 'gd_dt_bias', 'gd_norm_w', 'w_branch_a', 'w_branch_b', 'w_out', 'ffn_norm_w', 'w_ffn_in', 'w_ffn_out', 'final_norm_w']
TWIN_WEIGHTS = ['meta_tokens', 'lb_logits', 'mix_norm_w', 'w_in', 'hg_norm_w', 'gd_conv_w', 'gd_a_log', 'gd_dt_bias', 'gd_norm_w', 'w_branch_a', 'w_branch_b', 'w_out', 'ffn_norm_w', 'w_ffn_in', 'w_ffn_out', 'final_norm_w']
TWIN_DIFF_INPUT = 'x'
TWIN_INPUTS = ['x', 'meta_tokens', 'lb_logits', 'mix_norm_w', 'w_in', 'hg_norm_w', 'gd_conv_w', 'gd_a_log', 'gd_dt_bias', 'gd_norm_w', 'w_branch_a', 'w_branch_b', 'w_out', 'ffn_norm_w', 'w_ffn_in', 'w_ffn_out', 'final_norm_w', 'loss_target', 'm_meta_tokens', 'm_lb_logits', 'm_mix_norm_w', 'm_w_in', 'm_hg_norm_w', 'm_gd_conv_w', 'm_gd_a_log', 'm_gd_dt_bias', 'm_gd_norm_w', 'm_w_branch_a', 'm_w_branch_b', 'm_w_out', 'm_ffn_norm_w', 'm_w_ffn_in', 'm_w_ffn_out', 'm_final_norm_w', 'v_meta_tokens', 'v_lb_logits', 'v_mix_norm_w', 'v_w_in', 'v_hg_norm_w', 'v_gd_conv_w', 'v_gd_a_log', 'v_gd_dt_bias', 'v_gd_norm_w', 'v_w_branch_a', 'v_w_branch_b', 'v_w_out', 'v_ffn_norm_w', 'v_w_ffn_in', 'v_w_ffn_out', 'v_final_norm_w']
TWIN_OUTPUTS = ['loss', 'grad_x', 'grad_meta_tokens', 'grad_lb_logits', 'grad_mix_norm_w', 'grad_w_in', 'grad_hg_norm_w', 'grad_gd_conv_w', 'grad_gd_a_log', 'grad_gd_dt_bias', 'grad_gd_norm_w', 'grad_w_branch_a', 'grad_w_branch_b', 'grad_w_out', 'grad_ffn_norm_w', 'grad_w_ffn_in', 'grad_w_ffn_out', 'grad_final_norm_w', 'delta_meta_tokens', 'delta_lb_logits', 'delta_mix_norm_w', 'delta_w_in', 'delta_hg_norm_w', 'delta_gd_conv_w', 'delta_gd_a_log', 'delta_gd_dt_bias', 'delta_gd_norm_w', 'delta_w_branch_a', 'delta_w_branch_b', 'delta_w_out', 'delta_ffn_norm_w', 'delta_w_ffn_in', 'delta_w_ffn_out', 'delta_final_norm_w', 'new_m_meta_tokens', 'new_m_lb_logits', 'new_m_mix_norm_w', 'new_m_w_in', 'new_m_hg_norm_w', 'new_m_gd_conv_w', 'new_m_gd_a_log', 'new_m_gd_dt_bias', 'new_m_gd_norm_w', 'new_m_w_branch_a', 'new_m_w_branch_b', 'new_m_w_out', 'new_m_ffn_norm_w', 'new_m_w_ffn_in', 'new_m_w_ffn_out', 'new_m_final_norm_w', 'new_v_meta_tokens', 'new_v_lb_logits', 'new_v_mix_norm_w', 'new_v_w_in', 'new_v_hg_norm_w', 'new_v_gd_conv_w', 'new_v_gd_a_log', 'new_v_gd_dt_bias', 'new_v_gd_norm_w', 'new_v_w_branch_a', 'new_v_w_branch_b', 'new_v_w_out', 'new_v_ffn_norm_w', 'new_v_w_ffn_in', 'new_v_w_ffn_out', 'new_v_final_norm_w']
TWIN_LEAF_KINDS = {'loss': 'loss', 'grad_x': 'grad_x', 'grad_meta_tokens': 'grad_w', 'grad_lb_logits': 'grad_w', 'grad_mix_norm_w': 'grad_w', 'grad_w_in': 'grad_w', 'grad_hg_norm_w': 'grad_w', 'grad_gd_conv_w': 'grad_w', 'grad_gd_a_log': 'grad_w', 'grad_gd_dt_bias': 'grad_w', 'grad_gd_norm_w': 'grad_w', 'grad_w_branch_a': 'grad_w', 'grad_w_branch_b': 'grad_w', 'grad_w_out': 'grad_w', 'grad_ffn_norm_w': 'grad_w', 'grad_w_ffn_in': 'grad_w', 'grad_w_ffn_out': 'grad_w', 'grad_final_norm_w': 'grad_w', 'delta_meta_tokens': 'delta_w', 'delta_lb_logits': 'delta_w', 'delta_mix_norm_w': 'delta_w', 'delta_w_in': 'delta_w', 'delta_hg_norm_w': 'delta_w', 'delta_gd_conv_w': 'delta_w', 'delta_gd_a_log': 'delta_w', 'delta_gd_dt_bias': 'delta_w', 'delta_gd_norm_w': 'delta_w', 'delta_w_branch_a': 'delta_w', 'delta_w_branch_b': 'delta_w', 'delta_w_out': 'delta_w', 'delta_ffn_norm_w': 'delta_w', 'delta_w_ffn_in': 'delta_w', 'delta_w_ffn_out': 'delta_w', 'delta_final_norm_w': 'delta_w', 'new_m_meta_tokens': 'new_m', 'new_m_lb_logits': 'new_m', 'new_m_mix_norm_w': 'new_m', 'new_m_w_in': 'new_m', 'new_m_hg_norm_w': 'new_m', 'new_m_gd_conv_w': 'new_m', 'new_m_gd_a_log': 'new_m', 'new_m_gd_dt_bias': 'new_m', 'new_m_gd_norm_w': 'new_m', 'new_m_w_branch_a': 'new_m', 'new_m_w_branch_b': 'new_m', 'new_m_w_out': 'new_m', 'new_m_ffn_norm_w': 'new_m', 'new_m_w_ffn_in': 'new_m', 'new_m_w_ffn_out': 'new_m', 'new_m_final_norm_w': 'new_m', 'new_v_meta_tokens': 'new_v', 'new_v_lb_logits': 'new_v', 'new_v_mix_norm_w': 'new_v', 'new_v_w_in': 'new_v', 'new_v_hg_norm_w': 'new_v', 'new_v_gd_conv_w': 'new_v', 'new_v_gd_a_log': 'new_v', 'new_v_gd_dt_bias': 'new_v', 'new_v_gd_norm_w': 'new_v', 'new_v_w_branch_a': 'new_v', 'new_v_w_branch_b': 'new_v', 'new_v_w_out': 'new_v', 'new_v_ffn_norm_w': 'new_v', 'new_v_w_ffn_in': 'new_v', 'new_v_w_ffn_out': 'new_v', 'new_v_final_norm_w': 'new_v'}


def _forward(args):
    return _fwd_reference(*[args[k] for k in FWD_PARAMS])


def _output_shape():
    def fwd():
        inp = _fwd_setup_inputs(0)
        return _fwd_reference(*[inp[k] for k in FWD_PARAMS])
    out = _jax.eval_shape(fwd)
    return out.shape, out.dtype

N_MICROBATCH = 1
ADAM_LR = 0.001
ADAM_B1 = 0.9
ADAM_B2 = 0.999
ADAM_EPS = 1e-08
ADAM_WD = 0.01
ADAM_STEP = 10
PER_EXAMPLE_BATCH_AXIS = {'x': 0, 'loss_target': 0}
SHARED_INPUTS = []
_WEIGHT_DTYPES = {'meta_tokens': _jnp.float32, 'lb_logits': _jnp.float32, 'mix_norm_w': _jnp.float32, 'w_in': _jnp.float32, 'hg_norm_w': _jnp.float32, 'gd_conv_w': _jnp.float32, 'gd_a_log': _jnp.float32, 'gd_dt_bias': _jnp.float32, 'gd_norm_w': _jnp.float32, 'w_branch_a': _jnp.float32, 'w_branch_b': _jnp.float32, 'w_out': _jnp.float32, 'ffn_norm_w': _jnp.float32, 'w_ffn_in': _jnp.float32, 'w_ffn_out': _jnp.float32, 'final_norm_w': _jnp.float32}
MOMENT_SCALE = {'meta_tokens': 2.586696e-03, 'lb_logits': 5.772844e-03, 'mix_norm_w': 9.565688e-02, 'w_in': 3.817961e-02, 'hg_norm_w': 1.528378e-01, 'gd_conv_w': 4.498980e-02, 'gd_a_log': 1.722898e-01, 'gd_dt_bias': 1.731303e-01, 'gd_norm_w': 1.672194e-01, 'w_branch_a': 4.118564e-02, 'w_branch_b': 4.080172e-02, 'w_out': 5.807554e-02, 'ffn_norm_w': 9.061974e-02, 'w_ffn_in': 3.673327e-02, 'w_ffn_out': 5.988994e-02, 'final_norm_w': 3.195965e+01}


def _to_microbatches(a, axis):
    t = _jnp.moveaxis(a, axis, 0)
    t = t.reshape((N_MICROBATCH, t.shape[0] // N_MICROBATCH) + t.shape[1:])
    return _jnp.moveaxis(t, 1, axis + 1)


def setup_inputs(seed: int = 0) -> dict:
    inp = _fwd_setup_inputs(seed)
    key = _jax.random.fold_in(_jax.random.key(seed), 7919)
    shape, _ = _output_shape()
    out = dict(inp)
    out["loss_target"] = _jax.random.normal(_jax.random.fold_in(key, 0), shape, _jnp.float32)
    for i, name in enumerate(TWIN_WEIGHTS):
        w = inp[name].astype(_jnp.float32)
        if MOMENT_SCALE is None:
            s = _jnp.sqrt(_jnp.mean(_jnp.square(w)) + 1e-30)
        else:
            s = MOMENT_SCALE[name]
        km, kv = _jax.random.split(_jax.random.fold_in(key, i + 1))
        out[name] = w
        out["m_" + name] = s * _jax.random.normal(km, w.shape, _jnp.float32)
        out["v_" + name] = (s * s) * _jax.random.uniform(kv, w.shape, _jnp.float32, 0.5, 1.5)
    if N_MICROBATCH > 1:
        for name, axis in PER_EXAMPLE_BATCH_AXIS.items():
            out[name] = _to_microbatches(out[name], axis)
    return {'x': out['x'], 'meta_tokens': out['meta_tokens'], 'lb_logits': out['lb_logits'], 'mix_norm_w': out['mix_norm_w'], 'w_in': out['w_in'], 'hg_norm_w': out['hg_norm_w'], 'gd_conv_w': out['gd_conv_w'], 'gd_a_log': out['gd_a_log'], 'gd_dt_bias': out['gd_dt_bias'], 'gd_norm_w': out['gd_norm_w'], 'w_branch_a': out['w_branch_a'], 'w_branch_b': out['w_branch_b'], 'w_out': out['w_out'], 'ffn_norm_w': out['ffn_norm_w'], 'w_ffn_in': out['w_ffn_in'], 'w_ffn_out': out['w_ffn_out'], 'final_norm_w': out['final_norm_w'], 'loss_target': out['loss_target'], 'm_meta_tokens': out['m_meta_tokens'], 'm_lb_logits': out['m_lb_logits'], 'm_mix_norm_w': out['m_mix_norm_w'], 'm_w_in': out['m_w_in'], 'm_hg_norm_w': out['m_hg_norm_w'], 'm_gd_conv_w': out['m_gd_conv_w'], 'm_gd_a_log': out['m_gd_a_log'], 'm_gd_dt_bias': out['m_gd_dt_bias'], 'm_gd_norm_w': out['m_gd_norm_w'], 'm_w_branch_a': out['m_w_branch_a'], 'm_w_branch_b': out['m_w_branch_b'], 'm_w_out': out['m_w_out'], 'm_ffn_norm_w': out['m_ffn_norm_w'], 'm_w_ffn_in': out['m_w_ffn_in'], 'm_w_ffn_out': out['m_w_ffn_out'], 'm_final_norm_w': out['m_final_norm_w'], 'v_meta_tokens': out['v_meta_tokens'], 'v_lb_logits': out['v_lb_logits'], 'v_mix_norm_w': out['v_mix_norm_w'], 'v_w_in': out['v_w_in'], 'v_hg_norm_w': out['v_hg_norm_w'], 'v_gd_conv_w': out['v_gd_conv_w'], 'v_gd_a_log': out['v_gd_a_log'], 'v_gd_dt_bias': out['v_gd_dt_bias'], 'v_gd_norm_w': out['v_gd_norm_w'], 'v_w_branch_a': out['v_w_branch_a'], 'v_w_branch_b': out['v_w_branch_b'], 'v_w_out': out['v_w_out'], 'v_ffn_norm_w': out['v_ffn_norm_w'], 'v_w_ffn_in': out['v_w_ffn_in'], 'v_w_ffn_out': out['v_w_ffn_out'], 'v_final_norm_w': out['v_final_norm_w']}


def _loss(weights, diff, rest, loss_target):
    with _jax.named_scope("forward"):
        args = {**rest, TWIN_DIFF_INPUT: diff, **{k: w.astype(_WEIGHT_DTYPES[k]) for k, w in weights.items()}}
        y = _forward(args)
    with _jax.named_scope("loss_head"):
        err = _jnp.square(y.astype(_jnp.float32) - loss_target)
        return 0.5 * _jnp.sum(_jnp.mean(err, axis=-1)) if err.ndim else 0.5 * err


def _adamw(w, g, m, v):
    m = ADAM_B1 * m + (1.0 - ADAM_B1) * g
    v = ADAM_B2 * v + (1.0 - ADAM_B2) * _jnp.square(g)
    m_hat = m / (1.0 - ADAM_B1 ** ADAM_STEP)
    v_hat = v / (1.0 - ADAM_B2 ** ADAM_STEP)
    delta = -ADAM_LR * (m_hat / (_jnp.sqrt(v_hat) + ADAM_EPS) + ADAM_WD * w)
    return delta, m, v


def reference(x, meta_tokens, lb_logits, mix_norm_w, w_in, hg_norm_w, gd_conv_w, gd_a_log, gd_dt_bias, gd_norm_w, w_branch_a, w_branch_b, w_out, ffn_norm_w, w_ffn_in, w_ffn_out, final_norm_w, loss_target, m_meta_tokens, m_lb_logits, m_mix_norm_w, m_w_in, m_hg_norm_w, m_gd_conv_w, m_gd_a_log, m_gd_dt_bias, m_gd_norm_w, m_w_branch_a, m_w_branch_b, m_w_out, m_ffn_norm_w, m_w_ffn_in, m_w_ffn_out, m_final_norm_w, v_meta_tokens, v_lb_logits, v_mix_norm_w, v_w_in, v_hg_norm_w, v_gd_conv_w, v_gd_a_log, v_gd_dt_bias, v_gd_norm_w, v_w_branch_a, v_w_branch_b, v_w_out, v_ffn_norm_w, v_w_ffn_in, v_w_ffn_out, v_final_norm_w):
    given = dict(x=x, meta_tokens=meta_tokens, lb_logits=lb_logits, mix_norm_w=mix_norm_w, w_in=w_in, hg_norm_w=hg_norm_w, gd_conv_w=gd_conv_w, gd_a_log=gd_a_log, gd_dt_bias=gd_dt_bias, gd_norm_w=gd_norm_w, w_branch_a=w_branch_a, w_branch_b=w_branch_b, w_out=w_out, ffn_norm_w=ffn_norm_w, w_ffn_in=w_ffn_in, w_ffn_out=w_ffn_out, final_norm_w=final_norm_w, loss_target=loss_target, m_meta_tokens=m_meta_tokens, m_lb_logits=m_lb_logits, m_mix_norm_w=m_mix_norm_w, m_w_in=m_w_in, m_hg_norm_w=m_hg_norm_w, m_gd_conv_w=m_gd_conv_w, m_gd_a_log=m_gd_a_log, m_gd_dt_bias=m_gd_dt_bias, m_gd_norm_w=m_gd_norm_w, m_w_branch_a=m_w_branch_a, m_w_branch_b=m_w_branch_b, m_w_out=m_w_out, m_ffn_norm_w=m_ffn_norm_w, m_w_ffn_in=m_w_ffn_in, m_w_ffn_out=m_w_ffn_out, m_final_norm_w=m_final_norm_w, v_meta_tokens=v_meta_tokens, v_lb_logits=v_lb_logits, v_mix_norm_w=v_mix_norm_w, v_w_in=v_w_in, v_hg_norm_w=v_hg_norm_w, v_gd_conv_w=v_gd_conv_w, v_gd_a_log=v_gd_a_log, v_gd_dt_bias=v_gd_dt_bias, v_gd_norm_w=v_gd_norm_w, v_w_branch_a=v_w_branch_a, v_w_branch_b=v_w_branch_b, v_w_out=v_w_out, v_ffn_norm_w=v_ffn_norm_w, v_w_ffn_in=v_w_ffn_in, v_w_ffn_out=v_w_ffn_out, v_final_norm_w=v_final_norm_w)
    weights = {n: given[n] for n in TWIN_WEIGHTS}
    shared = {n: given[n] for n in SHARED_INPUTS}
    per_example = {n: given[n] for n in ['x']}
    grad_fn = _jax.value_and_grad(_loss, argnums=(0, 1))

    def one_microbatch(ex, loss_target):
        ex = dict(ex)
        diff = ex.pop(TWIN_DIFF_INPUT)
        return grad_fn(weights, diff, {**shared, **ex}, loss_target)

    if N_MICROBATCH == 1:
        loss, (grad_w, grad_x) = one_microbatch(per_example, given["loss_target"])
    else:
        def body(carry, xs):
            loss_sum, grad_sum = carry
            l_k, (gw_k, gx_k) = one_microbatch(xs[0], xs[1])
            with _jax.named_scope("update"):
                return (loss_sum + l_k, _jax.tree.map(_jnp.add, grad_sum, gw_k)), gx_k

        init = (_jnp.zeros((), _jnp.float32), _jax.tree.map(_jnp.zeros_like, weights))
        (loss, grad_w), grad_x = _jax.lax.scan(body, init, (per_example, given["loss_target"]))
    with _jax.named_scope("update"):
        delta_w, new_m, new_v = {}, {}, {}
        for n in TWIN_WEIGHTS:
            delta_w[n], new_m[n], new_v[n] = _adamw(weights[n], grad_w[n], given["m_" + n], given["v_" + n])
    return (loss, grad_x, *[grad_w[n] for n in TWIN_WEIGHTS], *[delta_w[n] for n in TWIN_WEIGHTS],
            *[new_m[n] for n in TWIN_WEIGHTS], *[new_v[n] for n in TWIN_WEIGHTS])
```

```python
import functools
import math

import jax
import jax.numpy as jnp
from jax import lax
from jax.experimental import pallas as pl
from jax.experimental.pallas import tpu as pltpu

F32, BF16 = jnp.float32, jnp.bfloat16
HI = lax.Precision.HIGHEST
EPS = 1e-6
D_MODEL = 2048
N_META = 16
FRONT = 256
CH = 64
SUB = 16
DH = 128
NH = 8
HW = NH * DH
CONV_K = 4
RT = 256
VMEM_LIMIT = 56 * 1024 * 1024
ADAM_LR, ADAM_B1, ADAM_B2, ADAM_EPS, ADAM_WD, ADAM_STEP = 0.001, 0.9, 0.999, 1e-08, 0.01, 10

NN = (((1,), (0,)), ((), ()))
NT = (((1,), (1,)), ((), ()))
TN = (((0,), (0,)), ((), ()))


def _dot(a, b, dn=NN, prec=HI):
    return lax.dot_general(a, b, dn, precision=prec, preferred_element_type=F32)


def _call(body, *, name, grid, in_specs, out_specs, out_shape, scratch=(), sem=None):
    return pl.pallas_call(
        body, name=name, grid=grid, in_specs=in_specs, out_specs=out_specs, out_shape=out_shape,
        scratch_shapes=list(scratch),
        compiler_params=pltpu.CompilerParams(dimension_semantics=sem, vmem_limit_bytes=VMEM_LIMIT))


def _sds(shape, dtype):
    return jax.ShapeDtypeStruct(tuple(shape), dtype)


def _sigmoid(x):
    return 1.0 / (1.0 + jnp.exp(-x))


def _silu(x):
    return x * _sigmoid(x)


def _dsilu(x):
    s = _sigmoid(x)
    return s * (1.0 + x * (1.0 - s))


def _tri(n, kind):
    r = lax.broadcasted_iota(jnp.int32, (n, n), 0)
    c = lax.broadcasted_iota(jnp.int32, (n, n), 1)
    return {"incl": r >= c, "strict": r > c, "upper": c >= r}[kind]


def _mm(a, b, mode, out_dtype, tm, tn, tk, name, add=None, n_outer=False):
    if mode == "nn":
        (M, K), (_, N) = a.shape, b.shape
        a_blk, b_blk, dn = (tm, tk), (tk, tn), NN
        a_idx, b_idx = (lambda i, j, k: (i, k)), (lambda i, j, k: (k, j))
    elif mode == "nt":
        (M, K), (N, _) = a.shape, b.shape
        a_blk, b_blk, dn = (tm, tk), (tn, tk), NT
        a_idx, b_idx = (lambda i, j, k: (i, k)), (lambda i, j, k: (j, k))
    else:
        (K, M), (_, N) = a.shape, b.shape
        a_blk, b_blk, dn = (tk, tm), (tk, tn), TN
        a_idx, b_idx = (lambda i, j, k: (k, i)), (lambda i, j, k: (k, j))
    tm, tn, tk = min(tm, M), min(tn, N), min(tk, K)
    a_blk = tuple({"m": tm, "k": tk}[x] for x in ("mk" if mode != "tn" else "km"))
    b_blk = tuple({"n": tn, "k": tk}[x] for x in ("kn" if mode != "nt" else "nk"))
    assert M % tm == 0 and N % tn == 0 and K % tk == 0, (name, M, N, K, tm, tn, tk)
    nk = K // tk
    o_idx = lambda i, j, k: (i, j)
    if n_outer:
        sw = lambda f: (lambda j, i, k: f(i, j, k))
        a_idx, b_idx, o_idx = sw(a_idx), sw(b_idx), sw(o_idx)
        grid = (N // tn, M // tm, nk)
    else:
        grid = (M // tm, N // tn, nk)
    has_add = add is not None

    def body(*refs):
        if has_add:
            a_ref, b_ref, c_ref, o_ref, acc_ref = refs
        else:
            a_ref, b_ref, o_ref, acc_ref = refs
            c_ref = None
        part = lax.dot_general(a_ref[...].astype(BF16), b_ref[...].astype(BF16), dn, preferred_element_type=F32)

        def fin(val):
            if has_add:
                val = val + c_ref[...]
            o_ref[...] = val.astype(out_dtype)

        if nk == 1:
            fin(part)
        else:
            k = pl.program_id(2)

            @pl.when(k == 0)
            def _():
                acc_ref[...] = part

            @pl.when(k > 0)
            def _():
                acc_ref[...] += part

            @pl.when(k == nk - 1)
            def _():
                fin(acc_ref[...])

    in_specs = [pl.BlockSpec(a_blk, a_idx), pl.BlockSpec(b_blk, b_idx)]
    args = [a, b]
    if has_add:
        in_specs.append(pl.BlockSpec((tm, tn), o_idx))
        args.append(add)
    acc_shape = (tm, tn) if nk > 1 else (8, 128)
    return _call(body, name=name, grid=grid, in_specs=in_specs, out_specs=pl.BlockSpec((tm, tn), o_idx),
                 out_shape=_sds((M, N), out_dtype), scratch=[pltpu.VMEM(acc_shape, F32)],
                 sem=("parallel", "parallel", "arbitrary"))(*args)


def _rms1_fwd(x, meta, w):
    seq, d = x.shape
    nt = (FRONT + seq) // RT

    def body(x_ref, m_ref, w_ref, o_ref):
        i = pl.program_id(0)

        def norm(v):
            r = lax.rsqrt(jnp.mean(v * v, axis=-1, keepdims=True) + EPS)
            return (v * r * w_ref[...]).astype(BF16)

        @pl.when(i == 0)
        def _():
            o_ref[0:RT - N_META, :] = jnp.zeros((RT - N_META, d), BF16)
            o_ref[RT - N_META:RT, :] = norm(m_ref[...])

        @pl.when(i > 0)
        def _():
            o_ref[...] = norm(x_ref[...])

    return _call(body, name="rms1_fwd", grid=(nt,),
                 in_specs=[pl.BlockSpec((RT, d), lambda i: (jnp.maximum(i - 1, 0), 0)),
                           pl.BlockSpec((N_META, d), lambda i: (0, 0)),
                           pl.BlockSpec((1, d), lambda i: (0, 0))],
                 out_specs=pl.BlockSpec((RT, d), lambda i: (i, 0)),
                 out_shape=_sds((FRONT + seq, d), BF16), sem=("parallel",))(x, meta, w)


def _rms1_bwd(x, meta, w, dxn, dh1):
    seq, d = x.shape
    nt = (FRONT + seq) // RT

    def body(x_ref, m_ref, w_ref, g_ref, r_ref, dx_ref, dm_ref, dw_ref):
        i = pl.program_id(0)

        def bwd(v, g):
            r = lax.rsqrt(jnp.mean(v * v, axis=-1, keepdims=True) + EPS)
            vh = v * r
            gh = g * w_ref[...]
            return r * (gh - vh * jnp.mean(gh * vh, axis=-1, keepdims=True)), jnp.sum(g * vh, axis=0, keepdims=True)

        @pl.when(i == 0)
        def _():
            dm, dw = bwd(m_ref[...], g_ref[RT - N_META:RT, :])
            dm_ref[...] = dm
            dw_ref[...] = dw

        @pl.when(i > 0)
        def _():
            dx, dw = bwd(x_ref[...], g_ref[...])
            dx_ref[...] = dx + r_ref[...]
            dw_ref[...] += dw

    xs = pl.BlockSpec((RT, d), lambda i: (jnp.maximum(i - 1, 0), 0))
    return _call(body, name="rms1_bwd", grid=(nt,),
                 in_specs=[xs, pl.BlockSpec((N_META, d), lambda i: (0, 0)), pl.BlockSpec((1, d), lambda i: (0, 0)),
                           pl.BlockSpec((RT, d), lambda i: (i, 0)), xs],
                 out_specs=[xs, pl.BlockSpec((N_META, d), lambda i: (0, 0)), pl.BlockSpec((1, d), lambda i: (0, 0))],
                 out_shape=[_sds((seq, d), F32), _sds((N_META, d), F32), _sds((1, d), F32)],
                 sem=("arbitrary",))(x, meta, w, dxn, dh1)


def _merge_fwd(proj, za, zb):
    seq, d = za.shape
    off = FRONT // RT
    ca, cb = 8 * HW // d, 8 * HW // d + 1

    def body(ga_ref, gb_ref, za_ref, zb_ref, o_ref):
        o_ref[...] = (_sigmoid(ga_ref[...]) * za_ref[...] + _sigmoid(gb_ref[...]) * zb_ref[...]).astype(BF16)

    zs = pl.BlockSpec((RT, d), lambda i: (i, 0))
    return _call(body, name="merge_fwd", grid=(seq // RT,),
                 in_specs=[pl.BlockSpec((RT, d), lambda i: (i + off, ca)), pl.BlockSpec((RT, d), lambda i: (i + off, cb)), zs, zs],
                 out_specs=zs, out_shape=_sds((seq, d), BF16), sem=("parallel",))(proj, proj, za, zb)


def _merge_bwd(proj, za, zb, dmerged):
    seq, d = za.shape
    off = FRONT // RT
    ca, cb = 8 * HW // d, 8 * HW // d + 1
    nt = (FRONT + seq) // RT

    def body(ga_ref, gb_ref, za_ref, zb_ref, dm_ref, dza_ref, dzb_ref, dg_ref):
        i = pl.program_id(0)

        @pl.when(i < off)
        def _():
            dg_ref[...] = jnp.zeros((RT, 2 * d), BF16)

        @pl.when(i >= off)
        def _():
            sa, sb, dm = _sigmoid(ga_ref[...]), _sigmoid(gb_ref[...]), dm_ref[...]
            dza_ref[...] = (dm * sa).astype(BF16)
            dzb_ref[...] = (dm * sb).astype(BF16)
            dg_ref[:, 0:d] = (dm * za_ref[...] * sa * (1.0 - sa)).astype(BF16)
            dg_ref[:, d:2 * d] = (dm * zb_ref[...] * sb * (1.0 - sb)).astype(BF16)

    rs = pl.BlockSpec((RT, d), lambda i: (jnp.maximum(i - off, 0), 0))
    return _call(body, name="merge_bwd", grid=(nt,),
                 in_specs=[pl.BlockSpec((RT, d), lambda i: (i, ca)), pl.BlockSpec((RT, d), lambda i: (i, cb)), rs, rs, rs],
                 out_specs=[rs, rs, pl.BlockSpec((RT, 2 * d), lambda i: (i, 0))],
                 out_shape=[_sds((seq, d), BF16), _sds((seq, d), BF16), _sds((FRONT + seq, 2 * d), BF16)],
                 sem=("arbitrary",))(proj, proj, za, zb, dmerged)


def _resid_norm_fwd(x, mix, w):
    seq, d = x.shape

    def body(x_ref, m_ref, w_ref, h_ref, n_ref):
        h = x_ref[...] + m_ref[...]
        h_ref[...] = h
        r = lax.rsqrt(jnp.mean(h * h, axis=-1, keepdims=True) + EPS)
        n_ref[...] = (h * r * w_ref[...]).astype(BF16)

    rs = pl.BlockSpec((RT, d), lambda i: (i, 0))
    return _call(body, name="resid_norm_fwd", grid=(seq // RT,),
                 in_specs=[rs, rs, pl.BlockSpec((1, d), lambda i: (0, 0))], out_specs=[rs, rs],
                 out_shape=[_sds((seq, d), F32), _sds((seq, d), BF16)], sem=("parallel",))(x, mix, w)


def _resid_norm_bwd(h1, w, dn2, dh2):
    seq, d = h1.shape

    def body(h_ref, w_ref, g_ref, r_ref, o_ref, ob_ref, dw_ref):
        i = pl.program_id(0)
        h, g = h_ref[...], g_ref[...]
        r = lax.rsqrt(jnp.mean(h * h, axis=-1, keepdims=True) + EPS)
        hh = h * r
        gh = g * w_ref[...]
        dh = r_ref[...] + r * (gh - hh * jnp.mean(gh * hh, axis=-1, keepdims=True))
        o_ref[...] = dh
        ob_ref[...] = dh.astype(BF16)
        dw = jnp.sum(g * hh, axis=0, keepdims=True)

        @pl.when(i == 0)
        def _():
            dw_ref[...] = dw

        @pl.when(i > 0)
        def _():
            dw_ref[...] += dw

    rs = pl.BlockSpec((RT, d), lambda i: (i, 0))
    ws = pl.BlockSpec((1, d), lambda i: (0, 0))
    return _call(body, name="resid_norm_bwd", grid=(seq // RT,), in_specs=[rs, ws, rs, rs], out_specs=[rs, rs, ws],
                 out_shape=[_sds((seq, d), F32), _sds((seq, d), BF16), _sds((1, d), F32)], sem=("arbitrary",))(h1, w, dn2, dh2)


def _swiglu_fwd(gu):
    seq, f2 = gu.shape
    ff = f2 // 2
    tc = 512
    nb = ff // tc

    def body(g_ref, u_ref, o_ref):
        o_ref[...] = (_silu(g_ref[...]) * u_ref[...]).astype(BF16)

    return _call(body, name="swiglu_fwd", grid=(seq // RT, nb),
                 in_specs=[pl.BlockSpec((RT, tc), lambda i, j: (i, j)), pl.BlockSpec((RT, tc), lambda i, j: (i, j + nb))],
                 out_specs=pl.BlockSpec((RT, tc), lambda i, j: (i, j)), out_shape=_sds((seq, ff), BF16),
                 sem=("parallel", "parallel"))(gu, gu)


def _swiglu_bwd(gu, dact):
    seq, f2 = gu.shape
    ff = f2 // 2
    tc = 512
    nb = ff // tc

    def body(g_ref, u_ref, d_ref, dg_ref, du_ref):
        g, d = g_ref[...], d_ref[...]
        dg_ref[...] = (d * u_ref[...] * _dsilu(g)).astype(BF16)
        du_ref[...] = (d * _silu(g)).astype(BF16)

    bs = pl.BlockSpec((RT, tc), lambda i, j: (i, j))
    dgate, dup = _call(body, name="swiglu_bwd", grid=(seq // RT, nb),
                       in_specs=[bs, pl.BlockSpec((RT, tc), lambda i, j: (i, j + nb)), bs], out_specs=[bs, bs],
                       out_shape=[_sds((seq, ff), BF16), _sds((seq, ff), BF16)], sem=("parallel", "parallel"))(gu, gu, dact)
    return jnp.concatenate([dgate, dup], axis=1)


def _loss_head(h1, f, w, tgt):
    seq, d = h1.shape
    nt = seq // RT

    def body(h_ref, f_ref, w_ref, t_ref, l_ref, dh_ref, dhb_ref, dw_ref):
        i = pl.program_id(0)
        h = h_ref[...] + f_ref[...]
        r = lax.rsqrt(jnp.mean(h * h, axis=-1, keepdims=True) + EPS)
        hh = h * r
        err = hh * w_ref[...] - t_ref[...]
        l_ref[...] = jnp.full((8, 128), 0.5 * jnp.sum(jnp.mean(err * err, axis=-1, keepdims=True)), F32)
        dy = err * (1.0 / d)
        gh = dy * w_ref[...]
        dh = r * (gh - hh * jnp.mean(gh * hh, axis=-1, keepdims=True))
        dh_ref[...] = dh
        dhb_ref[...] = dh.astype(BF16)
        dw = jnp.sum(dy * hh, axis=0, keepdims=True)

        @pl.when(i == 0)
        def _():
            dw_ref[...] = dw

        @pl.when(i > 0)
        def _():
            dw_ref[...] += dw

    rs = pl.BlockSpec((RT, d), lambda i: (i, 0))
    ws = pl.BlockSpec((1, d), lambda i: (0, 0))
    return _call(body, name="loss_head", grid=(nt,), in_specs=[rs, rs, ws, rs],
                 out_specs=[pl.BlockSpec((8, 128), lambda i: (i, 0)), rs, rs, ws],
                 out_shape=[_sds((nt * 8, 128), F32), _sds((seq, d), F32), _sds((seq, d), BF16), _sds((1, d), F32)],
                 sem=("arbitrary",))(h1, f, w, tgt)


def _sum_tiles(lt):
    n = lt.shape[0]

    def body(l_ref, o_ref):
        v = l_ref[...]
        r = lax.broadcasted_iota(jnp.int32, v.shape, 0)
        c = lax.broadcasted_iota(jnp.int32, v.shape, 1)
        o_ref[...] = jnp.sum(jnp.where((r % 8 == 0) & (c == 0), v, 0.0), keepdims=True)

    return _call(body, name="loss_sum", grid=(1,), in_specs=[pl.BlockSpec((n, 128), lambda i: (0, 0))],
                 out_specs=pl.BlockSpec((1, 1), lambda i: (0, 0)), out_shape=_sds((1, 1), F32))(lt)


def _gated_norm_fwd(o, g, nw):
    r = lax.rsqrt(jnp.mean(o * o, axis=-1, keepdims=True) + EPS)
    return o * r * nw * _silu(g)


def _gated_norm_bwd(o, g, nw, dout):
    r = lax.rsqrt(jnp.mean(o * o, axis=-1, keepdims=True) + EPS)
    oh = o * r
    don = dout * _silu(g)
    dg = dout * (oh * nw) * _dsilu(g)
    dnw = jnp.sum(don * oh, axis=0, keepdims=True)
    doh = don * nw
    return r * (doh - oh * jnp.mean(doh * oh, axis=-1, keepdims=True)), dg, dnw


def _hg_gates(fs, lbl):
    l0, l1 = lbl[0:1, :], lbl[1:2, :]
    m = jnp.maximum(l0, l1)
    e0, e1 = jnp.exp(l0 - m), jnp.exp(l1 - m)
    lb = e0 / (e0 + e1)
    sig = _sigmoid(fs)
    f = lb + (1.0 - lb) * sig
    return lb, sig, f, jnp.log(f), (1.0 - lb) * _sigmoid(-fs)


def _decay_blocks(q, k, b, p_ref):
    p_ref[...] = jnp.zeros((CH, CH), F32)
    m16 = _tri(SUB, "incl")
    for I in range(CH // SUB):
        s0 = I * SUB
        bI, qI, kI = b[s0:s0 + SUB], q[s0:s0 + SUB], k[s0:s0 + SUB]
        dec = jnp.exp(jnp.minimum(bI[:, None, :] - bI[None, :, :], 0.0))
        pii = jnp.sum(qI[:, None, :] * kI[None, :, :] * dec, axis=-1)
        p_ref[s0:s0 + SUB, s0:s0 + SUB] = jnp.where(m16, pii, 0.0)
        if I > 0:
            rI = b[s0 - 1:s0]
            qs = qI * jnp.exp(bI - rI)
            ks = k[0:s0] * jnp.exp(rI - b[0:s0])
            p_ref[s0:s0 + SUB, 0:s0] = _dot(qs, ks, NT)


def _hgrn2_fwd(proj, lb_logits, nw):
    tp = proj.shape[0]
    nt, cpt = tp // RT, RT // CH

    def body(q_ref, f_ref, i_ref, g_ref, lbl_ref, nw_ref, og_ref, or_ref, st_ref, s_ref, p_ref):
        @pl.when(pl.program_id(1) == 0)
        def _():
            s_ref[...] = jnp.zeros((DH, DH), F32)

        tril = _tri(CH, "incl").astype(F32)

        def chunk(c, carry):
            rows = pl.ds(pl.multiple_of(c * CH, CH), CH)
            _, _, _, w, k = _hg_gates(f_ref[rows, :], lbl_ref[...])
            q, v = _silu(q_ref[rows, :]), i_ref[rows, :]
            b = _dot(tril, w)
            st = s_ref[...]
            st_ref[0, c] = st
            _decay_blocks(q, k, b, p_ref)
            o = _dot(q * jnp.exp(b), st, NT) + _dot(p_ref[...], v)
            bl = b[CH - 1:CH]
            s_ref[...] = st * jnp.exp(bl) + _dot(v, k * jnp.exp(bl - b), TN)
            or_ref[rows, :] = o
            og_ref[rows, :] = _gated_norm_fwd(o, g_ref[rows, :], nw_ref[...]).astype(BF16)
            return carry

        lax.fori_loop(0, cpt, chunk, 0)

    col = lambda g: pl.BlockSpec((RT, DH), lambda h, i: (i, g * NH + h))
    hs = pl.BlockSpec((RT, DH), lambda h, i: (i, h))
    return _call(body, name="hgrn2_fwd", grid=(NH, nt),
                 in_specs=[col(0), col(1), col(2), col(3), pl.BlockSpec((2, DH), lambda h, i: (0, h)),
                           pl.BlockSpec((1, DH), lambda h, i: (0, 0))],
                 out_specs=[hs, hs, pl.BlockSpec((1, cpt, DH, DH), lambda h, i: (h, i, 0, 0))],
                 out_shape=[_sds((tp, HW), BF16), _sds((tp, HW), F32), _sds((NH, tp // CH, DH, DH), F32)],
                 scratch=[pltpu.VMEM((DH, DH), F32), pltpu.VMEM((CH, CH), F32)],
                 sem=("parallel", "arbitrary"))(proj, proj, proj, proj, lb_logits, nw)


def _hgrn2_bwd(proj, lb_logits, nw, o_raw, states, dog):
    tp = proj.shape[0]
    nt, cpt = tp // RT, RT // CH

    def body(q_ref, f_ref, i_ref, g_ref, lbl_ref, nw_ref, or_ref, st_ref, dog_ref,
             dq_ref, df_ref, di_ref, dg_ref, dl_ref, dnw_ref, ds_ref, p_ref, dk_ref, dqa_ref, do_ref):
        step = pl.program_id(1)

        @pl.when(step == 0)
        def _():
            ds_ref[...] = jnp.zeros((DH, DH), F32)
            dl_ref[...] = jnp.zeros((2, DH), F32)

        @pl.when((step == 0) & (pl.program_id(0) == 0))
        def _():
            dnw_ref[...] = jnp.zeros((1, DH), F32)

        do_t, dg_t, dnw = _gated_norm_bwd(or_ref[...], g_ref[...], nw_ref[...], dog_ref[...])
        do_ref[...] = do_t
        dg_ref[...] = dg_t.astype(BF16)
        dnw_ref[...] += dnw
        tril = _tri(CH, "incl")
        trilf = tril.astype(F32)
        triuf = _tri(CH, "upper").astype(F32)
        m16 = _tri(SUB, "incl")

        def chunk(cc, carry):
            c = cpt - 1 - cc
            rows = pl.ds(pl.multiple_of(c * CH, CH), CH)
            fs = f_ref[rows, :]
            lb, sig, f, w, k = _hg_gates(fs, lbl_ref[...])
            hq = q_ref[rows, :]
            q, v, do = _silu(hq), i_ref[rows, :], do_ref[rows, :]
            b = _dot(trilf, w)
            bl = b[CH - 1:CH]
            eb = jnp.exp(b)
            qs, kd = q * eb, k * jnp.exp(bl - b)
            st, dst = st_ref[0, c], ds_ref[...]
            _decay_blocks(q, k, b, p_ref)
            dv = _dot(p_ref[...], do, TN) + _dot(kd, dst, NT)
            dp = jnp.where(tril, _dot(do, v, NT), 0.0)
            dqa_ref[...] = eb * _dot(do, st)
            dk_ref[...] = jnp.exp(bl - b) * _dot(v, dst)
            for I in range(CH // SUB):
                s0 = I * SUB
                bI, qI, kI = b[s0:s0 + SUB], q[s0:s0 + SUB], k[s0:s0 + SUB]
                dec = jnp.exp(jnp.minimum(bI[:, None, :] - bI[None, :, :], 0.0))
                dpii = jnp.where(m16, dp[s0:s0 + SUB, s0:s0 + SUB], 0.0)[:, :, None] * dec
                dqa_ref[s0:s0 + SUB, :] += jnp.sum(dpii * kI[None, :, :], axis=1)
                dk_ref[s0:s0 + SUB, :] += jnp.sum(dpii * qI[:, None, :], axis=0)
                if I > 0:
                    rI = b[s0 - 1:s0]
                    eq, ek = jnp.exp(bI - rI), jnp.exp(rI - b[0:s0])
                    dpij = dp[s0:s0 + SUB, 0:s0]
                    dqa_ref[s0:s0 + SUB, :] += eq * _dot(dpij, k[0:s0] * ek)
                    dk_ref[0:s0, :] += ek * _dot(dpij, qI * eq, TN)
            dq, dk = dqa_ref[...], dk_ref[...]
            st_end = st * jnp.exp(bl) + _dot(v, kd, TN)
            dw = _dot(triuf, q * dq - k * dk) + jnp.sum(dst * st_end, axis=0, keepdims=True)
            ds_ref[...] = dst * jnp.exp(bl) + _dot(do, qs, TN)
            one_m = 1.0 - sig
            dq_ref[rows, :] = (dq * _dsilu(hq)).astype(BF16)
            df_ref[rows, :] = ((dw / f - dk) * (1.0 - lb) * sig * one_m).astype(BF16)
            di_ref[rows, :] = dv.astype(BF16)
            dl_ref[0:1, :] += jnp.sum((dw / f - dk) * one_m, axis=0, keepdims=True)
            return carry

        lax.fori_loop(0, cpt, chunk, 0)

        @pl.when(step == nt - 1)
        def _():
            lbl = lbl_ref[...]
            l0, l1 = lbl[0:1, :], lbl[1:2, :]
            m = jnp.maximum(l0, l1)
            e0, e1 = jnp.exp(l0 - m), jnp.exp(l1 - m)
            p0 = e0 / (e0 + e1)
            dl0 = dl_ref[0:1, :] * p0 * (1.0 - p0)
            dl_ref[0:1, :] = dl0
            dl_ref[1:2, :] = -dl0

    rev = lambda i: nt - 1 - i
    col = lambda g: pl.BlockSpec((RT, DH), lambda h, i: (rev(i), g * NH + h))
    hs = pl.BlockSpec((RT, DH), lambda h, i: (rev(i), h))
    return _call(body, name="hgrn2_bwd", grid=(NH, nt),
                 in_specs=[col(0), col(1), col(2), col(3), pl.BlockSpec((2, DH), lambda h, i: (0, h)),
                           pl.BlockSpec((1, DH), lambda h, i: (0, 0)), hs,
                           pl.BlockSpec((1, cpt, DH, DH), lambda h, i: (h, rev(i), 0, 0)), hs],
                 out_specs=[hs, hs, hs, hs, pl.BlockSpec((2, DH), lambda h, i: (0, h)),
                            pl.BlockSpec((1, DH), lambda h, i: (0, 0))],
                 out_shape=[_sds((tp, HW), BF16)] * 4 + [_sds((2, HW), F32), _sds((1, DH), F32)],
                 scratch=[pltpu.VMEM((DH, DH), F32), pltpu.VMEM((CH, CH), F32), pltpu.VMEM((CH, DH), F32),
                          pltpu.VMEM((CH, DH), F32), pltpu.VMEM((RT, DH), F32)],
                 sem=("arbitrary", "arbitrary"))(proj, proj, proj, proj, lb_logits, nw, o_raw, states, dog)


GQ0 = 4 * HW
CW = 3 * HW


def _gd_scalars(ab, alog, dtb):
    g = -jnp.exp(alog) * jax.nn.softplus(ab + dtb)
    return g, _sigmoid(ab)


def _conv_ext_specs(row_of):
    main = [pl.BlockSpec((RT, HW), lambda i, g=g: (row_of(i), GQ0 // HW + g)) for g in range(3)]
    prev = [pl.BlockSpec((8, HW), lambda i, g=g: (jnp.maximum(row_of(i) * (RT // 8) - 1, 0), GQ0 // HW + g)) for g in range(3)]
    return main + prev


def _conv_fill(ext_ref, xs, xps, first):
    for g in range(3):
        ext_ref[0:8, g * HW:(g + 1) * HW] = jnp.where(first, 0.0, xps[g][...])
        ext_ref[8:8 + RT, g * HW:(g + 1) * HW] = xs[g][...]


def _conv_apply(ext_ref, cw):
    y = cw[CONV_K - 1:CONV_K, :] * ext_ref[pl.ds(8, RT), :]
    for s in range(1, CONV_K):
        y += cw[CONV_K - 1 - s:CONV_K - s, :] * ext_ref[pl.ds(8 - s, RT), :]
    return y


def _gdn_prep_fwd(proj, pab, conv_w, alog, dtb):
    tp = proj.shape[0]
    nt = tp // RT

    def body(x0, x1, x2, p0, p1, p2, ab_ref, cw_ref, al_ref, dt_ref, q_ref, k_ref, v_ref, g_ref, b_ref, ext_ref):
        _conv_fill(ext_ref, (x0, x1, x2), (p0, p1, p2), pl.program_id(0) == 0)
        a = _silu(_conv_apply(ext_ref, cw_ref[...]))
        for h in range(NH):
            for part, ref, sc in ((0, q_ref, DH ** -0.5), (1, k_ref, 1.0)):
                seg = a[:, part * HW + h * DH:part * HW + (h + 1) * DH]
                ref[:, h * DH:(h + 1) * DH] = seg * (lax.rsqrt(jnp.sum(seg * seg, axis=-1, keepdims=True) + EPS) * sc)
        v_ref[...] = a[:, 2 * HW:3 * HW]
        g, beta = _gd_scalars(ab_ref[...], al_ref[...], dt_ref[...])
        for h in range(NH):
            g_ref[h] = jnp.broadcast_to(g[:, h:h + 1], (RT, DH))
            b_ref[h] = jnp.broadcast_to(beta[:, NH + h:NH + h + 1], (RT, DH))

    hs = pl.BlockSpec((RT, HW), lambda i: (i, 0))
    sc = pl.BlockSpec((NH, RT, DH), lambda i: (0, i, 0))
    one = pl.BlockSpec((1, DH), lambda i: (0, 0))
    return _call(body, name="gdn_prep_fwd", grid=(nt,),
                 in_specs=_conv_ext_specs(lambda i: i) + [pl.BlockSpec((RT, DH), lambda i: (i, 0)),
                                                           pl.BlockSpec((CONV_K, CW), lambda i: (0, 0)), one, one],
                 out_specs=[hs, hs, hs, sc, sc],
                 out_shape=[_sds((tp, HW), F32)] * 3 + [_sds((NH, tp, DH), F32)] * 2,
                 scratch=[pltpu.VMEM((RT + 8, CW), F32)], sem=("parallel",))(*([proj] * 6), pab, conv_w, alog, dtb)


def _gdn_prep_bwd(proj, pab, conv_w, alog, dtb, dq, dk, dv, dgb, dbb):
    tp = proj.shape[0]
    nt = tp // RT

    def body(x0, x1, x2, p0, p1, p2, ab_ref, cw_ref, al_ref, dt_ref, dq_ref, dk_ref, dv_ref, dg_ref, db_ref,
             dx_ref, dab_ref, dcw_ref, dal_ref, ddt_ref, ext_ref, dy_ref):
        step = pl.program_id(0)
        i = nt - 1 - step

        @pl.when(step == 0)
        def _():
            dy_ref[RT:RT + 8, :] = jnp.zeros((8, CW), F32)
            dcw_ref[...] = jnp.zeros((8, CW), F32)
            dal_ref[...] = jnp.zeros((1, DH), F32)
            ddt_ref[...] = jnp.zeros((1, DH), F32)

        _conv_fill(ext_ref, (x0, x1, x2), (p0, p1, p2), i == 0)
        cw = cw_ref[...]
        y = _conv_apply(ext_ref, cw)
        a = _silu(y)
        dsl = _dsilu(y)
        for h in range(NH):
            for part, ref, sc in ((0, dq_ref, DH ** -0.5), (1, dk_ref, 1.0)):
                lo = part * HW + h * DH
                seg = a[:, lo:lo + DH]
                r = lax.rsqrt(jnp.sum(seg * seg, axis=-1, keepdims=True) + EPS)
                xh = seg * r
                dxh = ref[:, h * DH:(h + 1) * DH] * sc
                dy_ref[0:RT, lo:lo + DH] = r * (dxh - xh * jnp.sum(dxh * xh, axis=-1, keepdims=True)) * dsl[:, lo:lo + DH]
        dy_ref[0:RT, 2 * HW:3 * HW] = dv_ref[...] * dsl[:, 2 * HW:3 * HW]
        dy = dy_ref[0:RT, :]
        dx = cw[CONV_K - 1:CONV_K, :] * dy
        dcw_ref[CONV_K - 1:CONV_K, :] += jnp.sum(dy * ext_ref[pl.ds(8, RT), :], axis=0, keepdims=True)
        for s in range(1, CONV_K):
            dx += cw[CONV_K - 1 - s:CONV_K - s, :] * dy_ref[pl.ds(s, RT), :]
            dcw_ref[CONV_K - 1 - s:CONV_K - s, :] += jnp.sum(dy * ext_ref[pl.ds(8 - s, RT), :], axis=0, keepdims=True)
        dx_ref[...] = dx.astype(BF16)
        dy_ref[RT:RT + 8, :] = dy[0:8, :]
        ab = ab_ref[...]
        g, beta = _gd_scalars(ab, al_ref[...], dt_ref[...])
        lane = lax.broadcasted_iota(jnp.int32, (RT, DH), 1)
        dgl = jnp.zeros((RT, DH), F32)
        dbl = jnp.zeros((RT, DH), F32)
        for h in range(NH):
            dgl = jnp.where(lane == h, dg_ref[h], dgl)
            dbl = jnp.where(lane == NH + h, db_ref[h], dbl)
        dsp = dgl * (-jnp.exp(al_ref[...])) * _sigmoid(ab + dt_ref[...])
        dab_ref[...] = (dsp + dbl * beta * (1.0 - beta)).astype(BF16)
        ddt_ref[...] += jnp.sum(dsp, axis=0, keepdims=True)
        dal_ref[...] += jnp.sum(dgl * g, axis=0, keepdims=True)

    hs = pl.BlockSpec((RT, HW), lambda s: (nt - 1 - s, 0))
    sc = pl.BlockSpec((NH, RT, DH), lambda s: (0, nt - 1 - s, 0))
    one = pl.BlockSpec((1, DH), lambda s: (0, 0))
    xs = pl.BlockSpec((RT, CW), lambda s: (nt - 1 - s, 0))
    return _call(body, name="gdn_prep_bwd", grid=(nt,),
                 in_specs=_conv_ext_specs(lambda s: nt - 1 - s) + [
                     pl.BlockSpec((RT, DH), lambda s: (nt - 1 - s, 0)), pl.BlockSpec((CONV_K, CW), lambda s: (0, 0)),
                     one, one, hs, hs, hs, sc, sc],
                 out_specs=[xs, pl.BlockSpec((RT, DH), lambda s: (nt - 1 - s, 0)), pl.BlockSpec((8, CW), lambda s: (0, 0)), one, one],
                 out_shape=[_sds((tp, CW), BF16), _sds((tp, DH), BF16), _sds((8, CW), F32), _sds((1, DH), F32), _sds((1, DH), F32)],
                 scratch=[pltpu.VMEM((RT + 8, CW), F32), pltpu.VMEM((RT + 8, CW), F32)],
                 sem=("arbitrary",))(*([proj] * 6), pab, conv_w, alog, dtb, dq, dk, dv, dgb, dbb)


def _unit_lower_inverse(a):
    r = lax.broadcasted_iota(jnp.int32, (CH, CH), 0)
    c = lax.broadcasted_iota(jnp.int32, (CH, CH), 1)
    blk_of = lambda t, size: lax.shift_right_logical(t, int(math.log2(size)))
    eye = (r == c).astype(F32)
    a16 = jnp.where(blk_of(r, SUB) == blk_of(c, SUB), a, 0.0)
    x = eye - a16
    p = a16
    for _ in range(3):
        p = _dot(p, p)
        x = _dot(x, eye + p)
    for blk in (2 * SUB, 4 * SUB):
        off = jnp.where((blk_of(r, blk) == blk_of(c, blk)) & (blk_of(r, blk // 2) != blk_of(c, blk // 2)), a, 0.0)
        x = x - _dot(x, _dot(off, x))
    return x


def _gdn_chunk_common(q, k, v, gl, bt):
    gc = _dot(_tri(CH, "incl").astype(F32), gl)
    e = jnp.exp(gc)
    rel = jnp.exp(jnp.minimum(gc[:, 0:CH] - gc.T[0:CH, :], 0.0))
    kb = bt * k
    a = jnp.where(_tri(CH, "strict"), bt[:, 0:CH] * _dot(k, k, NT) * rel, 0.0)
    x = _unit_lower_inverse(a)
    w = _dot(x, kb * e)
    u = _dot(x, bt * v)
    attn = jnp.where(_tri(CH, "incl"), _dot(q, k, NT) * rel, 0.0)
    return gc, e, rel, kb, a, x, w, u, attn


def _gdn_fwd(q, k, v, gb, bb, proj, nw):
    tp = q.shape[0]
    nt, cpt = tp // RT, RT // CH

    def body(q_ref, k_ref, v_ref, g_ref, b_ref, z_ref, nw_ref, og_ref, or_ref, st_ref, s_ref):
        @pl.when(pl.program_id(1) == 0)
        def _():
            s_ref[...] = jnp.zeros((DH, DH), F32)

        def chunk(c, carry):
            rows = pl.ds(pl.multiple_of(c * CH, CH), CH)
            qc, kc, vc = q_ref[rows, :], k_ref[rows, :], v_ref[rows, :]
            gc, e, rel, kb, a, x, w, u, attn = _gdn_chunk_common(qc, kc, vc, g_ref[0, rows, :], b_ref[0, rows, :])
            s = s_ref[...]
            st_ref[0, c] = s
            vn = u - _dot(w, s)
            o = _dot(qc * e, s) + _dot(attn, vn)
            gl = gc[CH - 1:CH]
            s_ref[...] = s * jnp.exp(gl) + _dot(kc * jnp.exp(gl - gc), vn, TN)
            or_ref[rows, :] = o
            og_ref[rows, :] = _gated_norm_fwd(o, z_ref[rows, :], nw_ref[...]).astype(BF16)
            return carry

        lax.fori_loop(0, cpt, chunk, 0)

    hs = pl.BlockSpec((RT, DH), lambda h, i: (i, h))
    sc = pl.BlockSpec((1, RT, DH), lambda h, i: (h, i, 0))
    return _call(body, name="gdn_fwd", grid=(NH, nt),
                 in_specs=[hs, hs, hs, sc, sc, pl.BlockSpec((RT, DH), lambda h, i: (i, 7 * NH + h)),
                           pl.BlockSpec((1, DH), lambda h, i: (0, 0))],
                 out_specs=[hs, hs, pl.BlockSpec((1, cpt, DH, DH), lambda h, i: (h, i, 0, 0))],
                 out_shape=[_sds((tp, HW), BF16), _sds((tp, HW), F32), _sds((NH, tp // CH, DH, DH), F32)],
                 scratch=[pltpu.VMEM((DH, DH), F32)], sem=("parallel", "arbitrary"))(q, k, v, gb, bb, proj, nw)


def _gdn_bwd(q, k, v, gb, bb, proj, nw, o_raw, states, dog):
    tp = q.shape[0]
    nt, cpt = tp // RT, RT // CH

    def body(q_ref, k_ref, v_ref, g_ref, b_ref, z_ref, nw_ref, or_ref, st_ref, dog_ref,
             dq_ref, dk_ref, dv_ref, dg_ref, db_ref, dz_ref, dnw_ref, ds_ref, do_ref):
        step = pl.program_id(1)

        @pl.when(step == 0)
        def _():
            ds_ref[...] = jnp.zeros((DH, DH), F32)

        @pl.when((step == 0) & (pl.program_id(0) == 0))
        def _():
            dnw_ref[...] = jnp.zeros((1, DH), F32)

        do_t, dz_t, dnw = _gated_norm_bwd(or_ref[...], z_ref[...], nw_ref[...], dog_ref[...])
        do_ref[...] = do_t
        dz_ref[...] = dz_t.astype(BF16)
        dnw_ref[...] += dnw
        incl, strict = _tri(CH, "incl"), _tri(CH, "strict")
        triuf = _tri(CH, "upper").astype(F32)
        ones = jnp.ones((CH, DH), F32)
        rowid = lax.broadcasted_iota(jnp.int32, (CH, DH), 0)

        def rsum(t):
            return jnp.sum(t, axis=-1, keepdims=True)

        def chunk(cc, carry):
            c = cpt - 1 - cc
            rows = pl.ds(pl.multiple_of(c * CH, CH), CH)
            qc, kc, vc, do = q_ref[rows, :], k_ref[rows, :], v_ref[rows, :], do_ref[rows, :]
            bt = b_ref[0, rows, :]
            gc, e, rel, kb, a, x, w, u, attn = _gdn_chunk_common(qc, kc, vc, g_ref[0, rows, :], bt)
            s, dsn = st_ref[0, c], ds_ref[...]
            gl = gc[CH - 1:CH]
            el = jnp.exp(gl)
            cdec = jnp.exp(gl - gc)
            vn = u - _dot(w, s)
            dvn = _dot(attn, do, TN) + _dot(kc * cdec, dsn)
            dattn = jnp.where(incl, _dot(do, vn, NT), 0.0)
            dos = _dot(do, s, NT)
            vds = _dot(vn, dsn, NT)
            dar = dattn * rel
            dq = _dot(dar, kc) + e * dos
            dk = _dot(dar, qc, TN) + cdec * vds
            dgc = rsum(qc * e * dos)
            dc = cdec[:, 0:1] * rsum(kc * vds)
            dgc = dgc - dc
            dglast = jnp.sum(dc, axis=0, keepdims=True) + el[:, 0:1] * jnp.sum(rsum(dsn * s), axis=0, keepdims=True)
            ds_ref[...] = dsn * el + _dot(qc * e, do, TN) - _dot(w, dvn, TN)
            dw = -_dot(dvn, s, NT)
            drw = _dot(x, dw, TN)
            dru = _dot(x, dvn, TN)
            da = -jnp.where(strict, _dot(drw, w, NT) + _dot(dru, u, NT), 0.0)
            dar2 = da * rel
            dkb = _dot(dar2, kc)
            rwk = rsum(drw * kc)
            dk = dk + _dot(dar2, kb, TN) + bt * dkb + (bt * e) * drw
            dbeta = rsum(dkb * kc) + e[:, 0:1] * rwk + rsum(dru * vc)
            dgc = dgc + bt[:, 0:1] * e[:, 0:1] * rwk
            z = dattn * attn + da * a
            dgc = dgc + rsum(z) - _dot(z, ones, TN)[:, 0:1]
            dgcb = jnp.broadcast_to(dgc, (CH, DH)) + jnp.where(rowid == CH - 1, jnp.broadcast_to(dglast, (CH, DH)), 0.0)
            dq_ref[rows, :] = dq
            dk_ref[rows, :] = dk
            dv_ref[rows, :] = bt * dru
            dg_ref[0, rows, :] = _dot(triuf, dgcb)
            db_ref[0, rows, :] = jnp.broadcast_to(dbeta, (CH, DH))
            return carry

        lax.fori_loop(0, cpt, chunk, 0)

    rev = lambda i: nt - 1 - i
    hs = pl.BlockSpec((RT, DH), lambda h, i: (rev(i), h))
    sc = pl.BlockSpec((1, RT, DH), lambda h, i: (h, rev(i), 0))
    return _call(body, name="gdn_bwd", grid=(NH, nt),
                 in_specs=[hs, hs, hs, sc, sc, pl.BlockSpec((RT, DH), lambda h, i: (rev(i), 7 * NH + h)),
                           pl.BlockSpec((1, DH), lambda h, i: (0, 0)), hs,
                           pl.BlockSpec((1, cpt, DH, DH), lambda h, i: (h, rev(i), 0, 0)), hs],
                 out_specs=[hs, hs, hs, sc, sc, hs, pl.BlockSpec((1, DH), lambda h, i: (0, 0))],
                 out_shape=[_sds((tp, HW), F32)] * 3 + [_sds((NH, tp, DH), F32)] * 2 + [_sds((tp, HW), BF16), _sds((1, DH), F32)],
                 scratch=[pltpu.VMEM((DH, DH), F32), pltpu.VMEM((RT, DH), F32)],
                 sem=("arbitrary", "arbitrary"))(q, k, v, gb, bb, proj, nw, o_raw, states, dog)


MAIN_W = 8 * HW
AB_W = 2 * NH


def _split_w_in(w_in):
    main = jnp.concatenate([w_in[:, :MAIN_W], w_in[:, MAIN_W + AB_W:]], axis=1)
    ab = jnp.pad(w_in[:, MAIN_W:MAIN_W + AB_W], ((0, 0), (0, DH - AB_W)))
    return main, ab


def _pad_lanes(v):
    return jnp.pad(v, ((0, 0), (0, DH - v.shape[1])))


def _local_step(x, tgt, meta, lb_logits, mix_w, w_in, hg_nw, conv_w, a_log, dt_bias, gd_nw, w_a, w_b, w_out,
                ffn_nw, w_ffn_in, w_ffn_out, final_w):
    seq = x.shape[0]
    w_main, w_ab = _split_w_in(w_in)
    alog, dtb = _pad_lanes(a_log), _pad_lanes(dt_bias)
    final_w = final_w.reshape(1, -1)
    xn = _rms1_fwd(x, meta, mix_w)
    proj = _mm(xn, w_main, "nn", F32, 768, 1024, 2048, "proj_main", n_outer=True)
    pab = _mm(xn, w_ab, "nn", F32, 768, 128, 2048, "proj_ab")
    oa_g, oa_raw, st_a = _hgrn2_fwd(proj, lb_logits, hg_nw)
    q, k, v, gb, bb = _gdn_prep_fwd(proj, pab, conv_w, alog, dtb)
    ob_g, ob_raw, st_b = _gdn_fwd(q, k, v, gb, bb, proj, gd_nw)
    oa_r, ob_r = oa_g[FRONT:], ob_g[FRONT:]
    za = _mm(oa_r, w_a, "nn", F32, 1024, 2048, 1024, "branch_a")
    zb = _mm(ob_r, w_b, "nn", F32, 1024, 2048, 1024, "branch_b")
    merged = _merge_fwd(proj, za, zb)
    mix = _mm(merged, w_out, "nn", F32, 1024, 2048, 2048, "mix_out")
    h1, n2 = _resid_norm_fwd(x, mix, ffn_nw)
    gu = _mm(n2, w_ffn_in, "nn", F32, 1024, 1024, 2048, "ffn_in", n_outer=True)
    act = _swiglu_fwd(gu)
    f = _mm(act, w_ffn_out, "nn", F32, 1024, 2048, 512, "ffn_out")
    lt, dh2, dh2b, dfinal = _loss_head(h1, f, final_w, tgt)
    loss = _sum_tiles(lt)
    dact = _mm(dh2b, w_ffn_out, "nt", F32, 1024, 512, 2048, "d_act", n_outer=True)
    dw_ffn_out = _mm(act, dh2b, "tn", F32, 512, 2048, 1024, "dw_ffn_out")
    dgu = _swiglu_bwd(gu, dact)
    dn2 = _mm(dgu, w_ffn_in, "nt", F32, 1024, 2048, 512, "d_n2")
    dw_ffn_in = _mm(n2, dgu, "tn", F32, 2048, 1024, 1024, "dw_ffn_in")
    dh1, dh1b, dffn_nw = _resid_norm_bwd(h1, ffn_nw, dn2, dh2)
    dmerged = _mm(dh1b, w_out, "nt", F32, 1024, 2048, 2048, "d_merged")
    dw_out = _mm(merged, dh1b, "tn", F32, 2048, 1024, 1024, "dw_out")
    dza, dzb, dgate = _merge_bwd(proj, za, zb, dmerged)
    front = ((FRONT, 0), (0, 0))
    doa = jnp.pad(_mm(dza, w_a, "nt", F32, 1024, 1024, 2048, "d_oa"), front)
    dob = jnp.pad(_mm(dzb, w_b, "nt", F32, 1024, 1024, 2048, "d_ob"), front)
    dw_a = _mm(oa_r, dza, "tn", F32, 1024, 2048, 1024, "dw_branch_a")
    dw_b = _mm(ob_r, dzb, "tn", F32, 1024, 2048, 1024, "dw_branch_b")
    dhq, dhf, dhi, dhg, dlbl, dhg_nw = _hgrn2_bwd(proj, lb_logits, hg_nw, oa_raw, st_a, doa)
    dq, dk, dv, dg, dbeta, dz, dgd_nw = _gdn_bwd(q, k, v, gb, bb, proj, gd_nw, ob_raw, st_b, dob)
    dx3, dab, dconv, dalog, ddtb = _gdn_prep_bwd(proj, pab, conv_w, alog, dtb, dq, dk, dv, dg, dbeta)
    dproj = jnp.concatenate([dhq, dhf, dhi, dhg, dx3, dz, dgate], axis=1)
    dxn = _mm(dproj, w_main, "nt", F32, 768, 2048, 1024, "d_xn")
    dxn = _mm(dab, w_ab, "nt", F32, 768, 2048, 128, "d_xn_ab", add=dxn)
    dw_main = _mm(xn, dproj, "tn", F32, 2048, 1024, 768, "dw_in_main")
    dw_ab = _mm(xn, dab, "tn", F32, 2048, 128, 768, "dw_in_ab")
    dx, dmeta, dmix_w = _rms1_bwd(x, meta, mix_w, dxn, dh1)
    dw_in = jnp.concatenate([dw_main[:, :MAIN_W], dw_ab[:, :AB_W], dw_main[:, MAIN_W:]], axis=1)
    grads = dict(meta_tokens=dmeta, lb_logits=dlbl, mix_norm_w=dmix_w, w_in=dw_in,
                 hg_norm_w=dhg_nw, gd_conv_w=dconv[:CONV_K], gd_a_log=dalog[:, :NH],
                 gd_dt_bias=ddtb[:, :NH], gd_norm_w=dgd_nw, w_branch_a=dw_a, w_branch_b=dw_b,
                 w_out=dw_out, ffn_norm_w=dffn_nw, w_ffn_in=dw_ffn_in, w_ffn_out=dw_ffn_out,
                 final_norm_w=dfinal.reshape(-1))
    return loss, dx, grads


def _adamw(g, w, m, v, name):
    rows, cols = g.shape
    tr = rows
    for cand in (128, 64, 32, 16, 8):
        if rows % cand == 0 and rows > cand:
            tr = cand
            break

    def body(g_ref, w_ref, m_ref, v_ref, d_ref, nm_ref, nv_ref):
        gg = g_ref[...]
        nm = ADAM_B1 * m_ref[...] + (1.0 - ADAM_B1) * gg
        nv = ADAM_B2 * v_ref[...] + (1.0 - ADAM_B2) * (gg * gg)
        m_hat = nm / (1.0 - ADAM_B1 ** ADAM_STEP)
        v_hat = nv / (1.0 - ADAM_B2 ** ADAM_STEP)
        d_ref[...] = -ADAM_LR * (m_hat / (jnp.sqrt(v_hat) + ADAM_EPS) + ADAM_WD * w_ref[...])
        nm_ref[...] = nm
        nv_ref[...] = nv

    bs = pl.BlockSpec((tr, cols), lambda i: (i, 0))
    return _call(body, name=name, grid=(rows // tr,), in_specs=[bs] * 4, out_specs=[bs] * 3,
                 out_shape=[_sds((rows, cols), F32)] * 3, sem=("parallel",))(g, w, m, v)


PACK_W = 1024
HBM = pl.BlockSpec(memory_space=pltpu.HBM)
MESH = pl.DeviceIdType.MESH


def _place():
    x, y, c = lax.axis_index("x"), lax.axis_index("y"), lax.axis_index("c")
    return x, y, c, [(1 - x, y), (x, 1 - y), (1 - x, 1 - y)]


def _comm_call(body, name, out_shape, n_in, scratch):
    return pl.pallas_call(body, name=name, out_shape=out_shape, in_specs=[HBM] * n_in,
                          out_specs=jax.tree.map(lambda _: HBM, out_shape), scratch_shapes=scratch)


def _gather_weights(wp):
    pr = wp.shape[0]
    hh = pr // 2

    def body(w_ref, out_ref, send_sems, recv_sems, local_sem):
        x, y, c, chips = _place()
        s_me = 2 * x + y
        half = pl.ds(pl.multiple_of(c * hh, 16), hh)
        other = pl.ds(pl.multiple_of((1 - c) * hh, 16), hh)

        def copy(k, src, dst, to):
            return pltpu.make_async_remote_copy(src_ref=src, dst_ref=dst, send_sem=send_sems.at[k], recv_sem=recv_sems.at[k],
                                                device_id=to, device_id_type=MESH)

        mine = pltpu.make_async_copy(w_ref, out_ref.at[s_me], local_sem)
        mine.start()
        first = [copy(j, w_ref.at[half], out_ref.at[s_me, half], (cx, cy, c)) for j, (cx, cy) in enumerate(chips)]
        for cp in first:
            cp.start()
        passed = [copy(3 + j, out_ref.at[2 * cx + cy, half], out_ref.at[2 * cx + cy, half], (x, y, 1 - c))
                  for j, (cx, cy) in enumerate(chips)]
        for j, (cx, cy) in enumerate(chips):
            copy(j, w_ref.at[half], out_ref.at[2 * cx + cy, half], (cx, cy, c)).wait_recv()
            passed[j].start()
        for j, (cx, cy) in enumerate(chips):
            copy(3 + j, w_ref.at[other], out_ref.at[2 * cx + cy, other], (x, y, 1 - c)).wait_recv()
        for cp in first + passed:
            cp.wait_send()
        mine.wait()

    return _comm_call(body, "gather_weights", _sds((4, pr, PACK_W), wp.dtype), 1,
                      [pltpu.SemaphoreType.DMA((6,)), pltpu.SemaphoreType.DMA((6,)), pltpu.SemaphoreType.DMA])(wp)


def _swap_halves(g):
    _, pr, _ = g.shape
    hh = pr // 2

    def body(g_ref, out_ref, send_sem, recv_sem):
        x, y, c, _ = _place()
        other = pl.ds(pl.multiple_of((1 - c) * hh, 8), hh)
        cp = pltpu.make_async_remote_copy(src_ref=g_ref.at[:, other, :], dst_ref=out_ref, send_sem=send_sem, recv_sem=recv_sem,
                                          device_id=(x, y, 1 - c), device_id_type=MESH)
        cp.start()
        cp.wait()

    return _comm_call(body, "swap_halves", _sds((4, hh, PACK_W), g.dtype), 1,
                      [pltpu.SemaphoreType.DMA, pltpu.SemaphoreType.DMA])(g)


def _add_half(g, got, c):
    _, pr, _ = g.shape
    hh = pr // 2
    tr = 768
    nb = hh // tr
    assert hh % tr == 0

    def body(c_ref, a_ref, b_ref, o_ref):
        o_ref[...] = a_ref[...] + b_ref[...]

    gs = pltpu.PrefetchScalarGridSpec(
        num_scalar_prefetch=1, grid=(4, nb),
        in_specs=[pl.BlockSpec((1, tr, PACK_W), lambda s, i, c_ref: (s, c_ref[0] * nb + i, 0)),
                  pl.BlockSpec((1, tr, PACK_W), lambda s, i, c_ref: (s, i, 0))],
        out_specs=pl.BlockSpec((1, tr, PACK_W), lambda s, i, c_ref: (s, i, 0)))
    return pl.pallas_call(body, name="add_half", grid_spec=gs, out_shape=_sds((4, hh, PACK_W), F32),
                          compiler_params=pltpu.CompilerParams(dimension_semantics=("parallel", "parallel"),
                                                               vmem_limit_bytes=VMEM_LIMIT))(c, g, got)


def _scatter_chips(p):
    def body(p_ref, out_ref, send_sems, recv_sems, local_sem):
        x, y, c, chips = _place()
        s_me = 2 * x + y
        mine = pltpu.make_async_copy(p_ref.at[s_me], out_ref.at[s_me], local_sem)
        mine.start()
        cps = [pltpu.make_async_remote_copy(src_ref=p_ref.at[2 * cx + cy], dst_ref=out_ref.at[s_me], send_sem=send_sems.at[j],
                                            recv_sem=recv_sems.at[j], device_id=(cx, cy, c), device_id_type=MESH)
               for j, (cx, cy) in enumerate(chips)]
        for cp in cps:
            cp.start()
        for cp in cps:
            cp.wait()
        mine.wait()

    return _comm_call(body, "scatter_chips", _sds(p.shape, p.dtype), 1,
                      [pltpu.SemaphoreType.DMA((3,)), pltpu.SemaphoreType.DMA((3,)), pltpu.SemaphoreType.DMA])(p)


def _sum_slabs(b, name):
    n, h, wd = b.shape
    tr = h
    if n * h * wd * 4 > (4 << 20):
        tr = next(cand for cand in (768, 512, 256, 128, 64, 32, 16, 8) if h % cand == 0)

    def body(b_ref, o_ref):
        acc = b_ref[0]
        for s in range(1, n):
            acc = acc + b_ref[s]
        o_ref[...] = acc

    return _call(body, name=name, grid=(h // tr,), in_specs=[pl.BlockSpec((n, tr, wd), lambda i: (0, i, 0))],
                 out_specs=pl.BlockSpec((tr, wd), lambda i: (i, 0)), out_shape=_sds((h, wd), F32), sem=("parallel",))(b)


def _share_halves(gh):
    hh = gh.shape[0]

    def body(g_ref, out_ref, send_sem, recv_sem, local_sem):
        x, y, c, _ = _place()
        half = pl.ds(pl.multiple_of(c * hh, 8), hh)
        mine = pltpu.make_async_copy(g_ref, out_ref.at[half], local_sem)
        mine.start()
        cp = pltpu.make_async_remote_copy(src_ref=g_ref, dst_ref=out_ref.at[half], send_sem=send_sem, recv_sem=recv_sem,
                                          device_id=(x, y, 1 - c), device_id_type=MESH)
        cp.start()
        cp.wait()
        mine.wait()

    return _comm_call(body, "share_halves", _sds((2 * hh, PACK_W), gh.dtype), 1,
                      [pltpu.SemaphoreType.DMA, pltpu.SemaphoreType.DMA, pltpu.SemaphoreType.DMA])(gh)


def _gather_all(v, name):
    def body(v_ref, out_ref, send_sems, recv_sems, local_sem):
        x, y, c = lax.axis_index("x"), lax.axis_index("y"), lax.axis_index("c")
        me = 4 * x + 2 * y + c
        flip = lambda t, d: 1 - t if d else t
        mine = pltpu.make_async_copy(v_ref, out_ref.at[me], local_sem)
        mine.start()
        cps = []
        for k in range(1, 8):
            to = (flip(x, k & 4), flip(y, k & 2), flip(c, k & 1))
            cps.append(pltpu.make_async_remote_copy(src_ref=v_ref, dst_ref=out_ref.at[me], send_sem=send_sems.at[k - 1],
                                                    recv_sem=recv_sems.at[k - 1], device_id=to, device_id_type=MESH))
        for cp in cps:
            cp.start()
        for cp in cps:
            cp.wait()
        mine.wait()

    return _comm_call(body, name, _sds((8,) + v.shape, v.dtype), 1,
                      [pltpu.SemaphoreType.DMA((7,)), pltpu.SemaphoreType.DMA((7,)), pltpu.SemaphoreType.DMA])(v)


BIG = (("w_in", 1), ("w_branch_a", 1), ("w_branch_b", 1), ("w_out", 0), ("w_ffn_in", 1), ("w_ffn_out", 0))
SMALL = ("meta_tokens", "lb_logits", "mix_norm_w", "hg_norm_w", "gd_conv_w", "gd_a_log", "gd_dt_bias", "gd_norm_w",
         "ffn_norm_w", "final_norm_w")
PACK_ALIGN = 2 * 768


def _pack_rows(parts, dtype, align):
    flat = jnp.concatenate([p.reshape(-1).astype(dtype) for p in parts])
    rows = -(-flat.shape[0] // PACK_W)
    rows = -(-rows // align) * align
    return jnp.pad(flat, (0, rows * PACK_W - flat.shape[0])).reshape(rows, PACK_W)


def _unpack_rows(buf, shapes):
    flat = buf.reshape(-1)
    out, off = [], 0
    for shp in shapes:
        n = math.prod(shp)
        out.append(flat[off:off + n].reshape(shp))
        off += n
    return out


def _pack_lanes(parts):
    rows = []
    for p in parts:
        f = p.reshape(-1).astype(F32)
        n = -(-f.shape[0] // DH) * DH
        rows.append(jnp.pad(f, (0, n - f.shape[0])).reshape(-1, DH))
    buf = jnp.concatenate(rows, axis=0)
    return jnp.pad(buf, ((0, -buf.shape[0] % 8), (0, 0)))


def _unpack_lanes(buf, shapes):
    out, off = [], 0
    for shp in shapes:
        n = math.prod(shp)
        r = -(-n // DH)
        out.append(buf[off:off + r].reshape(-1)[:n].reshape(shp))
        off += r
    return out


def kernel(x, meta_tokens, lb_logits, mix_norm_w, w_in, hg_norm_w, gd_conv_w, gd_a_log, gd_dt_bias, gd_norm_w, w_branch_a, w_branch_b, w_out, ffn_norm_w, w_ffn_in, w_ffn_out, final_norm_w, loss_target, m_meta_tokens, m_lb_logits, m_mix_norm_w, m_w_in, m_hg_norm_w, m_gd_conv_w, m_gd_a_log, m_gd_dt_bias, m_gd_norm_w, m_w_branch_a, m_w_branch_b, m_w_out, m_ffn_norm_w, m_w_ffn_in, m_w_ffn_out, m_final_norm_w, v_meta_tokens, v_lb_logits, v_mix_norm_w, v_w_in, v_hg_norm_w, v_gd_conv_w, v_gd_a_log, v_gd_dt_bias, v_gd_norm_w, v_w_branch_a, v_w_branch_b, v_w_out, v_ffn_norm_w, v_w_ffn_in, v_w_ffn_out, v_final_norm_w):
    args = dict(locals())
    w = {n: args[n] for n in SMALL + tuple(n for n, _ in BIG)}
    m = {n: args["m_" + n] for n in w}
    v = {n: args["v_" + n] for n in w}
    xi, yi, ci = lax.axis_index("x"), lax.axis_index("y"), lax.axis_index("c")
    shard = 2 * xi + yi
    big_local = {n: w[n][0] for n, _ in BIG}
    big_shapes = [big_local[n].shape for n, _ in BIG]

    meta_cols, conv_cols = meta_tokens.shape[1], gd_conv_w.shape[-1]
    sm_all = _gather_all(_pack_lanes([meta_tokens, gd_conv_w[0]]), "gather_meta")
    sm_parts = [_unpack_lanes(sm_all[2 * s], [meta_tokens.shape, gd_conv_w[0].shape]) for s in range(4)]
    meta_full = jnp.concatenate([p[0] for p in sm_parts], axis=1)
    conv_full = jnp.concatenate([p[1] for p in sm_parts], axis=1)
    full = _gather_weights(_pack_rows([big_local[n] for n, _ in BIG], BF16, PACK_ALIGN))
    per_shard = [_unpack_rows(full[s], big_shapes) for s in range(4)]
    wfull = {n: jnp.concatenate([per_shard[s][i] for s in range(4)], axis=ax) for i, (n, ax) in enumerate(BIG)}

    loss, dx, g = _local_step(x[0], loss_target[0], meta_full, lb_logits, mix_norm_w, wfull["w_in"], hg_norm_w, conv_full,
                              gd_a_log, gd_dt_bias, gd_norm_w, wfull["w_branch_a"], wfull["w_branch_b"], wfull["w_out"],
                              ffn_norm_w, wfull["w_ffn_in"], wfull["w_ffn_out"], final_norm_w)
    loss = lax.psum(loss[0, 0], ("x", "y", "c"))

    def shard_of(a, ax, s):
        n = a.shape[ax] // 4
        return a[:, s * n:(s + 1) * n] if ax == 1 else a[s * n:(s + 1) * n]

    gp = jnp.stack([_pack_rows([shard_of(g[n], ax, s) for n, ax in BIG], F32, PACK_ALIGN) for s in range(4)])
    part = _add_half(gp, _swap_halves(gp), ci.reshape(1).astype(jnp.int32))
    g_half = _sum_slabs(_scatter_chips(part), "sum_chips")
    g_big = dict(zip([n for n, _ in BIG], _unpack_rows(_share_halves(g_half), big_shapes)))

    small_shapes = [g[n].shape for n in SMALL]
    g_all = _gather_all(_pack_lanes([g[n] for n in SMALL]), "gather_small")
    g_small = dict(zip(SMALL, _unpack_lanes(_sum_slabs(g_all, "sum_small"), small_shapes)))
    g_small["meta_tokens"] = lax.dynamic_slice_in_dim(g_small["meta_tokens"], shard * meta_cols, meta_cols, axis=1)
    g_small["gd_conv_w"] = lax.dynamic_slice_in_dim(g_small["gd_conv_w"], shard * conv_cols, conv_cols, axis=1)

    grad, delta, new_m, new_v = {}, {}, {}, {}
    for n, _ in BIG:
        grad[n] = g_big[n].reshape(w[n].shape)
        d_, m_, v_ = _adamw(g_big[n], big_local[n], m[n][0], v[n][0], "adamw_" + n)
        delta[n], new_m[n], new_v[n] = d_.reshape(w[n].shape), m_.reshape(w[n].shape), v_.reshape(w[n].shape)
    local_shapes = [w[n].shape for n in SMALL]
    d_, m_, v_ = _adamw(_pack_lanes([g_small[n] for n in SMALL]), _pack_lanes([w[n] for n in SMALL]),
                        _pack_lanes([m[n] for n in SMALL]), _pack_lanes([v[n] for n in SMALL]), "adamw_small")
    for n, gs, dd, mm, vv in zip(SMALL, [g_small[n] for n in SMALL], _unpack_lanes(d_, local_shapes), _unpack_lanes(m_, local_shapes),
                                 _unpack_lanes(v_, local_shapes)):
        grad[n], delta[n], new_m[n], new_v[n] = gs.reshape(w[n].shape), dd, mm, vv

    order = ["meta_tokens", "lb_logits", "mix_norm_w", "w_in", "hg_norm_w", "gd_conv_w", "gd_a_log", "gd_dt_bias", "gd_norm_w",
             "w_branch_a", "w_branch_b", "w_out", "ffn_norm_w", "w_ffn_in", "w_ffn_out", "final_norm_w"]
    return (loss, dx[None], *[grad[n] for n in order], *[delta[n] for n in order], *[new_m[n] for n in order],
            *[new_v[n] for n in order])
```

```python
import functools
import math

import jax
import jax.numpy as jnp
from jax import lax
from jax.experimental import pallas as pl
from jax.experimental.pallas import tpu as pltpu

F32, BF16 = jnp.float32, jnp.bfloat16
HI = lax.Precision.HIGHEST
EPS = 1e-6
D_MODEL = 2048
N_META = 16
FRONT = 256
CH = 64
SUB = 16
DH = 128
NH = 8
HW = NH * DH
CONV_K = 4
RT = 256
VMEM_LIMIT = 56 * 1024 * 1024
ADAM_LR, ADAM_B1, ADAM_B2, ADAM_EPS, ADAM_WD, ADAM_STEP = 0.001, 0.9, 0.999, 1e-08, 0.01, 10

NN = (((1,), (0,)), ((), ()))
NT = (((1,), (1,)), ((), ()))
TN = (((0,), (0,)), ((), ()))


def _dot(a, b, dn=NN):
    return lax.dot_general(a.astype(BF16), b.astype(BF16), dn, preferred_element_type=F32)


def _dotx(a, b, dn=NN):
    return lax.dot_general(a, b, dn, precision=HI, preferred_element_type=F32)


def _call(body, *, name, grid, in_specs, out_specs, out_shape, scratch=(), sem=None):
    return pl.pallas_call(
        body, name=name, grid=grid, in_specs=in_specs, out_specs=out_specs, out_shape=out_shape,
        scratch_shapes=list(scratch),
        compiler_params=pltpu.CompilerParams(dimension_semantics=sem, vmem_limit_bytes=VMEM_LIMIT))


def _divmod(j, per):
    if per == 1:
        return j, 0
    return lax.div(j, jnp.int32(per)), lax.rem(j, jnp.int32(per))


def _sds(shape, dtype):
    return jax.ShapeDtypeStruct(tuple(shape), dtype)


def _sigmoid(x):
    return 1.0 / (1.0 + jnp.exp(-x))


def _silu(x):
    return x * _sigmoid(x)


def _dsilu(x):
    s = _sigmoid(x)
    return s * (1.0 + x * (1.0 - s))


def _tri(n, kind):
    r = lax.broadcasted_iota(jnp.int32, (n, n), 0)
    c = lax.broadcasted_iota(jnp.int32, (n, n), 1)
    return {"incl": r >= c, "strict": r > c, "upper": c >= r}[kind]


def _mm(a, b, mode, out_dtype, tm, tn, tk, name, add=None, n_outer=False, out_shards=0):
    sharded_b = b.ndim == 3
    if sharded_b:
        S, R, n = b.shape
        b_rows, b_cols = R, S * n
    else:
        b_rows, b_cols = b.shape
    if mode == "nn":
        (M, K), N, dn = a.shape, b_cols, NN
    elif mode == "nt":
        (M, K), N, dn = a.shape, b_rows, NT
    else:
        (K, M), N, dn = a.shape, b_cols, TN
    tm, tn, tk = min(tm, M), min(tn, N), min(tk, K)
    if sharded_b:
        tn, tk = (min(tn, n), tk) if mode != "nt" else (tn, min(tk, n))
    if out_shards:
        tn = min(tn, N // out_shards)
    assert M % tm == 0 and N % tn == 0 and K % tk == 0, (name, M, N, K, tm, tn, tk)
    nk = K // tk
    a_blk, a_idx = ((tm, tk), lambda i, j, k: (i, k)) if mode != "tn" else ((tk, tm), lambda i, j, k: (k, i))
    if not sharded_b:
        b_blk, b_idx = ((tk, tn), lambda i, j, k: (k, j)) if mode != "nt" else ((tn, tk), lambda i, j, k: (j, k))
    elif mode != "nt":
        per = n // tn
        assert n % tn == 0
        b_blk, b_idx = (None, tk, tn), lambda i, j, k: (_divmod(j, per)[0], k, _divmod(j, per)[1])
    else:
        per = n // tk
        assert n % tk == 0
        b_blk, b_idx = (None, tn, tk), lambda i, j, k: (_divmod(k, per)[0], j, _divmod(k, per)[1])
    if out_shards:
        per_o = N // out_shards // tn
        assert (N // out_shards) % tn == 0
        o_blk, o_idx = (None, tm, tn), lambda i, j, k: (_divmod(j, per_o)[0], i, _divmod(j, per_o)[1])
        o_shape = (out_shards, M, N // out_shards)
    else:
        o_blk, o_idx, o_shape = (tm, tn), (lambda i, j, k: (i, j)), (M, N)
    c_idx = lambda i, j, k: (i, j)
    if n_outer:
        sw = lambda f: (lambda j, i, k: f(i, j, k))
        a_idx, b_idx, o_idx, c_idx = sw(a_idx), sw(b_idx), sw(o_idx), sw(c_idx)
        grid = (N // tn, M // tm, nk)
    else:
        grid = (M // tm, N // tn, nk)
    has_add = add is not None

    def body(*refs):
        if has_add:
            a_ref, b_ref, c_ref, o_ref, acc_ref = refs
        else:
            a_ref, b_ref, o_ref, acc_ref = refs
            c_ref = None
        part = lax.dot_general(a_ref[...].astype(BF16), b_ref[...].astype(BF16), dn, preferred_element_type=F32)

        def fin(val):
            if has_add:
                val = val + c_ref[...]
            o_ref[...] = val.astype(out_dtype)

        if nk == 1:
            fin(part)
        else:
            k = pl.program_id(2)

            @pl.when(k == 0)
            def _():
                acc_ref[...] = part

            @pl.when(k > 0)
            def _():
                acc_ref[...] += part

            @pl.when(k == nk - 1)
            def _():
                fin(acc_ref[...])

    in_specs = [pl.BlockSpec(a_blk, a_idx), pl.BlockSpec(b_blk, b_idx)]
    args = [a, b]
    if has_add:
        in_specs.append(pl.BlockSpec((tm, tn), c_idx))
        args.append(add)
    acc_shape = (tm, tn) if nk > 1 else (8, 128)
    return _call(body, name=name, grid=grid, in_specs=in_specs, out_specs=pl.BlockSpec(o_blk, o_idx),
                 out_shape=_sds(o_shape, out_dtype), scratch=[pltpu.VMEM(acc_shape, F32)],
                 sem=("parallel", "parallel", "arbitrary"))(*args)


def _rms1_fwd(x, meta, w):
    seq, d = x.shape
    nt = (FRONT + seq) // RT

    def body(x_ref, m_ref, w_ref, o_ref):
        i = pl.program_id(0)

        def norm(v):
            r = lax.rsqrt(jnp.mean(v * v, axis=-1, keepdims=True) + EPS)
            return (v * r * w_ref[...]).astype(BF16)

        @pl.when(i == 0)
        def _():
            o_ref[0:RT - N_META, :] = jnp.zeros((RT - N_META, d), BF16)
            o_ref[RT - N_META:RT, :] = norm(m_ref[...])

        @pl.when(i > 0)
        def _():
            o_ref[...] = norm(x_ref[...])

    return _call(body, name="rms1_fwd", grid=(nt,),
                 in_specs=[pl.BlockSpec((RT, d), lambda i: (jnp.maximum(i - 1, 0), 0)),
                           pl.BlockSpec((N_META, d), lambda i: (0, 0)),
                           pl.BlockSpec((1, d), lambda i: (0, 0))],
                 out_specs=pl.BlockSpec((RT, d), lambda i: (i, 0)),
                 out_shape=_sds((FRONT + seq, d), BF16), sem=("parallel",))(x, meta, w)


def _rms1_bwd(x, meta, w, dxn, dh1):
    seq, d = x.shape
    nt = (FRONT + seq) // RT

    def body(x_ref, m_ref, w_ref, g_ref, r_ref, dx_ref, dm_ref, dw_ref):
        i = pl.program_id(0)

        def bwd(v, g):
            r = lax.rsqrt(jnp.mean(v * v, axis=-1, keepdims=True) + EPS)
            vh = v * r
            gh = g * w_ref[...]
            return r * (gh - vh * jnp.mean(gh * vh, axis=-1, keepdims=True)), jnp.sum(g * vh, axis=0, keepdims=True)

        @pl.when(i == 0)
        def _():
            dm, dw = bwd(m_ref[...], g_ref[RT - N_META:RT, :])
            dm_ref[...] = dm
            dw_ref[...] = dw

        @pl.when(i > 0)
        def _():
            dx, dw = bwd(x_ref[...], g_ref[...])
            dx_ref[...] = dx + r_ref[...]
            dw_ref[...] += dw

    xs = pl.BlockSpec((RT, d), lambda i: (jnp.maximum(i - 1, 0), 0))
    return _call(body, name="rms1_bwd", grid=(nt,),
                 in_specs=[xs, pl.BlockSpec((N_META, d), lambda i: (0, 0)), pl.BlockSpec((1, d), lambda i: (0, 0)),
                           pl.BlockSpec((RT, d), lambda i: (i, 0)), xs],
                 out_specs=[xs, pl.BlockSpec((N_META, d), lambda i: (0, 0)), pl.BlockSpec((1, d), lambda i: (0, 0))],
                 out_shape=[_sds((seq, d), F32), _sds((N_META, d), F32), _sds((1, d), F32)],
                 sem=("arbitrary",))(x, meta, w, dxn, dh1)


def _merge_fwd(proj, za, zb):
    seq, d = za.shape
    off = FRONT // RT
    ca, cb = 8 * HW // d, 8 * HW // d + 1

    def body(ga_ref, gb_ref, za_ref, zb_ref, o_ref):
        o_ref[...] = (_sigmoid(ga_ref[...]) * za_ref[...] + _sigmoid(gb_ref[...]) * zb_ref[...]).astype(BF16)

    zs = pl.BlockSpec((RT, d), lambda i: (i, 0))
    return _call(body, name="merge_fwd", grid=(seq // RT,),
                 in_specs=[pl.BlockSpec((RT, d), lambda i: (i + off, ca)), pl.BlockSpec((RT, d), lambda i: (i + off, cb)), zs, zs],
                 out_specs=zs, out_shape=_sds((seq, d), BF16), sem=("parallel",))(proj, proj, za, zb)


def _merge_bwd(proj, za, zb, dmerged):
    seq, d = za.shape
    off = FRONT // RT
    ca, cb = 8 * HW // d, 8 * HW // d + 1
    nt = (FRONT + seq) // RT

    def body(ga_ref, gb_ref, za_ref, zb_ref, dm_ref, dza_ref, dzb_ref, dg_ref):
        i = pl.program_id(0)

        @pl.when(i < off)
        def _():
            dg_ref[...] = jnp.zeros((RT, 2 * d), BF16)

        @pl.when(i >= off)
        def _():
            sa, sb, dm = _sigmoid(ga_ref[...]), _sigmoid(gb_ref[...]), dm_ref[...]
            dza_ref[...] = (dm * sa).astype(BF16)
            dzb_ref[...] = (dm * sb).astype(BF16)
            dg_ref[:, 0:d] = (dm * za_ref[...] * sa * (1.0 - sa)).astype(BF16)
            dg_ref[:, d:2 * d] = (dm * zb_ref[...] * sb * (1.0 - sb)).astype(BF16)

    rs = pl.BlockSpec((RT, d), lambda i: (jnp.maximum(i - off, 0), 0))
    return _call(body, name="merge_bwd", grid=(nt,),
                 in_specs=[pl.BlockSpec((RT, d), lambda i: (i, ca)), pl.BlockSpec((RT, d), lambda i: (i, cb)), rs, rs, rs],
                 out_specs=[rs, rs, pl.BlockSpec((RT, 2 * d), lambda i: (i, 0))],
                 out_shape=[_sds((seq, d), BF16), _sds((seq, d), BF16), _sds((FRONT + seq, 2 * d), BF16)],
                 sem=("arbitrary",))(proj, proj, za, zb, dmerged)


def _resid_norm_fwd(x, mix, w):
    seq, d = x.shape

    def body(x_ref, m_ref, w_ref, h_ref, n_ref):
        h = x_ref[...] + m_ref[...]
        h_ref[...] = h
        r = lax.rsqrt(jnp.mean(h * h, axis=-1, keepdims=True) + EPS)
        n_ref[...] = (h * r * w_ref[...]).astype(BF16)

    rs = pl.BlockSpec((RT, d), lambda i: (i, 0))
    return _call(body, name="resid_norm_fwd", grid=(seq // RT,),
                 in_specs=[rs, rs, pl.BlockSpec((1, d), lambda i: (0, 0))], out_specs=[rs, rs],
                 out_shape=[_sds((seq, d), F32), _sds((seq, d), BF16)], sem=("parallel",))(x, mix, w)


def _resid_norm_bwd(h1, w, dn2, dh2):
    seq, d = h1.shape

    def body(h_ref, w_ref, g_ref, r_ref, o_ref, ob_ref, dw_ref):
        i = pl.program_id(0)
        h, g = h_ref[...], g_ref[...]
        r = lax.rsqrt(jnp.mean(h * h, axis=-1, keepdims=True) + EPS)
        hh = h * r
        gh = g * w_ref[...]
        dh = r_ref[...] + r * (gh - hh * jnp.mean(gh * hh, axis=-1, keepdims=True))
        o_ref[...] = dh
        ob_ref[...] = dh.astype(BF16)
        dw = jnp.sum(g * hh, axis=0, keepdims=True)

        @pl.when(i == 0)
        def _():
            dw_ref[...] = dw

        @pl.when(i > 0)
        def _():
            dw_ref[...] += dw

    rs = pl.BlockSpec((RT, d), lambda i: (i, 0))
    ws = pl.BlockSpec((1, d), lambda i: (0, 0))
    return _call(body, name="resid_norm_bwd", grid=(seq // RT,), in_specs=[rs, ws, rs, rs], out_specs=[rs, rs, ws],
                 out_shape=[_sds((seq, d), F32), _sds((seq, d), BF16), _sds((1, d), F32)], sem=("arbitrary",))(h1, w, dn2, dh2)


def _swiglu_fwd(gu):
    seq, f2 = gu.shape
    ff = f2 // 2
    tc = 512
    nb = ff // tc

    def body(g_ref, u_ref, o_ref):
        o_ref[...] = (_silu(g_ref[...]) * u_ref[...]).astype(BF16)

    return _call(body, name="swiglu_fwd", grid=(seq // RT, nb),
                 in_specs=[pl.BlockSpec((RT, tc), lambda i, j: (i, j)), pl.BlockSpec((RT, tc), lambda i, j: (i, j + nb))],
                 out_specs=pl.BlockSpec((RT, tc), lambda i, j: (i, j)), out_shape=_sds((seq, ff), BF16),
                 sem=("parallel", "parallel"))(gu, gu)


def _swiglu_bwd(gu, dact):
    seq, f2 = gu.shape
    ff = f2 // 2
    tc = 512
    nb = ff // tc

    def body(g_ref, u_ref, d_ref, dg_ref, du_ref):
        g, d = g_ref[...], d_ref[...]
        dg_ref[...] = (d * u_ref[...] * _dsilu(g)).astype(BF16)
        du_ref[...] = (d * _silu(g)).astype(BF16)

    bs = pl.BlockSpec((RT, tc), lambda i, j: (i, j))
    dgate, dup = _call(body, name="swiglu_bwd", grid=(seq // RT, nb),
                       in_specs=[bs, pl.BlockSpec((RT, tc), lambda i, j: (i, j + nb)), bs], out_specs=[bs, bs],
                       out_shape=[_sds((seq, ff), BF16), _sds((seq, ff), BF16)], sem=("parallel", "parallel"))(gu, gu, dact)
    return jnp.concatenate([dgate, dup], axis=1)


def _loss_head(h1, f, w, tgt):
    seq, d = h1.shape
    nt = seq // RT

    def body(h_ref, f_ref, w_ref, t_ref, l_ref, dh_ref, dhb_ref, dw_ref):
        i = pl.program_id(0)
        h = h_ref[...] + f_ref[...]
        r = lax.rsqrt(jnp.mean(h * h, axis=-1, keepdims=True) + EPS)
        hh = h * r
        err = hh * w_ref[...] - t_ref[...]
        l_ref[...] = jnp.full((8, 128), 0.5 * jnp.sum(jnp.mean(err * err, axis=-1, keepdims=True)), F32)
        dy = err * (1.0 / d)
        gh = dy * w_ref[...]
        dh = r * (gh - hh * jnp.mean(gh * hh, axis=-1, keepdims=True))
        dh_ref[...] = dh
        dhb_ref[...] = dh.astype(BF16)
        dw = jnp.sum(dy * hh, axis=0, keepdims=True)

        @pl.when(i == 0)
        def _():
            dw_ref[...] = dw

        @pl.when(i > 0)
        def _():
            dw_ref[...] += dw

    rs = pl.BlockSpec((RT, d), lambda i: (i, 0))
    ws = pl.BlockSpec((1, d), lambda i: (0, 0))
    return _call(body, name="loss_head", grid=(nt,), in_specs=[rs, rs, ws, rs],
                 out_specs=[pl.BlockSpec((8, 128), lambda i: (i, 0)), rs, rs, ws],
                 out_shape=[_sds((nt * 8, 128), F32), _sds((seq, d), F32), _sds((seq, d), BF16), _sds((1, d), F32)],
                 sem=("arbitrary",))(h1, f, w, tgt)


def _sum_tiles(lt):
    n = lt.shape[0]

    def body(l_ref, o_ref):
        v = l_ref[...]
        r = lax.broadcasted_iota(jnp.int32, v.shape, 0)
        c = lax.broadcasted_iota(jnp.int32, v.shape, 1)
        o_ref[...] = jnp.sum(jnp.where((r % 8 == 0) & (c == 0), v, 0.0), keepdims=True)

    return _call(body, name="loss_sum", grid=(1,), in_specs=[pl.BlockSpec((n, 128), lambda i: (0, 0))],
                 out_specs=pl.BlockSpec((1, 1), lambda i: (0, 0)), out_shape=_sds((1, 1), F32))(lt)


def _gated_norm_fwd(o, g, nw):
    r = lax.rsqrt(jnp.mean(o * o, axis=-1, keepdims=True) + EPS)
    return o * r * nw * _silu(g)


def _gated_norm_bwd(o, g, nw, dout):
    r = lax.rsqrt(jnp.mean(o * o, axis=-1, keepdims=True) + EPS)
    oh = o * r
    don = dout * _silu(g)
    dg = dout * (oh * nw) * _dsilu(g)
    dnw = jnp.sum(don * oh, axis=0, keepdims=True)
    doh = don * nw
    return r * (doh - oh * jnp.mean(doh * oh, axis=-1, keepdims=True)), dg, dnw


def _hg_gates(fs, lbl):
    l0, l1 = lbl[0:1, :], lbl[1:2, :]
    m = jnp.maximum(l0, l1)
    e0, e1 = jnp.exp(l0 - m), jnp.exp(l1 - m)
    lb = e0 / (e0 + e1)
    sig = _sigmoid(fs)
    f = lb + (1.0 - lb) * sig
    return lb, sig, f, jnp.log(f), (1.0 - lb) * _sigmoid(-fs)


def _cumsum(w):
    row = lax.broadcasted_iota(jnp.int32, w.shape, 0)
    s = 1
    while s < CH:
        w = w + jnp.where(row >= s, pltpu.roll(w, s, 0), 0.0)
        s *= 2
    return w


def _rcumsum(w):
    row = lax.broadcasted_iota(jnp.int32, w.shape, 0)
    s = 1
    while s < CH:
        w = w + jnp.where(row < CH - s, pltpu.roll(w, CH - s, 0), 0.0)
        s *= 2
    return w


def _decay_blocks(q, k, b, p_ref):
    p_ref[...] = jnp.zeros((CH, CH), F32)
    m16 = _tri(SUB, "incl")
    for I in range(CH // SUB):
        s0 = I * SUB
        bI, qI, kI = b[s0:s0 + SUB], q[s0:s0 + SUB], k[s0:s0 + SUB]
        dec = jnp.exp(jnp.minimum(bI[:, None, :] - bI[None, :, :], 0.0))
        pii = jnp.sum(qI[:, None, :] * kI[None, :, :] * dec, axis=-1)
        p_ref[s0:s0 + SUB, s0:s0 + SUB] = jnp.where(m16, pii, 0.0)
        if I > 0:
            rI = b[s0 - 1:s0]
            qs = qI * jnp.exp(bI - rI)
            ks = k[0:s0] * jnp.exp(rI - b[0:s0])
            p_ref[s0:s0 + SUB, 0:s0] = _dot(qs, ks, NT)


HB = 4
NHB = NH // HB
OFF = FRONT // RT


def _head_specs(rev=None):
    row = (lambda i: i) if rev is None else rev
    col = lambda g: pl.BlockSpec((RT, HB * DH), lambda h, i: (row(i), g * NHB + h))
    full = pl.BlockSpec((RT, HB * DH), lambda h, i: (row(i), h))
    real = pl.BlockSpec((RT, HB * DH), lambda h, i: (jnp.maximum(row(i) - OFF, 0), h))
    state = lambda cpt: pl.BlockSpec((HB, cpt, DH, DH), lambda h, i: (h, row(i), 0, 0))
    scal = pl.BlockSpec((HB, RT, DH), lambda h, i: (h, row(i), 0))
    return col, full, real, state, scal


def _hgrn2_fwd(proj, lb_logits, nw):
    tp = proj.shape[0]
    nt, cpt = tp // RT, RT // CH

    def body(q_ref, f_ref, i_ref, g_ref, lbl_ref, nw_ref, og_ref, or_ref, st_ref, s_ref, p_ref):
        @pl.when(pl.program_id(1) == 0)
        def _():
            s_ref[...] = jnp.zeros((HB, DH, DH), F32)

        def chunk(c, carry):
            rows = pl.ds(pl.multiple_of(c * CH, CH), CH)
            for hh in range(HB):
                cols = slice(hh * DH, (hh + 1) * DH)
                _, _, _, w, k = _hg_gates(f_ref[rows, cols], lbl_ref[:, cols])
                q, v = _silu(q_ref[rows, cols]), i_ref[rows, cols]
                b = _cumsum(w)
                st = s_ref[hh]
                st_ref[hh, c] = st
                _decay_blocks(q, k, b, p_ref.at[hh])
                o = _dot(q * jnp.exp(b), st, NT) + _dot(p_ref[hh], v)
                bl = b[CH - 1:CH]
                s_ref[hh] = st * jnp.exp(bl) + _dot(v, k * jnp.exp(bl - b), TN)
                or_ref[rows, cols] = o
                og_ref[rows, cols] = _gated_norm_fwd(o, g_ref[rows, cols], nw_ref[...]).astype(BF16)
            return carry

        lax.fori_loop(0, cpt, chunk, 0)

    col, full, real, state, _ = _head_specs()
    return _call(body, name="hgrn2_fwd", grid=(NHB, nt),
                 in_specs=[col(0), col(1), col(2), col(3), pl.BlockSpec((2, HB * DH), lambda h, i: (0, h)),
                           pl.BlockSpec((1, DH), lambda h, i: (0, 0))],
                 out_specs=[real, full, state(cpt)],
                 out_shape=[_sds((tp - FRONT, HW), BF16), _sds((tp, HW), F32), _sds((NH, tp // CH, DH, DH), F32)],
                 scratch=[pltpu.VMEM((HB, DH, DH), F32), pltpu.VMEM((HB, CH, CH), F32)],
                 sem=("parallel", "arbitrary"))(proj, proj, proj, proj, lb_logits, nw)


def _hgrn2_bwd(proj, lb_logits, nw, o_raw, states, dog):
    tp = proj.shape[0]
    nt, cpt = tp // RT, RT // CH

    def body(q_ref, f_ref, i_ref, g_ref, lbl_ref, nw_ref, or_ref, st_ref, dog_ref,
             dq_ref, df_ref, di_ref, dg_ref, dl_ref, dnw_ref, ds_ref, p_ref, dk_ref, dqa_ref, do_ref):
        step = pl.program_id(1)

        @pl.when(step == 0)
        def _():
            ds_ref[...] = jnp.zeros((HB, DH, DH), F32)
            dl_ref[...] = jnp.zeros((2, HB * DH), F32)

        @pl.when((step == 0) & (pl.program_id(0) == 0))
        def _():
            dnw_ref[...] = jnp.zeros((1, DH), F32)

        front = nt - 1 - step < OFF
        for hh in range(HB):
            cols = slice(hh * DH, (hh + 1) * DH)
            dog_t = jnp.where(front, 0.0, dog_ref[:, cols])
            do_t, dg_t, dnw = _gated_norm_bwd(or_ref[:, cols], g_ref[:, cols], nw_ref[...], dog_t)
            do_ref[:, cols] = do_t
            dg_ref[:, cols] = dg_t.astype(BF16)
            dnw_ref[...] += dnw
        tril = _tri(CH, "incl")
        m16 = _tri(SUB, "incl")

        def chunk(cc, carry):
            c = cpt - 1 - cc
            rows = pl.ds(pl.multiple_of(c * CH, CH), CH)
            for hh in range(HB):
                cols = slice(hh * DH, (hh + 1) * DH)
                fs = f_ref[rows, cols]
                lb, sig, f, w, k = _hg_gates(fs, lbl_ref[:, cols])
                hq = q_ref[rows, cols]
                q, v, do = _silu(hq), i_ref[rows, cols], do_ref[rows, cols]
                b = _cumsum(w)
                bl = b[CH - 1:CH]
                eb = jnp.exp(b)
                qs, kd = q * eb, k * jnp.exp(bl - b)
                st, dst = st_ref[hh, c], ds_ref[hh]
                _decay_blocks(q, k, b, p_ref.at[hh])
                dv = _dot(p_ref[hh], do, TN) + _dot(kd, dst, NT)
                dp = jnp.where(tril, _dotx(do, v, NT), 0.0)
                dqa, dka = dqa_ref.at[hh], dk_ref.at[hh]
                dqa[...] = eb * _dotx(do, st)
                dka[...] = jnp.exp(bl - b) * _dotx(v, dst)
                for I in range(CH // SUB):
                    s0 = I * SUB
                    bI, qI, kI = b[s0:s0 + SUB], q[s0:s0 + SUB], k[s0:s0 + SUB]
                    dec = jnp.exp(jnp.minimum(bI[:, None, :] - bI[None, :, :], 0.0))
                    dpii = jnp.where(m16, dp[s0:s0 + SUB, s0:s0 + SUB], 0.0)[:, :, None] * dec
                    dqa[s0:s0 + SUB, :] += jnp.sum(dpii * kI[None, :, :], axis=1)
                    dka[s0:s0 + SUB, :] += jnp.sum(dpii * qI[:, None, :], axis=0)
                    if I > 0:
                        rI = b[s0 - 1:s0]
                        eq, ek = jnp.exp(bI - rI), jnp.exp(rI - b[0:s0])
                        dpij = dp[s0:s0 + SUB, 0:s0]
                        dqa[s0:s0 + SUB, :] += eq * _dotx(dpij, k[0:s0] * ek)
                        dka[0:s0, :] += ek * _dotx(dpij, qI * eq, TN)
                dq, dk = dqa[...], dka[...]
                st_end = st * jnp.exp(bl) + _dotx(v, kd, TN)
                dw = _rcumsum(q * dq - k * dk) + jnp.sum(dst * st_end, axis=0, keepdims=True)
                ds_ref[hh] = dst * jnp.exp(bl) + _dotx(do, qs, TN)
                one_m = 1.0 - sig
                dq_ref[rows, cols] = (dq * _dsilu(hq)).astype(BF16)
                df_ref[rows, cols] = ((dw / f - dk) * (1.0 - lb) * sig * one_m).astype(BF16)
                di_ref[rows, cols] = dv.astype(BF16)
                dl_ref[0:1, cols] += jnp.sum((dw / f - dk) * one_m, axis=0, keepdims=True)
            return carry

        lax.fori_loop(0, cpt, chunk, 0)

        @pl.when(step == nt - 1)
        def _():
            lbl = lbl_ref[...]
            l0, l1 = lbl[0:1, :], lbl[1:2, :]
            m = jnp.maximum(l0, l1)
            e0, e1 = jnp.exp(l0 - m), jnp.exp(l1 - m)
            p0 = e0 / (e0 + e1)
            dl0 = dl_ref[0:1, :] * p0 * (1.0 - p0)
            dl_ref[0:1, :] = dl0
            dl_ref[1:2, :] = -dl0

    col, full, real, state, _ = _head_specs(lambda i: nt - 1 - i)
    lbs = pl.BlockSpec((2, HB * DH), lambda h, i: (0, h))
    return _call(body, name="hgrn2_bwd", grid=(NHB, nt),
                 in_specs=[col(0), col(1), col(2), col(3), lbs, pl.BlockSpec((1, DH), lambda h, i: (0, 0)), full,
                           state(cpt), real],
                 out_specs=[full, full, full, full, lbs, pl.BlockSpec((1, DH), lambda h, i: (0, 0))],
                 out_shape=[_sds((tp, HW), BF16)] * 4 + [_sds((2, HW), F32), _sds((1, DH), F32)],
                 scratch=[pltpu.VMEM((HB, DH, DH), F32), pltpu.VMEM((HB, CH, CH), F32), pltpu.VMEM((HB, CH, DH), F32),
                          pltpu.VMEM((HB, CH, DH), F32), pltpu.VMEM((RT, HB * DH), F32)],
                 sem=("arbitrary", "arbitrary"))(proj, proj, proj, proj, lb_logits, nw, o_raw, states, dog)


GQ0 = 4 * HW
CW = 3 * HW


def _gd_scalars(ab, alog, dtb):
    g = -jnp.exp(alog) * jax.nn.softplus(ab + dtb)
    return g, _sigmoid(ab)


def _conv_ext_specs(row_of):
    main = [pl.BlockSpec((RT, HW), lambda i, g=g: (row_of(i), GQ0 // HW + g)) for g in range(3)]
    prev = [pl.BlockSpec((8, HW), lambda i, g=g: (jnp.maximum(row_of(i) * (RT // 8) - 1, 0), GQ0 // HW + g)) for g in range(3)]
    return main + prev


def _conv_fill(ext_ref, xs, xps, first):
    for g in range(3):
        ext_ref[0:8, g * HW:(g + 1) * HW] = jnp.where(first, 0.0, xps[g][...])
        ext_ref[8:8 + RT, g * HW:(g + 1) * HW] = xs[g][...]


def _conv_apply(ext_ref, cw):
    y = cw[CONV_K - 1:CONV_K, :] * ext_ref[pl.ds(8, RT), :]
    for s in range(1, CONV_K):
        y += cw[CONV_K - 1 - s:CONV_K - s, :] * ext_ref[pl.ds(8 - s, RT), :]
    return y


def _gdn_prep_fwd(proj, pab, conv_w, alog, dtb):
    tp = proj.shape[0]
    nt = tp // RT

    def body(x0, x1, x2, p0, p1, p2, ab_ref, cw_ref, al_ref, dt_ref, q_ref, k_ref, v_ref, g_ref, b_ref, ext_ref):
        _conv_fill(ext_ref, (x0, x1, x2), (p0, p1, p2), pl.program_id(0) == 0)
        a = _silu(_conv_apply(ext_ref, cw_ref[...]))
        for h in range(NH):
            for part, ref, sc in ((0, q_ref, DH ** -0.5), (1, k_ref, 1.0)):
                seg = a[:, part * HW + h * DH:part * HW + (h + 1) * DH]
                ref[:, h * DH:(h + 1) * DH] = seg * (lax.rsqrt(jnp.sum(seg * seg, axis=-1, keepdims=True) + EPS) * sc)
        v_ref[...] = a[:, 2 * HW:3 * HW]
        g, beta = _gd_scalars(ab_ref[...], al_ref[...], dt_ref[...])
        for h in range(NH):
            g_ref[h] = jnp.broadcast_to(g[:, h:h + 1], (RT, DH))
            b_ref[h] = jnp.broadcast_to(beta[:, NH + h:NH + h + 1], (RT, DH))

    hs = pl.BlockSpec((RT, HW), lambda i: (i, 0))
    sc = pl.BlockSpec((NH, RT, DH), lambda i: (0, i, 0))
    one = pl.BlockSpec((1, DH), lambda i: (0, 0))
    return _call(body, name="gdn_prep_fwd", grid=(nt,),
                 in_specs=_conv_ext_specs(lambda i: i) + [pl.BlockSpec((RT, DH), lambda i: (i, 0)),
                                                           pl.BlockSpec((CONV_K, CW), lambda i: (0, 0)), one, one],
                 out_specs=[hs, hs, hs, sc, sc],
                 out_shape=[_sds((tp, HW), F32)] * 3 + [_sds((NH, tp, DH), F32)] * 2,
                 scratch=[pltpu.VMEM((RT + 8, CW), F32)], sem=("parallel",))(*([proj] * 6), pab, conv_w, alog, dtb)


def _gdn_prep_bwd(proj, pab, conv_w, alog, dtb, dq, dk, dv, dgb, dbb):
    tp = proj.shape[0]
    nt = tp // RT

    def body(x0, x1, x2, p0, p1, p2, ab_ref, cw_ref, al_ref, dt_ref, dq_ref, dk_ref, dv_ref, dg_ref, db_ref,
             dx_ref, dab_ref, dcw_ref, dal_ref, ddt_ref, ext_ref, dy_ref):
        step = pl.program_id(0)
        i = nt - 1 - step

        @pl.when(step == 0)
        def _():
            dy_ref[RT:RT + 8, :] = jnp.zeros((8, CW), F32)
            dcw_ref[...] = jnp.zeros((8, CW), F32)
            dal_ref[...] = jnp.zeros((1, DH), F32)
            ddt_ref[...] = jnp.zeros((1, DH), F32)

        _conv_fill(ext_ref, (x0, x1, x2), (p0, p1, p2), i == 0)
        cw = cw_ref[...]
        y = _conv_apply(ext_ref, cw)
        a = _silu(y)
        dsl = _dsilu(y)
        for h in range(NH):
            for part, ref, sc in ((0, dq_ref, DH ** -0.5), (1, dk_ref, 1.0)):
                lo = part * HW + h * DH
                seg = a[:, lo:lo + DH]
                r = lax.rsqrt(jnp.sum(seg * seg, axis=-1, keepdims=True) + EPS)
                xh = seg * r
                dxh = ref[:, h * DH:(h + 1) * DH] * sc
                dy_ref[0:RT, lo:lo + DH] = r * (dxh - xh * jnp.sum(dxh * xh, axis=-1, keepdims=True)) * dsl[:, lo:lo + DH]
        dy_ref[0:RT, 2 * HW:3 * HW] = dv_ref[...] * dsl[:, 2 * HW:3 * HW]
        dy = dy_ref[0:RT, :]
        dx = cw[CONV_K - 1:CONV_K, :] * dy
        dcw_ref[CONV_K - 1:CONV_K, :] += jnp.sum(dy * ext_ref[pl.ds(8, RT), :], axis=0, keepdims=True)
        for s in range(1, CONV_K):
            dx += cw[CONV_K - 1 - s:CONV_K - s, :] * dy_ref[pl.ds(s, RT), :]
            dcw_ref[CONV_K - 1 - s:CONV_K - s, :] += jnp.sum(dy * ext_ref[pl.ds(8 - s, RT), :], axis=0, keepdims=True)
        dx_ref[...] = dx.astype(BF16)
        dy_ref[RT:RT + 8, :] = dy[0:8, :]
        ab = ab_ref[...]
        g, beta = _gd_scalars(ab, al_ref[...], dt_ref[...])
        lane = lax.broadcasted_iota(jnp.int32, (RT, DH), 1)
        dgl = jnp.zeros((RT, DH), F32)
        dbl = jnp.zeros((RT, DH), F32)
        for h in range(NH):
            dgl = jnp.where(lane == h, dg_ref[h], dgl)
            dbl = jnp.where(lane == NH + h, db_ref[h], dbl)
        dsp = dgl * (-jnp.exp(al_ref[...])) * _sigmoid(ab + dt_ref[...])
        dab_ref[...] = (dsp + dbl * beta * (1.0 - beta)).astype(BF16)
        ddt_ref[...] += jnp.sum(dsp, axis=0, keepdims=True)
        dal_ref[...] += jnp.sum(dgl * g, axis=0, keepdims=True)

    hs = pl.BlockSpec((RT, HW), lambda s: (nt - 1 - s, 0))
    sc = pl.BlockSpec((NH, RT, DH), lambda s: (0, nt - 1 - s, 0))
    one = pl.BlockSpec((1, DH), lambda s: (0, 0))
    xs = pl.BlockSpec((RT, CW), lambda s: (nt - 1 - s, 0))
    return _call(body, name="gdn_prep_bwd", grid=(nt,),
                 in_specs=_conv_ext_specs(lambda s: nt - 1 - s) + [
                     pl.BlockSpec((RT, DH), lambda s: (nt - 1 - s, 0)), pl.BlockSpec((CONV_K, CW), lambda s: (0, 0)),
                     one, one, hs, hs, hs, sc, sc],
                 out_specs=[xs, pl.BlockSpec((RT, DH), lambda s: (nt - 1 - s, 0)), pl.BlockSpec((8, CW), lambda s: (0, 0)), one, one],
                 out_shape=[_sds((tp, CW), BF16), _sds((tp, DH), BF16), _sds((8, CW), F32), _sds((1, DH), F32), _sds((1, DH), F32)],
                 scratch=[pltpu.VMEM((RT + 8, CW), F32), pltpu.VMEM((RT + 8, CW), F32)],
                 sem=("arbitrary",))(*([proj] * 6), pab, conv_w, alog, dtb, dq, dk, dv, dgb, dbb)


def _unit_lower_inverse(a):
    r = lax.broadcasted_iota(jnp.int32, (CH, CH), 0)
    c = lax.broadcasted_iota(jnp.int32, (CH, CH), 1)
    blk_of = lambda t, size: lax.shift_right_logical(t, int(math.log2(size)))
    a16 = jnp.where(blk_of(r, SUB) == blk_of(c, SUB), a, 0.0)
    x = (r == c).astype(F32) - a16
    p = a16
    for _ in range(3):
        p = _dot(p, p)
        x = x + _dot(x, p)
    for blk in (2 * SUB, 4 * SUB):
        off = jnp.where((blk_of(r, blk) == blk_of(c, blk)) & (blk_of(r, blk // 2) != blk_of(c, blk // 2)), a, 0.0)
        x = x - _dot(x, _dot(off, x))
    return x


def _gdn_chunk_common(q, k, v, gl, bt):
    gc = _cumsum(gl)
    e = jnp.exp(gc)
    rel = jnp.exp(jnp.minimum(gc[:, 0:CH] - gc.T[0:CH, :], 0.0))
    kb = bt * k
    a = jnp.where(_tri(CH, "strict"), bt[:, 0:CH] * _dot(k, k, NT) * rel, 0.0)
    x = _unit_lower_inverse(a)
    w = _dot(x, kb * e)
    u = _dot(x, bt * v)
    attn = jnp.where(_tri(CH, "incl"), _dot(q, k, NT) * rel, 0.0)
    return gc, e, rel, kb, a, x, w, u, attn


def _gdn_fwd(q, k, v, gb, bb, proj, nw):
    tp = q.shape[0]
    nt, cpt = tp // RT, RT // CH

    def body(q_ref, k_ref, v_ref, g_ref, b_ref, z_ref, nw_ref, og_ref, or_ref, st_ref, s_ref):
        @pl.when(pl.program_id(1) == 0)
        def _():
            s_ref[...] = jnp.zeros((HB, DH, DH), F32)

        def chunk(c, carry):
            rows = pl.ds(pl.multiple_of(c * CH, CH), CH)
            for hh in range(HB):
                cols = slice(hh * DH, (hh + 1) * DH)
                qc, kc, vc = q_ref[rows, cols], k_ref[rows, cols], v_ref[rows, cols]
                gc, e, rel, kb, a, x, w, u, attn = _gdn_chunk_common(qc, kc, vc, g_ref[hh, rows, :], b_ref[hh, rows, :])
                s = s_ref[hh]
                st_ref[hh, c] = s
                vn = u - _dot(w, s)
                o = _dot(qc * e, s) + _dot(attn, vn)
                gl = gc[CH - 1:CH]
                s_ref[hh] = s * jnp.exp(gl) + _dot(kc * jnp.exp(gl - gc), vn, TN)
                or_ref[rows, cols] = o
                og_ref[rows, cols] = _gated_norm_fwd(o, z_ref[rows, cols], nw_ref[...]).astype(BF16)
            return carry

        lax.fori_loop(0, cpt, chunk, 0)

    col, full, real, state, scal = _head_specs()
    return _call(body, name="gdn_fwd", grid=(NHB, nt),
                 in_specs=[full, full, full, scal, scal, col(7), pl.BlockSpec((1, DH), lambda h, i: (0, 0))],
                 out_specs=[real, full, state(cpt)],
                 out_shape=[_sds((tp - FRONT, HW), BF16), _sds((tp, HW), F32), _sds((NH, tp // CH, DH, DH), F32)],
                 scratch=[pltpu.VMEM((HB, DH, DH), F32)], sem=("parallel", "arbitrary"))(q, k, v, gb, bb, proj, nw)


def _gdn_bwd(q, k, v, gb, bb, proj, nw, o_raw, states, dog):
    tp = q.shape[0]
    nt, cpt = tp // RT, RT // CH

    def body(q_ref, k_ref, v_ref, g_ref, b_ref, z_ref, nw_ref, or_ref, st_ref, dog_ref,
             dq_ref, dk_ref, dv_ref, dg_ref, db_ref, dz_ref, dnw_ref, ds_ref, do_ref):
        step = pl.program_id(1)

        @pl.when(step == 0)
        def _():
            ds_ref[...] = jnp.zeros((HB, DH, DH), F32)

        @pl.when((step == 0) & (pl.program_id(0) == 0))
        def _():
            dnw_ref[...] = jnp.zeros((1, DH), F32)

        front = nt - 1 - step < OFF
        for hh in range(HB):
            cols = slice(hh * DH, (hh + 1) * DH)
            dog_t = jnp.where(front, 0.0, dog_ref[:, cols])
            do_t, dz_t, dnw = _gated_norm_bwd(or_ref[:, cols], z_ref[:, cols], nw_ref[...], dog_t)
            do_ref[:, cols] = do_t
            dz_ref[:, cols] = dz_t.astype(BF16)
            dnw_ref[...] += dnw
        incl, strict = _tri(CH, "incl"), _tri(CH, "strict")
        ones = jnp.ones((CH, DH), F32)
        rowid = lax.broadcasted_iota(jnp.int32, (CH, DH), 0)

        def rsum(t):
            return jnp.sum(t, axis=-1, keepdims=True)

        def chunk(cc, carry):
            c = cpt - 1 - cc
            rows = pl.ds(pl.multiple_of(c * CH, CH), CH)
            for hh in range(HB):
                cols = slice(hh * DH, (hh + 1) * DH)
                qc, kc, vc, do = q_ref[rows, cols], k_ref[rows, cols], v_ref[rows, cols], do_ref[rows, cols]
                bt = b_ref[hh, rows, :]
                gc, e, rel, kb, a, x, w, u, attn = _gdn_chunk_common(qc, kc, vc, g_ref[hh, rows, :], bt)
                s, dsn = st_ref[hh, c], ds_ref[hh]
                gl = gc[CH - 1:CH]
                el = jnp.exp(gl)
                cdec = jnp.exp(gl - gc)
                vn = u - _dot(w, s)
                dvn = _dot(attn, do, TN) + _dot(kc * cdec, dsn)
                dattn = jnp.where(incl, _dot(do, vn, NT), 0.0)
                dos = _dot(do, s, NT)
                vds = _dot(vn, dsn, NT)
                dar = dattn * rel
                dq = _dot(dar, kc) + e * dos
                dk = _dot(dar, qc, TN) + cdec * vds
                dgc = rsum(qc * e * dos)
                dc = cdec[:, 0:1] * rsum(kc * vds)
                dgc = dgc - dc
                dglast = jnp.sum(dc, axis=0, keepdims=True) + el[:, 0:1] * jnp.sum(rsum(dsn * s), axis=0, keepdims=True)
                ds_ref[hh] = dsn * el + _dot(qc * e, do, TN) - _dot(w, dvn, TN)
                dw = -_dot(dvn, s, NT)
                drw = _dot(x, dw, TN)
                dru = _dot(x, dvn, TN)
                da = -jnp.where(strict, _dot(drw, w, NT) + _dot(dru, u, NT), 0.0)
                dar2 = da * rel
                dkb = _dot(dar2, kc)
                rwk = rsum(drw * kc)
                dk = dk + _dot(dar2, kb, TN) + bt * dkb + (bt * e) * drw
                dbeta = rsum(dkb * kc) + e[:, 0:1] * rwk + rsum(dru * vc)
                dgc = dgc + bt[:, 0:1] * e[:, 0:1] * rwk
                z = dattn * attn + da * a
                dgc = dgc + rsum(z) - _dotx(z, ones, TN)[:, 0:1]
                dgcb = jnp.broadcast_to(dgc, (CH, DH)) + jnp.where(rowid == CH - 1, jnp.broadcast_to(dglast, (CH, DH)), 0.0)
                dq_ref[rows, cols] = dq
                dk_ref[rows, cols] = dk
                dv_ref[rows, cols] = bt * dru
                dg_ref[hh, rows, :] = _rcumsum(dgcb)
                db_ref[hh, rows, :] = jnp.broadcast_to(dbeta, (CH, DH))
            return carry

        lax.fori_loop(0, cpt, chunk, 0)

    col, full, real, state, scal = _head_specs(lambda i: nt - 1 - i)
    one = pl.BlockSpec((1, DH), lambda h, i: (0, 0))
    return _call(body, name="gdn_bwd", grid=(NHB, nt),
                 in_specs=[full, full, full, scal, scal, col(7), one, full, state(cpt), real],
                 out_specs=[full, full, full, scal, scal, full, one],
                 out_shape=[_sds((tp, HW), F32)] * 3 + [_sds((NH, tp, DH), F32)] * 2 + [_sds((tp, HW), BF16), _sds((1, DH), F32)],
                 scratch=[pltpu.VMEM((HB, DH, DH), F32), pltpu.VMEM((RT, HB * DH), F32)],
                 sem=("arbitrary", "arbitrary"))(q, k, v, gb, bb, proj, nw, o_raw, states, dog)


MAIN_W = 8 * HW
AB_W = 2 * NH


def _split_w_in(w_in):
    main = jnp.concatenate([w_in[:, :MAIN_W], w_in[:, MAIN_W + AB_W:]], axis=1)
    ab = jnp.pad(w_in[:, MAIN_W:MAIN_W + AB_W], ((0, 0), (0, DH - AB_W)))
    return main, ab


def _pad_lanes(v):
    return jnp.pad(v, ((0, 0), (0, DH - v.shape[1])))


def _local_step(x, tgt, meta, lb_logits, mix_w, w_in, hg_nw, conv_w, a_log, dt_bias, gd_nw, w_a, w_b, w_out,
                ffn_nw, w_ffn_in, w_ffn_out, final_w):
    w_main, w_ab = _split_w_in(w_in)
    alog, dtb = _pad_lanes(a_log), _pad_lanes(dt_bias)
    final_w = final_w.reshape(1, -1)
    xn = _rms1_fwd(x, meta, mix_w)
    proj = _mm(xn, w_main, "nn", F32, 768, 1024, 2048, "proj_main", n_outer=True)
    pab = _mm(xn, w_ab, "nn", F32, 768, 128, 2048, "proj_ab")
    oa_g, oa_raw, st_a = _hgrn2_fwd(proj, lb_logits, hg_nw)
    q, k, v, gb, bb = _gdn_prep_fwd(proj, pab, conv_w, alog, dtb)
    ob_g, ob_raw, st_b = _gdn_fwd(q, k, v, gb, bb, proj, gd_nw)
    za = _mm(oa_g, w_a, "nn", F32, 1024, 512, 1024, "branch_a", n_outer=True)
    zb = _mm(ob_g, w_b, "nn", F32, 1024, 512, 1024, "branch_b", n_outer=True)
    merged = _merge_fwd(proj, za, zb)
    mix = _mm(merged, w_out, "nn", F32, 1024, 2048, 2048, "mix_out")
    h1, n2 = _resid_norm_fwd(x, mix, ffn_nw)
    gu = _mm(n2, w_ffn_in, "nn", F32, 1024, 1408, 2048, "ffn_in", n_outer=True)
    act = _swiglu_fwd(gu)
    f = _mm(act, w_ffn_out, "nn", F32, 1024, 2048, 512, "ffn_out")
    lt, dh2, dh2b, dfinal = _loss_head(h1, f, final_w, tgt)
    loss = _sum_tiles(lt)
    dact = _mm(dh2b, w_ffn_out, "nt", F32, 1024, 512, 2048, "d_act", n_outer=True)
    dw_ffn_out = _mm(act, dh2b, "tn", F32, 512, 2048, 1024, "dw_ffn_out")
    dgu = _swiglu_bwd(gu, dact)
    dn2 = _mm(dgu, w_ffn_in, "nt", F32, 1024, 2048, 1408, "d_n2")
    dw_ffn_in = _mm(n2, dgu, "tn", F32, 1024, 1408, 1024, "dw_ffn_in", out_shards=4)
    dh1, dh1b, dffn_nw = _resid_norm_bwd(h1, ffn_nw, dn2, dh2)
    dmerged = _mm(dh1b, w_out, "nt", F32, 1024, 2048, 2048, "d_merged")
    dw_out = _mm(merged, dh1b, "tn", F32, 2048, 1024, 1024, "dw_out")
    dza, dzb, dgate = _merge_bwd(proj, za, zb, dmerged)
    doa = _mm(dza, w_a, "nt", F32, 1024, 1024, 512, "d_oa")
    dob = _mm(dzb, w_b, "nt", F32, 1024, 1024, 512, "d_ob")
    dw_a = _mm(oa_g, dza, "tn", F32, 1024, 512, 1024, "dw_branch_a", out_shards=4)
    dw_b = _mm(ob_g, dzb, "tn", F32, 1024, 512, 1024, "dw_branch_b", out_shards=4)
    dhq, dhf, dhi, dhg, dlbl, dhg_nw = _hgrn2_bwd(proj, lb_logits, hg_nw, oa_raw, st_a, doa)
    dq, dk, dv, dg, dbeta, dz, dgd_nw = _gdn_bwd(q, k, v, gb, bb, proj, gd_nw, ob_raw, st_b, dob)
    dx3, dab, dconv, dalog, ddtb = _gdn_prep_bwd(proj, pab, conv_w, alog, dtb, dq, dk, dv, dg, dbeta)
    dproj = jnp.concatenate([dhq, dhf, dhi, dhg, dx3, dz, dgate], axis=1)
    dxn = _mm(dproj, w_main, "nt", F32, 768, 2048, 1024, "d_xn")
    dxn = _mm(dab, w_ab, "nt", F32, 768, 2048, 128, "d_xn_ab", add=dxn)
    dw_main = _mm(xn, dproj, "tn", F32, 2048, 1024, 768, "dw_in_main")
    dw_ab = _mm(xn, dab, "tn", F32, 2048, 128, 768, "dw_in_ab")
    dx, dmeta, dmix_w = _rms1_bwd(x, meta, mix_w, dxn, dh1)
    d_model = dw_main.shape[0]
    dw_in = jnp.concatenate([dw_main[:, :MAIN_W], dw_ab[:, :AB_W], dw_main[:, MAIN_W:]], axis=1)
    dw_in = dw_in.reshape(d_model, 4, -1).transpose(1, 0, 2)
    rows4 = lambda t: t.reshape(4, t.shape[0] // 4, t.shape[1])
    grads = dict(meta_tokens=dmeta, lb_logits=dlbl, mix_norm_w=dmix_w, w_in=dw_in,
                 hg_norm_w=dhg_nw, gd_conv_w=dconv[:CONV_K], gd_a_log=dalog[:, :NH],
                 gd_dt_bias=ddtb[:, :NH], gd_norm_w=dgd_nw, w_branch_a=dw_a, w_branch_b=dw_b,
                 w_out=rows4(dw_out), ffn_norm_w=dffn_nw, w_ffn_in=dw_ffn_in, w_ffn_out=rows4(dw_ffn_out),
                 final_norm_w=dfinal.reshape(-1))
    return loss, dx, grads


def _adamw(g, w, m, v, name):
    rows, cols = g.shape
    tr = rows
    for cand in (128, 64, 32, 16, 8):
        if rows % cand == 0 and rows > cand:
            tr = cand
            break

    def body(g_ref, w_ref, m_ref, v_ref, d_ref, nm_ref, nv_ref):
        gg = g_ref[...]
        nm = ADAM_B1 * m_ref[...] + (1.0 - ADAM_B1) * gg
        nv = ADAM_B2 * v_ref[...] + (1.0 - ADAM_B2) * (gg * gg)
        m_hat = nm / (1.0 - ADAM_B1 ** ADAM_STEP)
        v_hat = nv / (1.0 - ADAM_B2 ** ADAM_STEP)
        d_ref[...] = -ADAM_LR * (m_hat / (jnp.sqrt(v_hat) + ADAM_EPS) + ADAM_WD * w_ref[...])
        nm_ref[...] = nm
        nv_ref[...] = nv

    bs = pl.BlockSpec((tr, cols), lambda i: (i, 0))
    return _call(body, name=name, grid=(rows // tr,), in_specs=[bs] * 4, out_specs=[bs] * 3,
                 out_shape=[_sds((rows, cols), F32)] * 3, sem=("parallel",))(g, w, m, v)


HBM = pl.BlockSpec(memory_space=pltpu.HBM)
MESH = pl.DeviceIdType.MESH


def _place():
    x, y, c = lax.axis_index("x"), lax.axis_index("y"), lax.axis_index("c")
    return x, y, c, [(1 - x, y), (x, 1 - y), (1 - x, 1 - y)]


def _comm_call(body, name, out_shape, n_in, scratch):
    return pl.pallas_call(body, name=name, out_shape=out_shape, in_specs=[HBM] * n_in,
                          out_specs=jax.tree.map(lambda _: HBM, out_shape), scratch_shapes=scratch)


def _half_rows(rows, c, tile):
    hh = rows // 2
    assert rows % 2 == 0 and hh % tile == 0, (rows, tile)
    return pl.ds(pl.multiple_of(c * hh, tile), hh)


def _gather_weights(shards):
    n = len(shards)

    def body(*refs):
        w_refs, out_refs = refs[:n], refs[n:2 * n]
        send_sems, recv_sems, local_sems = refs[2 * n:]
        x, y, c, chips = _place()
        s_me = 2 * x + y

        def copy(k, j, src, dst, to):
            return pltpu.make_async_remote_copy(src_ref=src, dst_ref=dst, send_sem=send_sems.at[6 * k + j],
                                                recv_sem=recv_sems.at[6 * k + j], device_id=to, device_id_type=MESH)

        mine, first, passed = [], [], []
        for k in range(n):
            half = _half_rows(w_refs[k].shape[0], c, 16)
            mine.append(pltpu.make_async_copy(w_refs[k], out_refs[k].at[s_me], local_sems.at[k]))
            mine[-1].start()
            for j, (cx, cy) in enumerate(chips):
                first.append(copy(k, j, w_refs[k].at[half], out_refs[k].at[s_me, half], (cx, cy, c)))
                first[-1].start()
        for j, (cx, cy) in enumerate(chips):
            for k in range(n):
                half = _half_rows(w_refs[k].shape[0], c, 16)
                landed = out_refs[k].at[2 * cx + cy, half]
                copy(k, j, w_refs[k].at[half], landed, (cx, cy, c)).wait_recv()
                passed.append(copy(k, 3 + j, landed, landed, (x, y, 1 - c)))
                passed[-1].start()
        for j, (cx, cy) in enumerate(chips):
            for k in range(n):
                other = _half_rows(w_refs[k].shape[0], 1 - c, 16)
                copy(k, 3 + j, w_refs[k].at[other], out_refs[k].at[2 * cx + cy, other], (x, y, 1 - c)).wait_recv()
        for cp in first + passed:
            cp.wait_send()
        for cp in mine:
            cp.wait()

    return _comm_call(body, "gather_weights", [_sds((4,) + w.shape, w.dtype) for w in shards], n,
                      [pltpu.SemaphoreType.DMA((6 * n,)), pltpu.SemaphoreType.DMA((6 * n,)), pltpu.SemaphoreType.DMA((n,))])(*shards)


def _swap_halves(gs):
    n = len(gs)

    def body(*refs):
        g_refs, out_refs = refs[:n], refs[n:2 * n]
        send_sems, recv_sems = refs[2 * n:]
        x, y, c, _ = _place()
        cps = []
        for k in range(n):
            other = _half_rows(g_refs[k].shape[1], 1 - c, 8)
            cps.append(pltpu.make_async_remote_copy(src_ref=g_refs[k].at[:, other, :], dst_ref=out_refs[k], send_sem=send_sems.at[k],
                                                    recv_sem=recv_sems.at[k], device_id=(x, y, 1 - c), device_id_type=MESH))
            cps[-1].start()
        for cp in cps:
            cp.wait()

    return _comm_call(body, "swap_halves", [_sds((4, g.shape[1] // 2, g.shape[2]), g.dtype) for g in gs], n,
                      [pltpu.SemaphoreType.DMA((n,)), pltpu.SemaphoreType.DMA((n,))])(*gs)


def _row_tile(rows, row_bytes, budget=3 << 20):
    if rows * row_bytes <= budget:
        return rows
    return max(t for t in range(8, rows, 8) if rows % t == 0 and t * row_bytes <= budget)


def _add_half(g, got, c, name):
    _, rows, cols = g.shape
    hh = rows // 2
    tr = _row_tile(hh, cols * 4)
    nb = hh // tr

    def body(c_ref, a_ref, b_ref, o_ref):
        o_ref[...] = a_ref[...] + b_ref[...]

    gs = pltpu.PrefetchScalarGridSpec(
        num_scalar_prefetch=1, grid=(4, nb),
        in_specs=[pl.BlockSpec((1, tr, cols), lambda s, i, c_ref: (s, c_ref[0] * nb + i, 0)),
                  pl.BlockSpec((1, tr, cols), lambda s, i, c_ref: (s, i, 0))],
        out_specs=pl.BlockSpec((1, tr, cols), lambda s, i, c_ref: (s, i, 0)))
    return pl.pallas_call(body, name=name, grid_spec=gs, out_shape=_sds((4, hh, cols), F32),
                          compiler_params=pltpu.CompilerParams(dimension_semantics=("parallel", "parallel"),
                                                               vmem_limit_bytes=VMEM_LIMIT))(c, g, got)


def _scatter_chips(ps):
    n = len(ps)

    def body(*refs):
        p_refs, out_refs = refs[:n], refs[n:2 * n]
        send_sems, recv_sems, local_sems = refs[2 * n:]
        x, y, c, chips = _place()
        s_me = 2 * x + y
        mine, cps = [], []
        for k in range(n):
            mine.append(pltpu.make_async_copy(p_refs[k].at[s_me], out_refs[k].at[s_me], local_sems.at[k]))
            mine[-1].start()
            for j, (cx, cy) in enumerate(chips):
                cps.append(pltpu.make_async_remote_copy(src_ref=p_refs[k].at[2 * cx + cy], dst_ref=out_refs[k].at[s_me],
                                                        send_sem=send_sems.at[3 * k + j], recv_sem=recv_sems.at[3 * k + j],
                                                        device_id=(cx, cy, c), device_id_type=MESH))
                cps[-1].start()
        for cp in cps:
            cp.wait()
        for cp in mine:
            cp.wait()

    return _comm_call(body, "scatter_chips", [_sds(p_.shape, p_.dtype) for p_ in ps], n,
                      [pltpu.SemaphoreType.DMA((3 * n,)), pltpu.SemaphoreType.DMA((3 * n,)), pltpu.SemaphoreType.DMA((n,))])(*ps)


def _sum_slabs(b, name):
    n, h, wd = b.shape
    tr = _row_tile(h, n * wd * 4, 6 << 20)

    def body(b_ref, o_ref):
        acc = b_ref[0]
        for s in range(1, n):
            acc = acc + b_ref[s]
        o_ref[...] = acc

    return _call(body, name=name, grid=(h // tr,), in_specs=[pl.BlockSpec((n, tr, wd), lambda i: (0, i, 0))],
                 out_specs=pl.BlockSpec((tr, wd), lambda i: (i, 0)), out_shape=_sds((h, wd), F32), sem=("parallel",))(b)


def _share_halves(ghs):
    n = len(ghs)

    def body(*refs):
        g_refs, out_refs = refs[:n], refs[n:2 * n]
        send_sems, recv_sems, local_sems = refs[2 * n:]
        x, y, c, _ = _place()
        mine, cps = [], []
        for k in range(n):
            half = _half_rows(out_refs[k].shape[0], c, 8)
            mine.append(pltpu.make_async_copy(g_refs[k], out_refs[k].at[half], local_sems.at[k]))
            mine[-1].start()
            cps.append(pltpu.make_async_remote_copy(src_ref=g_refs[k], dst_ref=out_refs[k].at[half], send_sem=send_sems.at[k],
                                                    recv_sem=recv_sems.at[k], device_id=(x, y, 1 - c), device_id_type=MESH))
            cps[-1].start()
        for cp in cps:
            cp.wait()
        for cp in mine:
            cp.wait()

    return _comm_call(body, "share_halves", [_sds((2 * g.shape[0], g.shape[1]), g.dtype) for g in ghs], n,
                      [pltpu.SemaphoreType.DMA((n,)), pltpu.SemaphoreType.DMA((n,)), pltpu.SemaphoreType.DMA((n,))])(*ghs)


def _gather_all(v, name):
    def body(v_ref, out_ref, send_sems, recv_sems, local_sem):
        x, y, c = lax.axis_index("x"), lax.axis_index("y"), lax.axis_index("c")
        me = 4 * x + 2 * y + c
        flip = lambda t, d: 1 - t if d else t
        mine = pltpu.make_async_copy(v_ref, out_ref.at[me], local_sem)
        mine.start()
        cps = []
        for k in range(1, 8):
            to = (flip(x, k & 4), flip(y, k & 2), flip(c, k & 1))
            cps.append(pltpu.make_async_remote_copy(src_ref=v_ref, dst_ref=out_ref.at[me], send_sem=send_sems.at[k - 1],
                                                    recv_sem=recv_sems.at[k - 1], device_id=to, device_id_type=MESH))
        for cp in cps:
            cp.start()
        for cp in cps:
            cp.wait()
        mine.wait()

    return _comm_call(body, name, _sds((8,) + v.shape, v.dtype), 1,
                      [pltpu.SemaphoreType.DMA((7,)), pltpu.SemaphoreType.DMA((7,)), pltpu.SemaphoreType.DMA])(v)


BIG = (("w_in", 1), ("w_branch_a", 1), ("w_branch_b", 1), ("w_out", 0), ("w_ffn_in", 1), ("w_ffn_out", 0))
SMALL = ("meta_tokens", "lb_logits", "mix_norm_w", "hg_norm_w", "gd_conv_w", "gd_a_log", "gd_dt_bias", "gd_norm_w",
         "ffn_norm_w", "final_norm_w")


def _pack_lanes(parts):
    rows = []
    for p in parts:
        f = p.reshape(-1).astype(F32)
        n = -(-f.shape[0] // DH) * DH
        rows.append(jnp.pad(f, (0, n - f.shape[0])).reshape(-1, DH))
    buf = jnp.concatenate(rows, axis=0)
    return jnp.pad(buf, ((0, -buf.shape[0] % 8), (0, 0)))


def _unpack_lanes(buf, shapes):
    out, off = [], 0
    for shp in shapes:
        n = math.prod(shp)
        r = -(-n // DH)
        out.append(buf[off:off + r].reshape(-1)[:n].reshape(shp))
        off += r
    return out


def kernel(x, meta_tokens, lb_logits, mix_norm_w, w_in, hg_norm_w, gd_conv_w, gd_a_log, gd_dt_bias, gd_norm_w, w_branch_a, w_branch_b, w_out, ffn_norm_w, w_ffn_in, w_ffn_out, final_norm_w, loss_target, m_meta_tokens, m_lb_logits, m_mix_norm_w, m_w_in, m_hg_norm_w, m_gd_conv_w, m_gd_a_log, m_gd_dt_bias, m_gd_norm_w, m_w_branch_a, m_w_branch_b, m_w_out, m_ffn_norm_w, m_w_ffn_in, m_w_ffn_out, m_final_norm_w, v_meta_tokens, v_lb_logits, v_mix_norm_w, v_w_in, v_hg_norm_w, v_gd_conv_w, v_gd_a_log, v_gd_dt_bias, v_gd_norm_w, v_w_branch_a, v_w_branch_b, v_w_out, v_ffn_norm_w, v_w_ffn_in, v_w_ffn_out, v_final_norm_w):
    args = dict(locals())
    big = [n for n, _ in BIG]
    w = {n: args[n] for n in SMALL + tuple(big)}
    m = {n: args["m_" + n] for n in w}
    v = {n: args["v_" + n] for n in w}
    xi, yi, ci = lax.axis_index("x"), lax.axis_index("y"), lax.axis_index("c")
    shard = 2 * xi + yi
    big_local = {n: w[n][0] for n in big}

    meta_cols, conv_cols = meta_tokens.shape[1], gd_conv_w.shape[-1]
    sm_all = _gather_all(_pack_lanes([meta_tokens, gd_conv_w[0]]), "gather_meta")
    sm_parts = [_unpack_lanes(sm_all[2 * s], [meta_tokens.shape, gd_conv_w[0].shape]) for s in range(4)]
    meta_full = jnp.concatenate([p[0] for p in sm_parts], axis=1)
    conv_full = jnp.concatenate([p[1] for p in sm_parts], axis=1)
    wg = dict(zip(big, _gather_weights([big_local[n].astype(BF16) for n in big])))
    d_model = wg["w_in"].shape[1]
    w_in_full = wg["w_in"].transpose(1, 0, 2).reshape(d_model, -1)
    rows_full = lambda t: t.reshape(t.shape[0] * t.shape[1], t.shape[2])

    loss, dx, g = _local_step(x[0], loss_target[0], meta_full, lb_logits, mix_norm_w, w_in_full, hg_norm_w, conv_full,
                              gd_a_log, gd_dt_bias, gd_norm_w, wg["w_branch_a"], wg["w_branch_b"], rows_full(wg["w_out"]),
                              ffn_norm_w, wg["w_ffn_in"], rows_full(wg["w_ffn_out"]), final_norm_w)
    loss = lax.psum(loss[0, 0], ("x", "y", "c"))

    gs = [g[n] for n in big]
    cvec = ci.reshape(1).astype(jnp.int32)
    parts = [_add_half(gk, got, cvec, "add_half_" + n) for n, gk, got in zip(big, gs, _swap_halves(gs))]
    halves = [_sum_slabs(b, "sum_chips_" + n) for n, b in zip(big, _scatter_chips(parts))]
    g_big = dict(zip(big, _share_halves(halves)))

    small_shapes = [g[n].shape for n in SMALL]
    g_all = _gather_all(_pack_lanes([g[n] for n in SMALL]), "gather_small")
    g_small = dict(zip(SMALL, _unpack_lanes(_sum_slabs(g_all, "sum_small"), small_shapes)))
    g_small["meta_tokens"] = lax.dynamic_slice_in_dim(g_small["meta_tokens"], shard * meta_cols, meta_cols, axis=1)
    g_small["gd_conv_w"] = lax.dynamic_slice_in_dim(g_small["gd_conv_w"], shard * conv_cols, conv_cols, axis=1)

    grad, delta, new_m, new_v = {}, {}, {}, {}
    for n in big:
        grad[n] = g_big[n].reshape(w[n].shape)
        d_, m_, v_ = _adamw(g_big[n], big_local[n], m[n][0], v[n][0], "adamw_" + n)
        delta[n], new_m[n], new_v[n] = d_.reshape(w[n].shape), m_.reshape(w[n].shape), v_.reshape(w[n].shape)
    local_shapes = [w[n].shape for n in SMALL]
    d_, m_, v_ = _adamw(_pack_lanes([g_small[n] for n in SMALL]), _pack_lanes([w[n] for n in SMALL]),
                        _pack_lanes([m[n] for n in SMALL]), _pack_lanes([v[n] for n in SMALL]), "adamw_small")
    for n, gs_, dd, mm, vv in zip(SMALL, [g_small[n] for n in SMALL], _unpack_lanes(d_, local_shapes), _unpack_lanes(m_, local_shapes),
                                  _unpack_lanes(v_, local_shapes)):
        grad[n], delta[n], new_m[n], new_v[n] = gs_.reshape(w[n].shape), dd, mm, vv

    order = ["meta_tokens", "lb_logits", "mix_norm_w", "w_in", "hg_norm_w", "gd_conv_w", "gd_a_log", "gd_dt_bias", "gd_norm_w",
             "w_branch_a", "w_branch_b", "w_out", "ffn_norm_w", "w_ffn_in", "w_ffn_out", "final_norm_w"]
    return (loss, dx[None], *[grad[n] for n in order], *[delta[n] for n in order], *[new_m[n] for n in order],
            *[new_v[n] for n in order])
```

```python
import functools
import math

import jax
import jax.numpy as jnp
from jax import lax
from jax.experimental import pallas as pl
from jax.experimental.pallas import tpu as pltpu

F32, BF16 = jnp.float32, jnp.bfloat16
HI = lax.Precision.HIGHEST
EPS = 1e-6
D_MODEL = 2048
N_META = 16
FRONT = 256
CH = 64
SUB = 16
DH = 128
NH = 8
HW = NH * DH
CONV_K = 4
RT = 256
VMEM_LIMIT = 56 * 1024 * 1024
ADAM_LR, ADAM_B1, ADAM_B2, ADAM_EPS, ADAM_WD, ADAM_STEP = 0.001, 0.9, 0.999, 1e-08, 0.01, 10

NN = (((1,), (0,)), ((), ()))
NT = (((1,), (1,)), ((), ()))
TN = (((0,), (0,)), ((), ()))


def _dot(a, b, dn=NN):
    return lax.dot_general(a.astype(BF16), b.astype(BF16), dn, preferred_element_type=F32)


def _dotx(a, b, dn=NN):
    return lax.dot_general(a, b, dn, precision=HI, preferred_element_type=F32)


class _Side:
    def __init__(self, inputs, out_shapes, scratch, start, wait):
        self.inputs, self.out_shapes, self.scratch, self.start, self.wait = inputs, out_shapes, scratch, start, wait


def _call(body, *, name, grid, in_specs, out_specs, out_shape, scratch=(), sem=None, side=None):
    params = pltpu.CompilerParams(dimension_semantics=sem, vmem_limit_bytes=VMEM_LIMIT)
    if side is None:
        return pl.pallas_call(body, name=name, grid=grid, in_specs=in_specs, out_specs=out_specs, out_shape=out_shape,
                              scratch_shapes=list(scratch), compiler_params=params)
    single = not isinstance(out_specs, (list, tuple))
    out_specs, out_shape = ([out_specs], [out_shape]) if single else (list(out_specs), list(out_shape))
    ni, no, ns = len(in_specs), len(out_specs), len(scratch)
    nsi, nso = len(side.inputs), len(side.out_shapes)
    hbm = pl.BlockSpec(memory_space=pltpu.HBM)

    def wrapped(*refs):
        main_in, side_in = refs[:ni], refs[ni:ni + nsi]
        main_out, side_out = refs[ni + nsi:ni + nsi + no], refs[ni + nsi + no:ni + nsi + no + nso]
        main_scr, side_scr = refs[ni + nsi + no + nso:ni + nsi + no + nso + ns], refs[ni + nsi + no + nso + ns:]
        pids = [pl.program_id(d) for d in range(len(grid))]
        first = functools.reduce(lambda a, b: a & b, [p == 0 for p in pids])
        last = functools.reduce(lambda a, b: a & b, [p == g - 1 for p, g in zip(pids, grid)])

        @pl.when(first)
        def _():
            side.start(side_in, side_out, side_scr)

        body(*main_in, *main_out, *main_scr)

        @pl.when(last)
        def _():
            side.wait(side_in, side_out, side_scr)

    call = pl.pallas_call(wrapped, name=name, grid=grid, in_specs=list(in_specs) + [hbm] * nsi,
                          out_specs=out_specs + [hbm] * nso, out_shape=out_shape + list(side.out_shapes),
                          scratch_shapes=list(scratch) + list(side.scratch), compiler_params=params)

    def run(*args):
        outs = call(*args, *side.inputs)
        main = outs[0] if single else list(outs[:no])
        return main, list(outs[no:])

    return run


def _divmod(j, per):
    if per == 1:
        return j, 0
    return lax.div(j, jnp.int32(per)), lax.rem(j, jnp.int32(per))


def _sds(shape, dtype):
    return jax.ShapeDtypeStruct(tuple(shape), dtype)


def _sigmoid(x):
    return 1.0 / (1.0 + jnp.exp(-x))


def _silu(x):
    return x * _sigmoid(x)


def _dsilu(x):
    s = _sigmoid(x)
    return s * (1.0 + x * (1.0 - s))


def _tri(n, kind):
    r = lax.broadcasted_iota(jnp.int32, (n, n), 0)
    c = lax.broadcasted_iota(jnp.int32, (n, n), 1)
    return {"incl": r >= c, "strict": r > c, "upper": c >= r}[kind]


def _mm(a, b, mode, out_dtype, tm, tn, tk, name, add=None, n_outer=False, out_shards=0, side=None):
    sharded_b = b.ndim == 3
    if sharded_b:
        S, R, n = b.shape
        b_rows, b_cols = R, S * n
    else:
        b_rows, b_cols = b.shape
    if mode == "nn":
        (M, K), N, dn = a.shape, b_cols, NN
    elif mode == "nt":
        (M, K), N, dn = a.shape, b_rows, NT
    else:
        (K, M), N, dn = a.shape, b_cols, TN
    tm, tn, tk = min(tm, M), min(tn, N), min(tk, K)
    if sharded_b:
        tn, tk = (min(tn, n), tk) if mode != "nt" else (tn, min(tk, n))
    if out_shards:
        tn = min(tn, N // out_shards)
    assert M % tm == 0 and N % tn == 0 and K % tk == 0, (name, M, N, K, tm, tn, tk)
    nk = K // tk
    a_blk, a_idx = ((tm, tk), lambda i, j, k: (i, k)) if mode != "tn" else ((tk, tm), lambda i, j, k: (k, i))
    if not sharded_b:
        b_blk, b_idx = ((tk, tn), lambda i, j, k: (k, j)) if mode != "nt" else ((tn, tk), lambda i, j, k: (j, k))
    elif mode != "nt":
        per = n // tn
        assert n % tn == 0
        b_blk, b_idx = (None, tk, tn), lambda i, j, k: (_divmod(j, per)[0], k, _divmod(j, per)[1])
    else:
        per = n // tk
        assert n % tk == 0
        b_blk, b_idx = (None, tn, tk), lambda i, j, k: (_divmod(k, per)[0], j, _divmod(k, per)[1])
    if out_shards:
        per_o = N // out_shards // tn
        assert (N // out_shards) % tn == 0
        o_blk, o_idx = (None, tm, tn), lambda i, j, k: (_divmod(j, per_o)[0], i, _divmod(j, per_o)[1])
        o_shape = (out_shards, M, N // out_shards)
    else:
        o_blk, o_idx, o_shape = (tm, tn), (lambda i, j, k: (i, j)), (M, N)
    c_idx = lambda i, j, k: (i, j)
    if n_outer:
        sw = lambda f: (lambda j, i, k: f(i, j, k))
        a_idx, b_idx, o_idx, c_idx = sw(a_idx), sw(b_idx), sw(o_idx), sw(c_idx)
        grid = (N // tn, M // tm, nk)
    else:
        grid = (M // tm, N // tn, nk)
    has_add = add is not None

    def body(*refs):
        if has_add:
            a_ref, b_ref, c_ref, o_ref, acc_ref = refs
        else:
            a_ref, b_ref, o_ref, acc_ref = refs
            c_ref = None
        part = lax.dot_general(a_ref[...].astype(BF16), b_ref[...].astype(BF16), dn, preferred_element_type=F32)

        def fin(val):
            if has_add:
                val = val + c_ref[...]
            o_ref[...] = val.astype(out_dtype)

        if nk == 1:
            fin(part)
        else:
            k = pl.program_id(2)

            @pl.when(k == 0)
            def _():
                acc_ref[...] = part

            @pl.when(k > 0)
            def _():
                acc_ref[...] += part

            @pl.when(k == nk - 1)
            def _():
                fin(acc_ref[...])

    in_specs = [pl.BlockSpec(a_blk, a_idx), pl.BlockSpec(b_blk, b_idx)]
    args = [a, b]
    if has_add:
        in_specs.append(pl.BlockSpec((tm, tn), c_idx))
        args.append(add)
    acc_shape = (tm, tn) if nk > 1 else (8, 128)
    return _call(body, name=name, grid=grid, in_specs=in_specs, out_specs=pl.BlockSpec(o_blk, o_idx),
                 out_shape=_sds(o_shape, out_dtype), scratch=[pltpu.VMEM(acc_shape, F32)],
                 sem=("arbitrary",) * 3 if side is not None else ("parallel", "parallel", "arbitrary"), side=side)(*args)


def _rms1_fwd(x, meta, w):
    seq, d = x.shape
    nt = (FRONT + seq) // RT

    def body(x_ref, m_ref, w_ref, o_ref):
        i = pl.program_id(0)

        def norm(v):
            r = lax.rsqrt(jnp.mean(v * v, axis=-1, keepdims=True) + EPS)
            return (v * r * w_ref[...]).astype(BF16)

        @pl.when(i == 0)
        def _():
            o_ref[0:RT - N_META, :] = jnp.zeros((RT - N_META, d), BF16)
            o_ref[RT - N_META:RT, :] = norm(m_ref[...])

        @pl.when(i > 0)
        def _():
            o_ref[...] = norm(x_ref[...])

    return _call(body, name="rms1_fwd", grid=(nt,),
                 in_specs=[pl.BlockSpec((RT, d), lambda i: (jnp.maximum(i - 1, 0), 0)),
                           pl.BlockSpec((N_META, d), lambda i: (0, 0)),
                           pl.BlockSpec((1, d), lambda i: (0, 0))],
                 out_specs=pl.BlockSpec((RT, d), lambda i: (i, 0)),
                 out_shape=_sds((FRONT + seq, d), BF16), sem=("parallel",))(x, meta, w)


def _rms1_bwd(x, meta, w, dxn, dh1):
    seq, d = x.shape
    nt = (FRONT + seq) // RT

    def body(x_ref, m_ref, w_ref, g_ref, r_ref, dx_ref, dm_ref, dw_ref):
        i = pl.program_id(0)

        def bwd(v, g):
            r = lax.rsqrt(jnp.mean(v * v, axis=-1, keepdims=True) + EPS)
            vh = v * r
            gh = g * w_ref[...]
            return r * (gh - vh * jnp.mean(gh * vh, axis=-1, keepdims=True)), jnp.sum(g * vh, axis=0, keepdims=True)

        @pl.when(i == 0)
        def _():
            dm, dw = bwd(m_ref[...], g_ref[RT - N_META:RT, :])
            dm_ref[...] = dm
            dw_ref[...] = dw

        @pl.when(i > 0)
        def _():
            dx, dw = bwd(x_ref[...], g_ref[...])
            dx_ref[...] = dx + r_ref[...]
            dw_ref[...] += dw

    xs = pl.BlockSpec((RT, d), lambda i: (jnp.maximum(i - 1, 0), 0))
    return _call(body, name="rms1_bwd", grid=(nt,),
                 in_specs=[xs, pl.BlockSpec((N_META, d), lambda i: (0, 0)), pl.BlockSpec((1, d), lambda i: (0, 0)),
                           pl.BlockSpec((RT, d), lambda i: (i, 0)), xs],
                 out_specs=[xs, pl.BlockSpec((N_META, d), lambda i: (0, 0)), pl.BlockSpec((1, d), lambda i: (0, 0))],
                 out_shape=[_sds((seq, d), F32), _sds((N_META, d), F32), _sds((1, d), F32)],
                 sem=("arbitrary",))(x, meta, w, dxn, dh1)


def _merge_fwd(proj, za, zb):
    seq, d = za.shape
    off = FRONT // RT
    ca, cb = 8 * HW // d, 8 * HW // d + 1

    def body(ga_ref, gb_ref, za_ref, zb_ref, o_ref):
        o_ref[...] = (_sigmoid(ga_ref[...]) * za_ref[...] + _sigmoid(gb_ref[...]) * zb_ref[...]).astype(BF16)

    zs = pl.BlockSpec((RT, d), lambda i: (i, 0))
    return _call(body, name="merge_fwd", grid=(seq // RT,),
                 in_specs=[pl.BlockSpec((RT, d), lambda i: (i + off, ca)), pl.BlockSpec((RT, d), lambda i: (i + off, cb)), zs, zs],
                 out_specs=zs, out_shape=_sds((seq, d), BF16), sem=("parallel",))(proj, proj, za, zb)


def _merge_bwd(proj, za, zb, dmerged):
    seq, d = za.shape
    off = FRONT // RT
    ca, cb = 8 * HW // d, 8 * HW // d + 1
    nt = (FRONT + seq) // RT

    def body(ga_ref, gb_ref, za_ref, zb_ref, dm_ref, dza_ref, dzb_ref, dg_ref):
        i = pl.program_id(0)

        @pl.when(i < off)
        def _():
            dg_ref[...] = jnp.zeros((RT, 2 * d), BF16)

        @pl.when(i >= off)
        def _():
            sa, sb, dm = _sigmoid(ga_ref[...]), _sigmoid(gb_ref[...]), dm_ref[...]
            dza_ref[...] = (dm * sa).astype(BF16)
            dzb_ref[...] = (dm * sb).astype(BF16)
            dg_ref[:, 0:d] = (dm * za_ref[...] * sa * (1.0 - sa)).astype(BF16)
            dg_ref[:, d:2 * d] = (dm * zb_ref[...] * sb * (1.0 - sb)).astype(BF16)

    rs = pl.BlockSpec((RT, d), lambda i: (jnp.maximum(i - off, 0), 0))
    return _call(body, name="merge_bwd", grid=(nt,),
                 in_specs=[pl.BlockSpec((RT, d), lambda i: (i, ca)), pl.BlockSpec((RT, d), lambda i: (i, cb)), rs, rs, rs],
                 out_specs=[rs, rs, pl.BlockSpec((RT, 2 * d), lambda i: (i, 0))],
                 out_shape=[_sds((seq, d), BF16), _sds((seq, d), BF16), _sds((FRONT + seq, 2 * d), BF16)],
                 sem=("arbitrary",))(proj, proj, za, zb, dmerged)


def _resid_norm_fwd(x, mix, w):
    seq, d = x.shape

    def body(x_ref, m_ref, w_ref, h_ref, n_ref):
        h = x_ref[...] + m_ref[...]
        h_ref[...] = h
        r = lax.rsqrt(jnp.mean(h * h, axis=-1, keepdims=True) + EPS)
        n_ref[...] = (h * r * w_ref[...]).astype(BF16)

    rs = pl.BlockSpec((RT, d), lambda i: (i, 0))
    return _call(body, name="resid_norm_fwd", grid=(seq // RT,),
                 in_specs=[rs, rs, pl.BlockSpec((1, d), lambda i: (0, 0))], out_specs=[rs, rs],
                 out_shape=[_sds((seq, d), F32), _sds((seq, d), BF16)], sem=("parallel",))(x, mix, w)


def _resid_norm_bwd(h1, w, dn2, dh2):
    seq, d = h1.shape

    def body(h_ref, w_ref, g_ref, r_ref, o_ref, ob_ref, dw_ref):
        i = pl.program_id(0)
        h, g = h_ref[...], g_ref[...]
        r = lax.rsqrt(jnp.mean(h * h, axis=-1, keepdims=True) + EPS)
        hh = h * r
        gh = g * w_ref[...]
        dh = r_ref[...] + r * (gh - hh * jnp.mean(gh * hh, axis=-1, keepdims=True))
        o_ref[...] = dh
        ob_ref[...] = dh.astype(BF16)
        dw = jnp.sum(g * hh, axis=0, keepdims=True)

        @pl.when(i == 0)
        def _():
            dw_ref[...] = dw

        @pl.when(i > 0)
        def _():
            dw_ref[...] += dw

    rs = pl.BlockSpec((RT, d), lambda i: (i, 0))
    ws = pl.BlockSpec((1, d), lambda i: (0, 0))
    return _call(body, name="resid_norm_bwd", grid=(seq // RT,), in_specs=[rs, ws, rs, rs], out_specs=[rs, rs, ws],
                 out_shape=[_sds((seq, d), F32), _sds((seq, d), BF16), _sds((1, d), F32)], sem=("arbitrary",))(h1, w, dn2, dh2)


def _swiglu_fwd(gu):
    seq, f2 = gu.shape
    ff = f2 // 2
    tc = 512
    nb = ff // tc

    def body(g_ref, u_ref, o_ref):
        o_ref[...] = (_silu(g_ref[...]) * u_ref[...]).astype(BF16)

    return _call(body, name="swiglu_fwd", grid=(seq // RT, nb),
                 in_specs=[pl.BlockSpec((RT, tc), lambda i, j: (i, j)), pl.BlockSpec((RT, tc), lambda i, j: (i, j + nb))],
                 out_specs=pl.BlockSpec((RT, tc), lambda i, j: (i, j)), out_shape=_sds((seq, ff), BF16),
                 sem=("parallel", "parallel"))(gu, gu)


def _swiglu_bwd(gu, dact):
    seq, f2 = gu.shape
    ff = f2 // 2
    tc = 512
    nb = ff // tc

    def body(g_ref, u_ref, d_ref, dg_ref, du_ref):
        g, d = g_ref[...], d_ref[...]
        dg_ref[...] = (d * u_ref[...] * _dsilu(g)).astype(BF16)
        du_ref[...] = (d * _silu(g)).astype(BF16)

    bs = pl.BlockSpec((RT, tc), lambda i, j: (i, j))
    dgate, dup = _call(body, name="swiglu_bwd", grid=(seq // RT, nb),
                       in_specs=[bs, pl.BlockSpec((RT, tc), lambda i, j: (i, j + nb)), bs], out_specs=[bs, bs],
                       out_shape=[_sds((seq, ff), BF16), _sds((seq, ff), BF16)], sem=("parallel", "parallel"))(gu, gu, dact)
    return jnp.concatenate([dgate, dup], axis=1)


def _loss_head(h1, f, w, tgt):
    seq, d = h1.shape
    nt = seq // RT

    def body(h_ref, f_ref, w_ref, t_ref, l_ref, dh_ref, dhb_ref, dw_ref):
        i = pl.program_id(0)
        h = h_ref[...] + f_ref[...]
        r = lax.rsqrt(jnp.mean(h * h, axis=-1, keepdims=True) + EPS)
        hh = h * r
        err = hh * w_ref[...] - t_ref[...]
        l_ref[...] = jnp.full((8, 128), 0.5 * jnp.sum(jnp.mean(err * err, axis=-1, keepdims=True)), F32)
        dy = err * (1.0 / d)
        gh = dy * w_ref[...]
        dh = r * (gh - hh * jnp.mean(gh * hh, axis=-1, keepdims=True))
        dh_ref[...] = dh
        dhb_ref[...] = dh.astype(BF16)
        dw = jnp.sum(dy * hh, axis=0, keepdims=True)

        @pl.when(i == 0)
        def _():
            dw_ref[...] = dw

        @pl.when(i > 0)
        def _():
            dw_ref[...] += dw

    rs = pl.BlockSpec((RT, d), lambda i: (i, 0))
    ws = pl.BlockSpec((1, d), lambda i: (0, 0))
    return _call(body, name="loss_head", grid=(nt,), in_specs=[rs, rs, ws, rs],
                 out_specs=[pl.BlockSpec((8, 128), lambda i: (i, 0)), rs, rs, ws],
                 out_shape=[_sds((nt * 8, 128), F32), _sds((seq, d), F32), _sds((seq, d), BF16), _sds((1, d), F32)],
                 sem=("arbitrary",))(h1, f, w, tgt)


def _sum_tiles(lt):
    n = lt.shape[0]

    def body(l_ref, o_ref):
        v = l_ref[...]
        r = lax.broadcasted_iota(jnp.int32, v.shape, 0)
        c = lax.broadcasted_iota(jnp.int32, v.shape, 1)
        o_ref[...] = jnp.sum(jnp.where((r % 8 == 0) & (c == 0), v, 0.0), keepdims=True)

    return _call(body, name="loss_sum", grid=(1,), in_specs=[pl.BlockSpec((n, 128), lambda i: (0, 0))],
                 out_specs=pl.BlockSpec((1, 1), lambda i: (0, 0)), out_shape=_sds((1, 1), F32))(lt)


def _gated_norm_fwd(o, g, nw):
    r = lax.rsqrt(jnp.mean(o * o, axis=-1, keepdims=True) + EPS)
    return o * r * nw * _silu(g)


def _gated_norm_bwd(o, g, nw, dout):
    r = lax.rsqrt(jnp.mean(o * o, axis=-1, keepdims=True) + EPS)
    oh = o * r
    don = dout * _silu(g)
    dg = dout * (oh * nw) * _dsilu(g)
    dnw = jnp.sum(don * oh, axis=0, keepdims=True)
    doh = don * nw
    return r * (doh - oh * jnp.mean(doh * oh, axis=-1, keepdims=True)), dg, dnw


def _hg_gates(fs, lbl):
    l0, l1 = lbl[0:1, :], lbl[1:2, :]
    m = jnp.maximum(l0, l1)
    e0, e1 = jnp.exp(l0 - m), jnp.exp(l1 - m)
    lb = e0 / (e0 + e1)
    sig = _sigmoid(fs)
    f = lb + (1.0 - lb) * sig
    return lb, sig, f, jnp.log(f), (1.0 - lb) * _sigmoid(-fs)


def _cumsum(w):
    row = lax.broadcasted_iota(jnp.int32, w.shape, 0)
    s = 1
    while s < CH:
        w = w + jnp.where(row >= s, pltpu.roll(w, s, 0), 0.0)
        s *= 2
    return w


def _rcumsum(w):
    row = lax.broadcasted_iota(jnp.int32, w.shape, 0)
    s = 1
    while s < CH:
        w = w + jnp.where(row < CH - s, pltpu.roll(w, CH - s, 0), 0.0)
        s *= 2
    return w


def _decay_blocks(q, k, b, p_ref):
    p_ref[...] = jnp.zeros((CH, CH), F32)
    m16 = _tri(SUB, "incl")
    for I in range(CH // SUB):
        s0 = I * SUB
        bI, qI, kI = b[s0:s0 + SUB], q[s0:s0 + SUB], k[s0:s0 + SUB]
        dec = jnp.exp(jnp.minimum(bI[:, None, :] - bI[None, :, :], 0.0))
        pii = jnp.sum(qI[:, None, :] * kI[None, :, :] * dec, axis=-1)
        p_ref[s0:s0 + SUB, s0:s0 + SUB] = jnp.where(m16, pii, 0.0)
        if I > 0:
            rI = b[s0 - 1:s0]
            qs = qI * jnp.exp(bI - rI)
            ks = k[0:s0] * jnp.exp(rI - b[0:s0])
            p_ref[s0:s0 + SUB, 0:s0] = _dot(qs, ks, NT)


HB = 4
NHB = NH // HB
OFF = FRONT // RT


def _head_specs(rev=None):
    row = (lambda i: i) if rev is None else rev
    col = lambda g: pl.BlockSpec((RT, HB * DH), lambda h, i: (row(i), g * NHB + h))
    full = pl.BlockSpec((RT, HB * DH), lambda h, i: (row(i), h))
    real = pl.BlockSpec((RT, HB * DH), lambda h, i: (jnp.maximum(row(i) - OFF, 0), h))
    state = lambda cpt: pl.BlockSpec((HB, cpt, DH, DH), lambda h, i: (h, row(i), 0, 0))
    scal = pl.BlockSpec((HB, RT, DH), lambda h, i: (h, row(i), 0))
    return col, full, real, state, scal


def _hgrn2_fwd(proj, lb_logits, nw):
    tp = proj.shape[0]
    nt, cpt = tp // RT, RT // CH

    def body(q_ref, f_ref, i_ref, g_ref, lbl_ref, nw_ref, og_ref, or_ref, st_ref, s_ref, p_ref):
        @pl.when(pl.program_id(1) == 0)
        def _():
            s_ref[...] = jnp.zeros((HB, DH, DH), F32)

        def chunk(c, carry):
            rows = pl.ds(pl.multiple_of(c * CH, CH), CH)
            for hh in range(HB):
                cols = slice(hh * DH, (hh + 1) * DH)
                _, _, _, w, k = _hg_gates(f_ref[rows, cols], lbl_ref[:, cols])
                q, v = _silu(q_ref[rows, cols]), i_ref[rows, cols]
                b = _cumsum(w)
                st = s_ref[hh]
                st_ref[hh, c] = st
                _decay_blocks(q, k, b, p_ref.at[hh])
                o = _dot(q * jnp.exp(b), st, NT) + _dot(p_ref[hh], v)
                bl = b[CH - 1:CH]
                s_ref[hh] = st * jnp.exp(bl) + _dot(v, k * jnp.exp(bl - b), TN)
                or_ref[rows, cols] = o
                og_ref[rows, cols] = _gated_norm_fwd(o, g_ref[rows, cols], nw_ref[...]).astype(BF16)
            return carry

        lax.fori_loop(0, cpt, chunk, 0)

    col, full, real, state, _ = _head_specs()
    return _call(body, name="hgrn2_fwd", grid=(NHB, nt),
                 in_specs=[col(0), col(1), col(2), col(3), pl.BlockSpec((2, HB * DH), lambda h, i: (0, h)),
                           pl.BlockSpec((1, DH), lambda h, i: (0, 0))],
                 out_specs=[real, full, state(cpt)],
                 out_shape=[_sds((tp - FRONT, HW), BF16), _sds((tp, HW), F32), _sds((NH, tp // CH, DH, DH), F32)],
                 scratch=[pltpu.VMEM((HB, DH, DH), F32), pltpu.VMEM((HB, CH, CH), F32)],
                 sem=("parallel", "arbitrary"))(proj, proj, proj, proj, lb_logits, nw)


def _hgrn2_bwd(proj, lb_logits, nw, o_raw, states, dog, side=None):
    tp = proj.shape[0]
    nt, cpt = tp // RT, RT // CH

    def body(q_ref, f_ref, i_ref, g_ref, lbl_ref, nw_ref, or_ref, st_ref, dog_ref,
             dq_ref, df_ref, di_ref, dg_ref, dl_ref, dnw_ref, ds_ref, p_ref, dk_ref, dqa_ref, do_ref):
        step = pl.program_id(1)

        @pl.when(step == 0)
        def _():
            ds_ref[...] = jnp.zeros((HB, DH, DH), F32)
            dl_ref[...] = jnp.zeros((2, HB * DH), F32)

        @pl.when((step == 0) & (pl.program_id(0) == 0))
        def _():
            dnw_ref[...] = jnp.zeros((1, DH), F32)

        front = nt - 1 - step < OFF
        for hh in range(HB):
            cols = slice(hh * DH, (hh + 1) * DH)
            dog_t = jnp.where(front, 0.0, dog_ref[:, cols])
            do_t, dg_t, dnw = _gated_norm_bwd(or_ref[:, cols], g_ref[:, cols], nw_ref[...], dog_t)
            do_ref[:, cols] = do_t
            dg_ref[:, cols] = dg_t.astype(BF16)
            dnw_ref[...] += dnw
        tril = _tri(CH, "incl")
        m16 = _tri(SUB, "incl")

        def chunk(cc, carry):
            c = cpt - 1 - cc
            rows = pl.ds(pl.multiple_of(c * CH, CH), CH)
            for hh in range(HB):
                cols = slice(hh * DH, (hh + 1) * DH)
                fs = f_ref[rows, cols]
                lb, sig, f, w, k = _hg_gates(fs, lbl_ref[:, cols])
                hq = q_ref[rows, cols]
                q, v, do = _silu(hq), i_ref[rows, cols], do_ref[rows, cols]
                b = _cumsum(w)
                bl = b[CH - 1:CH]
                eb = jnp.exp(b)
                qs, kd = q * eb, k * jnp.exp(bl - b)
                st, dst = st_ref[hh, c], ds_ref[hh]
                _decay_blocks(q, k, b, p_ref.at[hh])
                dv = _dot(p_ref[hh], do, TN) + _dot(kd, dst, NT)
                dp = jnp.where(tril, _dotx(do, v, NT), 0.0)
                dqa, dka = dqa_ref.at[hh], dk_ref.at[hh]
                dqa[...] = eb * _dotx(do, st)
                dka[...] = jnp.exp(bl - b) * _dotx(v, dst)
                for I in range(CH // SUB):
                    s0 = I * SUB
                    bI, qI, kI = b[s0:s0 + SUB], q[s0:s0 + SUB], k[s0:s0 + SUB]
                    dec = jnp.exp(jnp.minimum(bI[:, None, :] - bI[None, :, :], 0.0))
                    dpii = jnp.where(m16, dp[s0:s0 + SUB, s0:s0 + SUB], 0.0)[:, :, None] * dec
                    dqa[s0:s0 + SUB, :] += jnp.sum(dpii * kI[None, :, :], axis=1)
                    dka[s0:s0 + SUB, :] += jnp.sum(dpii * qI[:, None, :], axis=0)
                    if I > 0:
                        rI = b[s0 - 1:s0]
                        eq, ek = jnp.exp(bI - rI), jnp.exp(rI - b[0:s0])
                        dpij = dp[s0:s0 + SUB, 0:s0]
                        dqa[s0:s0 + SUB, :] += eq * _dotx(dpij, k[0:s0] * ek)
                        dka[0:s0, :] += ek * _dotx(dpij, qI * eq, TN)
                dq, dk = dqa[...], dka[...]
                st_end = st * jnp.exp(bl) + _dotx(v, kd, TN)
                dw = _rcumsum(q * dq - k * dk) + jnp.sum(dst * st_end, axis=0, keepdims=True)
                ds_ref[hh] = dst * jnp.exp(bl) + _dotx(do, qs, TN)
                one_m = 1.0 - sig
                dq_ref[rows, cols] = (dq * _dsilu(hq)).astype(BF16)
                df_ref[rows, cols] = ((dw / f - dk) * (1.0 - lb) * sig * one_m).astype(BF16)
                di_ref[rows, cols] = dv.astype(BF16)
                dl_ref[0:1, cols] += jnp.sum((dw / f - dk) * one_m, axis=0, keepdims=True)
            return carry

        lax.fori_loop(0, cpt, chunk, 0)

        @pl.when(step == nt - 1)
        def _():
            lbl = lbl_ref[...]
            l0, l1 = lbl[0:1, :], lbl[1:2, :]
            m = jnp.maximum(l0, l1)
            e0, e1 = jnp.exp(l0 - m), jnp.exp(l1 - m)
            p0 = e0 / (e0 + e1)
            dl0 = dl_ref[0:1, :] * p0 * (1.0 - p0)
            dl_ref[0:1, :] = dl0
            dl_ref[1:2, :] = -dl0

    col, full, real, state, _ = _head_specs(lambda i: nt - 1 - i)
    lbs = pl.BlockSpec((2, HB * DH), lambda h, i: (0, h))
    return _call(body, name="hgrn2_bwd", grid=(NHB, nt),
                 in_specs=[col(0), col(1), col(2), col(3), lbs, pl.BlockSpec((1, DH), lambda h, i: (0, 0)), full,
                           state(cpt), real],
                 out_specs=[full, full, full, full, lbs, pl.BlockSpec((1, DH), lambda h, i: (0, 0))],
                 out_shape=[_sds((tp, HW), BF16)] * 4 + [_sds((2, HW), F32), _sds((1, DH), F32)],
                 scratch=[pltpu.VMEM((HB, DH, DH), F32), pltpu.VMEM((HB, CH, CH), F32), pltpu.VMEM((HB, CH, DH), F32),
                          pltpu.VMEM((HB, CH, DH), F32), pltpu.VMEM((RT, HB * DH), F32)],
                 sem=("arbitrary", "arbitrary"), side=side)(proj, proj, proj, proj, lb_logits, nw, o_raw, states, dog)


GQ0 = 4 * HW
CW = 3 * HW


def _gd_scalars(ab, alog, dtb):
    g = -jnp.exp(alog) * jax.nn.softplus(ab + dtb)
    return g, _sigmoid(ab)


def _conv_ext_specs(row_of):
    main = [pl.BlockSpec((RT, HW), lambda i, g=g: (row_of(i), GQ0 // HW + g)) for g in range(3)]
    prev = [pl.BlockSpec((8, HW), lambda i, g=g: (jnp.maximum(row_of(i) * (RT // 8) - 1, 0), GQ0 // HW + g)) for g in range(3)]
    return main + prev


def _conv_fill(ext_ref, xs, xps, first):
    for g in range(3):
        ext_ref[0:8, g * HW:(g + 1) * HW] = jnp.where(first, 0.0, xps[g][...])
        ext_ref[8:8 + RT, g * HW:(g + 1) * HW] = xs[g][...]


def _conv_apply(ext_ref, cw):
    y = cw[CONV_K - 1:CONV_K, :] * ext_ref[pl.ds(8, RT), :]
    for s in range(1, CONV_K):
        y += cw[CONV_K - 1 - s:CONV_K - s, :] * ext_ref[pl.ds(8 - s, RT), :]
    return y


def _gdn_prep_fwd(proj, pab, conv_w, alog, dtb):
    tp = proj.shape[0]
    nt = tp // RT

    def body(x0, x1, x2, p0, p1, p2, ab_ref, cw_ref, al_ref, dt_ref, q_ref, k_ref, v_ref, g_ref, b_ref, ext_ref):
        _conv_fill(ext_ref, (x0, x1, x2), (p0, p1, p2), pl.program_id(0) == 0)
        a = _silu(_conv_apply(ext_ref, cw_ref[...]))
        for h in range(NH):
            for part, ref, sc in ((0, q_ref, DH ** -0.5), (1, k_ref, 1.0)):
                seg = a[:, part * HW + h * DH:part * HW + (h + 1) * DH]
                ref[:, h * DH:(h + 1) * DH] = seg * (lax.rsqrt(jnp.sum(seg * seg, axis=-1, keepdims=True) + EPS) * sc)
        v_ref[...] = a[:, 2 * HW:3 * HW]
        g, beta = _gd_scalars(ab_ref[...], al_ref[...], dt_ref[...])
        for h in range(NH):
            g_ref[h] = jnp.broadcast_to(g[:, h:h + 1], (RT, DH))
            b_ref[h] = jnp.broadcast_to(beta[:, NH + h:NH + h + 1], (RT, DH))

    hs = pl.BlockSpec((RT, HW), lambda i: (i, 0))
    sc = pl.BlockSpec((NH, RT, DH), lambda i: (0, i, 0))
    one = pl.BlockSpec((1, DH), lambda i: (0, 0))
    return _call(body, name="gdn_prep_fwd", grid=(nt,),
                 in_specs=_conv_ext_specs(lambda i: i) + [pl.BlockSpec((RT, DH), lambda i: (i, 0)),
                                                           pl.BlockSpec((CONV_K, CW), lambda i: (0, 0)), one, one],
                 out_specs=[hs, hs, hs, sc, sc],
                 out_shape=[_sds((tp, HW), F32)] * 3 + [_sds((NH, tp, DH), F32)] * 2,
                 scratch=[pltpu.VMEM((RT + 8, CW), F32)], sem=("parallel",))(*([proj] * 6), pab, conv_w, alog, dtb)


def _gdn_prep_bwd(proj, pab, conv_w, alog, dtb, dq, dk, dv, dgb, dbb):
    tp = proj.shape[0]
    nt = tp // RT

    def body(x0, x1, x2, p0, p1, p2, ab_ref, cw_ref, al_ref, dt_ref, dq_ref, dk_ref, dv_ref, dg_ref, db_ref,
             dx_ref, dab_ref, dcw_ref, dal_ref, ddt_ref, ext_ref, dy_ref):
        step = pl.program_id(0)
        i = nt - 1 - step

        @pl.when(step == 0)
        def _():
            dy_ref[RT:RT + 8, :] = jnp.zeros((8, CW), F32)
            dcw_ref[...] = jnp.zeros((8, CW), F32)
            dal_ref[...] = jnp.zeros((1, DH), F32)
            ddt_ref[...] = jnp.zeros((1, DH), F32)

        _conv_fill(ext_ref, (x0, x1, x2), (p0, p1, p2), i == 0)
        cw = cw_ref[...]
        y = _conv_apply(ext_ref, cw)
        a = _silu(y)
        dsl = _dsilu(y)
        for h in range(NH):
            for part, ref, sc in ((0, dq_ref, DH ** -0.5), (1, dk_ref, 1.0)):
                lo = part * HW + h * DH
                seg = a[:, lo:lo + DH]
                r = lax.rsqrt(jnp.sum(seg * seg, axis=-1, keepdims=True) + EPS)
                xh = seg * r
                dxh = ref[:, h * DH:(h + 1) * DH] * sc
                dy_ref[0:RT, lo:lo + DH] = r * (dxh - xh * jnp.sum(dxh * xh, axis=-1, keepdims=True)) * dsl[:, lo:lo + DH]
        dy_ref[0:RT, 2 * HW:3 * HW] = dv_ref[...] * dsl[:, 2 * HW:3 * HW]
        dy = dy_ref[0:RT, :]
        dx = cw[CONV_K - 1:CONV_K, :] * dy
        dcw_ref[CONV_K - 1:CONV_K, :] += jnp.sum(dy * ext_ref[pl.ds(8, RT), :], axis=0, keepdims=True)
        for s in range(1, CONV_K):
            dx += cw[CONV_K - 1 - s:CONV_K - s, :] * dy_ref[pl.ds(s, RT), :]
            dcw_ref[CONV_K - 1 - s:CONV_K - s, :] += jnp.sum(dy * ext_ref[pl.ds(8 - s, RT), :], axis=0, keepdims=True)
        dx_ref[...] = dx.astype(BF16)
        dy_ref[RT:RT + 8, :] = dy[0:8, :]
        ab = ab_ref[...]
        g, beta = _gd_scalars(ab, al_ref[...], dt_ref[...])
        lane = lax.broadcasted_iota(jnp.int32, (RT, DH), 1)
        dgl = jnp.zeros((RT, DH), F32)
        dbl = jnp.zeros((RT, DH), F32)
        for h in range(NH):
            dgl = jnp.where(lane == h, dg_ref[h], dgl)
            dbl = jnp.where(lane == NH + h, db_ref[h], dbl)
        dsp = dgl * (-jnp.exp(al_ref[...])) * _sigmoid(ab + dt_ref[...])
        dab_ref[...] = (dsp + dbl * beta * (1.0 - beta)).astype(BF16)
        ddt_ref[...] += jnp.sum(dsp, axis=0, keepdims=True)
        dal_ref[...] += jnp.sum(dgl * g, axis=0, keepdims=True)

    hs = pl.BlockSpec((RT, HW), lambda s: (nt - 1 - s, 0))
    sc = pl.BlockSpec((NH, RT, DH), lambda s: (0, nt - 1 - s, 0))
    one = pl.BlockSpec((1, DH), lambda s: (0, 0))
    xs = pl.BlockSpec((RT, CW), lambda s: (nt - 1 - s, 0))
    return _call(body, name="gdn_prep_bwd", grid=(nt,),
                 in_specs=_conv_ext_specs(lambda s: nt - 1 - s) + [
                     pl.BlockSpec((RT, DH), lambda s: (nt - 1 - s, 0)), pl.BlockSpec((CONV_K, CW), lambda s: (0, 0)),
                     one, one, hs, hs, hs, sc, sc],
                 out_specs=[xs, pl.BlockSpec((RT, DH), lambda s: (nt - 1 - s, 0)), pl.BlockSpec((8, CW), lambda s: (0, 0)), one, one],
                 out_shape=[_sds((tp, CW), BF16), _sds((tp, DH), BF16), _sds((8, CW), F32), _sds((1, DH), F32), _sds((1, DH), F32)],
                 scratch=[pltpu.VMEM((RT + 8, CW), F32), pltpu.VMEM((RT + 8, CW), F32)],
                 sem=("arbitrary",))(*([proj] * 6), pab, conv_w, alog, dtb, dq, dk, dv, dgb, dbb)


def _unit_lower_inverse(a):
    r = lax.broadcasted_iota(jnp.int32, (CH, CH), 0)
    c = lax.broadcasted_iota(jnp.int32, (CH, CH), 1)
    blk_of = lambda t, size: lax.shift_right_logical(t, int(math.log2(size)))
    a16 = jnp.where(blk_of(r, SUB) == blk_of(c, SUB), a, 0.0)
    x = (r == c).astype(F32) - a16
    p = a16
    for _ in range(3):
        p = _dot(p, p)
        x = x + _dot(x, p)
    for blk in (2 * SUB, 4 * SUB):
        off = jnp.where((blk_of(r, blk) == blk_of(c, blk)) & (blk_of(r, blk // 2) != blk_of(c, blk // 2)), a, 0.0)
        x = x - _dot(x, _dot(off, x))
    return x


def _gdn_chunk_common(q, k, v, gl, bt):
    gc = _cumsum(gl)
    e = jnp.exp(gc)
    rel = jnp.exp(jnp.minimum(gc[:, 0:CH] - gc.T[0:CH, :], 0.0))
    kb = bt * k
    a = jnp.where(_tri(CH, "strict"), bt[:, 0:CH] * _dot(k, k, NT) * rel, 0.0)
    x = _unit_lower_inverse(a)
    w = _dot(x, kb * e)
    u = _dot(x, bt * v)
    attn = jnp.where(_tri(CH, "incl"), _dot(q, k, NT) * rel, 0.0)
    return gc, e, rel, kb, a, x, w, u, attn


def _gdn_fwd(q, k, v, gb, bb, proj, nw):
    tp = q.shape[0]
    nt, cpt = tp // RT, RT // CH

    def body(q_ref, k_ref, v_ref, g_ref, b_ref, z_ref, nw_ref, og_ref, or_ref, st_ref, s_ref):
        @pl.when(pl.program_id(1) == 0)
        def _():
            s_ref[...] = jnp.zeros((HB, DH, DH), F32)

        def chunk(c, carry):
            rows = pl.ds(pl.multiple_of(c * CH, CH), CH)
            for hh in range(HB):
                cols = slice(hh * DH, (hh + 1) * DH)
                qc, kc, vc = q_ref[rows, cols], k_ref[rows, cols], v_ref[rows, cols]
                gc, e, rel, kb, a, x, w, u, attn = _gdn_chunk_common(qc, kc, vc, g_ref[hh, rows, :], b_ref[hh, rows, :])
                s = s_ref[hh]
                st_ref[hh, c] = s
                vn = u - _dot(w, s)
                o = _dot(qc * e, s) + _dot(attn, vn)
                gl = gc[CH - 1:CH]
                s_ref[hh] = s * jnp.exp(gl) + _dot(kc * jnp.exp(gl - gc), vn, TN)
                or_ref[rows, cols] = o
                og_ref[rows, cols] = _gated_norm_fwd(o, z_ref[rows, cols], nw_ref[...]).astype(BF16)
            return carry

        lax.fori_loop(0, cpt, chunk, 0)

    col, full, real, state, scal = _head_specs()
    return _call(body, name="gdn_fwd", grid=(NHB, nt),
                 in_specs=[full, full, full, scal, scal, col(7), pl.BlockSpec((1, DH), lambda h, i: (0, 0))],
                 out_specs=[real, full, state(cpt)],
                 out_shape=[_sds((tp - FRONT, HW), BF16), _sds((tp, HW), F32), _sds((NH, tp // CH, DH, DH), F32)],
                 scratch=[pltpu.VMEM((HB, DH, DH), F32)], sem=("parallel", "arbitrary"))(q, k, v, gb, bb, proj, nw)


def _gdn_bwd(q, k, v, gb, bb, proj, nw, o_raw, states, dog):
    tp = q.shape[0]
    nt, cpt = tp // RT, RT // CH

    def body(q_ref, k_ref, v_ref, g_ref, b_ref, z_ref, nw_ref, or_ref, st_ref, dog_ref,
             dq_ref, dk_ref, dv_ref, dg_ref, db_ref, dz_ref, dnw_ref, ds_ref, do_ref):
        step = pl.program_id(1)

        @pl.when(step == 0)
        def _():
            ds_ref[...] = jnp.zeros((HB, DH, DH), F32)

        @pl.when((step == 0) & (pl.program_id(0) == 0))
        def _():
            dnw_ref[...] = jnp.zeros((1, DH), F32)

        front = nt - 1 - step < OFF
        for hh in range(HB):
            cols = slice(hh * DH, (hh + 1) * DH)
            dog_t = jnp.where(front, 0.0, dog_ref[:, cols])
            do_t, dz_t, dnw = _gated_norm_bwd(or_ref[:, cols], z_ref[:, cols], nw_ref[...], dog_t)
            do_ref[:, cols] = do_t
            dz_ref[:, cols] = dz_t.astype(BF16)
            dnw_ref[...] += dnw
        incl, strict = _tri(CH, "incl"), _tri(CH, "strict")
        ones = jnp.ones((CH, DH), F32)
        rowid = lax.broadcasted_iota(jnp.int32, (CH, DH), 0)

        def rsum(t):
            return jnp.sum(t, axis=-1, keepdims=True)

        def chunk(cc, carry):
            c = cpt - 1 - cc
            rows = pl.ds(pl.multiple_of(c * CH, CH), CH)
            for hh in range(HB):
                cols = slice(hh * DH, (hh + 1) * DH)
                qc, kc, vc, do = q_ref[rows, cols], k_ref[rows, cols], v_ref[rows, cols], do_ref[rows, cols]
                bt = b_ref[hh, rows, :]
                gc, e, rel, kb, a, x, w, u, attn = _gdn_chunk_common(qc, kc, vc, g_ref[hh, rows, :], bt)
                s, dsn = st_ref[hh, c], ds_ref[hh]
                gl = gc[CH - 1:CH]
                el = jnp.exp(gl)
                cdec = jnp.exp(gl - gc)
                vn = u - _dot(w, s)
                dvn = _dot(attn, do, TN) + _dot(kc * cdec, dsn)
                dattn = jnp.where(incl, _dot(do, vn, NT), 0.0)
                dos = _dot(do, s, NT)
                vds = _dot(vn, dsn, NT)
                dar = dattn * rel
                dq = _dot(dar, kc) + e * dos
                dk = _dot(dar, qc, TN) + cdec * vds
                dgc = rsum(qc * e * dos)
                dc = cdec[:, 0:1] * rsum(kc * vds)
                dgc = dgc - dc
                dglast = jnp.sum(dc, axis=0, keepdims=True) + el[:, 0:1] * jnp.sum(rsum(dsn * s), axis=0, keepdims=True)
                ds_ref[hh] = dsn * el + _dot(qc * e, do, TN) - _dot(w, dvn, TN)
                dw = -_dot(dvn, s, NT)
                drw = _dot(x, dw, TN)
                dru = _dot(x, dvn, TN)
                da = -jnp.where(strict, _dot(drw, w, NT) + _dot(dru, u, NT), 0.0)
                dar2 = da * rel
                dkb = _dot(dar2, kc)
                rwk = rsum(drw * kc)
                dk = dk + _dot(dar2, kb, TN) + bt * dkb + (bt * e) * drw
                dbeta = rsum(dkb * kc) + e[:, 0:1] * rwk + rsum(dru * vc)
                dgc = dgc + bt[:, 0:1] * e[:, 0:1] * rwk
                z = dattn * attn + da * a
                dgc = dgc + rsum(z) - _dotx(z, ones, TN)[:, 0:1]
                dgcb = jnp.broadcast_to(dgc, (CH, DH)) + jnp.where(rowid == CH - 1, jnp.broadcast_to(dglast, (CH, DH)), 0.0)
                dq_ref[rows, cols] = dq
                dk_ref[rows, cols] = dk
                dv_ref[rows, cols] = bt * dru
                dg_ref[hh, rows, :] = _rcumsum(dgcb)
                db_ref[hh, rows, :] = jnp.broadcast_to(dbeta, (CH, DH))
            return carry

        lax.fori_loop(0, cpt, chunk, 0)

    col, full, real, state, scal = _head_specs(lambda i: nt - 1 - i)
    one = pl.BlockSpec((1, DH), lambda h, i: (0, 0))
    return _call(body, name="gdn_bwd", grid=(NHB, nt),
                 in_specs=[full, full, full, scal, scal, col(7), one, full, state(cpt), real],
                 out_specs=[full, full, full, scal, scal, full, one],
                 out_shape=[_sds((tp, HW), F32)] * 3 + [_sds((NH, tp, DH), F32)] * 2 + [_sds((tp, HW), BF16), _sds((1, DH), F32)],
                 scratch=[pltpu.VMEM((HB, DH, DH), F32), pltpu.VMEM((RT, HB * DH), F32)],
                 sem=("arbitrary", "arbitrary"))(q, k, v, gb, bb, proj, nw, o_raw, states, dog)


MAIN_W = 8 * HW
AB_W = 2 * NH


def _split_w_in(w_in):
    main = jnp.concatenate([w_in[:, :MAIN_W], w_in[:, MAIN_W + AB_W:]], axis=1)
    ab = jnp.pad(w_in[:, MAIN_W:MAIN_W + AB_W], ((0, 0), (0, DH - AB_W)))
    return main, ab


def _pad_lanes(v):
    return jnp.pad(v, ((0, 0), (0, DH - v.shape[1])))


class _NoComm:
    def __init__(self, late):
        self.late = late

    def proj_side(self):
        return None

    def late_weights(self, side_outs):
        return self.late

    def ffn_grads_side(self, dw_ffn_in, dw_ffn_out):
        return None

    def ffn_grads_done(self, side_outs):
        pass


def _local_step(x, tgt, meta, lb_logits, mix_w, w_in, hg_nw, conv_w, a_log, dt_bias, gd_nw, ffn_nw, final_w, comm):
    w_main, w_ab = _split_w_in(w_in)
    alog, dtb = _pad_lanes(a_log), _pad_lanes(dt_bias)
    final_w = final_w.reshape(1, -1)
    rows4 = lambda t: t.reshape(4, t.shape[0] // 4, t.shape[1])
    xn = _rms1_fwd(x, meta, mix_w)
    side = comm.proj_side()
    proj = _mm(xn, w_main, "nn", F32, 768, 1024, 2048, "proj_main", n_outer=True, side=side)
    proj, landed = proj if side is not None else (proj, None)
    w_a, w_b, w_out, w_ffn_in, w_ffn_out = comm.late_weights(landed)
    pab = _mm(xn, w_ab, "nn", F32, 768, 128, 2048, "proj_ab")
    oa_g, oa_raw, st_a = _hgrn2_fwd(proj, lb_logits, hg_nw)
    q, k, v, gb, bb = _gdn_prep_fwd(proj, pab, conv_w, alog, dtb)
    ob_g, ob_raw, st_b = _gdn_fwd(q, k, v, gb, bb, proj, gd_nw)
    za = _mm(oa_g, w_a, "nn", F32, 1024, 512, 1024, "branch_a", n_outer=True)
    zb = _mm(ob_g, w_b, "nn", F32, 1024, 512, 1024, "branch_b", n_outer=True)
    merged = _merge_fwd(proj, za, zb)
    mix = _mm(merged, w_out, "nn", F32, 1024, 2048, 2048, "mix_out")
    h1, n2 = _resid_norm_fwd(x, mix, ffn_nw)
    gu = _mm(n2, w_ffn_in, "nn", F32, 1024, 1408, 2048, "ffn_in", n_outer=True)
    act = _swiglu_fwd(gu)
    f = _mm(act, w_ffn_out, "nn", F32, 1024, 2048, 512, "ffn_out")
    lt, dh2, dh2b, dfinal = _loss_head(h1, f, final_w, tgt)
    loss = _sum_tiles(lt)
    dact = _mm(dh2b, w_ffn_out, "nt", F32, 1024, 512, 2048, "d_act", n_outer=True)
    dw_ffn_out = rows4(_mm(act, dh2b, "tn", F32, 512, 2048, 1024, "dw_ffn_out"))
    dgu = _swiglu_bwd(gu, dact)
    dn2 = _mm(dgu, w_ffn_in, "nt", F32, 1024, 2048, 1408, "d_n2")
    dw_ffn_in = _mm(n2, dgu, "tn", F32, 1024, 1408, 1024, "dw_ffn_in", out_shards=4)
    side = comm.ffn_grads_side(dw_ffn_in, dw_ffn_out)
    dh1, dh1b, dffn_nw = _resid_norm_bwd(h1, ffn_nw, dn2, dh2)
    dmerged = _mm(dh1b, w_out, "nt", F32, 1024, 2048, 2048, "d_merged")
    dw_out = _mm(merged, dh1b, "tn", F32, 2048, 1024, 1024, "dw_out")
    dza, dzb, dgate = _merge_bwd(proj, za, zb, dmerged)
    doa = _mm(dza, w_a, "nt", F32, 1024, 1024, 512, "d_oa")
    dob = _mm(dzb, w_b, "nt", F32, 1024, 1024, 512, "d_ob")
    dw_a = _mm(oa_g, dza, "tn", F32, 1024, 512, 1024, "dw_branch_a", out_shards=4)
    dw_b = _mm(ob_g, dzb, "tn", F32, 1024, 512, 1024, "dw_branch_b", out_shards=4)
    hg = _hgrn2_bwd(proj, lb_logits, hg_nw, oa_raw, st_a, doa, side=side)
    if side is not None:
        hg, arrived = hg
        comm.ffn_grads_done(arrived)
    dhq, dhf, dhi, dhg, dlbl, dhg_nw = hg
    dq, dk, dv, dg, dbeta, dz, dgd_nw = _gdn_bwd(q, k, v, gb, bb, proj, gd_nw, ob_raw, st_b, dob)
    dx3, dab, dconv, dalog, ddtb = _gdn_prep_bwd(proj, pab, conv_w, alog, dtb, dq, dk, dv, dg, dbeta)
    dproj = jnp.concatenate([dhq, dhf, dhi, dhg, dx3, dz, dgate], axis=1)
    dxn = _mm(dproj, w_main, "nt", F32, 768, 2048, 1024, "d_xn")
    dxn = _mm(dab, w_ab, "nt", F32, 768, 2048, 128, "d_xn_ab", add=dxn)
    dw_main = _mm(xn, dproj, "tn", F32, 2048, 1024, 768, "dw_in_main")
    dw_ab = _mm(xn, dab, "tn", F32, 2048, 128, 768, "dw_in_ab")
    dx, dmeta, dmix_w = _rms1_bwd(x, meta, mix_w, dxn, dh1)
    d_model = dw_main.shape[0]
    dw_in = jnp.concatenate([dw_main[:, :MAIN_W], dw_ab[:, :AB_W], dw_main[:, MAIN_W:]], axis=1)
    dw_in = dw_in.reshape(d_model, 4, -1).transpose(1, 0, 2)
    grads = dict(meta_tokens=dmeta, lb_logits=dlbl, mix_norm_w=dmix_w, w_in=dw_in,
                 hg_norm_w=dhg_nw, gd_conv_w=dconv[:CONV_K], gd_a_log=dalog[:, :NH],
                 gd_dt_bias=ddtb[:, :NH], gd_norm_w=dgd_nw, w_branch_a=dw_a, w_branch_b=dw_b,
                 w_out=rows4(dw_out), ffn_norm_w=dffn_nw, w_ffn_in=dw_ffn_in, w_ffn_out=dw_ffn_out,
                 final_norm_w=dfinal.reshape(-1))
    return loss, dx, grads


def _adamw(g, w, m, v, name):
    rows, cols = g.shape
    tr = rows
    for cand in (128, 64, 32, 16, 8):
        if rows % cand == 0 and rows > cand:
            tr = cand
            break

    def body(g_ref, w_ref, m_ref, v_ref, d_ref, nm_ref, nv_ref):
        gg = g_ref[...]
        nm = ADAM_B1 * m_ref[...] + (1.0 - ADAM_B1) * gg
        nv = ADAM_B2 * v_ref[...] + (1.0 - ADAM_B2) * (gg * gg)
        m_hat = nm / (1.0 - ADAM_B1 ** ADAM_STEP)
        v_hat = nv / (1.0 - ADAM_B2 ** ADAM_STEP)
        d_ref[...] = -ADAM_LR * (m_hat / (jnp.sqrt(v_hat) + ADAM_EPS) + ADAM_WD * w_ref[...])
        nm_ref[...] = nm
        nv_ref[...] = nv

    bs = pl.BlockSpec((tr, cols), lambda i: (i, 0))
    return _call(body, name=name, grid=(rows // tr,), in_specs=[bs] * 4, out_specs=[bs] * 3,
                 out_shape=[_sds((rows, cols), F32)] * 3, sem=("parallel",))(g, w, m, v)


HBM = pl.BlockSpec(memory_space=pltpu.HBM)
MESH = pl.DeviceIdType.MESH


def _place():
    x, y, c = lax.axis_index("x"), lax.axis_index("y"), lax.axis_index("c")
    return x, y, c, [(1 - x, y), (x, 1 - y), (1 - x, 1 - y)]


def _comm_call(body, name, out_shape, n_in, scratch):
    return pl.pallas_call(body, name=name, out_shape=out_shape, in_specs=[HBM] * n_in,
                          out_specs=jax.tree.map(lambda _: HBM, out_shape), scratch_shapes=scratch)


def _half_rows(rows, c, tile):
    hh = rows // 2
    assert rows % 2 == 0 and hh % tile == 0, (rows, tile)
    return pl.ds(pl.multiple_of(c * hh, tile), hh)


def _gather_copies(w_refs, out_refs, sems):
    send_sems, recv_sems, local_sems = sems
    x, y, c, chips = _place()
    s_me = 2 * x + y
    mine, sends, recvs = [], [], []
    for k, (w_ref, out_ref) in enumerate(zip(w_refs, out_refs)):
        half = _half_rows(w_ref.shape[0], c, 16)
        mine.append(pltpu.make_async_copy(w_ref, out_ref.at[s_me], local_sems.at[k]))
        for j, (cx, cy) in enumerate(chips):
            sem = dict(send_sem=send_sems.at[3 * k + j], recv_sem=recv_sems.at[3 * k + j], device_id=(cx, cy, c), device_id_type=MESH)
            sends.append(pltpu.make_async_remote_copy(src_ref=w_ref.at[half], dst_ref=out_ref.at[s_me, half], **sem))
            recvs.append(pltpu.make_async_remote_copy(src_ref=w_ref.at[half], dst_ref=out_ref.at[2 * cx + cy, half], **sem))
    return mine, sends, recvs


def _gather_sems(n):
    return [pltpu.SemaphoreType.DMA((3 * n,)), pltpu.SemaphoreType.DMA((3 * n,)), pltpu.SemaphoreType.DMA((n,))]


def _gather_start(w_refs, out_refs, sems):
    mine, sends, _ = _gather_copies(w_refs, out_refs, sems)
    for cp in mine + sends:
        cp.start()


def _gather_wait(w_refs, out_refs, sems):
    mine, sends, recvs = _gather_copies(w_refs, out_refs, sems)
    for cp in recvs:
        cp.wait_recv()
    for cp in sends:
        cp.wait_send()
    for cp in mine:
        cp.wait()


def _gather_chips(shards):
    n = len(shards)

    def body(*refs):
        _gather_start(refs[:n], refs[n:2 * n], refs[2 * n:])
        _gather_wait(refs[:n], refs[n:2 * n], refs[2 * n:])

    return _comm_call(body, "gather_chips", [_sds((4,) + w.shape, w.dtype) for w in shards], n, _gather_sems(n))(*shards)


def _gather_side(shards):
    return _Side(shards, [_sds((4,) + w.shape, w.dtype) for w in shards], _gather_sems(len(shards)), _gather_start, _gather_wait)


def _forward_halves(outs, name):
    n = len(outs)

    def body(*refs):
        out_refs = refs[n:2 * n]
        send_sems, recv_sems = refs[2 * n:]
        x, y, c, chips = _place()
        cps = []
        for k in range(n):
            rows = out_refs[k].shape[1]
            half, other = _half_rows(rows, c, 16), _half_rows(rows, 1 - c, 16)
            for j, (cx, cy) in enumerate(chips):
                sem = dict(send_sem=send_sems.at[3 * k + j], recv_sem=recv_sems.at[3 * k + j], device_id=(x, y, 1 - c), device_id_type=MESH)
                landed = out_refs[k].at[2 * cx + cy, half]
                cps.append(pltpu.make_async_remote_copy(src_ref=landed, dst_ref=landed, **sem))
                cps[-1].start()
        for k in range(n):
            rows = out_refs[k].shape[1]
            half, other = _half_rows(rows, c, 16), _half_rows(rows, 1 - c, 16)
            for j, (cx, cy) in enumerate(chips):
                sem = dict(send_sem=send_sems.at[3 * k + j], recv_sem=recv_sems.at[3 * k + j], device_id=(x, y, 1 - c), device_id_type=MESH)
                pltpu.make_async_remote_copy(src_ref=out_refs[k].at[2 * cx + cy, half], dst_ref=out_refs[k].at[2 * cx + cy, other], **sem).wait_recv()
        for cp in cps:
            cp.wait_send()

    shapes = [_sds(o.shape, o.dtype) for o in outs]
    return pl.pallas_call(body, name=name, out_shape=shapes, in_specs=[HBM] * n, out_specs=[HBM] * n,
                          input_output_aliases={k: k for k in range(n)},
                          scratch_shapes=[pltpu.SemaphoreType.DMA((3 * n,)), pltpu.SemaphoreType.DMA((3 * n,))])(*outs)


def _swap_halves(gs, name):
    n = len(gs)

    def body(*refs):
        g_refs, out_refs = refs[:n], refs[n:2 * n]
        send_sems, recv_sems = refs[2 * n:]
        x, y, c, _ = _place()
        cps = []
        for k in range(n):
            other = _half_rows(g_refs[k].shape[1], 1 - c, 8)
            cps.append(pltpu.make_async_remote_copy(src_ref=g_refs[k].at[:, other, :], dst_ref=out_refs[k], send_sem=send_sems.at[k],
                                                    recv_sem=recv_sems.at[k], device_id=(x, y, 1 - c), device_id_type=MESH))
            cps[-1].start()
        for cp in cps:
            cp.wait()

    return _comm_call(body, name, [_sds((4, g.shape[1] // 2, g.shape[2]), g.dtype) for g in gs], n,
                      [pltpu.SemaphoreType.DMA((n,)), pltpu.SemaphoreType.DMA((n,))])(*gs)


def _row_tile(rows, row_bytes, budget=3 << 20):
    if rows * row_bytes <= budget:
        return rows
    return max(t for t in range(16, rows, 16) if rows % t == 0 and t * row_bytes <= budget)


def _add_half(g, got, c, name):
    _, rows, cols = g.shape
    hh = rows // 2
    tr = _row_tile(hh, cols * 4)
    nb = hh // tr

    def body(c_ref, a_ref, b_ref, o_ref):
        o_ref[...] = (a_ref[...] + b_ref[...]).astype(BF16)

    gs = pltpu.PrefetchScalarGridSpec(
        num_scalar_prefetch=1, grid=(4, nb),
        in_specs=[pl.BlockSpec((1, tr, cols), lambda s, i, c_ref: (s, c_ref[0] * nb + i, 0)),
                  pl.BlockSpec((1, tr, cols), lambda s, i, c_ref: (s, i, 0))],
        out_specs=pl.BlockSpec((1, tr, cols), lambda s, i, c_ref: (s, i, 0)))
    return pl.pallas_call(body, name=name, grid_spec=gs, out_shape=_sds((4, hh, cols), BF16),
                          compiler_params=pltpu.CompilerParams(dimension_semantics=("parallel", "parallel"),
                                                               vmem_limit_bytes=VMEM_LIMIT))(c, g, got)


def _scatter_copies(p_refs, out_refs, sems):
    send_sems, recv_sems, local_sems = sems
    x, y, c, chips = _place()
    s_me = 2 * x + y
    mine, cps = [], []
    for k, (p_ref, out_ref) in enumerate(zip(p_refs, out_refs)):
        mine.append(pltpu.make_async_copy(p_ref.at[s_me], out_ref.at[s_me], local_sems.at[k]))
        for j, (cx, cy) in enumerate(chips):
            cps.append(pltpu.make_async_remote_copy(src_ref=p_ref.at[2 * cx + cy], dst_ref=out_ref.at[s_me],
                                                    send_sem=send_sems.at[3 * k + j], recv_sem=recv_sems.at[3 * k + j],
                                                    device_id=(cx, cy, c), device_id_type=MESH))
    return mine, cps


def _scatter_start(p_refs, out_refs, sems):
    mine, cps = _scatter_copies(p_refs, out_refs, sems)
    for cp in mine + cps:
        cp.start()


def _scatter_wait(p_refs, out_refs, sems):
    mine, cps = _scatter_copies(p_refs, out_refs, sems)
    for cp in cps:
        cp.wait()
    for cp in mine:
        cp.wait()


def _scatter_chips(ps):
    n = len(ps)

    def body(*refs):
        _scatter_start(refs[:n], refs[n:2 * n], refs[2 * n:])
        _scatter_wait(refs[:n], refs[n:2 * n], refs[2 * n:])

    return _comm_call(body, "scatter_chips", [_sds(p_.shape, p_.dtype) for p_ in ps], n, _gather_sems(n))(*ps)


def _scatter_side(ps):
    return _Side(ps, [_sds(p_.shape, p_.dtype) for p_ in ps], _gather_sems(len(ps)), _scatter_start, _scatter_wait)


def _sum_slabs(b, name):
    n, h, wd = b.shape
    tr = _row_tile(h, n * wd * 4, 6 << 20)

    def body(b_ref, o_ref):
        acc = b_ref[0].astype(F32)
        for s in range(1, n):
            acc = acc + b_ref[s].astype(F32)
        o_ref[...] = acc

    return _call(body, name=name, grid=(h // tr,), in_specs=[pl.BlockSpec((n, tr, wd), lambda i: (0, i, 0))],
                 out_specs=pl.BlockSpec((tr, wd), lambda i: (i, 0)), out_shape=_sds((h, wd), F32), sem=("parallel",))(b)


def _share_halves(ghs):
    n = len(ghs)

    def body(*refs):
        g_refs, out_refs = refs[:n], refs[n:2 * n]
        send_sems, recv_sems, local_sems = refs[2 * n:]
        x, y, c, _ = _place()
        mine, cps = [], []
        for k in range(n):
            half = _half_rows(out_refs[k].shape[0], c, 8)
            mine.append(pltpu.make_async_copy(g_refs[k], out_refs[k].at[half], local_sems.at[k]))
            mine[-1].start()
            cps.append(pltpu.make_async_remote_copy(src_ref=g_refs[k], dst_ref=out_refs[k].at[half], send_sem=send_sems.at[k],
                                                    recv_sem=recv_sems.at[k], device_id=(x, y, 1 - c), device_id_type=MESH))
            cps[-1].start()
        for cp in cps:
            cp.wait()
        for cp in mine:
            cp.wait()

    return _comm_call(body, "share_halves", [_sds((2 * g.shape[0], g.shape[1]), g.dtype) for g in ghs], n,
                      [pltpu.SemaphoreType.DMA((n,)), pltpu.SemaphoreType.DMA((n,)), pltpu.SemaphoreType.DMA((n,))])(*ghs)


def _gather_all(v, name):
    def body(v_ref, out_ref, send_sems, recv_sems, local_sem):
        x, y, c = lax.axis_index("x"), lax.axis_index("y"), lax.axis_index("c")
        me = 4 * x + 2 * y + c
        flip = lambda t, d: 1 - t if d else t
        mine = pltpu.make_async_copy(v_ref, out_ref.at[me], local_sem)
        mine.start()
        cps = []
        for k in range(1, 8):
            to = (flip(x, k & 4), flip(y, k & 2), flip(c, k & 1))
            cps.append(pltpu.make_async_remote_copy(src_ref=v_ref, dst_ref=out_ref.at[me], send_sem=send_sems.at[k - 1],
                                                    recv_sem=recv_sems.at[k - 1], device_id=to, device_id_type=MESH))
        for cp in cps:
            cp.start()
        for cp in cps:
            cp.wait()
        mine.wait()

    return _comm_call(body, name, _sds((8,) + v.shape, v.dtype), 1,
                      [pltpu.SemaphoreType.DMA((7,)), pltpu.SemaphoreType.DMA((7,)), pltpu.SemaphoreType.DMA])(v)


BIG = (("w_in", 1), ("w_branch_a", 1), ("w_branch_b", 1), ("w_out", 0), ("w_ffn_in", 1), ("w_ffn_out", 0))
SMALL = ("meta_tokens", "lb_logits", "mix_norm_w", "hg_norm_w", "gd_conv_w", "gd_a_log", "gd_dt_bias", "gd_norm_w",
         "ffn_norm_w", "final_norm_w")


def _pack_lanes(parts):
    rows = []
    for p in parts:
        f = p.reshape(-1).astype(F32)
        n = -(-f.shape[0] // DH) * DH
        rows.append(jnp.pad(f, (0, n - f.shape[0])).reshape(-1, DH))
    buf = jnp.concatenate(rows, axis=0)
    return jnp.pad(buf, ((0, -buf.shape[0] % 8), (0, 0)))


def _unpack_lanes(buf, shapes):
    out, off = [], 0
    for shp in shapes:
        n = math.prod(shp)
        r = -(-n // DH)
        out.append(buf[off:off + r].reshape(-1)[:n].reshape(shp))
        off += r
    return out


def kernel(x, meta_tokens, lb_logits, mix_norm_w, w_in, hg_norm_w, gd_conv_w, gd_a_log, gd_dt_bias, gd_norm_w, w_branch_a, w_branch_b, w_out, ffn_norm_w, w_ffn_in, w_ffn_out, final_norm_w, loss_target, m_meta_tokens, m_lb_logits, m_mix_norm_w, m_w_in, m_hg_norm_w, m_gd_conv_w, m_gd_a_log, m_gd_dt_bias, m_gd_norm_w, m_w_branch_a, m_w_branch_b, m_w_out, m_ffn_norm_w, m_w_ffn_in, m_w_ffn_out, m_final_norm_w, v_meta_tokens, v_lb_logits, v_mix_norm_w, v_w_in, v_hg_norm_w, v_gd_conv_w, v_gd_a_log, v_gd_dt_bias, v_gd_norm_w, v_w_branch_a, v_w_branch_b, v_w_out, v_ffn_norm_w, v_w_ffn_in, v_w_ffn_out, v_final_norm_w):
    args = dict(locals())
    big = [n for n, _ in BIG]
    w = {n: args[n] for n in SMALL + tuple(big)}
    m = {n: args["m_" + n] for n in w}
    v = {n: args["v_" + n] for n in w}
    xi, yi, ci = lax.axis_index("x"), lax.axis_index("y"), lax.axis_index("c")
    shard = 2 * xi + yi
    big_local = {n: w[n][0] for n in big}

    meta_cols, conv_cols = meta_tokens.shape[1], gd_conv_w.shape[-1]
    sm_all = _gather_all(_pack_lanes([meta_tokens, gd_conv_w[0]]), "gather_meta")
    sm_parts = [_unpack_lanes(sm_all[2 * s], [meta_tokens.shape, gd_conv_w[0].shape]) for s in range(4)]
    meta_full = jnp.concatenate([p[0] for p in sm_parts], axis=1)
    conv_full = jnp.concatenate([p[1] for p in sm_parts], axis=1)
    cvec = ci.reshape(1).astype(jnp.int32)
    late = [n for n in big if n != "w_in"]
    ffn = ["w_ffn_in", "w_ffn_out"]
    rows_full = lambda t: t.reshape(t.shape[0] * t.shape[1], t.shape[2])

    def pair_sums(names, gs):
        return [_add_half(gk, got, cvec, "add_half_" + n) for n, gk, got in zip(names, gs, _swap_halves(gs, "swap_" + names[0]))]

    class MeshComm:
        def proj_side(self):
            return _gather_side([big_local[n].astype(BF16) for n in late])

        def late_weights(self, landed):
            wl = dict(zip(late, _forward_halves(landed, "forward_late")))
            return (wl["w_branch_a"], wl["w_branch_b"], rows_full(wl["w_out"]), wl["w_ffn_in"], rows_full(wl["w_ffn_out"]))

        def ffn_grads_side(self, dw_ffn_in, dw_ffn_out):
            return _scatter_side(pair_sums(ffn, [dw_ffn_in, dw_ffn_out]))

        def ffn_grads_done(self, arrived):
            self.ffn_arrived = arrived

    comm = MeshComm()
    w_in_slabs = _forward_halves(_gather_chips([big_local["w_in"].astype(BF16)]), "forward_w_in")[0]
    d_model = w_in_slabs.shape[1]
    w_in_full = w_in_slabs.transpose(1, 0, 2).reshape(d_model, -1)
    loss, dx, g = _local_step(x[0], loss_target[0], meta_full, lb_logits, mix_norm_w, w_in_full, hg_norm_w, conv_full,
                              gd_a_log, gd_dt_bias, gd_norm_w, ffn_norm_w, final_norm_w, comm)
    loss = lax.psum(loss[0, 0], ("x", "y", "c"))

    rest = [n for n in big if n not in ffn]
    arrived = dict(zip(rest, _scatter_chips(pair_sums(rest, [g[n] for n in rest]))))
    arrived.update(zip(ffn, comm.ffn_arrived))
    halves = [_sum_slabs(arrived[n], "sum_chips_" + n) for n in big]
    g_big = dict(zip(big, _share_halves(halves)))

    small_shapes = [g[n].shape for n in SMALL]
    g_all = _gather_all(_pack_lanes([g[n] for n in SMALL]), "gather_small")
    g_small = dict(zip(SMALL, _unpack_lanes(_sum_slabs(g_all, "sum_small"), small_shapes)))
    g_small["meta_tokens"] = lax.dynamic_slice_in_dim(g_small["meta_tokens"], shard * meta_cols, meta_cols, axis=1)
    g_small["gd_conv_w"] = lax.dynamic_slice_in_dim(g_small["gd_conv_w"], shard * conv_cols, conv_cols, axis=1)

    grad, delta, new_m, new_v = {}, {}, {}, {}
    for n in big:
        grad[n] = g_big[n].reshape(w[n].shape)
        d_, m_, v_ = _adamw(g_big[n], big_local[n], m[n][0], v[n][0], "adamw_" + n)
        delta[n], new_m[n], new_v[n] = d_.reshape(w[n].shape), m_.reshape(w[n].shape), v_.reshape(w[n].shape)
    local_shapes = [w[n].shape for n in SMALL]
    d_, m_, v_ = _adamw(_pack_lanes([g_small[n] for n in SMALL]), _pack_lanes([w[n] for n in SMALL]),
                        _pack_lanes([m[n] for n in SMALL]), _pack_lanes([v[n] for n in SMALL]), "adamw_small")
    for n, gs_, dd, mm, vv in zip(SMALL, [g_small[n] for n in SMALL], _unpack_lanes(d_, local_shapes), _unpack_lanes(m_, local_shapes),
                                  _unpack_lanes(v_, local_shapes)):
        grad[n], delta[n], new_m[n], new_v[n] = gs_.reshape(w[n].shape), dd, mm, vv

    order = ["meta_tokens", "lb_logits", "mix_norm_w", "w_in", "hg_norm_w", "gd_conv_w", "gd_a_log", "gd_dt_bias", "gd_norm_w",
             "w_branch_a", "w_branch_b", "w_out", "ffn_norm_w", "w_ffn_in", "w_ffn_out", "final_norm_w"]
    return (loss, dx[None], *[grad[n] for n in order], *[delta[n] for n in order], *[new_m[n] for n in order],
            *[new_v[n] for n in order])
```

```python
import functools
import math

import jax
import jax.numpy as jnp
from jax import lax
from jax.experimental import pallas as pl
from jax.experimental.pallas import tpu as pltpu

F32, BF16 = jnp.float32, jnp.bfloat16
HI = lax.Precision.HIGHEST
EPS = 1e-6
D_MODEL = 2048
N_META = 16
FRONT = 256
CH = 64
SUB = 16
DH = 128
NH = 8
HW = NH * DH
CONV_K = 4
RT = 256
VMEM_LIMIT = 56 * 1024 * 1024
ADAM_LR, ADAM_B1, ADAM_B2, ADAM_EPS, ADAM_WD, ADAM_STEP = 0.001, 0.9, 0.999, 1e-08, 0.01, 10

NN = (((1,), (0,)), ((), ()))
NT = (((1,), (1,)), ((), ()))
TN = (((0,), (0,)), ((), ()))


def _dot(a, b, dn=NN):
    return lax.dot_general(a.astype(BF16), b.astype(BF16), dn, preferred_element_type=F32)


def _dotx(a, b, dn=NN):
    return lax.dot_general(a, b, dn, precision=HI, preferred_element_type=F32)


class _Side:
    def __init__(self, inputs, out_shapes, scratch, start, wait):
        self.inputs, self.out_shapes, self.scratch, self.start, self.wait = inputs, out_shapes, scratch, start, wait


def _call(body, *, name, grid, in_specs, out_specs, out_shape, scratch=(), sem=None, side=None):
    params = pltpu.CompilerParams(dimension_semantics=sem, vmem_limit_bytes=VMEM_LIMIT)
    if side is None:
        return pl.pallas_call(body, name=name, grid=grid, in_specs=in_specs, out_specs=out_specs, out_shape=out_shape,
                              scratch_shapes=list(scratch), compiler_params=params)
    single = not isinstance(out_specs, (list, tuple))
    out_specs, out_shape = ([out_specs], [out_shape]) if single else (list(out_specs), list(out_shape))
    ni, no, ns = len(in_specs), len(out_specs), len(scratch)
    nsi, nso = len(side.inputs), len(side.out_shapes)
    hbm = pl.BlockSpec(memory_space=pltpu.HBM)

    def wrapped(*refs):
        main_in, side_in = refs[:ni], refs[ni:ni + nsi]
        main_out, side_out = refs[ni + nsi:ni + nsi + no], refs[ni + nsi + no:ni + nsi + no + nso]
        main_scr, side_scr = refs[ni + nsi + no + nso:ni + nsi + no + nso + ns], refs[ni + nsi + no + nso + ns:]
        pids = [pl.program_id(d) for d in range(len(grid))]
        first = functools.reduce(lambda a, b: a & b, [p == 0 for p in pids])
        last = functools.reduce(lambda a, b: a & b, [p == g - 1 for p, g in zip(pids, grid)])

        @pl.when(first)
        def _():
            side.start(side_in, side_out, side_scr)

        body(*main_in, *main_out, *main_scr)

        @pl.when(last)
        def _():
            side.wait(side_in, side_out, side_scr)

    call = pl.pallas_call(wrapped, name=name, grid=grid, in_specs=list(in_specs) + [hbm] * nsi,
                          out_specs=out_specs + [hbm] * nso, out_shape=out_shape + list(side.out_shapes),
                          scratch_shapes=list(scratch) + list(side.scratch), compiler_params=params)

    def run(*args):
        outs = call(*args, *side.inputs)
        main = outs[0] if single else list(outs[:no])
        return main, list(outs[no:])

    return run


def _divmod(j, per):
    if per == 1:
        return j, 0
    return lax.div(j, jnp.int32(per)), lax.rem(j, jnp.int32(per))


def _sds(shape, dtype):
    return jax.ShapeDtypeStruct(tuple(shape), dtype)


def _sigmoid(x):
    return 1.0 / (1.0 + jnp.exp(-x))


def _silu(x):
    return x * _sigmoid(x)


def _dsilu(x):
    s = _sigmoid(x)
    return s * (1.0 + x * (1.0 - s))


def _tri(n, kind):
    r = lax.broadcasted_iota(jnp.int32, (n, n), 0)
    c = lax.broadcasted_iota(jnp.int32, (n, n), 1)
    return {"incl": r >= c, "strict": r > c, "upper": c >= r}[kind]


def _mm(a, b, mode, out_dtype, tm, tn, tk, name, add=None, n_outer=False, out_shards=0, side=None):
    sharded_b = b.ndim == 3
    if sharded_b:
        S, R, n = b.shape
        b_rows, b_cols = R, S * n
    else:
        b_rows, b_cols = b.shape
    if mode == "nn":
        (M, K), N, dn = a.shape, b_cols, NN
    elif mode == "nt":
        (M, K), N, dn = a.shape, b_rows, NT
    else:
        (K, M), N, dn = a.shape, b_cols, TN
    tm, tn, tk = min(tm, M), min(tn, N), min(tk, K)
    if sharded_b:
        tn, tk = (min(tn, n), tk) if mode != "nt" else (tn, min(tk, n))
    if out_shards:
        tn = min(tn, N // out_shards)
    assert M % tm == 0 and N % tn == 0 and K % tk == 0, (name, M, N, K, tm, tn, tk)
    nk = K // tk
    a_blk, a_idx = ((tm, tk), lambda i, j, k: (i, k)) if mode != "tn" else ((tk, tm), lambda i, j, k: (k, i))
    if not sharded_b:
        b_blk, b_idx = ((tk, tn), lambda i, j, k: (k, j)) if mode != "nt" else ((tn, tk), lambda i, j, k: (j, k))
    elif mode != "nt":
        per = n // tn
        assert n % tn == 0
        b_blk, b_idx = (None, tk, tn), lambda i, j, k: (_divmod(j, per)[0], k, _divmod(j, per)[1])
    else:
        per = n // tk
        assert n % tk == 0
        b_blk, b_idx = (None, tn, tk), lambda i, j, k: (_divmod(k, per)[0], j, _divmod(k, per)[1])
    if out_shards:
        per_o = N // out_shards // tn
        assert (N // out_shards) % tn == 0
        o_blk, o_idx = (None, tm, tn), lambda i, j, k: (_divmod(j, per_o)[0], i, _divmod(j, per_o)[1])
        o_shape = (out_shards, M, N // out_shards)
    else:
        o_blk, o_idx, o_shape = (tm, tn), (lambda i, j, k: (i, j)), (M, N)
    c_idx = lambda i, j, k: (i, j)
    if n_outer:
        sw = lambda f: (lambda j, i, k: f(i, j, k))
        a_idx, b_idx, o_idx, c_idx = sw(a_idx), sw(b_idx), sw(o_idx), sw(c_idx)
        grid = (N // tn, M // tm, nk)
    else:
        grid = (M // tm, N // tn, nk)
    has_add = add is not None

    def body(*refs):
        if has_add:
            a_ref, b_ref, c_ref, o_ref, acc_ref = refs
        else:
            a_ref, b_ref, o_ref, acc_ref = refs
            c_ref = None
        part = lax.dot_general(a_ref[...].astype(BF16), b_ref[...].astype(BF16), dn, preferred_element_type=F32)

        def fin(val):
            if has_add:
                val = val + c_ref[...]
            o_ref[...] = val.astype(out_dtype)

        if nk == 1:
            fin(part)
        else:
            k = pl.program_id(2)

            @pl.when(k == 0)
            def _():
                acc_ref[...] = part

            @pl.when(k > 0)
            def _():
                acc_ref[...] += part

            @pl.when(k == nk - 1)
            def _():
                fin(acc_ref[...])

    in_specs = [pl.BlockSpec(a_blk, a_idx), pl.BlockSpec(b_blk, b_idx)]
    args = [a, b]
    if has_add:
        in_specs.append(pl.BlockSpec((tm, tn), c_idx))
        args.append(add)
    acc_shape = (tm, tn) if nk > 1 else (8, 128)
    return _call(body, name=name, grid=grid, in_specs=in_specs, out_specs=pl.BlockSpec(o_blk, o_idx),
                 out_shape=_sds(o_shape, out_dtype), scratch=[pltpu.VMEM(acc_shape, F32)],
                 sem=("arbitrary",) * 3 if side is not None else ("parallel", "parallel", "arbitrary"), side=side)(*args)


def _rms1_fwd(x, meta, w):
    seq, d = x.shape
    nt = (FRONT + seq) // RT

    def body(x_ref, m_ref, w_ref, o_ref):
        i = pl.program_id(0)

        def norm(v):
            r = lax.rsqrt(jnp.mean(v * v, axis=-1, keepdims=True) + EPS)
            return (v * r * w_ref[...]).astype(BF16)

        @pl.when(i == 0)
        def _():
            o_ref[0:RT - N_META, :] = jnp.zeros((RT - N_META, d), BF16)
            o_ref[RT - N_META:RT, :] = norm(m_ref[...])

        @pl.when(i > 0)
        def _():
            o_ref[...] = norm(x_ref[...])

    return _call(body, name="rms1_fwd", grid=(nt,),
                 in_specs=[pl.BlockSpec((RT, d), lambda i: (jnp.maximum(i - 1, 0), 0)),
                           pl.BlockSpec((N_META, d), lambda i: (0, 0)),
                           pl.BlockSpec((1, d), lambda i: (0, 0))],
                 out_specs=pl.BlockSpec((RT, d), lambda i: (i, 0)),
                 out_shape=_sds((FRONT + seq, d), BF16), sem=("parallel",))(x, meta, w)


def _rms1_bwd(x, meta, w, dxn, dh1):
    seq, d = x.shape
    nt = (FRONT + seq) // RT

    def body(x_ref, m_ref, w_ref, g_ref, r_ref, dx_ref, dm_ref, dw_ref):
        i = pl.program_id(0)

        def bwd(v, g):
            r = lax.rsqrt(jnp.mean(v * v, axis=-1, keepdims=True) + EPS)
            vh = v * r
            gh = g * w_ref[...]
            return r * (gh - vh * jnp.mean(gh * vh, axis=-1, keepdims=True)), jnp.sum(g * vh, axis=0, keepdims=True)

        @pl.when(i == 0)
        def _():
            dm, dw = bwd(m_ref[...], g_ref[RT - N_META:RT, :])
            dm_ref[...] = dm
            dw_ref[...] = dw

        @pl.when(i > 0)
        def _():
            dx, dw = bwd(x_ref[...], g_ref[...])
            dx_ref[...] = dx + r_ref[...]
            dw_ref[...] += dw

    xs = pl.BlockSpec((RT, d), lambda i: (jnp.maximum(i - 1, 0), 0))
    return _call(body, name="rms1_bwd", grid=(nt,),
                 in_specs=[xs, pl.BlockSpec((N_META, d), lambda i: (0, 0)), pl.BlockSpec((1, d), lambda i: (0, 0)),
                           pl.BlockSpec((RT, d), lambda i: (i, 0)), xs],
                 out_specs=[xs, pl.BlockSpec((N_META, d), lambda i: (0, 0)), pl.BlockSpec((1, d), lambda i: (0, 0))],
                 out_shape=[_sds((seq, d), F32), _sds((N_META, d), F32), _sds((1, d), F32)],
                 sem=("arbitrary",))(x, meta, w, dxn, dh1)


def _merge_fwd(proj, za, zb):
    seq, d = za.shape
    off = FRONT // RT
    ca, cb = 8 * HW // d, 8 * HW // d + 1

    def body(ga_ref, gb_ref, za_ref, zb_ref, o_ref):
        o_ref[...] = (_sigmoid(ga_ref[...]) * za_ref[...] + _sigmoid(gb_ref[...]) * zb_ref[...]).astype(BF16)

    zs = pl.BlockSpec((RT, d), lambda i: (i, 0))
    return _call(body, name="merge_fwd", grid=(seq // RT,),
                 in_specs=[pl.BlockSpec((RT, d), lambda i: (i + off, ca)), pl.BlockSpec((RT, d), lambda i: (i + off, cb)), zs, zs],
                 out_specs=zs, out_shape=_sds((seq, d), BF16), sem=("parallel",))(proj, proj, za, zb)


def _merge_bwd(proj, za, zb, dmerged):
    seq, d = za.shape
    off = FRONT // RT
    ca, cb = 8 * HW // d, 8 * HW // d + 1
    nt = (FRONT + seq) // RT

    def body(ga_ref, gb_ref, za_ref, zb_ref, dm_ref, dza_ref, dzb_ref, dg_ref):
        i = pl.program_id(0)

        @pl.when(i < off)
        def _():
            dg_ref[...] = jnp.zeros((RT, 2 * d), BF16)

        @pl.when(i >= off)
        def _():
            sa, sb, dm = _sigmoid(ga_ref[...]), _sigmoid(gb_ref[...]), dm_ref[...]
            dza_ref[...] = (dm * sa).astype(BF16)
            dzb_ref[...] = (dm * sb).astype(BF16)
            dg_ref[:, 0:d] = (dm * za_ref[...] * sa * (1.0 - sa)).astype(BF16)
            dg_ref[:, d:2 * d] = (dm * zb_ref[...] * sb * (1.0 - sb)).astype(BF16)

    rs = pl.BlockSpec((RT, d), lambda i: (jnp.maximum(i - off, 0), 0))
    return _call(body, name="merge_bwd", grid=(nt,),
                 in_specs=[pl.BlockSpec((RT, d), lambda i: (i, ca)), pl.BlockSpec((RT, d), lambda i: (i, cb)), rs, rs, rs],
                 out_specs=[rs, rs, pl.BlockSpec((RT, 2 * d), lambda i: (i, 0))],
                 out_shape=[_sds((seq, d), BF16), _sds((seq, d), BF16), _sds((FRONT + seq, 2 * d), BF16)],
                 sem=("arbitrary",))(proj, proj, za, zb, dmerged)


def _resid_norm_fwd(x, mix, w):
    seq, d = x.shape

    def body(x_ref, m_ref, w_ref, h_ref, n_ref):
        h = x_ref[...] + m_ref[...]
        h_ref[...] = h
        r = lax.rsqrt(jnp.mean(h * h, axis=-1, keepdims=True) + EPS)
        n_ref[...] = (h * r * w_ref[...]).astype(BF16)

    rs = pl.BlockSpec((RT, d), lambda i: (i, 0))
    return _call(body, name="resid_norm_fwd", grid=(seq // RT,),
                 in_specs=[rs, rs, pl.BlockSpec((1, d), lambda i: (0, 0))], out_specs=[rs, rs],
                 out_shape=[_sds((seq, d), F32), _sds((seq, d), BF16)], sem=("parallel",))(x, mix, w)


def _resid_norm_bwd(h1, w, dn2, dh2):
    seq, d = h1.shape

    def body(h_ref, w_ref, g_ref, r_ref, o_ref, ob_ref, dw_ref):
        i = pl.program_id(0)
        h, g = h_ref[...], g_ref[...]
        r = lax.rsqrt(jnp.mean(h * h, axis=-1, keepdims=True) + EPS)
        hh = h * r
        gh = g * w_ref[...]
        dh = r_ref[...] + r * (gh - hh * jnp.mean(gh * hh, axis=-1, keepdims=True))
        o_ref[...] = dh
        ob_ref[...] = dh.astype(BF16)
        dw = jnp.sum(g * hh, axis=0, keepdims=True)

        @pl.when(i == 0)
        def _():
            dw_ref[...] = dw

        @pl.when(i > 0)
        def _():
            dw_ref[...] += dw

    rs = pl.BlockSpec((RT, d), lambda i: (i, 0))
    ws = pl.BlockSpec((1, d), lambda i: (0, 0))
    return _call(body, name="resid_norm_bwd", grid=(seq // RT,), in_specs=[rs, ws, rs, rs], out_specs=[rs, rs, ws],
                 out_shape=[_sds((seq, d), F32), _sds((seq, d), BF16), _sds((1, d), F32)], sem=("arbitrary",))(h1, w, dn2, dh2)


def _swiglu_fwd(gu):
    seq, f2 = gu.shape
    ff = f2 // 2
    tc = 512
    nb = ff // tc

    def body(g_ref, u_ref, o_ref):
        o_ref[...] = (_silu(g_ref[...]) * u_ref[...]).astype(BF16)

    return _call(body, name="swiglu_fwd", grid=(seq // RT, nb),
                 in_specs=[pl.BlockSpec((RT, tc), lambda i, j: (i, j)), pl.BlockSpec((RT, tc), lambda i, j: (i, j + nb))],
                 out_specs=pl.BlockSpec((RT, tc), lambda i, j: (i, j)), out_shape=_sds((seq, ff), BF16),
                 sem=("parallel", "parallel"))(gu, gu)


def _swiglu_bwd(gu, dact):
    seq, f2 = gu.shape
    ff = f2 // 2
    tc = 512
    nb = ff // tc

    def body(g_ref, u_ref, d_ref, dg_ref, du_ref):
        g, d = g_ref[...], d_ref[...]
        dg_ref[...] = (d * u_ref[...] * _dsilu(g)).astype(BF16)
        du_ref[...] = (d * _silu(g)).astype(BF16)

    bs = pl.BlockSpec((RT, tc), lambda i, j: (i, j))
    dgate, dup = _call(body, name="swiglu_bwd", grid=(seq // RT, nb),
                       in_specs=[bs, pl.BlockSpec((RT, tc), lambda i, j: (i, j + nb)), bs], out_specs=[bs, bs],
                       out_shape=[_sds((seq, ff), BF16), _sds((seq, ff), BF16)], sem=("parallel", "parallel"))(gu, gu, dact)
    return jnp.concatenate([dgate, dup], axis=1)


def _loss_head(h1, f, w, tgt):
    seq, d = h1.shape
    nt = seq // RT

    def body(h_ref, f_ref, w_ref, t_ref, l_ref, dh_ref, dhb_ref, dw_ref):
        i = pl.program_id(0)
        h = h_ref[...] + f_ref[...]
        r = lax.rsqrt(jnp.mean(h * h, axis=-1, keepdims=True) + EPS)
        hh = h * r
        err = hh * w_ref[...] - t_ref[...]
        l_ref[...] = jnp.full((8, 128), 0.5 * jnp.sum(jnp.mean(err * err, axis=-1, keepdims=True)), F32)
        dy = err * (1.0 / d)
        gh = dy * w_ref[...]
        dh = r * (gh - hh * jnp.mean(gh * hh, axis=-1, keepdims=True))
        dh_ref[...] = dh
        dhb_ref[...] = dh.astype(BF16)
        dw = jnp.sum(dy * hh, axis=0, keepdims=True)

        @pl.when(i == 0)
        def _():
            dw_ref[...] = dw

        @pl.when(i > 0)
        def _():
            dw_ref[...] += dw

    rs = pl.BlockSpec((RT, d), lambda i: (i, 0))
    ws = pl.BlockSpec((1, d), lambda i: (0, 0))
    return _call(body, name="loss_head", grid=(nt,), in_specs=[rs, rs, ws, rs],
                 out_specs=[pl.BlockSpec((8, 128), lambda i: (i, 0)), rs, rs, ws],
                 out_shape=[_sds((nt * 8, 128), F32), _sds((seq, d), F32), _sds((seq, d), BF16), _sds((1, d), F32)],
                 sem=("arbitrary",))(h1, f, w, tgt)


def _sum_tiles(lt):
    n = lt.shape[0]

    def body(l_ref, o_ref):
        v = l_ref[...]
        r = lax.broadcasted_iota(jnp.int32, v.shape, 0)
        c = lax.broadcasted_iota(jnp.int32, v.shape, 1)
        o_ref[...] = jnp.sum(jnp.where((r % 8 == 0) & (c == 0), v, 0.0), keepdims=True)

    return _call(body, name="loss_sum", grid=(1,), in_specs=[pl.BlockSpec((n, 128), lambda i: (0, 0))],
                 out_specs=pl.BlockSpec((1, 1), lambda i: (0, 0)), out_shape=_sds((1, 1), F32))(lt)


def _gated_norm_fwd(o, g, nw):
    r = lax.rsqrt(jnp.mean(o * o, axis=-1, keepdims=True) + EPS)
    return o * r * nw * _silu(g)


def _gated_norm_bwd(o, g, nw, dout):
    r = lax.rsqrt(jnp.mean(o * o, axis=-1, keepdims=True) + EPS)
    oh = o * r
    don = dout * _silu(g)
    dg = dout * (oh * nw) * _dsilu(g)
    dnw = jnp.sum(don * oh, axis=0, keepdims=True)
    doh = don * nw
    return r * (doh - oh * jnp.mean(doh * oh, axis=-1, keepdims=True)), dg, dnw


def _hg_gates(fs, lbl):
    l0, l1 = lbl[0:1, :], lbl[1:2, :]
    m = jnp.maximum(l0, l1)
    e0, e1 = jnp.exp(l0 - m), jnp.exp(l1 - m)
    lb = e0 / (e0 + e1)
    sig = _sigmoid(fs)
    f = lb + (1.0 - lb) * sig
    return lb, sig, f, jnp.log(f), (1.0 - lb) * _sigmoid(-fs)


def _cumsum(w):
    row = lax.broadcasted_iota(jnp.int32, w.shape, 0)
    s = 1
    while s < CH:
        w = w + jnp.where(row >= s, pltpu.roll(w, s, 0), 0.0)
        s *= 2
    return w


def _rcumsum(w):
    row = lax.broadcasted_iota(jnp.int32, w.shape, 0)
    s = 1
    while s < CH:
        w = w + jnp.where(row < CH - s, pltpu.roll(w, CH - s, 0), 0.0)
        s *= 2
    return w


def _decay_blocks(q, k, b, p_ref):
    p_ref[...] = jnp.zeros((CH, CH), F32)
    m16 = _tri(SUB, "incl")
    for I in range(CH // SUB):
        s0 = I * SUB
        bI, qI, kI = b[s0:s0 + SUB], q[s0:s0 + SUB], k[s0:s0 + SUB]
        dec = jnp.exp(jnp.minimum(bI[:, None, :] - bI[None, :, :], 0.0))
        pii = jnp.sum(qI[:, None, :] * kI[None, :, :] * dec, axis=-1)
        p_ref[s0:s0 + SUB, s0:s0 + SUB] = jnp.where(m16, pii, 0.0)
        if I > 0:
            rI = b[s0 - 1:s0]
            qs = qI * jnp.exp(bI - rI)
            ks = k[0:s0] * jnp.exp(rI - b[0:s0])
            p_ref[s0:s0 + SUB, 0:s0] = _dot(qs, ks, NT)


HB = 4
NHB = NH // HB
OFF = FRONT // RT


def _head_specs(rev=None):
    row = (lambda i: i) if rev is None else rev
    col = lambda g: pl.BlockSpec((RT, HB * DH), lambda h, i: (row(i), g * NHB + h))
    full = pl.BlockSpec((RT, HB * DH), lambda h, i: (row(i), h))
    real = pl.BlockSpec((RT, HB * DH), lambda h, i: (jnp.maximum(row(i) - OFF, 0), h))
    state = lambda cpt: pl.BlockSpec((HB, cpt, DH, DH), lambda h, i: (h, row(i), 0, 0))
    scal = pl.BlockSpec((HB, RT, DH), lambda h, i: (h, row(i), 0))
    return col, full, real, state, scal


def _hgrn2_fwd(proj, lb_logits, nw):
    tp = proj.shape[0]
    nt, cpt = tp // RT, RT // CH

    def body(q_ref, f_ref, i_ref, g_ref, lbl_ref, nw_ref, og_ref, or_ref, st_ref, s_ref, p_ref):
        @pl.when(pl.program_id(1) == 0)
        def _():
            s_ref[...] = jnp.zeros((HB, DH, DH), F32)

        def chunk(c, carry):
            rows = pl.ds(pl.multiple_of(c * CH, CH), CH)
            for hh in range(HB):
                cols = slice(hh * DH, (hh + 1) * DH)
                _, _, _, w, k = _hg_gates(f_ref[rows, cols], lbl_ref[:, cols])
                q, v = _silu(q_ref[rows, cols]), i_ref[rows, cols]
                b = _cumsum(w)
                st = s_ref[hh]
                st_ref[hh, c] = st
                _decay_blocks(q, k, b, p_ref.at[hh])
                o = _dot(q * jnp.exp(b), st, NT) + _dot(p_ref[hh], v)
                bl = b[CH - 1:CH]
                s_ref[hh] = st * jnp.exp(bl) + _dot(v, k * jnp.exp(bl - b), TN)
                or_ref[rows, cols] = o
                og_ref[rows, cols] = _gated_norm_fwd(o, g_ref[rows, cols], nw_ref[...]).astype(BF16)
            return carry

        lax.fori_loop(0, cpt, chunk, 0)

    col, full, real, state, _ = _head_specs()
    return _call(body, name="hgrn2_fwd", grid=(NHB, nt),
                 in_specs=[col(0), col(1), col(2), col(3), pl.BlockSpec((2, HB * DH), lambda h, i: (0, h)),
                           pl.BlockSpec((1, DH), lambda h, i: (0, 0))],
                 out_specs=[real, full, state(cpt)],
                 out_shape=[_sds((tp - FRONT, HW), BF16), _sds((tp, HW), F32), _sds((NH, tp // CH, DH, DH), F32)],
                 scratch=[pltpu.VMEM((HB, DH, DH), F32), pltpu.VMEM((HB, CH, CH), F32)],
                 sem=("parallel", "arbitrary"))(proj, proj, proj, proj, lb_logits, nw)


def _hgrn2_bwd(proj, lb_logits, nw, o_raw, states, dog, side=None):
    tp = proj.shape[0]
    nt, cpt = tp // RT, RT // CH

    def body(q_ref, f_ref, i_ref, g_ref, lbl_ref, nw_ref, or_ref, st_ref, dog_ref,
             dq_ref, df_ref, di_ref, dg_ref, dl_ref, dnw_ref, ds_ref, p_ref, dk_ref, dqa_ref, do_ref):
        step = pl.program_id(1)

        @pl.when(step == 0)
        def _():
            ds_ref[...] = jnp.zeros((HB, DH, DH), F32)
            dl_ref[...] = jnp.zeros((2, HB * DH), F32)

        @pl.when((step == 0) & (pl.program_id(0) == 0))
        def _():
            dnw_ref[...] = jnp.zeros((1, DH), F32)

        front = nt - 1 - step < OFF
        for hh in range(HB):
            cols = slice(hh * DH, (hh + 1) * DH)
            dog_t = jnp.where(front, 0.0, dog_ref[:, cols])
            do_t, dg_t, dnw = _gated_norm_bwd(or_ref[:, cols], g_ref[:, cols], nw_ref[...], dog_t)
            do_ref[:, cols] = do_t
            dg_ref[:, cols] = dg_t.astype(BF16)
            dnw_ref[...] += dnw
        tril = _tri(CH, "incl")
        m16 = _tri(SUB, "incl")

        def chunk(cc, carry):
            c = cpt - 1 - cc
            rows = pl.ds(pl.multiple_of(c * CH, CH), CH)
            for hh in range(HB):
                cols = slice(hh * DH, (hh + 1) * DH)
                fs = f_ref[rows, cols]
                lb, sig, f, w, k = _hg_gates(fs, lbl_ref[:, cols])
                hq = q_ref[rows, cols]
                q, v, do = _silu(hq), i_ref[rows, cols], do_ref[rows, cols]
                b = _cumsum(w)
                bl = b[CH - 1:CH]
                eb = jnp.exp(b)
                qs, kd = q * eb, k * jnp.exp(bl - b)
                st, dst = st_ref[hh, c], ds_ref[hh]
                _decay_blocks(q, k, b, p_ref.at[hh])
                dv = _dot(p_ref[hh], do, TN) + _dot(kd, dst, NT)
                dp = jnp.where(tril, _dotx(do, v, NT), 0.0)
                dqa, dka = dqa_ref.at[hh], dk_ref.at[hh]
                dqa[...] = eb * _dotx(do, st)
                dka[...] = jnp.exp(bl - b) * _dotx(v, dst)
                for I in range(CH // SUB):
                    s0 = I * SUB
                    bI, qI, kI = b[s0:s0 + SUB], q[s0:s0 + SUB], k[s0:s0 + SUB]
                    dec = jnp.exp(jnp.minimum(bI[:, None, :] - bI[None, :, :], 0.0))
                    dpii = jnp.where(m16, dp[s0:s0 + SUB, s0:s0 + SUB], 0.0)[:, :, None] * dec
                    dqa[s0:s0 + SUB, :] += jnp.sum(dpii * kI[None, :, :], axis=1)
                    dka[s0:s0 + SUB, :] += jnp.sum(dpii * qI[:, None, :], axis=0)
                    if I > 0:
                        rI = b[s0 - 1:s0]
                        eq, ek = jnp.exp(bI - rI), jnp.exp(rI - b[0:s0])
                        dpij = dp[s0:s0 + SUB, 0:s0]
                        dqa[s0:s0 + SUB, :] += eq * _dotx(dpij, k[0:s0] * ek)
                        dka[0:s0, :] += ek * _dotx(dpij, qI * eq, TN)
                dq, dk = dqa[...], dka[...]
                st_end = st * jnp.exp(bl) + _dotx(v, kd, TN)
                dw = _rcumsum(q * dq - k * dk) + jnp.sum(dst * st_end, axis=0, keepdims=True)
                ds_ref[hh] = dst * jnp.exp(bl) + _dotx(do, qs, TN)
                one_m = 1.0 - sig
                dq_ref[rows, cols] = (dq * _dsilu(hq)).astype(BF16)
                df_ref[rows, cols] = ((dw / f - dk) * (1.0 - lb) * sig * one_m).astype(BF16)
                di_ref[rows, cols] = dv.astype(BF16)
                dl_ref[0:1, cols] += jnp.sum((dw / f - dk) * one_m, axis=0, keepdims=True)
            return carry

        lax.fori_loop(0, cpt, chunk, 0)

        @pl.when(step == nt - 1)
        def _():
            lbl = lbl_ref[...]
            l0, l1 = lbl[0:1, :], lbl[1:2, :]
            m = jnp.maximum(l0, l1)
            e0, e1 = jnp.exp(l0 - m), jnp.exp(l1 - m)
            p0 = e0 / (e0 + e1)
            dl0 = dl_ref[0:1, :] * p0 * (1.0 - p0)
            dl_ref[0:1, :] = dl0
            dl_ref[1:2, :] = -dl0

    col, full, real, state, _ = _head_specs(lambda i: nt - 1 - i)
    lbs = pl.BlockSpec((2, HB * DH), lambda h, i: (0, h))
    return _call(body, name="hgrn2_bwd", grid=(NHB, nt),
                 in_specs=[col(0), col(1), col(2), col(3), lbs, pl.BlockSpec((1, DH), lambda h, i: (0, 0)), full,
                           state(cpt), real],
                 out_specs=[full, full, full, full, lbs, pl.BlockSpec((1, DH), lambda h, i: (0, 0))],
                 out_shape=[_sds((tp, HW), BF16)] * 4 + [_sds((2, HW), F32), _sds((1, DH), F32)],
                 scratch=[pltpu.VMEM((HB, DH, DH), F32), pltpu.VMEM((HB, CH, CH), F32), pltpu.VMEM((HB, CH, DH), F32),
                          pltpu.VMEM((HB, CH, DH), F32), pltpu.VMEM((RT, HB * DH), F32)],
                 sem=("arbitrary", "arbitrary"), side=side)(proj, proj, proj, proj, lb_logits, nw, o_raw, states, dog)


GQ0 = 4 * HW
CW = 3 * HW


def _gd_scalars(ab, alog, dtb):
    g = -jnp.exp(alog) * jax.nn.softplus(ab + dtb)
    return g, _sigmoid(ab)


def _conv_ext_specs(row_of):
    main = [pl.BlockSpec((RT, HW), lambda i, g=g: (row_of(i), GQ0 // HW + g)) for g in range(3)]
    prev = [pl.BlockSpec((8, HW), lambda i, g=g: (jnp.maximum(row_of(i) * (RT // 8) - 1, 0), GQ0 // HW + g)) for g in range(3)]
    return main + prev


def _conv_fill(ext_ref, xs, xps, first):
    for g in range(3):
        ext_ref[0:8, g * HW:(g + 1) * HW] = jnp.where(first, 0.0, xps[g][...])
        ext_ref[8:8 + RT, g * HW:(g + 1) * HW] = xs[g][...]


def _conv_apply(ext_ref, cw):
    y = cw[CONV_K - 1:CONV_K, :] * ext_ref[pl.ds(8, RT), :]
    for s in range(1, CONV_K):
        y += cw[CONV_K - 1 - s:CONV_K - s, :] * ext_ref[pl.ds(8 - s, RT), :]
    return y


def _gdn_prep_fwd(proj, pab, conv_w, alog, dtb):
    tp = proj.shape[0]
    nt = tp // RT

    def body(x0, x1, x2, p0, p1, p2, ab_ref, cw_ref, al_ref, dt_ref, q_ref, k_ref, v_ref, g_ref, b_ref, ext_ref):
        _conv_fill(ext_ref, (x0, x1, x2), (p0, p1, p2), pl.program_id(0) == 0)
        a = _silu(_conv_apply(ext_ref, cw_ref[...]))
        for h in range(NH):
            for part, ref, sc in ((0, q_ref, DH ** -0.5), (1, k_ref, 1.0)):
                seg = a[:, part * HW + h * DH:part * HW + (h + 1) * DH]
                ref[:, h * DH:(h + 1) * DH] = seg * (lax.rsqrt(jnp.sum(seg * seg, axis=-1, keepdims=True) + EPS) * sc)
        v_ref[...] = a[:, 2 * HW:3 * HW]
        g, beta = _gd_scalars(ab_ref[...], al_ref[...], dt_ref[...])
        for h in range(NH):
            g_ref[h] = jnp.broadcast_to(g[:, h:h + 1], (RT, DH))
            b_ref[h] = jnp.broadcast_to(beta[:, NH + h:NH + h + 1], (RT, DH))

    hs = pl.BlockSpec((RT, HW), lambda i: (i, 0))
    sc = pl.BlockSpec((NH, RT, DH), lambda i: (0, i, 0))
    one = pl.BlockSpec((1, DH), lambda i: (0, 0))
    return _call(body, name="gdn_prep_fwd", grid=(nt,),
                 in_specs=_conv_ext_specs(lambda i: i) + [pl.BlockSpec((RT, DH), lambda i: (i, 0)),
                                                           pl.BlockSpec((CONV_K, CW), lambda i: (0, 0)), one, one],
                 out_specs=[hs, hs, hs, sc, sc],
                 out_shape=[_sds((tp, HW), F32)] * 3 + [_sds((NH, tp, DH), F32)] * 2,
                 scratch=[pltpu.VMEM((RT + 8, CW), F32)], sem=("parallel",))(*([proj] * 6), pab, conv_w, alog, dtb)


def _gdn_prep_bwd(proj, pab, conv_w, alog, dtb, dq, dk, dv, dgb, dbb):
    tp = proj.shape[0]
    nt = tp // RT

    def body(x0, x1, x2, p0, p1, p2, ab_ref, cw_ref, al_ref, dt_ref, dq_ref, dk_ref, dv_ref, dg_ref, db_ref,
             dx_ref, dab_ref, dcw_ref, dal_ref, ddt_ref, ext_ref, dy_ref):
        step = pl.program_id(0)
        i = nt - 1 - step

        @pl.when(step == 0)
        def _():
            dy_ref[RT:RT + 8, :] = jnp.zeros((8, CW), F32)
            dcw_ref[...] = jnp.zeros((8, CW), F32)
            dal_ref[...] = jnp.zeros((1, DH), F32)
            ddt_ref[...] = jnp.zeros((1, DH), F32)

        _conv_fill(ext_ref, (x0, x1, x2), (p0, p1, p2), i == 0)
        cw = cw_ref[...]
        y = _conv_apply(ext_ref, cw)
        a = _silu(y)
        dsl = _dsilu(y)
        for h in range(NH):
            for part, ref, sc in ((0, dq_ref, DH ** -0.5), (1, dk_ref, 1.0)):
                lo = part * HW + h * DH
                seg = a[:, lo:lo + DH]
                r = lax.rsqrt(jnp.sum(seg * seg, axis=-1, keepdims=True) + EPS)
                xh = seg * r
                dxh = ref[:, h * DH:(h + 1) * DH] * sc
                dy_ref[0:RT, lo:lo + DH] = r * (dxh - xh * jnp.sum(dxh * xh, axis=-1, keepdims=True)) * dsl[:, lo:lo + DH]
        dy_ref[0:RT, 2 * HW:3 * HW] = dv_ref[...] * dsl[:, 2 * HW:3 * HW]
        dy = dy_ref[0:RT, :]
        dx = cw[CONV_K - 1:CONV_K, :] * dy
        dcw_ref[CONV_K - 1:CONV_K, :] += jnp.sum(dy * ext_ref[pl.ds(8, RT), :], axis=0, keepdims=True)
        for s in range(1, CONV_K):
            dx += cw[CONV_K - 1 - s:CONV_K - s, :] * dy_ref[pl.ds(s, RT), :]
            dcw_ref[CONV_K - 1 - s:CONV_K - s, :] += jnp.sum(dy * ext_ref[pl.ds(8 - s, RT), :], axis=0, keepdims=True)
        dx_ref[...] = dx.astype(BF16)
        dy_ref[RT:RT + 8, :] = dy[0:8, :]
        ab = ab_ref[...]
        g, beta = _gd_scalars(ab, al_ref[...], dt_ref[...])
        lane = lax.broadcasted_iota(jnp.int32, (RT, DH), 1)
        dgl = jnp.zeros((RT, DH), F32)
        dbl = jnp.zeros((RT, DH), F32)
        for h in range(NH):
            dgl = jnp.where(lane == h, dg_ref[h], dgl)
            dbl = jnp.where(lane == NH + h, db_ref[h], dbl)
        dsp = dgl * (-jnp.exp(al_ref[...])) * _sigmoid(ab + dt_ref[...])
        dab_ref[...] = (dsp + dbl * beta * (1.0 - beta)).astype(BF16)
        ddt_ref[...] += jnp.sum(dsp, axis=0, keepdims=True)
        dal_ref[...] += jnp.sum(dgl * g, axis=0, keepdims=True)

    hs = pl.BlockSpec((RT, HW), lambda s: (nt - 1 - s, 0))
    sc = pl.BlockSpec((NH, RT, DH), lambda s: (0, nt - 1 - s, 0))
    one = pl.BlockSpec((1, DH), lambda s: (0, 0))
    xs = pl.BlockSpec((RT, CW), lambda s: (nt - 1 - s, 0))
    return _call(body, name="gdn_prep_bwd", grid=(nt,),
                 in_specs=_conv_ext_specs(lambda s: nt - 1 - s) + [
                     pl.BlockSpec((RT, DH), lambda s: (nt - 1 - s, 0)), pl.BlockSpec((CONV_K, CW), lambda s: (0, 0)),
                     one, one, hs, hs, hs, sc, sc],
                 out_specs=[xs, pl.BlockSpec((RT, DH), lambda s: (nt - 1 - s, 0)), pl.BlockSpec((8, CW), lambda s: (0, 0)), one, one],
                 out_shape=[_sds((tp, CW), BF16), _sds((tp, DH), BF16), _sds((8, CW), F32), _sds((1, DH), F32), _sds((1, DH), F32)],
                 scratch=[pltpu.VMEM((RT + 8, CW), F32), pltpu.VMEM((RT + 8, CW), F32)],
                 sem=("arbitrary",))(*([proj] * 6), pab, conv_w, alog, dtb, dq, dk, dv, dgb, dbb)


def _unit_lower_inverse(a):
    r = lax.broadcasted_iota(jnp.int32, (CH, CH), 0)
    c = lax.broadcasted_iota(jnp.int32, (CH, CH), 1)
    blk_of = lambda t, size: lax.shift_right_logical(t, int(math.log2(size)))
    a16 = jnp.where(blk_of(r, SUB) == blk_of(c, SUB), a, 0.0)
    x = (r == c).astype(F32) - a16
    p = a16
    for _ in range(3):
        p = _dot(p, p)
        x = x + _dot(x, p)
    for blk in (2 * SUB, 4 * SUB):
        off = jnp.where((blk_of(r, blk) == blk_of(c, blk)) & (blk_of(r, blk // 2) != blk_of(c, blk // 2)), a, 0.0)
        x = x - _dot(x, _dot(off, x))
    return x


def _gdn_chunk_common(q, k, v, gl, bt):
    gc = _cumsum(gl)
    e = jnp.exp(gc)
    rel = jnp.exp(jnp.minimum(gc[:, 0:CH] - gc.T[0:CH, :], 0.0))
    kb = bt * k
    a = jnp.where(_tri(CH, "strict"), bt[:, 0:CH] * _dot(k, k, NT) * rel, 0.0)
    x = _unit_lower_inverse(a)
    w = _dot(x, kb * e)
    u = _dot(x, bt * v)
    attn = jnp.where(_tri(CH, "incl"), _dot(q, k, NT) * rel, 0.0)
    return gc, e, rel, kb, a, x, w, u, attn


def _gdn_fwd(q, k, v, gb, bb, proj, nw):
    tp = q.shape[0]
    nt, cpt = tp // RT, RT // CH

    def body(q_ref, k_ref, v_ref, g_ref, b_ref, z_ref, nw_ref, og_ref, or_ref, st_ref, s_ref):
        @pl.when(pl.program_id(1) == 0)
        def _():
            s_ref[...] = jnp.zeros((HB, DH, DH), F32)

        def chunk(c, carry):
            rows = pl.ds(pl.multiple_of(c * CH, CH), CH)
            for hh in range(HB):
                cols = slice(hh * DH, (hh + 1) * DH)
                qc, kc, vc = q_ref[rows, cols], k_ref[rows, cols], v_ref[rows, cols]
                gc, e, rel, kb, a, x, w, u, attn = _gdn_chunk_common(qc, kc, vc, g_ref[hh, rows, :], b_ref[hh, rows, :])
                s = s_ref[hh]
                st_ref[hh, c] = s
                vn = u - _dot(w, s)
                o = _dot(qc * e, s) + _dot(attn, vn)
                gl = gc[CH - 1:CH]
                s_ref[hh] = s * jnp.exp(gl) + _dot(kc * jnp.exp(gl - gc), vn, TN)
                or_ref[rows, cols] = o
                og_ref[rows, cols] = _gated_norm_fwd(o, z_ref[rows, cols], nw_ref[...]).astype(BF16)
            return carry

        lax.fori_loop(0, cpt, chunk, 0)

    col, full, real, state, scal = _head_specs()
    return _call(body, name="gdn_fwd", grid=(NHB, nt),
                 in_specs=[full, full, full, scal, scal, col(7), pl.BlockSpec((1, DH), lambda h, i: (0, 0))],
                 out_specs=[real, full, state(cpt)],
                 out_shape=[_sds((tp - FRONT, HW), BF16), _sds((tp, HW), F32), _sds((NH, tp // CH, DH, DH), F32)],
                 scratch=[pltpu.VMEM((HB, DH, DH), F32)], sem=("parallel", "arbitrary"))(q, k, v, gb, bb, proj, nw)


def _gdn_bwd(q, k, v, gb, bb, proj, nw, o_raw, states, dog):
    tp = q.shape[0]
    nt, cpt = tp // RT, RT // CH

    def body(q_ref, k_ref, v_ref, g_ref, b_ref, z_ref, nw_ref, or_ref, st_ref, dog_ref,
             dq_ref, dk_ref, dv_ref, dg_ref, db_ref, dz_ref, dnw_ref, ds_ref, do_ref):
        step = pl.program_id(1)

        @pl.when(step == 0)
        def _():
            ds_ref[...] = jnp.zeros((HB, DH, DH), F32)

        @pl.when((step == 0) & (pl.program_id(0) == 0))
        def _():
            dnw_ref[...] = jnp.zeros((1, DH), F32)

        front = nt - 1 - step < OFF
        for hh in range(HB):
            cols = slice(hh * DH, (hh + 1) * DH)
            dog_t = jnp.where(front, 0.0, dog_ref[:, cols])
            do_t, dz_t, dnw = _gated_norm_bwd(or_ref[:, cols], z_ref[:, cols], nw_ref[...], dog_t)
            do_ref[:, cols] = do_t
            dz_ref[:, cols] = dz_t.astype(BF16)
            dnw_ref[...] += dnw
        incl, strict = _tri(CH, "incl"), _tri(CH, "strict")
        ones = jnp.ones((CH, DH), F32)
        rowid = lax.broadcasted_iota(jnp.int32, (CH, DH), 0)

        def rsum(t):
            return jnp.sum(t, axis=-1, keepdims=True)

        def chunk(cc, carry):
            c = cpt - 1 - cc
            rows = pl.ds(pl.multiple_of(c * CH, CH), CH)
            for hh in range(HB):
                cols = slice(hh * DH, (hh + 1) * DH)
                qc, kc, vc, do = q_ref[rows, cols], k_ref[rows, cols], v_ref[rows, cols], do_ref[rows, cols]
                bt = b_ref[hh, rows, :]
                gc, e, rel, kb, a, x, w, u, attn = _gdn_chunk_common(qc, kc, vc, g_ref[hh, rows, :], bt)
                s, dsn = st_ref[hh, c], ds_ref[hh]
                gl = gc[CH - 1:CH]
                el = jnp.exp(gl)
                cdec = jnp.exp(gl - gc)
                vn = u - _dot(w, s)
                dvn = _dot(attn, do, TN) + _dot(kc * cdec, dsn)
                dattn = jnp.where(incl, _dot(do, vn, NT), 0.0)
                dos = _dot(do, s, NT)
                vds = _dot(vn, dsn, NT)
                dar = dattn * rel
                dq = _dot(dar, kc) + e * dos
                dk = _dot(dar, qc, TN) + cdec * vds
                dgc = rsum(qc * e * dos)
                dc = cdec[:, 0:1] * rsum(kc * vds)
                dgc = dgc - dc
                dglast = jnp.sum(dc, axis=0, keepdims=True) + el[:, 0:1] * jnp.sum(rsum(dsn * s), axis=0, keepdims=True)
                ds_ref[hh] = dsn * el + _dot(qc * e, do, TN) - _dot(w, dvn, TN)
                dw = -_dot(dvn, s, NT)
                drw = _dot(x, dw, TN)
                dru = _dot(x, dvn, TN)
                da = -jnp.where(strict, _dot(drw, w, NT) + _dot(dru, u, NT), 0.0)
                dar2 = da * rel
                dkb = _dot(dar2, kc)
                rwk = rsum(drw * kc)
                dk = dk + _dot(dar2, kb, TN) + bt * dkb + (bt * e) * drw
                dbeta = rsum(dkb * kc) + e[:, 0:1] * rwk + rsum(dru * vc)
                dgc = dgc + bt[:, 0:1] * e[:, 0:1] * rwk
                z = dattn * attn + da * a
                dgc = dgc + rsum(z) - _dotx(z, ones, TN)[:, 0:1]
                dgcb = jnp.broadcast_to(dgc, (CH, DH)) + jnp.where(rowid == CH - 1, jnp.broadcast_to(dglast, (CH, DH)), 0.0)
                dq_ref[rows, cols] = dq
                dk_ref[rows, cols] = dk
                dv_ref[rows, cols] = bt * dru
                dg_ref[hh, rows, :] = _rcumsum(dgcb)
                db_ref[hh, rows, :] = jnp.broadcast_to(dbeta, (CH, DH))
            return carry

        lax.fori_loop(0, cpt, chunk, 0)

    col, full, real, state, scal = _head_specs(lambda i: nt - 1 - i)
    one = pl.BlockSpec((1, DH), lambda h, i: (0, 0))
    return _call(body, name="gdn_bwd", grid=(NHB, nt),
                 in_specs=[full, full, full, scal, scal, col(7), one, full, state(cpt), real],
                 out_specs=[full, full, full, scal, scal, full, one],
                 out_shape=[_sds((tp, HW), F32)] * 3 + [_sds((NH, tp, DH), F32)] * 2 + [_sds((tp, HW), BF16), _sds((1, DH), F32)],
                 scratch=[pltpu.VMEM((HB, DH, DH), F32), pltpu.VMEM((RT, HB * DH), F32)],
                 sem=("arbitrary", "arbitrary"))(q, k, v, gb, bb, proj, nw, o_raw, states, dog)


MAIN_W = 8 * HW
AB_W = 2 * NH


def _split_w_in(w_in):
    main = jnp.concatenate([w_in[:, :MAIN_W], w_in[:, MAIN_W + AB_W:]], axis=1)
    ab = jnp.pad(w_in[:, MAIN_W:MAIN_W + AB_W], ((0, 0), (0, DH - AB_W)))
    return main, ab


def _pad_lanes(v):
    return jnp.pad(v, ((0, 0), (0, DH - v.shape[1])))


class _NoComm:
    def __init__(self, late):
        self.late = late

    def proj_side(self):
        return None

    def late_weights(self, side_outs):
        return self.late

    def ffn_grads_side(self, dw_ffn_in, dw_ffn_out):
        return None

    def ffn_grads_done(self, side_outs):
        pass


def _local_step(x, tgt, meta, lb_logits, mix_w, w_in, hg_nw, conv_w, a_log, dt_bias, gd_nw, ffn_nw, final_w, comm):
    w_main, w_ab = _split_w_in(w_in)
    alog, dtb = _pad_lanes(a_log), _pad_lanes(dt_bias)
    final_w = final_w.reshape(1, -1)
    rows4 = lambda t: t.reshape(4, t.shape[0] // 4, t.shape[1])
    xn = _rms1_fwd(x, meta, mix_w)
    side = comm.proj_side()
    proj = _mm(xn, w_main, "nn", F32, 768, 1024, 2048, "proj_main", n_outer=True, side=side)
    proj, landed = proj if side is not None else (proj, None)
    w_a, w_b, w_out, w_ffn_in, w_ffn_out = comm.late_weights(landed)
    pab = _mm(xn, w_ab, "nn", F32, 768, 128, 2048, "proj_ab")
    oa_g, oa_raw, st_a = _hgrn2_fwd(proj, lb_logits, hg_nw)
    q, k, v, gb, bb = _gdn_prep_fwd(proj, pab, conv_w, alog, dtb)
    ob_g, ob_raw, st_b = _gdn_fwd(q, k, v, gb, bb, proj, gd_nw)
    za = _mm(oa_g, w_a, "nn", F32, 1024, 512, 1024, "branch_a", n_outer=True)
    zb = _mm(ob_g, w_b, "nn", F32, 1024, 512, 1024, "branch_b", n_outer=True)
    merged = _merge_fwd(proj, za, zb)
    mix = _mm(merged, w_out, "nn", F32, 1024, 2048, 2048, "mix_out")
    h1, n2 = _resid_norm_fwd(x, mix, ffn_nw)
    gu = _mm(n2, w_ffn_in, "nn", F32, 1024, 1408, 2048, "ffn_in", n_outer=True)
    act = _swiglu_fwd(gu)
    f = _mm(act, w_ffn_out, "nn", F32, 1024, 2048, 512, "ffn_out")
    lt, dh2, dh2b, dfinal = _loss_head(h1, f, final_w, tgt)
    loss = _sum_tiles(lt)
    dact = _mm(dh2b, w_ffn_out, "nt", F32, 1024, 512, 2048, "d_act", n_outer=True)
    dw_ffn_out = rows4(_mm(act, dh2b, "tn", F32, 512, 2048, 1024, "dw_ffn_out"))
    dgu = _swiglu_bwd(gu, dact)
    dn2 = _mm(dgu, w_ffn_in, "nt", F32, 1024, 2048, 1408, "d_n2")
    dw_ffn_in = _mm(n2, dgu, "tn", F32, 1024, 1408, 1024, "dw_ffn_in", out_shards=4)
    side = comm.ffn_grads_side(dw_ffn_in, dw_ffn_out)
    dh1, dh1b, dffn_nw = _resid_norm_bwd(h1, ffn_nw, dn2, dh2)
    dmerged = _mm(dh1b, w_out, "nt", F32, 1024, 2048, 2048, "d_merged")
    dw_out = _mm(merged, dh1b, "tn", F32, 2048, 1024, 1024, "dw_out")
    dza, dzb, dgate = _merge_bwd(proj, za, zb, dmerged)
    doa = _mm(dza, w_a, "nt", F32, 1024, 1024, 512, "d_oa")
    dob = _mm(dzb, w_b, "nt", F32, 1024, 1024, 512, "d_ob")
    dw_a = _mm(oa_g, dza, "tn", F32, 1024, 512, 1024, "dw_branch_a", out_shards=4)
    dw_b = _mm(ob_g, dzb, "tn", F32, 1024, 512, 1024, "dw_branch_b", out_shards=4)
    hg = _hgrn2_bwd(proj, lb_logits, hg_nw, oa_raw, st_a, doa, side=side)
    if side is not None:
        hg, arrived = hg
        comm.ffn_grads_done(arrived)
    dhq, dhf, dhi, dhg, dlbl, dhg_nw = hg
    dq, dk, dv, dg, dbeta, dz, dgd_nw = _gdn_bwd(q, k, v, gb, bb, proj, gd_nw, ob_raw, st_b, dob)
    dx3, dab, dconv, dalog, ddtb = _gdn_prep_bwd(proj, pab, conv_w, alog, dtb, dq, dk, dv, dg, dbeta)
    dproj = jnp.concatenate([dhq, dhf, dhi, dhg, dx3, dz, dgate], axis=1)
    dxn = _mm(dproj, w_main, "nt", F32, 768, 2048, 1024, "d_xn")
    dxn = _mm(dab, w_ab, "nt", F32, 768, 2048, 128, "d_xn_ab", add=dxn)
    dw_main = _mm(xn, dproj, "tn", F32, 2048, 1024, 768, "dw_in_main")
    dw_ab = _mm(xn, dab, "tn", F32, 2048, 128, 768, "dw_in_ab")
    dx, dmeta, dmix_w = _rms1_bwd(x, meta, mix_w, dxn, dh1)
    d_model = dw_main.shape[0]
    dw_in = jnp.concatenate([dw_main[:, :MAIN_W], dw_ab[:, :AB_W], dw_main[:, MAIN_W:]], axis=1)
    dw_in = dw_in.reshape(d_model, 4, -1).transpose(1, 0, 2)
    grads = dict(meta_tokens=dmeta, lb_logits=dlbl, mix_norm_w=dmix_w, w_in=dw_in,
                 hg_norm_w=dhg_nw, gd_conv_w=dconv[:CONV_K], gd_a_log=dalog[:, :NH],
                 gd_dt_bias=ddtb[:, :NH], gd_norm_w=dgd_nw, w_branch_a=dw_a, w_branch_b=dw_b,
                 w_out=rows4(dw_out), ffn_norm_w=dffn_nw, w_ffn_in=dw_ffn_in, w_ffn_out=dw_ffn_out,
                 final_norm_w=dfinal.reshape(-1))
    return loss, dx, grads


def _adamw(g, w, m, v, name):
    rows, cols = g.shape
    tr = rows
    for cand in (128, 64, 32, 16, 8):
        if rows % cand == 0 and rows > cand:
            tr = cand
            break

    def body(g_ref, w_ref, m_ref, v_ref, d_ref, nm_ref, nv_ref):
        gg = g_ref[...]
        nm = ADAM_B1 * m_ref[...] + (1.0 - ADAM_B1) * gg
        nv = ADAM_B2 * v_ref[...] + (1.0 - ADAM_B2) * (gg * gg)
        m_hat = nm / (1.0 - ADAM_B1 ** ADAM_STEP)
        v_hat = nv / (1.0 - ADAM_B2 ** ADAM_STEP)
        d_ref[...] = -ADAM_LR * (m_hat / (jnp.sqrt(v_hat) + ADAM_EPS) + ADAM_WD * w_ref[...])
        nm_ref[...] = nm
        nv_ref[...] = nv

    bs = pl.BlockSpec((tr, cols), lambda i: (i, 0))
    return _call(body, name=name, grid=(rows // tr,), in_specs=[bs] * 4, out_specs=[bs] * 3,
                 out_shape=[_sds((rows, cols), F32)] * 3, sem=("parallel",))(g, w, m, v)


HBM = pl.BlockSpec(memory_space=pltpu.HBM)
MESH = pl.DeviceIdType.MESH


def _place():
    x, y, c = lax.axis_index("x"), lax.axis_index("y"), lax.axis_index("c")
    return x, y, c, [(1 - x, y), (x, 1 - y), (1 - x, 1 - y)]


def _comm_call(body, name, out_shape, n_in, scratch):
    return pl.pallas_call(body, name=name, out_shape=out_shape, in_specs=[HBM] * n_in,
                          out_specs=jax.tree.map(lambda _: HBM, out_shape), scratch_shapes=scratch)


def _half_rows(rows, c, tile):
    hh = rows // 2
    assert rows % 2 == 0 and hh % tile == 0, (rows, tile)
    return pl.ds(pl.multiple_of(c * hh, tile), hh)


def _gather_copies(w_refs, out_refs, sems):
    send_sems, recv_sems = sems
    x, y, c, chips = _place()
    s_me = 2 * x + y
    sends, recvs = [], []
    for k, (w_ref, out_ref) in enumerate(zip(w_refs, out_refs)):
        half = _half_rows(w_ref.shape[0], c, 16)
        for j, (cx, cy) in enumerate(chips):
            sem = dict(send_sem=send_sems.at[3 * k + j], recv_sem=recv_sems.at[3 * k + j], device_id=(cx, cy, c), device_id_type=MESH)
            sends.append(pltpu.make_async_remote_copy(src_ref=w_ref.at[half], dst_ref=out_ref.at[s_me, half], **sem))
            recvs.append(pltpu.make_async_remote_copy(src_ref=w_ref.at[half], dst_ref=out_ref.at[2 * cx + cy, half], **sem))
    return sends, recvs


def _gather_sems(n):
    return [pltpu.SemaphoreType.DMA((3 * n,)), pltpu.SemaphoreType.DMA((3 * n,))]


def _gather_start(w_refs, out_refs, sems):
    for cp in _gather_copies(w_refs, out_refs, sems)[0]:
        cp.start()


def _gather_wait(w_refs, out_refs, sems):
    sends, recvs = _gather_copies(w_refs, out_refs, sems)
    for cp in recvs:
        cp.wait_recv()
    for cp in sends:
        cp.wait_send()


def _gather_chips(shards):
    n = len(shards)

    def body(*refs):
        _gather_start(refs[:n], refs[n:2 * n], refs[2 * n:])
        _gather_wait(refs[:n], refs[n:2 * n], refs[2 * n:])

    return _comm_call(body, "gather_chips", [_sds((4,) + w.shape, w.dtype) for w in shards], n, _gather_sems(n))(*shards)


def _gather_side(shards):
    return _Side(shards, [_sds((4,) + w.shape, w.dtype) for w in shards], _gather_sems(len(shards)), _gather_start, _gather_wait)


def _forward_halves(outs, name):
    n = len(outs)

    def body(*refs):
        out_refs = refs[n:2 * n]
        send_sems, recv_sems = refs[2 * n:]
        x, y, c, chips = _place()
        cps = []
        for k in range(n):
            rows = out_refs[k].shape[1]
            half, other = _half_rows(rows, c, 16), _half_rows(rows, 1 - c, 16)
            for j, (cx, cy) in enumerate(chips):
                sem = dict(send_sem=send_sems.at[3 * k + j], recv_sem=recv_sems.at[3 * k + j], device_id=(x, y, 1 - c), device_id_type=MESH)
                landed = out_refs[k].at[2 * cx + cy, half]
                cps.append(pltpu.make_async_remote_copy(src_ref=landed, dst_ref=landed, **sem))
                cps[-1].start()
        for k in range(n):
            rows = out_refs[k].shape[1]
            half, other = _half_rows(rows, c, 16), _half_rows(rows, 1 - c, 16)
            for j, (cx, cy) in enumerate(chips):
                sem = dict(send_sem=send_sems.at[3 * k + j], recv_sem=recv_sems.at[3 * k + j], device_id=(x, y, 1 - c), device_id_type=MESH)
                pltpu.make_async_remote_copy(src_ref=out_refs[k].at[2 * cx + cy, half], dst_ref=out_refs[k].at[2 * cx + cy, other], **sem).wait_recv()
        for cp in cps:
            cp.wait_send()

    shapes = [_sds(o.shape, o.dtype) for o in outs]
    return pl.pallas_call(body, name=name, out_shape=shapes, in_specs=[HBM] * n, out_specs=[HBM] * n,
                          input_output_aliases={k: k for k in range(n)},
                          scratch_shapes=[pltpu.SemaphoreType.DMA((3 * n,)), pltpu.SemaphoreType.DMA((3 * n,))])(*outs)


def _swap_halves(gs, name):
    n = len(gs)

    def body(*refs):
        g_refs, out_refs = refs[:n], refs[n:2 * n]
        send_sems, recv_sems = refs[2 * n:]
        x, y, c, _ = _place()
        cps = []
        for k in range(n):
            other = _half_rows(g_refs[k].shape[1], 1 - c, 8)
            cps.append(pltpu.make_async_remote_copy(src_ref=g_refs[k].at[:, other, :], dst_ref=out_refs[k], send_sem=send_sems.at[k],
                                                    recv_sem=recv_sems.at[k], device_id=(x, y, 1 - c), device_id_type=MESH))
            cps[-1].start()
        for cp in cps:
            cp.wait()

    return _comm_call(body, name, [_sds((4, g.shape[1] // 2, g.shape[2]), g.dtype) for g in gs], n,
                      [pltpu.SemaphoreType.DMA((n,)), pltpu.SemaphoreType.DMA((n,))])(*gs)


def _row_tile(rows, row_bytes, budget=3 << 20):
    if rows * row_bytes <= budget:
        return rows
    return max(t for t in range(16, rows, 16) if rows % t == 0 and t * row_bytes <= budget)


def _add_half(g, got, c, name):
    _, rows, cols = g.shape
    hh = rows // 2
    tr = _row_tile(hh, cols * 4)
    nb = hh // tr

    def body(c_ref, a_ref, b_ref, o_ref):
        o_ref[...] = (a_ref[...] + b_ref[...]).astype(BF16)

    gs = pltpu.PrefetchScalarGridSpec(
        num_scalar_prefetch=1, grid=(4, nb),
        in_specs=[pl.BlockSpec((1, tr, cols), lambda s, i, c_ref: (s, c_ref[0] * nb + i, 0)),
                  pl.BlockSpec((1, tr, cols), lambda s, i, c_ref: (s, i, 0))],
        out_specs=pl.BlockSpec((1, tr, cols), lambda s, i, c_ref: (s, i, 0)))
    return pl.pallas_call(body, name=name, grid_spec=gs, out_shape=_sds((4, hh, cols), BF16),
                          compiler_params=pltpu.CompilerParams(dimension_semantics=("parallel", "parallel"),
                                                               vmem_limit_bytes=VMEM_LIMIT))(c, g, got)


def _scatter_copies(p_refs, out_refs, sems):
    send_sems, recv_sems = sems
    x, y, c, chips = _place()
    s_me = 2 * x + y
    cps = []
    for k, (p_ref, out_ref) in enumerate(zip(p_refs, out_refs)):
        for j, (cx, cy) in enumerate(chips):
            cps.append(pltpu.make_async_remote_copy(src_ref=p_ref.at[2 * cx + cy], dst_ref=out_ref.at[s_me],
                                                    send_sem=send_sems.at[3 * k + j], recv_sem=recv_sems.at[3 * k + j],
                                                    device_id=(cx, cy, c), device_id_type=MESH))
    return cps


def _scatter_start(p_refs, out_refs, sems):
    for cp in _scatter_copies(p_refs, out_refs, sems):
        cp.start()


def _scatter_wait(p_refs, out_refs, sems):
    for cp in _scatter_copies(p_refs, out_refs, sems):
        cp.wait()


def _scatter_chips(ps):
    n = len(ps)

    def body(*refs):
        _scatter_start(refs[:n], refs[n:2 * n], refs[2 * n:])
        _scatter_wait(refs[:n], refs[n:2 * n], refs[2 * n:])

    return _comm_call(body, "scatter_chips", [_sds(p_.shape, p_.dtype) for p_ in ps], n, _gather_sems(n))(*ps)


def _scatter_side(ps):
    return _Side(ps, [_sds(p_.shape, p_.dtype) for p_ in ps], _gather_sems(len(ps)), _scatter_start, _scatter_wait)


def _sum_slabs(b, name):
    n, h, wd = b.shape
    tr = _row_tile(h, n * wd * 4, 6 << 20)

    def body(b_ref, o_ref):
        acc = b_ref[0]
        for s in range(1, n):
            acc = acc + b_ref[s]
        o_ref[...] = acc

    return _call(body, name=name, grid=(h // tr,), in_specs=[pl.BlockSpec((n, tr, wd), lambda i: (0, i, 0))],
                 out_specs=pl.BlockSpec((tr, wd), lambda i: (i, 0)), out_shape=_sds((h, wd), F32), sem=("parallel",))(b)


def _sum_chips(arrived, own, name):
    n, h, wd = arrived.shape
    tr = _row_tile(h, n * wd * 2, 6 << 20)
    nb = h // tr
    my_chip = lambda: 2 * lax.axis_index("x") + lax.axis_index("y")

    def body(b_ref, p_ref, o_ref):
        acc = None
        for s in range(n):
            term = jnp.where(my_chip() == s, p_ref[0], b_ref[s]).astype(F32)
            acc = term if acc is None else acc + term
        o_ref[...] = acc

    return _call(body, name=name, grid=(nb,),
                 in_specs=[pl.BlockSpec((n, tr, wd), lambda i: (0, i, 0)), pl.BlockSpec((1, tr, wd), lambda i: (my_chip(), i, 0))],
                 out_specs=pl.BlockSpec((tr, wd), lambda i: (lax.axis_index("c") * nb + i, 0)),
                 out_shape=_sds((2 * h, wd), F32), sem=("parallel",))(arrived, own)


def _share_halves(gs):
    n = len(gs)

    def body(*refs):
        out_refs = refs[n:2 * n]
        send_sems, recv_sems = refs[2 * n:]
        x, y, c, _ = _place()
        cps = []
        for k in range(n):
            half, other = _half_rows(out_refs[k].shape[0], c, 8), _half_rows(out_refs[k].shape[0], 1 - c, 8)
            sem = dict(send_sem=send_sems.at[k], recv_sem=recv_sems.at[k], device_id=(x, y, 1 - c), device_id_type=MESH)
            cps.append((pltpu.make_async_remote_copy(src_ref=out_refs[k].at[half], dst_ref=out_refs[k].at[half], **sem),
                        pltpu.make_async_remote_copy(src_ref=out_refs[k].at[half], dst_ref=out_refs[k].at[other], **sem)))
            cps[-1][0].start()
        for send, recv in cps:
            recv.wait_recv()
            send.wait_send()

    return pl.pallas_call(body, name="share_halves", out_shape=[_sds(g.shape, g.dtype) for g in gs], in_specs=[HBM] * n,
                          out_specs=[HBM] * n, input_output_aliases={k: k for k in range(n)},
                          scratch_shapes=[pltpu.SemaphoreType.DMA((n,)), pltpu.SemaphoreType.DMA((n,))])(*gs)


def _gather_all(v, name):
    def body(v_ref, out_ref, send_sems, recv_sems, local_sem):
        x, y, c = lax.axis_index("x"), lax.axis_index("y"), lax.axis_index("c")
        me = 4 * x + 2 * y + c
        flip = lambda t, d: 1 - t if d else t
        mine = pltpu.make_async_copy(v_ref, out_ref.at[me], local_sem)
        mine.start()
        cps = []
        for k in range(1, 8):
            to = (flip(x, k & 4), flip(y, k & 2), flip(c, k & 1))
            cps.append(pltpu.make_async_remote_copy(src_ref=v_ref, dst_ref=out_ref.at[me], send_sem=send_sems.at[k - 1],
                                                    recv_sem=recv_sems.at[k - 1], device_id=to, device_id_type=MESH))
        for cp in cps:
            cp.start()
        for cp in cps:
            cp.wait()
        mine.wait()

    return _comm_call(body, name, _sds((8,) + v.shape, v.dtype), 1,
                      [pltpu.SemaphoreType.DMA((7,)), pltpu.SemaphoreType.DMA((7,)), pltpu.SemaphoreType.DMA])(v)


BIG = (("w_in", 1), ("w_branch_a", 1), ("w_branch_b", 1), ("w_out", 0), ("w_ffn_in", 1), ("w_ffn_out", 0))
SMALL = ("meta_tokens", "lb_logits", "mix_norm_w", "hg_norm_w", "gd_conv_w", "gd_a_log", "gd_dt_bias", "gd_norm_w",
         "ffn_norm_w", "final_norm_w")


def _pack_lanes(parts):
    rows = []
    for p in parts:
        f = p.reshape(-1).astype(F32)
        n = -(-f.shape[0] // DH) * DH
        rows.append(jnp.pad(f, (0, n - f.shape[0])).reshape(-1, DH))
    buf = jnp.concatenate(rows, axis=0)
    return jnp.pad(buf, ((0, -buf.shape[0] % 8), (0, 0)))


def _unpack_lanes(buf, shapes):
    out, off = [], 0
    for shp in shapes:
        n = math.prod(shp)
        r = -(-n // DH)
        out.append(buf[off:off + r].reshape(-1)[:n].reshape(shp))
        off += r
    return out


def kernel(x, meta_tokens, lb_logits, mix_norm_w, w_in, hg_norm_w, gd_conv_w, gd_a_log, gd_dt_bias, gd_norm_w, w_branch_a, w_branch_b, w_out, ffn_norm_w, w_ffn_in, w_ffn_out, final_norm_w, loss_target, m_meta_tokens, m_lb_logits, m_mix_norm_w, m_w_in, m_hg_norm_w, m_gd_conv_w, m_gd_a_log, m_gd_dt_bias, m_gd_norm_w, m_w_branch_a, m_w_branch_b, m_w_out, m_ffn_norm_w, m_w_ffn_in, m_w_ffn_out, m_final_norm_w, v_meta_tokens, v_lb_logits, v_mix_norm_w, v_w_in, v_hg_norm_w, v_gd_conv_w, v_gd_a_log, v_gd_dt_bias, v_gd_norm_w, v_w_branch_a, v_w_branch_b, v_w_out, v_ffn_norm_w, v_w_ffn_in, v_w_ffn_out, v_final_norm_w):
    args = dict(locals())
    big = [n for n, _ in BIG]
    w = {n: args[n] for n in SMALL + tuple(big)}
    m = {n: args["m_" + n] for n in w}
    v = {n: args["v_" + n] for n in w}
    xi, yi, ci = lax.axis_index("x"), lax.axis_index("y"), lax.axis_index("c")
    shard = 2 * xi + yi
    big_local = {n: w[n][0] for n in big}

    meta_cols, conv_cols = meta_tokens.shape[1], gd_conv_w.shape[-1]
    sm_all = _gather_all(_pack_lanes([meta_tokens, gd_conv_w[0]]), "gather_meta")
    sm_parts = [_unpack_lanes(sm_all[2 * s], [meta_tokens.shape, gd_conv_w[0].shape]) for s in range(4)]
    meta_full = jnp.concatenate([p[0] for p in sm_parts], axis=1)
    conv_full = jnp.concatenate([p[1] for p in sm_parts], axis=1)
    cvec = ci.reshape(1).astype(jnp.int32)
    late = [n for n in big if n != "w_in"]
    ffn = ["w_ffn_in", "w_ffn_out"]
    rows_full = lambda t: t.reshape(t.shape[0] * t.shape[1], t.shape[2])

    def pair_sums(names, gs):
        return [_add_half(gk, got, cvec, "add_half_" + n) for n, gk, got in zip(names, gs, _swap_halves(gs, "swap_" + names[0]))]

    def with_own(slabs, n):
        return lax.dynamic_update_index_in_dim(slabs, big_local[n].astype(BF16), shard, 0)

    class MeshComm:
        def proj_side(self):
            return _gather_side([big_local[n].astype(BF16) for n in late])

        def late_weights(self, landed):
            wl = {n: with_own(t, n) for n, t in zip(late, _forward_halves(landed, "forward_late"))}
            return (wl["w_branch_a"], wl["w_branch_b"], rows_full(wl["w_out"]), wl["w_ffn_in"], rows_full(wl["w_ffn_out"]))

        def ffn_grads_side(self, dw_ffn_in, dw_ffn_out):
            self.ffn_parts = pair_sums(ffn, [dw_ffn_in, dw_ffn_out])
            return _scatter_side(self.ffn_parts)

        def ffn_grads_done(self, arrived):
            self.ffn_arrived = arrived

    comm = MeshComm()
    w_in_slabs = with_own(_forward_halves(_gather_chips([big_local["w_in"].astype(BF16)]), "forward_w_in")[0], "w_in")
    d_model = w_in_slabs.shape[1]
    w_in_full = w_in_slabs.transpose(1, 0, 2).reshape(d_model, -1)
    loss, dx, g = _local_step(x[0], loss_target[0], meta_full, lb_logits, mix_norm_w, w_in_full, hg_norm_w, conv_full,
                              gd_a_log, gd_dt_bias, gd_norm_w, ffn_norm_w, final_norm_w, comm)
    loss = lax.psum(loss[0, 0], ("x", "y", "c"))

    rest = [n for n in big if n not in ffn]
    parts = dict(zip(rest, pair_sums(rest, [g[n] for n in rest])))
    arrived = dict(zip(rest, _scatter_chips([parts[n] for n in rest])))
    parts.update(zip(ffn, comm.ffn_parts))
    arrived.update(zip(ffn, comm.ffn_arrived))
    g_big = dict(zip(big, _share_halves([_sum_chips(arrived[n], parts[n], "sum_chips_" + n) for n in big])))

    small_shapes = [g[n].shape for n in SMALL]
    g_all = _gather_all(_pack_lanes([g[n] for n in SMALL]), "gather_small")
    g_small = dict(zip(SMALL, _unpack_lanes(_sum_slabs(g_all, "sum_small"), small_shapes)))
    g_small["meta_tokens"] = lax.dynamic_slice_in_dim(g_small["meta_tokens"], shard * meta_cols, meta_cols, axis=1)
    g_small["gd_conv_w"] = lax.dynamic_slice_in_dim(g_small["gd_conv_w"], shard * conv_cols, conv_cols, axis=1)

    grad, delta, new_m, new_v = {}, {}, {}, {}
    for n in big:
        grad[n] = g_big[n].reshape(w[n].shape)
        d_, m_, v_ = _adamw(g_big[n], big_local[n], m[n][0], v[n][0], "adamw_" + n)
        delta[n], new_m[n], new_v[n] = d_.reshape(w[n].shape), m_.reshape(w[n].shape), v_.reshape(w[n].shape)
    local_shapes = [w[n].shape for n in SMALL]
    d_, m_, v_ = _adamw(_pack_lanes([g_small[n] for n in SMALL]), _pack_lanes([w[n] for n in SMALL]),
                        _pack_lanes([m[n] for n in SMALL]), _pack_lanes([v[n] for n in SMALL]), "adamw_small")
    for n, gs_, dd, mm, vv in zip(SMALL, [g_small[n] for n in SMALL], _unpack_lanes(d_, local_shapes), _unpack_lanes(m_, local_shapes),
                                  _unpack_lanes(v_, local_shapes)):
        grad[n], delta[n], new_m[n], new_v[n] = gs_.reshape(w[n].shape), dd, mm, vv

    order = ["meta_tokens", "lb_logits", "mix_norm_w", "w_in", "hg_norm_w", "gd_conv_w", "gd_a_log", "gd_dt_bias", "gd_norm_w",
             "w_branch_a", "w_branch_b", "w_out", "ffn_norm_w", "w_ffn_in", "w_ffn_out", "final_norm_w"]
    return (loss, dx[None], *[grad[n] for n in order], *[delta[n] for n in order], *[new_m[n] for n in order],
            *[new_v[n] for n in order])
```

```python
import functools
import math

import jax
import jax.numpy as jnp
from jax import lax
from jax.experimental import pallas as pl
from jax.experimental.pallas import tpu as pltpu

F32, BF16 = jnp.float32, jnp.bfloat16
HI = lax.Precision.HIGHEST
EPS = 1e-6
D_MODEL = 2048
N_META = 16
FRONT = 256
CH = 64
SUB = 16
DH = 128
NH = 8
HW = NH * DH
CONV_K = 4
RT = 256
VMEM_LIMIT = 56 * 1024 * 1024
ADAM_LR, ADAM_B1, ADAM_B2, ADAM_EPS, ADAM_WD, ADAM_STEP = 0.001, 0.9, 0.999, 1e-08, 0.01, 10

NN = (((1,), (0,)), ((), ()))
NT = (((1,), (1,)), ((), ()))
TN = (((0,), (0,)), ((), ()))


def _dot(a, b, dn=NN):
    return lax.dot_general(a.astype(BF16), b.astype(BF16), dn, preferred_element_type=F32)


def _dotx(a, b, dn=NN):
    return lax.dot_general(a, b, dn, precision=HI, preferred_element_type=F32)


class _Side:
    def __init__(self, inputs, out_shapes, scratch, start, wait):
        self.inputs, self.out_shapes, self.scratch, self.start, self.wait = inputs, out_shapes, scratch, start, wait


def _call(body, *, name, grid, in_specs, out_specs, out_shape, scratch=(), sem=None, side=None):
    params = pltpu.CompilerParams(dimension_semantics=sem, vmem_limit_bytes=VMEM_LIMIT)
    if side is None:
        return pl.pallas_call(body, name=name, grid=grid, in_specs=in_specs, out_specs=out_specs, out_shape=out_shape,
                              scratch_shapes=list(scratch), compiler_params=params)
    single = not isinstance(out_specs, (list, tuple))
    out_specs, out_shape = ([out_specs], [out_shape]) if single else (list(out_specs), list(out_shape))
    ni, no, ns = len(in_specs), len(out_specs), len(scratch)
    nsi, nso = len(side.inputs), len(side.out_shapes)
    hbm = pl.BlockSpec(memory_space=pltpu.HBM)

    def wrapped(*refs):
        main_in, side_in = refs[:ni], refs[ni:ni + nsi]
        main_out, side_out = refs[ni + nsi:ni + nsi + no], refs[ni + nsi + no:ni + nsi + no + nso]
        main_scr, side_scr = refs[ni + nsi + no + nso:ni + nsi + no + nso + ns], refs[ni + nsi + no + nso + ns:]
        pids = [pl.program_id(d) for d in range(len(grid))]
        first = functools.reduce(lambda a, b: a & b, [p == 0 for p in pids])
        last = functools.reduce(lambda a, b: a & b, [p == g - 1 for p, g in zip(pids, grid)])

        @pl.when(first)
        def _():
            side.start(side_in, side_out, side_scr)

        body(*main_in, *main_out, *main_scr)

        @pl.when(last)
        def _():
            side.wait(side_in, side_out, side_scr)

    call = pl.pallas_call(wrapped, name=name, grid=grid, in_specs=list(in_specs) + [hbm] * nsi,
                          out_specs=out_specs + [hbm] * nso, out_shape=out_shape + list(side.out_shapes),
                          scratch_shapes=list(scratch) + list(side.scratch), compiler_params=params)

    def run(*args):
        outs = call(*args, *side.inputs)
        main = outs[0] if single else list(outs[:no])
        return main, list(outs[no:])

    return run


def _divmod(j, per):
    if per == 1:
        return j, 0
    return lax.div(j, jnp.int32(per)), lax.rem(j, jnp.int32(per))


def _sds(shape, dtype):
    return jax.ShapeDtypeStruct(tuple(shape), dtype)


def _sigmoid(x):
    return 1.0 / (1.0 + jnp.exp(-x))


def _silu(x):
    return x * _sigmoid(x)


def _dsilu(x):
    s = _sigmoid(x)
    return s * (1.0 + x * (1.0 - s))


def _tri(n, kind):
    r = lax.broadcasted_iota(jnp.int32, (n, n), 0)
    c = lax.broadcasted_iota(jnp.int32, (n, n), 1)
    return {"incl": r >= c, "strict": r > c, "upper": c >= r}[kind]


def _mm(a, b, mode, out_dtype, tm, tn, tk, name, add=None, n_outer=False, out_shards=0, side=None):
    sharded_b = b.ndim == 3
    if sharded_b:
        S, R, n = b.shape
        b_rows, b_cols = R, S * n
    else:
        b_rows, b_cols = b.shape
    if mode == "nn":
        (M, K), N, dn = a.shape, b_cols, NN
    elif mode == "nt":
        (M, K), N, dn = a.shape, b_rows, NT
    else:
        (K, M), N, dn = a.shape, b_cols, TN
    tm, tn, tk = min(tm, M), min(tn, N), min(tk, K)
    if sharded_b:
        tn, tk = (min(tn, n), tk) if mode != "nt" else (tn, min(tk, n))
    if out_shards:
        tn = min(tn, N // out_shards)
    assert M % tm == 0 and N % tn == 0 and K % tk == 0, (name, M, N, K, tm, tn, tk)
    nk = K // tk
    a_blk, a_idx = ((tm, tk), lambda i, j, k: (i, k)) if mode != "tn" else ((tk, tm), lambda i, j, k: (k, i))
    if not sharded_b:
        b_blk, b_idx = ((tk, tn), lambda i, j, k: (k, j)) if mode != "nt" else ((tn, tk), lambda i, j, k: (j, k))
    elif mode != "nt":
        per = n // tn
        assert n % tn == 0
        b_blk, b_idx = (None, tk, tn), lambda i, j, k: (_divmod(j, per)[0], k, _divmod(j, per)[1])
    else:
        per = n // tk
        assert n % tk == 0
        b_blk, b_idx = (None, tn, tk), lambda i, j, k: (_divmod(k, per)[0], j, _divmod(k, per)[1])
    if out_shards:
        per_o = N // out_shards // tn
        assert (N // out_shards) % tn == 0
        o_blk, o_idx = (None, tm, tn), lambda i, j, k: (_divmod(j, per_o)[0], i, _divmod(j, per_o)[1])
        o_shape = (out_shards, M, N // out_shards)
    else:
        o_blk, o_idx, o_shape = (tm, tn), (lambda i, j, k: (i, j)), (M, N)
    c_idx = lambda i, j, k: (i, j)
    if n_outer:
        sw = lambda f: (lambda j, i, k: f(i, j, k))
        a_idx, b_idx, o_idx, c_idx = sw(a_idx), sw(b_idx), sw(o_idx), sw(c_idx)
        grid = (N // tn, M // tm, nk)
    else:
        grid = (M // tm, N // tn, nk)
    has_add = add is not None

    def body(*refs):
        if has_add:
            a_ref, b_ref, c_ref, o_ref, acc_ref = refs
        else:
            a_ref, b_ref, o_ref, acc_ref = refs
            c_ref = None
        part = lax.dot_general(a_ref[...].astype(BF16), b_ref[...].astype(BF16), dn, preferred_element_type=F32)

        def fin(val):
            if has_add:
                val = val + c_ref[...]
            o_ref[...] = val.astype(out_dtype)

        if nk == 1:
            fin(part)
        else:
            k = pl.program_id(2)

            @pl.when(k == 0)
            def _():
                acc_ref[...] = part

            @pl.when(k > 0)
            def _():
                acc_ref[...] += part

            @pl.when(k == nk - 1)
            def _():
                fin(acc_ref[...])

    in_specs = [pl.BlockSpec(a_blk, a_idx), pl.BlockSpec(b_blk, b_idx)]
    args = [a, b]
    if has_add:
        in_specs.append(pl.BlockSpec((tm, tn), c_idx))
        args.append(add)
    acc_shape = (tm, tn) if nk > 1 else (8, 128)
    return _call(body, name=name, grid=grid, in_specs=in_specs, out_specs=pl.BlockSpec(o_blk, o_idx),
                 out_shape=_sds(o_shape, out_dtype), scratch=[pltpu.VMEM(acc_shape, F32)],
                 sem=("arbitrary",) * 3 if side is not None else ("parallel", "parallel", "arbitrary"), side=side)(*args)


def _rms1_fwd(x, meta, w):
    seq, d = x.shape
    nt = (FRONT + seq) // RT

    def body(x_ref, m_ref, w_ref, o_ref):
        i = pl.program_id(0)

        def norm(v):
            r = lax.rsqrt(jnp.mean(v * v, axis=-1, keepdims=True) + EPS)
            return (v * r * w_ref[...]).astype(BF16)

        @pl.when(i == 0)
        def _():
            o_ref[0:RT - N_META, :] = jnp.zeros((RT - N_META, d), BF16)
            o_ref[RT - N_META:RT, :] = norm(m_ref[...])

        @pl.when(i > 0)
        def _():
            o_ref[...] = norm(x_ref[...])

    return _call(body, name="rms1_fwd", grid=(nt,),
                 in_specs=[pl.BlockSpec((RT, d), lambda i: (jnp.maximum(i - 1, 0), 0)),
                           pl.BlockSpec((N_META, d), lambda i: (0, 0)),
                           pl.BlockSpec((1, d), lambda i: (0, 0))],
                 out_specs=pl.BlockSpec((RT, d), lambda i: (i, 0)),
                 out_shape=_sds((FRONT + seq, d), BF16), sem=("parallel",))(x, meta, w)


def _rms1_bwd(x, meta, w, dxn, dh1):
    seq, d = x.shape
    nt = (FRONT + seq) // RT

    def body(x_ref, m_ref, w_ref, g_ref, r_ref, dx_ref, dm_ref, dw_ref):
        i = pl.program_id(0)

        def bwd(v, g):
            r = lax.rsqrt(jnp.mean(v * v, axis=-1, keepdims=True) + EPS)
            vh = v * r
            gh = g * w_ref[...]
            return r * (gh - vh * jnp.mean(gh * vh, axis=-1, keepdims=True)), jnp.sum(g * vh, axis=0, keepdims=True)

        @pl.when(i == 0)
        def _():
            dm, dw = bwd(m_ref[...], g_ref[RT - N_META:RT, :])
            dm_ref[...] = dm
            dw_ref[...] = dw

        @pl.when(i > 0)
        def _():
            dx, dw = bwd(x_ref[...], g_ref[...])
            dx_ref[...] = dx + r_ref[...]
            dw_ref[...] += dw

    xs = pl.BlockSpec((RT, d), lambda i: (jnp.maximum(i - 1, 0), 0))
    return _call(body, name="rms1_bwd", grid=(nt,),
                 in_specs=[xs, pl.BlockSpec((N_META, d), lambda i: (0, 0)), pl.BlockSpec((1, d), lambda i: (0, 0)),
                           pl.BlockSpec((RT, d), lambda i: (i, 0)), xs],
                 out_specs=[xs, pl.BlockSpec((N_META, d), lambda i: (0, 0)), pl.BlockSpec((1, d), lambda i: (0, 0))],
                 out_shape=[_sds((seq, d), F32), _sds((N_META, d), F32), _sds((1, d), F32)],
                 sem=("arbitrary",))(x, meta, w, dxn, dh1)


def _merge_fwd(proj, za, zb):
    seq, d = za.shape
    off = FRONT // RT
    ca, cb = 8 * HW // d, 8 * HW // d + 1

    def body(ga_ref, gb_ref, za_ref, zb_ref, o_ref):
        o_ref[...] = (_sigmoid(ga_ref[...]) * za_ref[...] + _sigmoid(gb_ref[...]) * zb_ref[...]).astype(BF16)

    zs = pl.BlockSpec((RT, d), lambda i: (i, 0))
    return _call(body, name="merge_fwd", grid=(seq // RT,),
                 in_specs=[pl.BlockSpec((RT, d), lambda i: (i + off, ca)), pl.BlockSpec((RT, d), lambda i: (i + off, cb)), zs, zs],
                 out_specs=zs, out_shape=_sds((seq, d), BF16), sem=("parallel",))(proj, proj, za, zb)


def _merge_bwd(proj, za, zb, dmerged):
    seq, d = za.shape
    off = FRONT // RT
    ca, cb = 8 * HW // d, 8 * HW // d + 1
    nt = (FRONT + seq) // RT

    def body(ga_ref, gb_ref, za_ref, zb_ref, dm_ref, dza_ref, dzb_ref, dg_ref):
        i = pl.program_id(0)

        @pl.when(i < off)
        def _():
            dg_ref[...] = jnp.zeros((RT, 2 * d), BF16)

        @pl.when(i >= off)
        def _():
            sa, sb, dm = _sigmoid(ga_ref[...]), _sigmoid(gb_ref[...]), dm_ref[...]
            dza_ref[...] = (dm * sa).astype(BF16)
            dzb_ref[...] = (dm * sb).astype(BF16)
            dg_ref[:, 0:d] = (dm * za_ref[...] * sa * (1.0 - sa)).astype(BF16)
            dg_ref[:, d:2 * d] = (dm * zb_ref[...] * sb * (1.0 - sb)).astype(BF16)

    rs = pl.BlockSpec((RT, d), lambda i: (jnp.maximum(i - off, 0), 0))
    return _call(body, name="merge_bwd", grid=(nt,),
                 in_specs=[pl.BlockSpec((RT, d), lambda i: (i, ca)), pl.BlockSpec((RT, d), lambda i: (i, cb)), rs, rs, rs],
                 out_specs=[rs, rs, pl.BlockSpec((RT, 2 * d), lambda i: (i, 0))],
                 out_shape=[_sds((seq, d), BF16), _sds((seq, d), BF16), _sds((FRONT + seq, 2 * d), BF16)],
                 sem=("arbitrary",))(proj, proj, za, zb, dmerged)


def _resid_norm_fwd(x, mix, w):
    seq, d = x.shape

    def body(x_ref, m_ref, w_ref, h_ref, n_ref):
        h = x_ref[...] + m_ref[...]
        h_ref[...] = h
        r = lax.rsqrt(jnp.mean(h * h, axis=-1, keepdims=True) + EPS)
        n_ref[...] = (h * r * w_ref[...]).astype(BF16)

    rs = pl.BlockSpec((RT, d), lambda i: (i, 0))
    return _call(body, name="resid_norm_fwd", grid=(seq // RT,),
                 in_specs=[rs, rs, pl.BlockSpec((1, d), lambda i: (0, 0))], out_specs=[rs, rs],
                 out_shape=[_sds((seq, d), F32), _sds((seq, d), BF16)], sem=("parallel",))(x, mix, w)


def _resid_norm_bwd(h1, w, dn2, dh2):
    seq, d = h1.shape

    def body(h_ref, w_ref, g_ref, r_ref, o_ref, ob_ref, dw_ref):
        i = pl.program_id(0)
        h, g = h_ref[...], g_ref[...]
        r = lax.rsqrt(jnp.mean(h * h, axis=-1, keepdims=True) + EPS)
        hh = h * r
        gh = g * w_ref[...]
        dh = r_ref[...] + r * (gh - hh * jnp.mean(gh * hh, axis=-1, keepdims=True))
        o_ref[...] = dh
        ob_ref[...] = dh.astype(BF16)
        dw = jnp.sum(g * hh, axis=0, keepdims=True)

        @pl.when(i == 0)
        def _():
            dw_ref[...] = dw

        @pl.when(i > 0)
        def _():
            dw_ref[...] += dw

    rs = pl.BlockSpec((RT, d), lambda i: (i, 0))
    ws = pl.BlockSpec((1, d), lambda i: (0, 0))
    return _call(body, name="resid_norm_bwd", grid=(seq // RT,), in_specs=[rs, ws, rs, rs], out_specs=[rs, rs, ws],
                 out_shape=[_sds((seq, d), F32), _sds((seq, d), BF16), _sds((1, d), F32)], sem=("arbitrary",))(h1, w, dn2, dh2)


def _swiglu_fwd(gu):
    seq, f2 = gu.shape
    ff = f2 // 2
    tc = 512
    nb = ff // tc

    def body(g_ref, u_ref, o_ref):
        o_ref[...] = (_silu(g_ref[...]) * u_ref[...]).astype(BF16)

    return _call(body, name="swiglu_fwd", grid=(seq // RT, nb),
                 in_specs=[pl.BlockSpec((RT, tc), lambda i, j: (i, j)), pl.BlockSpec((RT, tc), lambda i, j: (i, j + nb))],
                 out_specs=pl.BlockSpec((RT, tc), lambda i, j: (i, j)), out_shape=_sds((seq, ff), BF16),
                 sem=("parallel", "parallel"))(gu, gu)


def _swiglu_bwd(gu, dact):
    seq, f2 = gu.shape
    ff = f2 // 2
    tc = 512
    nb = ff // tc

    def body(g_ref, u_ref, d_ref, dg_ref, du_ref):
        g, d = g_ref[...], d_ref[...]
        dg_ref[...] = (d * u_ref[...] * _dsilu(g)).astype(BF16)
        du_ref[...] = (d * _silu(g)).astype(BF16)

    bs = pl.BlockSpec((RT, tc), lambda i, j: (i, j))
    dgate, dup = _call(body, name="swiglu_bwd", grid=(seq // RT, nb),
                       in_specs=[bs, pl.BlockSpec((RT, tc), lambda i, j: (i, j + nb)), bs], out_specs=[bs, bs],
                       out_shape=[_sds((seq, ff), BF16), _sds((seq, ff), BF16)], sem=("parallel", "parallel"))(gu, gu, dact)
    return jnp.concatenate([dgate, dup], axis=1)


def _loss_head(h1, f, w, tgt):
    seq, d = h1.shape
    nt = seq // RT

    def body(h_ref, f_ref, w_ref, t_ref, l_ref, dh_ref, dhb_ref, dw_ref):
        i = pl.program_id(0)
        h = h_ref[...] + f_ref[...]
        r = lax.rsqrt(jnp.mean(h * h, axis=-1, keepdims=True) + EPS)
        hh = h * r
        err = hh * w_ref[...] - t_ref[...]
        l_ref[...] = jnp.full((8, 128), 0.5 * jnp.sum(jnp.mean(err * err, axis=-1, keepdims=True)), F32)
        dy = err * (1.0 / d)
        gh = dy * w_ref[...]
        dh = r * (gh - hh * jnp.mean(gh * hh, axis=-1, keepdims=True))
        dh_ref[...] = dh
        dhb_ref[...] = dh.astype(BF16)
        dw = jnp.sum(dy * hh, axis=0, keepdims=True)

        @pl.when(i == 0)
        def _():
            dw_ref[...] = dw

        @pl.when(i > 0)
        def _():
            dw_ref[...] += dw

    rs = pl.BlockSpec((RT, d), lambda i: (i, 0))
    ws = pl.BlockSpec((1, d), lambda i: (0, 0))
    return _call(body, name="loss_head", grid=(nt,), in_specs=[rs, rs, ws, rs],
                 out_specs=[pl.BlockSpec((8, 128), lambda i: (i, 0)), rs, rs, ws],
                 out_shape=[_sds((nt * 8, 128), F32), _sds((seq, d), F32), _sds((seq, d), BF16), _sds((1, d), F32)],
                 sem=("arbitrary",))(h1, f, w, tgt)


def _sum_tiles(lt):
    n = lt.shape[0]

    def body(l_ref, o_ref):
        v = l_ref[...]
        r = lax.broadcasted_iota(jnp.int32, v.shape, 0)
        c = lax.broadcasted_iota(jnp.int32, v.shape, 1)
        o_ref[...] = jnp.sum(jnp.where((r % 8 == 0) & (c == 0), v, 0.0), keepdims=True)

    return _call(body, name="loss_sum", grid=(1,), in_specs=[pl.BlockSpec((n, 128), lambda i: (0, 0))],
                 out_specs=pl.BlockSpec((1, 1), lambda i: (0, 0)), out_shape=_sds((1, 1), F32))(lt)


def _gated_norm_fwd(o, g, nw):
    r = lax.rsqrt(jnp.mean(o * o, axis=-1, keepdims=True) + EPS)
    return o * r * nw * _silu(g)


def _gated_norm_bwd(o, g, nw, dout):
    r = lax.rsqrt(jnp.mean(o * o, axis=-1, keepdims=True) + EPS)
    oh = o * r
    don = dout * _silu(g)
    dg = dout * (oh * nw) * _dsilu(g)
    dnw = jnp.sum(don * oh, axis=0, keepdims=True)
    doh = don * nw
    return r * (doh - oh * jnp.mean(doh * oh, axis=-1, keepdims=True)), dg, dnw


def _hg_gates(fs, lbl):
    l0, l1 = lbl[0:1, :], lbl[1:2, :]
    m = jnp.maximum(l0, l1)
    e0, e1 = jnp.exp(l0 - m), jnp.exp(l1 - m)
    lb = e0 / (e0 + e1)
    sig = _sigmoid(fs)
    f = lb + (1.0 - lb) * sig
    return lb, sig, f, jnp.log(f), (1.0 - lb) * _sigmoid(-fs)


def _cumsum(w):
    row = lax.broadcasted_iota(jnp.int32, w.shape, 0) & (CH - 1)
    s = 1
    while s < CH:
        w = w + jnp.where(row >= s, pltpu.roll(w, s, 0), 0.0)
        s *= 2
    return w


def _rcumsum(w):
    row = lax.broadcasted_iota(jnp.int32, w.shape, 0) & (CH - 1)
    s = 1
    while s < CH:
        w = w + jnp.where(row < CH - s, pltpu.roll(w, w.shape[0] - s, 0), 0.0)
        s *= 2
    return w


def _decay_blocks(q, k, b, p_ref):
    p_ref[...] = jnp.zeros((CH, CH), F32)
    m16 = _tri(SUB, "incl")
    for I in range(CH // SUB):
        s0 = I * SUB
        bI, qI, kI = b[s0:s0 + SUB], q[s0:s0 + SUB], k[s0:s0 + SUB]
        dec = jnp.exp(jnp.minimum(bI[:, None, :] - bI[None, :, :], 0.0))
        pii = jnp.sum(qI[:, None, :] * kI[None, :, :] * dec, axis=-1)
        p_ref[s0:s0 + SUB, s0:s0 + SUB] = jnp.where(m16, pii, 0.0)
        if I > 0:
            rI = b[s0 - 1:s0]
            qs = qI * jnp.exp(bI - rI)
            ks = k[0:s0] * jnp.exp(rI - b[0:s0])
            p_ref[s0:s0 + SUB, 0:s0] = _dot(qs, ks, NT)


HPS = 8
HPS_HGRN2_FWD = 4
HB = 4
NHB = NH // HPS
OFF = FRONT // RT


def _head_specs(hps, rev=None):
    row = (lambda i: i) if rev is None else rev
    col = lambda g: pl.BlockSpec((RT, hps * DH), lambda h, i: (row(i), g * (NH // hps) + h))
    full = pl.BlockSpec((RT, hps * DH), lambda h, i: (row(i), h))
    real = pl.BlockSpec((RT, hps * DH), lambda h, i: (jnp.maximum(row(i) - OFF, 0), h))
    state = lambda cpt: pl.BlockSpec((hps, cpt, DH, DH), lambda h, i: (h, row(i), 0, 0))
    scal = pl.BlockSpec((hps, RT, DH), lambda h, i: (h, row(i), 0))
    return col, full, real, state, scal


def _hgrn2_fwd(proj, lb_logits, nw):
    tp = proj.shape[0]
    nt, cpt = tp // RT, RT // CH
    hps = HPS_HGRN2_FWD

    def body(q_ref, f_ref, i_ref, g_ref, lbl_ref, nw_ref, og_ref, or_ref, st_ref, s_ref, p_ref):
        @pl.when(pl.program_id(1) == 0)
        def _():
            s_ref[...] = jnp.zeros((hps, DH, DH), F32)

        def chunk(c, carry):
            rows = pl.ds(pl.multiple_of(c * CH, CH), CH)
            for hh in range(hps):
                cols = slice(hh * DH, (hh + 1) * DH)
                _, _, _, w, k = _hg_gates(f_ref[rows, cols], lbl_ref[:, cols])
                q, v = _silu(q_ref[rows, cols]), i_ref[rows, cols]
                b = _cumsum(w)
                st = s_ref[hh]
                st_ref[hh, c] = st
                _decay_blocks(q, k, b, p_ref.at[hh])
                o = _dot(q * jnp.exp(b), st, NT) + _dot(p_ref[hh], v)
                bl = b[CH - 1:CH]
                s_ref[hh] = st * jnp.exp(bl) + _dot(v, k * jnp.exp(bl - b), TN)
                or_ref[rows, cols] = o
                og_ref[rows, cols] = _gated_norm_fwd(o, g_ref[rows, cols], nw_ref[...]).astype(BF16)
            return carry

        lax.fori_loop(0, cpt, chunk, 0)

    col, full, real, state, _ = _head_specs(hps)
    return _call(body, name="hgrn2_fwd", grid=(NH // hps, nt),
                 in_specs=[col(0), col(1), col(2), col(3), pl.BlockSpec((2, hps * DH), lambda h, i: (0, h)),
                           pl.BlockSpec((1, DH), lambda h, i: (0, 0))],
                 out_specs=[real, full, state(cpt)],
                 out_shape=[_sds((tp - FRONT, HW), BF16), _sds((tp, HW), F32), _sds((NH, tp // CH, DH, DH), F32)],
                 scratch=[pltpu.VMEM((hps, DH, DH), F32), pltpu.VMEM((hps, CH, CH), F32)],
                 sem=("parallel", "arbitrary"))(proj, proj, proj, proj, lb_logits, nw)


def _hgrn2_bwd(proj, lb_logits, nw, o_raw, states, dog, side=None):
    tp = proj.shape[0]
    nt, cpt = tp // RT, RT // CH

    def body(q_ref, f_ref, i_ref, g_ref, lbl_ref, nw_ref, or_ref, st_ref, dog_ref,
             dq_ref, df_ref, di_ref, dg_ref, dl_ref, dnw_ref, ds_ref, p_ref, dk_ref, dqa_ref, do_ref):
        step = pl.program_id(1)

        @pl.when(step == 0)
        def _():
            ds_ref[...] = jnp.zeros((HPS, DH, DH), F32)
            dl_ref[...] = jnp.zeros((2, HPS * DH), F32)

        @pl.when((step == 0) & (pl.program_id(0) == 0))
        def _():
            dnw_ref[...] = jnp.zeros((1, DH), F32)

        front = nt - 1 - step < OFF
        for hh in range(HPS):
            cols = slice(hh * DH, (hh + 1) * DH)
            dog_t = jnp.where(front, 0.0, dog_ref[:, cols])
            do_t, dg_t, dnw = _gated_norm_bwd(or_ref[:, cols], g_ref[:, cols], nw_ref[...], dog_t)
            do_ref[:, cols] = do_t
            dg_ref[:, cols] = dg_t.astype(BF16)
            dnw_ref[...] += dnw
        tril = _tri(CH, "incl")
        m16 = _tri(SUB, "incl")

        def chunk(cc, carry):
            c = cpt - 1 - cc
            rows = pl.ds(pl.multiple_of(c * CH, CH), CH)
            for hh in range(HPS):
                cols = slice(hh * DH, (hh + 1) * DH)
                fs = f_ref[rows, cols]
                lb, sig, f, w, k = _hg_gates(fs, lbl_ref[:, cols])
                hq = q_ref[rows, cols]
                q, v, do = _silu(hq), i_ref[rows, cols], do_ref[rows, cols]
                b = _cumsum(w)
                bl = b[CH - 1:CH]
                eb = jnp.exp(b)
                qs, kd = q * eb, k * jnp.exp(bl - b)
                st, dst = st_ref[hh, c], ds_ref[hh]
                _decay_blocks(q, k, b, p_ref.at[hh])
                dv = _dot(p_ref[hh], do, TN) + _dot(kd, dst, NT)
                dp = jnp.where(tril, _dotx(do, v, NT), 0.0)
                dqa, dka = dqa_ref.at[hh], dk_ref.at[hh]
                dqa[...] = eb * _dotx(do, st)
                dka[...] = jnp.exp(bl - b) * _dotx(v, dst)
                for I in range(CH // SUB):
                    s0 = I * SUB
                    bI, qI, kI = b[s0:s0 + SUB], q[s0:s0 + SUB], k[s0:s0 + SUB]
                    dec = jnp.exp(jnp.minimum(bI[:, None, :] - bI[None, :, :], 0.0))
                    dpii = jnp.where(m16, dp[s0:s0 + SUB, s0:s0 + SUB], 0.0)[:, :, None] * dec
                    dqa[s0:s0 + SUB, :] += jnp.sum(dpii * kI[None, :, :], axis=1)
                    dka[s0:s0 + SUB, :] += jnp.sum(dpii * qI[:, None, :], axis=0)
                    if I > 0:
                        rI = b[s0 - 1:s0]
                        eq, ek = jnp.exp(bI - rI), jnp.exp(rI - b[0:s0])
                        dpij = dp[s0:s0 + SUB, 0:s0]
                        dqa[s0:s0 + SUB, :] += eq * _dotx(dpij, k[0:s0] * ek)
                        dka[0:s0, :] += ek * _dotx(dpij, qI * eq, TN)
                dq, dk = dqa[...], dka[...]
                st_end = st * jnp.exp(bl) + _dotx(v, kd, TN)
                dw = _rcumsum(q * dq - k * dk) + jnp.sum(dst * st_end, axis=0, keepdims=True)
                ds_ref[hh] = dst * jnp.exp(bl) + _dotx(do, qs, TN)
                one_m = 1.0 - sig
                dq_ref[rows, cols] = (dq * _dsilu(hq)).astype(BF16)
                df_ref[rows, cols] = ((dw / f - dk) * (1.0 - lb) * sig * one_m).astype(BF16)
                di_ref[rows, cols] = dv.astype(BF16)
                dl_ref[0:1, cols] += jnp.sum((dw / f - dk) * one_m, axis=0, keepdims=True)
            return carry

        lax.fori_loop(0, cpt, chunk, 0)

        @pl.when(step == nt - 1)
        def _():
            lbl = lbl_ref[...]
            l0, l1 = lbl[0:1, :], lbl[1:2, :]
            m = jnp.maximum(l0, l1)
            e0, e1 = jnp.exp(l0 - m), jnp.exp(l1 - m)
            p0 = e0 / (e0 + e1)
            dl0 = dl_ref[0:1, :] * p0 * (1.0 - p0)
            dl_ref[0:1, :] = dl0
            dl_ref[1:2, :] = -dl0

    col, full, real, state, _ = _head_specs(HPS, lambda i: nt - 1 - i)
    lbs = pl.BlockSpec((2, HPS * DH), lambda h, i: (0, h))
    return _call(body, name="hgrn2_bwd", grid=(NHB, nt),
                 in_specs=[col(0), col(1), col(2), col(3), lbs, pl.BlockSpec((1, DH), lambda h, i: (0, 0)), full,
                           state(cpt), real],
                 out_specs=[full, full, full, full, lbs, pl.BlockSpec((1, DH), lambda h, i: (0, 0))],
                 out_shape=[_sds((tp, HW), BF16)] * 4 + [_sds((2, HW), F32), _sds((1, DH), F32)],
                 scratch=[pltpu.VMEM((HPS, DH, DH), F32), pltpu.VMEM((HPS, CH, CH), F32), pltpu.VMEM((HPS, CH, DH), F32),
                          pltpu.VMEM((HPS, CH, DH), F32), pltpu.VMEM((RT, HPS * DH), F32)],
                 sem=("arbitrary", "arbitrary"), side=side)(proj, proj, proj, proj, lb_logits, nw, o_raw, states, dog)


GQ0 = 4 * HW
CW = 3 * HW


def _gd_scalars(ab, alog, dtb):
    g = -jnp.exp(alog) * jax.nn.softplus(ab + dtb)
    return g, _sigmoid(ab)


def _conv_ext_specs(row_of):
    main = [pl.BlockSpec((RT, HW), lambda i, g=g: (row_of(i), GQ0 // HW + g)) for g in range(3)]
    prev = [pl.BlockSpec((8, HW), lambda i, g=g: (jnp.maximum(row_of(i) * (RT // 8) - 1, 0), GQ0 // HW + g)) for g in range(3)]
    return main + prev


def _conv_fill(ext_ref, xs, xps, first):
    for g in range(3):
        ext_ref[0:8, g * HW:(g + 1) * HW] = jnp.where(first, 0.0, xps[g][...])
        ext_ref[8:8 + RT, g * HW:(g + 1) * HW] = xs[g][...]


def _conv_apply(ext_ref, cw):
    y = cw[CONV_K - 1:CONV_K, :] * ext_ref[pl.ds(8, RT), :]
    for s in range(1, CONV_K):
        y += cw[CONV_K - 1 - s:CONV_K - s, :] * ext_ref[pl.ds(8 - s, RT), :]
    return y


def _gdn_prep_fwd(proj, pab, conv_w, alog, dtb):
    tp = proj.shape[0]
    nt = tp // RT

    def body(x0, x1, x2, p0, p1, p2, ab_ref, cw_ref, al_ref, dt_ref, q_ref, k_ref, v_ref, g_ref, b_ref, ext_ref):
        _conv_fill(ext_ref, (x0, x1, x2), (p0, p1, p2), pl.program_id(0) == 0)
        a = _silu(_conv_apply(ext_ref, cw_ref[...]))
        for h in range(NH):
            for part, ref, sc in ((0, q_ref, DH ** -0.5), (1, k_ref, 1.0)):
                seg = a[:, part * HW + h * DH:part * HW + (h + 1) * DH]
                ref[:, h * DH:(h + 1) * DH] = seg * (lax.rsqrt(jnp.sum(seg * seg, axis=-1, keepdims=True) + EPS) * sc)
        v_ref[...] = a[:, 2 * HW:3 * HW]
        g, beta = _gd_scalars(ab_ref[...], al_ref[...], dt_ref[...])
        for h in range(NH):
            g_ref[h] = jnp.broadcast_to(g[:, h:h + 1], (RT, DH))
            b_ref[h] = jnp.broadcast_to(beta[:, NH + h:NH + h + 1], (RT, DH))

    hs = pl.BlockSpec((RT, HW), lambda i: (i, 0))
    sc = pl.BlockSpec((NH, RT, DH), lambda i: (0, i, 0))
    one = pl.BlockSpec((1, DH), lambda i: (0, 0))
    return _call(body, name="gdn_prep_fwd", grid=(nt,),
                 in_specs=_conv_ext_specs(lambda i: i) + [pl.BlockSpec((RT, DH), lambda i: (i, 0)),
                                                           pl.BlockSpec((CONV_K, CW), lambda i: (0, 0)), one, one],
                 out_specs=[hs, hs, hs, sc, sc],
                 out_shape=[_sds((tp, HW), F32)] * 3 + [_sds((NH, tp, DH), F32)] * 2,
                 scratch=[pltpu.VMEM((RT + 8, CW), F32)], sem=("parallel",))(*([proj] * 6), pab, conv_w, alog, dtb)


def _gdn_prep_bwd(proj, pab, conv_w, alog, dtb, dq, dk, dv, dgb, dbb):
    tp = proj.shape[0]
    nt = tp // RT

    def body(x0, x1, x2, p0, p1, p2, ab_ref, cw_ref, al_ref, dt_ref, dq_ref, dk_ref, dv_ref, dg_ref, db_ref,
             dx_ref, dab_ref, dcw_ref, dal_ref, ddt_ref, ext_ref, dy_ref):
        step = pl.program_id(0)
        i = nt - 1 - step

        @pl.when(step == 0)
        def _():
            dy_ref[RT:RT + 8, :] = jnp.zeros((8, CW), F32)
            dcw_ref[...] = jnp.zeros((8, CW), F32)
            dal_ref[...] = jnp.zeros((1, DH), F32)
            ddt_ref[...] = jnp.zeros((1, DH), F32)

        _conv_fill(ext_ref, (x0, x1, x2), (p0, p1, p2), i == 0)
        cw = cw_ref[...]
        y = _conv_apply(ext_ref, cw)
        a = _silu(y)
        dsl = _dsilu(y)
        for h in range(NH):
            for part, ref, sc in ((0, dq_ref, DH ** -0.5), (1, dk_ref, 1.0)):
                lo = part * HW + h * DH
                seg = a[:, lo:lo + DH]
                r = lax.rsqrt(jnp.sum(seg * seg, axis=-1, keepdims=True) + EPS)
                xh = seg * r
                dxh = ref[:, h * DH:(h + 1) * DH] * sc
                dy_ref[0:RT, lo:lo + DH] = r * (dxh - xh * jnp.sum(dxh * xh, axis=-1, keepdims=True)) * dsl[:, lo:lo + DH]
        dy_ref[0:RT, 2 * HW:3 * HW] = dv_ref[...] * dsl[:, 2 * HW:3 * HW]
        dy = dy_ref[0:RT, :]
        dx = cw[CONV_K - 1:CONV_K, :] * dy
        dcw_ref[CONV_K - 1:CONV_K, :] += jnp.sum(dy * ext_ref[pl.ds(8, RT), :], axis=0, keepdims=True)
        for s in range(1, CONV_K):
            dx += cw[CONV_K - 1 - s:CONV_K - s, :] * dy_ref[pl.ds(s, RT), :]
            dcw_ref[CONV_K - 1 - s:CONV_K - s, :] += jnp.sum(dy * ext_ref[pl.ds(8 - s, RT), :], axis=0, keepdims=True)
        dx_ref[...] = dx.astype(BF16)
        dy_ref[RT:RT + 8, :] = dy[0:8, :]
        ab = ab_ref[...]
        g, beta = _gd_scalars(ab, al_ref[...], dt_ref[...])
        lane = lax.broadcasted_iota(jnp.int32, (RT, DH), 1)
        dgl = jnp.zeros((RT, DH), F32)
        dbl = jnp.zeros((RT, DH), F32)
        for h in range(NH):
            dgl = jnp.where(lane == h, dg_ref[h], dgl)
            dbl = jnp.where(lane == NH + h, db_ref[h], dbl)
        dsp = dgl * (-jnp.exp(al_ref[...])) * _sigmoid(ab + dt_ref[...])
        dab_ref[...] = (dsp + dbl * beta * (1.0 - beta)).astype(BF16)
        ddt_ref[...] += jnp.sum(dsp, axis=0, keepdims=True)
        dal_ref[...] += jnp.sum(dgl * g, axis=0, keepdims=True)

    hs = pl.BlockSpec((RT, HW), lambda s: (nt - 1 - s, 0))
    sc = pl.BlockSpec((NH, RT, DH), lambda s: (0, nt - 1 - s, 0))
    one = pl.BlockSpec((1, DH), lambda s: (0, 0))
    xs = pl.BlockSpec((RT, CW), lambda s: (nt - 1 - s, 0))
    return _call(body, name="gdn_prep_bwd", grid=(nt,),
                 in_specs=_conv_ext_specs(lambda s: nt - 1 - s) + [
                     pl.BlockSpec((RT, DH), lambda s: (nt - 1 - s, 0)), pl.BlockSpec((CONV_K, CW), lambda s: (0, 0)),
                     one, one, hs, hs, hs, sc, sc],
                 out_specs=[xs, pl.BlockSpec((RT, DH), lambda s: (nt - 1 - s, 0)), pl.BlockSpec((8, CW), lambda s: (0, 0)), one, one],
                 out_shape=[_sds((tp, CW), BF16), _sds((tp, DH), BF16), _sds((8, CW), F32), _sds((1, DH), F32), _sds((1, DH), F32)],
                 scratch=[pltpu.VMEM((RT + 8, CW), F32), pltpu.VMEM((RT + 8, CW), F32)],
                 sem=("arbitrary",))(*([proj] * 6), pab, conv_w, alog, dtb, dq, dk, dv, dgb, dbb)


HS = HB * CH


def _stack_heads(ref, rows, base):
    return jnp.concatenate([ref[rows, (base + hh) * DH:(base + hh + 1) * DH] for hh in range(HB)], axis=0)


def _stack_scal(ref, rows, base):
    return jnp.concatenate([ref[base + hh, rows, :] for hh in range(HB)], axis=0)


def _store_heads(ref, rows, base, val):
    for hh in range(HB):
        ref[rows, (base + hh) * DH:(base + hh + 1) * DH] = val[hh * CH:(hh + 1) * CH].astype(ref.dtype)


def _bd_masks():
    r = lax.broadcasted_iota(jnp.int32, (HS, HS), 0)
    c = lax.broadcasted_iota(jnp.int32, (HS, HS), 1)
    same = lax.shift_right_logical(r, int(math.log2(CH))) == lax.shift_right_logical(c, int(math.log2(CH)))
    return same & (r >= c), same & (r > c)


def _unit_lower_inverse(a):
    n = a.shape[0]
    r = lax.broadcasted_iota(jnp.int32, (n, n), 0)
    c = lax.broadcasted_iota(jnp.int32, (n, n), 1)
    blk_of = lambda t, size: lax.shift_right_logical(t, int(math.log2(size)))
    a16 = jnp.where(blk_of(r, SUB) == blk_of(c, SUB), a, 0.0)
    x = (r == c).astype(F32) - a16
    p = a16
    for _ in range(3):
        p = _dot(p, p)
        x = x + _dot(x, p)
    for blk in (2 * SUB, 4 * SUB):
        off = jnp.where((blk_of(r, blk) == blk_of(c, blk)) & (blk_of(r, blk // 2) != blk_of(c, blk // 2)), a, 0.0)
        x = x - _dot(x, _dot(off, x))
    return x


def _gdn_chunk_common(q, k, v, gl, bt, incl, strict):
    gc = _cumsum(gl)
    e = jnp.exp(gc)
    rel = jnp.exp(jnp.minimum(gc[:, 0:1] - gc.T[0:1, :], 0.0))
    kb = bt * k
    a = jnp.where(strict, bt[:, 0:1] * _dot(k, k, NT) * rel, 0.0)
    x = _unit_lower_inverse(a)
    wu = _dot(x, jnp.concatenate([kb * e, bt * v], axis=1))
    attn = jnp.where(incl, _dot(q, k, NT) * rel, 0.0)
    return gc, e, rel, kb, a, x, wu[:, 0:DH], wu[:, DH:2 * DH], attn


def _gdn_fwd(q, k, v, gb, bb, proj, nw):
    tp = q.shape[0]
    nt, cpt = tp // RT, RT // CH

    def body(q_ref, k_ref, v_ref, g_ref, b_ref, z_ref, nw_ref, og_ref, or_ref, st_ref, s_ref):
        @pl.when(pl.program_id(1) == 0)
        def _():
            s_ref[...] = jnp.zeros((HPS, DH, DH), F32)

        incl, strict = _bd_masks()

        def chunk(c, carry):
            rows = pl.ds(pl.multiple_of(c * CH, CH), CH)
            for base in range(0, HPS, HB):
                qc, kc, vc = _stack_heads(q_ref, rows, base), _stack_heads(k_ref, rows, base), _stack_heads(v_ref, rows, base)
                gc, e, rel, kb, a, x, w, u, attn = _gdn_chunk_common(qc, kc, vc, _stack_scal(g_ref, rows, base),
                                                                    _stack_scal(b_ref, rows, base), incl, strict)
                qe = qc * e
                ws, qs = [], []
                for hh in range(HB):
                    blk = slice(hh * CH, (hh + 1) * CH)
                    s = s_ref[base + hh]
                    st_ref[base + hh, c] = s
                    both = _dot(jnp.concatenate([w[blk], qe[blk]], axis=0), s)
                    ws.append(both[0:CH])
                    qs.append(both[CH:2 * CH])
                vn = u - jnp.concatenate(ws, axis=0)
                o = jnp.concatenate(qs, axis=0) + _dot(attn, vn)
                for hh in range(HB):
                    blk = slice(hh * CH, (hh + 1) * CH)
                    gl = gc[(hh + 1) * CH - 1:(hh + 1) * CH]
                    s_ref[base + hh] = s_ref[base + hh] * jnp.exp(gl) + _dot(kc[blk] * jnp.exp(gl - gc[blk]), vn[blk], TN)
                _store_heads(or_ref, rows, base, o)
                for hh in range(HB):
                    cols = slice((base + hh) * DH, (base + hh + 1) * DH)
                    og_ref[rows, cols] = _gated_norm_fwd(o[hh * CH:(hh + 1) * CH], z_ref[rows, cols], nw_ref[...]).astype(BF16)
            return carry

        lax.fori_loop(0, cpt, chunk, 0)

    col, full, real, state, scal = _head_specs(HPS)
    return _call(body, name="gdn_fwd", grid=(NHB, nt),
                 in_specs=[full, full, full, scal, scal, col(7), pl.BlockSpec((1, DH), lambda h, i: (0, 0))],
                 out_specs=[real, full, state(cpt)],
                 out_shape=[_sds((tp - FRONT, HW), BF16), _sds((tp, HW), F32), _sds((NH, tp // CH, DH, DH), F32)],
                 scratch=[pltpu.VMEM((HPS, DH, DH), F32)], sem=("parallel", "arbitrary"))(q, k, v, gb, bb, proj, nw)


def _gdn_bwd(q, k, v, gb, bb, proj, nw, o_raw, states, dog):
    tp = q.shape[0]
    nt, cpt = tp // RT, RT // CH

    def body(q_ref, k_ref, v_ref, g_ref, b_ref, z_ref, nw_ref, or_ref, st_ref, dog_ref,
             dq_ref, dk_ref, dv_ref, dg_ref, db_ref, dz_ref, dnw_ref, ds_ref, do_ref):
        step = pl.program_id(1)

        @pl.when(step == 0)
        def _():
            ds_ref[...] = jnp.zeros((HPS, DH, DH), F32)

        @pl.when((step == 0) & (pl.program_id(0) == 0))
        def _():
            dnw_ref[...] = jnp.zeros((1, DH), F32)

        front = nt - 1 - step < OFF
        for hh in range(HPS):
            cols = slice(hh * DH, (hh + 1) * DH)
            dog_t = jnp.where(front, 0.0, dog_ref[:, cols])
            do_t, dz_t, dnw = _gated_norm_bwd(or_ref[:, cols], z_ref[:, cols], nw_ref[...], dog_t)
            do_ref[:, cols] = do_t
            dz_ref[:, cols] = dz_t.astype(BF16)
            dnw_ref[...] += dnw
        incl, strict = _bd_masks()
        last_row = (lax.broadcasted_iota(jnp.int32, (CH, 1), 0) == CH - 1)

        def rsum(t):
            return jnp.sum(t, axis=-1, keepdims=True)

        def chunk(cc, carry):
            c = cpt - 1 - cc
            rows = pl.ds(pl.multiple_of(c * CH, CH), CH)
            for base in range(0, HPS, HB):
                qc, kc, vc, do = (_stack_heads(q_ref, rows, base), _stack_heads(k_ref, rows, base), _stack_heads(v_ref, rows, base),
                                  _stack_heads(do_ref, rows, base))
                bt = _stack_scal(b_ref, rows, base)
                gc, e, rel, kb, a, x, w, u, attn = _gdn_chunk_common(qc, kc, vc, _stack_scal(g_ref, rows, base), bt, incl, strict)
                qe = qc * e
                heads = [slice(hh * CH, (hh + 1) * CH) for hh in range(HB)]
                gls = [gc[(hh + 1) * CH - 1:(hh + 1) * CH] for hh in range(HB)]
                cdec = jnp.concatenate([jnp.exp(gl - gc[blk]) for gl, blk in zip(gls, heads)], axis=0)
                kcd = kc * cdec
                vn = u - jnp.concatenate([_dot(w[blk], st_ref[base + hh, c]) for hh, blk in enumerate(heads)], axis=0)
                dos = jnp.concatenate([_dot(do[blk], st_ref[base + hh, c], NT) for hh, blk in enumerate(heads)], axis=0)
                kds = jnp.concatenate([_dot(kcd[blk], ds_ref[base + hh]) for hh, blk in enumerate(heads)], axis=0)
                vds = jnp.concatenate([_dot(vn[blk], ds_ref[base + hh], NT) for hh, blk in enumerate(heads)], axis=0)
                dvn = _dot(attn, do, TN) + kds
                dattn = jnp.where(incl, _dot(do, vn, NT), 0.0)
                dar = dattn * rel
                dq = _dot(dar, kc) + e * dos
                dk = _dot(dar, qc, TN) + cdec * vds
                dc = cdec[:, 0:1] * rsum(kc * vds)
                dgc = rsum(qe * dos) - dc
                dw = -jnp.concatenate([_dot(dvn[blk], st_ref[base + hh, c], NT) for hh, blk in enumerate(heads)], axis=0)
                extra = []
                for hh, blk in enumerate(heads):
                    s, dsn = st_ref[base + hh, c], ds_ref[base + hh]
                    el = jnp.exp(gls[hh])
                    dglast = jnp.sum(dc[blk], axis=0, keepdims=True) + el[:, 0:1] * jnp.sum(rsum(dsn * s), axis=0, keepdims=True)
                    extra.append(jnp.where(last_row, dglast, 0.0))
                    ds_ref[base + hh] = dsn * el + _dot(jnp.concatenate([qe[blk], -w[blk]], axis=0),
                                                 jnp.concatenate([do[blk], dvn[blk]], axis=0), TN)
                dr = _dot(x, jnp.concatenate([dw, dvn], axis=1), TN)
                drw, dru = dr[:, 0:DH], dr[:, DH:2 * DH]
                da = -jnp.where(strict, _dot(dr, jnp.concatenate([w, u], axis=1), NT), 0.0)
                dar2 = da * rel
                dkb = _dot(dar2, kc)
                rwk = rsum(drw * kc)
                dk = dk + _dot(dar2, kb, TN) + bt * dkb + (bt * e) * drw
                dbeta = rsum(dkb * kc) + e[:, 0:1] * rwk + rsum(dru * vc)
                z = dattn * attn + da * a
                dgc = dgc + bt[:, 0:1] * e[:, 0:1] * rwk + rsum(z) - rsum(z.T) + jnp.concatenate(extra, axis=0)
                _store_heads(dq_ref, rows, base, dq)
                _store_heads(dk_ref, rows, base, dk)
                _store_heads(dv_ref, rows, base, bt * dru)
                dg = _rcumsum(jnp.broadcast_to(dgc, (HS, DH)))
                dbb = jnp.broadcast_to(dbeta, (HS, DH))
                for hh, blk in enumerate(heads):
                    dg_ref[base + hh, rows, :] = dg[blk]
                    db_ref[base + hh, rows, :] = dbb[blk]
            return carry

        lax.fori_loop(0, cpt, chunk, 0)

    col, full, real, state, scal = _head_specs(HPS, lambda i: nt - 1 - i)
    one = pl.BlockSpec((1, DH), lambda h, i: (0, 0))
    return _call(body, name="gdn_bwd", grid=(NHB, nt),
                 in_specs=[full, full, full, scal, scal, col(7), one, full, state(cpt), real],
                 out_specs=[full, full, full, scal, scal, full, one],
                 out_shape=[_sds((tp, HW), F32)] * 3 + [_sds((NH, tp, DH), F32)] * 2 + [_sds((tp, HW), BF16), _sds((1, DH), F32)],
                 scratch=[pltpu.VMEM((HPS, DH, DH), F32), pltpu.VMEM((RT, HPS * DH), F32)],
                 sem=("arbitrary", "arbitrary"))(q, k, v, gb, bb, proj, nw, o_raw, states, dog)


MAIN_W = 8 * HW
AB_W = 2 * NH


def _split_w_in(w_in):
    main = jnp.concatenate([w_in[:, :MAIN_W], w_in[:, MAIN_W + AB_W:]], axis=1)
    ab = jnp.pad(w_in[:, MAIN_W:MAIN_W + AB_W], ((0, 0), (0, DH - AB_W)))
    return main, ab


def _pad_lanes(v):
    return jnp.pad(v, ((0, 0), (0, DH - v.shape[1])))


class _NoComm:
    def __init__(self, late):
        self.late = late

    def proj_side(self):
        return None

    def late_weights(self, side_outs):
        return self.late

    def early_grads_side(self, grads):
        return None

    def early_grads_done(self, side_outs):
        pass


def _local_step(x, tgt, meta, lb_logits, mix_w, w_in, hg_nw, conv_w, a_log, dt_bias, gd_nw, ffn_nw, final_w, comm):
    w_main, w_ab = _split_w_in(w_in)
    alog, dtb = _pad_lanes(a_log), _pad_lanes(dt_bias)
    final_w = final_w.reshape(1, -1)
    rows4 = lambda t: t.reshape(4, t.shape[0] // 4, t.shape[1])
    xn = _rms1_fwd(x, meta, mix_w)
    side = comm.proj_side()
    proj = _mm(xn, w_main, "nn", F32, 768, 1024, 2048, "proj_main", n_outer=True, side=side)
    proj, landed = proj if side is not None else (proj, None)
    w_a, w_b, w_out, w_ffn_in, w_ffn_out = comm.late_weights(landed)
    pab = _mm(xn, w_ab, "nn", F32, 768, 128, 2048, "proj_ab")
    oa_g, oa_raw, st_a = _hgrn2_fwd(proj, lb_logits, hg_nw)
    q, k, v, gb, bb = _gdn_prep_fwd(proj, pab, conv_w, alog, dtb)
    ob_g, ob_raw, st_b = _gdn_fwd(q, k, v, gb, bb, proj, gd_nw)
    za = _mm(oa_g, w_a, "nn", F32, 1024, 512, 1024, "branch_a", n_outer=True)
    zb = _mm(ob_g, w_b, "nn", F32, 1024, 512, 1024, "branch_b", n_outer=True)
    merged = _merge_fwd(proj, za, zb)
    mix = _mm(merged, w_out, "nn", F32, 1024, 2048, 2048, "mix_out")
    h1, n2 = _resid_norm_fwd(x, mix, ffn_nw)
    gu = _mm(n2, w_ffn_in, "nn", F32, 1024, 1408, 2048, "ffn_in", n_outer=True)
    act = _swiglu_fwd(gu)
    f = _mm(act, w_ffn_out, "nn", F32, 1024, 2048, 512, "ffn_out")
    lt, dh2, dh2b, dfinal = _loss_head(h1, f, final_w, tgt)
    loss = _sum_tiles(lt)
    dact = _mm(dh2b, w_ffn_out, "nt", F32, 1024, 512, 2048, "d_act", n_outer=True)
    dw_ffn_out = rows4(_mm(act, dh2b, "tn", F32, 512, 2048, 1024, "dw_ffn_out"))
    dgu = _swiglu_bwd(gu, dact)
    dn2 = _mm(dgu, w_ffn_in, "nt", F32, 1024, 2048, 1408, "d_n2")
    dw_ffn_in = _mm(n2, dgu, "tn", F32, 1024, 1408, 1024, "dw_ffn_in", out_shards=4)
    dh1, dh1b, dffn_nw = _resid_norm_bwd(h1, ffn_nw, dn2, dh2)
    dmerged = _mm(dh1b, w_out, "nt", F32, 1024, 2048, 2048, "d_merged")
    dw_out = _mm(merged, dh1b, "tn", F32, 2048, 1024, 1024, "dw_out")
    dza, dzb, dgate = _merge_bwd(proj, za, zb, dmerged)
    doa = _mm(dza, w_a, "nt", F32, 1024, 1024, 512, "d_oa")
    dob = _mm(dzb, w_b, "nt", F32, 1024, 1024, 512, "d_ob")
    dw_a = _mm(oa_g, dza, "tn", F32, 1024, 512, 1024, "dw_branch_a", out_shards=4)
    dw_b = _mm(ob_g, dzb, "tn", F32, 1024, 512, 1024, "dw_branch_b", out_shards=4)
    early = dict(w_ffn_in=dw_ffn_in, w_ffn_out=dw_ffn_out, w_out=rows4(dw_out), w_branch_a=dw_a, w_branch_b=dw_b)
    side = comm.early_grads_side(early)
    hg = _hgrn2_bwd(proj, lb_logits, hg_nw, oa_raw, st_a, doa, side=side)
    if side is not None:
        hg, arrived = hg
        comm.early_grads_done(arrived)
    dhq, dhf, dhi, dhg, dlbl, dhg_nw = hg
    dq, dk, dv, dg, dbeta, dz, dgd_nw = _gdn_bwd(q, k, v, gb, bb, proj, gd_nw, ob_raw, st_b, dob)
    dx3, dab, dconv, dalog, ddtb = _gdn_prep_bwd(proj, pab, conv_w, alog, dtb, dq, dk, dv, dg, dbeta)
    dproj = jnp.concatenate([dhq, dhf, dhi, dhg, dx3, dz, dgate], axis=1)
    dxn = _mm(dproj, w_main, "nt", F32, 768, 2048, 1024, "d_xn")
    dxn = _mm(dab, w_ab, "nt", F32, 768, 2048, 128, "d_xn_ab", add=dxn)
    dw_main = _mm(xn, dproj, "tn", F32, 2048, 1024, 768, "dw_in_main")
    dw_ab = _mm(xn, dab, "tn", F32, 2048, 128, 768, "dw_in_ab")
    dx, dmeta, dmix_w = _rms1_bwd(x, meta, mix_w, dxn, dh1)
    d_model = dw_main.shape[0]
    dw_in = jnp.concatenate([dw_main[:, :MAIN_W], dw_ab[:, :AB_W], dw_main[:, MAIN_W:]], axis=1)
    dw_in = dw_in.reshape(d_model, 4, -1).transpose(1, 0, 2)
    grads = dict(meta_tokens=dmeta, lb_logits=dlbl, mix_norm_w=dmix_w, w_in=dw_in,
                 hg_norm_w=dhg_nw, gd_conv_w=dconv[:CONV_K], gd_a_log=dalog[:, :NH],
                 gd_dt_bias=ddtb[:, :NH], gd_norm_w=dgd_nw, w_branch_a=dw_a, w_branch_b=dw_b,
                 w_out=rows4(dw_out), ffn_norm_w=dffn_nw, w_ffn_in=dw_ffn_in, w_ffn_out=dw_ffn_out,
                 final_norm_w=dfinal.reshape(-1))
    return loss, dx, grads


def _adamw(g, w, m, v, name):
    rows, cols = g.shape
    tr = rows
    for cand in (128, 64, 32, 16, 8):
        if rows % cand == 0 and rows > cand:
            tr = cand
            break

    def body(g_ref, w_ref, m_ref, v_ref, d_ref, nm_ref, nv_ref):
        gg = g_ref[...]
        nm = ADAM_B1 * m_ref[...] + (1.0 - ADAM_B1) * gg
        nv = ADAM_B2 * v_ref[...] + (1.0 - ADAM_B2) * (gg * gg)
        m_hat = nm / (1.0 - ADAM_B1 ** ADAM_STEP)
        v_hat = nv / (1.0 - ADAM_B2 ** ADAM_STEP)
        d_ref[...] = -ADAM_LR * (m_hat / (jnp.sqrt(v_hat) + ADAM_EPS) + ADAM_WD * w_ref[...])
        nm_ref[...] = nm
        nv_ref[...] = nv

    bs = pl.BlockSpec((tr, cols), lambda i: (i, 0))
    return _call(body, name=name, grid=(rows // tr,), in_specs=[bs] * 4, out_specs=[bs] * 3,
                 out_shape=[_sds((rows, cols), F32)] * 3, sem=("parallel",))(g, w, m, v)


HBM = pl.BlockSpec(memory_space=pltpu.HBM)
MESH = pl.DeviceIdType.MESH


def _place():
    x, y, c = lax.axis_index("x"), lax.axis_index("y"), lax.axis_index("c")
    return x, y, c, [(1 - x, y), (x, 1 - y), (1 - x, 1 - y)]


def _comm_call(body, name, out_shape, n_in, scratch):
    return pl.pallas_call(body, name=name, out_shape=out_shape, in_specs=[HBM] * n_in,
                          out_specs=jax.tree.map(lambda _: HBM, out_shape), scratch_shapes=scratch)


def _half_rows(rows, c, tile):
    hh = rows // 2
    assert rows % 2 == 0 and hh % tile == 0, (rows, tile)
    return pl.ds(pl.multiple_of(c * hh, tile), hh)


def _gather_copies(w_refs, out_refs, sems):
    send_sems, recv_sems = sems
    x, y, c, chips = _place()
    s_me = 2 * x + y
    sends, recvs = [], []
    for k, (w_ref, out_ref) in enumerate(zip(w_refs, out_refs)):
        half = _half_rows(w_ref.shape[0], c, 16)
        for j, (cx, cy) in enumerate(chips):
            sem = dict(send_sem=send_sems.at[3 * k + j], recv_sem=recv_sems.at[3 * k + j], device_id=(cx, cy, c), device_id_type=MESH)
            sends.append(pltpu.make_async_remote_copy(src_ref=w_ref.at[half], dst_ref=out_ref.at[s_me, half], **sem))
            recvs.append(pltpu.make_async_remote_copy(src_ref=w_ref.at[half], dst_ref=out_ref.at[2 * cx + cy, half], **sem))
    return sends, recvs


def _gather_sems(n):
    return [pltpu.SemaphoreType.DMA((3 * n,)), pltpu.SemaphoreType.DMA((3 * n,))]


def _gather_start(w_refs, out_refs, sems):
    for cp in _gather_copies(w_refs, out_refs, sems)[0]:
        cp.start()


def _gather_wait(w_refs, out_refs, sems):
    sends, recvs = _gather_copies(w_refs, out_refs, sems)
    for cp in recvs:
        cp.wait_recv()
    for cp in sends:
        cp.wait_send()


def _gather_chips(shards):
    n = len(shards)

    def body(*refs):
        _gather_start(refs[:n], refs[n:2 * n], refs[2 * n:])
        _gather_wait(refs[:n], refs[n:2 * n], refs[2 * n:])

    return _comm_call(body, "gather_chips", [_sds((4,) + w.shape, w.dtype) for w in shards], n, _gather_sems(n))(*shards)


def _gather_side(shards):
    return _Side(shards, [_sds((4,) + w.shape, w.dtype) for w in shards], _gather_sems(len(shards)), _gather_start, _gather_wait)


def _forward_halves(outs, name):
    n = len(outs)

    def body(*refs):
        out_refs = refs[n:2 * n]
        send_sems, recv_sems = refs[2 * n:]
        x, y, c, chips = _place()
        cps = []
        for k in range(n):
            rows = out_refs[k].shape[1]
            half, other = _half_rows(rows, c, 16), _half_rows(rows, 1 - c, 16)
            for j, (cx, cy) in enumerate(chips):
                sem = dict(send_sem=send_sems.at[3 * k + j], recv_sem=recv_sems.at[3 * k + j], device_id=(x, y, 1 - c), device_id_type=MESH)
                landed = out_refs[k].at[2 * cx + cy, half]
                cps.append(pltpu.make_async_remote_copy(src_ref=landed, dst_ref=landed, **sem))
                cps[-1].start()
        for k in range(n):
            rows = out_refs[k].shape[1]
            half, other = _half_rows(rows, c, 16), _half_rows(rows, 1 - c, 16)
            for j, (cx, cy) in enumerate(chips):
                sem = dict(send_sem=send_sems.at[3 * k + j], recv_sem=recv_sems.at[3 * k + j], device_id=(x, y, 1 - c), device_id_type=MESH)
                pltpu.make_async_remote_copy(src_ref=out_refs[k].at[2 * cx + cy, half], dst_ref=out_refs[k].at[2 * cx + cy, other], **sem).wait_recv()
        for cp in cps:
            cp.wait_send()

    shapes = [_sds(o.shape, o.dtype) for o in outs]
    return pl.pallas_call(body, name=name, out_shape=shapes, in_specs=[HBM] * n, out_specs=[HBM] * n,
                          input_output_aliases={k: k for k in range(n)},
                          scratch_shapes=[pltpu.SemaphoreType.DMA((3 * n,)), pltpu.SemaphoreType.DMA((3 * n,))])(*outs)


def _swap_halves(gs, name):
    n = len(gs)

    def body(*refs):
        g_refs, out_refs = refs[:n], refs[n:2 * n]
        send_sems, recv_sems = refs[2 * n:]
        x, y, c, _ = _place()
        cps = []
        for k in range(n):
            other = _half_rows(g_refs[k].shape[1], 1 - c, 8)
            cps.append(pltpu.make_async_remote_copy(src_ref=g_refs[k].at[:, other, :], dst_ref=out_refs[k], send_sem=send_sems.at[k],
                                                    recv_sem=recv_sems.at[k], device_id=(x, y, 1 - c), device_id_type=MESH))
            cps[-1].start()
        for cp in cps:
            cp.wait()

    return _comm_call(body, name, [_sds((4, g.shape[1] // 2, g.shape[2]), g.dtype) for g in gs], n,
                      [pltpu.SemaphoreType.DMA((n,)), pltpu.SemaphoreType.DMA((n,))])(*gs)


def _row_tile(rows, row_bytes, budget=3 << 20):
    if rows * row_bytes <= budget:
        return rows
    return max(t for t in range(16, rows, 16) if rows % t == 0 and t * row_bytes <= budget)


def _add_half(g, got, c, name):
    _, rows, cols = g.shape
    hh = rows // 2
    tr = _row_tile(hh, cols * 4)
    nb = hh // tr

    def body(c_ref, a_ref, b_ref, o_ref):
        o_ref[...] = (a_ref[...] + b_ref[...]).astype(BF16)

    gs = pltpu.PrefetchScalarGridSpec(
        num_scalar_prefetch=1, grid=(4, nb),
        in_specs=[pl.BlockSpec((1, tr, cols), lambda s, i, c_ref: (s, c_ref[0] * nb + i, 0)),
                  pl.BlockSpec((1, tr, cols), lambda s, i, c_ref: (s, i, 0))],
        out_specs=pl.BlockSpec((1, tr, cols), lambda s, i, c_ref: (s, i, 0)))
    return pl.pallas_call(body, name=name, grid_spec=gs, out_shape=_sds((4, hh, cols), BF16),
                          compiler_params=pltpu.CompilerParams(dimension_semantics=("parallel", "parallel"),
                                                               vmem_limit_bytes=VMEM_LIMIT))(c, g, got)


def _scatter_copies(p_refs, out_refs, sems):
    send_sems, recv_sems = sems
    x, y, c, chips = _place()
    s_me = 2 * x + y
    cps = []
    for k, (p_ref, out_ref) in enumerate(zip(p_refs, out_refs)):
        for j, (cx, cy) in enumerate(chips):
            cps.append(pltpu.make_async_remote_copy(src_ref=p_ref.at[2 * cx + cy], dst_ref=out_ref.at[s_me],
                                                    send_sem=send_sems.at[3 * k + j], recv_sem=recv_sems.at[3 * k + j],
                                                    device_id=(cx, cy, c), device_id_type=MESH))
    return cps


def _scatter_start(p_refs, out_refs, sems):
    for cp in _scatter_copies(p_refs, out_refs, sems):
        cp.start()


def _scatter_wait(p_refs, out_refs, sems):
    for cp in _scatter_copies(p_refs, out_refs, sems):
        cp.wait()


def _scatter_chips(ps):
    n = len(ps)

    def body(*refs):
        _scatter_start(refs[:n], refs[n:2 * n], refs[2 * n:])
        _scatter_wait(refs[:n], refs[n:2 * n], refs[2 * n:])

    return _comm_call(body, "scatter_chips", [_sds(p_.shape, p_.dtype) for p_ in ps], n, _gather_sems(n))(*ps)


def _scatter_side(ps):
    return _Side(ps, [_sds(p_.shape, p_.dtype) for p_ in ps], _gather_sems(len(ps)), _scatter_start, _scatter_wait)


def _sum_slabs(b, name):
    n, h, wd = b.shape
    tr = _row_tile(h, n * wd * 4, 6 << 20)

    def body(b_ref, o_ref):
        acc = b_ref[0]
        for s in range(1, n):
            acc = acc + b_ref[s]
        o_ref[...] = acc

    return _call(body, name=name, grid=(h // tr,), in_specs=[pl.BlockSpec((n, tr, wd), lambda i: (0, i, 0))],
                 out_specs=pl.BlockSpec((tr, wd), lambda i: (i, 0)), out_shape=_sds((h, wd), F32), sem=("parallel",))(b)


def _sum_chips(arrived, own, name):
    n, h, wd = arrived.shape
    tr = _row_tile(h, n * wd * 2, 6 << 20)
    nb = h // tr
    my_chip = lambda: 2 * lax.axis_index("x") + lax.axis_index("y")

    def body(b_ref, p_ref, o_ref):
        acc = None
        for s in range(n):
            term = jnp.where(my_chip() == s, p_ref[0], b_ref[s]).astype(F32)
            acc = term if acc is None else acc + term
        o_ref[...] = acc

    return _call(body, name=name, grid=(nb,),
                 in_specs=[pl.BlockSpec((n, tr, wd), lambda i: (0, i, 0)), pl.BlockSpec((1, tr, wd), lambda i: (my_chip(), i, 0))],
                 out_specs=pl.BlockSpec((tr, wd), lambda i: (lax.axis_index("c") * nb + i, 0)),
                 out_shape=_sds((2 * h, wd), F32), sem=("parallel",))(arrived, own)


def _share_halves(gs):
    n = len(gs)

    def body(*refs):
        out_refs = refs[n:2 * n]
        send_sems, recv_sems = refs[2 * n:]
        x, y, c, _ = _place()
        cps = []
        for k in range(n):
            half, other = _half_rows(out_refs[k].shape[0], c, 8), _half_rows(out_refs[k].shape[0], 1 - c, 8)
            sem = dict(send_sem=send_sems.at[k], recv_sem=recv_sems.at[k], device_id=(x, y, 1 - c), device_id_type=MESH)
            cps.append((pltpu.make_async_remote_copy(src_ref=out_refs[k].at[half], dst_ref=out_refs[k].at[half], **sem),
                        pltpu.make_async_remote_copy(src_ref=out_refs[k].at[half], dst_ref=out_refs[k].at[other], **sem)))
            cps[-1][0].start()
        for send, recv in cps:
            recv.wait_recv()
            send.wait_send()

    return pl.pallas_call(body, name="share_halves", out_shape=[_sds(g.shape, g.dtype) for g in gs], in_specs=[HBM] * n,
                          out_specs=[HBM] * n, input_output_aliases={k: k for k in range(n)},
                          scratch_shapes=[pltpu.SemaphoreType.DMA((n,)), pltpu.SemaphoreType.DMA((n,))])(*gs)


def _gather_all(v, name):
    def body(v_ref, out_ref, send_sems, recv_sems, local_sem):
        x, y, c = lax.axis_index("x"), lax.axis_index("y"), lax.axis_index("c")
        me = 4 * x + 2 * y + c
        flip = lambda t, d: 1 - t if d else t
        mine = pltpu.make_async_copy(v_ref, out_ref.at[me], local_sem)
        mine.start()
        cps = []
        for k in range(1, 8):
            to = (flip(x, k & 4), flip(y, k & 2), flip(c, k & 1))
            cps.append(pltpu.make_async_remote_copy(src_ref=v_ref, dst_ref=out_ref.at[me], send_sem=send_sems.at[k - 1],
                                                    recv_sem=recv_sems.at[k - 1], device_id=to, device_id_type=MESH))
        for cp in cps:
            cp.start()
        for cp in cps:
            cp.wait()
        mine.wait()

    return _comm_call(body, name, _sds((8,) + v.shape, v.dtype), 1,
                      [pltpu.SemaphoreType.DMA((7,)), pltpu.SemaphoreType.DMA((7,)), pltpu.SemaphoreType.DMA])(v)


BIG = (("w_in", 1), ("w_branch_a", 1), ("w_branch_b", 1), ("w_out", 0), ("w_ffn_in", 1), ("w_ffn_out", 0))
SMALL = ("meta_tokens", "lb_logits", "mix_norm_w", "hg_norm_w", "gd_conv_w", "gd_a_log", "gd_dt_bias", "gd_norm_w",
         "ffn_norm_w", "final_norm_w")


def _pack_lanes(parts):
    rows = []
    for p in parts:
        f = p.reshape(-1).astype(F32)
        n = -(-f.shape[0] // DH) * DH
        rows.append(jnp.pad(f, (0, n - f.shape[0])).reshape(-1, DH))
    buf = jnp.concatenate(rows, axis=0)
    return jnp.pad(buf, ((0, -buf.shape[0] % 8), (0, 0)))


def _unpack_lanes(buf, shapes):
    out, off = [], 0
    for shp in shapes:
        n = math.prod(shp)
        r = -(-n // DH)
        out.append(buf[off:off + r].reshape(-1)[:n].reshape(shp))
        off += r
    return out


def kernel(x, meta_tokens, lb_logits, mix_norm_w, w_in, hg_norm_w, gd_conv_w, gd_a_log, gd_dt_bias, gd_norm_w, w_branch_a, w_branch_b, w_out, ffn_norm_w, w_ffn_in, w_ffn_out, final_norm_w, loss_target, m_meta_tokens, m_lb_logits, m_mix_norm_w, m_w_in, m_hg_norm_w, m_gd_conv_w, m_gd_a_log, m_gd_dt_bias, m_gd_norm_w, m_w_branch_a, m_w_branch_b, m_w_out, m_ffn_norm_w, m_w_ffn_in, m_w_ffn_out, m_final_norm_w, v_meta_tokens, v_lb_logits, v_mix_norm_w, v_w_in, v_hg_norm_w, v_gd_conv_w, v_gd_a_log, v_gd_dt_bias, v_gd_norm_w, v_w_branch_a, v_w_branch_b, v_w_out, v_ffn_norm_w, v_w_ffn_in, v_w_ffn_out, v_final_norm_w):
    args = dict(locals())
    big = [n for n, _ in BIG]
    w = {n: args[n] for n in SMALL + tuple(big)}
    m = {n: args["m_" + n] for n in w}
    v = {n: args["v_" + n] for n in w}
    xi, yi, ci = lax.axis_index("x"), lax.axis_index("y"), lax.axis_index("c")
    shard = 2 * xi + yi
    big_local = {n: w[n][0] for n in big}

    meta_cols, conv_cols = meta_tokens.shape[1], gd_conv_w.shape[-1]
    sm_all = _gather_all(_pack_lanes([meta_tokens, gd_conv_w[0]]), "gather_meta")
    sm_parts = [_unpack_lanes(sm_all[2 * s], [meta_tokens.shape, gd_conv_w[0].shape]) for s in range(4)]
    meta_full = jnp.concatenate([p[0] for p in sm_parts], axis=1)
    conv_full = jnp.concatenate([p[1] for p in sm_parts], axis=1)
    cvec = ci.reshape(1).astype(jnp.int32)
    late = [n for n in big if n != "w_in"]
    rows_full = lambda t: t.reshape(t.shape[0] * t.shape[1], t.shape[2])

    def pair_sums(names, gs):
        return [_add_half(gk, got, cvec, "add_half_" + n) for n, gk, got in zip(names, gs, _swap_halves(gs, "swap_" + names[0]))]

    def with_own(slabs, n):
        return lax.dynamic_update_index_in_dim(slabs, big_local[n].astype(BF16), shard, 0)

    class MeshComm:
        def proj_side(self):
            return _gather_side([big_local[n].astype(BF16) for n in late])

        def late_weights(self, landed):
            wl = {n: with_own(t, n) for n, t in zip(late, _forward_halves(landed, "forward_late"))}
            return (wl["w_branch_a"], wl["w_branch_b"], rows_full(wl["w_out"]), wl["w_ffn_in"], rows_full(wl["w_ffn_out"]))

        def early_grads_side(self, grads):
            self.early = list(grads)
            self.early_parts = pair_sums(self.early, [grads[n] for n in self.early])
            return _scatter_side(self.early_parts)

        def early_grads_done(self, arrived):
            self.early_arrived = arrived

    comm = MeshComm()
    w_in_slabs = with_own(_forward_halves(_gather_chips([big_local["w_in"].astype(BF16)]), "forward_w_in")[0], "w_in")
    d_model = w_in_slabs.shape[1]
    w_in_full = w_in_slabs.transpose(1, 0, 2).reshape(d_model, -1)
    loss, dx, g = _local_step(x[0], loss_target[0], meta_full, lb_logits, mix_norm_w, w_in_full, hg_norm_w, conv_full,
                              gd_a_log, gd_dt_bias, gd_norm_w, ffn_norm_w, final_norm_w, comm)
    loss = lax.psum(loss[0, 0], ("x", "y", "c"))

    rest = [n for n in big if n not in comm.early]
    parts = dict(zip(rest, pair_sums(rest, [g[n] for n in rest])))
    arrived = dict(zip(rest, _scatter_chips([parts[n] for n in rest])))
    parts.update(zip(comm.early, comm.early_parts))
    arrived.update(zip(comm.early, comm.early_arrived))
    g_big = dict(zip(big, _share_halves([_sum_chips(arrived[n], parts[n], "sum_chips_" + n) for n in big])))

    small_shapes = [g[n].shape for n in SMALL]
    g_all = _gather_all(_pack_lanes([g[n] for n in SMALL]), "gather_small")
    g_small = dict(zip(SMALL, _unpack_lanes(_sum_slabs(g_all, "sum_small"), small_shapes)))
    g_small["meta_tokens"] = lax.dynamic_slice_in_dim(g_small["meta_tokens"], shard * meta_cols, meta_cols, axis=1)
    g_small["gd_conv_w"] = lax.dynamic_slice_in_dim(g_small["gd_conv_w"], shard * conv_cols, conv_cols, axis=1)

    grad, delta, new_m, new_v = {}, {}, {}, {}
    for n in big:
        grad[n] = g_big[n].reshape(w[n].shape)
        d_, m_, v_ = _adamw(g_big[n], big_local[n], m[n][0], v[n][0], "adamw_" + n)
        delta[n], new_m[n], new_v[n] = d_.reshape(w[n].shape), m_.reshape(w[n].shape), v_.reshape(w[n].shape)
    local_shapes = [w[n].shape for n in SMALL]
    d_, m_, v_ = _adamw(_pack_lanes([g_small[n] for n in SMALL]), _pack_lanes([w[n] for n in SMALL]),
                        _pack_lanes([m[n] for n in SMALL]), _pack_lanes([v[n] for n in SMALL]), "adamw_small")
    for n, gs_, dd, mm, vv in zip(SMALL, [g_small[n] for n in SMALL], _unpack_lanes(d_, local_shapes), _unpack_lanes(m_, local_shapes),
                                  _unpack_lanes(v_, local_shapes)):
        grad[n], delta[n], new_m[n], new_v[n] = gs_.reshape(w[n].shape), dd, mm, vv

    order = ["meta_tokens", "lb_logits", "mix_norm_w", "w_in", "hg_norm_w", "gd_conv_w", "gd_a_log", "gd_dt_bias", "gd_norm_w",
             "w_branch_a", "w_branch_b", "w_out", "ffn_norm_w", "w_ffn_in", "w_ffn_out", "final_norm_w"]
    return (loss, dx[None], *[grad[n] for n in order], *[delta[n] for n in order], *[new_m[n] for n in order],
            *[new_v[n] for n in order])
```

```python
import functools
import math

import jax
import jax.numpy as jnp
from jax import lax
from jax.experimental import pallas as pl
from jax.experimental.pallas import tpu as pltpu

F32, BF16 = jnp.float32, jnp.bfloat16
HI = lax.Precision.HIGHEST
EPS = 1e-6
D_MODEL = 2048
N_META = 16
FRONT = 256
CH = 64
SUB = 16
SUBH = 8
DH = 128
NH = 8
HW = NH * DH
CONV_K = 4
RT = 256
VMEM_LIMIT = 56 * 1024 * 1024
ADAM_LR, ADAM_B1, ADAM_B2, ADAM_EPS, ADAM_WD, ADAM_STEP = 0.001, 0.9, 0.999, 1e-08, 0.01, 10

NN = (((1,), (0,)), ((), ()))
NT = (((1,), (1,)), ((), ()))
TN = (((0,), (0,)), ((), ()))


def _dot(a, b, dn=NN):
    return lax.dot_general(a.astype(BF16), b.astype(BF16), dn, preferred_element_type=F32)


def _dotx(a, b, dn=NN):
    return lax.dot_general(a, b, dn, precision=HI, preferred_element_type=F32)


class _Side:
    def __init__(self, inputs, out_shapes, scratch, start, wait):
        self.inputs, self.out_shapes, self.scratch, self.start, self.wait = inputs, out_shapes, scratch, start, wait


def _call(body, *, name, grid, in_specs, out_specs, out_shape, scratch=(), sem=None, side=None):
    params = pltpu.CompilerParams(dimension_semantics=sem, vmem_limit_bytes=VMEM_LIMIT)
    if side is None:
        return pl.pallas_call(body, name=name, grid=grid, in_specs=in_specs, out_specs=out_specs, out_shape=out_shape,
                              scratch_shapes=list(scratch), compiler_params=params)
    single = not isinstance(out_specs, (list, tuple))
    out_specs, out_shape = ([out_specs], [out_shape]) if single else (list(out_specs), list(out_shape))
    ni, no, ns = len(in_specs), len(out_specs), len(scratch)
    nsi, nso = len(side.inputs), len(side.out_shapes)
    hbm = pl.BlockSpec(memory_space=pltpu.HBM)

    def wrapped(*refs):
        main_in, side_in = refs[:ni], refs[ni:ni + nsi]
        main_out, side_out = refs[ni + nsi:ni + nsi + no], refs[ni + nsi + no:ni + nsi + no + nso]
        main_scr, side_scr = refs[ni + nsi + no + nso:ni + nsi + no + nso + ns], refs[ni + nsi + no + nso + ns:]
        pids = [pl.program_id(d) for d in range(len(grid))]
        first = functools.reduce(lambda a, b: a & b, [p == 0 for p in pids])
        last = functools.reduce(lambda a, b: a & b, [p == g - 1 for p, g in zip(pids, grid)])

        @pl.when(first)
        def _():
            side.start(side_in, side_out, side_scr)

        body(*main_in, *main_out, *main_scr)

        @pl.when(last)
        def _():
            side.wait(side_in, side_out, side_scr)

    call = pl.pallas_call(wrapped, name=name, grid=grid, in_specs=list(in_specs) + [hbm] * nsi,
                          out_specs=out_specs + [hbm] * nso, out_shape=out_shape + list(side.out_shapes),
                          scratch_shapes=list(scratch) + list(side.scratch), compiler_params=params)

    def run(*args):
        outs = call(*args, *side.inputs)
        main = outs[0] if single else list(outs[:no])
        return main, list(outs[no:])

    return run


def _divmod(j, per):
    if per == 1:
        return j, 0
    return lax.div(j, jnp.int32(per)), lax.rem(j, jnp.int32(per))


def _sds(shape, dtype):
    return jax.ShapeDtypeStruct(tuple(shape), dtype)


def _sigmoid(x):
    return 1.0 / (1.0 + jnp.exp(-x))


def _silu(x):
    return x * _sigmoid(x)


def _dsilu(x):
    s = _sigmoid(x)
    return s * (1.0 + x * (1.0 - s))


def _tri(n, kind):
    r = lax.broadcasted_iota(jnp.int32, (n, n), 0)
    c = lax.broadcasted_iota(jnp.int32, (n, n), 1)
    return {"incl": r >= c, "strict": r > c, "upper": c >= r}[kind]


def _mm(a, b, mode, out_dtype, tm, tn, tk, name, add=None, n_outer=False, out_shards=0, side=None):
    sharded_b = b.ndim == 3
    if sharded_b:
        S, R, n = b.shape
        b_rows, b_cols = R, S * n
    else:
        b_rows, b_cols = b.shape
    if mode == "nn":
        (M, K), N, dn = a.shape, b_cols, NN
    elif mode == "nt":
        (M, K), N, dn = a.shape, b_rows, NT
    else:
        (K, M), N, dn = a.shape, b_cols, TN
    tm, tn, tk = min(tm, M), min(tn, N), min(tk, K)
    if sharded_b:
        tn, tk = (min(tn, n), tk) if mode != "nt" else (tn, min(tk, n))
    if out_shards:
        tn = min(tn, N // out_shards)
    assert M % tm == 0 and N % tn == 0 and K % tk == 0, (name, M, N, K, tm, tn, tk)
    nk = K // tk
    a_blk, a_idx = ((tm, tk), lambda i, j, k: (i, k)) if mode != "tn" else ((tk, tm), lambda i, j, k: (k, i))
    if not sharded_b:
        b_blk, b_idx = ((tk, tn), lambda i, j, k: (k, j)) if mode != "nt" else ((tn, tk), lambda i, j, k: (j, k))
    elif mode != "nt":
        per = n // tn
        assert n % tn == 0
        b_blk, b_idx = (None, tk, tn), lambda i, j, k: (_divmod(j, per)[0], k, _divmod(j, per)[1])
    else:
        per = n // tk
        assert n % tk == 0
        b_blk, b_idx = (None, tn, tk), lambda i, j, k: (_divmod(k, per)[0], j, _divmod(k, per)[1])
    if out_shards:
        per_o = N // out_shards // tn
        assert (N // out_shards) % tn == 0
        o_blk, o_idx = (None, tm, tn), lambda i, j, k: (_divmod(j, per_o)[0], i, _divmod(j, per_o)[1])
        o_shape = (out_shards, M, N // out_shards)
    else:
        o_blk, o_idx, o_shape = (tm, tn), (lambda i, j, k: (i, j)), (M, N)
    c_idx = lambda i, j, k: (i, j)
    if n_outer:
        sw = lambda f: (lambda j, i, k: f(i, j, k))
        a_idx, b_idx, o_idx, c_idx = sw(a_idx), sw(b_idx), sw(o_idx), sw(c_idx)
        grid = (N // tn, M // tm, nk)
    else:
        grid = (M // tm, N // tn, nk)
    has_add = add is not None

    def body(*refs):
        if has_add:
            a_ref, b_ref, c_ref, o_ref, acc_ref = refs
        else:
            a_ref, b_ref, o_ref, acc_ref = refs
            c_ref = None
        part = lax.dot_general(a_ref[...].astype(BF16), b_ref[...].astype(BF16), dn, preferred_element_type=F32)

        def fin(val):
            if has_add:
                val = val + c_ref[...]
            o_ref[...] = val.astype(out_dtype)

        if nk == 1:
            fin(part)
        else:
            k = pl.program_id(2)

            @pl.when(k == 0)
            def _():
                acc_ref[...] = part

            @pl.when(k > 0)
            def _():
                acc_ref[...] += part

            @pl.when(k == nk - 1)
            def _():
                fin(acc_ref[...])

    in_specs = [pl.BlockSpec(a_blk, a_idx), pl.BlockSpec(b_blk, b_idx)]
    args = [a, b]
    if has_add:
        in_specs.append(pl.BlockSpec((tm, tn), c_idx))
        args.append(add)
    acc_shape = (tm, tn) if nk > 1 else (8, 128)
    return _call(body, name=name, grid=grid, in_specs=in_specs, out_specs=pl.BlockSpec(o_blk, o_idx),
                 out_shape=_sds(o_shape, out_dtype), scratch=[pltpu.VMEM(acc_shape, F32)],
                 sem=("arbitrary",) * 3 if side is not None else ("parallel", "parallel", "arbitrary"), side=side)(*args)


def _rms1_fwd(x, meta, w):
    seq, d = x.shape
    nt = (FRONT + seq) // RT

    def body(x_ref, m_ref, w_ref, o_ref):
        i = pl.program_id(0)

        def norm(v):
            r = lax.rsqrt(jnp.mean(v * v, axis=-1, keepdims=True) + EPS)
            return (v * r * w_ref[...]).astype(BF16)

        @pl.when(i == 0)
        def _():
            o_ref[0:RT - N_META, :] = jnp.zeros((RT - N_META, d), BF16)
            o_ref[RT - N_META:RT, :] = norm(m_ref[...])

        @pl.when(i > 0)
        def _():
            o_ref[...] = norm(x_ref[...])

    return _call(body, name="rms1_fwd", grid=(nt,),
                 in_specs=[pl.BlockSpec((RT, d), lambda i: (jnp.maximum(i - 1, 0), 0)),
                           pl.BlockSpec((N_META, d), lambda i: (0, 0)),
                           pl.BlockSpec((1, d), lambda i: (0, 0))],
                 out_specs=pl.BlockSpec((RT, d), lambda i: (i, 0)),
                 out_shape=_sds((FRONT + seq, d), BF16), sem=("parallel",))(x, meta, w)


def _rms1_bwd(x, meta, w, dxn, dh1):
    seq, d = x.shape
    nt = (FRONT + seq) // RT

    def body(x_ref, m_ref, w_ref, g_ref, r_ref, dx_ref, dm_ref, dw_ref):
        i = pl.program_id(0)

        def bwd(v, g):
            r = lax.rsqrt(jnp.mean(v * v, axis=-1, keepdims=True) + EPS)
            vh = v * r
            gh = g * w_ref[...]
            return r * (gh - vh * jnp.mean(gh * vh, axis=-1, keepdims=True)), jnp.sum(g * vh, axis=0, keepdims=True)

        @pl.when(i == 0)
        def _():
            dm, dw = bwd(m_ref[...], g_ref[RT - N_META:RT, :])
            dm_ref[...] = dm
            dw_ref[...] = dw

        @pl.when(i > 0)
        def _():
            dx, dw = bwd(x_ref[...], g_ref[...])
            dx_ref[...] = dx + r_ref[...]
            dw_ref[...] += dw

    xs = pl.BlockSpec((RT, d), lambda i: (jnp.maximum(i - 1, 0), 0))
    return _call(body, name="rms1_bwd", grid=(nt,),
                 in_specs=[xs, pl.BlockSpec((N_META, d), lambda i: (0, 0)), pl.BlockSpec((1, d), lambda i: (0, 0)),
                           pl.BlockSpec((RT, d), lambda i: (i, 0)), xs],
                 out_specs=[xs, pl.BlockSpec((N_META, d), lambda i: (0, 0)), pl.BlockSpec((1, d), lambda i: (0, 0))],
                 out_shape=[_sds((seq, d), F32), _sds((N_META, d), F32), _sds((1, d), F32)],
                 sem=("arbitrary",))(x, meta, w, dxn, dh1)


def _merge_fwd(proj, za, zb):
    seq, d = za.shape
    off = FRONT // RT
    ca, cb = 8 * HW // d, 8 * HW // d + 1

    def body(ga_ref, gb_ref, za_ref, zb_ref, o_ref):
        o_ref[...] = (_sigmoid(ga_ref[...]) * za_ref[...] + _sigmoid(gb_ref[...]) * zb_ref[...]).astype(BF16)

    zs = pl.BlockSpec((RT, d), lambda i: (i, 0))
    return _call(body, name="merge_fwd", grid=(seq // RT,),
                 in_specs=[pl.BlockSpec((RT, d), lambda i: (i + off, ca)), pl.BlockSpec((RT, d), lambda i: (i + off, cb)), zs, zs],
                 out_specs=zs, out_shape=_sds((seq, d), BF16), sem=("parallel",))(proj, proj, za, zb)


def _merge_bwd(proj, za, zb, dmerged):
    seq, d = za.shape
    off = FRONT // RT
    ca, cb = 8 * HW // d, 8 * HW // d + 1
    nt = (FRONT + seq) // RT

    def body(ga_ref, gb_ref, za_ref, zb_ref, dm_ref, dza_ref, dzb_ref, dg_ref):
        i = pl.program_id(0)

        @pl.when(i < off)
        def _():
            dg_ref[...] = jnp.zeros((RT, 2 * d), BF16)

        @pl.when(i >= off)
        def _():
            sa, sb, dm = _sigmoid(ga_ref[...]), _sigmoid(gb_ref[...]), dm_ref[...]
            dza_ref[...] = (dm * sa).astype(BF16)
            dzb_ref[...] = (dm * sb).astype(BF16)
            dg_ref[:, 0:d] = (dm * za_ref[...] * sa * (1.0 - sa)).astype(BF16)
            dg_ref[:, d:2 * d] = (dm * zb_ref[...] * sb * (1.0 - sb)).astype(BF16)

    rs = pl.BlockSpec((RT, d), lambda i: (jnp.maximum(i - off, 0), 0))
    return _call(body, name="merge_bwd", grid=(nt,),
                 in_specs=[pl.BlockSpec((RT, d), lambda i: (i, ca)), pl.BlockSpec((RT, d), lambda i: (i, cb)), rs, rs, rs],
                 out_specs=[rs, rs, pl.BlockSpec((RT, 2 * d), lambda i: (i, 0))],
                 out_shape=[_sds((seq, d), BF16), _sds((seq, d), BF16), _sds((FRONT + seq, 2 * d), BF16)],
                 sem=("arbitrary",))(proj, proj, za, zb, dmerged)


def _resid_norm_fwd(x, mix, w):
    seq, d = x.shape

    def body(x_ref, m_ref, w_ref, h_ref, n_ref):
        h = x_ref[...] + m_ref[...]
        h_ref[...] = h
        r = lax.rsqrt(jnp.mean(h * h, axis=-1, keepdims=True) + EPS)
        n_ref[...] = (h * r * w_ref[...]).astype(BF16)

    rs = pl.BlockSpec((RT, d), lambda i: (i, 0))
    return _call(body, name="resid_norm_fwd", grid=(seq // RT,),
                 in_specs=[rs, rs, pl.BlockSpec((1, d), lambda i: (0, 0))], out_specs=[rs, rs],
                 out_shape=[_sds((seq, d), F32), _sds((seq, d), BF16)], sem=("parallel",))(x, mix, w)


def _resid_norm_bwd(h1, w, dn2, dh2):
    seq, d = h1.shape

    def body(h_ref, w_ref, g_ref, r_ref, o_ref, ob_ref, dw_ref):
        i = pl.program_id(0)
        h, g = h_ref[...], g_ref[...]
        r = lax.rsqrt(jnp.mean(h * h, axis=-1, keepdims=True) + EPS)
        hh = h * r
        gh = g * w_ref[...]
        dh = r_ref[...] + r * (gh - hh * jnp.mean(gh * hh, axis=-1, keepdims=True))
        o_ref[...] = dh
        ob_ref[...] = dh.astype(BF16)
        dw = jnp.sum(g * hh, axis=0, keepdims=True)

        @pl.when(i == 0)
        def _():
            dw_ref[...] = dw

        @pl.when(i > 0)
        def _():
            dw_ref[...] += dw

    rs = pl.BlockSpec((RT, d), lambda i: (i, 0))
    ws = pl.BlockSpec((1, d), lambda i: (0, 0))
    return _call(body, name="resid_norm_bwd", grid=(seq // RT,), in_specs=[rs, ws, rs, rs], out_specs=[rs, rs, ws],
                 out_shape=[_sds((seq, d), F32), _sds((seq, d), BF16), _sds((1, d), F32)], sem=("arbitrary",))(h1, w, dn2, dh2)


def _swiglu_fwd(gu):
    seq, f2 = gu.shape
    ff = f2 // 2
    tc = 512
    nb = ff // tc

    def body(g_ref, u_ref, o_ref):
        o_ref[...] = (_silu(g_ref[...].astype(F32)) * u_ref[...].astype(F32)).astype(BF16)

    return _call(body, name="swiglu_fwd", grid=(seq // RT, nb),
                 in_specs=[pl.BlockSpec((RT, tc), lambda i, j: (i, j)), pl.BlockSpec((RT, tc), lambda i, j: (i, j + nb))],
                 out_specs=pl.BlockSpec((RT, tc), lambda i, j: (i, j)), out_shape=_sds((seq, ff), BF16),
                 sem=("parallel", "parallel"))(gu, gu)


def _swiglu_bwd(gu, dact):
    seq, f2 = gu.shape
    ff = f2 // 2
    tc = 512
    nb = ff // tc

    def body(g_ref, u_ref, d_ref, dg_ref, du_ref):
        g, d = g_ref[...].astype(F32), d_ref[...].astype(F32)
        dg_ref[...] = (d * u_ref[...].astype(F32) * _dsilu(g)).astype(BF16)
        du_ref[...] = (d * _silu(g)).astype(BF16)

    bs = pl.BlockSpec((RT, tc), lambda i, j: (i, j))
    dgate, dup = _call(body, name="swiglu_bwd", grid=(seq // RT, nb),
                       in_specs=[bs, pl.BlockSpec((RT, tc), lambda i, j: (i, j + nb)), bs], out_specs=[bs, bs],
                       out_shape=[_sds((seq, ff), BF16), _sds((seq, ff), BF16)], sem=("parallel", "parallel"))(gu, gu, dact)
    return jnp.concatenate([dgate, dup], axis=1)


def _loss_head(h1, f, w, tgt):
    seq, d = h1.shape
    nt = seq // RT

    def body(h_ref, f_ref, w_ref, t_ref, l_ref, dh_ref, dhb_ref, dw_ref):
        i = pl.program_id(0)
        h = h_ref[...] + f_ref[...]
        r = lax.rsqrt(jnp.mean(h * h, axis=-1, keepdims=True) + EPS)
        hh = h * r
        err = hh * w_ref[...] - t_ref[...]
        l_ref[...] = jnp.full((8, 128), 0.5 * jnp.sum(jnp.mean(err * err, axis=-1, keepdims=True)), F32)
        dy = err * (1.0 / d)
        gh = dy * w_ref[...]
        dh = r * (gh - hh * jnp.mean(gh * hh, axis=-1, keepdims=True))
        dh_ref[...] = dh
        dhb_ref[...] = dh.astype(BF16)
        dw = jnp.sum(dy * hh, axis=0, keepdims=True)

        @pl.when(i == 0)
        def _():
            dw_ref[...] = dw

        @pl.when(i > 0)
        def _():
            dw_ref[...] += dw

    rs = pl.BlockSpec((RT, d), lambda i: (i, 0))
    ws = pl.BlockSpec((1, d), lambda i: (0, 0))
    return _call(body, name="loss_head", grid=(nt,), in_specs=[rs, rs, ws, rs],
                 out_specs=[pl.BlockSpec((8, 128), lambda i: (i, 0)), rs, rs, ws],
                 out_shape=[_sds((nt * 8, 128), F32), _sds((seq, d), F32), _sds((seq, d), BF16), _sds((1, d), F32)],
                 sem=("arbitrary",))(h1, f, w, tgt)


def _sum_tiles(lt):
    n = lt.shape[0]

    def body(l_ref, o_ref):
        v = l_ref[...]
        r = lax.broadcasted_iota(jnp.int32, v.shape, 0)
        c = lax.broadcasted_iota(jnp.int32, v.shape, 1)
        o_ref[...] = jnp.sum(jnp.where((r % 8 == 0) & (c == 0), v, 0.0), keepdims=True)

    return _call(body, name="loss_sum", grid=(1,), in_specs=[pl.BlockSpec((n, 128), lambda i: (0, 0))],
                 out_specs=pl.BlockSpec((1, 1), lambda i: (0, 0)), out_shape=_sds((1, 1), F32))(lt)


def _gated_norm_fwd(o, g, nw):
    r = lax.rsqrt(jnp.mean(o * o, axis=-1, keepdims=True) + EPS)
    return o * r * nw * _silu(g)


def _gated_norm_bwd(o, g, nw, dout):
    r = lax.rsqrt(jnp.mean(o * o, axis=-1, keepdims=True) + EPS)
    oh = o * r
    don = dout * _silu(g)
    dg = dout * (oh * nw) * _dsilu(g)
    dnw = jnp.sum(don * oh, axis=0, keepdims=True)
    doh = don * nw
    return r * (doh - oh * jnp.mean(doh * oh, axis=-1, keepdims=True)), dg, dnw


def _hg_gates(fs, lbl):
    l0, l1 = lbl[0:1, :], lbl[1:2, :]
    m = jnp.maximum(l0, l1)
    e0, e1 = jnp.exp(l0 - m), jnp.exp(l1 - m)
    lb = e0 / (e0 + e1)
    sig = _sigmoid(fs)
    f = lb + (1.0 - lb) * sig
    return lb, sig, f, jnp.log(f), (1.0 - lb) * _sigmoid(-fs)


def _cumsum(w):
    row = lax.broadcasted_iota(jnp.int32, w.shape, 0) & (CH - 1)
    s = 1
    while s < CH:
        w = w + jnp.where(row >= s, pltpu.roll(w, s, 0), 0.0)
        s *= 2
    return w


def _rcumsum(w):
    row = lax.broadcasted_iota(jnp.int32, w.shape, 0) & (CH - 1)
    s = 1
    while s < CH:
        w = w + jnp.where(row < CH - s, pltpu.roll(w, w.shape[0] - s, 0), 0.0)
        s *= 2
    return w


def _decay_blocks(q, k, b, p_ref):
    p_ref[...] = jnp.zeros((CH, CH), F32)
    m16 = _tri(SUBH, "incl")
    for I in range(CH // SUBH):
        s0 = I * SUBH
        bI, qI, kI = b[s0:s0 + SUBH], q[s0:s0 + SUBH], k[s0:s0 + SUBH]
        dec = jnp.exp(jnp.minimum(bI[:, None, :] - bI[None, :, :], 0.0))
        pii = jnp.sum(qI[:, None, :] * kI[None, :, :] * dec, axis=-1)
        p_ref[s0:s0 + SUBH, s0:s0 + SUBH] = jnp.where(m16, pii, 0.0)
        if I > 0:
            rI = b[s0 - 1:s0]
            qs = qI * jnp.exp(bI - rI)
            ks = k[0:s0] * jnp.exp(rI - b[0:s0])
            p_ref[s0:s0 + SUBH, 0:s0] = _dot(qs, ks, NT)


HPS = 8
HPS_HGRN2_FWD = 4
HB = 4
NHB = NH // HPS
OFF = FRONT // RT


def _head_specs(hps, rev=None):
    row = (lambda i: i) if rev is None else rev
    col = lambda g: pl.BlockSpec((RT, hps * DH), lambda h, i: (row(i), g * (NH // hps) + h))
    full = pl.BlockSpec((RT, hps * DH), lambda h, i: (row(i), h))
    real = pl.BlockSpec((RT, hps * DH), lambda h, i: (jnp.maximum(row(i) - OFF, 0), h))
    state = lambda cpt: pl.BlockSpec((hps, cpt, DH, DH), lambda h, i: (h, row(i), 0, 0))
    scal = pl.BlockSpec((hps, RT, DH), lambda h, i: (h, row(i), 0))
    return col, full, real, state, scal


def _hgrn2_fwd(proj, lb_logits, nw):
    tp = proj.shape[0]
    nt, cpt = tp // RT, RT // CH
    hps = HPS_HGRN2_FWD

    def body(q_ref, f_ref, i_ref, g_ref, lbl_ref, nw_ref, og_ref, or_ref, st_ref, s_ref, p_ref):
        @pl.when(pl.program_id(1) == 0)
        def _():
            s_ref[...] = jnp.zeros((hps, DH, DH), F32)

        def chunk(c, carry):
            rows = pl.ds(pl.multiple_of(c * CH, CH), CH)
            for hh in range(hps):
                cols = slice(hh * DH, (hh + 1) * DH)
                _, _, _, w, k = _hg_gates(f_ref[rows, cols], lbl_ref[:, cols])
                q, v = _silu(q_ref[rows, cols]), i_ref[rows, cols]
                b = _cumsum(w)
                st = s_ref[hh]
                st_ref[hh, c] = st
                _decay_blocks(q, k, b, p_ref.at[hh])
                o = _dot(q * jnp.exp(b), st, NT) + _dot(p_ref[hh], v)
                bl = b[CH - 1:CH]
                s_ref[hh] = st * jnp.exp(bl) + _dot(v, k * jnp.exp(bl - b), TN)
                or_ref[rows, cols] = o
                og_ref[rows, cols] = _gated_norm_fwd(o, g_ref[rows, cols], nw_ref[...]).astype(BF16)
            return carry

        lax.fori_loop(0, cpt, chunk, 0)

    col, full, real, state, _ = _head_specs(hps)
    return _call(body, name="hgrn2_fwd", grid=(NH // hps, nt),
                 in_specs=[col(0), col(1), col(2), col(3), pl.BlockSpec((2, hps * DH), lambda h, i: (0, h)),
                           pl.BlockSpec((1, DH), lambda h, i: (0, 0))],
                 out_specs=[real, full, state(cpt)],
                 out_shape=[_sds((tp - FRONT, HW), BF16), _sds((tp, HW), F32), _sds((NH, tp // CH, DH, DH), F32)],
                 scratch=[pltpu.VMEM((hps, DH, DH), F32), pltpu.VMEM((hps, CH, CH), F32)],
                 sem=("parallel", "arbitrary"))(proj, proj, proj, proj, lb_logits, nw)


def _hgrn2_bwd(proj, lb_logits, nw, o_raw, states, dog, side=None):
    tp = proj.shape[0]
    nt, cpt = tp // RT, RT // CH

    def body(q_ref, f_ref, i_ref, g_ref, lbl_ref, nw_ref, or_ref, st_ref, dog_ref,
             dq_ref, df_ref, di_ref, dg_ref, dl_ref, dnw_ref, ds_ref, p_ref, dk_ref, dqa_ref, do_ref):
        step = pl.program_id(1)

        @pl.when(step == 0)
        def _():
            ds_ref[...] = jnp.zeros((HPS, DH, DH), F32)
            dl_ref[...] = jnp.zeros((2, HPS * DH), F32)

        @pl.when((step == 0) & (pl.program_id(0) == 0))
        def _():
            dnw_ref[...] = jnp.zeros((1, DH), F32)

        front = nt - 1 - step < OFF
        for hh in range(HPS):
            cols = slice(hh * DH, (hh + 1) * DH)
            dog_t = jnp.where(front, 0.0, dog_ref[:, cols])
            do_t, dg_t, dnw = _gated_norm_bwd(or_ref[:, cols], g_ref[:, cols], nw_ref[...], dog_t)
            do_ref[:, cols] = do_t
            dg_ref[:, cols] = dg_t.astype(BF16)
            dnw_ref[...] += dnw
        tril = _tri(CH, "incl")
        m16 = _tri(SUBH, "incl")

        def chunk(cc, carry):
            c = cpt - 1 - cc
            rows = pl.ds(pl.multiple_of(c * CH, CH), CH)
            for hh in range(HPS):
                cols = slice(hh * DH, (hh + 1) * DH)
                fs = f_ref[rows, cols]
                lb, sig, f, w, k = _hg_gates(fs, lbl_ref[:, cols])
                hq = q_ref[rows, cols]
                q, v, do = _silu(hq), i_ref[rows, cols], do_ref[rows, cols]
                b = _cumsum(w)
                bl = b[CH - 1:CH]
                eb = jnp.exp(b)
                qs, kd = q * eb, k * jnp.exp(bl - b)
                st, dst = st_ref[hh, c], ds_ref[hh]
                _decay_blocks(q, k, b, p_ref.at[hh])
                dv = _dot(p_ref[hh], do, TN) + _dot(kd, dst, NT)
                dp = jnp.where(tril, _dotx(do, v, NT), 0.0)
                dqa, dka = dqa_ref.at[hh], dk_ref.at[hh]
                dqa[...] = eb * _dotx(do, st)
                dka[...] = jnp.exp(bl - b) * _dotx(v, dst)
                for I in range(CH // SUBH):
                    s0 = I * SUBH
                    bI, qI, kI = b[s0:s0 + SUBH], q[s0:s0 + SUBH], k[s0:s0 + SUBH]
                    dec = jnp.exp(jnp.minimum(bI[:, None, :] - bI[None, :, :], 0.0))
                    dpii = jnp.where(m16, dp[s0:s0 + SUBH, s0:s0 + SUBH], 0.0)[:, :, None] * dec
                    dqa[s0:s0 + SUBH, :] += jnp.sum(dpii * kI[None, :, :], axis=1)
                    dka[s0:s0 + SUBH, :] += jnp.sum(dpii * qI[:, None, :], axis=0)
                    if I > 0:
                        rI = b[s0 - 1:s0]
                        eq, ek = jnp.exp(bI - rI), jnp.exp(rI - b[0:s0])
                        dpij = dp[s0:s0 + SUBH, 0:s0]
                        dqa[s0:s0 + SUBH, :] += eq * _dotx(dpij, k[0:s0] * ek)
                        dka[0:s0, :] += ek * _dotx(dpij, qI * eq, TN)
                dq, dk = dqa[...], dka[...]
                st_end = st * jnp.exp(bl) + _dotx(v, kd, TN)
                dw = _rcumsum(q * dq - k * dk) + jnp.sum(dst * st_end, axis=0, keepdims=True)
                ds_ref[hh] = dst * jnp.exp(bl) + _dotx(do, qs, TN)
                one_m = 1.0 - sig
                dq_ref[rows, cols] = (dq * _dsilu(hq)).astype(BF16)
                df_ref[rows, cols] = ((dw / f - dk) * (1.0 - lb) * sig * one_m).astype(BF16)
                di_ref[rows, cols] = dv.astype(BF16)
                dl_ref[0:1, cols] += jnp.sum((dw / f - dk) * one_m, axis=0, keepdims=True)
            return carry

        lax.fori_loop(0, cpt, chunk, 0)

        @pl.when(step == nt - 1)
        def _():
            lbl = lbl_ref[...]
            l0, l1 = lbl[0:1, :], lbl[1:2, :]
            m = jnp.maximum(l0, l1)
            e0, e1 = jnp.exp(l0 - m), jnp.exp(l1 - m)
            p0 = e0 / (e0 + e1)
            dl0 = dl_ref[0:1, :] * p0 * (1.0 - p0)
            dl_ref[0:1, :] = dl0
            dl_ref[1:2, :] = -dl0

    col, full, real, state, _ = _head_specs(HPS, lambda i: nt - 1 - i)
    lbs = pl.BlockSpec((2, HPS * DH), lambda h, i: (0, h))
    return _call(body, name="hgrn2_bwd", grid=(NHB, nt),
                 in_specs=[col(0), col(1), col(2), col(3), lbs, pl.BlockSpec((1, DH), lambda h, i: (0, 0)), full,
                           state(cpt), real],
                 out_specs=[full, full, full, full, lbs, pl.BlockSpec((1, DH), lambda h, i: (0, 0))],
                 out_shape=[_sds((tp, HW), BF16)] * 4 + [_sds((2, HW), F32), _sds((1, DH), F32)],
                 scratch=[pltpu.VMEM((HPS, DH, DH), F32), pltpu.VMEM((HPS, CH, CH), F32), pltpu.VMEM((HPS, CH, DH), F32),
                          pltpu.VMEM((HPS, CH, DH), F32), pltpu.VMEM((RT, HPS * DH), F32)],
                 sem=("arbitrary", "arbitrary"), side=side)(proj, proj, proj, proj, lb_logits, nw, o_raw, states, dog)


GQ0 = 4 * HW
CW = 3 * HW


def _gd_scalars(ab, alog, dtb):
    g = -jnp.exp(alog) * jax.nn.softplus(ab + dtb)
    return g, _sigmoid(ab)


def _conv_ext_specs(row_of):
    main = [pl.BlockSpec((RT, HW), lambda i, g=g: (row_of(i), GQ0 // HW + g)) for g in range(3)]
    prev = [pl.BlockSpec((8, HW), lambda i, g=g: (jnp.maximum(row_of(i) * (RT // 8) - 1, 0), GQ0 // HW + g)) for g in range(3)]
    return main + prev


def _conv_fill(ext_ref, xs, xps, first):
    for g in range(3):
        ext_ref[0:8, g * HW:(g + 1) * HW] = jnp.where(first, 0.0, xps[g][...])
        ext_ref[8:8 + RT, g * HW:(g + 1) * HW] = xs[g][...]


def _conv_apply(ext_ref, cw):
    y = cw[CONV_K - 1:CONV_K, :] * ext_ref[pl.ds(8, RT), :]
    for s in range(1, CONV_K):
        y += cw[CONV_K - 1 - s:CONV_K - s, :] * ext_ref[pl.ds(8 - s, RT), :]
    return y


def _gdn_prep_fwd(proj, pab, conv_w, alog, dtb):
    tp = proj.shape[0]
    nt = tp // RT

    def body(x0, x1, x2, p0, p1, p2, ab_ref, cw_ref, al_ref, dt_ref, q_ref, k_ref, v_ref, g_ref, b_ref, ext_ref):
        _conv_fill(ext_ref, (x0, x1, x2), (p0, p1, p2), pl.program_id(0) == 0)
        a = _silu(_conv_apply(ext_ref, cw_ref[...]))
        for h in range(NH):
            for part, ref, sc in ((0, q_ref, DH ** -0.5), (1, k_ref, 1.0)):
                seg = a[:, part * HW + h * DH:part * HW + (h + 1) * DH]
                ref[:, h * DH:(h + 1) * DH] = seg * (lax.rsqrt(jnp.sum(seg * seg, axis=-1, keepdims=True) + EPS) * sc)
        v_ref[...] = a[:, 2 * HW:3 * HW]
        g, beta = _gd_scalars(ab_ref[...], al_ref[...], dt_ref[...])
        for h in range(NH):
            g_ref[h] = jnp.broadcast_to(g[:, h:h + 1], (RT, DH))
            b_ref[h] = jnp.broadcast_to(beta[:, NH + h:NH + h + 1], (RT, DH))

    hs = pl.BlockSpec((RT, HW), lambda i: (i, 0))
    sc = pl.BlockSpec((NH, RT, DH), lambda i: (0, i, 0))
    one = pl.BlockSpec((1, DH), lambda i: (0, 0))
    return _call(body, name="gdn_prep_fwd", grid=(nt,),
                 in_specs=_conv_ext_specs(lambda i: i) + [pl.BlockSpec((RT, DH), lambda i: (i, 0)),
                                                           pl.BlockSpec((CONV_K, CW), lambda i: (0, 0)), one, one],
                 out_specs=[hs, hs, hs, sc, sc],
                 out_shape=[_sds((tp, HW), F32)] * 3 + [_sds((NH, tp, DH), F32)] * 2,
                 scratch=[pltpu.VMEM((RT + 8, CW), F32)], sem=("parallel",))(*([proj] * 6), pab, conv_w, alog, dtb)


def _gdn_prep_bwd(proj, pab, conv_w, alog, dtb, dq, dk, dv, dgb, dbb):
    tp = proj.shape[0]
    nt = tp // RT

    def body(x0, x1, x2, p0, p1, p2, ab_ref, cw_ref, al_ref, dt_ref, dq_ref, dk_ref, dv_ref, dg_ref, db_ref,
             dx_ref, dab_ref, dcw_ref, dal_ref, ddt_ref, ext_ref, dy_ref):
        step = pl.program_id(0)
        i = nt - 1 - step

        @pl.when(step == 0)
        def _():
            dy_ref[RT:RT + 8, :] = jnp.zeros((8, CW), F32)
            dcw_ref[...] = jnp.zeros((8, CW), F32)
            dal_ref[...] = jnp.zeros((1, DH), F32)
            ddt_ref[...] = jnp.zeros((1, DH), F32)

        _conv_fill(ext_ref, (x0, x1, x2), (p0, p1, p2), i == 0)
        cw = cw_ref[...]
        y = _conv_apply(ext_ref, cw)
        a = _silu(y)
        dsl = _dsilu(y)
        for h in range(NH):
            for part, ref, sc in ((0, dq_ref, DH ** -0.5), (1, dk_ref, 1.0)):
                lo = part * HW + h * DH
                seg = a[:, lo:lo + DH]
                r = lax.rsqrt(jnp.sum(seg * seg, axis=-1, keepdims=True) + EPS)
                xh = seg * r
                dxh = ref[:, h * DH:(h + 1) * DH] * sc
                dy_ref[0:RT, lo:lo + DH] = r * (dxh - xh * jnp.sum(dxh * xh, axis=-1, keepdims=True)) * dsl[:, lo:lo + DH]
        dy_ref[0:RT, 2 * HW:3 * HW] = dv_ref[...] * dsl[:, 2 * HW:3 * HW]
        dy = dy_ref[0:RT, :]
        dx = cw[CONV_K - 1:CONV_K, :] * dy
        dcw_ref[CONV_K - 1:CONV_K, :] += jnp.sum(dy * ext_ref[pl.ds(8, RT), :], axis=0, keepdims=True)
        for s in range(1, CONV_K):
            dx += cw[CONV_K - 1 - s:CONV_K - s, :] * dy_ref[pl.ds(s, RT), :]
            dcw_ref[CONV_K - 1 - s:CONV_K - s, :] += jnp.sum(dy * ext_ref[pl.ds(8 - s, RT), :], axis=0, keepdims=True)
        dx_ref[...] = dx.astype(BF16)
        dy_ref[RT:RT + 8, :] = dy[0:8, :]
        ab = ab_ref[...]
        g, beta = _gd_scalars(ab, al_ref[...], dt_ref[...])
        lane = lax.broadcasted_iota(jnp.int32, (RT, DH), 1)
        dgl = jnp.zeros((RT, DH), F32)
        dbl = jnp.zeros((RT, DH), F32)
        for h in range(NH):
            dgl = jnp.where(lane == h, dg_ref[h], dgl)
            dbl = jnp.where(lane == NH + h, db_ref[h], dbl)
        dsp = dgl * (-jnp.exp(al_ref[...])) * _sigmoid(ab + dt_ref[...])
        dab_ref[...] = (dsp + dbl * beta * (1.0 - beta)).astype(BF16)
        ddt_ref[...] += jnp.sum(dsp, axis=0, keepdims=True)
        dal_ref[...] += jnp.sum(dgl * g, axis=0, keepdims=True)

    hs = pl.BlockSpec((RT, HW), lambda s: (nt - 1 - s, 0))
    sc = pl.BlockSpec((NH, RT, DH), lambda s: (0, nt - 1 - s, 0))
    one = pl.BlockSpec((1, DH), lambda s: (0, 0))
    xs = pl.BlockSpec((RT, CW), lambda s: (nt - 1 - s, 0))
    return _call(body, name="gdn_prep_bwd", grid=(nt,),
                 in_specs=_conv_ext_specs(lambda s: nt - 1 - s) + [
                     pl.BlockSpec((RT, DH), lambda s: (nt - 1 - s, 0)), pl.BlockSpec((CONV_K, CW), lambda s: (0, 0)),
                     one, one, hs, hs, hs, sc, sc],
                 out_specs=[xs, pl.BlockSpec((RT, DH), lambda s: (nt - 1 - s, 0)), pl.BlockSpec((8, CW), lambda s: (0, 0)), one, one],
                 out_shape=[_sds((tp, CW), BF16), _sds((tp, DH), BF16), _sds((8, CW), F32), _sds((1, DH), F32), _sds((1, DH), F32)],
                 scratch=[pltpu.VMEM((RT + 8, CW), F32), pltpu.VMEM((RT + 8, CW), F32)],
                 sem=("arbitrary",))(*([proj] * 6), pab, conv_w, alog, dtb, dq, dk, dv, dgb, dbb)


HS = HB * CH


def _stack_heads(ref, rows, base):
    return jnp.concatenate([ref[rows, (base + hh) * DH:(base + hh + 1) * DH] for hh in range(HB)], axis=0)


def _stack_scal(ref, rows, base):
    return jnp.concatenate([ref[base + hh, rows, :] for hh in range(HB)], axis=0)


def _store_heads(ref, rows, base, val):
    for hh in range(HB):
        ref[rows, (base + hh) * DH:(base + hh + 1) * DH] = val[hh * CH:(hh + 1) * CH].astype(ref.dtype)


def _bd_masks():
    r = lax.broadcasted_iota(jnp.int32, (HS, HS), 0)
    c = lax.broadcasted_iota(jnp.int32, (HS, HS), 1)
    same = lax.shift_right_logical(r, int(math.log2(CH))) == lax.shift_right_logical(c, int(math.log2(CH)))
    return same & (r >= c), same & (r > c)


def _unit_lower_inverse(a):
    n = a.shape[0]
    r = lax.broadcasted_iota(jnp.int32, (n, n), 0)
    c = lax.broadcasted_iota(jnp.int32, (n, n), 1)
    blk_of = lambda t, size: lax.shift_right_logical(t, int(math.log2(size)))
    a16 = jnp.where(blk_of(r, SUB) == blk_of(c, SUB), a, 0.0)
    x = (r == c).astype(F32) - a16
    p = a16
    for _ in range(3):
        p = _dot(p, p)
        x = x + _dot(x, p)
    for blk in (2 * SUB, 4 * SUB):
        off = jnp.where((blk_of(r, blk) == blk_of(c, blk)) & (blk_of(r, blk // 2) != blk_of(c, blk // 2)), a, 0.0)
        x = x - _dot(x, _dot(off, x))
    return x


def _gdn_chunk_common(q, k, v, gl, bt, incl, strict):
    gc = _cumsum(gl)
    e = jnp.exp(gc)
    rel = jnp.exp(jnp.minimum(gc[:, 0:1] - gc.T[0:1, :], 0.0))
    kb = bt * k
    a = jnp.where(strict, bt[:, 0:1] * _dot(k, k, NT) * rel, 0.0)
    x = _unit_lower_inverse(a)
    wu = _dot(x, jnp.concatenate([kb * e, bt * v], axis=1))
    attn = jnp.where(incl, _dot(q, k, NT) * rel, 0.0)
    return gc, e, rel, kb, a, x, wu[:, 0:DH], wu[:, DH:2 * DH], attn


def _gdn_fwd(q, k, v, gb, bb, proj, nw):
    tp = q.shape[0]
    nt, cpt = tp // RT, RT // CH

    def body(q_ref, k_ref, v_ref, g_ref, b_ref, z_ref, nw_ref, og_ref, or_ref, st_ref, s_ref):
        @pl.when(pl.program_id(1) == 0)
        def _():
            s_ref[...] = jnp.zeros((HPS, DH, DH), F32)

        incl, strict = _bd_masks()

        def chunk(c, carry):
            rows = pl.ds(pl.multiple_of(c * CH, CH), CH)
            for base in range(0, HPS, HB):
                qc, kc, vc = _stack_heads(q_ref, rows, base), _stack_heads(k_ref, rows, base), _stack_heads(v_ref, rows, base)
                gc, e, rel, kb, a, x, w, u, attn = _gdn_chunk_common(qc, kc, vc, _stack_scal(g_ref, rows, base),
                                                                    _stack_scal(b_ref, rows, base), incl, strict)
                qe = qc * e
                ws, qs = [], []
                for hh in range(HB):
                    blk = slice(hh * CH, (hh + 1) * CH)
                    s = s_ref[base + hh]
                    st_ref[base + hh, c] = s
                    both = _dot(jnp.concatenate([w[blk], qe[blk]], axis=0), s)
                    ws.append(both[0:CH])
                    qs.append(both[CH:2 * CH])
                vn = u - jnp.concatenate(ws, axis=0)
                o = jnp.concatenate(qs, axis=0) + _dot(attn, vn)
                for hh in range(HB):
                    blk = slice(hh * CH, (hh + 1) * CH)
                    gl = gc[(hh + 1) * CH - 1:(hh + 1) * CH]
                    s_ref[base + hh] = s_ref[base + hh] * jnp.exp(gl) + _dot(kc[blk] * jnp.exp(gl - gc[blk]), vn[blk], TN)
                _store_heads(or_ref, rows, base, o)
                for hh in range(HB):
                    cols = slice((base + hh) * DH, (base + hh + 1) * DH)
                    og_ref[rows, cols] = _gated_norm_fwd(o[hh * CH:(hh + 1) * CH], z_ref[rows, cols], nw_ref[...]).astype(BF16)
            return carry

        lax.fori_loop(0, cpt, chunk, 0)

    col, full, real, state, scal = _head_specs(HPS)
    return _call(body, name="gdn_fwd", grid=(NHB, nt),
                 in_specs=[full, full, full, scal, scal, col(7), pl.BlockSpec((1, DH), lambda h, i: (0, 0))],
                 out_specs=[real, full, state(cpt)],
                 out_shape=[_sds((tp - FRONT, HW), BF16), _sds((tp, HW), F32), _sds((NH, tp // CH, DH, DH), F32)],
                 scratch=[pltpu.VMEM((HPS, DH, DH), F32)], sem=("parallel", "arbitrary"))(q, k, v, gb, bb, proj, nw)


def _gdn_bwd(q, k, v, gb, bb, proj, nw, o_raw, states, dog):
    tp = q.shape[0]
    nt, cpt = tp // RT, RT // CH

    def body(q_ref, k_ref, v_ref, g_ref, b_ref, z_ref, nw_ref, or_ref, st_ref, dog_ref,
             dq_ref, dk_ref, dv_ref, dg_ref, db_ref, dz_ref, dnw_ref, ds_ref, do_ref):
        step = pl.program_id(1)

        @pl.when(step == 0)
        def _():
            ds_ref[...] = jnp.zeros((HPS, DH, DH), F32)

        @pl.when((step == 0) & (pl.program_id(0) == 0))
        def _():
            dnw_ref[...] = jnp.zeros((1, DH), F32)

        front = nt - 1 - step < OFF
        for hh in range(HPS):
            cols = slice(hh * DH, (hh + 1) * DH)
            dog_t = jnp.where(front, 0.0, dog_ref[:, cols])
            do_t, dz_t, dnw = _gated_norm_bwd(or_ref[:, cols], z_ref[:, cols], nw_ref[...], dog_t)
            do_ref[:, cols] = do_t
            dz_ref[:, cols] = dz_t.astype(BF16)
            dnw_ref[...] += dnw
        incl, strict = _bd_masks()
        last_row = (lax.broadcasted_iota(jnp.int32, (CH, 1), 0) == CH - 1)

        def rsum(t):
            return jnp.sum(t, axis=-1, keepdims=True)

        def chunk(cc, carry):
            c = cpt - 1 - cc
            rows = pl.ds(pl.multiple_of(c * CH, CH), CH)
            for base in range(0, HPS, HB):
                qc, kc, vc, do = (_stack_heads(q_ref, rows, base), _stack_heads(k_ref, rows, base), _stack_heads(v_ref, rows, base),
                                  _stack_heads(do_ref, rows, base))
                bt = _stack_scal(b_ref, rows, base)
                gc, e, rel, kb, a, x, w, u, attn = _gdn_chunk_common(qc, kc, vc, _stack_scal(g_ref, rows, base), bt, incl, strict)
                qe = qc * e
                heads = [slice(hh * CH, (hh + 1) * CH) for hh in range(HB)]
                gls = [gc[(hh + 1) * CH - 1:(hh + 1) * CH] for hh in range(HB)]
                cdec = jnp.concatenate([jnp.exp(gl - gc[blk]) for gl, blk in zip(gls, heads)], axis=0)
                kcd = kc * cdec
                vn = u - jnp.concatenate([_dot(w[blk], st_ref[base + hh, c]) for hh, blk in enumerate(heads)], axis=0)
                dos = jnp.concatenate([_dot(do[blk], st_ref[base + hh, c], NT) for hh, blk in enumerate(heads)], axis=0)
                kds = jnp.concatenate([_dot(kcd[blk], ds_ref[base + hh]) for hh, blk in enumerate(heads)], axis=0)
                vds = jnp.concatenate([_dot(vn[blk], ds_ref[base + hh], NT) for hh, blk in enumerate(heads)], axis=0)
                dvn = _dot(attn, do, TN) + kds
                dattn = jnp.where(incl, _dot(do, vn, NT), 0.0)
                dar = dattn * rel
                dq = _dot(dar, kc) + e * dos
                dk = _dot(dar, qc, TN) + cdec * vds
                dc = cdec[:, 0:1] * rsum(kc * vds)
                dgc = rsum(qe * dos) - dc
                dw = -jnp.concatenate([_dot(dvn[blk], st_ref[base + hh, c], NT) for hh, blk in enumerate(heads)], axis=0)
                extra = []
                for hh, blk in enumerate(heads):
                    s, dsn = st_ref[base + hh, c], ds_ref[base + hh]
                    el = jnp.exp(gls[hh])
                    dglast = jnp.sum(dc[blk], axis=0, keepdims=True) + el[:, 0:1] * jnp.sum(rsum(dsn * s), axis=0, keepdims=True)
                    extra.append(jnp.where(last_row, dglast, 0.0))
                    ds_ref[base + hh] = dsn * el + _dot(jnp.concatenate([qe[blk], -w[blk]], axis=0),
                                                 jnp.concatenate([do[blk], dvn[blk]], axis=0), TN)
                dr = _dot(x, jnp.concatenate([dw, dvn], axis=1), TN)
                drw, dru = dr[:, 0:DH], dr[:, DH:2 * DH]
                da = -jnp.where(strict, _dot(dr, jnp.concatenate([w, u], axis=1), NT), 0.0)
                dar2 = da * rel
                dkb = _dot(dar2, kc)
                rwk = rsum(drw * kc)
                dk = dk + _dot(dar2, kb, TN) + bt * dkb + (bt * e) * drw
                dbeta = rsum(dkb * kc) + e[:, 0:1] * rwk + rsum(dru * vc)
                z = dattn * attn + da * a
                dgc = dgc + bt[:, 0:1] * e[:, 0:1] * rwk + rsum(z) - rsum(z.T) + jnp.concatenate(extra, axis=0)
                _store_heads(dq_ref, rows, base, dq)
                _store_heads(dk_ref, rows, base, dk)
                _store_heads(dv_ref, rows, base, bt * dru)
                dg = _rcumsum(jnp.broadcast_to(dgc, (HS, DH)))
                dbb = jnp.broadcast_to(dbeta, (HS, DH))
                for hh, blk in enumerate(heads):
                    dg_ref[base + hh, rows, :] = dg[blk]
                    db_ref[base + hh, rows, :] = dbb[blk]
            return carry

        lax.fori_loop(0, cpt, chunk, 0)

    col, full, real, state, scal = _head_specs(HPS, lambda i: nt - 1 - i)
    one = pl.BlockSpec((1, DH), lambda h, i: (0, 0))
    return _call(body, name="gdn_bwd", grid=(NHB, nt),
                 in_specs=[full, full, full, scal, scal, col(7), one, full, state(cpt), real],
                 out_specs=[full, full, full, scal, scal, full, one],
                 out_shape=[_sds((tp, HW), F32)] * 3 + [_sds((NH, tp, DH), F32)] * 2 + [_sds((tp, HW), BF16), _sds((1, DH), F32)],
                 scratch=[pltpu.VMEM((HPS, DH, DH), F32), pltpu.VMEM((RT, HPS * DH), F32)],
                 sem=("arbitrary", "arbitrary"))(q, k, v, gb, bb, proj, nw, o_raw, states, dog)


MAIN_W = 8 * HW
AB_W = 2 * NH


def _split_w_in(w_in):
    main = jnp.concatenate([w_in[:, :MAIN_W], w_in[:, MAIN_W + AB_W:]], axis=1)
    ab = jnp.pad(w_in[:, MAIN_W:MAIN_W + AB_W], ((0, 0), (0, DH - AB_W)))
    return main, ab


def _pad_lanes(v):
    return jnp.pad(v, ((0, 0), (0, DH - v.shape[1])))


class _NoComm:
    def __init__(self, late):
        self.late = late

    def proj_side(self):
        return None

    def late_weights(self, side_outs):
        return self.late

    def early_grads_side(self, grads):
        return None

    def early_grads_done(self, side_outs):
        pass

    def last_grad_side(self, dw_in):
        return None

    def last_grad_done(self, side_outs):
        pass


def _local_step(x, tgt, meta, lb_logits, mix_w, w_in, hg_nw, conv_w, a_log, dt_bias, gd_nw, ffn_nw, final_w, comm):
    w_main, w_ab = _split_w_in(w_in)
    alog, dtb = _pad_lanes(a_log), _pad_lanes(dt_bias)
    final_w = final_w.reshape(1, -1)
    rows4 = lambda t: t.reshape(4, t.shape[0] // 4, t.shape[1])
    xn = _rms1_fwd(x, meta, mix_w)
    side = comm.proj_side()
    proj = _mm(xn, w_main, "nn", F32, 768, 1024, 2048, "proj_main", n_outer=True, side=side)
    proj, landed = proj if side is not None else (proj, None)
    w_a, w_b, w_out, w_ffn_in, w_ffn_out = comm.late_weights(landed)
    pab = _mm(xn, w_ab, "nn", F32, 768, 128, 2048, "proj_ab")
    oa_g, oa_raw, st_a = _hgrn2_fwd(proj, lb_logits, hg_nw)
    q, k, v, gb, bb = _gdn_prep_fwd(proj, pab, conv_w, alog, dtb)
    ob_g, ob_raw, st_b = _gdn_fwd(q, k, v, gb, bb, proj, gd_nw)
    za = _mm(oa_g, w_a, "nn", F32, 1024, 512, 1024, "branch_a", n_outer=True)
    zb = _mm(ob_g, w_b, "nn", F32, 1024, 512, 1024, "branch_b", n_outer=True)
    merged = _merge_fwd(proj, za, zb)
    mix = _mm(merged, w_out, "nn", F32, 1024, 2048, 2048, "mix_out")
    h1, n2 = _resid_norm_fwd(x, mix, ffn_nw)
    gu = _mm(n2, w_ffn_in, "nn", BF16, 1024, 1408, 2048, "ffn_in", n_outer=True)
    act = _swiglu_fwd(gu)
    f = _mm(act, w_ffn_out, "nn", F32, 1024, 2048, 512, "ffn_out")
    lt, dh2, dh2b, dfinal = _loss_head(h1, f, final_w, tgt)
    loss = _sum_tiles(lt)
    dact = _mm(dh2b, w_ffn_out, "nt", BF16, 1024, 512, 2048, "d_act", n_outer=True)
    dw_ffn_out = rows4(_mm(act, dh2b, "tn", F32, 512, 2048, 1024, "dw_ffn_out"))
    dgu = _swiglu_bwd(gu, dact)
    dn2 = _mm(dgu, w_ffn_in, "nt", F32, 1024, 2048, 1408, "d_n2")
    dw_ffn_in = _mm(n2, dgu, "tn", F32, 1024, 1408, 1024, "dw_ffn_in", out_shards=4)
    dh1, dh1b, dffn_nw = _resid_norm_bwd(h1, ffn_nw, dn2, dh2)
    dmerged = _mm(dh1b, w_out, "nt", F32, 1024, 2048, 2048, "d_merged")
    dw_out = _mm(merged, dh1b, "tn", F32, 2048, 1024, 1024, "dw_out")
    dza, dzb, dgate = _merge_bwd(proj, za, zb, dmerged)
    doa = _mm(dza, w_a, "nt", F32, 1024, 1024, 512, "d_oa")
    dob = _mm(dzb, w_b, "nt", F32, 1024, 1024, 512, "d_ob")
    dw_a = _mm(oa_g, dza, "tn", F32, 1024, 512, 1024, "dw_branch_a", out_shards=4)
    dw_b = _mm(ob_g, dzb, "tn", F32, 1024, 512, 1024, "dw_branch_b", out_shards=4)
    early = dict(w_ffn_in=dw_ffn_in, w_ffn_out=dw_ffn_out, w_out=rows4(dw_out), w_branch_a=dw_a, w_branch_b=dw_b)
    side = comm.early_grads_side(early)
    hg = _hgrn2_bwd(proj, lb_logits, hg_nw, oa_raw, st_a, doa, side=side)
    if side is not None:
        hg, arrived = hg
        comm.early_grads_done(arrived)
    dhq, dhf, dhi, dhg, dlbl, dhg_nw = hg
    dq, dk, dv, dg, dbeta, dz, dgd_nw = _gdn_bwd(q, k, v, gb, bb, proj, gd_nw, ob_raw, st_b, dob)
    dx3, dab, dconv, dalog, ddtb = _gdn_prep_bwd(proj, pab, conv_w, alog, dtb, dq, dk, dv, dg, dbeta)
    dproj = jnp.concatenate([dhq, dhf, dhi, dhg, dx3, dz, dgate], axis=1)
    dw_main = _mm(xn, dproj, "tn", F32, 2048, 1024, 768, "dw_in_main")
    dw_ab = _mm(xn, dab, "tn", F32, 2048, 128, 768, "dw_in_ab")
    d_model = dw_main.shape[0]
    dw_in = jnp.concatenate([dw_main[:, :MAIN_W], dw_ab[:, :AB_W], dw_main[:, MAIN_W:]], axis=1)
    dw_in = dw_in.reshape(d_model, 4, -1).transpose(1, 0, 2)
    side = comm.last_grad_side(dw_in)
    dxn = _mm(dproj, w_main, "nt", F32, 768, 2048, 1024, "d_xn", side=side)
    if side is not None:
        dxn, arrived = dxn
        comm.last_grad_done(arrived)
    dxn = _mm(dab, w_ab, "nt", F32, 768, 2048, 128, "d_xn_ab", add=dxn)
    dx, dmeta, dmix_w = _rms1_bwd(x, meta, mix_w, dxn, dh1)
    grads = dict(meta_tokens=dmeta, lb_logits=dlbl, mix_norm_w=dmix_w, w_in=dw_in,
                 hg_norm_w=dhg_nw, gd_conv_w=dconv[:CONV_K], gd_a_log=dalog[:, :NH],
                 gd_dt_bias=ddtb[:, :NH], gd_norm_w=dgd_nw, w_branch_a=dw_a, w_branch_b=dw_b,
                 w_out=rows4(dw_out), ffn_norm_w=dffn_nw, w_ffn_in=dw_ffn_in, w_ffn_out=dw_ffn_out,
                 final_norm_w=dfinal.reshape(-1))
    return loss, dx, grads


def _adamw(g, w, m, v, name):
    rows, cols = g.shape
    tr = rows
    for cand in (128, 64, 32, 16, 8):
        if rows % cand == 0 and rows > cand:
            tr = cand
            break

    def body(g_ref, w_ref, m_ref, v_ref, d_ref, nm_ref, nv_ref):
        gg = g_ref[...]
        nm = ADAM_B1 * m_ref[...] + (1.0 - ADAM_B1) * gg
        nv = ADAM_B2 * v_ref[...] + (1.0 - ADAM_B2) * (gg * gg)
        m_hat = nm / (1.0 - ADAM_B1 ** ADAM_STEP)
        v_hat = nv / (1.0 - ADAM_B2 ** ADAM_STEP)
        d_ref[...] = -ADAM_LR * (m_hat / (jnp.sqrt(v_hat) + ADAM_EPS) + ADAM_WD * w_ref[...])
        nm_ref[...] = nm
        nv_ref[...] = nv

    bs = pl.BlockSpec((tr, cols), lambda i: (i, 0))
    return _call(body, name=name, grid=(rows // tr,), in_specs=[bs] * 4, out_specs=[bs] * 3,
                 out_shape=[_sds((rows, cols), F32)] * 3, sem=("parallel",))(g, w, m, v)


HBM = pl.BlockSpec(memory_space=pltpu.HBM)
MESH = pl.DeviceIdType.MESH


def _place():
    x, y, c = lax.axis_index("x"), lax.axis_index("y"), lax.axis_index("c")
    return x, y, c, [(1 - x, y), (x, 1 - y), (1 - x, 1 - y)]


def _comm_call(body, name, out_shape, n_in, scratch):
    return pl.pallas_call(body, name=name, out_shape=out_shape, in_specs=[HBM] * n_in,
                          out_specs=jax.tree.map(lambda _: HBM, out_shape), scratch_shapes=scratch)


def _half_rows(rows, c, tile):
    hh = rows // 2
    assert rows % 2 == 0 and hh % tile == 0, (rows, tile)
    return pl.ds(pl.multiple_of(c * hh, tile), hh)


def _gather_copies(w_refs, out_refs, sems):
    send_sems, recv_sems = sems
    x, y, c, chips = _place()
    s_me = 2 * x + y
    sends, recvs = [], []
    for k, (w_ref, out_ref) in enumerate(zip(w_refs, out_refs)):
        half = _half_rows(w_ref.shape[0], c, 16)
        for j, (cx, cy) in enumerate(chips):
            sem = dict(send_sem=send_sems.at[3 * k + j], recv_sem=recv_sems.at[3 * k + j], device_id=(cx, cy, c), device_id_type=MESH)
            sends.append(pltpu.make_async_remote_copy(src_ref=w_ref.at[half], dst_ref=out_ref.at[s_me, half], **sem))
            recvs.append(pltpu.make_async_remote_copy(src_ref=w_ref.at[half], dst_ref=out_ref.at[2 * cx + cy, half], **sem))
    return sends, recvs


def _gather_sems(n):
    return [pltpu.SemaphoreType.DMA((3 * n,)), pltpu.SemaphoreType.DMA((3 * n,))]


def _gather_start(w_refs, out_refs, sems):
    for cp in _gather_copies(w_refs, out_refs, sems)[0]:
        cp.start()


def _gather_wait(w_refs, out_refs, sems):
    sends, recvs = _gather_copies(w_refs, out_refs, sems)
    for cp in recvs:
        cp.wait_recv()
    for cp in sends:
        cp.wait_send()


def _gather_chips(shards):
    n = len(shards)

    def body(*refs):
        _gather_start(refs[:n], refs[n:2 * n], refs[2 * n:])
        _gather_wait(refs[:n], refs[n:2 * n], refs[2 * n:])

    return _comm_call(body, "gather_chips", [_sds((4,) + w.shape, w.dtype) for w in shards], n, _gather_sems(n))(*shards)


def _gather_side(shards):
    return _Side(shards, [_sds((4,) + w.shape, w.dtype) for w in shards], _gather_sems(len(shards)), _gather_start, _gather_wait)


def _forward_halves(outs, name):
    n = len(outs)

    def body(*refs):
        out_refs = refs[n:2 * n]
        send_sems, recv_sems = refs[2 * n:]
        x, y, c, chips = _place()
        cps = []
        for k in range(n):
            rows = out_refs[k].shape[1]
            half, other = _half_rows(rows, c, 16), _half_rows(rows, 1 - c, 16)
            for j, (cx, cy) in enumerate(chips):
                sem = dict(send_sem=send_sems.at[3 * k + j], recv_sem=recv_sems.at[3 * k + j], device_id=(x, y, 1 - c), device_id_type=MESH)
                landed = out_refs[k].at[2 * cx + cy, half]
                cps.append(pltpu.make_async_remote_copy(src_ref=landed, dst_ref=landed, **sem))
                cps[-1].start()
        for k in range(n):
            rows = out_refs[k].shape[1]
            half, other = _half_rows(rows, c, 16), _half_rows(rows, 1 - c, 16)
            for j, (cx, cy) in enumerate(chips):
                sem = dict(send_sem=send_sems.at[3 * k + j], recv_sem=recv_sems.at[3 * k + j], device_id=(x, y, 1 - c), device_id_type=MESH)
                pltpu.make_async_remote_copy(src_ref=out_refs[k].at[2 * cx + cy, half], dst_ref=out_refs[k].at[2 * cx + cy, other], **sem).wait_recv()
        for cp in cps:
            cp.wait_send()

    shapes = [_sds(o.shape, o.dtype) for o in outs]
    return pl.pallas_call(body, name=name, out_shape=shapes, in_specs=[HBM] * n, out_specs=[HBM] * n,
                          input_output_aliases={k: k for k in range(n)},
                          scratch_shapes=[pltpu.SemaphoreType.DMA((3 * n,)), pltpu.SemaphoreType.DMA((3 * n,))])(*outs)


def _swap_halves(gs, name):
    n = len(gs)

    def body(*refs):
        g_refs, out_refs = refs[:n], refs[n:2 * n]
        send_sems, recv_sems = refs[2 * n:]
        x, y, c, _ = _place()
        cps = []
        for k in range(n):
            other = _half_rows(g_refs[k].shape[1], 1 - c, 8)
            cps.append(pltpu.make_async_remote_copy(src_ref=g_refs[k].at[:, other, :], dst_ref=out_refs[k], send_sem=send_sems.at[k],
                                                    recv_sem=recv_sems.at[k], device_id=(x, y, 1 - c), device_id_type=MESH))
            cps[-1].start()
        for cp in cps:
            cp.wait()

    return _comm_call(body, name, [_sds((4, g.shape[1] // 2, g.shape[2]), g.dtype) for g in gs], n,
                      [pltpu.SemaphoreType.DMA((n,)), pltpu.SemaphoreType.DMA((n,))])(*gs)


def _row_tile(rows, row_bytes, budget=3 << 20):
    if rows * row_bytes <= budget:
        return rows
    return max(t for t in range(16, rows, 16) if rows % t == 0 and t * row_bytes <= budget)


def _add_half(g, got, c, name):
    _, rows, cols = g.shape
    hh = rows // 2
    tr = _row_tile(hh, cols * 4)
    nb = hh // tr

    def body(c_ref, a_ref, b_ref, o_ref):
        o_ref[...] = (a_ref[...] + b_ref[...]).astype(BF16)

    gs = pltpu.PrefetchScalarGridSpec(
        num_scalar_prefetch=1, grid=(4, nb),
        in_specs=[pl.BlockSpec((1, tr, cols), lambda s, i, c_ref: (s, c_ref[0] * nb + i, 0)),
                  pl.BlockSpec((1, tr, cols), lambda s, i, c_ref: (s, i, 0))],
        out_specs=pl.BlockSpec((1, tr, cols), lambda s, i, c_ref: (s, i, 0)))
    return pl.pallas_call(body, name=name, grid_spec=gs, out_shape=_sds((4, hh, cols), BF16),
                          compiler_params=pltpu.CompilerParams(dimension_semantics=("parallel", "parallel"),
                                                               vmem_limit_bytes=VMEM_LIMIT))(c, g, got)


def _scatter_copies(p_refs, out_refs, sems):
    send_sems, recv_sems = sems
    x, y, c, chips = _place()
    s_me = 2 * x + y
    cps = []
    for k, (p_ref, out_ref) in enumerate(zip(p_refs, out_refs)):
        for j, (cx, cy) in enumerate(chips):
            cps.append(pltpu.make_async_remote_copy(src_ref=p_ref.at[2 * cx + cy], dst_ref=out_ref.at[s_me],
                                                    send_sem=send_sems.at[3 * k + j], recv_sem=recv_sems.at[3 * k + j],
                                                    device_id=(cx, cy, c), device_id_type=MESH))
    return cps


def _scatter_start(p_refs, out_refs, sems):
    for cp in _scatter_copies(p_refs, out_refs, sems):
        cp.start()


def _scatter_wait(p_refs, out_refs, sems):
    for cp in _scatter_copies(p_refs, out_refs, sems):
        cp.wait()


def _scatter_chips(ps):
    n = len(ps)

    def body(*refs):
        _scatter_start(refs[:n], refs[n:2 * n], refs[2 * n:])
        _scatter_wait(refs[:n], refs[n:2 * n], refs[2 * n:])

    return _comm_call(body, "scatter_chips", [_sds(p_.shape, p_.dtype) for p_ in ps], n, _gather_sems(n))(*ps)


def _scatter_side(ps):
    return _Side(ps, [_sds(p_.shape, p_.dtype) for p_ in ps], _gather_sems(len(ps)), _scatter_start, _scatter_wait)


def _sum_slabs(b, name):
    n, h, wd = b.shape
    tr = _row_tile(h, n * wd * 4, 6 << 20)

    def body(b_ref, o_ref):
        acc = b_ref[0]
        for s in range(1, n):
            acc = acc + b_ref[s]
        o_ref[...] = acc

    return _call(body, name=name, grid=(h // tr,), in_specs=[pl.BlockSpec((n, tr, wd), lambda i: (0, i, 0))],
                 out_specs=pl.BlockSpec((tr, wd), lambda i: (i, 0)), out_shape=_sds((h, wd), F32), sem=("parallel",))(b)


def _sum_chips(arrived, own, name):
    n, h, wd = arrived.shape
    tr = _row_tile(h, n * wd * 2, 6 << 20)
    nb = h // tr
    my_chip = lambda: 2 * lax.axis_index("x") + lax.axis_index("y")

    def body(b_ref, p_ref, o_ref):
        acc = None
        for s in range(n):
            term = jnp.where(my_chip() == s, p_ref[0], b_ref[s]).astype(F32)
            acc = term if acc is None else acc + term
        o_ref[...] = acc

    return _call(body, name=name, grid=(nb,),
                 in_specs=[pl.BlockSpec((n, tr, wd), lambda i: (0, i, 0)), pl.BlockSpec((1, tr, wd), lambda i: (my_chip(), i, 0))],
                 out_specs=pl.BlockSpec((tr, wd), lambda i: (lax.axis_index("c") * nb + i, 0)),
                 out_shape=_sds((2 * h, wd), F32), sem=("parallel",))(arrived, own)


def _share_halves(gs):
    n = len(gs)

    def body(*refs):
        out_refs = refs[n:2 * n]
        send_sems, recv_sems = refs[2 * n:]
        x, y, c, _ = _place()
        cps = []
        for k in range(n):
            half, other = _half_rows(out_refs[k].shape[0], c, 8), _half_rows(out_refs[k].shape[0], 1 - c, 8)
            sem = dict(send_sem=send_sems.at[k], recv_sem=recv_sems.at[k], device_id=(x, y, 1 - c), device_id_type=MESH)
            cps.append((pltpu.make_async_remote_copy(src_ref=out_refs[k].at[half], dst_ref=out_refs[k].at[half], **sem),
                        pltpu.make_async_remote_copy(src_ref=out_refs[k].at[half], dst_ref=out_refs[k].at[other], **sem)))
            cps[-1][0].start()
        for send, recv in cps:
            recv.wait_recv()
            send.wait_send()

    return pl.pallas_call(body, name="share_halves", out_shape=[_sds(g.shape, g.dtype) for g in gs], in_specs=[HBM] * n,
                          out_specs=[HBM] * n, input_output_aliases={k: k for k in range(n)},
                          scratch_shapes=[pltpu.SemaphoreType.DMA((n,)), pltpu.SemaphoreType.DMA((n,))])(*gs)


def _gather_all(v, name):
    def body(v_ref, out_ref, send_sems, recv_sems, local_sem):
        x, y, c = lax.axis_index("x"), lax.axis_index("y"), lax.axis_index("c")
        me = 4 * x + 2 * y + c
        flip = lambda t, d: 1 - t if d else t
        mine = pltpu.make_async_copy(v_ref, out_ref.at[me], local_sem)
        mine.start()
        cps = []
        for k in range(1, 8):
            to = (flip(x, k & 4), flip(y, k & 2), flip(c, k & 1))
            cps.append(pltpu.make_async_remote_copy(src_ref=v_ref, dst_ref=out_ref.at[me], send_sem=send_sems.at[k - 1],
                                                    recv_sem=recv_sems.at[k - 1], device_id=to, device_id_type=MESH))
        for cp in cps:
            cp.start()
        for cp in cps:
            cp.wait()
        mine.wait()

    return _comm_call(body, name, _sds((8,) + v.shape, v.dtype), 1,
                      [pltpu.SemaphoreType.DMA((7,)), pltpu.SemaphoreType.DMA((7,)), pltpu.SemaphoreType.DMA])(v)


BIG = (("w_in", 1), ("w_branch_a", 1), ("w_branch_b", 1), ("w_out", 0), ("w_ffn_in", 1), ("w_ffn_out", 0))
SMALL = ("meta_tokens", "lb_logits", "mix_norm_w", "hg_norm_w", "gd_conv_w", "gd_a_log", "gd_dt_bias", "gd_norm_w",
         "ffn_norm_w", "final_norm_w")


def _pack_lanes(parts):
    rows = []
    for p in parts:
        f = p.reshape(-1).astype(F32)
        n = -(-f.shape[0] // DH) * DH
        rows.append(jnp.pad(f, (0, n - f.shape[0])).reshape(-1, DH))
    buf = jnp.concatenate(rows, axis=0)
    return jnp.pad(buf, ((0, -buf.shape[0] % 8), (0, 0)))


def _unpack_lanes(buf, shapes):
    out, off = [], 0
    for shp in shapes:
        n = math.prod(shp)
        r = -(-n // DH)
        out.append(buf[off:off + r].reshape(-1)[:n].reshape(shp))
        off += r
    return out


def kernel(x, meta_tokens, lb_logits, mix_norm_w, w_in, hg_norm_w, gd_conv_w, gd_a_log, gd_dt_bias, gd_norm_w, w_branch_a, w_branch_b, w_out, ffn_norm_w, w_ffn_in, w_ffn_out, final_norm_w, loss_target, m_meta_tokens, m_lb_logits, m_mix_norm_w, m_w_in, m_hg_norm_w, m_gd_conv_w, m_gd_a_log, m_gd_dt_bias, m_gd_norm_w, m_w_branch_a, m_w_branch_b, m_w_out, m_ffn_norm_w, m_w_ffn_in, m_w_ffn_out, m_final_norm_w, v_meta_tokens, v_lb_logits, v_mix_norm_w, v_w_in, v_hg_norm_w, v_gd_conv_w, v_gd_a_log, v_gd_dt_bias, v_gd_norm_w, v_w_branch_a, v_w_branch_b, v_w_out, v_ffn_norm_w, v_w_ffn_in, v_w_ffn_out, v_final_norm_w):
    args = dict(locals())
    big = [n for n, _ in BIG]
    w = {n: args[n] for n in SMALL + tuple(big)}
    m = {n: args["m_" + n] for n in w}
    v = {n: args["v_" + n] for n in w}
    xi, yi, ci = lax.axis_index("x"), lax.axis_index("y"), lax.axis_index("c")
    shard = 2 * xi + yi
    big_local = {n: w[n][0] for n in big}

    meta_cols, conv_cols = meta_tokens.shape[1], gd_conv_w.shape[-1]
    sm_all = _gather_all(_pack_lanes([meta_tokens, gd_conv_w[0]]), "gather_meta")
    sm_parts = [_unpack_lanes(sm_all[2 * s], [meta_tokens.shape, gd_conv_w[0].shape]) for s in range(4)]
    meta_full = jnp.concatenate([p[0] for p in sm_parts], axis=1)
    conv_full = jnp.concatenate([p[1] for p in sm_parts], axis=1)
    cvec = ci.reshape(1).astype(jnp.int32)
    late = [n for n in big if n != "w_in"]
    rows_full = lambda t: t.reshape(t.shape[0] * t.shape[1], t.shape[2])

    def pair_sums(names, gs):
        return [_add_half(gk, got, cvec, "add_half_" + n) for n, gk, got in zip(names, gs, _swap_halves(gs, "swap_" + names[0]))]

    def with_own(slabs, n):
        return lax.dynamic_update_index_in_dim(slabs, big_local[n].astype(BF16), shard, 0)

    class MeshComm:
        def proj_side(self):
            return _gather_side([big_local[n].astype(BF16) for n in late])

        def late_weights(self, landed):
            wl = {n: with_own(t, n) for n, t in zip(late, _forward_halves(landed, "forward_late"))}
            return (wl["w_branch_a"], wl["w_branch_b"], rows_full(wl["w_out"]), wl["w_ffn_in"], rows_full(wl["w_ffn_out"]))

        def early_grads_side(self, grads):
            self.early = list(grads)
            self.early_parts = pair_sums(self.early, [grads[n] for n in self.early])
            return _scatter_side(self.early_parts)

        def early_grads_done(self, arrived):
            self.early_arrived = arrived

        def last_grad_side(self, dw_in):
            self.last_parts = pair_sums(["w_in"], [dw_in])
            return _scatter_side(self.last_parts)

        def last_grad_done(self, arrived):
            self.last_arrived = arrived

    comm = MeshComm()
    w_in_slabs = with_own(_forward_halves(_gather_chips([big_local["w_in"].astype(BF16)]), "forward_w_in")[0], "w_in")
    d_model = w_in_slabs.shape[1]
    w_in_full = w_in_slabs.transpose(1, 0, 2).reshape(d_model, -1)
    loss, dx, g = _local_step(x[0], loss_target[0], meta_full, lb_logits, mix_norm_w, w_in_full, hg_norm_w, conv_full,
                              gd_a_log, gd_dt_bias, gd_norm_w, ffn_norm_w, final_norm_w, comm)
    loss = lax.psum(loss[0, 0], ("x", "y", "c"))

    parts = dict(zip(comm.early + ["w_in"], comm.early_parts + comm.last_parts))
    arrived = dict(zip(comm.early + ["w_in"], comm.early_arrived + comm.last_arrived))
    g_big = dict(zip(big, _share_halves([_sum_chips(arrived[n], parts[n], "sum_chips_" + n) for n in big])))

    small_shapes = [g[n].shape for n in SMALL]
    g_all = _gather_all(_pack_lanes([g[n] for n in SMALL]), "gather_small")
    g_small = dict(zip(SMALL, _unpack_lanes(_sum_slabs(g_all, "sum_small"), small_shapes)))
    g_small["meta_tokens"] = lax.dynamic_slice_in_dim(g_small["meta_tokens"], shard * meta_cols, meta_cols, axis=1)
    g_small["gd_conv_w"] = lax.dynamic_slice_in_dim(g_small["gd_conv_w"], shard * conv_cols, conv_cols, axis=1)

    grad, delta, new_m, new_v = {}, {}, {}, {}
    for n in big:
        grad[n] = g_big[n].reshape(w[n].shape)
        d_, m_, v_ = _adamw(g_big[n], big_local[n], m[n][0], v[n][0], "adamw_" + n)
        delta[n], new_m[n], new_v[n] = d_.reshape(w[n].shape), m_.reshape(w[n].shape), v_.reshape(w[n].shape)
    local_shapes = [w[n].shape for n in SMALL]
    d_, m_, v_ = _adamw(_pack_lanes([g_small[n] for n in SMALL]), _pack_lanes([w[n] for n in SMALL]),
                        _pack_lanes([m[n] for n in SMALL]), _pack_lanes([v[n] for n in SMALL]), "adamw_small")
    for n, gs_, dd, mm, vv in zip(SMALL, [g_small[n] for n in SMALL], _unpack_lanes(d_, local_shapes), _unpack_lanes(m_, local_shapes),
                                  _unpack_lanes(v_, local_shapes)):
        grad[n], delta[n], new_m[n], new_v[n] = gs_.reshape(w[n].shape), dd, mm, vv

    order = ["meta_tokens", "lb_logits", "mix_norm_w", "w_in", "hg_norm_w", "gd_conv_w", "gd_a_log", "gd_dt_bias", "gd_norm_w",
             "w_branch_a", "w_branch_b", "w_out", "ffn_norm_w", "w_ffn_in", "w_ffn_out", "final_norm_w"]
    return (loss, dx[None], *[grad[n] for n in order], *[delta[n] for n in order], *[new_m[n] for n in order],
            *[new_v[n] for n in order])
```

```python
import functools
import math

import jax
import jax.numpy as jnp
from jax import lax
from jax.experimental import pallas as pl
from jax.experimental.pallas import tpu as pltpu

F32, BF16 = jnp.float32, jnp.bfloat16
HI = lax.Precision.HIGHEST
EPS = 1e-6
D_MODEL = 2048
N_META = 16
FRONT = 256
CH = 64
SUB = 16
SUBH = 16
DH = 128
NH = 8
HW = NH * DH
CONV_K = 4
RT = 256
VMEM_LIMIT = 56 * 1024 * 1024
ADAM_LR, ADAM_B1, ADAM_B2, ADAM_EPS, ADAM_WD, ADAM_STEP = 0.001, 0.9, 0.999, 1e-08, 0.01, 10

NN = (((1,), (0,)), ((), ()))
NT = (((1,), (1,)), ((), ()))
TN = (((0,), (0,)), ((), ()))


def _dot(a, b, dn=NN):
    return lax.dot_general(a.astype(BF16), b.astype(BF16), dn, preferred_element_type=F32)


def _dotx(a, b, dn=NN):
    return lax.dot_general(a, b, dn, precision=HI, preferred_element_type=F32)


class _Side:
    def __init__(self, inputs, out_shapes, scratch, start, wait):
        self.inputs, self.out_shapes, self.scratch, self.start, self.wait = inputs, out_shapes, scratch, start, wait


def _call(body, *, name, grid, in_specs, out_specs, out_shape, scratch=(), sem=None, side=None):
    params = pltpu.CompilerParams(dimension_semantics=sem, vmem_limit_bytes=VMEM_LIMIT)
    if side is None:
        return pl.pallas_call(body, name=name, grid=grid, in_specs=in_specs, out_specs=out_specs, out_shape=out_shape,
                              scratch_shapes=list(scratch), compiler_params=params)
    single = not isinstance(out_specs, (list, tuple))
    out_specs, out_shape = ([out_specs], [out_shape]) if single else (list(out_specs), list(out_shape))
    ni, no, ns = len(in_specs), len(out_specs), len(scratch)
    nsi, nso = len(side.inputs), len(side.out_shapes)
    hbm = pl.BlockSpec(memory_space=pltpu.HBM)

    def wrapped(*refs):
        main_in, side_in = refs[:ni], refs[ni:ni + nsi]
        main_out, side_out = refs[ni + nsi:ni + nsi + no], refs[ni + nsi + no:ni + nsi + no + nso]
        main_scr, side_scr = refs[ni + nsi + no + nso:ni + nsi + no + nso + ns], refs[ni + nsi + no + nso + ns:]
        pids = [pl.program_id(d) for d in range(len(grid))]
        first = functools.reduce(lambda a, b: a & b, [p == 0 for p in pids])
        last = functools.reduce(lambda a, b: a & b, [p == g - 1 for p, g in zip(pids, grid)])

        @pl.when(first)
        def _():
            side.start(side_in, side_out, side_scr)

        body(*main_in, *main_out, *main_scr)

        @pl.when(last)
        def _():
            side.wait(side_in, side_out, side_scr)

    call = pl.pallas_call(wrapped, name=name, grid=grid, in_specs=list(in_specs) + [hbm] * nsi,
                          out_specs=out_specs + [hbm] * nso, out_shape=out_shape + list(side.out_shapes),
                          scratch_shapes=list(scratch) + list(side.scratch), compiler_params=params)

    def run(*args):
        outs = call(*args, *side.inputs)
        main = outs[0] if single else list(outs[:no])
        return main, list(outs[no:])

    return run


def _divmod(j, per):
    if per == 1:
        return j, 0
    return lax.div(j, jnp.int32(per)), lax.rem(j, jnp.int32(per))


def _sds(shape, dtype):
    return jax.ShapeDtypeStruct(tuple(shape), dtype)


def _sigmoid(x):
    return 0.5 * jnp.tanh(0.5 * x) + 0.5


def _silu(x):
    return x * _sigmoid(x)


def _dsilu(x):
    s = _sigmoid(x)
    return s * (1.0 + x * (1.0 - s))


def _tri(n, kind):
    r = lax.broadcasted_iota(jnp.int32, (n, n), 0)
    c = lax.broadcasted_iota(jnp.int32, (n, n), 1)
    return {"incl": r >= c, "strict": r > c, "upper": c >= r}[kind]


def _mm(a, b, mode, out_dtype, tm, tn, tk, name, add=None, n_outer=False, out_shards=0, side=None):
    sharded_b = b.ndim == 3
    if sharded_b:
        S, R, n = b.shape
        b_rows, b_cols = R, S * n
    else:
        b_rows, b_cols = b.shape
    if mode == "nn":
        (M, K), N, dn = a.shape, b_cols, NN
    elif mode == "nt":
        (M, K), N, dn = a.shape, b_rows, NT
    else:
        (K, M), N, dn = a.shape, b_cols, TN
    tm, tn, tk = min(tm, M), min(tn, N), min(tk, K)
    if sharded_b:
        tn, tk = (min(tn, n), tk) if mode != "nt" else (tn, min(tk, n))
    if out_shards:
        tn = min(tn, N // out_shards)
    assert M % tm == 0 and N % tn == 0 and K % tk == 0, (name, M, N, K, tm, tn, tk)
    nk = K // tk
    a_blk, a_idx = ((tm, tk), lambda i, j, k: (i, k)) if mode != "tn" else ((tk, tm), lambda i, j, k: (k, i))
    if not sharded_b:
        b_blk, b_idx = ((tk, tn), lambda i, j, k: (k, j)) if mode != "nt" else ((tn, tk), lambda i, j, k: (j, k))
    elif mode != "nt":
        per = n // tn
        assert n % tn == 0
        b_blk, b_idx = (None, tk, tn), lambda i, j, k: (_divmod(j, per)[0], k, _divmod(j, per)[1])
    else:
        per = n // tk
        assert n % tk == 0
        b_blk, b_idx = (None, tn, tk), lambda i, j, k: (_divmod(k, per)[0], j, _divmod(k, per)[1])
    if out_shards:
        per_o = N // out_shards // tn
        assert (N // out_shards) % tn == 0
        o_blk, o_idx = (None, tm, tn), lambda i, j, k: (_divmod(j, per_o)[0], i, _divmod(j, per_o)[1])
        o_shape = (out_shards, M, N // out_shards)
    else:
        o_blk, o_idx, o_shape = (tm, tn), (lambda i, j, k: (i, j)), (M, N)
    c_idx = lambda i, j, k: (i, j)
    if n_outer:
        sw = lambda f: (lambda j, i, k: f(i, j, k))
        a_idx, b_idx, o_idx, c_idx = sw(a_idx), sw(b_idx), sw(o_idx), sw(c_idx)
        grid = (N // tn, M // tm, nk)
    else:
        grid = (M // tm, N // tn, nk)
    has_add = add is not None

    def body(*refs):
        if has_add:
            a_ref, b_ref, c_ref, o_ref, acc_ref = refs
        else:
            a_ref, b_ref, o_ref, acc_ref = refs
            c_ref = None
        part = lax.dot_general(a_ref[...].astype(BF16), b_ref[...].astype(BF16), dn, preferred_element_type=F32)

        def fin(val):
            if has_add:
                val = val + c_ref[...]
            o_ref[...] = val.astype(out_dtype)

        if nk == 1:
            fin(part)
        else:
            k = pl.program_id(2)

            @pl.when(k == 0)
            def _():
                acc_ref[...] = part

            @pl.when(k > 0)
            def _():
                acc_ref[...] += part

            @pl.when(k == nk - 1)
            def _():
                fin(acc_ref[...])

    in_specs = [pl.BlockSpec(a_blk, a_idx), pl.BlockSpec(b_blk, b_idx)]
    args = [a, b]
    if has_add:
        in_specs.append(pl.BlockSpec((tm, tn), c_idx))
        args.append(add)
    acc_shape = (tm, tn) if nk > 1 else (8, 128)
    return _call(body, name=name, grid=grid, in_specs=in_specs, out_specs=pl.BlockSpec(o_blk, o_idx),
                 out_shape=_sds(o_shape, out_dtype), scratch=[pltpu.VMEM(acc_shape, F32)],
                 sem=("arbitrary",) * 3 if side is not None else ("parallel", "parallel", "arbitrary"), side=side)(*args)


def _rms1_fwd(x, meta, w):
    seq, d = x.shape
    nt = (FRONT + seq) // RT

    def body(x_ref, m_ref, w_ref, o_ref):
        i = pl.program_id(0)

        def norm(v):
            r = lax.rsqrt(jnp.mean(v * v, axis=-1, keepdims=True) + EPS)
            return (v * r * w_ref[...]).astype(BF16)

        @pl.when(i == 0)
        def _():
            o_ref[0:RT - N_META, :] = jnp.zeros((RT - N_META, d), BF16)
            o_ref[RT - N_META:RT, :] = norm(m_ref[...])

        @pl.when(i > 0)
        def _():
            o_ref[...] = norm(x_ref[...])

    return _call(body, name="rms1_fwd", grid=(nt,),
                 in_specs=[pl.BlockSpec((RT, d), lambda i: (jnp.maximum(i - 1, 0), 0)),
                           pl.BlockSpec((N_META, d), lambda i: (0, 0)),
                           pl.BlockSpec((1, d), lambda i: (0, 0))],
                 out_specs=pl.BlockSpec((RT, d), lambda i: (i, 0)),
                 out_shape=_sds((FRONT + seq, d), BF16), sem=("parallel",))(x, meta, w)


def _rms1_bwd(x, meta, w, dxn, dh1):
    seq, d = x.shape
    nt = (FRONT + seq) // RT

    def body(x_ref, m_ref, w_ref, g_ref, r_ref, dx_ref, dm_ref, dw_ref):
        i = pl.program_id(0)

        def bwd(v, g):
            r = lax.rsqrt(jnp.mean(v * v, axis=-1, keepdims=True) + EPS)
            vh = v * r
            gh = g * w_ref[...]
            return r * (gh - vh * jnp.mean(gh * vh, axis=-1, keepdims=True)), jnp.sum(g * vh, axis=0, keepdims=True)

        @pl.when(i == 0)
        def _():
            dm, dw = bwd(m_ref[...], g_ref[RT - N_META:RT, :])
            dm_ref[...] = dm
            dw_ref[...] = dw

        @pl.when(i > 0)
        def _():
            dx, dw = bwd(x_ref[...], g_ref[...])
            dx_ref[...] = dx + r_ref[...]
            dw_ref[...] += dw

    xs = pl.BlockSpec((RT, d), lambda i: (jnp.maximum(i - 1, 0), 0))
    return _call(body, name="rms1_bwd", grid=(nt,),
                 in_specs=[xs, pl.BlockSpec((N_META, d), lambda i: (0, 0)), pl.BlockSpec((1, d), lambda i: (0, 0)),
                           pl.BlockSpec((RT, d), lambda i: (i, 0)), xs],
                 out_specs=[xs, pl.BlockSpec((N_META, d), lambda i: (0, 0)), pl.BlockSpec((1, d), lambda i: (0, 0))],
                 out_shape=[_sds((seq, d), F32), _sds((N_META, d), F32), _sds((1, d), F32)],
                 sem=("arbitrary",))(x, meta, w, dxn, dh1)


def _merge_fwd(proj, za, zb):
    seq, d = za.shape
    off = FRONT // RT
    ca, cb = 8 * HW // d, 8 * HW // d + 1

    def body(ga_ref, gb_ref, za_ref, zb_ref, o_ref):
        o_ref[...] = (_sigmoid(ga_ref[...]) * za_ref[...] + _sigmoid(gb_ref[...]) * zb_ref[...]).astype(BF16)

    zs = pl.BlockSpec((RT, d), lambda i: (i, 0))
    return _call(body, name="merge_fwd", grid=(seq // RT,),
                 in_specs=[pl.BlockSpec((RT, d), lambda i: (i + off, ca)), pl.BlockSpec((RT, d), lambda i: (i + off, cb)), zs, zs],
                 out_specs=zs, out_shape=_sds((seq, d), BF16), sem=("parallel",))(proj, proj, za, zb)


def _merge_bwd(proj, za, zb, dmerged):
    seq, d = za.shape
    off = FRONT // RT
    ca, cb = 8 * HW // d, 8 * HW // d + 1
    nt = (FRONT + seq) // RT

    def body(ga_ref, gb_ref, za_ref, zb_ref, dm_ref, dza_ref, dzb_ref, dg_ref):
        i = pl.program_id(0)

        @pl.when(i < off)
        def _():
            dg_ref[...] = jnp.zeros((RT, 2 * d), BF16)

        @pl.when(i >= off)
        def _():
            sa, sb, dm = _sigmoid(ga_ref[...]), _sigmoid(gb_ref[...]), dm_ref[...]
            dza_ref[...] = (dm * sa).astype(BF16)
            dzb_ref[...] = (dm * sb).astype(BF16)
            dg_ref[:, 0:d] = (dm * za_ref[...] * sa * (1.0 - sa)).astype(BF16)
            dg_ref[:, d:2 * d] = (dm * zb_ref[...] * sb * (1.0 - sb)).astype(BF16)

    rs = pl.BlockSpec((RT, d), lambda i: (jnp.maximum(i - off, 0), 0))
    return _call(body, name="merge_bwd", grid=(nt,),
                 in_specs=[pl.BlockSpec((RT, d), lambda i: (i, ca)), pl.BlockSpec((RT, d), lambda i: (i, cb)), rs, rs, rs],
                 out_specs=[rs, rs, pl.BlockSpec((RT, 2 * d), lambda i: (i, 0))],
                 out_shape=[_sds((seq, d), BF16), _sds((seq, d), BF16), _sds((FRONT + seq, 2 * d), BF16)],
                 sem=("arbitrary",))(proj, proj, za, zb, dmerged)


def _resid_norm_fwd(x, mix, w):
    seq, d = x.shape

    def body(x_ref, m_ref, w_ref, h_ref, n_ref):
        h = x_ref[...] + m_ref[...]
        h_ref[...] = h
        r = lax.rsqrt(jnp.mean(h * h, axis=-1, keepdims=True) + EPS)
        n_ref[...] = (h * r * w_ref[...]).astype(BF16)

    rs = pl.BlockSpec((RT, d), lambda i: (i, 0))
    return _call(body, name="resid_norm_fwd", grid=(seq // RT,),
                 in_specs=[rs, rs, pl.BlockSpec((1, d), lambda i: (0, 0))], out_specs=[rs, rs],
                 out_shape=[_sds((seq, d), F32), _sds((seq, d), BF16)], sem=("parallel",))(x, mix, w)


def _resid_norm_bwd(h1, w, dn2, dh2):
    seq, d = h1.shape

    def body(h_ref, w_ref, g_ref, r_ref, o_ref, ob_ref, dw_ref):
        i = pl.program_id(0)
        h, g = h_ref[...], g_ref[...]
        r = lax.rsqrt(jnp.mean(h * h, axis=-1, keepdims=True) + EPS)
        hh = h * r
        gh = g * w_ref[...]
        dh = r_ref[...] + r * (gh - hh * jnp.mean(gh * hh, axis=-1, keepdims=True))
        o_ref[...] = dh
        ob_ref[...] = dh.astype(BF16)
        dw = jnp.sum(g * hh, axis=0, keepdims=True)

        @pl.when(i == 0)
        def _():
            dw_ref[...] = dw

        @pl.when(i > 0)
        def _():
            dw_ref[...] += dw

    rs = pl.BlockSpec((RT, d), lambda i: (i, 0))
    ws = pl.BlockSpec((1, d), lambda i: (0, 0))
    return _call(body, name="resid_norm_bwd", grid=(seq // RT,), in_specs=[rs, ws, rs, rs], out_specs=[rs, rs, ws],
                 out_shape=[_sds((seq, d), F32), _sds((seq, d), BF16), _sds((1, d), F32)], sem=("arbitrary",))(h1, w, dn2, dh2)


def _swiglu_fwd(gu):
    seq, f2 = gu.shape
    ff = f2 // 2
    tc = 512
    nb = ff // tc

    def body(g_ref, u_ref, o_ref):
        o_ref[...] = (_silu(g_ref[...].astype(F32)) * u_ref[...].astype(F32)).astype(BF16)

    return _call(body, name="swiglu_fwd", grid=(seq // RT, nb),
                 in_specs=[pl.BlockSpec((RT, tc), lambda i, j: (i, j)), pl.BlockSpec((RT, tc), lambda i, j: (i, j + nb))],
                 out_specs=pl.BlockSpec((RT, tc), lambda i, j: (i, j)), out_shape=_sds((seq, ff), BF16),
                 sem=("parallel", "parallel"))(gu, gu)


def _swiglu_bwd(gu, dact):
    seq, f2 = gu.shape
    ff = f2 // 2
    tc = 512
    nb = ff // tc

    def body(g_ref, u_ref, d_ref, dg_ref, du_ref):
        g, d = g_ref[...].astype(F32), d_ref[...].astype(F32)
        dg_ref[...] = (d * u_ref[...].astype(F32) * _dsilu(g)).astype(BF16)
        du_ref[...] = (d * _silu(g)).astype(BF16)

    bs = pl.BlockSpec((RT, tc), lambda i, j: (i, j))
    dgate, dup = _call(body, name="swiglu_bwd", grid=(seq // RT, nb),
                       in_specs=[bs, pl.BlockSpec((RT, tc), lambda i, j: (i, j + nb)), bs], out_specs=[bs, bs],
                       out_shape=[_sds((seq, ff), BF16), _sds((seq, ff), BF16)], sem=("parallel", "parallel"))(gu, gu, dact)
    return jnp.concatenate([dgate, dup], axis=1)


def _loss_head(h1, f, w, tgt):
    seq, d = h1.shape
    nt = seq // RT

    def body(h_ref, f_ref, w_ref, t_ref, l_ref, dh_ref, dhb_ref, dw_ref):
        i = pl.program_id(0)
        h = h_ref[...] + f_ref[...]
        r = lax.rsqrt(jnp.mean(h * h, axis=-1, keepdims=True) + EPS)
        hh = h * r
        err = hh * w_ref[...] - t_ref[...]
        l_ref[...] = jnp.full((8, 128), 0.5 * jnp.sum(jnp.mean(err * err, axis=-1, keepdims=True)), F32)
        dy = err * (1.0 / d)
        gh = dy * w_ref[...]
        dh = r * (gh - hh * jnp.mean(gh * hh, axis=-1, keepdims=True))
        dh_ref[...] = dh
        dhb_ref[...] = dh.astype(BF16)
        dw = jnp.sum(dy * hh, axis=0, keepdims=True)

        @pl.when(i == 0)
        def _():
            dw_ref[...] = dw

        @pl.when(i > 0)
        def _():
            dw_ref[...] += dw

    rs = pl.BlockSpec((RT, d), lambda i: (i, 0))
    ws = pl.BlockSpec((1, d), lambda i: (0, 0))
    return _call(body, name="loss_head", grid=(nt,), in_specs=[rs, rs, ws, rs],
                 out_specs=[pl.BlockSpec((8, 128), lambda i: (i, 0)), rs, rs, ws],
                 out_shape=[_sds((nt * 8, 128), F32), _sds((seq, d), F32), _sds((seq, d), BF16), _sds((1, d), F32)],
                 sem=("arbitrary",))(h1, f, w, tgt)


def _sum_tiles(lt):
    n = lt.shape[0]

    def body(l_ref, o_ref):
        v = l_ref[...]
        r = lax.broadcasted_iota(jnp.int32, v.shape, 0)
        c = lax.broadcasted_iota(jnp.int32, v.shape, 1)
        o_ref[...] = jnp.sum(jnp.where((r % 8 == 0) & (c == 0), v, 0.0), keepdims=True)

    return _call(body, name="loss_sum", grid=(1,), in_specs=[pl.BlockSpec((n, 128), lambda i: (0, 0))],
                 out_specs=pl.BlockSpec((1, 1), lambda i: (0, 0)), out_shape=_sds((1, 1), F32))(lt)


def _gated_norm_fwd(o, g, nw):
    r = lax.rsqrt(jnp.mean(o * o, axis=-1, keepdims=True) + EPS)
    return o * r * nw * _silu(g)


def _gated_norm_bwd(o, g, nw, dout):
    r = lax.rsqrt(jnp.mean(o * o, axis=-1, keepdims=True) + EPS)
    oh = o * r
    don = dout * _silu(g)
    dg = dout * (oh * nw) * _dsilu(g)
    dnw = jnp.sum(don * oh, axis=0, keepdims=True)
    doh = don * nw
    return r * (doh - oh * jnp.mean(doh * oh, axis=-1, keepdims=True)), dg, dnw


def _hg_gates(fs, lbl):
    l0, l1 = lbl[0:1, :], lbl[1:2, :]
    m = jnp.maximum(l0, l1)
    e0, e1 = jnp.exp(l0 - m), jnp.exp(l1 - m)
    lb = e0 / (e0 + e1)
    sig = _sigmoid(fs)
    f = lb + (1.0 - lb) * sig
    return lb, sig, f, jnp.log(f), (1.0 - lb) * _sigmoid(-fs)


def _cumsum(w):
    row = lax.broadcasted_iota(jnp.int32, w.shape, 0) & (CH - 1)
    s = 1
    while s < CH:
        w = w + jnp.where(row >= s, pltpu.roll(w, s, 0), 0.0)
        s *= 2
    return w


def _rcumsum(w):
    row = lax.broadcasted_iota(jnp.int32, w.shape, 0) & (CH - 1)
    s = 1
    while s < CH:
        w = w + jnp.where(row < CH - s, pltpu.roll(w, w.shape[0] - s, 0), 0.0)
        s *= 2
    return w


def _decay_blocks(q, k, b, p_ref):
    p_ref[...] = jnp.zeros((CH, CH), F32)
    m16 = _tri(SUBH, "incl")
    for I in range(CH // SUBH):
        s0 = I * SUBH
        bI, qI, kI = b[s0:s0 + SUBH], q[s0:s0 + SUBH], k[s0:s0 + SUBH]
        dec = jnp.exp(jnp.minimum(bI[:, None, :] - bI[None, :, :], 0.0))
        pii = jnp.sum(qI[:, None, :] * kI[None, :, :] * dec, axis=-1)
        p_ref[s0:s0 + SUBH, s0:s0 + SUBH] = jnp.where(m16, pii, 0.0)
        if I > 0:
            rI = b[s0 - 1:s0]
            qs = qI * jnp.exp(bI - rI)
            ks = k[0:s0] * jnp.exp(rI - b[0:s0])
            p_ref[s0:s0 + SUBH, 0:s0] = _dot(qs, ks, NT)


HPS = 8
HPS_HGRN2_FWD = 4
HB = 4
NHB = NH // HPS
OFF = FRONT // RT


def _head_specs(hps, rev=None):
    row = (lambda i: i) if rev is None else rev
    col = lambda g: pl.BlockSpec((RT, hps * DH), lambda h, i: (row(i), g * (NH // hps) + h))
    full = pl.BlockSpec((RT, hps * DH), lambda h, i: (row(i), h))
    real = pl.BlockSpec((RT, hps * DH), lambda h, i: (jnp.maximum(row(i) - OFF, 0), h))
    state = lambda cpt: pl.BlockSpec((hps, cpt, DH, DH), lambda h, i: (h, row(i), 0, 0))
    scal = pl.BlockSpec((hps, RT, DH), lambda h, i: (h, row(i), 0))
    return col, full, real, state, scal


def _hgrn2_fwd(proj, lb_logits, nw):
    tp = proj.shape[0]
    nt, cpt = tp // RT, RT // CH
    hps = HPS_HGRN2_FWD

    def body(q_ref, f_ref, i_ref, g_ref, lbl_ref, nw_ref, og_ref, or_ref, st_ref, s_ref, p_ref):
        @pl.when(pl.program_id(1) == 0)
        def _():
            s_ref[...] = jnp.zeros((hps, DH, DH), F32)

        def chunk(c, carry):
            rows = pl.ds(pl.multiple_of(c * CH, CH), CH)
            for hh in range(hps):
                cols = slice(hh * DH, (hh + 1) * DH)
                _, _, _, w, k = _hg_gates(f_ref[rows, cols], lbl_ref[:, cols])
                q, v = _silu(q_ref[rows, cols]), i_ref[rows, cols]
                b = _cumsum(w)
                st = s_ref[hh]
                st_ref[hh, c] = st
                _decay_blocks(q, k, b, p_ref.at[hh])
                o = _dot(q * jnp.exp(b), st, NT) + _dot(p_ref[hh], v)
                bl = b[CH - 1:CH]
                s_ref[hh] = st * jnp.exp(bl) + _dot(v, k * jnp.exp(bl - b), TN)
                or_ref[rows, cols] = o
                og_ref[rows, cols] = _gated_norm_fwd(o, g_ref[rows, cols], nw_ref[...]).astype(BF16)
            return carry

        lax.fori_loop(0, cpt, chunk, 0)

    col, full, real, state, _ = _head_specs(hps)
    return _call(body, name="hgrn2_fwd", grid=(NH // hps, nt),
                 in_specs=[col(0), col(1), col(2), col(3), pl.BlockSpec((2, hps * DH), lambda h, i: (0, h)),
                           pl.BlockSpec((1, DH), lambda h, i: (0, 0))],
                 out_specs=[real, full, state(cpt)],
                 out_shape=[_sds((tp - FRONT, HW), BF16), _sds((tp, HW), F32), _sds((NH, tp // CH, DH, DH), F32)],
                 scratch=[pltpu.VMEM((hps, DH, DH), F32), pltpu.VMEM((hps, CH, CH), F32)],
                 sem=("parallel", "arbitrary"))(proj, proj, proj, proj, lb_logits, nw)


def _hgrn2_bwd(proj, lb_logits, nw, o_raw, states, dog, side=None):
    tp = proj.shape[0]
    nt, cpt = tp // RT, RT // CH

    def body(q_ref, f_ref, i_ref, g_ref, lbl_ref, nw_ref, or_ref, st_ref, dog_ref,
             dq_ref, df_ref, di_ref, dg_ref, dl_ref, dnw_ref, ds_ref, p_ref, dk_ref, dqa_ref, do_ref):
        step = pl.program_id(1)

        @pl.when(step == 0)
        def _():
            ds_ref[...] = jnp.zeros((HPS, DH, DH), F32)
            dl_ref[...] = jnp.zeros((2, HPS * DH), F32)

        @pl.when((step == 0) & (pl.program_id(0) == 0))
        def _():
            dnw_ref[...] = jnp.zeros((1, DH), F32)

        front = nt - 1 - step < OFF
        for hh in range(HPS):
            cols = slice(hh * DH, (hh + 1) * DH)
            dog_t = jnp.where(front, 0.0, dog_ref[:, cols])
            do_t, dg_t, dnw = _gated_norm_bwd(or_ref[:, cols], g_ref[:, cols], nw_ref[...], dog_t)
            do_ref[:, cols] = do_t
            dg_ref[:, cols] = dg_t.astype(BF16)
            dnw_ref[...] += dnw
        tril = _tri(CH, "incl")
        m16 = _tri(SUBH, "incl")

        def chunk(cc, carry):
            c = cpt - 1 - cc
            rows = pl.ds(pl.multiple_of(c * CH, CH), CH)
            for hh in range(HPS):
                cols = slice(hh * DH, (hh + 1) * DH)
                fs = f_ref[rows, cols]
                lb, sig, f, w, k = _hg_gates(fs, lbl_ref[:, cols])
                hq = q_ref[rows, cols]
                q, v, do = _silu(hq), i_ref[rows, cols], do_ref[rows, cols]
                b = _cumsum(w)
                bl = b[CH - 1:CH]
                eb = jnp.exp(b)
                qs, kd = q * eb, k * jnp.exp(bl - b)
                st, dst = st_ref[hh, c], ds_ref[hh]
                _decay_blocks(q, k, b, p_ref.at[hh])
                dv = _dot(p_ref[hh], do, TN) + _dot(kd, dst, NT)
                dp = jnp.where(tril, _dotx(do, v, NT), 0.0)
                dqa, dka = dqa_ref.at[hh], dk_ref.at[hh]
                dqa[...] = eb * _dotx(do, st)
                dka[...] = jnp.exp(bl - b) * _dotx(v, dst)
                for I in range(CH // SUBH):
                    s0 = I * SUBH
                    bI, qI, kI = b[s0:s0 + SUBH], q[s0:s0 + SUBH], k[s0:s0 + SUBH]
                    dec = jnp.exp(jnp.minimum(bI[:, None, :] - bI[None, :, :], 0.0))
                    dpii = jnp.where(m16, dp[s0:s0 + SUBH, s0:s0 + SUBH], 0.0)[:, :, None] * dec
                    dqa[s0:s0 + SUBH, :] += jnp.sum(dpii * kI[None, :, :], axis=1)
                    dka[s0:s0 + SUBH, :] += jnp.sum(dpii * qI[:, None, :], axis=0)
                    if I > 0:
                        rI = b[s0 - 1:s0]
                        eq, ek = jnp.exp(bI - rI), jnp.exp(rI - b[0:s0])
                        dpij = dp[s0:s0 + SUBH, 0:s0]
                        dqa[s0:s0 + SUBH, :] += eq * _dotx(dpij, k[0:s0] * ek)
                        dka[0:s0, :] += ek * _dotx(dpij, qI * eq, TN)
                dq, dk = dqa[...], dka[...]
                st_end = st * jnp.exp(bl) + _dotx(v, kd, TN)
                dw = _rcumsum(q * dq - k * dk) + jnp.sum(dst * st_end, axis=0, keepdims=True)
                ds_ref[hh] = dst * jnp.exp(bl) + _dotx(do, qs, TN)
                one_m = 1.0 - sig
                dq_ref[rows, cols] = (dq * _dsilu(hq)).astype(BF16)
                df_ref[rows, cols] = ((dw / f - dk) * (1.0 - lb) * sig * one_m).astype(BF16)
                di_ref[rows, cols] = dv.astype(BF16)
                dl_ref[0:1, cols] += jnp.sum((dw / f - dk) * one_m, axis=0, keepdims=True)
            return carry

        lax.fori_loop(0, cpt, chunk, 0)

        @pl.when(step == nt - 1)
        def _():
            lbl = lbl_ref[...]
            l0, l1 = lbl[0:1, :], lbl[1:2, :]
            m = jnp.maximum(l0, l1)
            e0, e1 = jnp.exp(l0 - m), jnp.exp(l1 - m)
            p0 = e0 / (e0 + e1)
            dl0 = dl_ref[0:1, :] * p0 * (1.0 - p0)
            dl_ref[0:1, :] = dl0
            dl_ref[1:2, :] = -dl0

    col, full, real, state, _ = _head_specs(HPS, lambda i: nt - 1 - i)
    lbs = pl.BlockSpec((2, HPS * DH), lambda h, i: (0, h))
    return _call(body, name="hgrn2_bwd", grid=(NHB, nt),
                 in_specs=[col(0), col(1), col(2), col(3), lbs, pl.BlockSpec((1, DH), lambda h, i: (0, 0)), full,
                           state(cpt), real],
                 out_specs=[full, full, full, full, lbs, pl.BlockSpec((1, DH), lambda h, i: (0, 0))],
                 out_shape=[_sds((tp, HW), BF16)] * 4 + [_sds((2, HW), F32), _sds((1, DH), F32)],
                 scratch=[pltpu.VMEM((HPS, DH, DH), F32), pltpu.VMEM((HPS, CH, CH), F32), pltpu.VMEM((HPS, CH, DH), F32),
                          pltpu.VMEM((HPS, CH, DH), F32), pltpu.VMEM((RT, HPS * DH), F32)],
                 sem=("arbitrary", "arbitrary"), side=side)(proj, proj, proj, proj, lb_logits, nw, o_raw, states, dog)


GQ0 = 4 * HW
CW = 3 * HW


def _gd_scalars(ab, alog, dtb):
    g = -jnp.exp(alog) * jax.nn.softplus(ab + dtb)
    return g, _sigmoid(ab)


def _conv_ext_specs(row_of):
    main = [pl.BlockSpec((RT, HW), lambda i, g=g: (row_of(i), GQ0 // HW + g)) for g in range(3)]
    prev = [pl.BlockSpec((8, HW), lambda i, g=g: (jnp.maximum(row_of(i) * (RT // 8) - 1, 0), GQ0 // HW + g)) for g in range(3)]
    return main + prev


def _conv_fill(ext_ref, xs, xps, first):
    for g in range(3):
        ext_ref[0:8, g * HW:(g + 1) * HW] = jnp.where(first, 0.0, xps[g][...])
        ext_ref[8:8 + RT, g * HW:(g + 1) * HW] = xs[g][...]


def _conv_apply(ext_ref, cw):
    y = cw[CONV_K - 1:CONV_K, :] * ext_ref[pl.ds(8, RT), :]
    for s in range(1, CONV_K):
        y += cw[CONV_K - 1 - s:CONV_K - s, :] * ext_ref[pl.ds(8 - s, RT), :]
    return y


def _gdn_prep_fwd(proj, pab, conv_w, alog, dtb):
    tp = proj.shape[0]
    nt = tp // RT

    def body(x0, x1, x2, p0, p1, p2, ab_ref, cw_ref, al_ref, dt_ref, q_ref, k_ref, v_ref, g_ref, b_ref, ext_ref):
        _conv_fill(ext_ref, (x0, x1, x2), (p0, p1, p2), pl.program_id(0) == 0)
        a = _silu(_conv_apply(ext_ref, cw_ref[...]))
        for h in range(NH):
            for part, ref, sc in ((0, q_ref, DH ** -0.5), (1, k_ref, 1.0)):
                seg = a[:, part * HW + h * DH:part * HW + (h + 1) * DH]
                ref[:, h * DH:(h + 1) * DH] = seg * (lax.rsqrt(jnp.sum(seg * seg, axis=-1, keepdims=True) + EPS) * sc)
        v_ref[...] = a[:, 2 * HW:3 * HW]
        g, beta = _gd_scalars(ab_ref[...], al_ref[...], dt_ref[...])
        for h in range(NH):
            g_ref[h] = jnp.broadcast_to(g[:, h:h + 1], (RT, DH))
            b_ref[h] = jnp.broadcast_to(beta[:, NH + h:NH + h + 1], (RT, DH))

    hs = pl.BlockSpec((RT, HW), lambda i: (i, 0))
    sc = pl.BlockSpec((NH, RT, DH), lambda i: (0, i, 0))
    one = pl.BlockSpec((1, DH), lambda i: (0, 0))
    return _call(body, name="gdn_prep_fwd", grid=(nt,),
                 in_specs=_conv_ext_specs(lambda i: i) + [pl.BlockSpec((RT, DH), lambda i: (i, 0)),
                                                           pl.BlockSpec((CONV_K, CW), lambda i: (0, 0)), one, one],
                 out_specs=[hs, hs, hs, sc, sc],
                 out_shape=[_sds((tp, HW), F32)] * 3 + [_sds((NH, tp, DH), F32)] * 2,
                 scratch=[pltpu.VMEM((RT + 8, CW), F32)], sem=("parallel",))(*([proj] * 6), pab, conv_w, alog, dtb)


def _gdn_prep_bwd(proj, pab, conv_w, alog, dtb, dq, dk, dv, dgb, dbb):
    tp = proj.shape[0]
    nt = tp // RT

    def body(x0, x1, x2, p0, p1, p2, ab_ref, cw_ref, al_ref, dt_ref, dq_ref, dk_ref, dv_ref, dg_ref, db_ref,
             dx_ref, dab_ref, dcw_ref, dal_ref, ddt_ref, ext_ref, dy_ref):
        step = pl.program_id(0)
        i = nt - 1 - step

        @pl.when(step == 0)
        def _():
            dy_ref[RT:RT + 8, :] = jnp.zeros((8, CW), F32)
            dcw_ref[...] = jnp.zeros((8, CW), F32)
            dal_ref[...] = jnp.zeros((1, DH), F32)
            ddt_ref[...] = jnp.zeros((1, DH), F32)

        _conv_fill(ext_ref, (x0, x1, x2), (p0, p1, p2), i == 0)
        cw = cw_ref[...]
        y = _conv_apply(ext_ref, cw)
        a = _silu(y)
        dsl = _dsilu(y)
        for h in range(NH):
            for part, ref, sc in ((0, dq_ref, DH ** -0.5), (1, dk_ref, 1.0)):
                lo = part * HW + h * DH
                seg = a[:, lo:lo + DH]
                r = lax.rsqrt(jnp.sum(seg * seg, axis=-1, keepdims=True) + EPS)
                xh = seg * r
                dxh = ref[:, h * DH:(h + 1) * DH] * sc
                dy_ref[0:RT, lo:lo + DH] = r * (dxh - xh * jnp.sum(dxh * xh, axis=-1, keepdims=True)) * dsl[:, lo:lo + DH]
        dy_ref[0:RT, 2 * HW:3 * HW] = dv_ref[...] * dsl[:, 2 * HW:3 * HW]
        dy = dy_ref[0:RT, :]
        dx = cw[CONV_K - 1:CONV_K, :] * dy
        dcw_ref[CONV_K - 1:CONV_K, :] += jnp.sum(dy * ext_ref[pl.ds(8, RT), :], axis=0, keepdims=True)
        for s in range(1, CONV_K):
            dx += cw[CONV_K - 1 - s:CONV_K - s, :] * dy_ref[pl.ds(s, RT), :]
            dcw_ref[CONV_K - 1 - s:CONV_K - s, :] += jnp.sum(dy * ext_ref[pl.ds(8 - s, RT), :], axis=0, keepdims=True)
        dx_ref[...] = dx.astype(BF16)
        dy_ref[RT:RT + 8, :] = dy[0:8, :]
        ab = ab_ref[...]
        g, beta = _gd_scalars(ab, al_ref[...], dt_ref[...])
        lane = lax.broadcasted_iota(jnp.int32, (RT, DH), 1)
        dgl = jnp.zeros((RT, DH), F32)
        dbl = jnp.zeros((RT, DH), F32)
        for h in range(NH):
            dgl = jnp.where(lane == h, dg_ref[h], dgl)
            dbl = jnp.where(lane == NH + h, db_ref[h], dbl)
        dsp = dgl * (-jnp.exp(al_ref[...])) * _sigmoid(ab + dt_ref[...])
        dab_ref[...] = (dsp + dbl * beta * (1.0 - beta)).astype(BF16)
        ddt_ref[...] += jnp.sum(dsp, axis=0, keepdims=True)
        dal_ref[...] += jnp.sum(dgl * g, axis=0, keepdims=True)

    hs = pl.BlockSpec((RT, HW), lambda s: (nt - 1 - s, 0))
    sc = pl.BlockSpec((NH, RT, DH), lambda s: (0, nt - 1 - s, 0))
    one = pl.BlockSpec((1, DH), lambda s: (0, 0))
    xs = pl.BlockSpec((RT, CW), lambda s: (nt - 1 - s, 0))
    return _call(body, name="gdn_prep_bwd", grid=(nt,),
                 in_specs=_conv_ext_specs(lambda s: nt - 1 - s) + [
                     pl.BlockSpec((RT, DH), lambda s: (nt - 1 - s, 0)), pl.BlockSpec((CONV_K, CW), lambda s: (0, 0)),
                     one, one, hs, hs, hs, sc, sc],
                 out_specs=[xs, pl.BlockSpec((RT, DH), lambda s: (nt - 1 - s, 0)), pl.BlockSpec((8, CW), lambda s: (0, 0)), one, one],
                 out_shape=[_sds((tp, CW), BF16), _sds((tp, DH), BF16), _sds((8, CW), F32), _sds((1, DH), F32), _sds((1, DH), F32)],
                 scratch=[pltpu.VMEM((RT + 8, CW), F32), pltpu.VMEM((RT + 8, CW), F32)],
                 sem=("arbitrary",))(*([proj] * 6), pab, conv_w, alog, dtb, dq, dk, dv, dgb, dbb)


HS = HB * CH


def _stack_heads(ref, rows, base):
    return jnp.concatenate([ref[rows, (base + hh) * DH:(base + hh + 1) * DH] for hh in range(HB)], axis=0)


def _stack_scal(ref, rows, base):
    return jnp.concatenate([ref[base + hh, rows, :] for hh in range(HB)], axis=0)


def _store_heads(ref, rows, base, val):
    for hh in range(HB):
        ref[rows, (base + hh) * DH:(base + hh + 1) * DH] = val[hh * CH:(hh + 1) * CH].astype(ref.dtype)


def _bd_masks():
    r = lax.broadcasted_iota(jnp.int32, (HS, HS), 0)
    c = lax.broadcasted_iota(jnp.int32, (HS, HS), 1)
    same = lax.shift_right_logical(r, int(math.log2(CH))) == lax.shift_right_logical(c, int(math.log2(CH)))
    return same & (r >= c), same & (r > c)


def _unit_lower_inverse(a):
    n = a.shape[0]
    r = lax.broadcasted_iota(jnp.int32, (n, n), 0)
    c = lax.broadcasted_iota(jnp.int32, (n, n), 1)
    blk_of = lambda t, size: lax.shift_right_logical(t, int(math.log2(size)))
    a16 = jnp.where(blk_of(r, SUB) == blk_of(c, SUB), a, 0.0)
    x = (r == c).astype(F32) - a16
    p = a16
    for _ in range(3):
        p = _dot(p, p)
        x = x + _dot(x, p)
    for blk in (2 * SUB, 4 * SUB):
        off = jnp.where((blk_of(r, blk) == blk_of(c, blk)) & (blk_of(r, blk // 2) != blk_of(c, blk // 2)), a, 0.0)
        x = x - _dot(x, _dot(off, x))
    return x


def _gdn_chunk_common(q, k, v, gl, bt, incl, strict):
    gc = _cumsum(gl)
    e = jnp.exp(gc)
    rel = jnp.exp(jnp.minimum(gc[:, 0:1] - gc.T[0:1, :], 0.0))
    kb = bt * k
    a = jnp.where(strict, bt[:, 0:1] * _dot(k, k, NT) * rel, 0.0)
    x = _unit_lower_inverse(a)
    wu = _dot(x, jnp.concatenate([kb * e, bt * v], axis=1))
    attn = jnp.where(incl, _dot(q, k, NT) * rel, 0.0)
    return gc, e, rel, kb, a, x, wu[:, 0:DH], wu[:, DH:2 * DH], attn


def _gdn_fwd(q, k, v, gb, bb, proj, nw):
    tp = q.shape[0]
    nt, cpt = tp // RT, RT // CH

    def body(q_ref, k_ref, v_ref, g_ref, b_ref, z_ref, nw_ref, og_ref, or_ref, st_ref, s_ref):
        @pl.when(pl.program_id(1) == 0)
        def _():
            s_ref[...] = jnp.zeros((HPS, DH, DH), F32)

        incl, strict = _bd_masks()

        def chunk(c, carry):
            rows = pl.ds(pl.multiple_of(c * CH, CH), CH)
            for base in range(0, HPS, HB):
                qc, kc, vc = _stack_heads(q_ref, rows, base), _stack_heads(k_ref, rows, base), _stack_heads(v_ref, rows, base)
                gc, e, rel, kb, a, x, w, u, attn = _gdn_chunk_common(qc, kc, vc, _stack_scal(g_ref, rows, base),
                                                                    _stack_scal(b_ref, rows, base), incl, strict)
                qe = qc * e
                ws, qs = [], []
                for hh in range(HB):
                    blk = slice(hh * CH, (hh + 1) * CH)
                    s = s_ref[base + hh]
                    st_ref[base + hh, c] = s
                    both = _dot(jnp.concatenate([w[blk], qe[blk]], axis=0), s)
                    ws.append(both[0:CH])
                    qs.append(both[CH:2 * CH])
                vn = u - jnp.concatenate(ws, axis=0)
                o = jnp.concatenate(qs, axis=0) + _dot(attn, vn)
                for hh in range(HB):
                    blk = slice(hh * CH, (hh + 1) * CH)
                    gl = gc[(hh + 1) * CH - 1:(hh + 1) * CH]
                    s_ref[base + hh] = s_ref[base + hh] * jnp.exp(gl) + _dot(kc[blk] * jnp.exp(gl - gc[blk]), vn[blk], TN)
                _store_heads(or_ref, rows, base, o)
                for hh in range(HB):
                    cols = slice((base + hh) * DH, (base + hh + 1) * DH)
                    og_ref[rows, cols] = _gated_norm_fwd(o[hh * CH:(hh + 1) * CH], z_ref[rows, cols], nw_ref[...]).astype(BF16)
            return carry

        lax.fori_loop(0, cpt, chunk, 0)

    col, full, real, state, scal = _head_specs(HPS)
    return _call(body, name="gdn_fwd", grid=(NHB, nt),
                 in_specs=[full, full, full, scal, scal, col(7), pl.BlockSpec((1, DH), lambda h, i: (0, 0))],
                 out_specs=[real, full, state(cpt)],
                 out_shape=[_sds((tp - FRONT, HW), BF16), _sds((tp, HW), F32), _sds((NH, tp // CH, DH, DH), F32)],
                 scratch=[pltpu.VMEM((HPS, DH, DH), F32)], sem=("parallel", "arbitrary"))(q, k, v, gb, bb, proj, nw)


def _gdn_bwd(q, k, v, gb, bb, proj, nw, o_raw, states, dog):
    tp = q.shape[0]
    nt, cpt = tp // RT, RT // CH

    def body(q_ref, k_ref, v_ref, g_ref, b_ref, z_ref, nw_ref, or_ref, st_ref, dog_ref,
             dq_ref, dk_ref, dv_ref, dg_ref, db_ref, dz_ref, dnw_ref, ds_ref, do_ref):
        step = pl.program_id(1)

        @pl.when(step == 0)
        def _():
            ds_ref[...] = jnp.zeros((HPS, DH, DH), F32)

        @pl.when((step == 0) & (pl.program_id(0) == 0))
        def _():
            dnw_ref[...] = jnp.zeros((1, DH), F32)

        front = nt - 1 - step < OFF
        for hh in range(HPS):
            cols = slice(hh * DH, (hh + 1) * DH)
            dog_t = jnp.where(front, 0.0, dog_ref[:, cols])
            do_t, dz_t, dnw = _gated_norm_bwd(or_ref[:, cols], z_ref[:, cols], nw_ref[...], dog_t)
            do_ref[:, cols] = do_t
            dz_ref[:, cols] = dz_t.astype(BF16)
            dnw_ref[...] += dnw
        incl, strict = _bd_masks()
        last_row = (lax.broadcasted_iota(jnp.int32, (CH, 1), 0) == CH - 1)

        def rsum(t):
            return jnp.sum(t, axis=-1, keepdims=True)

        def chunk(cc, carry):
            c = cpt - 1 - cc
            rows = pl.ds(pl.multiple_of(c * CH, CH), CH)
            for base in range(0, HPS, HB):
                qc, kc, vc, do = (_stack_heads(q_ref, rows, base), _stack_heads(k_ref, rows, base), _stack_heads(v_ref, rows, base),
                                  _stack_heads(do_ref, rows, base))
                bt = _stack_scal(b_ref, rows, base)
                gc, e, rel, kb, a, x, w, u, attn = _gdn_chunk_common(qc, kc, vc, _stack_scal(g_ref, rows, base), bt, incl, strict)
                qe = qc * e
                heads = [slice(hh * CH, (hh + 1) * CH) for hh in range(HB)]
                gls = [gc[(hh + 1) * CH - 1:(hh + 1) * CH] for hh in range(HB)]
                cdec = jnp.concatenate([jnp.exp(gl - gc[blk]) for gl, blk in zip(gls, heads)], axis=0)
                kcd = kc * cdec
                vn = u - jnp.concatenate([_dot(w[blk], st_ref[base + hh, c]) for hh, blk in enumerate(heads)], axis=0)
                dos = jnp.concatenate([_dot(do[blk], st_ref[base + hh, c], NT) for hh, blk in enumerate(heads)], axis=0)
                kds = jnp.concatenate([_dot(kcd[blk], ds_ref[base + hh]) for hh, blk in enumerate(heads)], axis=0)
                vds = jnp.concatenate([_dot(vn[blk], ds_ref[base + hh], NT) for hh, blk in enumerate(heads)], axis=0)
                dvn = _dot(attn, do, TN) + kds
                dattn = jnp.where(incl, _dot(do, vn, NT), 0.0)
                dar = dattn * rel
                dq = _dot(dar, kc) + e * dos
                dk = _dot(dar, qc, TN) + cdec * vds
                dc = cdec[:, 0:1] * rsum(kc * vds)
                dgc = rsum(qe * dos) - dc
                dw = -jnp.concatenate([_dot(dvn[blk], st_ref[base + hh, c], NT) for hh, blk in enumerate(heads)], axis=0)
                extra = []
                for hh, blk in enumerate(heads):
                    s, dsn = st_ref[base + hh, c], ds_ref[base + hh]
                    el = jnp.exp(gls[hh])
                    dglast = jnp.sum(dc[blk], axis=0, keepdims=True) + el[:, 0:1] * jnp.sum(rsum(dsn * s), axis=0, keepdims=True)
                    extra.append(jnp.where(last_row, dglast, 0.0))
                    ds_ref[base + hh] = dsn * el + _dot(jnp.concatenate([qe[blk], -w[blk]], axis=0),
                                                 jnp.concatenate([do[blk], dvn[blk]], axis=0), TN)
                dr = _dot(x, jnp.concatenate([dw, dvn], axis=1), TN)
                drw, dru = dr[:, 0:DH], dr[:, DH:2 * DH]
                da = -jnp.where(strict, _dot(dr, jnp.concatenate([w, u], axis=1), NT), 0.0)
                dar2 = da * rel
                dkb = _dot(dar2, kc)
                rwk = rsum(drw * kc)
                dk = dk + _dot(dar2, kb, TN) + bt * dkb + (bt * e) * drw
                dbeta = rsum(dkb * kc) + e[:, 0:1] * rwk + rsum(dru * vc)
                z = dattn * attn + da * a
                dgc = dgc + bt[:, 0:1] * e[:, 0:1] * rwk + rsum(z) - rsum(z.T) + jnp.concatenate(extra, axis=0)
                _store_heads(dq_ref, rows, base, dq)
                _store_heads(dk_ref, rows, base, dk)
                _store_heads(dv_ref, rows, base, bt * dru)
                dg = _rcumsum(jnp.broadcast_to(dgc, (HS, DH)))
                dbb = jnp.broadcast_to(dbeta, (HS, DH))
                for hh, blk in enumerate(heads):
                    dg_ref[base + hh, rows, :] = dg[blk]
                    db_ref[base + hh, rows, :] = dbb[blk]
            return carry

        lax.fori_loop(0, cpt, chunk, 0)

    col, full, real, state, scal = _head_specs(HPS, lambda i: nt - 1 - i)
    one = pl.BlockSpec((1, DH), lambda h, i: (0, 0))
    return _call(body, name="gdn_bwd", grid=(NHB, nt),
                 in_specs=[full, full, full, scal, scal, col(7), one, full, state(cpt), real],
                 out_specs=[full, full, full, scal, scal, full, one],
                 out_shape=[_sds((tp, HW), F32)] * 3 + [_sds((NH, tp, DH), F32)] * 2 + [_sds((tp, HW), BF16), _sds((1, DH), F32)],
                 scratch=[pltpu.VMEM((HPS, DH, DH), F32), pltpu.VMEM((RT, HPS * DH), F32)],
                 sem=("arbitrary", "arbitrary"))(q, k, v, gb, bb, proj, nw, o_raw, states, dog)


MAIN_W = 8 * HW
AB_W = 2 * NH


def _split_w_in(w_in):
    main = jnp.concatenate([w_in[:, :MAIN_W], w_in[:, MAIN_W + AB_W:]], axis=1)
    ab = jnp.pad(w_in[:, MAIN_W:MAIN_W + AB_W], ((0, 0), (0, DH - AB_W)))
    return main, ab


def _pad_lanes(v):
    return jnp.pad(v, ((0, 0), (0, DH - v.shape[1])))


class _NoComm:
    def __init__(self, late):
        self.late = late

    def proj_side(self):
        return None

    def late_weights(self, side_outs):
        return self.late

    def early_grads_side(self, grads):
        return None

    def early_grads_done(self, side_outs):
        pass

    def last_grad_side(self, dw_in):
        return None

    def last_grad_done(self, side_outs):
        pass


def _local_step(x, tgt, meta, lb_logits, mix_w, w_in, hg_nw, conv_w, a_log, dt_bias, gd_nw, ffn_nw, final_w, comm):
    w_main, w_ab = _split_w_in(w_in)
    alog, dtb = _pad_lanes(a_log), _pad_lanes(dt_bias)
    final_w = final_w.reshape(1, -1)
    rows4 = lambda t: t.reshape(4, t.shape[0] // 4, t.shape[1])
    xn = _rms1_fwd(x, meta, mix_w)
    side = comm.proj_side()
    proj = _mm(xn, w_main, "nn", F32, 768, 2048, 2048, "proj_main", n_outer=True, side=side)
    proj, landed = proj if side is not None else (proj, None)
    w_a, w_b, w_out, w_ffn_in, w_ffn_out = comm.late_weights(landed)
    pab = _mm(xn, w_ab, "nn", F32, 768, 128, 2048, "proj_ab")
    oa_g, oa_raw, st_a = _hgrn2_fwd(proj, lb_logits, hg_nw)
    q, k, v, gb, bb = _gdn_prep_fwd(proj, pab, conv_w, alog, dtb)
    ob_g, ob_raw, st_b = _gdn_fwd(q, k, v, gb, bb, proj, gd_nw)
    za = _mm(oa_g, w_a, "nn", F32, 1024, 512, 1024, "branch_a", n_outer=True)
    zb = _mm(ob_g, w_b, "nn", F32, 1024, 512, 1024, "branch_b", n_outer=True)
    merged = _merge_fwd(proj, za, zb)
    mix = _mm(merged, w_out, "nn", F32, 1024, 2048, 2048, "mix_out")
    h1, n2 = _resid_norm_fwd(x, mix, ffn_nw)
    gu = _mm(n2, w_ffn_in, "nn", BF16, 1024, 1408, 2048, "ffn_in", n_outer=True)
    act = _swiglu_fwd(gu)
    f = _mm(act, w_ffn_out, "nn", F32, 1024, 2048, 1408, "ffn_out")
    lt, dh2, dh2b, dfinal = _loss_head(h1, f, final_w, tgt)
    loss = _sum_tiles(lt)
    dact = _mm(dh2b, w_ffn_out, "nt", BF16, 1024, 1408, 2048, "d_act", n_outer=True)
    dw_ffn_out = rows4(_mm(act, dh2b, "tn", F32, 512, 2048, 2048, "dw_ffn_out"))
    dgu = _swiglu_bwd(gu, dact)
    dn2 = _mm(dgu, w_ffn_in, "nt", F32, 1024, 2048, 1408, "d_n2")
    dw_ffn_in = _mm(n2, dgu, "tn", F32, 1024, 1408, 2048, "dw_ffn_in", out_shards=4)
    dh1, dh1b, dffn_nw = _resid_norm_bwd(h1, ffn_nw, dn2, dh2)
    dmerged = _mm(dh1b, w_out, "nt", F32, 1024, 2048, 2048, "d_merged")
    dw_out = _mm(merged, dh1b, "tn", F32, 2048, 1024, 1024, "dw_out")
    dza, dzb, dgate = _merge_bwd(proj, za, zb, dmerged)
    doa = _mm(dza, w_a, "nt", F32, 1024, 1024, 512, "d_oa")
    dob = _mm(dzb, w_b, "nt", F32, 1024, 1024, 512, "d_ob")
    dw_a = _mm(oa_g, dza, "tn", F32, 1024, 512, 1024, "dw_branch_a", out_shards=4)
    dw_b = _mm(ob_g, dzb, "tn", F32, 1024, 512, 1024, "dw_branch_b", out_shards=4)
    early = dict(w_ffn_in=dw_ffn_in, w_ffn_out=dw_ffn_out, w_out=rows4(dw_out), w_branch_a=dw_a, w_branch_b=dw_b)
    side = comm.early_grads_side(early)
    hg = _hgrn2_bwd(proj, lb_logits, hg_nw, oa_raw, st_a, doa, side=side)
    if side is not None:
        hg, arrived = hg
        comm.early_grads_done(arrived)
    dhq, dhf, dhi, dhg, dlbl, dhg_nw = hg
    dq, dk, dv, dg, dbeta, dz, dgd_nw = _gdn_bwd(q, k, v, gb, bb, proj, gd_nw, ob_raw, st_b, dob)
    dx3, dab, dconv, dalog, ddtb = _gdn_prep_bwd(proj, pab, conv_w, alog, dtb, dq, dk, dv, dg, dbeta)
    dproj = jnp.concatenate([dhq, dhf, dhi, dhg, dx3, dz, dgate], axis=1)
    dw_main = _mm(xn, dproj, "tn", F32, 1024, 1024, 2816, "dw_in_main")
    dw_ab = _mm(xn, dab, "tn", F32, 2048, 128, 768, "dw_in_ab")
    d_model = dw_main.shape[0]
    dw_in = jnp.concatenate([dw_main[:, :MAIN_W], dw_ab[:, :AB_W], dw_main[:, MAIN_W:]], axis=1)
    dw_in = dw_in.reshape(d_model, 4, -1).transpose(1, 0, 2)
    side = comm.last_grad_side(dw_in)
    dxn = _mm(dproj, w_main, "nt", F32, 768, 2048, 2048, "d_xn", side=side)
    if side is not None:
        dxn, arrived = dxn
        comm.last_grad_done(arrived)
    dxn = _mm(dab, w_ab, "nt", F32, 768, 2048, 128, "d_xn_ab", add=dxn)
    dx, dmeta, dmix_w = _rms1_bwd(x, meta, mix_w, dxn, dh1)
    grads = dict(meta_tokens=dmeta, lb_logits=dlbl, mix_norm_w=dmix_w, w_in=dw_in,
                 hg_norm_w=dhg_nw, gd_conv_w=dconv[:CONV_K], gd_a_log=dalog[:, :NH],
                 gd_dt_bias=ddtb[:, :NH], gd_norm_w=dgd_nw, w_branch_a=dw_a, w_branch_b=dw_b,
                 w_out=rows4(dw_out), ffn_norm_w=dffn_nw, w_ffn_in=dw_ffn_in, w_ffn_out=dw_ffn_out,
                 final_norm_w=dfinal.reshape(-1))
    return loss, dx, grads


def _adamw(g, w, m, v, name):
    rows, cols = g.shape
    tr = rows
    for cand in (128, 64, 32, 16, 8):
        if rows % cand == 0 and rows > cand:
            tr = cand
            break

    def body(g_ref, w_ref, m_ref, v_ref, d_ref, nm_ref, nv_ref):
        gg = g_ref[...]
        nm = ADAM_B1 * m_ref[...] + (1.0 - ADAM_B1) * gg
        nv = ADAM_B2 * v_ref[...] + (1.0 - ADAM_B2) * (gg * gg)
        m_hat = nm / (1.0 - ADAM_B1 ** ADAM_STEP)
        v_hat = nv / (1.0 - ADAM_B2 ** ADAM_STEP)
        d_ref[...] = -ADAM_LR * (m_hat / (jnp.sqrt(v_hat) + ADAM_EPS) + ADAM_WD * w_ref[...])
        nm_ref[...] = nm
        nv_ref[...] = nv

    bs = pl.BlockSpec((tr, cols), lambda i: (i, 0))
    return _call(body, name=name, grid=(rows // tr,), in_specs=[bs] * 4, out_specs=[bs] * 3,
                 out_shape=[_sds((rows, cols), F32)] * 3, sem=("parallel",))(g, w, m, v)


HBM = pl.BlockSpec(memory_space=pltpu.HBM)
MESH = pl.DeviceIdType.MESH


def _place():
    x, y, c = lax.axis_index("x"), lax.axis_index("y"), lax.axis_index("c")
    return x, y, c, [(1 - x, y), (x, 1 - y), (1 - x, 1 - y)]


def _comm_call(body, name, out_shape, n_in, scratch):
    return pl.pallas_call(body, name=name, out_shape=out_shape, in_specs=[HBM] * n_in,
                          out_specs=jax.tree.map(lambda _: HBM, out_shape), scratch_shapes=scratch)


def _half_rows(rows, c, tile):
    hh = rows // 2
    assert rows % 2 == 0 and hh % tile == 0, (rows, tile)
    return pl.ds(pl.multiple_of(c * hh, tile), hh)


def _gather_copies(w_refs, out_refs, sems):
    send_sems, recv_sems = sems
    x, y, c, chips = _place()
    s_me = 2 * x + y
    sends, recvs = [], []
    for k, (w_ref, out_ref) in enumerate(zip(w_refs, out_refs)):
        half = _half_rows(w_ref.shape[0], c, 16)
        for j, (cx, cy) in enumerate(chips):
            sem = dict(send_sem=send_sems.at[3 * k + j], recv_sem=recv_sems.at[3 * k + j], device_id=(cx, cy, c), device_id_type=MESH)
            sends.append(pltpu.make_async_remote_copy(src_ref=w_ref.at[half], dst_ref=out_ref.at[s_me, half], **sem))
            recvs.append(pltpu.make_async_remote_copy(src_ref=w_ref.at[half], dst_ref=out_ref.at[2 * cx + cy, half], **sem))
    return sends, recvs


def _gather_sems(n):
    return [pltpu.SemaphoreType.DMA((3 * n,)), pltpu.SemaphoreType.DMA((3 * n,))]


def _gather_start(w_refs, out_refs, sems):
    for cp in _gather_copies(w_refs, out_refs, sems)[0]:
        cp.start()


def _gather_wait(w_refs, out_refs, sems):
    sends, recvs = _gather_copies(w_refs, out_refs, sems)
    for cp in recvs:
        cp.wait_recv()
    for cp in sends:
        cp.wait_send()


def _gather_chips(shards):
    n = len(shards)

    def body(*refs):
        _gather_start(refs[:n], refs[n:2 * n], refs[2 * n:])
        _gather_wait(refs[:n], refs[n:2 * n], refs[2 * n:])

    return _comm_call(body, "gather_chips", [_sds((4,) + w.shape, w.dtype) for w in shards], n, _gather_sems(n))(*shards)


def _gather_side(shards):
    return _Side(shards, [_sds((4,) + w.shape, w.dtype) for w in shards], _gather_sems(len(shards)), _gather_start, _gather_wait)


def _forward_halves(outs, name):
    n = len(outs)

    def body(*refs):
        out_refs = refs[n:2 * n]
        send_sems, recv_sems = refs[2 * n:]
        x, y, c, chips = _place()
        cps = []
        for k in range(n):
            rows = out_refs[k].shape[1]
            half, other = _half_rows(rows, c, 16), _half_rows(rows, 1 - c, 16)
            for j, (cx, cy) in enumerate(chips):
                sem = dict(send_sem=send_sems.at[3 * k + j], recv_sem=recv_sems.at[3 * k + j], device_id=(x, y, 1 - c), device_id_type=MESH)
                landed = out_refs[k].at[2 * cx + cy, half]
                cps.append(pltpu.make_async_remote_copy(src_ref=landed, dst_ref=landed, **sem))
                cps[-1].start()
        for k in range(n):
            rows = out_refs[k].shape[1]
            half, other = _half_rows(rows, c, 16), _half_rows(rows, 1 - c, 16)
            for j, (cx, cy) in enumerate(chips):
                sem = dict(send_sem=send_sems.at[3 * k + j], recv_sem=recv_sems.at[3 * k + j], device_id=(x, y, 1 - c), device_id_type=MESH)
                pltpu.make_async_remote_copy(src_ref=out_refs[k].at[2 * cx + cy, half], dst_ref=out_refs[k].at[2 * cx + cy, other], **sem).wait_recv()
        for cp in cps:
            cp.wait_send()

    shapes = [_sds(o.shape, o.dtype) for o in outs]
    return pl.pallas_call(body, name=name, out_shape=shapes, in_specs=[HBM] * n, out_specs=[HBM] * n,
                          input_output_aliases={k: k for k in range(n)},
                          scratch_shapes=[pltpu.SemaphoreType.DMA((3 * n,)), pltpu.SemaphoreType.DMA((3 * n,))])(*outs)


def _swap_halves(gs, name):
    n = len(gs)

    def body(*refs):
        g_refs, out_refs = refs[:n], refs[n:2 * n]
        send_sems, recv_sems = refs[2 * n:]
        x, y, c, _ = _place()
        cps = []
        for k in range(n):
            other = _half_rows(g_refs[k].shape[1], 1 - c, 8)
            cps.append(pltpu.make_async_remote_copy(src_ref=g_refs[k].at[:, other, :], dst_ref=out_refs[k], send_sem=send_sems.at[k],
                                                    recv_sem=recv_sems.at[k], device_id=(x, y, 1 - c), device_id_type=MESH))
            cps[-1].start()
        for cp in cps:
            cp.wait()

    return _comm_call(body, name, [_sds((4, g.shape[1] // 2, g.shape[2]), g.dtype) for g in gs], n,
                      [pltpu.SemaphoreType.DMA((n,)), pltpu.SemaphoreType.DMA((n,))])(*gs)


def _row_tile(rows, row_bytes, budget=3 << 20):
    if rows * row_bytes <= budget:
        return rows
    return max(t for t in range(16, rows, 16) if rows % t == 0 and t * row_bytes <= budget)


def _add_half(g, got, c, name):
    _, rows, cols = g.shape
    hh = rows // 2
    tr = _row_tile(hh, cols * 4)
    nb = hh // tr

    def body(c_ref, a_ref, b_ref, o_ref):
        o_ref[...] = (a_ref[...] + b_ref[...]).astype(BF16)

    gs = pltpu.PrefetchScalarGridSpec(
        num_scalar_prefetch=1, grid=(4, nb),
        in_specs=[pl.BlockSpec((1, tr, cols), lambda s, i, c_ref: (s, c_ref[0] * nb + i, 0)),
                  pl.BlockSpec((1, tr, cols), lambda s, i, c_ref: (s, i, 0))],
        out_specs=pl.BlockSpec((1, tr, cols), lambda s, i, c_ref: (s, i, 0)))
    return pl.pallas_call(body, name=name, grid_spec=gs, out_shape=_sds((4, hh, cols), BF16),
                          compiler_params=pltpu.CompilerParams(dimension_semantics=("parallel", "parallel"),
                                                               vmem_limit_bytes=VMEM_LIMIT))(c, g, got)


def _scatter_copies(p_refs, out_refs, sems):
    send_sems, recv_sems = sems
    x, y, c, chips = _place()
    s_me = 2 * x + y
    cps = []
    for k, (p_ref, out_ref) in enumerate(zip(p_refs, out_refs)):
        for j, (cx, cy) in enumerate(chips):
            cps.append(pltpu.make_async_remote_copy(src_ref=p_ref.at[2 * cx + cy], dst_ref=out_ref.at[s_me],
                                                    send_sem=send_sems.at[3 * k + j], recv_sem=recv_sems.at[3 * k + j],
                                                    device_id=(cx, cy, c), device_id_type=MESH))
    return cps


def _scatter_start(p_refs, out_refs, sems):
    for cp in _scatter_copies(p_refs, out_refs, sems):
        cp.start()


def _scatter_wait(p_refs, out_refs, sems):
    for cp in _scatter_copies(p_refs, out_refs, sems):
        cp.wait()


def _scatter_chips(ps):
    n = len(ps)

    def body(*refs):
        _scatter_start(refs[:n], refs[n:2 * n], refs[2 * n:])
        _scatter_wait(refs[:n], refs[n:2 * n], refs[2 * n:])

    return _comm_call(body, "scatter_chips", [_sds(p_.shape, p_.dtype) for p_ in ps], n, _gather_sems(n))(*ps)


def _scatter_side(ps):
    return _Side(ps, [_sds(p_.shape, p_.dtype) for p_ in ps], _gather_sems(len(ps)), _scatter_start, _scatter_wait)


def _sum_slabs(b, name):
    n, h, wd = b.shape
    tr = _row_tile(h, n * wd * 4, 6 << 20)

    def body(b_ref, o_ref):
        acc = b_ref[0]
        for s in range(1, n):
            acc = acc + b_ref[s]
        o_ref[...] = acc

    return _call(body, name=name, grid=(h // tr,), in_specs=[pl.BlockSpec((n, tr, wd), lambda i: (0, i, 0))],
                 out_specs=pl.BlockSpec((tr, wd), lambda i: (i, 0)), out_shape=_sds((h, wd), F32), sem=("parallel",))(b)


def _sum_chips(arrived, own, name):
    n, h, wd = arrived.shape
    tr = _row_tile(h, n * wd * 2, 6 << 20)
    nb = h // tr
    my_chip = lambda: 2 * lax.axis_index("x") + lax.axis_index("y")

    def body(b_ref, p_ref, o_ref):
        acc = None
        for s in range(n):
            term = jnp.where(my_chip() == s, p_ref[0], b_ref[s]).astype(F32)
            acc = term if acc is None else acc + term
        o_ref[...] = acc

    return _call(body, name=name, grid=(nb,),
                 in_specs=[pl.BlockSpec((n, tr, wd), lambda i: (0, i, 0)), pl.BlockSpec((1, tr, wd), lambda i: (my_chip(), i, 0))],
                 out_specs=pl.BlockSpec((tr, wd), lambda i: (lax.axis_index("c") * nb + i, 0)),
                 out_shape=_sds((2 * h, wd), F32), sem=("parallel",))(arrived, own)


def _share_halves(gs):
    n = len(gs)

    def body(*refs):
        out_refs = refs[n:2 * n]
        send_sems, recv_sems = refs[2 * n:]
        x, y, c, _ = _place()
        cps = []
        for k in range(n):
            half, other = _half_rows(out_refs[k].shape[0], c, 8), _half_rows(out_refs[k].shape[0], 1 - c, 8)
            sem = dict(send_sem=send_sems.at[k], recv_sem=recv_sems.at[k], device_id=(x, y, 1 - c), device_id_type=MESH)
            cps.append((pltpu.make_async_remote_copy(src_ref=out_refs[k].at[half], dst_ref=out_refs[k].at[half], **sem),
                        pltpu.make_async_remote_copy(src_ref=out_refs[k].at[half], dst_ref=out_refs[k].at[other], **sem)))
            cps[-1][0].start()
        for send, recv in cps:
            recv.wait_recv()
            send.wait_send()

    return pl.pallas_call(body, name="share_halves", out_shape=[_sds(g.shape, g.dtype) for g in gs], in_specs=[HBM] * n,
                          out_specs=[HBM] * n, input_output_aliases={k: k for k in range(n)},
                          scratch_shapes=[pltpu.SemaphoreType.DMA((n,)), pltpu.SemaphoreType.DMA((n,))])(*gs)


def _gather_all(v, name):
    def body(v_ref, out_ref, send_sems, recv_sems, local_sem):
        x, y, c = lax.axis_index("x"), lax.axis_index("y"), lax.axis_index("c")
        me = 4 * x + 2 * y + c
        flip = lambda t, d: 1 - t if d else t
        mine = pltpu.make_async_copy(v_ref, out_ref.at[me], local_sem)
        mine.start()
        cps = []
        for k in range(1, 8):
            to = (flip(x, k & 4), flip(y, k & 2), flip(c, k & 1))
            cps.append(pltpu.make_async_remote_copy(src_ref=v_ref, dst_ref=out_ref.at[me], send_sem=send_sems.at[k - 1],
                                                    recv_sem=recv_sems.at[k - 1], device_id=to, device_id_type=MESH))
        for cp in cps:
            cp.start()
        for cp in cps:
            cp.wait()
        mine.wait()

    return _comm_call(body, name, _sds((8,) + v.shape, v.dtype), 1,
                      [pltpu.SemaphoreType.DMA((7,)), pltpu.SemaphoreType.DMA((7,)), pltpu.SemaphoreType.DMA])(v)


BIG = (("w_in", 1), ("w_branch_a", 1), ("w_branch_b", 1), ("w_out", 0), ("w_ffn_in", 1), ("w_ffn_out", 0))
SMALL = ("meta_tokens", "lb_logits", "mix_norm_w", "hg_norm_w", "gd_conv_w", "gd_a_log", "gd_dt_bias", "gd_norm_w",
         "ffn_norm_w", "final_norm_w")


def _pack_lanes(parts):
    rows = []
    for p in parts:
        f = p.reshape(-1).astype(F32)
        n = -(-f.shape[0] // DH) * DH
        rows.append(jnp.pad(f, (0, n - f.shape[0])).reshape(-1, DH))
    buf = jnp.concatenate(rows, axis=0)
    return jnp.pad(buf, ((0, -buf.shape[0] % 8), (0, 0)))


def _unpack_lanes(buf, shapes):
    out, off = [], 0
    for shp in shapes:
        n = math.prod(shp)
        r = -(-n // DH)
        out.append(buf[off:off + r].reshape(-1)[:n].reshape(shp))
        off += r
    return out


def kernel(x, meta_tokens, lb_logits, mix_norm_w, w_in, hg_norm_w, gd_conv_w, gd_a_log, gd_dt_bias, gd_norm_w, w_branch_a, w_branch_b, w_out, ffn_norm_w, w_ffn_in, w_ffn_out, final_norm_w, loss_target, m_meta_tokens, m_lb_logits, m_mix_norm_w, m_w_in, m_hg_norm_w, m_gd_conv_w, m_gd_a_log, m_gd_dt_bias, m_gd_norm_w, m_w_branch_a, m_w_branch_b, m_w_out, m_ffn_norm_w, m_w_ffn_in, m_w_ffn_out, m_final_norm_w, v_meta_tokens, v_lb_logits, v_mix_norm_w, v_w_in, v_hg_norm_w, v_gd_conv_w, v_gd_a_log, v_gd_dt_bias, v_gd_norm_w, v_w_branch_a, v_w_branch_b, v_w_out, v_ffn_norm_w, v_w_ffn_in, v_w_ffn_out, v_final_norm_w):
    args = dict(locals())
    big = [n for n, _ in BIG]
    w = {n: args[n] for n in SMALL + tuple(big)}
    m = {n: args["m_" + n] for n in w}
    v = {n: args["v_" + n] for n in w}
    xi, yi, ci = lax.axis_index("x"), lax.axis_index("y"), lax.axis_index("c")
    shard = 2 * xi + yi
    big_local = {n: w[n][0] for n in big}

    meta_cols, conv_cols = meta_tokens.shape[1], gd_conv_w.shape[-1]
    sm_all = _gather_all(_pack_lanes([meta_tokens, gd_conv_w[0]]), "gather_meta")
    sm_parts = [_unpack_lanes(sm_all[2 * s], [meta_tokens.shape, gd_conv_w[0].shape]) for s in range(4)]
    meta_full = jnp.concatenate([p[0] for p in sm_parts], axis=1)
    conv_full = jnp.concatenate([p[1] for p in sm_parts], axis=1)
    cvec = ci.reshape(1).astype(jnp.int32)
    late = [n for n in big if n != "w_in"]
    rows_full = lambda t: t.reshape(t.shape[0] * t.shape[1], t.shape[2])

    def pair_sums(names, gs):
        return [_add_half(gk, got, cvec, "add_half_" + n) for n, gk, got in zip(names, gs, _swap_halves(gs, "swap_" + names[0]))]

    def with_own(slabs, n):
        return lax.dynamic_update_index_in_dim(slabs, big_local[n].astype(BF16), shard, 0)

    class MeshComm:
        def proj_side(self):
            return _gather_side([big_local[n].astype(BF16) for n in late])

        def late_weights(self, landed):
            wl = {n: with_own(t, n) for n, t in zip(late, _forward_halves(landed, "forward_late"))}
            return (wl["w_branch_a"], wl["w_branch_b"], rows_full(wl["w_out"]), wl["w_ffn_in"], rows_full(wl["w_ffn_out"]))

        def early_grads_side(self, grads):
            self.early = list(grads)
            self.early_parts = pair_sums(self.early, [grads[n] for n in self.early])
            return _scatter_side(self.early_parts)

        def early_grads_done(self, arrived):
            self.early_arrived = arrived

        def last_grad_side(self, dw_in):
            self.last_parts = pair_sums(["w_in"], [dw_in])
            return _scatter_side(self.last_parts)

        def last_grad_done(self, arrived):
            self.last_arrived = arrived

    comm = MeshComm()
    w_in_slabs = with_own(_forward_halves(_gather_chips([big_local["w_in"].astype(BF16)]), "forward_w_in")[0], "w_in")
    d_model = w_in_slabs.shape[1]
    w_in_full = w_in_slabs.transpose(1, 0, 2).reshape(d_model, -1)
    loss, dx, g = _local_step(x[0], loss_target[0], meta_full, lb_logits, mix_norm_w, w_in_full, hg_norm_w, conv_full,
                              gd_a_log, gd_dt_bias, gd_norm_w, ffn_norm_w, final_norm_w, comm)
    loss = lax.psum(loss[0, 0], ("x", "y", "c"))

    parts = dict(zip(comm.early + ["w_in"], comm.early_parts + comm.last_parts))
    arrived = dict(zip(comm.early + ["w_in"], comm.early_arrived + comm.last_arrived))
    g_big = dict(zip(big, _share_halves([_sum_chips(arrived[n], parts[n], "sum_chips_" + n) for n in big])))

    small_shapes = [g[n].shape for n in SMALL]
    g_all = _gather_all(_pack_lanes([g[n] for n in SMALL]), "gather_small")
    g_small = dict(zip(SMALL, _unpack_lanes(_sum_slabs(g_all, "sum_small"), small_shapes)))
    g_small["meta_tokens"] = lax.dynamic_slice_in_dim(g_small["meta_tokens"], shard * meta_cols, meta_cols, axis=1)
    g_small["gd_conv_w"] = lax.dynamic_slice_in_dim(g_small["gd_conv_w"], shard * conv_cols, conv_cols, axis=1)

    grad, delta, new_m, new_v = {}, {}, {}, {}
    for n in big:
        grad[n] = g_big[n].reshape(w[n].shape)
        d_, m_, v_ = _adamw(g_big[n], big_local[n], m[n][0], v[n][0], "adamw_" + n)
        delta[n], new_m[n], new_v[n] = d_.reshape(w[n].shape), m_.reshape(w[n].shape), v_.reshape(w[n].shape)
    local_shapes = [w[n].shape for n in SMALL]
    d_, m_, v_ = _adamw(_pack_lanes([g_small[n] for n in SMALL]), _pack_lanes([w[n] for n in SMALL]),
                        _pack_lanes([m[n] for n in SMALL]), _pack_lanes([v[n] for n in SMALL]), "adamw_small")
    for n, gs_, dd, mm, vv in zip(SMALL, [g_small[n] for n in SMALL], _unpack_lanes(d_, local_shapes), _unpack_lanes(m_, local_shapes),
                                  _unpack_lanes(v_, local_shapes)):
        grad[n], delta[n], new_m[n], new_v[n] = gs_.reshape(w[n].shape), dd, mm, vv

    order = ["meta_tokens", "lb_logits", "mix_norm_w", "w_in", "hg_norm_w", "gd_conv_w", "gd_a_log", "gd_dt_bias", "gd_norm_w",
             "w_branch_a", "w_branch_b", "w_out", "ffn_norm_w", "w_ffn_in", "w_ffn_out", "final_norm_w"]
    return (loss, dx[None], *[grad[n] for n in order], *[delta[n] for n in order], *[new_m[n] for n in order],
            *[new_v[n] for n in order])
```

```python
import functools
import math

import jax
import jax.numpy as jnp
from jax import lax
from jax.experimental import pallas as pl
from jax.experimental.pallas import tpu as pltpu

F32, BF16 = jnp.float32, jnp.bfloat16
EPS = 1e-6
D_MODEL = 2048
N_META = 16
FRONT = 256
CH = 64
SUB = 16
SUBH = 16
DH = 128
NH = 8
HW = NH * DH
CONV_K = 4
RT = 256
VMEM_LIMIT = 56 * 1024 * 1024
ADAM_LR, ADAM_B1, ADAM_B2, ADAM_EPS, ADAM_WD, ADAM_STEP = 0.001, 0.9, 0.999, 1e-08, 0.01, 10

NN = (((1,), (0,)), ((), ()))
NT = (((1,), (1,)), ((), ()))
TN = (((0,), (0,)), ((), ()))


def _dot(a, b, dn=NN):
    return lax.dot_general(a.astype(BF16), b.astype(BF16), dn, preferred_element_type=F32)


def _dot3(a, b, dn=NN):
    def split(t):
        hi = t.astype(BF16)
        return hi, (t - hi.astype(F32)).astype(BF16)

    (ah, al), (bh, bl) = split(a), split(b)
    dot = lambda u, v: lax.dot_general(u, v, dn, preferred_element_type=F32)
    return dot(ah, bh) + (dot(ah, bl) + dot(al, bh))


class _Side:
    def __init__(self, inputs, out_shapes, scratch, start, wait):
        self.inputs, self.out_shapes, self.scratch, self.start, self.wait = inputs, out_shapes, scratch, start, wait


def _call(body, *, name, grid, in_specs, out_specs, out_shape, scratch=(), sem=None, side=None):
    params = pltpu.CompilerParams(dimension_semantics=sem, vmem_limit_bytes=VMEM_LIMIT)
    if side is None:
        return pl.pallas_call(body, name=name, grid=grid, in_specs=in_specs, out_specs=out_specs, out_shape=out_shape,
                              scratch_shapes=list(scratch), compiler_params=params)
    single = not isinstance(out_specs, (list, tuple))
    out_specs, out_shape = ([out_specs], [out_shape]) if single else (list(out_specs), list(out_shape))
    ni, no, ns = len(in_specs), len(out_specs), len(scratch)
    nsi, nso = len(side.inputs), len(side.out_shapes)
    hbm = pl.BlockSpec(memory_space=pltpu.HBM)

    def wrapped(*refs):
        main_in, side_in = refs[:ni], refs[ni:ni + nsi]
        main_out, side_out = refs[ni + nsi:ni + nsi + no], refs[ni + nsi + no:ni + nsi + no + nso]
        main_scr, side_scr = refs[ni + nsi + no + nso:ni + nsi + no + nso + ns], refs[ni + nsi + no + nso + ns:]
        pids = [pl.program_id(d) for d in range(len(grid))]
        first = functools.reduce(lambda a, b: a & b, [p == 0 for p in pids])
        last = functools.reduce(lambda a, b: a & b, [p == g - 1 for p, g in zip(pids, grid)])

        @pl.when(first)
        def _():
            side.start(side_in, side_out, side_scr)

        body(*main_in, *main_out, *main_scr)

        @pl.when(last)
        def _():
            side.wait(side_in, side_out, side_scr)

    call = pl.pallas_call(wrapped, name=name, grid=grid, in_specs=list(in_specs) + [hbm] * nsi,
                          out_specs=out_specs + [hbm] * nso, out_shape=out_shape + list(side.out_shapes),
                          scratch_shapes=list(scratch) + list(side.scratch), compiler_params=params)

    def run(*args):
        outs = call(*args, *side.inputs)
        main = outs[0] if single else list(outs[:no])
        return main, list(outs[no:])

    return run


def _divmod(j, per):
    if per == 1:
        return j, 0
    return lax.div(j, jnp.int32(per)), lax.rem(j, jnp.int32(per))


def _sds(shape, dtype):
    return jax.ShapeDtypeStruct(tuple(shape), dtype)


def _sigmoid(x):
    return 0.5 * jnp.tanh(0.5 * x) + 0.5


def _silu(x):
    return x * _sigmoid(x)


def _dsilu(x):
    s = _sigmoid(x)
    return s * (1.0 + x * (1.0 - s))


def _tri(n, kind):
    r = lax.broadcasted_iota(jnp.int32, (n, n), 0)
    c = lax.broadcasted_iota(jnp.int32, (n, n), 1)
    return {"incl": r >= c, "strict": r > c, "upper": c >= r}[kind]


def _mm(a, b, mode, out_dtype, tm, tn, tk, name, add=None, n_outer=False, out_shards=0, side=None):
    sharded_a = a.ndim == 3
    if sharded_a:
        n_a = a.shape[2]
        a_shape = (a.shape[1], a.shape[0] * n_a)
    else:
        a_shape = a.shape
    sharded_b = b.ndim == 3
    if sharded_b:
        S, R, n = b.shape
        b_rows, b_cols = R, S * n
    else:
        b_rows, b_cols = b.shape
    if mode == "nn":
        (M, K), N, dn = a_shape, b_cols, NN
    elif mode == "nt":
        (M, K), N, dn = a_shape, b_rows, NT
    else:
        (K, M), N, dn = a_shape, b_cols, TN
    tm, tn, tk = min(tm, M), min(tn, N), min(tk, K)
    if sharded_a:
        tk = min(tk, n_a)
    if sharded_b:
        tn, tk = (min(tn, n), tk) if mode != "nt" else (tn, min(tk, n))
    if out_shards:
        tn = min(tn, N // out_shards)
    assert M % tm == 0 and N % tn == 0 and K % tk == 0, (name, M, N, K, tm, tn, tk)
    nk = K // tk
    a_blk, a_idx = ((tm, tk), lambda i, j, k: (i, k)) if mode != "tn" else ((tk, tm), lambda i, j, k: (k, i))
    if sharded_a:
        per_a = n_a // tk
        assert mode != "tn" and n_a % tk == 0
        a_blk, a_idx = (None, tm, tk), lambda i, j, k: (_divmod(k, per_a)[0], i, _divmod(k, per_a)[1])
    if not sharded_b:
        b_blk, b_idx = ((tk, tn), lambda i, j, k: (k, j)) if mode != "nt" else ((tn, tk), lambda i, j, k: (j, k))
    elif mode != "nt":
        per = n // tn
        assert n % tn == 0
        b_blk, b_idx = (None, tk, tn), lambda i, j, k: (_divmod(j, per)[0], k, _divmod(j, per)[1])
    else:
        per = n // tk
        assert n % tk == 0
        b_blk, b_idx = (None, tn, tk), lambda i, j, k: (_divmod(k, per)[0], j, _divmod(k, per)[1])
    if out_shards:
        per_o = N // out_shards // tn
        assert (N // out_shards) % tn == 0
        o_blk, o_idx = (None, tm, tn), lambda i, j, k: (_divmod(j, per_o)[0], i, _divmod(j, per_o)[1])
        o_shape = (out_shards, M, N // out_shards)
    else:
        o_blk, o_idx, o_shape = (tm, tn), (lambda i, j, k: (i, j)), (M, N)
    c_idx = lambda i, j, k: (i, j)
    if n_outer:
        sw = lambda f: (lambda j, i, k: f(i, j, k))
        a_idx, b_idx, o_idx, c_idx = sw(a_idx), sw(b_idx), sw(o_idx), sw(c_idx)
        grid = (N // tn, M // tm, nk)
    else:
        grid = (M // tm, N // tn, nk)
    has_add = add is not None

    def body(*refs):
        if has_add:
            a_ref, b_ref, c_ref, o_ref, acc_ref = refs
        else:
            a_ref, b_ref, o_ref, acc_ref = refs
            c_ref = None
        part = lax.dot_general(a_ref[...].astype(BF16), b_ref[...].astype(BF16), dn, preferred_element_type=F32)

        def fin(val):
            if has_add:
                val = val + c_ref[...]
            o_ref[...] = val.astype(out_dtype)

        if nk == 1:
            fin(part)
        else:
            k = pl.program_id(2)

            @pl.when(k == 0)
            def _():
                acc_ref[...] = part

            @pl.when(k > 0)
            def _():
                acc_ref[...] += part

            @pl.when(k == nk - 1)
            def _():
                fin(acc_ref[...])

    in_specs = [pl.BlockSpec(a_blk, a_idx), pl.BlockSpec(b_blk, b_idx)]
    args = [a, b]
    if has_add:
        in_specs.append(pl.BlockSpec((tm, tn), c_idx))
        args.append(add)
    acc_shape = (tm, tn) if nk > 1 else (8, 128)
    return _call(body, name=name, grid=grid, in_specs=in_specs, out_specs=pl.BlockSpec(o_blk, o_idx),
                 out_shape=_sds(o_shape, out_dtype), scratch=[pltpu.VMEM(acc_shape, F32)],
                 sem=("arbitrary",) * 3 if side is not None else ("parallel", "parallel", "arbitrary"), side=side)(*args)


def _rms1_fwd(x, meta, w):
    seq, d = x.shape
    nt = (FRONT + seq) // RT

    def body(x_ref, m_ref, w_ref, o_ref):
        i = pl.program_id(0)

        def norm(v):
            r = lax.rsqrt(jnp.mean(v * v, axis=-1, keepdims=True) + EPS)
            return (v * r * w_ref[...]).astype(BF16)

        @pl.when(i == 0)
        def _():
            o_ref[0:RT - N_META, :] = jnp.zeros((RT - N_META, d), BF16)
            o_ref[RT - N_META:RT, :] = norm(m_ref[...])

        @pl.when(i > 0)
        def _():
            o_ref[...] = norm(x_ref[...])

    return _call(body, name="rms1_fwd", grid=(nt,),
                 in_specs=[pl.BlockSpec((RT, d), lambda i: (jnp.maximum(i - 1, 0), 0)),
                           pl.BlockSpec((N_META, d), lambda i: (0, 0)),
                           pl.BlockSpec((1, d), lambda i: (0, 0))],
                 out_specs=pl.BlockSpec((RT, d), lambda i: (i, 0)),
                 out_shape=_sds((FRONT + seq, d), BF16), sem=("parallel",))(x, meta, w)


def _rms1_bwd(x, meta, w, dxn, dh1):
    seq, d = x.shape
    nt = (FRONT + seq) // RT

    def body(x_ref, m_ref, w_ref, g_ref, r_ref, dx_ref, dm_ref, dw_ref):
        i = pl.program_id(0)

        def bwd(v, g):
            r = lax.rsqrt(jnp.mean(v * v, axis=-1, keepdims=True) + EPS)
            vh = v * r
            gh = g * w_ref[...]
            return r * (gh - vh * jnp.mean(gh * vh, axis=-1, keepdims=True)), jnp.sum(g * vh, axis=0, keepdims=True)

        @pl.when(i == 0)
        def _():
            dm, dw = bwd(m_ref[...], g_ref[RT - N_META:RT, :])
            dm_ref[...] = dm
            dw_ref[...] = dw

        @pl.when(i > 0)
        def _():
            dx, dw = bwd(x_ref[...], g_ref[...])
            dx_ref[...] = dx + r_ref[...]
            dw_ref[...] += dw

    xs = pl.BlockSpec((RT, d), lambda i: (jnp.maximum(i - 1, 0), 0))
    return _call(body, name="rms1_bwd", grid=(nt,),
                 in_specs=[xs, pl.BlockSpec((N_META, d), lambda i: (0, 0)), pl.BlockSpec((1, d), lambda i: (0, 0)),
                           pl.BlockSpec((RT, d), lambda i: (i, 0)), xs],
                 out_specs=[xs, pl.BlockSpec((N_META, d), lambda i: (0, 0)), pl.BlockSpec((1, d), lambda i: (0, 0))],
                 out_shape=[_sds((seq, d), F32), _sds((N_META, d), F32), _sds((1, d), F32)],
                 sem=("arbitrary",))(x, meta, w, dxn, dh1)


def _merge_fwd(proj, za, zb):
    seq, d = za.shape
    off = FRONT // RT
    ca, cb = 8 * HW // d, 8 * HW // d + 1

    def body(ga_ref, gb_ref, za_ref, zb_ref, o_ref):
        o_ref[...] = (_sigmoid(ga_ref[...]) * za_ref[...] + _sigmoid(gb_ref[...]) * zb_ref[...]).astype(BF16)

    zs = pl.BlockSpec((RT, d), lambda i: (i, 0))
    return _call(body, name="merge_fwd", grid=(seq // RT,),
                 in_specs=[pl.BlockSpec((RT, d), lambda i: (i + off, ca)), pl.BlockSpec((RT, d), lambda i: (i + off, cb)), zs, zs],
                 out_specs=zs, out_shape=_sds((seq, d), BF16), sem=("parallel",))(proj, proj, za, zb)


def _merge_bwd(proj, za, zb, dmerged):
    seq, d = za.shape
    off = FRONT // RT
    ca, cb = 8 * HW // d, 8 * HW // d + 1
    nt = (FRONT + seq) // RT

    def body(ga_ref, gb_ref, za_ref, zb_ref, dm_ref, dza_ref, dzb_ref, dg_ref):
        i = pl.program_id(0)

        @pl.when(i < off)
        def _():
            dg_ref[...] = jnp.zeros((RT, 2 * d), BF16)

        @pl.when(i >= off)
        def _():
            sa, sb, dm = _sigmoid(ga_ref[...]), _sigmoid(gb_ref[...]), dm_ref[...]
            dza_ref[...] = (dm * sa).astype(BF16)
            dzb_ref[...] = (dm * sb).astype(BF16)
            dg_ref[:, 0:d] = (dm * za_ref[...] * sa * (1.0 - sa)).astype(BF16)
            dg_ref[:, d:2 * d] = (dm * zb_ref[...] * sb * (1.0 - sb)).astype(BF16)

    rs = pl.BlockSpec((RT, d), lambda i: (jnp.maximum(i - off, 0), 0))
    return _call(body, name="merge_bwd", grid=(nt,),
                 in_specs=[pl.BlockSpec((RT, d), lambda i: (i, ca)), pl.BlockSpec((RT, d), lambda i: (i, cb)), rs, rs, rs],
                 out_specs=[rs, rs, pl.BlockSpec((RT, 2 * d), lambda i: (i, 0))],
                 out_shape=[_sds((seq, d), BF16), _sds((seq, d), BF16), _sds((FRONT + seq, 2 * d), BF16)],
                 sem=("arbitrary",))(proj, proj, za, zb, dmerged)


def _resid_norm_fwd(x, mix, w):
    seq, d = x.shape

    def body(x_ref, m_ref, w_ref, h_ref, n_ref):
        h = x_ref[...] + m_ref[...]
        h_ref[...] = h
        r = lax.rsqrt(jnp.mean(h * h, axis=-1, keepdims=True) + EPS)
        n_ref[...] = (h * r * w_ref[...]).astype(BF16)

    rs = pl.BlockSpec((RT, d), lambda i: (i, 0))
    return _call(body, name="resid_norm_fwd", grid=(seq // RT,),
                 in_specs=[rs, rs, pl.BlockSpec((1, d), lambda i: (0, 0))], out_specs=[rs, rs],
                 out_shape=[_sds((seq, d), F32), _sds((seq, d), BF16)], sem=("parallel",))(x, mix, w)


def _resid_norm_bwd(h1, w, dn2, dh2):
    seq, d = h1.shape

    def body(h_ref, w_ref, g_ref, r_ref, o_ref, ob_ref, dw_ref):
        i = pl.program_id(0)
        h, g = h_ref[...], g_ref[...]
        r = lax.rsqrt(jnp.mean(h * h, axis=-1, keepdims=True) + EPS)
        hh = h * r
        gh = g * w_ref[...]
        dh = r_ref[...] + r * (gh - hh * jnp.mean(gh * hh, axis=-1, keepdims=True))
        o_ref[...] = dh
        ob_ref[...] = dh.astype(BF16)
        dw = jnp.sum(g * hh, axis=0, keepdims=True)

        @pl.when(i == 0)
        def _():
            dw_ref[...] = dw

        @pl.when(i > 0)
        def _():
            dw_ref[...] += dw

    rs = pl.BlockSpec((RT, d), lambda i: (i, 0))
    ws = pl.BlockSpec((1, d), lambda i: (0, 0))
    return _call(body, name="resid_norm_bwd", grid=(seq // RT,), in_specs=[rs, ws, rs, rs], out_specs=[rs, rs, ws],
                 out_shape=[_sds((seq, d), F32), _sds((seq, d), BF16), _sds((1, d), F32)], sem=("arbitrary",))(h1, w, dn2, dh2)


def _swiglu_tiles(seq, ff):
    return min(512, seq), (1408 if ff % 1408 == 0 else 512)


def _swiglu_fwd(gu):
    seq, f2 = gu.shape
    ff = f2 // 2
    rt, tc = _swiglu_tiles(seq, ff)
    nb = ff // tc

    def body(g_ref, u_ref, o_ref):
        o_ref[...] = (_silu(g_ref[...].astype(F32)) * u_ref[...].astype(F32)).astype(BF16)

    return _call(body, name="swiglu_fwd", grid=(seq // rt, nb),
                 in_specs=[pl.BlockSpec((rt, tc), lambda i, j: (i, j)), pl.BlockSpec((rt, tc), lambda i, j: (i, j + nb))],
                 out_specs=pl.BlockSpec((rt, tc), lambda i, j: (i, j)), out_shape=_sds((seq, ff), BF16),
                 sem=("parallel", "parallel"))(gu, gu)


def _swiglu_bwd(gu, dact):
    seq, f2 = gu.shape
    ff = f2 // 2
    rt, tc = _swiglu_tiles(seq, ff)
    nb = ff // tc

    def body(g_ref, u_ref, d_ref, o_ref):
        g, d = g_ref[...].astype(F32), d_ref[...].astype(F32)
        o_ref[0] = (d * u_ref[...].astype(F32) * _dsilu(g)).astype(BF16)
        o_ref[1] = (d * _silu(g)).astype(BF16)

    bs = pl.BlockSpec((rt, tc), lambda i, j: (i, j))
    return _call(body, name="swiglu_bwd", grid=(seq // rt, nb),
                 in_specs=[bs, pl.BlockSpec((rt, tc), lambda i, j: (i, j + nb)), bs],
                 out_specs=pl.BlockSpec((2, rt, tc), lambda i, j: (0, i, j)),
                 out_shape=_sds((2, seq, ff), BF16), sem=("parallel", "parallel"))(gu, gu, dact)


def _loss_head(h1, f, w, tgt):
    seq, d = h1.shape
    nt = seq // RT

    def body(h_ref, f_ref, w_ref, t_ref, l_ref, dh_ref, dhb_ref, dw_ref):
        i = pl.program_id(0)
        h = h_ref[...] + f_ref[...]
        r = lax.rsqrt(jnp.mean(h * h, axis=-1, keepdims=True) + EPS)
        hh = h * r
        err = hh * w_ref[...] - t_ref[...]
        l_ref[...] = jnp.full((8, 128), 0.5 * jnp.sum(jnp.mean(err * err, axis=-1, keepdims=True)), F32)
        dy = err * (1.0 / d)
        gh = dy * w_ref[...]
        dh = r * (gh - hh * jnp.mean(gh * hh, axis=-1, keepdims=True))
        dh_ref[...] = dh
        dhb_ref[...] = dh.astype(BF16)
        dw = jnp.sum(dy * hh, axis=0, keepdims=True)

        @pl.when(i == 0)
        def _():
            dw_ref[...] = dw

        @pl.when(i > 0)
        def _():
            dw_ref[...] += dw

    rs = pl.BlockSpec((RT, d), lambda i: (i, 0))
    ws = pl.BlockSpec((1, d), lambda i: (0, 0))
    return _call(body, name="loss_head", grid=(nt,), in_specs=[rs, rs, ws, rs],
                 out_specs=[pl.BlockSpec((8, 128), lambda i: (i, 0)), rs, rs, ws],
                 out_shape=[_sds((nt * 8, 128), F32), _sds((seq, d), F32), _sds((seq, d), BF16), _sds((1, d), F32)],
                 sem=("arbitrary",))(h1, f, w, tgt)


def _sum_tiles(lt):
    n = lt.shape[0]

    def body(l_ref, o_ref):
        v = l_ref[...]
        r = lax.broadcasted_iota(jnp.int32, v.shape, 0)
        c = lax.broadcasted_iota(jnp.int32, v.shape, 1)
        o_ref[...] = jnp.sum(jnp.where((r % 8 == 0) & (c == 0), v, 0.0), keepdims=True)

    return _call(body, name="loss_sum", grid=(1,), in_specs=[pl.BlockSpec((n, 128), lambda i: (0, 0))],
                 out_specs=pl.BlockSpec((1, 1), lambda i: (0, 0)), out_shape=_sds((1, 1), F32))(lt)


def _gated_norm_fwd(o, g, nw):
    r = lax.rsqrt(jnp.mean(o * o, axis=-1, keepdims=True) + EPS)
    return o * r * nw * _silu(g)


def _gated_norm_bwd(o, g, nw, dout):
    r = lax.rsqrt(jnp.mean(o * o, axis=-1, keepdims=True) + EPS)
    oh = o * r
    don = dout * _silu(g)
    dg = dout * (oh * nw) * _dsilu(g)
    dnw = jnp.sum(don * oh, axis=0, keepdims=True)
    doh = don * nw
    return r * (doh - oh * jnp.mean(doh * oh, axis=-1, keepdims=True)), dg, dnw


def _hg_gates(fs, lbl):
    l0, l1 = lbl[0:1, :], lbl[1:2, :]
    m = jnp.maximum(l0, l1)
    e0, e1 = jnp.exp(l0 - m), jnp.exp(l1 - m)
    lb = e0 / (e0 + e1)
    sig = _sigmoid(fs)
    f = lb + (1.0 - lb) * sig
    return lb, sig, f, jnp.log(f), (1.0 - lb) * _sigmoid(-fs)


def _cumsum(w):
    row = lax.broadcasted_iota(jnp.int32, w.shape, 0) & (CH - 1)
    s = 1
    while s < CH:
        w = w + jnp.where(row >= s, pltpu.roll(w, s, 0), 0.0)
        s *= 2
    return w


def _rcumsum(w):
    row = lax.broadcasted_iota(jnp.int32, w.shape, 0) & (CH - 1)
    s = 1
    while s < CH:
        w = w + jnp.where(row < CH - s, pltpu.roll(w, w.shape[0] - s, 0), 0.0)
        s *= 2
    return w


def _decay_blocks(q, k, b, p_ref):
    p_ref[...] = jnp.zeros((CH, CH), F32)
    m16 = _tri(SUBH, "incl")
    for I in range(CH // SUBH):
        s0 = I * SUBH
        bI, qI, kI = b[s0:s0 + SUBH], q[s0:s0 + SUBH], k[s0:s0 + SUBH]
        dec = jnp.exp(jnp.minimum(bI[:, None, :] - bI[None, :, :], 0.0))
        pii = jnp.sum(qI[:, None, :] * kI[None, :, :] * dec, axis=-1)
        p_ref[s0:s0 + SUBH, s0:s0 + SUBH] = jnp.where(m16, pii, 0.0)
        if I > 0:
            rI = b[s0 - 1:s0]
            qs = qI * jnp.exp(bI - rI)
            ks = k[0:s0] * jnp.exp(rI - b[0:s0])
            p_ref[s0:s0 + SUBH, 0:s0] = _dot(qs, ks, NT)


HPS = 8
HPS_HGRN2_FWD = 4
HB = 4
NHB = NH // HPS
OFF = FRONT // RT


def _head_specs(hps, rev=None):
    row = (lambda i: i) if rev is None else rev
    col = lambda g: pl.BlockSpec((RT, hps * DH), lambda h, i: (row(i), g * (NH // hps) + h))
    full = pl.BlockSpec((RT, hps * DH), lambda h, i: (row(i), h))
    real = pl.BlockSpec((RT, hps * DH), lambda h, i: (jnp.maximum(row(i) - OFF, 0), h))
    state = lambda cpt: pl.BlockSpec((hps, cpt, DH, DH), lambda h, i: (h, row(i), 0, 0))
    scal = pl.BlockSpec((hps, RT, DH), lambda h, i: (h, row(i), 0))
    return col, full, real, state, scal


def _hgrn2_fwd(proj, lb_logits, nw):
    tp = proj.shape[0]
    nt, cpt = tp // RT, RT // CH
    hps = HPS_HGRN2_FWD

    def body(q_ref, f_ref, i_ref, g_ref, lbl_ref, nw_ref, og_ref, or_ref, st_ref, s_ref, p_ref):
        @pl.when(pl.program_id(1) == 0)
        def _():
            s_ref[...] = jnp.zeros((hps, DH, DH), F32)

        def chunk(c, carry):
            rows = pl.ds(pl.multiple_of(c * CH, CH), CH)
            for hh in range(hps):
                cols = slice(hh * DH, (hh + 1) * DH)
                _, _, _, w, k = _hg_gates(f_ref[rows, cols], lbl_ref[:, cols])
                q, v = _silu(q_ref[rows, cols]), i_ref[rows, cols]
                b = _cumsum(w)
                st = s_ref[hh]
                st_ref[hh, c] = st
                _decay_blocks(q, k, b, p_ref.at[hh])
                o = _dot(q * jnp.exp(b), st, NT) + _dot(p_ref[hh], v)
                bl = b[CH - 1:CH]
                s_ref[hh] = st * jnp.exp(bl) + _dot(v, k * jnp.exp(bl - b), TN)
                or_ref[rows, cols] = o
                og_ref[rows, cols] = _gated_norm_fwd(o, g_ref[rows, cols], nw_ref[...]).astype(BF16)
            return carry

        lax.fori_loop(0, cpt, chunk, 0)

    col, full, real, state, _ = _head_specs(hps)
    return _call(body, name="hgrn2_fwd", grid=(NH // hps, nt),
                 in_specs=[col(0), col(1), col(2), col(3), pl.BlockSpec((2, hps * DH), lambda h, i: (0, h)),
                           pl.BlockSpec((1, DH), lambda h, i: (0, 0))],
                 out_specs=[real, full, state(cpt)],
                 out_shape=[_sds((tp - FRONT, HW), BF16), _sds((tp, HW), F32), _sds((NH, tp // CH, DH, DH), F32)],
                 scratch=[pltpu.VMEM((hps, DH, DH), F32), pltpu.VMEM((hps, CH, CH), F32)],
                 sem=("parallel", "arbitrary"))(proj, proj, proj, proj, lb_logits, nw)


def _hgrn2_bwd(proj, lb_logits, nw, o_raw, states, dog, side=None):
    tp = proj.shape[0]
    nt, cpt = tp // RT, RT // CH

    def body(q_ref, f_ref, i_ref, g_ref, lbl_ref, nw_ref, or_ref, st_ref, dog_ref,
             dq_ref, df_ref, di_ref, dg_ref, dl_ref, dnw_ref, ds_ref, p_ref, dk_ref, dqa_ref, do_ref):
        step = pl.program_id(1)

        @pl.when(step == 0)
        def _():
            ds_ref[...] = jnp.zeros((HPS, DH, DH), F32)
            dl_ref[...] = jnp.zeros((2, HPS * DH), F32)

        @pl.when((step == 0) & (pl.program_id(0) == 0))
        def _():
            dnw_ref[...] = jnp.zeros((1, DH), F32)

        front = nt - 1 - step < OFF
        for hh in range(HPS):
            cols = slice(hh * DH, (hh + 1) * DH)
            dog_t = jnp.where(front, 0.0, dog_ref[:, cols])
            do_t, dg_t, dnw = _gated_norm_bwd(or_ref[:, cols], g_ref[:, cols], nw_ref[...], dog_t)
            do_ref[:, cols] = do_t
            dg_ref[:, cols] = dg_t.astype(BF16)
            dnw_ref[...] += dnw
        tril = _tri(CH, "incl")
        m16 = _tri(SUBH, "incl")

        def chunk(cc, carry):
            c = cpt - 1 - cc
            rows = pl.ds(pl.multiple_of(c * CH, CH), CH)
            for hh in range(HPS):
                cols = slice(hh * DH, (hh + 1) * DH)
                fs = f_ref[rows, cols]
                lb, sig, f, w, k = _hg_gates(fs, lbl_ref[:, cols])
                hq = q_ref[rows, cols]
                q, v, do = _silu(hq), i_ref[rows, cols], do_ref[rows, cols]
                b = _cumsum(w)
                bl = b[CH - 1:CH]
                eb = jnp.exp(b)
                qs, kd = q * eb, k * jnp.exp(bl - b)
                st, dst = st_ref[hh, c], ds_ref[hh]
                _decay_blocks(q, k, b, p_ref.at[hh])
                dv = _dot(p_ref[hh], do, TN) + _dot(kd, dst, NT)
                dp = jnp.where(tril, _dot3(do, v, NT), 0.0)
                dqa, dka = dqa_ref.at[hh], dk_ref.at[hh]
                dqa[...] = eb * _dot3(do, st)
                dka[...] = jnp.exp(bl - b) * _dot3(v, dst)
                for I in range(CH // SUBH):
                    s0 = I * SUBH
                    bI, qI, kI = b[s0:s0 + SUBH], q[s0:s0 + SUBH], k[s0:s0 + SUBH]
                    dec = jnp.exp(jnp.minimum(bI[:, None, :] - bI[None, :, :], 0.0))
                    dpii = jnp.where(m16, dp[s0:s0 + SUBH, s0:s0 + SUBH], 0.0)[:, :, None] * dec
                    dqa[s0:s0 + SUBH, :] += jnp.sum(dpii * kI[None, :, :], axis=1)
                    dka[s0:s0 + SUBH, :] += jnp.sum(dpii * qI[:, None, :], axis=0)
                    if I > 0:
                        rI = b[s0 - 1:s0]
                        eq, ek = jnp.exp(bI - rI), jnp.exp(rI - b[0:s0])
                        dpij = dp[s0:s0 + SUBH, 0:s0]
                        dqa[s0:s0 + SUBH, :] += eq * _dot3(dpij, k[0:s0] * ek)
                        dka[0:s0, :] += ek * _dot3(dpij, qI * eq, TN)
                dq, dk = dqa[...], dka[...]
                st_end = st * jnp.exp(bl) + _dot3(v, kd, TN)
                dw = _rcumsum(q * dq - k * dk) + jnp.sum(dst * st_end, axis=0, keepdims=True)
                ds_ref[hh] = dst * jnp.exp(bl) + _dot3(do, qs, TN)
                one_m = 1.0 - sig
                dq_ref[rows, cols] = (dq * _dsilu(hq)).astype(BF16)
                df_ref[rows, cols] = ((dw / f - dk) * (1.0 - lb) * sig * one_m).astype(BF16)
                di_ref[rows, cols] = dv.astype(BF16)
                dl_ref[0:1, cols] += jnp.sum((dw / f - dk) * one_m, axis=0, keepdims=True)
            return carry

        lax.fori_loop(0, cpt, chunk, 0)

        @pl.when(step == nt - 1)
        def _():
            lbl = lbl_ref[...]
            l0, l1 = lbl[0:1, :], lbl[1:2, :]
            m = jnp.maximum(l0, l1)
            e0, e1 = jnp.exp(l0 - m), jnp.exp(l1 - m)
            p0 = e0 / (e0 + e1)
            dl0 = dl_ref[0:1, :] * p0 * (1.0 - p0)
            dl_ref[0:1, :] = dl0
            dl_ref[1:2, :] = -dl0

    col, full, real, state, _ = _head_specs(HPS, lambda i: nt - 1 - i)
    lbs = pl.BlockSpec((2, HPS * DH), lambda h, i: (0, h))
    return _call(body, name="hgrn2_bwd", grid=(NHB, nt),
                 in_specs=[col(0), col(1), col(2), col(3), lbs, pl.BlockSpec((1, DH), lambda h, i: (0, 0)), full,
                           state(cpt), real],
                 out_specs=[full, full, full, full, lbs, pl.BlockSpec((1, DH), lambda h, i: (0, 0))],
                 out_shape=[_sds((tp, HW), BF16)] * 4 + [_sds((2, HW), F32), _sds((1, DH), F32)],
                 scratch=[pltpu.VMEM((HPS, DH, DH), F32), pltpu.VMEM((HPS, CH, CH), F32), pltpu.VMEM((HPS, CH, DH), F32),
                          pltpu.VMEM((HPS, CH, DH), F32), pltpu.VMEM((RT, HPS * DH), F32)],
                 sem=("arbitrary", "arbitrary"), side=side)(proj, proj, proj, proj, lb_logits, nw, o_raw, states, dog)


GQ0 = 4 * HW
CW = 3 * HW


def _gd_scalars(ab, alog, dtb):
    g = -jnp.exp(alog) * jax.nn.softplus(ab + dtb)
    return g, _sigmoid(ab)


def _conv_ext_specs(row_of):
    main = [pl.BlockSpec((RT, HW), lambda i, g=g: (row_of(i), GQ0 // HW + g)) for g in range(3)]
    prev = [pl.BlockSpec((8, HW), lambda i, g=g: (jnp.maximum(row_of(i) * (RT // 8) - 1, 0), GQ0 // HW + g)) for g in range(3)]
    return main + prev


def _conv_fill(ext_ref, xs, xps, first):
    for g in range(3):
        ext_ref[0:8, g * HW:(g + 1) * HW] = jnp.where(first, 0.0, xps[g][...])
        ext_ref[8:8 + RT, g * HW:(g + 1) * HW] = xs[g][...]


def _conv_apply(ext_ref, cw):
    y = cw[CONV_K - 1:CONV_K, :] * ext_ref[pl.ds(8, RT), :]
    for s in range(1, CONV_K):
        y += cw[CONV_K - 1 - s:CONV_K - s, :] * ext_ref[pl.ds(8 - s, RT), :]
    return y


def _gdn_prep_fwd(proj, pab, conv_w, alog, dtb):
    tp = proj.shape[0]
    nt = tp // RT

    def body(x0, x1, x2, p0, p1, p2, ab_ref, cw_ref, al_ref, dt_ref, q_ref, k_ref, v_ref, g_ref, b_ref, ext_ref):
        _conv_fill(ext_ref, (x0, x1, x2), (p0, p1, p2), pl.program_id(0) == 0)
        a = _silu(_conv_apply(ext_ref, cw_ref[...]))
        for h in range(NH):
            for part, ref, sc in ((0, q_ref, DH ** -0.5), (1, k_ref, 1.0)):
                seg = a[:, part * HW + h * DH:part * HW + (h + 1) * DH]
                ref[:, h * DH:(h + 1) * DH] = seg * (lax.rsqrt(jnp.sum(seg * seg, axis=-1, keepdims=True) + EPS) * sc)
        v_ref[...] = a[:, 2 * HW:3 * HW]
        g, beta = _gd_scalars(ab_ref[...], al_ref[...], dt_ref[...])
        for h in range(NH):
            g_ref[h] = jnp.broadcast_to(g[:, h:h + 1], (RT, DH))
            b_ref[h] = jnp.broadcast_to(beta[:, NH + h:NH + h + 1], (RT, DH))

    hs = pl.BlockSpec((RT, HW), lambda i: (i, 0))
    sc = pl.BlockSpec((NH, RT, DH), lambda i: (0, i, 0))
    one = pl.BlockSpec((1, DH), lambda i: (0, 0))
    return _call(body, name="gdn_prep_fwd", grid=(nt,),
                 in_specs=_conv_ext_specs(lambda i: i) + [pl.BlockSpec((RT, DH), lambda i: (i, 0)),
                                                           pl.BlockSpec((CONV_K, CW), lambda i: (0, 0)), one, one],
                 out_specs=[hs, hs, hs, sc, sc],
                 out_shape=[_sds((tp, HW), F32)] * 3 + [_sds((NH, tp, DH), F32)] * 2,
                 scratch=[pltpu.VMEM((RT + 8, CW), F32)], sem=("parallel",))(*([proj] * 6), pab, conv_w, alog, dtb)


def _gdn_prep_bwd(proj, pab, conv_w, alog, dtb, dq, dk, dv, dgb, dbb):
    tp = proj.shape[0]
    nt = tp // RT

    def body(x0, x1, x2, p0, p1, p2, ab_ref, cw_ref, al_ref, dt_ref, dq_ref, dk_ref, dv_ref, dg_ref, db_ref,
             dx_ref, dab_ref, dcw_ref, dal_ref, ddt_ref, ext_ref, dy_ref):
        step = pl.program_id(0)
        i = nt - 1 - step

        @pl.when(step == 0)
        def _():
            dy_ref[RT:RT + 8, :] = jnp.zeros((8, CW), F32)
            dcw_ref[...] = jnp.zeros((8, CW), F32)
            dal_ref[...] = jnp.zeros((1, DH), F32)
            ddt_ref[...] = jnp.zeros((1, DH), F32)

        _conv_fill(ext_ref, (x0, x1, x2), (p0, p1, p2), i == 0)
        cw = cw_ref[...]
        y = _conv_apply(ext_ref, cw)
        a = _silu(y)
        dsl = _dsilu(y)
        for h in range(NH):
            for part, ref, sc in ((0, dq_ref, DH ** -0.5), (1, dk_ref, 1.0)):
                lo = part * HW + h * DH
                seg = a[:, lo:lo + DH]
                r = lax.rsqrt(jnp.sum(seg * seg, axis=-1, keepdims=True) + EPS)
                xh = seg * r
                dxh = ref[:, h * DH:(h + 1) * DH] * sc
                dy_ref[0:RT, lo:lo + DH] = r * (dxh - xh * jnp.sum(dxh * xh, axis=-1, keepdims=True)) * dsl[:, lo:lo + DH]
        dy_ref[0:RT, 2 * HW:3 * HW] = dv_ref[...] * dsl[:, 2 * HW:3 * HW]
        dy = dy_ref[0:RT, :]
        dx = cw[CONV_K - 1:CONV_K, :] * dy
        dcw_ref[CONV_K - 1:CONV_K, :] += jnp.sum(dy * ext_ref[pl.ds(8, RT), :], axis=0, keepdims=True)
        for s in range(1, CONV_K):
            dx += cw[CONV_K - 1 - s:CONV_K - s, :] * dy_ref[pl.ds(s, RT), :]
            dcw_ref[CONV_K - 1 - s:CONV_K - s, :] += jnp.sum(dy * ext_ref[pl.ds(8 - s, RT), :], axis=0, keepdims=True)
        dx_ref[...] = dx.astype(BF16)
        dy_ref[RT:RT + 8, :] = dy[0:8, :]
        ab = ab_ref[...]
        g, beta = _gd_scalars(ab, al_ref[...], dt_ref[...])
        lane = lax.broadcasted_iota(jnp.int32, (RT, DH), 1)
        dgl = jnp.zeros((RT, DH), F32)
        dbl = jnp.zeros((RT, DH), F32)
        for h in range(NH):
            dgl = jnp.where(lane == h, dg_ref[h], dgl)
            dbl = jnp.where(lane == NH + h, db_ref[h], dbl)
        dsp = dgl * (-jnp.exp(al_ref[...])) * _sigmoid(ab + dt_ref[...])
        dab_ref[...] = (dsp + dbl * beta * (1.0 - beta)).astype(BF16)
        ddt_ref[...] += jnp.sum(dsp, axis=0, keepdims=True)
        dal_ref[...] += jnp.sum(dgl * g, axis=0, keepdims=True)

    hs = pl.BlockSpec((RT, HW), lambda s: (nt - 1 - s, 0))
    sc = pl.BlockSpec((NH, RT, DH), lambda s: (0, nt - 1 - s, 0))
    one = pl.BlockSpec((1, DH), lambda s: (0, 0))
    xs = pl.BlockSpec((RT, CW), lambda s: (nt - 1 - s, 0))
    return _call(body, name="gdn_prep_bwd", grid=(nt,),
                 in_specs=_conv_ext_specs(lambda s: nt - 1 - s) + [
                     pl.BlockSpec((RT, DH), lambda s: (nt - 1 - s, 0)), pl.BlockSpec((CONV_K, CW), lambda s: (0, 0)),
                     one, one, hs, hs, hs, sc, sc],
                 out_specs=[xs, pl.BlockSpec((RT, DH), lambda s: (nt - 1 - s, 0)), pl.BlockSpec((8, CW), lambda s: (0, 0)), one, one],
                 out_shape=[_sds((tp, CW), BF16), _sds((tp, DH), BF16), _sds((8, CW), F32), _sds((1, DH), F32), _sds((1, DH), F32)],
                 scratch=[pltpu.VMEM((RT + 8, CW), F32), pltpu.VMEM((RT + 8, CW), F32)],
                 sem=("arbitrary",))(*([proj] * 6), pab, conv_w, alog, dtb, dq, dk, dv, dgb, dbb)


HS = HB * CH


def _stack_heads(ref, rows, base):
    return jnp.concatenate([ref[rows, (base + hh) * DH:(base + hh + 1) * DH] for hh in range(HB)], axis=0)


def _stack_scal(ref, rows, base):
    return jnp.concatenate([ref[base + hh, rows, :] for hh in range(HB)], axis=0)


def _store_heads(ref, rows, base, val):
    for hh in range(HB):
        ref[rows, (base + hh) * DH:(base + hh + 1) * DH] = val[hh * CH:(hh + 1) * CH].astype(ref.dtype)


def _bd_masks():
    r = lax.broadcasted_iota(jnp.int32, (HS, HS), 0)
    c = lax.broadcasted_iota(jnp.int32, (HS, HS), 1)
    same = lax.shift_right_logical(r, int(math.log2(CH))) == lax.shift_right_logical(c, int(math.log2(CH)))
    return same & (r >= c), same & (r > c)


def _unit_lower_inverse(a):
    n = a.shape[0]
    r = lax.broadcasted_iota(jnp.int32, (n, n), 0)
    c = lax.broadcasted_iota(jnp.int32, (n, n), 1)
    blk_of = lambda t, size: lax.shift_right_logical(t, int(math.log2(size)))
    a16 = jnp.where(blk_of(r, SUB) == blk_of(c, SUB), a, 0.0)
    x = (r == c).astype(F32) - a16
    p = a16
    for _ in range(3):
        p = _dot(p, p)
        x = x + _dot(x, p)
    for blk in (2 * SUB, 4 * SUB):
        off = jnp.where((blk_of(r, blk) == blk_of(c, blk)) & (blk_of(r, blk // 2) != blk_of(c, blk // 2)), a, 0.0)
        x = x - _dot(x, _dot(off, x))
    return x


def _gdn_chunk_common(q, k, v, gl, bt, incl, strict):
    gc = _cumsum(gl)
    e = jnp.exp(gc)
    rel = jnp.exp(jnp.minimum(gc[:, 0:1] - gc.T[0:1, :], 0.0))
    kb = bt * k
    a = jnp.where(strict, bt[:, 0:1] * _dot(k, k, NT) * rel, 0.0)
    x = _unit_lower_inverse(a)
    wu = _dot(x, jnp.concatenate([kb * e, bt * v], axis=1))
    attn = jnp.where(incl, _dot(q, k, NT) * rel, 0.0)
    return gc, e, rel, kb, a, x, wu[:, 0:DH], wu[:, DH:2 * DH], attn


def _gdn_fwd(q, k, v, gb, bb, proj, nw):
    tp = q.shape[0]
    nt, cpt = tp // RT, RT // CH

    def body(q_ref, k_ref, v_ref, g_ref, b_ref, z_ref, nw_ref, og_ref, or_ref, st_ref, s_ref):
        @pl.when(pl.program_id(1) == 0)
        def _():
            s_ref[...] = jnp.zeros((HPS, DH, DH), F32)

        incl, strict = _bd_masks()

        def chunk(c, carry):
            rows = pl.ds(pl.multiple_of(c * CH, CH), CH)
            for base in range(0, HPS, HB):
                qc, kc, vc = _stack_heads(q_ref, rows, base), _stack_heads(k_ref, rows, base), _stack_heads(v_ref, rows, base)
                gc, e, rel, kb, a, x, w, u, attn = _gdn_chunk_common(qc, kc, vc, _stack_scal(g_ref, rows, base),
                                                                    _stack_scal(b_ref, rows, base), incl, strict)
                qe = qc * e
                ws, qs = [], []
                for hh in range(HB):
                    blk = slice(hh * CH, (hh + 1) * CH)
                    s = s_ref[base + hh]
                    st_ref[base + hh, c] = s
                    both = _dot(jnp.concatenate([w[blk], qe[blk]], axis=0), s)
                    ws.append(both[0:CH])
                    qs.append(both[CH:2 * CH])
                vn = u - jnp.concatenate(ws, axis=0)
                o = jnp.concatenate(qs, axis=0) + _dot(attn, vn)
                for hh in range(HB):
                    blk = slice(hh * CH, (hh + 1) * CH)
                    gl = gc[(hh + 1) * CH - 1:(hh + 1) * CH]
                    s_ref[base + hh] = s_ref[base + hh] * jnp.exp(gl) + _dot(kc[blk] * jnp.exp(gl - gc[blk]), vn[blk], TN)
                _store_heads(or_ref, rows, base, o)
                for hh in range(HB):
                    cols = slice((base + hh) * DH, (base + hh + 1) * DH)
                    og_ref[rows, cols] = _gated_norm_fwd(o[hh * CH:(hh + 1) * CH], z_ref[rows, cols], nw_ref[...]).astype(BF16)
            return carry

        lax.fori_loop(0, cpt, chunk, 0)

    col, full, real, state, scal = _head_specs(HPS)
    return _call(body, name="gdn_fwd", grid=(NHB, nt),
                 in_specs=[full, full, full, scal, scal, col(7), pl.BlockSpec((1, DH), lambda h, i: (0, 0))],
                 out_specs=[real, full, state(cpt)],
                 out_shape=[_sds((tp - FRONT, HW), BF16), _sds((tp, HW), F32), _sds((NH, tp // CH, DH, DH), F32)],
                 scratch=[pltpu.VMEM((HPS, DH, DH), F32)], sem=("parallel", "arbitrary"))(q, k, v, gb, bb, proj, nw)


def _gdn_bwd(q, k, v, gb, bb, proj, nw, o_raw, states, dog):
    tp = q.shape[0]
    nt, cpt = tp // RT, RT // CH

    def body(q_ref, k_ref, v_ref, g_ref, b_ref, z_ref, nw_ref, or_ref, st_ref, dog_ref,
             dq_ref, dk_ref, dv_ref, dg_ref, db_ref, dz_ref, dnw_ref, ds_ref, do_ref):
        step = pl.program_id(1)

        @pl.when(step == 0)
        def _():
            ds_ref[...] = jnp.zeros((HPS, DH, DH), F32)

        @pl.when((step == 0) & (pl.program_id(0) == 0))
        def _():
            dnw_ref[...] = jnp.zeros((1, DH), F32)

        front = nt - 1 - step < OFF
        for hh in range(HPS):
            cols = slice(hh * DH, (hh + 1) * DH)
            dog_t = jnp.where(front, 0.0, dog_ref[:, cols])
            do_t, dz_t, dnw = _gated_norm_bwd(or_ref[:, cols], z_ref[:, cols], nw_ref[...], dog_t)
            do_ref[:, cols] = do_t
            dz_ref[:, cols] = dz_t.astype(BF16)
            dnw_ref[...] += dnw
        incl, strict = _bd_masks()
        last_row = (lax.broadcasted_iota(jnp.int32, (CH, 1), 0) == CH - 1)

        def rsum(t):
            return jnp.sum(t, axis=-1, keepdims=True)

        def chunk(cc, carry):
            c = cpt - 1 - cc
            rows = pl.ds(pl.multiple_of(c * CH, CH), CH)
            for base in range(0, HPS, HB):
                qc, kc, vc, do = (_stack_heads(q_ref, rows, base), _stack_heads(k_ref, rows, base), _stack_heads(v_ref, rows, base),
                                  _stack_heads(do_ref, rows, base))
                bt = _stack_scal(b_ref, rows, base)
                gc, e, rel, kb, a, x, w, u, attn = _gdn_chunk_common(qc, kc, vc, _stack_scal(g_ref, rows, base), bt, incl, strict)
                qe = qc * e
                heads = [slice(hh * CH, (hh + 1) * CH) for hh in range(HB)]
                gls = [gc[(hh + 1) * CH - 1:(hh + 1) * CH] for hh in range(HB)]
                cdec = jnp.concatenate([jnp.exp(gl - gc[blk]) for gl, blk in zip(gls, heads)], axis=0)
                kcd = kc * cdec
                vn = u - jnp.concatenate([_dot(w[blk], st_ref[base + hh, c]) for hh, blk in enumerate(heads)], axis=0)
                dos = jnp.concatenate([_dot(do[blk], st_ref[base + hh, c], NT) for hh, blk in enumerate(heads)], axis=0)
                kds = jnp.concatenate([_dot(kcd[blk], ds_ref[base + hh]) for hh, blk in enumerate(heads)], axis=0)
                vds = jnp.concatenate([_dot(vn[blk], ds_ref[base + hh], NT) for hh, blk in enumerate(heads)], axis=0)
                dvn = _dot(attn, do, TN) + kds
                dattn = jnp.where(incl, _dot(do, vn, NT), 0.0)
                dar = dattn * rel
                dq = _dot(dar, kc) + e * dos
                dk = _dot(dar, qc, TN) + cdec * vds
                dc = cdec[:, 0:1] * rsum(kc * vds)
                dgc = rsum(qe * dos) - dc
                dw = -jnp.concatenate([_dot(dvn[blk], st_ref[base + hh, c], NT) for hh, blk in enumerate(heads)], axis=0)
                extra = []
                for hh, blk in enumerate(heads):
                    s, dsn = st_ref[base + hh, c], ds_ref[base + hh]
                    el = jnp.exp(gls[hh])
                    dglast = jnp.sum(dc[blk], axis=0, keepdims=True) + el[:, 0:1] * jnp.sum(rsum(dsn * s), axis=0, keepdims=True)
                    extra.append(jnp.where(last_row, dglast, 0.0))
                    ds_ref[base + hh] = dsn * el + _dot(jnp.concatenate([qe[blk], -w[blk]], axis=0),
                                                 jnp.concatenate([do[blk], dvn[blk]], axis=0), TN)
                dr = _dot(x, jnp.concatenate([dw, dvn], axis=1), TN)
                drw, dru = dr[:, 0:DH], dr[:, DH:2 * DH]
                da = -jnp.where(strict, _dot(dr, jnp.concatenate([w, u], axis=1), NT), 0.0)
                dar2 = da * rel
                dkb = _dot(dar2, kc)
                rwk = rsum(drw * kc)
                dk = dk + _dot(dar2, kb, TN) + bt * dkb + (bt * e) * drw
                dbeta = rsum(dkb * kc) + e[:, 0:1] * rwk + rsum(dru * vc)
                z = dattn * attn + da * a
                dgc = dgc + bt[:, 0:1] * e[:, 0:1] * rwk + rsum(z) - rsum(z.T) + jnp.concatenate(extra, axis=0)
                _store_heads(dq_ref, rows, base, dq)
                _store_heads(dk_ref, rows, base, dk)
                _store_heads(dv_ref, rows, base, bt * dru)
                dg = _rcumsum(jnp.broadcast_to(dgc, (HS, DH)))
                dbb = jnp.broadcast_to(dbeta, (HS, DH))
                for hh, blk in enumerate(heads):
                    dg_ref[base + hh, rows, :] = dg[blk]
                    db_ref[base + hh, rows, :] = dbb[blk]
            return carry

        lax.fori_loop(0, cpt, chunk, 0)

    col, full, real, state, scal = _head_specs(HPS, lambda i: nt - 1 - i)
    one = pl.BlockSpec((1, DH), lambda h, i: (0, 0))
    return _call(body, name="gdn_bwd", grid=(NHB, nt),
                 in_specs=[full, full, full, scal, scal, col(7), one, full, state(cpt), real],
                 out_specs=[full, full, full, scal, scal, full, one],
                 out_shape=[_sds((tp, HW), F32)] * 3 + [_sds((NH, tp, DH), F32)] * 2 + [_sds((tp, HW), BF16), _sds((1, DH), F32)],
                 scratch=[pltpu.VMEM((HPS, DH, DH), F32), pltpu.VMEM((RT, HPS * DH), F32)],
                 sem=("arbitrary", "arbitrary"))(q, k, v, gb, bb, proj, nw, o_raw, states, dog)


MAIN_W = 8 * HW
AB_W = 2 * NH


def _split_w_in(w_in):
    main = jnp.concatenate([w_in[:, :MAIN_W], w_in[:, MAIN_W + AB_W:]], axis=1)
    ab = jnp.pad(w_in[:, MAIN_W:MAIN_W + AB_W], ((0, 0), (0, DH - AB_W)))
    return main, ab


def _pad_lanes(v):
    return jnp.pad(v, ((0, 0), (0, DH - v.shape[1])))


class _NoComm:
    def __init__(self, late):
        self.late = late

    def proj_side(self):
        return None

    def late_weights(self, side_outs):
        return self.late

    def early_grads_side(self, grads):
        return None

    def early_grads_done(self, side_outs):
        pass

    def last_grad_side(self, dw_in):
        return None

    def last_grad_done(self, side_outs):
        pass


def _local_step(x, tgt, meta, lb_logits, mix_w, w_in, hg_nw, conv_w, a_log, dt_bias, gd_nw, ffn_nw, final_w, comm):
    w_main, w_ab = _split_w_in(w_in)
    alog, dtb = _pad_lanes(a_log), _pad_lanes(dt_bias)
    final_w = final_w.reshape(1, -1)
    rows4 = lambda t: t.reshape(4, t.shape[0] // 4, t.shape[1])
    xn = _rms1_fwd(x, meta, mix_w)
    side = comm.proj_side()
    proj = _mm(xn, w_main, "nn", F32, 768, 2048, 2048, "proj_main", n_outer=True, side=side)
    proj, landed = proj if side is not None else (proj, None)
    w_a, w_b, w_out, w_ffn_in, w_ffn_out = comm.late_weights(landed)
    pab = _mm(xn, w_ab, "nn", F32, 768, 128, 2048, "proj_ab")
    oa_g, oa_raw, st_a = _hgrn2_fwd(proj, lb_logits, hg_nw)
    q, k, v, gb, bb = _gdn_prep_fwd(proj, pab, conv_w, alog, dtb)
    ob_g, ob_raw, st_b = _gdn_fwd(q, k, v, gb, bb, proj, gd_nw)
    za = _mm(oa_g, w_a, "nn", F32, 1024, 512, 1024, "branch_a", n_outer=True)
    zb = _mm(ob_g, w_b, "nn", F32, 1024, 512, 1024, "branch_b", n_outer=True)
    merged = _merge_fwd(proj, za, zb)
    mix = _mm(merged, w_out, "nn", F32, 1024, 2048, 2048, "mix_out")
    h1, n2 = _resid_norm_fwd(x, mix, ffn_nw)
    gu = _mm(n2, w_ffn_in, "nn", BF16, 1024, 1408, 2048, "ffn_in", n_outer=True)
    act = _swiglu_fwd(gu)
    f = _mm(act, w_ffn_out, "nn", F32, 1024, 2048, 1408, "ffn_out")
    lt, dh2, dh2b, dfinal = _loss_head(h1, f, final_w, tgt)
    loss = _sum_tiles(lt)
    dact = _mm(dh2b, w_ffn_out, "nt", BF16, 1024, 1408, 2048, "d_act", n_outer=True)
    dw_ffn_out = rows4(_mm(act, dh2b, "tn", F32, 512, 2048, 2048, "dw_ffn_out"))
    dgu = _swiglu_bwd(gu, dact)
    dn2 = _mm(dgu, w_ffn_in, "nt", F32, 1024, 2048, 1408, "d_n2")
    dw_ffn_in = _mm(n2, dgu, "tn", F32, 1024, 1408, 2048, "dw_ffn_in", out_shards=4)
    dh1, dh1b, dffn_nw = _resid_norm_bwd(h1, ffn_nw, dn2, dh2)
    dmerged = _mm(dh1b, w_out, "nt", F32, 1024, 2048, 2048, "d_merged")
    dw_out = _mm(merged, dh1b, "tn", F32, 2048, 1024, 1024, "dw_out")
    dza, dzb, dgate = _merge_bwd(proj, za, zb, dmerged)
    doa = _mm(dza, w_a, "nt", F32, 1024, 1024, 512, "d_oa")
    dob = _mm(dzb, w_b, "nt", F32, 1024, 1024, 512, "d_ob")
    dw_a = _mm(oa_g, dza, "tn", F32, 1024, 512, 1024, "dw_branch_a", out_shards=4)
    dw_b = _mm(ob_g, dzb, "tn", F32, 1024, 512, 1024, "dw_branch_b", out_shards=4)
    early = dict(w_ffn_in=dw_ffn_in, w_ffn_out=dw_ffn_out, w_out=rows4(dw_out), w_branch_a=dw_a, w_branch_b=dw_b)
    side = comm.early_grads_side(early)
    hg = _hgrn2_bwd(proj, lb_logits, hg_nw, oa_raw, st_a, doa, side=side)
    if side is not None:
        hg, arrived = hg
        comm.early_grads_done(arrived)
    dhq, dhf, dhi, dhg, dlbl, dhg_nw = hg
    dq, dk, dv, dg, dbeta, dz, dgd_nw = _gdn_bwd(q, k, v, gb, bb, proj, gd_nw, ob_raw, st_b, dob)
    dx3, dab, dconv, dalog, ddtb = _gdn_prep_bwd(proj, pab, conv_w, alog, dtb, dq, dk, dv, dg, dbeta)
    dproj = jnp.concatenate([dhq, dhf, dhi, dhg, dx3, dz, dgate], axis=1)
    dw_main = _mm(xn, dproj, "tn", F32, 1024, 1024, 2816, "dw_in_main")
    dw_ab = _mm(xn, dab, "tn", F32, 2048, 128, 768, "dw_in_ab")
    d_model = dw_main.shape[0]
    dw_in = jnp.concatenate([dw_main[:, :MAIN_W], dw_ab[:, :AB_W], dw_main[:, MAIN_W:]], axis=1)
    dw_in = dw_in.reshape(d_model, 4, -1).transpose(1, 0, 2)
    side = comm.last_grad_side(dw_in)
    dxn = _mm(dproj, w_main, "nt", F32, 768, 2048, 2048, "d_xn", side=side)
    if side is not None:
        dxn, arrived = dxn
        comm.last_grad_done(arrived)
    dxn = _mm(dab, w_ab, "nt", F32, 768, 2048, 128, "d_xn_ab", add=dxn)
    dx, dmeta, dmix_w = _rms1_bwd(x, meta, mix_w, dxn, dh1)
    grads = dict(meta_tokens=dmeta, lb_logits=dlbl, mix_norm_w=dmix_w, w_in=dw_in,
                 hg_norm_w=dhg_nw, gd_conv_w=dconv[:CONV_K], gd_a_log=dalog[:, :NH],
                 gd_dt_bias=ddtb[:, :NH], gd_norm_w=dgd_nw, w_branch_a=dw_a, w_branch_b=dw_b,
                 w_out=rows4(dw_out), ffn_norm_w=dffn_nw, w_ffn_in=dw_ffn_in, w_ffn_out=dw_ffn_out,
                 final_norm_w=dfinal.reshape(-1))
    return loss, dx, grads


def _adamw(g, w, m, v, name):
    rows, cols = g.shape
    tr = rows
    for cand in (128, 64, 32, 16, 8):
        if rows % cand == 0 and rows > cand:
            tr = cand
            break

    def body(g_ref, w_ref, m_ref, v_ref, d_ref, nm_ref, nv_ref):
        gg = g_ref[...]
        nm = ADAM_B1 * m_ref[...] + (1.0 - ADAM_B1) * gg
        nv = ADAM_B2 * v_ref[...] + (1.0 - ADAM_B2) * (gg * gg)
        m_hat = nm / (1.0 - ADAM_B1 ** ADAM_STEP)
        v_hat = nv / (1.0 - ADAM_B2 ** ADAM_STEP)
        d_ref[...] = -ADAM_LR * (m_hat / (jnp.sqrt(v_hat) + ADAM_EPS) + ADAM_WD * w_ref[...])
        nm_ref[...] = nm
        nv_ref[...] = nv

    bs = pl.BlockSpec((tr, cols), lambda i: (i, 0))
    return _call(body, name=name, grid=(rows // tr,), in_specs=[bs] * 4, out_specs=[bs] * 3,
                 out_shape=[_sds((rows, cols), F32)] * 3, sem=("parallel",))(g, w, m, v)


HBM = pl.BlockSpec(memory_space=pltpu.HBM)
MESH = pl.DeviceIdType.MESH


def _place():
    x, y, c = lax.axis_index("x"), lax.axis_index("y"), lax.axis_index("c")
    return x, y, c, [(1 - x, y), (x, 1 - y), (1 - x, 1 - y)]


def _comm_call(body, name, out_shape, n_in, scratch):
    return pl.pallas_call(body, name=name, out_shape=out_shape, in_specs=[HBM] * n_in,
                          out_specs=jax.tree.map(lambda _: HBM, out_shape), scratch_shapes=scratch)


def _half_rows(rows, c, tile):
    hh = rows // 2
    assert rows % 2 == 0 and hh % tile == 0, (rows, tile)
    return pl.ds(pl.multiple_of(c * hh, tile), hh)


def _gather_copies(w_refs, out_refs, sems):
    send_sems, recv_sems = sems
    x, y, c, chips = _place()
    s_me = 2 * x + y
    sends, recvs = [], []
    for k, (w_ref, out_ref) in enumerate(zip(w_refs, out_refs)):
        half = _half_rows(w_ref.shape[0], c, 16)
        for j, (cx, cy) in enumerate(chips):
            sem = dict(send_sem=send_sems.at[3 * k + j], recv_sem=recv_sems.at[3 * k + j], device_id=(cx, cy, c), device_id_type=MESH)
            sends.append(pltpu.make_async_remote_copy(src_ref=w_ref.at[half], dst_ref=out_ref.at[s_me, half], **sem))
            recvs.append(pltpu.make_async_remote_copy(src_ref=w_ref.at[half], dst_ref=out_ref.at[2 * cx + cy, half], **sem))
    return sends, recvs


def _gather_sems(n):
    return [pltpu.SemaphoreType.DMA((3 * n,)), pltpu.SemaphoreType.DMA((3 * n,))]


def _gather_start(w_refs, out_refs, sems):
    for cp in _gather_copies(w_refs, out_refs, sems)[0]:
        cp.start()


def _gather_wait(w_refs, out_refs, sems):
    sends, recvs = _gather_copies(w_refs, out_refs, sems)
    for cp in recvs:
        cp.wait_recv()
    for cp in sends:
        cp.wait_send()


def _gather_chips(shards):
    n = len(shards)

    def body(*refs):
        _gather_start(refs[:n], refs[n:2 * n], refs[2 * n:])
        _gather_wait(refs[:n], refs[n:2 * n], refs[2 * n:])

    return _comm_call(body, "gather_chips", [_sds((4,) + w.shape, w.dtype) for w in shards], n, _gather_sems(n))(*shards)


def _gather_side(shards):
    return _Side(shards, [_sds((4,) + w.shape, w.dtype) for w in shards], _gather_sems(len(shards)), _gather_start, _gather_wait)


def _forward_halves(outs, name):
    n = len(outs)

    def body(*refs):
        out_refs = refs[n:2 * n]
        send_sems, recv_sems = refs[2 * n:]
        x, y, c, chips = _place()
        cps = []
        for k in range(n):
            rows = out_refs[k].shape[1]
            half, other = _half_rows(rows, c, 16), _half_rows(rows, 1 - c, 16)
            for j, (cx, cy) in enumerate(chips):
                sem = dict(send_sem=send_sems.at[3 * k + j], recv_sem=recv_sems.at[3 * k + j], device_id=(x, y, 1 - c), device_id_type=MESH)
                landed = out_refs[k].at[2 * cx + cy, half]
                cps.append(pltpu.make_async_remote_copy(src_ref=landed, dst_ref=landed, **sem))
                cps[-1].start()
        for k in range(n):
            rows = out_refs[k].shape[1]
            half, other = _half_rows(rows, c, 16), _half_rows(rows, 1 - c, 16)
            for j, (cx, cy) in enumerate(chips):
                sem = dict(send_sem=send_sems.at[3 * k + j], recv_sem=recv_sems.at[3 * k + j], device_id=(x, y, 1 - c), device_id_type=MESH)
                pltpu.make_async_remote_copy(src_ref=out_refs[k].at[2 * cx + cy, half], dst_ref=out_refs[k].at[2 * cx + cy, other], **sem).wait_recv()
        for cp in cps:
            cp.wait_send()

    shapes = [_sds(o.shape, o.dtype) for o in outs]
    return pl.pallas_call(body, name=name, out_shape=shapes, in_specs=[HBM] * n, out_specs=[HBM] * n,
                          input_output_aliases={k: k for k in range(n)},
                          scratch_shapes=[pltpu.SemaphoreType.DMA((3 * n,)), pltpu.SemaphoreType.DMA((3 * n,))])(*outs)


def _swap_halves(gs, name):
    n = len(gs)

    def body(*refs):
        g_refs, out_refs = refs[:n], refs[n:2 * n]
        send_sems, recv_sems = refs[2 * n:]
        x, y, c, _ = _place()
        cps = []
        for k in range(n):
            other = _half_rows(g_refs[k].shape[1], 1 - c, 8)
            cps.append(pltpu.make_async_remote_copy(src_ref=g_refs[k].at[:, other, :], dst_ref=out_refs[k], send_sem=send_sems.at[k],
                                                    recv_sem=recv_sems.at[k], device_id=(x, y, 1 - c), device_id_type=MESH))
            cps[-1].start()
        for cp in cps:
            cp.wait()

    return _comm_call(body, name, [_sds((4, g.shape[1] // 2, g.shape[2]), g.dtype) for g in gs], n,
                      [pltpu.SemaphoreType.DMA((n,)), pltpu.SemaphoreType.DMA((n,))])(*gs)


def _row_tile(rows, row_bytes, budget=3 << 20):
    if rows * row_bytes <= budget:
        return rows
    return max(t for t in range(16, rows, 16) if rows % t == 0 and t * row_bytes <= budget)


def _add_half(g, got, c, name):
    _, rows, cols = g.shape
    hh = rows // 2
    tr = _row_tile(hh, cols * 4)
    nb = hh // tr

    def body(c_ref, a_ref, b_ref, o_ref):
        o_ref[...] = (a_ref[...] + b_ref[...]).astype(BF16)

    gs = pltpu.PrefetchScalarGridSpec(
        num_scalar_prefetch=1, grid=(4, nb),
        in_specs=[pl.BlockSpec((1, tr, cols), lambda s, i, c_ref: (s, c_ref[0] * nb + i, 0)),
                  pl.BlockSpec((1, tr, cols), lambda s, i, c_ref: (s, i, 0))],
        out_specs=pl.BlockSpec((1, tr, cols), lambda s, i, c_ref: (s, i, 0)))
    return pl.pallas_call(body, name=name, grid_spec=gs, out_shape=_sds((4, hh, cols), BF16),
                          compiler_params=pltpu.CompilerParams(dimension_semantics=("parallel", "parallel"),
                                                               vmem_limit_bytes=VMEM_LIMIT))(c, g, got)


def _scatter_copies(p_refs, out_refs, sems):
    send_sems, recv_sems = sems
    x, y, c, chips = _place()
    s_me = 2 * x + y
    cps = []
    for k, (p_ref, out_ref) in enumerate(zip(p_refs, out_refs)):
        for j, (cx, cy) in enumerate(chips):
            cps.append(pltpu.make_async_remote_copy(src_ref=p_ref.at[2 * cx + cy], dst_ref=out_ref.at[s_me],
                                                    send_sem=send_sems.at[3 * k + j], recv_sem=recv_sems.at[3 * k + j],
                                                    device_id=(cx, cy, c), device_id_type=MESH))
    return cps


def _scatter_start(p_refs, out_refs, sems):
    for cp in _scatter_copies(p_refs, out_refs, sems):
        cp.start()


def _scatter_wait(p_refs, out_refs, sems):
    for cp in _scatter_copies(p_refs, out_refs, sems):
        cp.wait()


def _scatter_chips(ps):
    n = len(ps)

    def body(*refs):
        _scatter_start(refs[:n], refs[n:2 * n], refs[2 * n:])
        _scatter_wait(refs[:n], refs[n:2 * n], refs[2 * n:])

    return _comm_call(body, "scatter_chips", [_sds(p_.shape, p_.dtype) for p_ in ps], n, _gather_sems(n))(*ps)


def _scatter_side(ps):
    return _Side(ps, [_sds(p_.shape, p_.dtype) for p_ in ps], _gather_sems(len(ps)), _scatter_start, _scatter_wait)


def _sum_slabs(b, name):
    n, h, wd = b.shape
    tr = _row_tile(h, n * wd * 4, 6 << 20)

    def body(b_ref, o_ref):
        acc = b_ref[0]
        for s in range(1, n):
            acc = acc + b_ref[s]
        o_ref[...] = acc

    return _call(body, name=name, grid=(h // tr,), in_specs=[pl.BlockSpec((n, tr, wd), lambda i: (0, i, 0))],
                 out_specs=pl.BlockSpec((tr, wd), lambda i: (i, 0)), out_shape=_sds((h, wd), F32), sem=("parallel",))(b)


def _sum_chips(arrived, own, name):
    n, h, wd = arrived.shape
    tr = _row_tile(h, n * wd * 2, 6 << 20)
    nb = h // tr
    my_chip = lambda: 2 * lax.axis_index("x") + lax.axis_index("y")

    def body(b_ref, p_ref, o_ref):
        acc = None
        for s in range(n):
            term = jnp.where(my_chip() == s, p_ref[0], b_ref[s]).astype(F32)
            acc = term if acc is None else acc + term
        o_ref[...] = acc

    return _call(body, name=name, grid=(nb,),
                 in_specs=[pl.BlockSpec((n, tr, wd), lambda i: (0, i, 0)), pl.BlockSpec((1, tr, wd), lambda i: (my_chip(), i, 0))],
                 out_specs=pl.BlockSpec((tr, wd), lambda i: (lax.axis_index("c") * nb + i, 0)),
                 out_shape=_sds((2 * h, wd), F32), sem=("parallel",))(arrived, own)


def _share_halves(gs):
    n = len(gs)

    def body(*refs):
        out_refs = refs[n:2 * n]
        send_sems, recv_sems = refs[2 * n:]
        x, y, c, _ = _place()
        cps = []
        for k in range(n):
            half, other = _half_rows(out_refs[k].shape[0], c, 8), _half_rows(out_refs[k].shape[0], 1 - c, 8)
            sem = dict(send_sem=send_sems.at[k], recv_sem=recv_sems.at[k], device_id=(x, y, 1 - c), device_id_type=MESH)
            cps.append((pltpu.make_async_remote_copy(src_ref=out_refs[k].at[half], dst_ref=out_refs[k].at[half], **sem),
                        pltpu.make_async_remote_copy(src_ref=out_refs[k].at[half], dst_ref=out_refs[k].at[other], **sem)))
            cps[-1][0].start()
        for send, recv in cps:
            recv.wait_recv()
            send.wait_send()

    return pl.pallas_call(body, name="share_halves", out_shape=[_sds(g.shape, g.dtype) for g in gs], in_specs=[HBM] * n,
                          out_specs=[HBM] * n, input_output_aliases={k: k for k in range(n)},
                          scratch_shapes=[pltpu.SemaphoreType.DMA((n,)), pltpu.SemaphoreType.DMA((n,))])(*gs)


def _gather_all(v, name):
    def body(v_ref, out_ref, send_sems, recv_sems, local_sem):
        x, y, c = lax.axis_index("x"), lax.axis_index("y"), lax.axis_index("c")
        me = 4 * x + 2 * y + c
        flip = lambda t, d: 1 - t if d else t
        mine = pltpu.make_async_copy(v_ref, out_ref.at[me], local_sem)
        mine.start()
        cps = []
        for k in range(1, 8):
            to = (flip(x, k & 4), flip(y, k & 2), flip(c, k & 1))
            cps.append(pltpu.make_async_remote_copy(src_ref=v_ref, dst_ref=out_ref.at[me], send_sem=send_sems.at[k - 1],
                                                    recv_sem=recv_sems.at[k - 1], device_id=to, device_id_type=MESH))
        for cp in cps:
            cp.start()
        for cp in cps:
            cp.wait()
        mine.wait()

    return _comm_call(body, name, _sds((8,) + v.shape, v.dtype), 1,
                      [pltpu.SemaphoreType.DMA((7,)), pltpu.SemaphoreType.DMA((7,)), pltpu.SemaphoreType.DMA])(v)


BIG = (("w_in", 1), ("w_branch_a", 1), ("w_branch_b", 1), ("w_out", 0), ("w_ffn_in", 1), ("w_ffn_out", 0))
SMALL = ("meta_tokens", "lb_logits", "mix_norm_w", "hg_norm_w", "gd_conv_w", "gd_a_log", "gd_dt_bias", "gd_norm_w",
         "ffn_norm_w", "final_norm_w")


def _pack_lanes(parts):
    rows = []
    for p in parts:
        f = p.reshape(-1).astype(F32)
        n = -(-f.shape[0] // DH) * DH
        rows.append(jnp.pad(f, (0, n - f.shape[0])).reshape(-1, DH))
    buf = jnp.concatenate(rows, axis=0)
    return jnp.pad(buf, ((0, -buf.shape[0] % 8), (0, 0)))


def _unpack_lanes(buf, shapes):
    out, off = [], 0
    for shp in shapes:
        n = math.prod(shp)
        r = -(-n // DH)
        out.append(buf[off:off + r].reshape(-1)[:n].reshape(shp))
        off += r
    return out


def kernel(x, meta_tokens, lb_logits, mix_norm_w, w_in, hg_norm_w, gd_conv_w, gd_a_log, gd_dt_bias, gd_norm_w, w_branch_a, w_branch_b, w_out, ffn_norm_w, w_ffn_in, w_ffn_out, final_norm_w, loss_target, m_meta_tokens, m_lb_logits, m_mix_norm_w, m_w_in, m_hg_norm_w, m_gd_conv_w, m_gd_a_log, m_gd_dt_bias, m_gd_norm_w, m_w_branch_a, m_w_branch_b, m_w_out, m_ffn_norm_w, m_w_ffn_in, m_w_ffn_out, m_final_norm_w, v_meta_tokens, v_lb_logits, v_mix_norm_w, v_w_in, v_hg_norm_w, v_gd_conv_w, v_gd_a_log, v_gd_dt_bias, v_gd_norm_w, v_w_branch_a, v_w_branch_b, v_w_out, v_ffn_norm_w, v_w_ffn_in, v_w_ffn_out, v_final_norm_w):
    args = dict(locals())
    big = [n for n, _ in BIG]
    w = {n: args[n] for n in SMALL + tuple(big)}
    m = {n: args["m_" + n] for n in w}
    v = {n: args["v_" + n] for n in w}
    xi, yi, ci = lax.axis_index("x"), lax.axis_index("y"), lax.axis_index("c")
    shard = 2 * xi + yi
    big_local = {n: w[n][0] for n in big}

    meta_cols, conv_cols = meta_tokens.shape[1], gd_conv_w.shape[-1]
    sm_all = _gather_all(_pack_lanes([meta_tokens, gd_conv_w[0]]), "gather_meta")
    sm_parts = [_unpack_lanes(sm_all[2 * s], [meta_tokens.shape, gd_conv_w[0].shape]) for s in range(4)]
    meta_full = jnp.concatenate([p[0] for p in sm_parts], axis=1)
    conv_full = jnp.concatenate([p[1] for p in sm_parts], axis=1)
    cvec = ci.reshape(1).astype(jnp.int32)
    late = [n for n in big if n != "w_in"]
    rows_full = lambda t: t.reshape(t.shape[0] * t.shape[1], t.shape[2])

    def pair_sums(names, gs):
        return [_add_half(gk, got, cvec, "add_half_" + n) for n, gk, got in zip(names, gs, _swap_halves(gs, "swap_" + names[0]))]

    def with_own(slabs, n):
        return lax.dynamic_update_index_in_dim(slabs, big_local[n].astype(BF16), shard, 0)

    class MeshComm:
        def proj_side(self):
            return _gather_side([big_local[n].astype(BF16) for n in late])

        def late_weights(self, landed):
            wl = {n: with_own(t, n) for n, t in zip(late, _forward_halves(landed, "forward_late"))}
            return (wl["w_branch_a"], wl["w_branch_b"], rows_full(wl["w_out"]), wl["w_ffn_in"], rows_full(wl["w_ffn_out"]))

        def early_grads_side(self, grads):
            self.early = list(grads)
            self.early_parts = pair_sums(self.early, [grads[n] for n in self.early])
            return _scatter_side(self.early_parts)

        def early_grads_done(self, arrived):
            self.early_arrived = arrived

        def last_grad_side(self, dw_in):
            self.last_parts = pair_sums(["w_in"], [dw_in])
            return _scatter_side(self.last_parts)

        def last_grad_done(self, arrived):
            self.last_arrived = arrived

    comm = MeshComm()
    w_in_slabs = with_own(_forward_halves(_gather_chips([big_local["w_in"].astype(BF16)]), "forward_w_in")[0], "w_in")
    d_model = w_in_slabs.shape[1]
    w_in_full = w_in_slabs.transpose(1, 0, 2).reshape(d_model, -1)
    loss, dx, g = _local_step(x[0], loss_target[0], meta_full, lb_logits, mix_norm_w, w_in_full, hg_norm_w, conv_full,
                              gd_a_log, gd_dt_bias, gd_norm_w, ffn_norm_w, final_norm_w, comm)
    loss = lax.psum(loss[0, 0], ("x", "y", "c"))

    parts = dict(zip(comm.early + ["w_in"], comm.early_parts + comm.last_parts))
    arrived = dict(zip(comm.early + ["w_in"], comm.early_arrived + comm.last_arrived))
    g_big = dict(zip(big, _share_halves([_sum_chips(arrived[n], parts[n], "sum_chips_" + n) for n in big])))

    small_shapes = [g[n].shape for n in SMALL]
    g_all = _gather_all(_pack_lanes([g[n] for n in SMALL]), "gather_small")
    g_small = dict(zip(SMALL, _unpack_lanes(_sum_slabs(g_all, "sum_small"), small_shapes)))
    g_small["meta_tokens"] = lax.dynamic_slice_in_dim(g_small["meta_tokens"], shard * meta_cols, meta_cols, axis=1)
    g_small["gd_conv_w"] = lax.dynamic_slice_in_dim(g_small["gd_conv_w"], shard * conv_cols, conv_cols, axis=1)

    grad, delta, new_m, new_v = {}, {}, {}, {}
    for n in big:
        grad[n] = g_big[n].reshape(w[n].shape)
        d_, m_, v_ = _adamw(g_big[n], big_local[n], m[n][0], v[n][0], "adamw_" + n)
        delta[n], new_m[n], new_v[n] = d_.reshape(w[n].shape), m_.reshape(w[n].shape), v_.reshape(w[n].shape)
    local_shapes = [w[n].shape for n in SMALL]
    d_, m_, v_ = _adamw(_pack_lanes([g_small[n] for n in SMALL]), _pack_lanes([w[n] for n in SMALL]),
                        _pack_lanes([m[n] for n in SMALL]), _pack_lanes([v[n] for n in SMALL]), "adamw_small")
    for n, gs_, dd, mm, vv in zip(SMALL, [g_small[n] for n in SMALL], _unpack_lanes(d_, local_shapes), _unpack_lanes(m_, local_shapes),
                                  _unpack_lanes(v_, local_shapes)):
        grad[n], delta[n], new_m[n], new_v[n] = gs_.reshape(w[n].shape), dd, mm, vv

    order = ["meta_tokens", "lb_logits", "mix_norm_w", "w_in", "hg_norm_w", "gd_conv_w", "gd_a_log", "gd_dt_bias", "gd_norm_w",
             "w_branch_a", "w_branch_b", "w_out", "ffn_norm_w", "w_ffn_in", "w_ffn_out", "final_norm_w"]
    return (loss, dx[None], *[grad[n] for n in order], *[delta[n] for n in order], *[new_m[n] for n in order],
            *[new_v[n] for n in order])
```

```python
import functools
import math

import jax
import jax.numpy as jnp
from jax import lax
from jax.experimental import pallas as pl
from jax.experimental.pallas import tpu as pltpu

F32, BF16 = jnp.float32, jnp.bfloat16
EPS = 1e-6
D_MODEL = 2048
N_META = 16
FRONT = 256
CH = 64
SUB = 16
SUBH = 16
DH = 128
NH = 8
HW = NH * DH
CONV_K = 4
RT = 256
VMEM_LIMIT = 56 * 1024 * 1024
ADAM_LR, ADAM_B1, ADAM_B2, ADAM_EPS, ADAM_WD, ADAM_STEP = 0.001, 0.9, 0.999, 1e-08, 0.01, 10

NN = (((1,), (0,)), ((), ()))
NT = (((1,), (1,)), ((), ()))
TN = (((0,), (0,)), ((), ()))


def _dot(a, b, dn=NN):
    return lax.dot_general(a.astype(BF16), b.astype(BF16), dn, preferred_element_type=F32)


def _dotx(a, b, dn=NN):
    return lax.dot_general(a, b, dn, precision=lax.Precision.HIGHEST, preferred_element_type=F32)


class _Side:
    def __init__(self, inputs, out_shapes, scratch, start, wait):
        self.inputs, self.out_shapes, self.scratch, self.start, self.wait = inputs, out_shapes, scratch, start, wait


def _call(body, *, name, grid, in_specs, out_specs, out_shape, scratch=(), sem=None, side=None):
    params = pltpu.CompilerParams(dimension_semantics=sem, vmem_limit_bytes=VMEM_LIMIT)
    if side is None:
        return pl.pallas_call(body, name=name, grid=grid, in_specs=in_specs, out_specs=out_specs, out_shape=out_shape,
                              scratch_shapes=list(scratch), compiler_params=params)
    single = not isinstance(out_specs, (list, tuple))
    out_specs, out_shape = ([out_specs], [out_shape]) if single else (list(out_specs), list(out_shape))
    ni, no, ns = len(in_specs), len(out_specs), len(scratch)
    nsi, nso = len(side.inputs), len(side.out_shapes)
    hbm = pl.BlockSpec(memory_space=pltpu.HBM)

    def wrapped(*refs):
        main_in, side_in = refs[:ni], refs[ni:ni + nsi]
        main_out, side_out = refs[ni + nsi:ni + nsi + no], refs[ni + nsi + no:ni + nsi + no + nso]
        main_scr, side_scr = refs[ni + nsi + no + nso:ni + nsi + no + nso + ns], refs[ni + nsi + no + nso + ns:]
        pids = [pl.program_id(d) for d in range(len(grid))]
        first = functools.reduce(lambda a, b: a & b, [p == 0 for p in pids])
        last = functools.reduce(lambda a, b: a & b, [p == g - 1 for p, g in zip(pids, grid)])

        @pl.when(first)
        def _():
            side.start(side_in, side_out, side_scr)

        body(*main_in, *main_out, *main_scr)

        @pl.when(last)
        def _():
            side.wait(side_in, side_out, side_scr)

    call = pl.pallas_call(wrapped, name=name, grid=grid, in_specs=list(in_specs) + [hbm] * nsi,
                          out_specs=out_specs + [hbm] * nso, out_shape=out_shape + list(side.out_shapes),
                          scratch_shapes=list(scratch) + list(side.scratch), compiler_params=params)

    def run(*args):
        outs = call(*args, *side.inputs)
        main = outs[0] if single else list(outs[:no])
        return main, list(outs[no:])

    return run


def _divmod(j, per):
    if per == 1:
        return j, 0
    return lax.div(j, jnp.int32(per)), lax.rem(j, jnp.int32(per))


def _sds(shape, dtype):
    return jax.ShapeDtypeStruct(tuple(shape), dtype)


def _sigmoid(x):
    return 0.5 * jnp.tanh(0.5 * x) + 0.5


def _silu(x):
    return x * _sigmoid(x)


def _dsilu(x):
    s = _sigmoid(x)
    return s * (1.0 + x * (1.0 - s))


def _tri(n, kind):
    r = lax.broadcasted_iota(jnp.int32, (n, n), 0)
    c = lax.broadcasted_iota(jnp.int32, (n, n), 1)
    return {"incl": r >= c, "strict": r > c, "upper": c >= r}[kind]


def _mm(a, b, mode, out_dtype, tm, tn, tk, name, add=None, n_outer=False, out_shards=0, side=None):
    sharded_a = a.ndim == 3
    if sharded_a:
        n_a = a.shape[2]
        a_shape = (a.shape[1], a.shape[0] * n_a)
    else:
        a_shape = a.shape
    sharded_b = b.ndim == 3
    if sharded_b:
        S, R, n = b.shape
        b_rows, b_cols = R, S * n
    else:
        b_rows, b_cols = b.shape
    if mode == "nn":
        (M, K), N, dn = a_shape, b_cols, NN
    elif mode == "nt":
        (M, K), N, dn = a_shape, b_rows, NT
    else:
        (K, M), N, dn = a_shape, b_cols, TN
    tm, tn, tk = min(tm, M), min(tn, N), min(tk, K)
    if sharded_a:
        tk = min(tk, n_a)
    if sharded_b:
        tn, tk = (min(tn, n), tk) if mode != "nt" else (tn, min(tk, n))
    if out_shards:
        tn = min(tn, N // out_shards)
    assert M % tm == 0 and N % tn == 0 and K % tk == 0, (name, M, N, K, tm, tn, tk)
    nk = K // tk
    a_blk, a_idx = ((tm, tk), lambda i, j, k: (i, k)) if mode != "tn" else ((tk, tm), lambda i, j, k: (k, i))
    if sharded_a:
        per_a = n_a // tk
        assert mode != "tn" and n_a % tk == 0
        a_blk, a_idx = (None, tm, tk), lambda i, j, k: (_divmod(k, per_a)[0], i, _divmod(k, per_a)[1])
    if not sharded_b:
        b_blk, b_idx = ((tk, tn), lambda i, j, k: (k, j)) if mode != "nt" else ((tn, tk), lambda i, j, k: (j, k))
    elif mode != "nt":
        per = n // tn
        assert n % tn == 0
        b_blk, b_idx = (None, tk, tn), lambda i, j, k: (_divmod(j, per)[0], k, _divmod(j, per)[1])
    else:
        per = n // tk
        assert n % tk == 0
        b_blk, b_idx = (None, tn, tk), lambda i, j, k: (_divmod(k, per)[0], j, _divmod(k, per)[1])
    if out_shards:
        per_o = N // out_shards // tn
        assert (N // out_shards) % tn == 0
        o_blk, o_idx = (None, tm, tn), lambda i, j, k: (_divmod(j, per_o)[0], i, _divmod(j, per_o)[1])
        o_shape = (out_shards, M, N // out_shards)
    else:
        o_blk, o_idx, o_shape = (tm, tn), (lambda i, j, k: (i, j)), (M, N)
    c_idx = lambda i, j, k: (i, j)
    if n_outer:
        sw = lambda f: (lambda j, i, k: f(i, j, k))
        a_idx, b_idx, o_idx, c_idx = sw(a_idx), sw(b_idx), sw(o_idx), sw(c_idx)
        grid = (N // tn, M // tm, nk)
    else:
        grid = (M // tm, N // tn, nk)
    has_add = add is not None

    def body(*refs):
        if has_add:
            a_ref, b_ref, c_ref, o_ref, acc_ref = refs
        else:
            a_ref, b_ref, o_ref, acc_ref = refs
            c_ref = None
        part = lax.dot_general(a_ref[...].astype(BF16), b_ref[...].astype(BF16), dn, preferred_element_type=F32)

        def fin(val):
            if has_add:
                val = val + c_ref[...]
            o_ref[...] = val.astype(out_dtype)

        if nk == 1:
            fin(part)
        else:
            k = pl.program_id(2)

            @pl.when(k == 0)
            def _():
                acc_ref[...] = part

            @pl.when(k > 0)
            def _():
                acc_ref[...] += part

            @pl.when(k == nk - 1)
            def _():
                fin(acc_ref[...])

    in_specs = [pl.BlockSpec(a_blk, a_idx), pl.BlockSpec(b_blk, b_idx)]
    args = [a, b]
    if has_add:
        in_specs.append(pl.BlockSpec((tm, tn), c_idx))
        args.append(add)
    acc_shape = (tm, tn) if nk > 1 else (8, 128)
    return _call(body, name=name, grid=grid, in_specs=in_specs, out_specs=pl.BlockSpec(o_blk, o_idx),
                 out_shape=_sds(o_shape, out_dtype), scratch=[pltpu.VMEM(acc_shape, F32)],
                 sem=("arbitrary",) * 3 if side is not None else ("parallel", "parallel", "arbitrary"), side=side)(*args)


def _rms1_fwd(x, meta, w):
    seq, d = x.shape
    nt = (FRONT + seq) // RT

    def body(x_ref, m_ref, w_ref, o_ref):
        i = pl.program_id(0)

        def norm(v):
            r = lax.rsqrt(jnp.mean(v * v, axis=-1, keepdims=True) + EPS)
            return (v * r * w_ref[...]).astype(BF16)

        @pl.when(i == 0)
        def _():
            o_ref[0:RT - N_META, :] = jnp.zeros((RT - N_META, d), BF16)
            o_ref[RT - N_META:RT, :] = norm(m_ref[...])

        @pl.when(i > 0)
        def _():
            o_ref[...] = norm(x_ref[...])

    return _call(body, name="rms1_fwd", grid=(nt,),
                 in_specs=[pl.BlockSpec((RT, d), lambda i: (jnp.maximum(i - 1, 0), 0)),
                           pl.BlockSpec((N_META, d), lambda i: (0, 0)),
                           pl.BlockSpec((1, d), lambda i: (0, 0))],
                 out_specs=pl.BlockSpec((RT, d), lambda i: (i, 0)),
                 out_shape=_sds((FRONT + seq, d), BF16), sem=("parallel",))(x, meta, w)


def _rms1_bwd(x, meta, w, dxn, dh1):
    seq, d = x.shape
    nt = (FRONT + seq) // RT

    def body(x_ref, m_ref, w_ref, g_ref, r_ref, dx_ref, dm_ref, dw_ref):
        i = pl.program_id(0)

        def bwd(v, g):
            r = lax.rsqrt(jnp.mean(v * v, axis=-1, keepdims=True) + EPS)
            vh = v * r
            gh = g * w_ref[...]
            return r * (gh - vh * jnp.mean(gh * vh, axis=-1, keepdims=True)), jnp.sum(g * vh, axis=0, keepdims=True)

        @pl.when(i == 0)
        def _():
            dm, dw = bwd(m_ref[...], g_ref[RT - N_META:RT, :])
            dm_ref[...] = dm
            dw_ref[...] = dw

        @pl.when(i > 0)
        def _():
            dx, dw = bwd(x_ref[...], g_ref[...])
            dx_ref[...] = dx + r_ref[...]
            dw_ref[...] += dw

    xs = pl.BlockSpec((RT, d), lambda i: (jnp.maximum(i - 1, 0), 0))
    return _call(body, name="rms1_bwd", grid=(nt,),
                 in_specs=[xs, pl.BlockSpec((N_META, d), lambda i: (0, 0)), pl.BlockSpec((1, d), lambda i: (0, 0)),
                           pl.BlockSpec((RT, d), lambda i: (i, 0)), xs],
                 out_specs=[xs, pl.BlockSpec((N_META, d), lambda i: (0, 0)), pl.BlockSpec((1, d), lambda i: (0, 0))],
                 out_shape=[_sds((seq, d), F32), _sds((N_META, d), F32), _sds((1, d), F32)],
                 sem=("arbitrary",))(x, meta, w, dxn, dh1)


def _merge_fwd(proj, za, zb):
    seq, d = za.shape
    off = FRONT // RT
    ca, cb = 8 * HW // d, 8 * HW // d + 1

    def body(ga_ref, gb_ref, za_ref, zb_ref, o_ref):
        o_ref[...] = (_sigmoid(ga_ref[...]) * za_ref[...] + _sigmoid(gb_ref[...]) * zb_ref[...]).astype(BF16)

    zs = pl.BlockSpec((RT, d), lambda i: (i, 0))
    return _call(body, name="merge_fwd", grid=(seq // RT,),
                 in_specs=[pl.BlockSpec((RT, d), lambda i: (i + off, ca)), pl.BlockSpec((RT, d), lambda i: (i + off, cb)), zs, zs],
                 out_specs=zs, out_shape=_sds((seq, d), BF16), sem=("parallel",))(proj, proj, za, zb)


def _merge_bwd(proj, za, zb, dmerged):
    seq, d = za.shape
    off = FRONT // RT
    ca, cb = 8 * HW // d, 8 * HW // d + 1
    nt = (FRONT + seq) // RT

    def body(ga_ref, gb_ref, za_ref, zb_ref, dm_ref, dza_ref, dzb_ref, dg_ref):
        i = pl.program_id(0)

        @pl.when(i < off)
        def _():
            dg_ref[...] = jnp.zeros((RT, 2 * d), BF16)

        @pl.when(i >= off)
        def _():
            sa, sb, dm = _sigmoid(ga_ref[...]), _sigmoid(gb_ref[...]), dm_ref[...]
            dza_ref[...] = (dm * sa).astype(BF16)
            dzb_ref[...] = (dm * sb).astype(BF16)
            dg_ref[:, 0:d] = (dm * za_ref[...] * sa * (1.0 - sa)).astype(BF16)
            dg_ref[:, d:2 * d] = (dm * zb_ref[...] * sb * (1.0 - sb)).astype(BF16)

    rs = pl.BlockSpec((RT, d), lambda i: (jnp.maximum(i - off, 0), 0))
    return _call(body, name="merge_bwd", grid=(nt,),
                 in_specs=[pl.BlockSpec((RT, d), lambda i: (i, ca)), pl.BlockSpec((RT, d), lambda i: (i, cb)), rs, rs, rs],
                 out_specs=[rs, rs, pl.BlockSpec((RT, 2 * d), lambda i: (i, 0))],
                 out_shape=[_sds((seq, d), BF16), _sds((seq, d), BF16), _sds((FRONT + seq, 2 * d), BF16)],
                 sem=("arbitrary",))(proj, proj, za, zb, dmerged)


def _resid_norm_fwd(x, mix, w):
    seq, d = x.shape

    def body(x_ref, m_ref, w_ref, h_ref, n_ref):
        h = x_ref[...] + m_ref[...]
        h_ref[...] = h
        r = lax.rsqrt(jnp.mean(h * h, axis=-1, keepdims=True) + EPS)
        n_ref[...] = (h * r * w_ref[...]).astype(BF16)

    rs = pl.BlockSpec((RT, d), lambda i: (i, 0))
    return _call(body, name="resid_norm_fwd", grid=(seq // RT,),
                 in_specs=[rs, rs, pl.BlockSpec((1, d), lambda i: (0, 0))], out_specs=[rs, rs],
                 out_shape=[_sds((seq, d), F32), _sds((seq, d), BF16)], sem=("parallel",))(x, mix, w)


def _resid_norm_bwd(h1, w, dn2, dh2):
    seq, d = h1.shape

    def body(h_ref, w_ref, g_ref, r_ref, o_ref, ob_ref, dw_ref):
        i = pl.program_id(0)
        h, g = h_ref[...], g_ref[...]
        r = lax.rsqrt(jnp.mean(h * h, axis=-1, keepdims=True) + EPS)
        hh = h * r
        gh = g * w_ref[...]
        dh = r_ref[...] + r * (gh - hh * jnp.mean(gh * hh, axis=-1, keepdims=True))
        o_ref[...] = dh
        ob_ref[...] = dh.astype(BF16)
        dw = jnp.sum(g * hh, axis=0, keepdims=True)

        @pl.when(i == 0)
        def _():
            dw_ref[...] = dw

        @pl.when(i > 0)
        def _():
            dw_ref[...] += dw

    rs = pl.BlockSpec((RT, d), lambda i: (i, 0))
    ws = pl.BlockSpec((1, d), lambda i: (0, 0))
    return _call(body, name="resid_norm_bwd", grid=(seq // RT,), in_specs=[rs, ws, rs, rs], out_specs=[rs, rs, ws],
                 out_shape=[_sds((seq, d), F32), _sds((seq, d), BF16), _sds((1, d), F32)], sem=("arbitrary",))(h1, w, dn2, dh2)


def _swiglu_tiles(seq, ff):
    return min(512, seq), (1408 if ff % 1408 == 0 else 512)


def _swiglu_fwd(gu):
    seq, f2 = gu.shape
    ff = f2 // 2
    rt, tc = _swiglu_tiles(seq, ff)
    nb = ff // tc

    def body(g_ref, u_ref, o_ref):
        o_ref[...] = (_silu(g_ref[...].astype(F32)) * u_ref[...].astype(F32)).astype(BF16)

    return _call(body, name="swiglu_fwd", grid=(seq // rt, nb),
                 in_specs=[pl.BlockSpec((rt, tc), lambda i, j: (i, j)), pl.BlockSpec((rt, tc), lambda i, j: (i, j + nb))],
                 out_specs=pl.BlockSpec((rt, tc), lambda i, j: (i, j)), out_shape=_sds((seq, ff), BF16),
                 sem=("parallel", "parallel"))(gu, gu)


def _swiglu_bwd(gu, dact):
    seq, f2 = gu.shape
    ff = f2 // 2
    rt, tc = _swiglu_tiles(seq, ff)
    nb = ff // tc

    def body(g_ref, u_ref, d_ref, o_ref):
        g, d = g_ref[...].astype(F32), d_ref[...].astype(F32)
        o_ref[0] = (d * u_ref[...].astype(F32) * _dsilu(g)).astype(BF16)
        o_ref[1] = (d * _silu(g)).astype(BF16)

    bs = pl.BlockSpec((rt, tc), lambda i, j: (i, j))
    return _call(body, name="swiglu_bwd", grid=(seq // rt, nb),
                 in_specs=[bs, pl.BlockSpec((rt, tc), lambda i, j: (i, j + nb)), bs],
                 out_specs=pl.BlockSpec((2, rt, tc), lambda i, j: (0, i, j)),
                 out_shape=_sds((2, seq, ff), BF16), sem=("parallel", "parallel"))(gu, gu, dact)


def _loss_head(h1, f, w, tgt):
    seq, d = h1.shape
    nt = seq // RT

    def body(h_ref, f_ref, w_ref, t_ref, l_ref, dh_ref, dhb_ref, dw_ref):
        i = pl.program_id(0)
        h = h_ref[...] + f_ref[...]
        r = lax.rsqrt(jnp.mean(h * h, axis=-1, keepdims=True) + EPS)
        hh = h * r
        err = hh * w_ref[...] - t_ref[...]
        l_ref[...] = jnp.full((8, 128), 0.5 * jnp.sum(jnp.mean(err * err, axis=-1, keepdims=True)), F32)
        dy = err * (1.0 / d)
        gh = dy * w_ref[...]
        dh = r * (gh - hh * jnp.mean(gh * hh, axis=-1, keepdims=True))
        dh_ref[...] = dh
        dhb_ref[...] = dh.astype(BF16)
        dw = jnp.sum(dy * hh, axis=0, keepdims=True)

        @pl.when(i == 0)
        def _():
            dw_ref[...] = dw

        @pl.when(i > 0)
        def _():
            dw_ref[...] += dw

    rs = pl.BlockSpec((RT, d), lambda i: (i, 0))
    ws = pl.BlockSpec((1, d), lambda i: (0, 0))
    return _call(body, name="loss_head", grid=(nt,), in_specs=[rs, rs, ws, rs],
                 out_specs=[pl.BlockSpec((8, 128), lambda i: (i, 0)), rs, rs, ws],
                 out_shape=[_sds((nt * 8, 128), F32), _sds((seq, d), F32), _sds((seq, d), BF16), _sds((1, d), F32)],
                 sem=("arbitrary",))(h1, f, w, tgt)


def _sum_tiles(lt):
    n = lt.shape[0]

    def body(l_ref, o_ref):
        v = l_ref[...]
        r = lax.broadcasted_iota(jnp.int32, v.shape, 0)
        c = lax.broadcasted_iota(jnp.int32, v.shape, 1)
        o_ref[...] = jnp.sum(jnp.where((r % 8 == 0) & (c == 0), v, 0.0), keepdims=True)

    return _call(body, name="loss_sum", grid=(1,), in_specs=[pl.BlockSpec((n, 128), lambda i: (0, 0))],
                 out_specs=pl.BlockSpec((1, 1), lambda i: (0, 0)), out_shape=_sds((1, 1), F32))(lt)


def _gated_norm_fwd(o, g, nw):
    r = lax.rsqrt(jnp.mean(o * o, axis=-1, keepdims=True) + EPS)
    return o * r * nw * _silu(g)


def _gated_norm_bwd(o, g, nw, dout):
    r = lax.rsqrt(jnp.mean(o * o, axis=-1, keepdims=True) + EPS)
    oh = o * r
    don = dout * _silu(g)
    dg = dout * (oh * nw) * _dsilu(g)
    dnw = jnp.sum(don * oh, axis=0, keepdims=True)
    doh = don * nw
    return r * (doh - oh * jnp.mean(doh * oh, axis=-1, keepdims=True)), dg, dnw


def _hg_gates(fs, lbl):
    l0, l1 = lbl[0:1, :], lbl[1:2, :]
    m = jnp.maximum(l0, l1)
    e0, e1 = jnp.exp(l0 - m), jnp.exp(l1 - m)
    lb = e0 / (e0 + e1)
    sig = _sigmoid(fs)
    f = lb + (1.0 - lb) * sig
    return lb, sig, f, jnp.log(f), (1.0 - lb) * _sigmoid(-fs)


def _cumsum(w):
    row = lax.broadcasted_iota(jnp.int32, w.shape, 0) & (CH - 1)
    s = 1
    while s < CH:
        w = w + jnp.where(row >= s, pltpu.roll(w, s, 0), 0.0)
        s *= 2
    return w


def _rcumsum(w):
    row = lax.broadcasted_iota(jnp.int32, w.shape, 0) & (CH - 1)
    s = 1
    while s < CH:
        w = w + jnp.where(row < CH - s, pltpu.roll(w, w.shape[0] - s, 0), 0.0)
        s *= 2
    return w


def _decay_blocks(q, k, b, p_ref):
    p_ref[...] = jnp.zeros((CH, CH), F32)
    m16 = _tri(SUBH, "incl")
    for I in range(CH // SUBH):
        s0 = I * SUBH
        bI, qI, kI = b[s0:s0 + SUBH], q[s0:s0 + SUBH], k[s0:s0 + SUBH]
        dec = jnp.exp(jnp.minimum(bI[:, None, :] - bI[None, :, :], 0.0))
        pii = jnp.sum(qI[:, None, :] * kI[None, :, :] * dec, axis=-1)
        p_ref[s0:s0 + SUBH, s0:s0 + SUBH] = jnp.where(m16, pii, 0.0)
        if I > 0:
            rI = b[s0 - 1:s0]
            qs = qI * jnp.exp(bI - rI)
            ks = k[0:s0] * jnp.exp(rI - b[0:s0])
            p_ref[s0:s0 + SUBH, 0:s0] = _dot(qs, ks, NT)


HPS = 8
HPS_HGRN2_FWD = 4
HB = 4
NHB = NH // HPS
OFF = FRONT // RT


def _head_specs(hps, rev=None):
    row = (lambda i: i) if rev is None else rev
    col = lambda g: pl.BlockSpec((RT, hps * DH), lambda h, i: (row(i), g * (NH // hps) + h))
    full = pl.BlockSpec((RT, hps * DH), lambda h, i: (row(i), h))
    real = pl.BlockSpec((RT, hps * DH), lambda h, i: (jnp.maximum(row(i) - OFF, 0), h))
    state = lambda cpt: pl.BlockSpec((hps, cpt, DH, DH), lambda h, i: (h, row(i), 0, 0))
    scal = pl.BlockSpec((hps, RT, DH), lambda h, i: (h, row(i), 0))
    return col, full, real, state, scal


def _hgrn2_fwd(proj, lb_logits, nw):
    tp = proj.shape[0]
    nt, cpt = tp // RT, RT // CH
    hps = HPS_HGRN2_FWD

    def body(q_ref, f_ref, i_ref, g_ref, lbl_ref, nw_ref, og_ref, or_ref, st_ref, s_ref, p_ref):
        @pl.when(pl.program_id(1) == 0)
        def _():
            s_ref[...] = jnp.zeros((hps, DH, DH), F32)

        def chunk(c, carry):
            rows = pl.ds(pl.multiple_of(c * CH, CH), CH)
            for hh in range(hps):
                cols = slice(hh * DH, (hh + 1) * DH)
                _, _, _, w, k = _hg_gates(f_ref[rows, cols], lbl_ref[:, cols])
                q, v = _silu(q_ref[rows, cols]), i_ref[rows, cols]
                b = _cumsum(w)
                st = s_ref[hh]
                st_ref[hh, c] = st
                _decay_blocks(q, k, b, p_ref.at[hh])
                o = _dot(q * jnp.exp(b), st, NT) + _dot(p_ref[hh], v)
                bl = b[CH - 1:CH]
                s_ref[hh] = st * jnp.exp(bl) + _dot(v, k * jnp.exp(bl - b), TN)
                or_ref[rows, cols] = o
                og_ref[rows, cols] = _gated_norm_fwd(o, g_ref[rows, cols], nw_ref[...]).astype(BF16)
            return carry

        lax.fori_loop(0, cpt, chunk, 0)

    col, full, real, state, _ = _head_specs(hps)
    return _call(body, name="hgrn2_fwd", grid=(NH // hps, nt),
                 in_specs=[col(0), col(1), col(2), col(3), pl.BlockSpec((2, hps * DH), lambda h, i: (0, h)),
                           pl.BlockSpec((1, DH), lambda h, i: (0, 0))],
                 out_specs=[real, full, state(cpt)],
                 out_shape=[_sds((tp - FRONT, HW), BF16), _sds((tp, HW), F32), _sds((NH, tp // CH, DH, DH), F32)],
                 scratch=[pltpu.VMEM((hps, DH, DH), F32), pltpu.VMEM((hps, CH, CH), F32)],
                 sem=("parallel", "arbitrary"))(proj, proj, proj, proj, lb_logits, nw)


def _hgrn2_bwd(proj, lb_logits, nw, o_raw, states, dog, side=None):
    tp = proj.shape[0]
    nt, cpt = tp // RT, RT // CH

    def body(q_ref, f_ref, i_ref, g_ref, lbl_ref, nw_ref, or_ref, st_ref, dog_ref,
             dq_ref, df_ref, di_ref, dg_ref, dl_ref, dnw_ref, ds_ref, p_ref, dk_ref, dqa_ref, do_ref):
        step = pl.program_id(1)

        @pl.when(step == 0)
        def _():
            ds_ref[...] = jnp.zeros((HPS, DH, DH), F32)
            dl_ref[...] = jnp.zeros((2, HPS * DH), F32)

        @pl.when((step == 0) & (pl.program_id(0) == 0))
        def _():
            dnw_ref[...] = jnp.zeros((1, DH), F32)

        front = nt - 1 - step < OFF
        for hh in range(HPS):
            cols = slice(hh * DH, (hh + 1) * DH)
            dog_t = jnp.where(front, 0.0, dog_ref[:, cols])
            do_t, dg_t, dnw = _gated_norm_bwd(or_ref[:, cols], g_ref[:, cols], nw_ref[...], dog_t)
            do_ref[:, cols] = do_t
            dg_ref[:, cols] = dg_t.astype(BF16)
            dnw_ref[...] += dnw
        tril = _tri(CH, "incl")
        m16 = _tri(SUBH, "incl")

        def chunk(cc, carry):
            c = cpt - 1 - cc
            rows = pl.ds(pl.multiple_of(c * CH, CH), CH)
            for hh in range(HPS):
                cols = slice(hh * DH, (hh + 1) * DH)
                fs = f_ref[rows, cols]
                lb, sig, f, w, k = _hg_gates(fs, lbl_ref[:, cols])
                hq = q_ref[rows, cols]
                q, v, do = _silu(hq), i_ref[rows, cols], do_ref[rows, cols]
                b = _cumsum(w)
                bl = b[CH - 1:CH]
                eb = jnp.exp(b)
                qs, kd = q * eb, k * jnp.exp(bl - b)
                st, dst = st_ref[hh, c], ds_ref[hh]
                _decay_blocks(q, k, b, p_ref.at[hh])
                dv = _dot(p_ref[hh], do, TN) + _dot(kd, dst, NT)
                dp = jnp.where(tril, _dot(do, v, NT), 0.0)
                dqa, dka = dqa_ref.at[hh], dk_ref.at[hh]
                dqa[...] = eb * _dotx(do, st)
                dka[...] = jnp.exp(bl - b) * _dotx(v, dst)
                for I in range(CH // SUBH):
                    s0 = I * SUBH
                    bI, qI, kI = b[s0:s0 + SUBH], q[s0:s0 + SUBH], k[s0:s0 + SUBH]
                    dec = jnp.exp(jnp.minimum(bI[:, None, :] - bI[None, :, :], 0.0))
                    dpii = jnp.where(m16, dp[s0:s0 + SUBH, s0:s0 + SUBH], 0.0)[:, :, None] * dec
                    dqa[s0:s0 + SUBH, :] += jnp.sum(dpii * kI[None, :, :], axis=1)
                    dka[s0:s0 + SUBH, :] += jnp.sum(dpii * qI[:, None, :], axis=0)
                    if I > 0:
                        rI = b[s0 - 1:s0]
                        eq, ek = jnp.exp(bI - rI), jnp.exp(rI - b[0:s0])
                        dpij = dp[s0:s0 + SUBH, 0:s0]
                        dqa[s0:s0 + SUBH, :] += eq * _dotx(dpij, k[0:s0] * ek)
                        dka[0:s0, :] += ek * _dotx(dpij, qI * eq, TN)
                dq, dk = dqa[...], dka[...]
                st_end = st * jnp.exp(bl) + _dotx(v, kd, TN)
                dw = _rcumsum(q * dq - k * dk) + jnp.sum(dst * st_end, axis=0, keepdims=True)
                ds_ref[hh] = dst * jnp.exp(bl) + _dotx(do, qs, TN)
                one_m = 1.0 - sig
                dq_ref[rows, cols] = (dq * _dsilu(hq)).astype(BF16)
                df_ref[rows, cols] = ((dw / f - dk) * (1.0 - lb) * sig * one_m).astype(BF16)
                di_ref[rows, cols] = dv.astype(BF16)
                dl_ref[0:1, cols] += jnp.sum((dw / f - dk) * one_m, axis=0, keepdims=True)
            return carry

        lax.fori_loop(0, cpt, chunk, 0)

        @pl.when(step == nt - 1)
        def _():
            lbl = lbl_ref[...]
            l0, l1 = lbl[0:1, :], lbl[1:2, :]
            m = jnp.maximum(l0, l1)
            e0, e1 = jnp.exp(l0 - m), jnp.exp(l1 - m)
            p0 = e0 / (e0 + e1)
            dl0 = dl_ref[0:1, :] * p0 * (1.0 - p0)
            dl_ref[0:1, :] = dl0
            dl_ref[1:2, :] = -dl0

    col, full, real, state, _ = _head_specs(HPS, lambda i: nt - 1 - i)
    lbs = pl.BlockSpec((2, HPS * DH), lambda h, i: (0, h))
    return _call(body, name="hgrn2_bwd", grid=(NHB, nt),
                 in_specs=[col(0), col(1), col(2), col(3), lbs, pl.BlockSpec((1, DH), lambda h, i: (0, 0)), full,
                           state(cpt), real],
                 out_specs=[full, full, full, full, lbs, pl.BlockSpec((1, DH), lambda h, i: (0, 0))],
                 out_shape=[_sds((tp, HW), BF16)] * 4 + [_sds((2, HW), F32), _sds((1, DH), F32)],
                 scratch=[pltpu.VMEM((HPS, DH, DH), F32), pltpu.VMEM((HPS, CH, CH), F32), pltpu.VMEM((HPS, CH, DH), F32),
                          pltpu.VMEM((HPS, CH, DH), F32), pltpu.VMEM((RT, HPS * DH), F32)],
                 sem=("arbitrary", "arbitrary"), side=side)(proj, proj, proj, proj, lb_logits, nw, o_raw, states, dog)


GQ0 = 4 * HW
CW = 3 * HW


def _gd_scalars(ab, alog, dtb):
    g = -jnp.exp(alog) * jax.nn.softplus(ab + dtb)
    return g, _sigmoid(ab)


def _conv_ext_specs(row_of):
    main = [pl.BlockSpec((RT, HW), lambda i, g=g: (row_of(i), GQ0 // HW + g)) for g in range(3)]
    prev = [pl.BlockSpec((8, HW), lambda i, g=g: (jnp.maximum(row_of(i) * (RT // 8) - 1, 0), GQ0 // HW + g)) for g in range(3)]
    return main + prev


def _conv_fill(ext_ref, xs, xps, first):
    for g in range(3):
        ext_ref[0:8, g * HW:(g + 1) * HW] = jnp.where(first, 0.0, xps[g][...])
        ext_ref[8:8 + RT, g * HW:(g + 1) * HW] = xs[g][...]


def _conv_apply(ext_ref, cw):
    y = cw[CONV_K - 1:CONV_K, :] * ext_ref[pl.ds(8, RT), :]
    for s in range(1, CONV_K):
        y += cw[CONV_K - 1 - s:CONV_K - s, :] * ext_ref[pl.ds(8 - s, RT), :]
    return y


def _gdn_prep_fwd(proj, pab, conv_w, alog, dtb):
    tp = proj.shape[0]
    nt = tp // RT

    def body(x0, x1, x2, p0, p1, p2, ab_ref, cw_ref, al_ref, dt_ref, q_ref, k_ref, v_ref, g_ref, b_ref, ext_ref):
        _conv_fill(ext_ref, (x0, x1, x2), (p0, p1, p2), pl.program_id(0) == 0)
        a = _silu(_conv_apply(ext_ref, cw_ref[...]))
        for h in range(NH):
            for part, ref, sc in ((0, q_ref, DH ** -0.5), (1, k_ref, 1.0)):
                seg = a[:, part * HW + h * DH:part * HW + (h + 1) * DH]
                ref[:, h * DH:(h + 1) * DH] = seg * (lax.rsqrt(jnp.sum(seg * seg, axis=-1, keepdims=True) + EPS) * sc)
        v_ref[...] = a[:, 2 * HW:3 * HW]
        g, beta = _gd_scalars(ab_ref[...], al_ref[...], dt_ref[...])
        for h in range(NH):
            g_ref[h] = jnp.broadcast_to(g[:, h:h + 1], (RT, DH))
            b_ref[h] = jnp.broadcast_to(beta[:, NH + h:NH + h + 1], (RT, DH))

    hs = pl.BlockSpec((RT, HW), lambda i: (i, 0))
    sc = pl.BlockSpec((NH, RT, DH), lambda i: (0, i, 0))
    one = pl.BlockSpec((1, DH), lambda i: (0, 0))
    return _call(body, name="gdn_prep_fwd", grid=(nt,),
                 in_specs=_conv_ext_specs(lambda i: i) + [pl.BlockSpec((RT, DH), lambda i: (i, 0)),
                                                           pl.BlockSpec((CONV_K, CW), lambda i: (0, 0)), one, one],
                 out_specs=[hs, hs, hs, sc, sc],
                 out_shape=[_sds((tp, HW), F32)] * 3 + [_sds((NH, tp, DH), F32)] * 2,
                 scratch=[pltpu.VMEM((RT + 8, CW), F32)], sem=("parallel",))(*([proj] * 6), pab, conv_w, alog, dtb)


def _gdn_prep_bwd(proj, pab, conv_w, alog, dtb, dq, dk, dv, dgb, dbb):
    tp = proj.shape[0]
    nt = tp // RT

    def body(x0, x1, x2, p0, p1, p2, ab_ref, cw_ref, al_ref, dt_ref, dq_ref, dk_ref, dv_ref, dg_ref, db_ref,
             dx_ref, dab_ref, dcw_ref, dal_ref, ddt_ref, ext_ref, dy_ref):
        step = pl.program_id(0)
        i = nt - 1 - step

        @pl.when(step == 0)
        def _():
            dy_ref[RT:RT + 8, :] = jnp.zeros((8, CW), F32)
            dcw_ref[...] = jnp.zeros((8, CW), F32)
            dal_ref[...] = jnp.zeros((1, DH), F32)
            ddt_ref[...] = jnp.zeros((1, DH), F32)

        _conv_fill(ext_ref, (x0, x1, x2), (p0, p1, p2), i == 0)
        cw = cw_ref[...]
        y = _conv_apply(ext_ref, cw)
        a = _silu(y)
        dsl = _dsilu(y)
        for h in range(NH):
            for part, ref, sc in ((0, dq_ref, DH ** -0.5), (1, dk_ref, 1.0)):
                lo = part * HW + h * DH
                seg = a[:, lo:lo + DH]
                r = lax.rsqrt(jnp.sum(seg * seg, axis=-1, keepdims=True) + EPS)
                xh = seg * r
                dxh = ref[:, h * DH:(h + 1) * DH] * sc
                dy_ref[0:RT, lo:lo + DH] = r * (dxh - xh * jnp.sum(dxh * xh, axis=-1, keepdims=True)) * dsl[:, lo:lo + DH]
        dy_ref[0:RT, 2 * HW:3 * HW] = dv_ref[...] * dsl[:, 2 * HW:3 * HW]
        dy = dy_ref[0:RT, :]
        dx = cw[CONV_K - 1:CONV_K, :] * dy
        dcw_ref[CONV_K - 1:CONV_K, :] += jnp.sum(dy * ext_ref[pl.ds(8, RT), :], axis=0, keepdims=True)
        for s in range(1, CONV_K):
            dx += cw[CONV_K - 1 - s:CONV_K - s, :] * dy_ref[pl.ds(s, RT), :]
            dcw_ref[CONV_K - 1 - s:CONV_K - s, :] += jnp.sum(dy * ext_ref[pl.ds(8 - s, RT), :], axis=0, keepdims=True)
        dx_ref[...] = dx.astype(BF16)
        dy_ref[RT:RT + 8, :] = dy[0:8, :]
        ab = ab_ref[...]
        g, beta = _gd_scalars(ab, al_ref[...], dt_ref[...])
        lane = lax.broadcasted_iota(jnp.int32, (RT, DH), 1)
        dgl = jnp.zeros((RT, DH), F32)
        dbl = jnp.zeros((RT, DH), F32)
        for h in range(NH):
            dgl = jnp.where(lane == h, dg_ref[h], dgl)
            dbl = jnp.where(lane == NH + h, db_ref[h], dbl)
        dsp = dgl * (-jnp.exp(al_ref[...])) * _sigmoid(ab + dt_ref[...])
        dab_ref[...] = (dsp + dbl * beta * (1.0 - beta)).astype(BF16)
        ddt_ref[...] += jnp.sum(dsp, axis=0, keepdims=True)
        dal_ref[...] += jnp.sum(dgl * g, axis=0, keepdims=True)

    hs = pl.BlockSpec((RT, HW), lambda s: (nt - 1 - s, 0))
    sc = pl.BlockSpec((NH, RT, DH), lambda s: (0, nt - 1 - s, 0))
    one = pl.BlockSpec((1, DH), lambda s: (0, 0))
    xs = pl.BlockSpec((RT, CW), lambda s: (nt - 1 - s, 0))
    return _call(body, name="gdn_prep_bwd", grid=(nt,),
                 in_specs=_conv_ext_specs(lambda s: nt - 1 - s) + [
                     pl.BlockSpec((RT, DH), lambda s: (nt - 1 - s, 0)), pl.BlockSpec((CONV_K, CW), lambda s: (0, 0)),
                     one, one, hs, hs, hs, sc, sc],
                 out_specs=[xs, pl.BlockSpec((RT, DH), lambda s: (nt - 1 - s, 0)), pl.BlockSpec((8, CW), lambda s: (0, 0)), one, one],
                 out_shape=[_sds((tp, CW), BF16), _sds((tp, DH), BF16), _sds((8, CW), F32), _sds((1, DH), F32), _sds((1, DH), F32)],
                 scratch=[pltpu.VMEM((RT + 8, CW), F32), pltpu.VMEM((RT + 8, CW), F32)],
                 sem=("arbitrary",))(*([proj] * 6), pab, conv_w, alog, dtb, dq, dk, dv, dgb, dbb)


HS = HB * CH


def _stack_heads(ref, rows, base):
    return jnp.concatenate([ref[rows, (base + hh) * DH:(base + hh + 1) * DH] for hh in range(HB)], axis=0)


def _stack_scal(ref, rows, base):
    return jnp.concatenate([ref[base + hh, rows, :] for hh in range(HB)], axis=0)


def _store_heads(ref, rows, base, val):
    for hh in range(HB):
        ref[rows, (base + hh) * DH:(base + hh + 1) * DH] = val[hh * CH:(hh + 1) * CH].astype(ref.dtype)


def _bd_masks():
    r = lax.broadcasted_iota(jnp.int32, (HS, HS), 0)
    c = lax.broadcasted_iota(jnp.int32, (HS, HS), 1)
    same = lax.shift_right_logical(r, int(math.log2(CH))) == lax.shift_right_logical(c, int(math.log2(CH)))
    return same & (r >= c), same & (r > c)


def _unit_lower_inverse(a):
    n = a.shape[0]
    r = lax.broadcasted_iota(jnp.int32, (n, n), 0)
    c = lax.broadcasted_iota(jnp.int32, (n, n), 1)
    blk_of = lambda t, size: lax.shift_right_logical(t, int(math.log2(size)))
    a16 = jnp.where(blk_of(r, SUB) == blk_of(c, SUB), a, 0.0)
    x = (r == c).astype(F32) - a16
    p = a16
    for _ in range(3):
        p = _dot(p, p)
        x = x + _dot(x, p)
    for blk in (2 * SUB, 4 * SUB):
        off = jnp.where((blk_of(r, blk) == blk_of(c, blk)) & (blk_of(r, blk // 2) != blk_of(c, blk // 2)), a, 0.0)
        x = x - _dot(x, _dot(off, x))
    return x


def _gdn_chunk_common(q, k, v, gl, bt, incl, strict, x=None):
    gc = _cumsum(gl)
    e = jnp.exp(gc)
    rel = jnp.exp(jnp.minimum(gc[:, 0:1] - gc.T[0:1, :], 0.0))
    kb = bt * k
    a = jnp.where(strict, bt[:, 0:1] * _dot(k, k, NT) * rel, 0.0)
    if x is None:
        x = _unit_lower_inverse(a)
    wu = _dot(x, jnp.concatenate([kb * e, bt * v], axis=1))
    attn = jnp.where(incl, _dot(q, k, NT) * rel, 0.0)
    return gc, e, rel, kb, a, x, wu[:, 0:DH], wu[:, DH:2 * DH], attn


def _gdn_fwd(q, k, v, gb, bb, proj, nw):
    tp = q.shape[0]
    nt, cpt = tp // RT, RT // CH

    def body(q_ref, k_ref, v_ref, g_ref, b_ref, z_ref, nw_ref, og_ref, or_ref, st_ref, x_ref, s_ref):
        @pl.when(pl.program_id(1) == 0)
        def _():
            s_ref[...] = jnp.zeros((HPS, DH, DH), F32)

        incl, strict = _bd_masks()

        def chunk(c, carry):
            rows = pl.ds(pl.multiple_of(c * CH, CH), CH)
            for base in range(0, HPS, HB):
                qc, kc, vc = _stack_heads(q_ref, rows, base), _stack_heads(k_ref, rows, base), _stack_heads(v_ref, rows, base)
                gc, e, rel, kb, a, x, w, u, attn = _gdn_chunk_common(qc, kc, vc, _stack_scal(g_ref, rows, base),
                                                                    _stack_scal(b_ref, rows, base), incl, strict)
                x_ref[base // HB, c] = x
                qe = qc * e
                ws, qs = [], []
                for hh in range(HB):
                    blk = slice(hh * CH, (hh + 1) * CH)
                    s = s_ref[base + hh]
                    st_ref[base + hh, c] = s
                    both = _dot(jnp.concatenate([w[blk], qe[blk]], axis=0), s)
                    ws.append(both[0:CH])
                    qs.append(both[CH:2 * CH])
                vn = u - jnp.concatenate(ws, axis=0)
                o = jnp.concatenate(qs, axis=0) + _dot(attn, vn)
                for hh in range(HB):
                    blk = slice(hh * CH, (hh + 1) * CH)
                    gl = gc[(hh + 1) * CH - 1:(hh + 1) * CH]
                    s_ref[base + hh] = s_ref[base + hh] * jnp.exp(gl) + _dot(kc[blk] * jnp.exp(gl - gc[blk]), vn[blk], TN)
                _store_heads(or_ref, rows, base, o)
                for hh in range(HB):
                    cols = slice((base + hh) * DH, (base + hh + 1) * DH)
                    og_ref[rows, cols] = _gated_norm_fwd(o[hh * CH:(hh + 1) * CH], z_ref[rows, cols], nw_ref[...]).astype(BF16)
            return carry

        lax.fori_loop(0, cpt, chunk, 0)

    col, full, real, state, scal = _head_specs(HPS)
    return _call(body, name="gdn_fwd", grid=(NHB, nt),
                 in_specs=[full, full, full, scal, scal, col(7), pl.BlockSpec((1, DH), lambda h, i: (0, 0))],
                 out_specs=[real, full, state(cpt), pl.BlockSpec((HPS // HB, cpt, HS, HS), lambda h, i: (h, i, 0, 0))],
                 out_shape=[_sds((tp - FRONT, HW), BF16), _sds((tp, HW), F32), _sds((NH, tp // CH, DH, DH), F32),
                            _sds((NH // HB, tp // CH, HS, HS), F32)],
                 scratch=[pltpu.VMEM((HPS, DH, DH), F32)], sem=("parallel", "arbitrary"))(q, k, v, gb, bb, proj, nw)


def _gdn_bwd(q, k, v, gb, bb, proj, nw, o_raw, states, xinv, dog):
    tp = q.shape[0]
    nt, cpt = tp // RT, RT // CH

    def body(q_ref, k_ref, v_ref, g_ref, b_ref, z_ref, nw_ref, or_ref, st_ref, x_ref, dog_ref,
             dq_ref, dk_ref, dv_ref, dg_ref, db_ref, dz_ref, dnw_ref, ds_ref, do_ref):
        step = pl.program_id(1)

        @pl.when(step == 0)
        def _():
            ds_ref[...] = jnp.zeros((HPS, DH, DH), F32)

        @pl.when((step == 0) & (pl.program_id(0) == 0))
        def _():
            dnw_ref[...] = jnp.zeros((1, DH), F32)

        front = nt - 1 - step < OFF
        for hh in range(HPS):
            cols = slice(hh * DH, (hh + 1) * DH)
            dog_t = jnp.where(front, 0.0, dog_ref[:, cols])
            do_t, dz_t, dnw = _gated_norm_bwd(or_ref[:, cols], z_ref[:, cols], nw_ref[...], dog_t)
            do_ref[:, cols] = do_t
            dz_ref[:, cols] = dz_t.astype(BF16)
            dnw_ref[...] += dnw
        incl, strict = _bd_masks()
        last_row = (lax.broadcasted_iota(jnp.int32, (CH, 1), 0) == CH - 1)

        def rsum(t):
            return jnp.sum(t, axis=-1, keepdims=True)

        def chunk(cc, carry):
            c = cpt - 1 - cc
            rows = pl.ds(pl.multiple_of(c * CH, CH), CH)
            for base in range(0, HPS, HB):
                qc, kc, vc, do = (_stack_heads(q_ref, rows, base), _stack_heads(k_ref, rows, base), _stack_heads(v_ref, rows, base),
                                  _stack_heads(do_ref, rows, base))
                bt = _stack_scal(b_ref, rows, base)
                gc, e, rel, kb, a, x, w, u, attn = _gdn_chunk_common(qc, kc, vc, _stack_scal(g_ref, rows, base), bt, incl, strict,
                                                                    x=x_ref[base // HB, c])
                qe = qc * e
                heads = [slice(hh * CH, (hh + 1) * CH) for hh in range(HB)]
                gls = [gc[(hh + 1) * CH - 1:(hh + 1) * CH] for hh in range(HB)]
                cdec = jnp.concatenate([jnp.exp(gl - gc[blk]) for gl, blk in zip(gls, heads)], axis=0)
                kcd = kc * cdec
                vn = u - jnp.concatenate([_dot(w[blk], st_ref[base + hh, c]) for hh, blk in enumerate(heads)], axis=0)
                dos = jnp.concatenate([_dot(do[blk], st_ref[base + hh, c], NT) for hh, blk in enumerate(heads)], axis=0)
                kds = jnp.concatenate([_dot(kcd[blk], ds_ref[base + hh]) for hh, blk in enumerate(heads)], axis=0)
                vds = jnp.concatenate([_dot(vn[blk], ds_ref[base + hh], NT) for hh, blk in enumerate(heads)], axis=0)
                dvn = _dot(attn, do, TN) + kds
                dattn = jnp.where(incl, _dot(do, vn, NT), 0.0)
                dar = dattn * rel
                dq = _dot(dar, kc) + e * dos
                dk = _dot(dar, qc, TN) + cdec * vds
                dc = cdec[:, 0:1] * rsum(kc * vds)
                dgc = rsum(qe * dos) - dc
                dw = -jnp.concatenate([_dot(dvn[blk], st_ref[base + hh, c], NT) for hh, blk in enumerate(heads)], axis=0)
                extra = []
                for hh, blk in enumerate(heads):
                    s, dsn = st_ref[base + hh, c], ds_ref[base + hh]
                    el = jnp.exp(gls[hh])
                    dglast = jnp.sum(dc[blk], axis=0, keepdims=True) + el[:, 0:1] * jnp.sum(rsum(dsn * s), axis=0, keepdims=True)
                    extra.append(jnp.where(last_row, dglast, 0.0))
                    ds_ref[base + hh] = dsn * el + _dot(jnp.concatenate([qe[blk], -w[blk]], axis=0),
                                                 jnp.concatenate([do[blk], dvn[blk]], axis=0), TN)
                dr = _dot(x, jnp.concatenate([dw, dvn], axis=1), TN)
                drw, dru = dr[:, 0:DH], dr[:, DH:2 * DH]
                da = -jnp.where(strict, _dot(dr, jnp.concatenate([w, u], axis=1), NT), 0.0)
                dar2 = da * rel
                dkb = _dot(dar2, kc)
                rwk = rsum(drw * kc)
                dk = dk + _dot(dar2, kb, TN) + bt * dkb + (bt * e) * drw
                dbeta = rsum(dkb * kc) + e[:, 0:1] * rwk + rsum(dru * vc)
                z = dattn * attn + da * a
                dgc = dgc + bt[:, 0:1] * e[:, 0:1] * rwk + rsum(z) - rsum(z.T) + jnp.concatenate(extra, axis=0)
                _store_heads(dq_ref, rows, base, dq)
                _store_heads(dk_ref, rows, base, dk)
                _store_heads(dv_ref, rows, base, bt * dru)
                dg = _rcumsum(jnp.broadcast_to(dgc, (HS, DH)))
                dbb = jnp.broadcast_to(dbeta, (HS, DH))
                for hh, blk in enumerate(heads):
                    dg_ref[base + hh, rows, :] = dg[blk]
                    db_ref[base + hh, rows, :] = dbb[blk]
            return carry

        lax.fori_loop(0, cpt, chunk, 0)

    col, full, real, state, scal = _head_specs(HPS, lambda i: nt - 1 - i)
    one = pl.BlockSpec((1, DH), lambda h, i: (0, 0))
    return _call(body, name="gdn_bwd", grid=(NHB, nt),
                 in_specs=[full, full, full, scal, scal, col(7), one, full, state(cpt),
                           pl.BlockSpec((HPS // HB, cpt, HS, HS), lambda h, i: (h, nt - 1 - i, 0, 0)), real],
                 out_specs=[full, full, full, scal, scal, full, one],
                 out_shape=[_sds((tp, HW), F32)] * 3 + [_sds((NH, tp, DH), F32)] * 2 + [_sds((tp, HW), BF16), _sds((1, DH), F32)],
                 scratch=[pltpu.VMEM((HPS, DH, DH), F32), pltpu.VMEM((RT, HPS * DH), F32)],
                 sem=("arbitrary", "arbitrary"))(q, k, v, gb, bb, proj, nw, o_raw, states, xinv, dog)


MAIN_W = 8 * HW
AB_W = 2 * NH


def _split_w_in(w_in):
    main = jnp.concatenate([w_in[:, :MAIN_W], w_in[:, MAIN_W + AB_W:]], axis=1)
    ab = jnp.pad(w_in[:, MAIN_W:MAIN_W + AB_W], ((0, 0), (0, DH - AB_W)))
    return main, ab


def _pad_lanes(v):
    return jnp.pad(v, ((0, 0), (0, DH - v.shape[1])))


class _NoComm:
    def __init__(self, late):
        self.late = late

    def proj_side(self):
        return None

    def late_weights(self, side_outs):
        return self.late

    def early_grads_side(self, grads):
        return None

    def early_grads_done(self, side_outs):
        pass

    def last_grad_side(self, dw_in):
        return None

    def last_grad_done(self, side_outs):
        pass


def _local_step(x, tgt, meta, lb_logits, mix_w, w_in, hg_nw, conv_w, a_log, dt_bias, gd_nw, ffn_nw, final_w, comm):
    w_main, w_ab = _split_w_in(w_in)
    alog, dtb = _pad_lanes(a_log), _pad_lanes(dt_bias)
    final_w = final_w.reshape(1, -1)
    rows4 = lambda t: t.reshape(4, t.shape[0] // 4, t.shape[1])
    xn = _rms1_fwd(x, meta, mix_w)
    side = comm.proj_side()
    proj = _mm(xn, w_main, "nn", F32, 768, 2048, 2048, "proj_main", n_outer=True, side=side)
    proj, landed = proj if side is not None else (proj, None)
    w_a, w_b, w_out, w_ffn_in, w_ffn_out = comm.late_weights(landed)
    pab = _mm(xn, w_ab, "nn", F32, 768, 128, 2048, "proj_ab")
    oa_g, oa_raw, st_a = _hgrn2_fwd(proj, lb_logits, hg_nw)
    q, k, v, gb, bb = _gdn_prep_fwd(proj, pab, conv_w, alog, dtb)
    ob_g, ob_raw, st_b, xinv = _gdn_fwd(q, k, v, gb, bb, proj, gd_nw)
    za = _mm(oa_g, w_a, "nn", F32, 1024, 512, 1024, "branch_a", n_outer=True)
    zb = _mm(ob_g, w_b, "nn", F32, 1024, 512, 1024, "branch_b", n_outer=True)
    merged = _merge_fwd(proj, za, zb)
    mix = _mm(merged, w_out, "nn", F32, 1024, 2048, 2048, "mix_out")
    h1, n2 = _resid_norm_fwd(x, mix, ffn_nw)
    gu = _mm(n2, w_ffn_in, "nn", BF16, 1024, 1408, 2048, "ffn_in", n_outer=True)
    act = _swiglu_fwd(gu)
    f = _mm(act, w_ffn_out, "nn", F32, 1024, 2048, 1408, "ffn_out")
    lt, dh2, dh2b, dfinal = _loss_head(h1, f, final_w, tgt)
    loss = _sum_tiles(lt)
    dact = _mm(dh2b, w_ffn_out, "nt", BF16, 1024, 1408, 2048, "d_act", n_outer=True)
    dw_ffn_out = rows4(_mm(act, dh2b, "tn", F32, 512, 2048, 2048, "dw_ffn_out"))
    dgu = _swiglu_bwd(gu, dact)
    dn2 = _mm(dgu, w_ffn_in, "nt", F32, 1024, 2048, 1408, "d_n2")
    dw_ffn_in = _mm(n2, dgu, "tn", F32, 1024, 1408, 2048, "dw_ffn_in", out_shards=4)
    dh1, dh1b, dffn_nw = _resid_norm_bwd(h1, ffn_nw, dn2, dh2)
    dmerged = _mm(dh1b, w_out, "nt", F32, 1024, 2048, 2048, "d_merged")
    dw_out = _mm(merged, dh1b, "tn", F32, 2048, 1024, 1024, "dw_out")
    dza, dzb, dgate = _merge_bwd(proj, za, zb, dmerged)
    doa = _mm(dza, w_a, "nt", F32, 1024, 1024, 512, "d_oa")
    dob = _mm(dzb, w_b, "nt", F32, 1024, 1024, 512, "d_ob")
    dw_a = _mm(oa_g, dza, "tn", F32, 1024, 512, 1024, "dw_branch_a", out_shards=4)
    dw_b = _mm(ob_g, dzb, "tn", F32, 1024, 512, 1024, "dw_branch_b", out_shards=4)
    early = dict(w_ffn_in=dw_ffn_in, w_ffn_out=dw_ffn_out, w_out=rows4(dw_out), w_branch_a=dw_a, w_branch_b=dw_b)
    side = comm.early_grads_side(early)
    hg = _hgrn2_bwd(proj, lb_logits, hg_nw, oa_raw, st_a, doa, side=side)
    if side is not None:
        hg, arrived = hg
        comm.early_grads_done(arrived)
    dhq, dhf, dhi, dhg, dlbl, dhg_nw = hg
    dq, dk, dv, dg, dbeta, dz, dgd_nw = _gdn_bwd(q, k, v, gb, bb, proj, gd_nw, ob_raw, st_b, xinv, dob)
    dx3, dab, dconv, dalog, ddtb = _gdn_prep_bwd(proj, pab, conv_w, alog, dtb, dq, dk, dv, dg, dbeta)
    dproj = jnp.concatenate([dhq, dhf, dhi, dhg, dx3, dz, dgate], axis=1)
    dw_main = _mm(xn, dproj, "tn", F32, 1024, 1024, 2816, "dw_in_main")
    dw_ab = _mm(xn, dab, "tn", F32, 2048, 128, 768, "dw_in_ab")
    d_model = dw_main.shape[0]
    dw_in = jnp.concatenate([dw_main[:, :MAIN_W], dw_ab[:, :AB_W], dw_main[:, MAIN_W:]], axis=1)
    dw_in = dw_in.reshape(d_model, 4, -1).transpose(1, 0, 2)
    side = comm.last_grad_side(dw_in)
    dxn = _mm(dproj, w_main, "nt", F32, 768, 2048, 2048, "d_xn", side=side)
    if side is not None:
        dxn, arrived = dxn
        comm.last_grad_done(arrived)
    dxn = _mm(dab, w_ab, "nt", F32, 768, 2048, 128, "d_xn_ab", add=dxn)
    dx, dmeta, dmix_w = _rms1_bwd(x, meta, mix_w, dxn, dh1)
    grads = dict(meta_tokens=dmeta, lb_logits=dlbl, mix_norm_w=dmix_w, w_in=dw_in,
                 hg_norm_w=dhg_nw, gd_conv_w=dconv[:CONV_K], gd_a_log=dalog[:, :NH],
                 gd_dt_bias=ddtb[:, :NH], gd_norm_w=dgd_nw, w_branch_a=dw_a, w_branch_b=dw_b,
                 w_out=rows4(dw_out), ffn_norm_w=dffn_nw, w_ffn_in=dw_ffn_in, w_ffn_out=dw_ffn_out,
                 final_norm_w=dfinal.reshape(-1))
    return loss, dx, grads


def _adamw(g, w, m, v, name):
    rows, cols = g.shape
    tr = rows
    for cand in (128, 64, 32, 16, 8):
        if rows % cand == 0 and rows > cand:
            tr = cand
            break

    def body(g_ref, w_ref, m_ref, v_ref, go_ref, d_ref, nm_ref, nv_ref):
        gg = g_ref[...]
        go_ref[...] = gg
        nm = ADAM_B1 * m_ref[...] + (1.0 - ADAM_B1) * gg
        nv = ADAM_B2 * v_ref[...] + (1.0 - ADAM_B2) * (gg * gg)
        m_hat = nm / (1.0 - ADAM_B1 ** ADAM_STEP)
        v_hat = nv / (1.0 - ADAM_B2 ** ADAM_STEP)
        d_ref[...] = -ADAM_LR * (m_hat / (jnp.sqrt(v_hat) + ADAM_EPS) + ADAM_WD * w_ref[...])
        nm_ref[...] = nm
        nv_ref[...] = nv

    bs = pl.BlockSpec((tr, cols), lambda i: (i, 0))
    return _call(body, name=name, grid=(rows // tr,), in_specs=[bs] * 4, out_specs=[bs] * 4,
                 out_shape=[_sds((rows, cols), F32)] * 4, sem=("parallel",))(g, w, m, v)


HBM = pl.BlockSpec(memory_space=pltpu.HBM)
MESH = pl.DeviceIdType.MESH


def _place():
    x, y, c = lax.axis_index("x"), lax.axis_index("y"), lax.axis_index("c")
    return x, y, c, [(1 - x, y), (x, 1 - y), (1 - x, 1 - y)]


def _comm_call(body, name, out_shape, n_in, scratch):
    return pl.pallas_call(body, name=name, out_shape=out_shape, in_specs=[HBM] * n_in,
                          out_specs=jax.tree.map(lambda _: HBM, out_shape), scratch_shapes=scratch)


def _half_rows(rows, c, tile):
    hh = rows // 2
    assert rows % 2 == 0 and hh % tile == 0, (rows, tile)
    return pl.ds(pl.multiple_of(c * hh, tile), hh)


def _gather_copies(w_refs, out_refs, sems):
    send_sems, recv_sems = sems
    x, y, c, chips = _place()
    s_me = 2 * x + y
    sends, recvs = [], []
    for k, (w_ref, out_ref) in enumerate(zip(w_refs, out_refs)):
        half = _half_rows(w_ref.shape[0], c, 16)
        for j, (cx, cy) in enumerate(chips):
            sem = dict(send_sem=send_sems.at[3 * k + j], recv_sem=recv_sems.at[3 * k + j], device_id=(cx, cy, c), device_id_type=MESH)
            sends.append(pltpu.make_async_remote_copy(src_ref=w_ref.at[half], dst_ref=out_ref.at[s_me, half], **sem))
            recvs.append(pltpu.make_async_remote_copy(src_ref=w_ref.at[half], dst_ref=out_ref.at[2 * cx + cy, half], **sem))
    return sends, recvs


def _gather_sems(n):
    return [pltpu.SemaphoreType.DMA((3 * n,)), pltpu.SemaphoreType.DMA((3 * n,))]


def _gather_start(w_refs, out_refs, sems):
    for cp in _gather_copies(w_refs, out_refs, sems)[0]:
        cp.start()


def _gather_wait(w_refs, out_refs, sems):
    sends, recvs = _gather_copies(w_refs, out_refs, sems)
    for cp in recvs:
        cp.wait_recv()
    for cp in sends:
        cp.wait_send()


def _gather_chips(shards):
    n = len(shards)

    def body(*refs):
        _gather_start(refs[:n], refs[n:2 * n], refs[2 * n:])
        _gather_wait(refs[:n], refs[n:2 * n], refs[2 * n:])

    return _comm_call(body, "gather_chips", [_sds((4,) + w.shape, w.dtype) for w in shards], n, _gather_sems(n))(*shards)


def _gather_side(shards):
    return _Side(shards, [_sds((4,) + w.shape, w.dtype) for w in shards], _gather_sems(len(shards)), _gather_start, _gather_wait)


def _forward_halves(outs, name):
    n = len(outs)

    def body(*refs):
        out_refs = refs[n:2 * n]
        send_sems, recv_sems = refs[2 * n:]
        x, y, c, chips = _place()
        cps = []
        for k in range(n):
            rows = out_refs[k].shape[1]
            half, other = _half_rows(rows, c, 16), _half_rows(rows, 1 - c, 16)
            for j, (cx, cy) in enumerate(chips):
                sem = dict(send_sem=send_sems.at[3 * k + j], recv_sem=recv_sems.at[3 * k + j], device_id=(x, y, 1 - c), device_id_type=MESH)
                landed = out_refs[k].at[2 * cx + cy, half]
                cps.append(pltpu.make_async_remote_copy(src_ref=landed, dst_ref=landed, **sem))
                cps[-1].start()
        for k in range(n):
            rows = out_refs[k].shape[1]
            half, other = _half_rows(rows, c, 16), _half_rows(rows, 1 - c, 16)
            for j, (cx, cy) in enumerate(chips):
                sem = dict(send_sem=send_sems.at[3 * k + j], recv_sem=recv_sems.at[3 * k + j], device_id=(x, y, 1 - c), device_id_type=MESH)
                pltpu.make_async_remote_copy(src_ref=out_refs[k].at[2 * cx + cy, half], dst_ref=out_refs[k].at[2 * cx + cy, other], **sem).wait_recv()
        for cp in cps:
            cp.wait_send()

    shapes = [_sds(o.shape, o.dtype) for o in outs]
    return pl.pallas_call(body, name=name, out_shape=shapes, in_specs=[HBM] * n, out_specs=[HBM] * n,
                          input_output_aliases={k: k for k in range(n)},
                          scratch_shapes=[pltpu.SemaphoreType.DMA((3 * n,)), pltpu.SemaphoreType.DMA((3 * n,))])(*outs)


def _swap_halves(gs, name):
    n = len(gs)

    def body(*refs):
        g_refs, out_refs = refs[:n], refs[n:2 * n]
        send_sems, recv_sems = refs[2 * n:]
        x, y, c, _ = _place()
        cps = []
        for k in range(n):
            other = _half_rows(g_refs[k].shape[1], 1 - c, 8)
            cps.append(pltpu.make_async_remote_copy(src_ref=g_refs[k].at[:, other, :], dst_ref=out_refs[k], send_sem=send_sems.at[k],
                                                    recv_sem=recv_sems.at[k], device_id=(x, y, 1 - c), device_id_type=MESH))
            cps[-1].start()
        for cp in cps:
            cp.wait()

    return _comm_call(body, name, [_sds((4, g.shape[1] // 2, g.shape[2]), g.dtype) for g in gs], n,
                      [pltpu.SemaphoreType.DMA((n,)), pltpu.SemaphoreType.DMA((n,))])(*gs)


def _row_tile(rows, row_bytes, budget=3 << 20):
    if rows * row_bytes <= budget:
        return rows
    return max(t for t in range(16, rows, 16) if rows % t == 0 and t * row_bytes <= budget)


def _add_half(g, got, c, name):
    _, rows, cols = g.shape
    hh = rows // 2
    tr = _row_tile(hh, cols * 4)
    nb = hh // tr

    def body(c_ref, a_ref, b_ref, o_ref):
        o_ref[...] = (a_ref[...] + b_ref[...]).astype(BF16)

    gs = pltpu.PrefetchScalarGridSpec(
        num_scalar_prefetch=1, grid=(4, nb),
        in_specs=[pl.BlockSpec((1, tr, cols), lambda s, i, c_ref: (s, c_ref[0] * nb + i, 0)),
                  pl.BlockSpec((1, tr, cols), lambda s, i, c_ref: (s, i, 0))],
        out_specs=pl.BlockSpec((1, tr, cols), lambda s, i, c_ref: (s, i, 0)))
    return pl.pallas_call(body, name=name, grid_spec=gs, out_shape=_sds((4, hh, cols), BF16),
                          compiler_params=pltpu.CompilerParams(dimension_semantics=("parallel", "parallel"),
                                                               vmem_limit_bytes=VMEM_LIMIT))(c, g, got)


def _scatter_copies(p_refs, out_refs, sems):
    send_sems, recv_sems = sems
    x, y, c, chips = _place()
    s_me = 2 * x + y
    cps = []
    for k, (p_ref, out_ref) in enumerate(zip(p_refs, out_refs)):
        for j, (cx, cy) in enumerate(chips):
            cps.append(pltpu.make_async_remote_copy(src_ref=p_ref.at[2 * cx + cy], dst_ref=out_ref.at[s_me],
                                                    send_sem=send_sems.at[3 * k + j], recv_sem=recv_sems.at[3 * k + j],
                                                    device_id=(cx, cy, c), device_id_type=MESH))
    return cps


def _scatter_start(p_refs, out_refs, sems):
    for cp in _scatter_copies(p_refs, out_refs, sems):
        cp.start()


def _scatter_wait(p_refs, out_refs, sems):
    for cp in _scatter_copies(p_refs, out_refs, sems):
        cp.wait()


def _scatter_chips(ps):
    n = len(ps)

    def body(*refs):
        _scatter_start(refs[:n], refs[n:2 * n], refs[2 * n:])
        _scatter_wait(refs[:n], refs[n:2 * n], refs[2 * n:])

    return _comm_call(body, "scatter_chips", [_sds(p_.shape, p_.dtype) for p_ in ps], n, _gather_sems(n))(*ps)


def _scatter_side(ps):
    return _Side(ps, [_sds(p_.shape, p_.dtype) for p_ in ps], _gather_sems(len(ps)), _scatter_start, _scatter_wait)


def _sum_slabs(b, name):
    n, h, wd = b.shape
    tr = _row_tile(h, n * wd * 4, 6 << 20)

    def body(b_ref, o_ref):
        acc = b_ref[0]
        for s in range(1, n):
            acc = acc + b_ref[s]
        o_ref[...] = acc

    return _call(body, name=name, grid=(h // tr,), in_specs=[pl.BlockSpec((n, tr, wd), lambda i: (0, i, 0))],
                 out_specs=pl.BlockSpec((tr, wd), lambda i: (i, 0)), out_shape=_sds((h, wd), F32), sem=("parallel",))(b)


def _sum_chips(arrived, own, name):
    n, h, wd = arrived.shape
    tr = _row_tile(h, n * wd * 2, 6 << 20)
    nb = h // tr
    my_chip = lambda: 2 * lax.axis_index("x") + lax.axis_index("y")

    def body(b_ref, p_ref, o_ref):
        acc = None
        for s in range(n):
            term = jnp.where(my_chip() == s, p_ref[0], b_ref[s]).astype(F32)
            acc = term if acc is None else acc + term
        o_ref[...] = acc

    return _call(body, name=name, grid=(nb,),
                 in_specs=[pl.BlockSpec((n, tr, wd), lambda i: (0, i, 0)), pl.BlockSpec((1, tr, wd), lambda i: (my_chip(), i, 0))],
                 out_specs=pl.BlockSpec((tr, wd), lambda i: (lax.axis_index("c") * nb + i, 0)),
                 out_shape=_sds((2 * h, wd), F32), sem=("parallel",))(arrived, own)


def _share_halves(gs):
    n = len(gs)

    def body(*refs):
        out_refs = refs[n:2 * n]
        send_sems, recv_sems = refs[2 * n:]
        x, y, c, _ = _place()
        cps = []
        for k in range(n):
            half, other = _half_rows(out_refs[k].shape[0], c, 8), _half_rows(out_refs[k].shape[0], 1 - c, 8)
            sem = dict(send_sem=send_sems.at[k], recv_sem=recv_sems.at[k], device_id=(x, y, 1 - c), device_id_type=MESH)
            cps.append((pltpu.make_async_remote_copy(src_ref=out_refs[k].at[half], dst_ref=out_refs[k].at[half], **sem),
                        pltpu.make_async_remote_copy(src_ref=out_refs[k].at[half], dst_ref=out_refs[k].at[other], **sem)))
            cps[-1][0].start()
        for send, recv in cps:
            recv.wait_recv()
            send.wait_send()

    return pl.pallas_call(body, name="share_halves", out_shape=[_sds(g.shape, g.dtype) for g in gs], in_specs=[HBM] * n,
                          out_specs=[HBM] * n, input_output_aliases={k: k for k in range(n)},
                          scratch_shapes=[pltpu.SemaphoreType.DMA((n,)), pltpu.SemaphoreType.DMA((n,))])(*gs)


def _gather_all(v, name):
    def body(v_ref, out_ref, send_sems, recv_sems, local_sem):
        x, y, c = lax.axis_index("x"), lax.axis_index("y"), lax.axis_index("c")
        me = 4 * x + 2 * y + c
        flip = lambda t, d: 1 - t if d else t
        mine = pltpu.make_async_copy(v_ref, out_ref.at[me], local_sem)
        mine.start()
        cps = []
        for k in range(1, 8):
            to = (flip(x, k & 4), flip(y, k & 2), flip(c, k & 1))
            cps.append(pltpu.make_async_remote_copy(src_ref=v_ref, dst_ref=out_ref.at[me], send_sem=send_sems.at[k - 1],
                                                    recv_sem=recv_sems.at[k - 1], device_id=to, device_id_type=MESH))
        for cp in cps:
            cp.start()
        for cp in cps:
            cp.wait()
        mine.wait()

    return _comm_call(body, name, _sds((8,) + v.shape, v.dtype), 1,
                      [pltpu.SemaphoreType.DMA((7,)), pltpu.SemaphoreType.DMA((7,)), pltpu.SemaphoreType.DMA])(v)


BIG = (("w_in", 1), ("w_branch_a", 1), ("w_branch_b", 1), ("w_out", 0), ("w_ffn_in", 1), ("w_ffn_out", 0))
SMALL = ("meta_tokens", "lb_logits", "mix_norm_w", "hg_norm_w", "gd_conv_w", "gd_a_log", "gd_dt_bias", "gd_norm_w",
         "ffn_norm_w", "final_norm_w")


def _pack_lanes(parts):
    rows = []
    for p in parts:
        f = p.reshape(-1).astype(F32)
        n = -(-f.shape[0] // DH) * DH
        rows.append(jnp.pad(f, (0, n - f.shape[0])).reshape(-1, DH))
    buf = jnp.concatenate(rows, axis=0)
    return jnp.pad(buf, ((0, -buf.shape[0] % 8), (0, 0)))


def _unpack_lanes(buf, shapes):
    out, off = [], 0
    for shp in shapes:
        n = math.prod(shp)
        r = -(-n // DH)
        out.append(buf[off:off + r].reshape(-1)[:n].reshape(shp))
        off += r
    return out


def kernel(x, meta_tokens, lb_logits, mix_norm_w, w_in, hg_norm_w, gd_conv_w, gd_a_log, gd_dt_bias, gd_norm_w, w_branch_a, w_branch_b, w_out, ffn_norm_w, w_ffn_in, w_ffn_out, final_norm_w, loss_target, m_meta_tokens, m_lb_logits, m_mix_norm_w, m_w_in, m_hg_norm_w, m_gd_conv_w, m_gd_a_log, m_gd_dt_bias, m_gd_norm_w, m_w_branch_a, m_w_branch_b, m_w_out, m_ffn_norm_w, m_w_ffn_in, m_w_ffn_out, m_final_norm_w, v_meta_tokens, v_lb_logits, v_mix_norm_w, v_w_in, v_hg_norm_w, v_gd_conv_w, v_gd_a_log, v_gd_dt_bias, v_gd_norm_w, v_w_branch_a, v_w_branch_b, v_w_out, v_ffn_norm_w, v_w_ffn_in, v_w_ffn_out, v_final_norm_w):
    args = dict(locals())
    big = [n for n, _ in BIG]
    w = {n: args[n] for n in SMALL + tuple(big)}
    m = {n: args["m_" + n] for n in w}
    v = {n: args["v_" + n] for n in w}
    xi, yi, ci = lax.axis_index("x"), lax.axis_index("y"), lax.axis_index("c")
    shard = 2 * xi + yi
    big_local = {n: w[n][0] for n in big}

    meta_cols, conv_cols = meta_tokens.shape[1], gd_conv_w.shape[-1]
    sm_all = _gather_all(_pack_lanes([meta_tokens, gd_conv_w[0]]), "gather_meta")
    sm_parts = [_unpack_lanes(sm_all[2 * s], [meta_tokens.shape, gd_conv_w[0].shape]) for s in range(4)]
    meta_full = jnp.concatenate([p[0] for p in sm_parts], axis=1)
    conv_full = jnp.concatenate([p[1] for p in sm_parts], axis=1)
    cvec = ci.reshape(1).astype(jnp.int32)
    late = [n for n in big if n != "w_in"]
    rows_full = lambda t: t.reshape(t.shape[0] * t.shape[1], t.shape[2])

    def pair_sums(names, gs):
        return [_add_half(gk, got, cvec, "add_half_" + n) for n, gk, got in zip(names, gs, _swap_halves(gs, "swap_" + names[0]))]

    def with_own(slabs, n):
        return lax.dynamic_update_index_in_dim(slabs, big_local[n].astype(BF16), shard, 0)

    class MeshComm:
        def proj_side(self):
            return _gather_side([big_local[n].astype(BF16) for n in late])

        def late_weights(self, landed):
            wl = {n: with_own(t, n) for n, t in zip(late, _forward_halves(landed, "forward_late"))}
            return (wl["w_branch_a"], wl["w_branch_b"], rows_full(wl["w_out"]), wl["w_ffn_in"], rows_full(wl["w_ffn_out"]))

        def early_grads_side(self, grads):
            self.early = list(grads)
            self.early_parts = pair_sums(self.early, [grads[n] for n in self.early])
            return _scatter_side(self.early_parts)

        def early_grads_done(self, arrived):
            self.early_arrived = arrived

        def last_grad_side(self, dw_in):
            self.last_parts = pair_sums(["w_in"], [dw_in])
            return _scatter_side(self.last_parts)

        def last_grad_done(self, arrived):
            self.last_arrived = arrived

    comm = MeshComm()
    w_in_slabs = with_own(_forward_halves(_gather_chips([big_local["w_in"].astype(BF16)]), "forward_w_in")[0], "w_in")
    d_model = w_in_slabs.shape[1]
    w_in_full = w_in_slabs.transpose(1, 0, 2).reshape(d_model, -1)
    loss, dx, g = _local_step(x[0], loss_target[0], meta_full, lb_logits, mix_norm_w, w_in_full, hg_norm_w, conv_full,
                              gd_a_log, gd_dt_bias, gd_norm_w, ffn_norm_w, final_norm_w, comm)
    loss = lax.psum(loss[0, 0], ("x", "y", "c"))

    parts = dict(zip(comm.early + ["w_in"], comm.early_parts + comm.last_parts))
    arrived = dict(zip(comm.early + ["w_in"], comm.early_arrived + comm.last_arrived))
    g_big = dict(zip(big, _share_halves([_sum_chips(arrived[n], parts[n], "sum_chips_" + n) for n in big])))

    small_shapes = [g[n].shape for n in SMALL]
    g_all = _gather_all(_pack_lanes([g[n] for n in SMALL]), "gather_small")
    g_small = dict(zip(SMALL, _unpack_lanes(_sum_slabs(g_all, "sum_small"), small_shapes)))
    g_small["meta_tokens"] = lax.dynamic_slice_in_dim(g_small["meta_tokens"], shard * meta_cols, meta_cols, axis=1)
    g_small["gd_conv_w"] = lax.dynamic_slice_in_dim(g_small["gd_conv_w"], shard * conv_cols, conv_cols, axis=1)

    grad, delta, new_m, new_v = {}, {}, {}, {}
    for n in big:
        g_, d_, m_, v_ = _adamw(g_big[n], big_local[n], m[n][0], v[n][0], "adamw_" + n)
        grad[n] = g_.reshape(w[n].shape)
        delta[n], new_m[n], new_v[n] = d_.reshape(w[n].shape), m_.reshape(w[n].shape), v_.reshape(w[n].shape)
    local_shapes = [w[n].shape for n in SMALL]
    _, d_, m_, v_ = _adamw(_pack_lanes([g_small[n] for n in SMALL]), _pack_lanes([w[n] for n in SMALL]),
                           _pack_lanes([m[n] for n in SMALL]), _pack_lanes([v[n] for n in SMALL]), "adamw_small")
    for n, gs_, dd, mm, vv in zip(SMALL, [g_small[n] for n in SMALL], _unpack_lanes(d_, local_shapes), _unpack_lanes(m_, local_shapes),
                                  _unpack_lanes(v_, local_shapes)):
        grad[n], delta[n], new_m[n], new_v[n] = gs_.reshape(w[n].shape), dd, mm, vv

    order = ["meta_tokens", "lb_logits", "mix_norm_w", "w_in", "hg_norm_w", "gd_conv_w", "gd_a_log", "gd_dt_bias", "gd_norm_w",
             "w_branch_a", "w_branch_b", "w_out", "ffn_norm_w", "w_ffn_in", "w_ffn_out", "final_norm_w"]
    return (loss, dx[None], *[grad[n] for n in order], *[delta[n] for n in order], *[new_m[n] for n in order],
            *[new_v[n] for n in order])
```

```python
import functools
import math

import jax
import jax.numpy as jnp
from jax import lax
from jax.experimental import pallas as pl
from jax.experimental.pallas import tpu as pltpu

F32, BF16 = jnp.float32, jnp.bfloat16
EPS = 1e-6
D_MODEL = 2048
N_META = 16
FRONT = 256
CH = 64
SUB = 16
SUBH = 16
DH = 128
NH = 8
HW = NH * DH
CONV_K = 4
RT = 256
VMEM_LIMIT = 56 * 1024 * 1024
ADAM_LR, ADAM_B1, ADAM_B2, ADAM_EPS, ADAM_WD, ADAM_STEP = 0.001, 0.9, 0.999, 1e-08, 0.01, 10

NN = (((1,), (0,)), ((), ()))
NT = (((1,), (1,)), ((), ()))
TN = (((0,), (0,)), ((), ()))


def _dot(a, b, dn=NN):
    return lax.dot_general(a.astype(BF16), b.astype(BF16), dn, preferred_element_type=F32)


def _dotx(a, b, dn=NN):
    return lax.dot_general(a, b, dn, precision=lax.Precision.HIGHEST, preferred_element_type=F32)


class _Side:
    def __init__(self, inputs, out_shapes, scratch, start, wait):
        self.inputs, self.out_shapes, self.scratch, self.start, self.wait = inputs, out_shapes, scratch, start, wait


def _call(body, *, name, grid, in_specs, out_specs, out_shape, scratch=(), sem=None, side=None):
    params = pltpu.CompilerParams(dimension_semantics=sem, vmem_limit_bytes=VMEM_LIMIT)
    if side is None:
        return pl.pallas_call(body, name=name, grid=grid, in_specs=in_specs, out_specs=out_specs, out_shape=out_shape,
                              scratch_shapes=list(scratch), compiler_params=params)
    single = not isinstance(out_specs, (list, tuple))
    out_specs, out_shape = ([out_specs], [out_shape]) if single else (list(out_specs), list(out_shape))
    ni, no, ns = len(in_specs), len(out_specs), len(scratch)
    nsi, nso = len(side.inputs), len(side.out_shapes)
    hbm = pl.BlockSpec(memory_space=pltpu.HBM)

    def wrapped(*refs):
        main_in, side_in = refs[:ni], refs[ni:ni + nsi]
        main_out, side_out = refs[ni + nsi:ni + nsi + no], refs[ni + nsi + no:ni + nsi + no + nso]
        main_scr, side_scr = refs[ni + nsi + no + nso:ni + nsi + no + nso + ns], refs[ni + nsi + no + nso + ns:]
        pids = [pl.program_id(d) for d in range(len(grid))]
        first = functools.reduce(lambda a, b: a & b, [p == 0 for p in pids])
        last = functools.reduce(lambda a, b: a & b, [p == g - 1 for p, g in zip(pids, grid)])

        @pl.when(first)
        def _():
            side.start(side_in, side_out, side_scr)

        body(*main_in, *main_out, *main_scr)

        @pl.when(last)
        def _():
            side.wait(side_in, side_out, side_scr)

    call = pl.pallas_call(wrapped, name=name, grid=grid, in_specs=list(in_specs) + [hbm] * nsi,
                          out_specs=out_specs + [hbm] * nso, out_shape=out_shape + list(side.out_shapes),
                          scratch_shapes=list(scratch) + list(side.scratch), compiler_params=params)

    def run(*args):
        outs = call(*args, *side.inputs)
        main = outs[0] if single else list(outs[:no])
        return main, list(outs[no:])

    return run


def _divmod(j, per):
    if per == 1:
        return j, 0
    return lax.div(j, jnp.int32(per)), lax.rem(j, jnp.int32(per))


def _sds(shape, dtype):
    return jax.ShapeDtypeStruct(tuple(shape), dtype)


def _sigmoid(x):
    return 0.5 * jnp.tanh(0.5 * x) + 0.5


def _silu(x):
    return x * _sigmoid(x)


def _dsilu(x):
    s = _sigmoid(x)
    return s * (1.0 + x * (1.0 - s))


def _tri(n, kind):
    r = lax.broadcasted_iota(jnp.int32, (n, n), 0)
    c = lax.broadcasted_iota(jnp.int32, (n, n), 1)
    return {"incl": r >= c, "strict": r > c, "upper": c >= r}[kind]


def _mm(a, b, mode, out_dtype, tm, tn, tk, name, add=None, n_outer=False, out_shards=0, side=None):
    sharded_a = a.ndim == 3
    if sharded_a:
        n_a = a.shape[2]
        a_shape = (a.shape[1], a.shape[0] * n_a)
    else:
        a_shape = a.shape
    sharded_b = b.ndim == 3
    if sharded_b:
        S, R, n = b.shape
        b_rows, b_cols = R, S * n
    else:
        b_rows, b_cols = b.shape
    if mode == "nn":
        (M, K), N, dn = a_shape, b_cols, NN
    elif mode == "nt":
        (M, K), N, dn = a_shape, b_rows, NT
    else:
        (K, M), N, dn = a_shape, b_cols, TN
    tm, tn, tk = min(tm, M), min(tn, N), min(tk, K)
    if sharded_a:
        tk = min(tk, n_a)
    if sharded_b:
        tn, tk = (min(tn, n), tk) if mode != "nt" else (tn, min(tk, n))
    if out_shards:
        tn = min(tn, N // out_shards)
    assert M % tm == 0 and N % tn == 0 and K % tk == 0, (name, M, N, K, tm, tn, tk)
    nk = K // tk
    a_blk, a_idx = ((tm, tk), lambda i, j, k: (i, k)) if mode != "tn" else ((tk, tm), lambda i, j, k: (k, i))
    if sharded_a:
        per_a = n_a // tk
        assert mode != "tn" and n_a % tk == 0
        a_blk, a_idx = (None, tm, tk), lambda i, j, k: (_divmod(k, per_a)[0], i, _divmod(k, per_a)[1])
    if not sharded_b:
        b_blk, b_idx = ((tk, tn), lambda i, j, k: (k, j)) if mode != "nt" else ((tn, tk), lambda i, j, k: (j, k))
    elif mode != "nt":
        per = n // tn
        assert n % tn == 0
        b_blk, b_idx = (None, tk, tn), lambda i, j, k: (_divmod(j, per)[0], k, _divmod(j, per)[1])
    else:
        per = n // tk
        assert n % tk == 0
        b_blk, b_idx = (None, tn, tk), lambda i, j, k: (_divmod(k, per)[0], j, _divmod(k, per)[1])
    if out_shards:
        per_o = N // out_shards // tn
        assert (N // out_shards) % tn == 0
        o_blk, o_idx = (None, tm, tn), lambda i, j, k: (_divmod(j, per_o)[0], i, _divmod(j, per_o)[1])
        o_shape = (out_shards, M, N // out_shards)
    else:
        o_blk, o_idx, o_shape = (tm, tn), (lambda i, j, k: (i, j)), (M, N)
    c_idx = lambda i, j, k: (i, j)
    if n_outer:
        sw = lambda f: (lambda j, i, k: f(i, j, k))
        a_idx, b_idx, o_idx, c_idx = sw(a_idx), sw(b_idx), sw(o_idx), sw(c_idx)
        grid = (N // tn, M // tm, nk)
    else:
        grid = (M // tm, N // tn, nk)
    has_add = add is not None

    def body(*refs):
        if has_add:
            a_ref, b_ref, c_ref, o_ref, acc_ref = refs
        else:
            a_ref, b_ref, o_ref, acc_ref = refs
            c_ref = None
        part = lax.dot_general(a_ref[...].astype(BF16), b_ref[...].astype(BF16), dn, preferred_element_type=F32)

        def fin(val):
            if has_add:
                val = val + c_ref[...]
            o_ref[...] = val.astype(out_dtype)

        if nk == 1:
            fin(part)
        else:
            k = pl.program_id(2)

            @pl.when(k == 0)
            def _():
                acc_ref[...] = part

            @pl.when(k > 0)
            def _():
                acc_ref[...] += part

            @pl.when(k == nk - 1)
            def _():
                fin(acc_ref[...])

    in_specs = [pl.BlockSpec(a_blk, a_idx), pl.BlockSpec(b_blk, b_idx)]
    args = [a, b]
    if has_add:
        in_specs.append(pl.BlockSpec((tm, tn), c_idx))
        args.append(add)
    acc_shape = (tm, tn) if nk > 1 else (8, 128)
    return _call(body, name=name, grid=grid, in_specs=in_specs, out_specs=pl.BlockSpec(o_blk, o_idx),
                 out_shape=_sds(o_shape, out_dtype), scratch=[pltpu.VMEM(acc_shape, F32)],
                 sem=("arbitrary",) * 3 if side is not None else ("parallel", "parallel", "arbitrary"), side=side)(*args)


def _rms1_fwd(x, meta, w, side=None):
    seq, d = x.shape
    nt = (FRONT + seq) // RT

    def body(x_ref, m_ref, w_ref, o_ref):
        i = pl.program_id(0)

        def norm(v):
            r = lax.rsqrt(jnp.mean(v * v, axis=-1, keepdims=True) + EPS)
            return (v * r * w_ref[...]).astype(BF16)

        @pl.when(i == 0)
        def _():
            o_ref[0:RT - N_META, :] = jnp.zeros((RT - N_META, d), BF16)
            o_ref[RT - N_META:RT, :] = norm(m_ref[...])

        @pl.when(i > 0)
        def _():
            o_ref[...] = norm(x_ref[...])

    return _call(body, name="rms1_fwd", grid=(nt,),
                 in_specs=[pl.BlockSpec((RT, d), lambda i: (jnp.maximum(i - 1, 0), 0)),
                           pl.BlockSpec((N_META, d), lambda i: (0, 0)),
                           pl.BlockSpec((1, d), lambda i: (0, 0))],
                 out_specs=pl.BlockSpec((RT, d), lambda i: (i, 0)),
                 out_shape=_sds((FRONT + seq, d), BF16), sem=("arbitrary",) if side is not None else ("parallel",),
                 side=side)(x, meta, w)


def _rms1_bwd(x, meta, w, dxn, dh1):
    seq, d = x.shape
    nt = (FRONT + seq) // RT

    def body(x_ref, m_ref, w_ref, g_ref, r_ref, dx_ref, dm_ref, dw_ref):
        i = pl.program_id(0)

        def bwd(v, g):
            r = lax.rsqrt(jnp.mean(v * v, axis=-1, keepdims=True) + EPS)
            vh = v * r
            gh = g * w_ref[...]
            return r * (gh - vh * jnp.mean(gh * vh, axis=-1, keepdims=True)), jnp.sum(g * vh, axis=0, keepdims=True)

        @pl.when(i == 0)
        def _():
            dm, dw = bwd(m_ref[...], g_ref[RT - N_META:RT, :])
            dm_ref[...] = dm
            dw_ref[...] = dw

        @pl.when(i > 0)
        def _():
            dx, dw = bwd(x_ref[...], g_ref[...])
            dx_ref[...] = dx + r_ref[...]
            dw_ref[...] += dw

    xs = pl.BlockSpec((RT, d), lambda i: (jnp.maximum(i - 1, 0), 0))
    return _call(body, name="rms1_bwd", grid=(nt,),
                 in_specs=[xs, pl.BlockSpec((N_META, d), lambda i: (0, 0)), pl.BlockSpec((1, d), lambda i: (0, 0)),
                           pl.BlockSpec((RT, d), lambda i: (i, 0)), xs],
                 out_specs=[xs, pl.BlockSpec((N_META, d), lambda i: (0, 0)), pl.BlockSpec((1, d), lambda i: (0, 0))],
                 out_shape=[_sds((seq, d), F32), _sds((N_META, d), F32), _sds((1, d), F32)],
                 sem=("arbitrary",))(x, meta, w, dxn, dh1)


def _merge_fwd(proj, za, zb):
    seq, d = za.shape
    off = FRONT // RT
    ca, cb = 8 * HW // d, 8 * HW // d + 1

    def body(ga_ref, gb_ref, za_ref, zb_ref, o_ref):
        o_ref[...] = (_sigmoid(ga_ref[...]) * za_ref[...] + _sigmoid(gb_ref[...]) * zb_ref[...]).astype(BF16)

    zs = pl.BlockSpec((RT, d), lambda i: (i, 0))
    return _call(body, name="merge_fwd", grid=(seq // RT,),
                 in_specs=[pl.BlockSpec((RT, d), lambda i: (i + off, ca)), pl.BlockSpec((RT, d), lambda i: (i + off, cb)), zs, zs],
                 out_specs=zs, out_shape=_sds((seq, d), BF16), sem=("parallel",))(proj, proj, za, zb)


def _merge_bwd(proj, za, zb, dmerged):
    seq, d = za.shape
    off = FRONT // RT
    ca, cb = 8 * HW // d, 8 * HW // d + 1
    nt = (FRONT + seq) // RT

    def body(ga_ref, gb_ref, za_ref, zb_ref, dm_ref, dza_ref, dzb_ref, dg_ref):
        i = pl.program_id(0)

        @pl.when(i < off)
        def _():
            dg_ref[...] = jnp.zeros((RT, 2 * d), BF16)

        @pl.when(i >= off)
        def _():
            sa, sb, dm = _sigmoid(ga_ref[...]), _sigmoid(gb_ref[...]), dm_ref[...]
            dza_ref[...] = (dm * sa).astype(BF16)
            dzb_ref[...] = (dm * sb).astype(BF16)
            dg_ref[:, 0:d] = (dm * za_ref[...] * sa * (1.0 - sa)).astype(BF16)
            dg_ref[:, d:2 * d] = (dm * zb_ref[...] * sb * (1.0 - sb)).astype(BF16)

    rs = pl.BlockSpec((RT, d), lambda i: (jnp.maximum(i - off, 0), 0))
    return _call(body, name="merge_bwd", grid=(nt,),
                 in_specs=[pl.BlockSpec((RT, d), lambda i: (i, ca)), pl.BlockSpec((RT, d), lambda i: (i, cb)), rs, rs, rs],
                 out_specs=[rs, rs, pl.BlockSpec((RT, 2 * d), lambda i: (i, 0))],
                 out_shape=[_sds((seq, d), BF16), _sds((seq, d), BF16), _sds((FRONT + seq, 2 * d), BF16)],
                 sem=("arbitrary",))(proj, proj, za, zb, dmerged)


def _resid_norm_fwd(x, mix, w):
    seq, d = x.shape

    def body(x_ref, m_ref, w_ref, h_ref, n_ref):
        h = x_ref[...] + m_ref[...]
        h_ref[...] = h
        r = lax.rsqrt(jnp.mean(h * h, axis=-1, keepdims=True) + EPS)
        n_ref[...] = (h * r * w_ref[...]).astype(BF16)

    rs = pl.BlockSpec((RT, d), lambda i: (i, 0))
    return _call(body, name="resid_norm_fwd", grid=(seq // RT,),
                 in_specs=[rs, rs, pl.BlockSpec((1, d), lambda i: (0, 0))], out_specs=[rs, rs],
                 out_shape=[_sds((seq, d), F32), _sds((seq, d), BF16)], sem=("parallel",))(x, mix, w)


def _resid_norm_bwd(h1, w, dn2, dh2):
    seq, d = h1.shape

    def body(h_ref, w_ref, g_ref, r_ref, o_ref, ob_ref, dw_ref):
        i = pl.program_id(0)
        h, g = h_ref[...], g_ref[...]
        r = lax.rsqrt(jnp.mean(h * h, axis=-1, keepdims=True) + EPS)
        hh = h * r
        gh = g * w_ref[...]
        dh = r_ref[...] + r * (gh - hh * jnp.mean(gh * hh, axis=-1, keepdims=True))
        o_ref[...] = dh
        ob_ref[...] = dh.astype(BF16)
        dw = jnp.sum(g * hh, axis=0, keepdims=True)

        @pl.when(i == 0)
        def _():
            dw_ref[...] = dw

        @pl.when(i > 0)
        def _():
            dw_ref[...] += dw

    rs = pl.BlockSpec((RT, d), lambda i: (i, 0))
    ws = pl.BlockSpec((1, d), lambda i: (0, 0))
    return _call(body, name="resid_norm_bwd", grid=(seq // RT,), in_specs=[rs, ws, rs, rs], out_specs=[rs, rs, ws],
                 out_shape=[_sds((seq, d), F32), _sds((seq, d), BF16), _sds((1, d), F32)], sem=("arbitrary",))(h1, w, dn2, dh2)


def _swiglu_tiles(seq, ff):
    return min(512, seq), (1408 if ff % 1408 == 0 else 512)


def _swiglu_fwd(gu):
    seq, f2 = gu.shape
    ff = f2 // 2
    rt, tc = _swiglu_tiles(seq, ff)
    nb = ff // tc

    def body(g_ref, u_ref, o_ref):
        o_ref[...] = (_silu(g_ref[...].astype(F32)) * u_ref[...].astype(F32)).astype(BF16)

    return _call(body, name="swiglu_fwd", grid=(seq // rt, nb),
                 in_specs=[pl.BlockSpec((rt, tc), lambda i, j: (i, j)), pl.BlockSpec((rt, tc), lambda i, j: (i, j + nb))],
                 out_specs=pl.BlockSpec((rt, tc), lambda i, j: (i, j)), out_shape=_sds((seq, ff), BF16),
                 sem=("parallel", "parallel"))(gu, gu)


def _swiglu_bwd(gu, dact):
    seq, f2 = gu.shape
    ff = f2 // 2
    rt, tc = _swiglu_tiles(seq, ff)
    nb = ff // tc

    def body(g_ref, u_ref, d_ref, o_ref):
        g, d = g_ref[...].astype(F32), d_ref[...].astype(F32)
        o_ref[0] = (d * u_ref[...].astype(F32) * _dsilu(g)).astype(BF16)
        o_ref[1] = (d * _silu(g)).astype(BF16)

    bs = pl.BlockSpec((rt, tc), lambda i, j: (i, j))
    return _call(body, name="swiglu_bwd", grid=(seq // rt, nb),
                 in_specs=[bs, pl.BlockSpec((rt, tc), lambda i, j: (i, j + nb)), bs],
                 out_specs=pl.BlockSpec((2, rt, tc), lambda i, j: (0, i, j)),
                 out_shape=_sds((2, seq, ff), BF16), sem=("parallel", "parallel"))(gu, gu, dact)


def _loss_head(h1, f, w, tgt):
    seq, d = h1.shape
    nt = seq // RT

    def body(h_ref, f_ref, w_ref, t_ref, l_ref, dh_ref, dhb_ref, dw_ref):
        i = pl.program_id(0)
        h = h_ref[...] + f_ref[...]
        r = lax.rsqrt(jnp.mean(h * h, axis=-1, keepdims=True) + EPS)
        hh = h * r
        err = hh * w_ref[...] - t_ref[...]
        l_ref[...] = jnp.full((8, 128), 0.5 * jnp.sum(jnp.mean(err * err, axis=-1, keepdims=True)), F32)
        dy = err * (1.0 / d)
        gh = dy * w_ref[...]
        dh = r * (gh - hh * jnp.mean(gh * hh, axis=-1, keepdims=True))
        dh_ref[...] = dh
        dhb_ref[...] = dh.astype(BF16)
        dw = jnp.sum(dy * hh, axis=0, keepdims=True)

        @pl.when(i == 0)
        def _():
            dw_ref[...] = dw

        @pl.when(i > 0)
        def _():
            dw_ref[...] += dw

    rs = pl.BlockSpec((RT, d), lambda i: (i, 0))
    ws = pl.BlockSpec((1, d), lambda i: (0, 0))
    return _call(body, name="loss_head", grid=(nt,), in_specs=[rs, rs, ws, rs],
                 out_specs=[pl.BlockSpec((8, 128), lambda i: (i, 0)), rs, rs, ws],
                 out_shape=[_sds((nt * 8, 128), F32), _sds((seq, d), F32), _sds((seq, d), BF16), _sds((1, d), F32)],
                 sem=("arbitrary",))(h1, f, w, tgt)


def _sum_tiles(lt):
    n = lt.shape[0]

    def body(l_ref, o_ref):
        v = l_ref[...]
        r = lax.broadcasted_iota(jnp.int32, v.shape, 0)
        c = lax.broadcasted_iota(jnp.int32, v.shape, 1)
        o_ref[...] = jnp.sum(jnp.where((r % 8 == 0) & (c == 0), v, 0.0), keepdims=True)

    return _call(body, name="loss_sum", grid=(1,), in_specs=[pl.BlockSpec((n, 128), lambda i: (0, 0))],
                 out_specs=pl.BlockSpec((1, 1), lambda i: (0, 0)), out_shape=_sds((1, 1), F32))(lt)


def _gated_norm_fwd(o, g, nw):
    r = lax.rsqrt(jnp.mean(o * o, axis=-1, keepdims=True) + EPS)
    return o * r * nw * _silu(g)


def _gated_norm_bwd(o, g, nw, dout):
    r = lax.rsqrt(jnp.mean(o * o, axis=-1, keepdims=True) + EPS)
    oh = o * r
    don = dout * _silu(g)
    dg = dout * (oh * nw) * _dsilu(g)
    dnw = jnp.sum(don * oh, axis=0, keepdims=True)
    doh = don * nw
    return r * (doh - oh * jnp.mean(doh * oh, axis=-1, keepdims=True)), dg, dnw


def _hg_gates(fs, lbl):
    l0, l1 = lbl[0:1, :], lbl[1:2, :]
    m = jnp.maximum(l0, l1)
    e0, e1 = jnp.exp(l0 - m), jnp.exp(l1 - m)
    lb = e0 / (e0 + e1)
    sig = _sigmoid(fs)
    f = lb + (1.0 - lb) * sig
    return lb, sig, f, jnp.log(f), (1.0 - lb) * _sigmoid(-fs)


def _cumsum(w):
    row = lax.broadcasted_iota(jnp.int32, w.shape, 0) & (CH - 1)
    s = 1
    while s < CH:
        w = w + jnp.where(row >= s, pltpu.roll(w, s, 0), 0.0)
        s *= 2
    return w


def _rcumsum(w):
    row = lax.broadcasted_iota(jnp.int32, w.shape, 0) & (CH - 1)
    s = 1
    while s < CH:
        w = w + jnp.where(row < CH - s, pltpu.roll(w, w.shape[0] - s, 0), 0.0)
        s *= 2
    return w


def _decay_blocks(q, k, b, p_ref):
    p_ref[...] = jnp.zeros((CH, CH), F32)
    m16 = _tri(SUBH, "incl")
    for I in range(CH // SUBH):
        s0 = I * SUBH
        bI, qI, kI = b[s0:s0 + SUBH], q[s0:s0 + SUBH], k[s0:s0 + SUBH]
        dec = jnp.exp(jnp.minimum(bI[:, None, :] - bI[None, :, :], 0.0))
        pii = jnp.sum(qI[:, None, :] * kI[None, :, :] * dec, axis=-1)
        p_ref[s0:s0 + SUBH, s0:s0 + SUBH] = jnp.where(m16, pii, 0.0)
        if I > 0:
            rI = b[s0 - 1:s0]
            qs = qI * jnp.exp(bI - rI)
            ks = k[0:s0] * jnp.exp(rI - b[0:s0])
            p_ref[s0:s0 + SUBH, 0:s0] = _dot(qs, ks, NT)


HPS = 8
HPS_HGRN2_FWD = 4
HB = 4
NHB = NH // HPS
OFF = FRONT // RT


def _head_specs(hps, rev=None):
    row = (lambda i: i) if rev is None else rev
    col = lambda g: pl.BlockSpec((RT, hps * DH), lambda h, i: (row(i), g * (NH // hps) + h))
    full = pl.BlockSpec((RT, hps * DH), lambda h, i: (row(i), h))
    real = pl.BlockSpec((RT, hps * DH), lambda h, i: (jnp.maximum(row(i) - OFF, 0), h))
    state = lambda cpt: pl.BlockSpec((hps, cpt, DH, DH), lambda h, i: (h, row(i), 0, 0))
    scal = pl.BlockSpec((hps, RT, DH), lambda h, i: (h, row(i), 0))
    return col, full, real, state, scal


def _hgrn2_fwd(proj, lb_logits, nw):
    tp = proj.shape[0]
    nt, cpt = tp // RT, RT // CH
    hps = HPS_HGRN2_FWD

    def body(q_ref, f_ref, i_ref, g_ref, lbl_ref, nw_ref, og_ref, or_ref, st_ref, s_ref, p_ref):
        @pl.when(pl.program_id(1) == 0)
        def _():
            s_ref[...] = jnp.zeros((hps, DH, DH), F32)

        def chunk(c, carry):
            rows = pl.ds(pl.multiple_of(c * CH, CH), CH)
            for hh in range(hps):
                cols = slice(hh * DH, (hh + 1) * DH)
                _, _, _, w, k = _hg_gates(f_ref[rows, cols], lbl_ref[:, cols])
                q, v = _silu(q_ref[rows, cols]), i_ref[rows, cols]
                b = _cumsum(w)
                st = s_ref[hh]
                st_ref[hh, c] = st
                _decay_blocks(q, k, b, p_ref.at[hh])
                o = _dot(q * jnp.exp(b), st, NT) + _dot(p_ref[hh], v)
                bl = b[CH - 1:CH]
                s_ref[hh] = st * jnp.exp(bl) + _dot(v, k * jnp.exp(bl - b), TN)
                or_ref[rows, cols] = o
                og_ref[rows, cols] = _gated_norm_fwd(o, g_ref[rows, cols], nw_ref[...]).astype(BF16)
            return carry

        lax.fori_loop(0, cpt, chunk, 0)

    col, full, real, state, _ = _head_specs(hps)
    return _call(body, name="hgrn2_fwd", grid=(NH // hps, nt),
                 in_specs=[col(0), col(1), col(2), col(3), pl.BlockSpec((2, hps * DH), lambda h, i: (0, h)),
                           pl.BlockSpec((1, DH), lambda h, i: (0, 0))],
                 out_specs=[real, full, state(cpt)],
                 out_shape=[_sds((tp - FRONT, HW), BF16), _sds((tp, HW), F32), _sds((NH, tp // CH, DH, DH), F32)],
                 scratch=[pltpu.VMEM((hps, DH, DH), F32), pltpu.VMEM((hps, CH, CH), F32)],
                 sem=("parallel", "arbitrary"))(proj, proj, proj, proj, lb_logits, nw)


def _hgrn2_bwd(proj, lb_logits, nw, o_raw, states, dog, side=None):
    tp = proj.shape[0]
    nt, cpt = tp // RT, RT // CH

    def body(q_ref, f_ref, i_ref, g_ref, lbl_ref, nw_ref, or_ref, st_ref, dog_ref,
             dq_ref, df_ref, di_ref, dg_ref, dl_ref, dnw_ref, ds_ref, p_ref, dk_ref, dqa_ref, do_ref):
        step = pl.program_id(1)

        @pl.when(step == 0)
        def _():
            ds_ref[...] = jnp.zeros((HPS, DH, DH), F32)
            dl_ref[...] = jnp.zeros((2, HPS * DH), F32)

        @pl.when((step == 0) & (pl.program_id(0) == 0))
        def _():
            dnw_ref[...] = jnp.zeros((1, DH), F32)

        front = nt - 1 - step < OFF
        for hh in range(HPS):
            cols = slice(hh * DH, (hh + 1) * DH)
            dog_t = jnp.where(front, 0.0, dog_ref[:, cols])
            do_t, dg_t, dnw = _gated_norm_bwd(or_ref[:, cols], g_ref[:, cols], nw_ref[...], dog_t)
            do_ref[:, cols] = do_t
            dg_ref[:, cols] = dg_t.astype(BF16)
            dnw_ref[...] += dnw
        tril = _tri(CH, "incl")
        m16 = _tri(SUBH, "incl")

        def chunk(cc, carry):
            c = cpt - 1 - cc
            rows = pl.ds(pl.multiple_of(c * CH, CH), CH)
            for hh in range(HPS):
                cols = slice(hh * DH, (hh + 1) * DH)
                fs = f_ref[rows, cols]
                lb, sig, f, w, k = _hg_gates(fs, lbl_ref[:, cols])
                hq = q_ref[rows, cols]
                q, v, do = _silu(hq), i_ref[rows, cols], do_ref[rows, cols]
                b = _cumsum(w)
                bl = b[CH - 1:CH]
                eb = jnp.exp(b)
                qs, kd = q * eb, k * jnp.exp(bl - b)
                st, dst = st_ref[hh, c], ds_ref[hh]
                _decay_blocks(q, k, b, p_ref.at[hh])
                dv = _dot(p_ref[hh], do, TN) + _dot(kd, dst, NT)
                dp = jnp.where(tril, _dot(do, v, NT), 0.0)
                dqa, dka = dqa_ref.at[hh], dk_ref.at[hh]
                dqa[...] = eb * _dotx(do, st)
                dka[...] = jnp.exp(bl - b) * _dotx(v, dst)
                for I in range(CH // SUBH):
                    s0 = I * SUBH
                    bI, qI, kI = b[s0:s0 + SUBH], q[s0:s0 + SUBH], k[s0:s0 + SUBH]
                    dec = jnp.exp(jnp.minimum(bI[:, None, :] - bI[None, :, :], 0.0))
                    dpii = jnp.where(m16, dp[s0:s0 + SUBH, s0:s0 + SUBH], 0.0)[:, :, None] * dec
                    dqa[s0:s0 + SUBH, :] += jnp.sum(dpii * kI[None, :, :], axis=1)
                    dka[s0:s0 + SUBH, :] += jnp.sum(dpii * qI[:, None, :], axis=0)
                    if I > 0:
                        rI = b[s0 - 1:s0]
                        eq, ek = jnp.exp(bI - rI), jnp.exp(rI - b[0:s0])
                        dpij = dp[s0:s0 + SUBH, 0:s0]
                        dqa[s0:s0 + SUBH, :] += eq * _dotx(dpij, k[0:s0] * ek)
                        dka[0:s0, :] += ek * _dotx(dpij, qI * eq, TN)
                dq, dk = dqa[...], dka[...]
                st_end = st * jnp.exp(bl) + _dotx(v, kd, TN)
                dw = _rcumsum(q * dq - k * dk) + jnp.sum(dst * st_end, axis=0, keepdims=True)
                ds_ref[hh] = dst * jnp.exp(bl) + _dotx(do, qs, TN)
                one_m = 1.0 - sig
                dq_ref[rows, cols] = (dq * _dsilu(hq)).astype(BF16)
                df_ref[rows, cols] = ((dw / f - dk) * (1.0 - lb) * sig * one_m).astype(BF16)
                di_ref[rows, cols] = dv.astype(BF16)
                dl_ref[0:1, cols] += jnp.sum((dw / f - dk) * one_m, axis=0, keepdims=True)
            return carry

        lax.fori_loop(0, cpt, chunk, 0)

        @pl.when(step == nt - 1)
        def _():
            lbl = lbl_ref[...]
            l0, l1 = lbl[0:1, :], lbl[1:2, :]
            m = jnp.maximum(l0, l1)
            e0, e1 = jnp.exp(l0 - m), jnp.exp(l1 - m)
            p0 = e0 / (e0 + e1)
            dl0 = dl_ref[0:1, :] * p0 * (1.0 - p0)
            dl_ref[0:1, :] = dl0
            dl_ref[1:2, :] = -dl0

    col, full, real, state, _ = _head_specs(HPS, lambda i: nt - 1 - i)
    lbs = pl.BlockSpec((2, HPS * DH), lambda h, i: (0, h))
    return _call(body, name="hgrn2_bwd", grid=(NHB, nt),
                 in_specs=[col(0), col(1), col(2), col(3), lbs, pl.BlockSpec((1, DH), lambda h, i: (0, 0)), full,
                           state(cpt), real],
                 out_specs=[full, full, full, full, lbs, pl.BlockSpec((1, DH), lambda h, i: (0, 0))],
                 out_shape=[_sds((tp, HW), BF16)] * 4 + [_sds((2, HW), F32), _sds((1, DH), F32)],
                 scratch=[pltpu.VMEM((HPS, DH, DH), F32), pltpu.VMEM((HPS, CH, CH), F32), pltpu.VMEM((HPS, CH, DH), F32),
                          pltpu.VMEM((HPS, CH, DH), F32), pltpu.VMEM((RT, HPS * DH), F32)],
                 sem=("arbitrary", "arbitrary"), side=side)(proj, proj, proj, proj, lb_logits, nw, o_raw, states, dog)


GQ0 = 4 * HW
CW = 3 * HW


def _gd_scalars(ab, alog, dtb):
    g = -jnp.exp(alog) * jax.nn.softplus(ab + dtb)
    return g, _sigmoid(ab)


def _conv_ext_specs(row_of):
    main = [pl.BlockSpec((RT, HW), lambda i, g=g: (row_of(i), GQ0 // HW + g)) for g in range(3)]
    prev = [pl.BlockSpec((8, HW), lambda i, g=g: (jnp.maximum(row_of(i) * (RT // 8) - 1, 0), GQ0 // HW + g)) for g in range(3)]
    return main + prev


def _conv_fill(ext_ref, xs, xps, first):
    for g in range(3):
        ext_ref[0:8, g * HW:(g + 1) * HW] = jnp.where(first, 0.0, xps[g][...])
        ext_ref[8:8 + RT, g * HW:(g + 1) * HW] = xs[g][...]


def _conv_apply(ext_ref, cw):
    y = cw[CONV_K - 1:CONV_K, :] * ext_ref[pl.ds(8, RT), :]
    for s in range(1, CONV_K):
        y += cw[CONV_K - 1 - s:CONV_K - s, :] * ext_ref[pl.ds(8 - s, RT), :]
    return y


def _gdn_prep_fwd(proj, pab, conv_w, alog, dtb):
    tp = proj.shape[0]
    nt = tp // RT

    def body(x0, x1, x2, p0, p1, p2, ab_ref, cw_ref, al_ref, dt_ref, q_ref, k_ref, v_ref, g_ref, b_ref, ext_ref):
        _conv_fill(ext_ref, (x0, x1, x2), (p0, p1, p2), pl.program_id(0) == 0)
        a = _silu(_conv_apply(ext_ref, cw_ref[...]))
        for h in range(NH):
            for part, ref, sc in ((0, q_ref, DH ** -0.5), (1, k_ref, 1.0)):
                seg = a[:, part * HW + h * DH:part * HW + (h + 1) * DH]
                ref[:, h * DH:(h + 1) * DH] = seg * (lax.rsqrt(jnp.sum(seg * seg, axis=-1, keepdims=True) + EPS) * sc)
        v_ref[...] = a[:, 2 * HW:3 * HW]
        g, beta = _gd_scalars(ab_ref[...], al_ref[...], dt_ref[...])
        for h in range(NH):
            g_ref[h] = jnp.broadcast_to(g[:, h:h + 1], (RT, DH))
            b_ref[h] = jnp.broadcast_to(beta[:, NH + h:NH + h + 1], (RT, DH))

    hs = pl.BlockSpec((RT, HW), lambda i: (i, 0))
    sc = pl.BlockSpec((NH, RT, DH), lambda i: (0, i, 0))
    one = pl.BlockSpec((1, DH), lambda i: (0, 0))
    return _call(body, name="gdn_prep_fwd", grid=(nt,),
                 in_specs=_conv_ext_specs(lambda i: i) + [pl.BlockSpec((RT, DH), lambda i: (i, 0)),
                                                           pl.BlockSpec((CONV_K, CW), lambda i: (0, 0)), one, one],
                 out_specs=[hs, hs, hs, sc, sc],
                 out_shape=[_sds((tp, HW), F32)] * 3 + [_sds((NH, tp, DH), F32)] * 2,
                 scratch=[pltpu.VMEM((RT + 8, CW), F32)], sem=("parallel",))(*([proj] * 6), pab, conv_w, alog, dtb)


def _gdn_prep_bwd(proj, pab, conv_w, alog, dtb, dq, dk, dv, dgb, dbb):
    tp = proj.shape[0]
    nt = tp // RT

    def body(x0, x1, x2, p0, p1, p2, ab_ref, cw_ref, al_ref, dt_ref, dq_ref, dk_ref, dv_ref, dg_ref, db_ref,
             dx_ref, dab_ref, dcw_ref, dal_ref, ddt_ref, ext_ref, dy_ref):
        step = pl.program_id(0)
        i = nt - 1 - step

        @pl.when(step == 0)
        def _():
            dy_ref[RT:RT + 8, :] = jnp.zeros((8, CW), F32)
            dcw_ref[...] = jnp.zeros((8, CW), F32)
            dal_ref[...] = jnp.zeros((1, DH), F32)
            ddt_ref[...] = jnp.zeros((1, DH), F32)

        _conv_fill(ext_ref, (x0, x1, x2), (p0, p1, p2), i == 0)
        cw = cw_ref[...]
        y = _conv_apply(ext_ref, cw)
        a = _silu(y)
        dsl = _dsilu(y)
        for h in range(NH):
            for part, ref, sc in ((0, dq_ref, DH ** -0.5), (1, dk_ref, 1.0)):
                lo = part * HW + h * DH
                seg = a[:, lo:lo + DH]
                r = lax.rsqrt(jnp.sum(seg * seg, axis=-1, keepdims=True) + EPS)
                xh = seg * r
                dxh = ref[:, h * DH:(h + 1) * DH] * sc
                dy_ref[0:RT, lo:lo + DH] = r * (dxh - xh * jnp.sum(dxh * xh, axis=-1, keepdims=True)) * dsl[:, lo:lo + DH]
        dy_ref[0:RT, 2 * HW:3 * HW] = dv_ref[...] * dsl[:, 2 * HW:3 * HW]
        dy = dy_ref[0:RT, :]
        dx = cw[CONV_K - 1:CONV_K, :] * dy
        dcw_ref[CONV_K - 1:CONV_K, :] += jnp.sum(dy * ext_ref[pl.ds(8, RT), :], axis=0, keepdims=True)
        for s in range(1, CONV_K):
            dx += cw[CONV_K - 1 - s:CONV_K - s, :] * dy_ref[pl.ds(s, RT), :]
            dcw_ref[CONV_K - 1 - s:CONV_K - s, :] += jnp.sum(dy * ext_ref[pl.ds(8 - s, RT), :], axis=0, keepdims=True)
        dx_ref[...] = dx.astype(BF16)
        dy_ref[RT:RT + 8, :] = dy[0:8, :]
        ab = ab_ref[...]
        g, beta = _gd_scalars(ab, al_ref[...], dt_ref[...])
        lane = lax.broadcasted_iota(jnp.int32, (RT, DH), 1)
        dgl = jnp.zeros((RT, DH), F32)
        dbl = jnp.zeros((RT, DH), F32)
        for h in range(NH):
            dgl = jnp.where(lane == h, dg_ref[h], dgl)
            dbl = jnp.where(lane == NH + h, db_ref[h], dbl)
        dsp = dgl * (-jnp.exp(al_ref[...])) * _sigmoid(ab + dt_ref[...])
        dab_ref[...] = (dsp + dbl * beta * (1.0 - beta)).astype(BF16)
        ddt_ref[...] += jnp.sum(dsp, axis=0, keepdims=True)
        dal_ref[...] += jnp.sum(dgl * g, axis=0, keepdims=True)

    hs = pl.BlockSpec((RT, HW), lambda s: (nt - 1 - s, 0))
    sc = pl.BlockSpec((NH, RT, DH), lambda s: (0, nt - 1 - s, 0))
    one = pl.BlockSpec((1, DH), lambda s: (0, 0))
    xs = pl.BlockSpec((RT, CW), lambda s: (nt - 1 - s, 0))
    return _call(body, name="gdn_prep_bwd", grid=(nt,),
                 in_specs=_conv_ext_specs(lambda s: nt - 1 - s) + [
                     pl.BlockSpec((RT, DH), lambda s: (nt - 1 - s, 0)), pl.BlockSpec((CONV_K, CW), lambda s: (0, 0)),
                     one, one, hs, hs, hs, sc, sc],
                 out_specs=[xs, pl.BlockSpec((RT, DH), lambda s: (nt - 1 - s, 0)), pl.BlockSpec((8, CW), lambda s: (0, 0)), one, one],
                 out_shape=[_sds((tp, CW), BF16), _sds((tp, DH), BF16), _sds((8, CW), F32), _sds((1, DH), F32), _sds((1, DH), F32)],
                 scratch=[pltpu.VMEM((RT + 8, CW), F32), pltpu.VMEM((RT + 8, CW), F32)],
                 sem=("arbitrary",))(*([proj] * 6), pab, conv_w, alog, dtb, dq, dk, dv, dgb, dbb)


HS = HB * CH


def _stack_heads(ref, rows, base):
    return jnp.concatenate([ref[rows, (base + hh) * DH:(base + hh + 1) * DH] for hh in range(HB)], axis=0)


def _stack_scal(ref, rows, base):
    return jnp.concatenate([ref[base + hh, rows, :] for hh in range(HB)], axis=0)


def _store_heads(ref, rows, base, val):
    for hh in range(HB):
        ref[rows, (base + hh) * DH:(base + hh + 1) * DH] = val[hh * CH:(hh + 1) * CH].astype(ref.dtype)


def _bd_masks():
    r = lax.broadcasted_iota(jnp.int32, (HS, HS), 0)
    c = lax.broadcasted_iota(jnp.int32, (HS, HS), 1)
    same = lax.shift_right_logical(r, int(math.log2(CH))) == lax.shift_right_logical(c, int(math.log2(CH)))
    return same & (r >= c), same & (r > c)


def _unit_lower_inverse(a):
    n = a.shape[0]
    r = lax.broadcasted_iota(jnp.int32, (n, n), 0)
    c = lax.broadcasted_iota(jnp.int32, (n, n), 1)
    blk_of = lambda t, size: lax.shift_right_logical(t, int(math.log2(size)))
    a16 = jnp.where(blk_of(r, SUB) == blk_of(c, SUB), a, 0.0)
    x = (r == c).astype(F32) - a16
    p = a16
    for _ in range(3):
        p = _dot(p, p)
        x = x + _dot(x, p)
    for blk in (2 * SUB, 4 * SUB):
        off = jnp.where((blk_of(r, blk) == blk_of(c, blk)) & (blk_of(r, blk // 2) != blk_of(c, blk // 2)), a, 0.0)
        x = x - _dot(x, _dot(off, x))
    return x


def _gdn_chunk_common(q, k, v, gl, bt, incl, strict, x=None):
    gc = _cumsum(gl)
    e = jnp.exp(gc)
    rel = jnp.exp(jnp.minimum(gc[:, 0:1] - gc.T[0:1, :], 0.0))
    kb = bt * k
    a = jnp.where(strict, bt[:, 0:1] * _dot(k, k, NT) * rel, 0.0)
    if x is None:
        x = _unit_lower_inverse(a)
    wu = _dot(x, jnp.concatenate([kb * e, bt * v], axis=1))
    attn = jnp.where(incl, _dot(q, k, NT) * rel, 0.0)
    return gc, e, rel, kb, a, x, wu[:, 0:DH], wu[:, DH:2 * DH], attn


def _gdn_fwd(q, k, v, gb, bb, proj, nw):
    tp = q.shape[0]
    nt, cpt = tp // RT, RT // CH

    def body(q_ref, k_ref, v_ref, g_ref, b_ref, z_ref, nw_ref, og_ref, or_ref, st_ref, x_ref, s_ref):
        @pl.when(pl.program_id(1) == 0)
        def _():
            s_ref[...] = jnp.zeros((HPS, DH, DH), F32)

        incl, strict = _bd_masks()

        def chunk(c, carry):
            rows = pl.ds(pl.multiple_of(c * CH, CH), CH)
            for base in range(0, HPS, HB):
                qc, kc, vc = _stack_heads(q_ref, rows, base), _stack_heads(k_ref, rows, base), _stack_heads(v_ref, rows, base)
                gc, e, rel, kb, a, x, w, u, attn = _gdn_chunk_common(qc, kc, vc, _stack_scal(g_ref, rows, base),
                                                                    _stack_scal(b_ref, rows, base), incl, strict)
                x_ref[base // HB, c] = x
                qe = qc * e
                ws, qs = [], []
                for hh in range(HB):
                    blk = slice(hh * CH, (hh + 1) * CH)
                    s = s_ref[base + hh]
                    st_ref[base + hh, c] = s
                    both = _dot(jnp.concatenate([w[blk], qe[blk]], axis=0), s)
                    ws.append(both[0:CH])
                    qs.append(both[CH:2 * CH])
                vn = u - jnp.concatenate(ws, axis=0)
                o = jnp.concatenate(qs, axis=0) + _dot(attn, vn)
                for hh in range(HB):
                    blk = slice(hh * CH, (hh + 1) * CH)
                    gl = gc[(hh + 1) * CH - 1:(hh + 1) * CH]
                    s_ref[base + hh] = s_ref[base + hh] * jnp.exp(gl) + _dot(kc[blk] * jnp.exp(gl - gc[blk]), vn[blk], TN)
                _store_heads(or_ref, rows, base, o)
                for hh in range(HB):
                    cols = slice((base + hh) * DH, (base + hh + 1) * DH)
                    og_ref[rows, cols] = _gated_norm_fwd(o[hh * CH:(hh + 1) * CH], z_ref[rows, cols], nw_ref[...]).astype(BF16)
            return carry

        lax.fori_loop(0, cpt, chunk, 0)

    col, full, real, state, scal = _head_specs(HPS)
    return _call(body, name="gdn_fwd", grid=(NHB, nt),
                 in_specs=[full, full, full, scal, scal, col(7), pl.BlockSpec((1, DH), lambda h, i: (0, 0))],
                 out_specs=[real, full, state(cpt), pl.BlockSpec((HPS // HB, cpt, HS, HS), lambda h, i: (h, i, 0, 0))],
                 out_shape=[_sds((tp - FRONT, HW), BF16), _sds((tp, HW), F32), _sds((NH, tp // CH, DH, DH), F32),
                            _sds((NH // HB, tp // CH, HS, HS), F32)],
                 scratch=[pltpu.VMEM((HPS, DH, DH), F32)], sem=("parallel", "arbitrary"))(q, k, v, gb, bb, proj, nw)


def _gdn_bwd(q, k, v, gb, bb, proj, nw, o_raw, states, xinv, dog):
    tp = q.shape[0]
    nt, cpt = tp // RT, RT // CH

    def body(q_ref, k_ref, v_ref, g_ref, b_ref, z_ref, nw_ref, or_ref, st_ref, x_ref, dog_ref,
             dq_ref, dk_ref, dv_ref, dg_ref, db_ref, dz_ref, dnw_ref, ds_ref, do_ref):
        step = pl.program_id(1)

        @pl.when(step == 0)
        def _():
            ds_ref[...] = jnp.zeros((HPS, DH, DH), F32)

        @pl.when((step == 0) & (pl.program_id(0) == 0))
        def _():
            dnw_ref[...] = jnp.zeros((1, DH), F32)

        front = nt - 1 - step < OFF
        for hh in range(HPS):
            cols = slice(hh * DH, (hh + 1) * DH)
            dog_t = jnp.where(front, 0.0, dog_ref[:, cols])
            do_t, dz_t, dnw = _gated_norm_bwd(or_ref[:, cols], z_ref[:, cols], nw_ref[...], dog_t)
            do_ref[:, cols] = do_t
            dz_ref[:, cols] = dz_t.astype(BF16)
            dnw_ref[...] += dnw
        incl, strict = _bd_masks()
        last_row = (lax.broadcasted_iota(jnp.int32, (CH, 1), 0) == CH - 1)

        def rsum(t):
            return jnp.sum(t, axis=-1, keepdims=True)

        def chunk(cc, carry):
            c = cpt - 1 - cc
            rows = pl.ds(pl.multiple_of(c * CH, CH), CH)
            for base in range(0, HPS, HB):
                qc, kc, vc, do = (_stack_heads(q_ref, rows, base), _stack_heads(k_ref, rows, base), _stack_heads(v_ref, rows, base),
                                  _stack_heads(do_ref, rows, base))
                bt = _stack_scal(b_ref, rows, base)
                gc, e, rel, kb, a, x, w, u, attn = _gdn_chunk_common(qc, kc, vc, _stack_scal(g_ref, rows, base), bt, incl, strict,
                                                                    x=x_ref[base // HB, c])
                qe = qc * e
                heads = [slice(hh * CH, (hh + 1) * CH) for hh in range(HB)]
                gls = [gc[(hh + 1) * CH - 1:(hh + 1) * CH] for hh in range(HB)]
                cdec = jnp.concatenate([jnp.exp(gl - gc[blk]) for gl, blk in zip(gls, heads)], axis=0)
                kcd = kc * cdec
                vn = u - jnp.concatenate([_dot(w[blk], st_ref[base + hh, c]) for hh, blk in enumerate(heads)], axis=0)
                dos = jnp.concatenate([_dot(do[blk], st_ref[base + hh, c], NT) for hh, blk in enumerate(heads)], axis=0)
                kds = jnp.concatenate([_dot(kcd[blk], ds_ref[base + hh]) for hh, blk in enumerate(heads)], axis=0)
                vds = jnp.concatenate([_dot(vn[blk], ds_ref[base + hh], NT) for hh, blk in enumerate(heads)], axis=0)
                dvn = _dot(attn, do, TN) + kds
                dattn = jnp.where(incl, _dot(do, vn, NT), 0.0)
                dar = dattn * rel
                dq = _dot(dar, kc) + e * dos
                dk = _dot(dar, qc, TN) + cdec * vds
                dc = cdec[:, 0:1] * rsum(kc * vds)
                dgc = rsum(qe * dos) - dc
                dw = -jnp.concatenate([_dot(dvn[blk], st_ref[base + hh, c], NT) for hh, blk in enumerate(heads)], axis=0)
                extra = []
                for hh, blk in enumerate(heads):
                    s, dsn = st_ref[base + hh, c], ds_ref[base + hh]
                    el = jnp.exp(gls[hh])
                    dglast = jnp.sum(dc[blk], axis=0, keepdims=True) + el[:, 0:1] * jnp.sum(rsum(dsn * s), axis=0, keepdims=True)
                    extra.append(jnp.where(last_row, dglast, 0.0))
                    ds_ref[base + hh] = dsn * el + _dot(jnp.concatenate([qe[blk], -w[blk]], axis=0),
                                                 jnp.concatenate([do[blk], dvn[blk]], axis=0), TN)
                dr = _dot(x, jnp.concatenate([dw, dvn], axis=1), TN)
                drw, dru = dr[:, 0:DH], dr[:, DH:2 * DH]
                da = -jnp.where(strict, _dot(dr, jnp.concatenate([w, u], axis=1), NT), 0.0)
                dar2 = da * rel
                dkb = _dot(dar2, kc)
                rwk = rsum(drw * kc)
                dk = dk + _dot(dar2, kb, TN) + bt * dkb + (bt * e) * drw
                dbeta = rsum(dkb * kc) + e[:, 0:1] * rwk + rsum(dru * vc)
                z = dattn * attn + da * a
                dgc = dgc + bt[:, 0:1] * e[:, 0:1] * rwk + rsum(z) - rsum(z.T) + jnp.concatenate(extra, axis=0)
                _store_heads(dq_ref, rows, base, dq)
                _store_heads(dk_ref, rows, base, dk)
                _store_heads(dv_ref, rows, base, bt * dru)
                dg = _rcumsum(jnp.broadcast_to(dgc, (HS, DH)))
                dbb = jnp.broadcast_to(dbeta, (HS, DH))
                for hh, blk in enumerate(heads):
                    dg_ref[base + hh, rows, :] = dg[blk]
                    db_ref[base + hh, rows, :] = dbb[blk]
            return carry

        lax.fori_loop(0, cpt, chunk, 0)

    col, full, real, state, scal = _head_specs(HPS, lambda i: nt - 1 - i)
    one = pl.BlockSpec((1, DH), lambda h, i: (0, 0))
    return _call(body, name="gdn_bwd", grid=(NHB, nt),
                 in_specs=[full, full, full, scal, scal, col(7), one, full, state(cpt),
                           pl.BlockSpec((HPS // HB, cpt, HS, HS), lambda h, i: (h, nt - 1 - i, 0, 0)), real],
                 out_specs=[full, full, full, scal, scal, full, one],
                 out_shape=[_sds((tp, HW), F32)] * 3 + [_sds((NH, tp, DH), F32)] * 2 + [_sds((tp, HW), BF16), _sds((1, DH), F32)],
                 scratch=[pltpu.VMEM((HPS, DH, DH), F32), pltpu.VMEM((RT, HPS * DH), F32)],
                 sem=("arbitrary", "arbitrary"))(q, k, v, gb, bb, proj, nw, o_raw, states, xinv, dog)


MAIN_W = 8 * HW
AB_W = 2 * NH


def _split_w_in(w_in):
    main = jnp.concatenate([w_in[:, :MAIN_W], w_in[:, MAIN_W + AB_W:]], axis=1)
    ab = jnp.pad(w_in[:, MAIN_W:MAIN_W + AB_W], ((0, 0), (0, DH - AB_W)))
    return main, ab


def _pad_lanes(v):
    return jnp.pad(v, ((0, 0), (0, DH - v.shape[1])))


class _NoComm:
    def __init__(self, w_in, late):
        self.w_in, self.late = w_in, late

    def first_side(self):
        return None

    def first_weight(self, side_outs):
        return self.w_in

    def proj_side(self):
        return None

    def late_weights(self, side_outs):
        return self.late

    def early_grads_side(self, grads):
        return None

    def early_grads_done(self, side_outs):
        pass

    def last_grad_side(self, dw_in):
        return None

    def last_grad_done(self, side_outs):
        pass


def _local_step(x, tgt, meta, lb_logits, mix_w, hg_nw, conv_w, a_log, dt_bias, gd_nw, ffn_nw, final_w, comm):
    alog, dtb = _pad_lanes(a_log), _pad_lanes(dt_bias)
    final_w = final_w.reshape(1, -1)
    rows4 = lambda t: t.reshape(4, t.shape[0] // 4, t.shape[1])
    side = comm.first_side()
    xn = _rms1_fwd(x, meta, mix_w, side=side)
    xn, landed = xn if side is not None else (xn, None)
    w_main, w_ab = _split_w_in(comm.first_weight(landed))
    side = comm.proj_side()
    proj = _mm(xn, w_main, "nn", F32, 768, 2048, 2048, "proj_main", n_outer=True, side=side)
    proj, landed = proj if side is not None else (proj, None)
    w_a, w_b, w_out, w_ffn_in, w_ffn_out = comm.late_weights(landed)
    pab = _mm(xn, w_ab, "nn", F32, 768, 128, 2048, "proj_ab")
    oa_g, oa_raw, st_a = _hgrn2_fwd(proj, lb_logits, hg_nw)
    q, k, v, gb, bb = _gdn_prep_fwd(proj, pab, conv_w, alog, dtb)
    ob_g, ob_raw, st_b, xinv = _gdn_fwd(q, k, v, gb, bb, proj, gd_nw)
    za = _mm(oa_g, w_a, "nn", F32, 1024, 512, 1024, "branch_a", n_outer=True)
    zb = _mm(ob_g, w_b, "nn", F32, 1024, 512, 1024, "branch_b", n_outer=True)
    merged = _merge_fwd(proj, za, zb)
    mix = _mm(merged, w_out, "nn", F32, 1024, 2048, 2048, "mix_out")
    h1, n2 = _resid_norm_fwd(x, mix, ffn_nw)
    gu = _mm(n2, w_ffn_in, "nn", BF16, 1024, 1408, 2048, "ffn_in", n_outer=True)
    act = _swiglu_fwd(gu)
    f = _mm(act, w_ffn_out, "nn", F32, 1024, 2048, 1408, "ffn_out")
    lt, dh2, dh2b, dfinal = _loss_head(h1, f, final_w, tgt)
    loss = _sum_tiles(lt)
    dact = _mm(dh2b, w_ffn_out, "nt", BF16, 1024, 1408, 2048, "d_act", n_outer=True)
    dw_ffn_out = rows4(_mm(act, dh2b, "tn", F32, 512, 2048, 2048, "dw_ffn_out"))
    dgu = _swiglu_bwd(gu, dact)
    dn2 = _mm(dgu, w_ffn_in, "nt", F32, 1024, 2048, 1408, "d_n2")
    dw_ffn_in = _mm(n2, dgu, "tn", F32, 1024, 1408, 2048, "dw_ffn_in", out_shards=4)
    dh1, dh1b, dffn_nw = _resid_norm_bwd(h1, ffn_nw, dn2, dh2)
    dmerged = _mm(dh1b, w_out, "nt", F32, 1024, 2048, 2048, "d_merged")
    dw_out = _mm(merged, dh1b, "tn", F32, 2048, 1024, 1024, "dw_out")
    dza, dzb, dgate = _merge_bwd(proj, za, zb, dmerged)
    doa = _mm(dza, w_a, "nt", F32, 1024, 1024, 512, "d_oa")
    dob = _mm(dzb, w_b, "nt", F32, 1024, 1024, 512, "d_ob")
    dw_a = _mm(oa_g, dza, "tn", F32, 1024, 512, 1024, "dw_branch_a", out_shards=4)
    dw_b = _mm(ob_g, dzb, "tn", F32, 1024, 512, 1024, "dw_branch_b", out_shards=4)
    early = dict(w_ffn_in=dw_ffn_in, w_ffn_out=dw_ffn_out, w_out=rows4(dw_out), w_branch_a=dw_a, w_branch_b=dw_b)
    side = comm.early_grads_side(early)
    hg = _hgrn2_bwd(proj, lb_logits, hg_nw, oa_raw, st_a, doa, side=side)
    if side is not None:
        hg, arrived = hg
        comm.early_grads_done(arrived)
    dhq, dhf, dhi, dhg, dlbl, dhg_nw = hg
    dq, dk, dv, dg, dbeta, dz, dgd_nw = _gdn_bwd(q, k, v, gb, bb, proj, gd_nw, ob_raw, st_b, xinv, dob)
    dx3, dab, dconv, dalog, ddtb = _gdn_prep_bwd(proj, pab, conv_w, alog, dtb, dq, dk, dv, dg, dbeta)
    dproj = jnp.concatenate([dhq, dhf, dhi, dhg, dx3, dz, dgate], axis=1)
    dw_main = _mm(xn, dproj, "tn", F32, 1024, 1024, 2816, "dw_in_main")
    dw_ab = _mm(xn, dab, "tn", F32, 2048, 128, 768, "dw_in_ab")
    d_model = dw_main.shape[0]
    dw_in = jnp.concatenate([dw_main[:, :MAIN_W], dw_ab[:, :AB_W], dw_main[:, MAIN_W:]], axis=1)
    dw_in = dw_in.reshape(d_model, 4, -1).transpose(1, 0, 2)
    side = comm.last_grad_side(dw_in)
    dxn = _mm(dproj, w_main, "nt", F32, 768, 2048, 2048, "d_xn", side=side)
    if side is not None:
        dxn, arrived = dxn
        comm.last_grad_done(arrived)
    dxn = _mm(dab, w_ab, "nt", F32, 768, 2048, 128, "d_xn_ab", add=dxn)
    dx, dmeta, dmix_w = _rms1_bwd(x, meta, mix_w, dxn, dh1)
    grads = dict(meta_tokens=dmeta, lb_logits=dlbl, mix_norm_w=dmix_w, w_in=dw_in,
                 hg_norm_w=dhg_nw, gd_conv_w=dconv[:CONV_K], gd_a_log=dalog[:, :NH],
                 gd_dt_bias=ddtb[:, :NH], gd_norm_w=dgd_nw, w_branch_a=dw_a, w_branch_b=dw_b,
                 w_out=rows4(dw_out), ffn_norm_w=dffn_nw, w_ffn_in=dw_ffn_in, w_ffn_out=dw_ffn_out,
                 final_norm_w=dfinal.reshape(-1))
    return loss, dx, grads


def _adamw(g, w, m, v, name):
    rows, cols = g.shape
    tr = rows
    for cand in (128, 64, 32, 16, 8):
        if rows % cand == 0 and rows > cand:
            tr = cand
            break

    def body(g_ref, w_ref, m_ref, v_ref, go_ref, d_ref, nm_ref, nv_ref):
        gg = g_ref[...]
        go_ref[...] = gg
        nm = ADAM_B1 * m_ref[...] + (1.0 - ADAM_B1) * gg
        nv = ADAM_B2 * v_ref[...] + (1.0 - ADAM_B2) * (gg * gg)
        m_hat = nm / (1.0 - ADAM_B1 ** ADAM_STEP)
        v_hat = nv / (1.0 - ADAM_B2 ** ADAM_STEP)
        d_ref[...] = -ADAM_LR * (m_hat / (jnp.sqrt(v_hat) + ADAM_EPS) + ADAM_WD * w_ref[...])
        nm_ref[...] = nm
        nv_ref[...] = nv

    bs = pl.BlockSpec((tr, cols), lambda i: (i, 0))
    ws = bs if w.ndim == 2 else pl.BlockSpec((None, tr, cols), lambda i: (0, i, 0))
    return _call(body, name=name, grid=(rows // tr,), in_specs=[bs, ws, ws, ws], out_specs=[ws] * 4,
                 out_shape=[_sds(w.shape, F32)] * 4, sem=("parallel",))(g, w, m, v)


HBM = pl.BlockSpec(memory_space=pltpu.HBM)
MESH = pl.DeviceIdType.MESH


def _place():
    x, y, c = lax.axis_index("x"), lax.axis_index("y"), lax.axis_index("c")
    return x, y, c, [(1 - x, y), (x, 1 - y), (1 - x, 1 - y)]


def _comm_call(body, name, out_shape, n_in, scratch):
    return pl.pallas_call(body, name=name, out_shape=out_shape, in_specs=[HBM] * n_in,
                          out_specs=jax.tree.map(lambda _: HBM, out_shape), scratch_shapes=scratch)


def _half_rows(rows, c, tile):
    hh = rows // 2
    assert rows % 2 == 0 and hh % tile == 0, (rows, tile)
    return pl.ds(pl.multiple_of(c * hh, tile), hh)


def _gather_copies(w_refs, out_refs, sems):
    send_sems, recv_sems = sems
    x, y, c, chips = _place()
    s_me = 2 * x + y
    sends, recvs = [], []
    for k, (w_ref, out_ref) in enumerate(zip(w_refs, out_refs)):
        half = _half_rows(w_ref.shape[0], c, 16)
        for j, (cx, cy) in enumerate(chips):
            sem = dict(send_sem=send_sems.at[3 * k + j], recv_sem=recv_sems.at[3 * k + j], device_id=(cx, cy, c), device_id_type=MESH)
            sends.append(pltpu.make_async_remote_copy(src_ref=w_ref.at[half], dst_ref=out_ref.at[s_me, half], **sem))
            recvs.append(pltpu.make_async_remote_copy(src_ref=w_ref.at[half], dst_ref=out_ref.at[2 * cx + cy, half], **sem))
    return sends, recvs


def _gather_sems(n):
    return [pltpu.SemaphoreType.DMA((3 * n,)), pltpu.SemaphoreType.DMA((3 * n,))]


def _gather_start(w_refs, out_refs, sems):
    for cp in _gather_copies(w_refs, out_refs, sems)[0]:
        cp.start()


def _gather_wait(w_refs, out_refs, sems):
    sends, recvs = _gather_copies(w_refs, out_refs, sems)
    for cp in recvs:
        cp.wait_recv()
    for cp in sends:
        cp.wait_send()


def _gather_side(shards):
    return _Side(shards, [_sds((4,) + w.shape, w.dtype) for w in shards], _gather_sems(len(shards)), _gather_start, _gather_wait)


def _forward_halves(outs, name):
    n = len(outs)

    def body(*refs):
        out_refs = refs[n:2 * n]
        send_sems, recv_sems = refs[2 * n:]
        x, y, c, chips = _place()
        cps = []
        for k in range(n):
            rows = out_refs[k].shape[1]
            half, other = _half_rows(rows, c, 16), _half_rows(rows, 1 - c, 16)
            for j, (cx, cy) in enumerate(chips):
                sem = dict(send_sem=send_sems.at[3 * k + j], recv_sem=recv_sems.at[3 * k + j], device_id=(x, y, 1 - c), device_id_type=MESH)
                landed = out_refs[k].at[2 * cx + cy, half]
                cps.append(pltpu.make_async_remote_copy(src_ref=landed, dst_ref=landed, **sem))
                cps[-1].start()
        for k in range(n):
            rows = out_refs[k].shape[1]
            half, other = _half_rows(rows, c, 16), _half_rows(rows, 1 - c, 16)
            for j, (cx, cy) in enumerate(chips):
                sem = dict(send_sem=send_sems.at[3 * k + j], recv_sem=recv_sems.at[3 * k + j], device_id=(x, y, 1 - c), device_id_type=MESH)
                pltpu.make_async_remote_copy(src_ref=out_refs[k].at[2 * cx + cy, half], dst_ref=out_refs[k].at[2 * cx + cy, other], **sem).wait_recv()
        for cp in cps:
            cp.wait_send()

    shapes = [_sds(o.shape, o.dtype) for o in outs]
    return pl.pallas_call(body, name=name, out_shape=shapes, in_specs=[HBM] * n, out_specs=[HBM] * n,
                          input_output_aliases={k: k for k in range(n)},
                          scratch_shapes=[pltpu.SemaphoreType.DMA((3 * n,)), pltpu.SemaphoreType.DMA((3 * n,))])(*outs)


def _swap_halves(gs, name):
    n = len(gs)

    def body(*refs):
        g_refs, out_refs = refs[:n], refs[n:2 * n]
        send_sems, recv_sems = refs[2 * n:]
        x, y, c, _ = _place()
        cps = []
        for k in range(n):
            other = _half_rows(g_refs[k].shape[1], 1 - c, 8)
            cps.append(pltpu.make_async_remote_copy(src_ref=g_refs[k].at[:, other, :], dst_ref=out_refs[k], send_sem=send_sems.at[k],
                                                    recv_sem=recv_sems.at[k], device_id=(x, y, 1 - c), device_id_type=MESH))
            cps[-1].start()
        for cp in cps:
            cp.wait()

    return _comm_call(body, name, [_sds((4, g.shape[1] // 2, g.shape[2]), g.dtype) for g in gs], n,
                      [pltpu.SemaphoreType.DMA((n,)), pltpu.SemaphoreType.DMA((n,))])(*gs)


def _row_tile(rows, row_bytes, budget=3 << 20):
    if rows * row_bytes <= budget:
        return rows
    return max(t for t in range(16, rows, 16) if rows % t == 0 and t * row_bytes <= budget)


def _add_half(g, got, c, name):
    _, rows, cols = g.shape
    hh = rows // 2
    tr = _row_tile(hh, cols * 4)
    nb = hh // tr

    def body(c_ref, a_ref, b_ref, o_ref):
        o_ref[...] = (a_ref[...] + b_ref[...]).astype(BF16)

    gs = pltpu.PrefetchScalarGridSpec(
        num_scalar_prefetch=1, grid=(4, nb),
        in_specs=[pl.BlockSpec((1, tr, cols), lambda s, i, c_ref: (s, c_ref[0] * nb + i, 0)),
                  pl.BlockSpec((1, tr, cols), lambda s, i, c_ref: (s, i, 0))],
        out_specs=pl.BlockSpec((1, tr, cols), lambda s, i, c_ref: (s, i, 0)))
    return pl.pallas_call(body, name=name, grid_spec=gs, out_shape=_sds((4, hh, cols), BF16),
                          compiler_params=pltpu.CompilerParams(dimension_semantics=("parallel", "parallel"),
                                                               vmem_limit_bytes=VMEM_LIMIT))(c, g, got)


def _scatter_copies(p_refs, out_refs, sems):
    send_sems, recv_sems = sems
    x, y, c, chips = _place()
    s_me = 2 * x + y
    cps = []
    for k, (p_ref, out_ref) in enumerate(zip(p_refs, out_refs)):
        for j, (cx, cy) in enumerate(chips):
            cps.append(pltpu.make_async_remote_copy(src_ref=p_ref.at[2 * cx + cy], dst_ref=out_ref.at[s_me],
                                                    send_sem=send_sems.at[3 * k + j], recv_sem=recv_sems.at[3 * k + j],
                                                    device_id=(cx, cy, c), device_id_type=MESH))
    return cps


def _scatter_start(p_refs, out_refs, sems):
    for cp in _scatter_copies(p_refs, out_refs, sems):
        cp.start()


def _scatter_wait(p_refs, out_refs, sems):
    for cp in _scatter_copies(p_refs, out_refs, sems):
        cp.wait()


def _scatter_side(ps):
    return _Side(ps, [_sds(p_.shape, p_.dtype) for p_ in ps], _gather_sems(len(ps)), _scatter_start, _scatter_wait)


def _sum_slabs(b, name):
    n, h, wd = b.shape
    tr = _row_tile(h, n * wd * 4, 6 << 20)

    def body(b_ref, o_ref):
        acc = b_ref[0]
        for s in range(1, n):
            acc = acc + b_ref[s]
        o_ref[...] = acc

    return _call(body, name=name, grid=(h // tr,), in_specs=[pl.BlockSpec((n, tr, wd), lambda i: (0, i, 0))],
                 out_specs=pl.BlockSpec((tr, wd), lambda i: (i, 0)), out_shape=_sds((h, wd), F32), sem=("parallel",))(b)


def _sum_chips(arrived, own, name):
    n, h, wd = arrived.shape
    tr = _row_tile(h, n * wd * 2, 6 << 20)
    nb = h // tr
    my_chip = lambda: 2 * lax.axis_index("x") + lax.axis_index("y")

    def body(b_ref, p_ref, o_ref):
        acc = None
        for s in range(n):
            term = jnp.where(my_chip() == s, p_ref[0], b_ref[s]).astype(F32)
            acc = term if acc is None else acc + term
        o_ref[...] = acc

    return _call(body, name=name, grid=(nb,),
                 in_specs=[pl.BlockSpec((n, tr, wd), lambda i: (0, i, 0)), pl.BlockSpec((1, tr, wd), lambda i: (my_chip(), i, 0))],
                 out_specs=pl.BlockSpec((tr, wd), lambda i: (lax.axis_index("c") * nb + i, 0)),
                 out_shape=_sds((2 * h, wd), F32), sem=("parallel",))(arrived, own)


def _share_halves(gs):
    n = len(gs)

    def body(*refs):
        out_refs = refs[n:2 * n]
        send_sems, recv_sems = refs[2 * n:]
        x, y, c, _ = _place()
        cps = []
        for k in range(n):
            half, other = _half_rows(out_refs[k].shape[0], c, 8), _half_rows(out_refs[k].shape[0], 1 - c, 8)
            sem = dict(send_sem=send_sems.at[k], recv_sem=recv_sems.at[k], device_id=(x, y, 1 - c), device_id_type=MESH)
            cps.append((pltpu.make_async_remote_copy(src_ref=out_refs[k].at[half], dst_ref=out_refs[k].at[half], **sem),
                        pltpu.make_async_remote_copy(src_ref=out_refs[k].at[half], dst_ref=out_refs[k].at[other], **sem)))
            cps[-1][0].start()
        for send, recv in cps:
            recv.wait_recv()
            send.wait_send()

    return pl.pallas_call(body, name="share_halves", out_shape=[_sds(g.shape, g.dtype) for g in gs], in_specs=[HBM] * n,
                          out_specs=[HBM] * n, input_output_aliases={k: k for k in range(n)},
                          scratch_shapes=[pltpu.SemaphoreType.DMA((n,)), pltpu.SemaphoreType.DMA((n,))])(*gs)


def _gather_all(v, name):
    def body(v_ref, out_ref, send_sems, recv_sems, local_sem):
        x, y, c = lax.axis_index("x"), lax.axis_index("y"), lax.axis_index("c")
        me = 4 * x + 2 * y + c
        flip = lambda t, d: 1 - t if d else t
        mine = pltpu.make_async_copy(v_ref, out_ref.at[me], local_sem)
        mine.start()
        cps = []
        for k in range(1, 8):
            to = (flip(x, k & 4), flip(y, k & 2), flip(c, k & 1))
            cps.append(pltpu.make_async_remote_copy(src_ref=v_ref, dst_ref=out_ref.at[me], send_sem=send_sems.at[k - 1],
                                                    recv_sem=recv_sems.at[k - 1], device_id=to, device_id_type=MESH))
        for cp in cps:
            cp.start()
        for cp in cps:
            cp.wait()
        mine.wait()

    return _comm_call(body, name, _sds((8,) + v.shape, v.dtype), 1,
                      [pltpu.SemaphoreType.DMA((7,)), pltpu.SemaphoreType.DMA((7,)), pltpu.SemaphoreType.DMA])(v)


BIG = (("w_in", 1), ("w_branch_a", 1), ("w_branch_b", 1), ("w_out", 0), ("w_ffn_in", 1), ("w_ffn_out", 0))
SMALL = ("meta_tokens", "lb_logits", "mix_norm_w", "hg_norm_w", "gd_conv_w", "gd_a_log", "gd_dt_bias", "gd_norm_w",
         "ffn_norm_w", "final_norm_w")


def _pack_lanes(parts):
    rows = []
    for p in parts:
        f = p.reshape(-1).astype(F32)
        n = -(-f.shape[0] // DH) * DH
        rows.append(jnp.pad(f, (0, n - f.shape[0])).reshape(-1, DH))
    buf = jnp.concatenate(rows, axis=0)
    return jnp.pad(buf, ((0, -buf.shape[0] % 8), (0, 0)))


def _unpack_lanes(buf, shapes):
    out, off = [], 0
    for shp in shapes:
        n = math.prod(shp)
        r = -(-n // DH)
        out.append(buf[off:off + r].reshape(-1)[:n].reshape(shp))
        off += r
    return out


def kernel(x, meta_tokens, lb_logits, mix_norm_w, w_in, hg_norm_w, gd_conv_w, gd_a_log, gd_dt_bias, gd_norm_w, w_branch_a, w_branch_b, w_out, ffn_norm_w, w_ffn_in, w_ffn_out, final_norm_w, loss_target, m_meta_tokens, m_lb_logits, m_mix_norm_w, m_w_in, m_hg_norm_w, m_gd_conv_w, m_gd_a_log, m_gd_dt_bias, m_gd_norm_w, m_w_branch_a, m_w_branch_b, m_w_out, m_ffn_norm_w, m_w_ffn_in, m_w_ffn_out, m_final_norm_w, v_meta_tokens, v_lb_logits, v_mix_norm_w, v_w_in, v_hg_norm_w, v_gd_conv_w, v_gd_a_log, v_gd_dt_bias, v_gd_norm_w, v_w_branch_a, v_w_branch_b, v_w_out, v_ffn_norm_w, v_w_ffn_in, v_w_ffn_out, v_final_norm_w):
    args = dict(locals())
    big = [n for n, _ in BIG]
    w = {n: args[n] for n in SMALL + tuple(big)}
    m = {n: args["m_" + n] for n in w}
    v = {n: args["v_" + n] for n in w}
    xi, yi, ci = lax.axis_index("x"), lax.axis_index("y"), lax.axis_index("c")
    shard = 2 * xi + yi
    big_local = {n: w[n][0] for n in big}

    meta_cols, conv_cols = meta_tokens.shape[1], gd_conv_w.shape[-1]
    sm_all = _gather_all(_pack_lanes([meta_tokens, gd_conv_w[0]]), "gather_meta")
    sm_parts = [_unpack_lanes(sm_all[2 * s], [meta_tokens.shape, gd_conv_w[0].shape]) for s in range(4)]
    meta_full = jnp.concatenate([p[0] for p in sm_parts], axis=1)
    conv_full = jnp.concatenate([p[1] for p in sm_parts], axis=1)
    cvec = ci.reshape(1).astype(jnp.int32)
    late = [n for n in big if n != "w_in"]
    rows_full = lambda t: t.reshape(t.shape[0] * t.shape[1], t.shape[2])

    def pair_sums(names, gs):
        return [_add_half(gk, got, cvec, "add_half_" + n) for n, gk, got in zip(names, gs, _swap_halves(gs, "swap_" + names[0]))]

    def with_own(slabs, n):
        return lax.dynamic_update_index_in_dim(slabs, big_local[n].astype(BF16), shard, 0)

    class MeshComm:
        def first_side(self):
            return _gather_side([big_local["w_in"].astype(BF16)])

        def first_weight(self, landed):
            slabs = with_own(_forward_halves(landed, "forward_w_in")[0], "w_in")
            return slabs.transpose(1, 0, 2).reshape(slabs.shape[1], -1)

        def proj_side(self):
            return _gather_side([big_local[n].astype(BF16) for n in late])

        def late_weights(self, landed):
            wl = {n: with_own(t, n) for n, t in zip(late, _forward_halves(landed, "forward_late"))}
            return (wl["w_branch_a"], wl["w_branch_b"], rows_full(wl["w_out"]), wl["w_ffn_in"], rows_full(wl["w_ffn_out"]))

        def early_grads_side(self, grads):
            self.early = list(grads)
            self.early_parts = pair_sums(self.early, [grads[n] for n in self.early])
            return _scatter_side(self.early_parts)

        def early_grads_done(self, arrived):
            self.early_arrived = arrived

        def last_grad_side(self, dw_in):
            self.last_parts = pair_sums(["w_in"], [dw_in])
            return _scatter_side(self.last_parts)

        def last_grad_done(self, arrived):
            self.last_arrived = arrived

    comm = MeshComm()
    loss, dx, g = _local_step(x[0], loss_target[0], meta_full, lb_logits, mix_norm_w, hg_norm_w, conv_full,
                              gd_a_log, gd_dt_bias, gd_norm_w, ffn_norm_w, final_norm_w, comm)
    loss = lax.psum(loss[0, 0], ("x", "y", "c"))

    parts = dict(zip(comm.early + ["w_in"], comm.early_parts + comm.last_parts))
    arrived = dict(zip(comm.early + ["w_in"], comm.early_arrived + comm.last_arrived))
    g_big = dict(zip(big, _share_halves([_sum_chips(arrived[n], parts[n], "sum_chips_" + n) for n in big])))

    small_shapes = [g[n].shape for n in SMALL]
    g_all = _gather_all(_pack_lanes([g[n] for n in SMALL]), "gather_small")
    g_small = dict(zip(SMALL, _unpack_lanes(_sum_slabs(g_all, "sum_small"), small_shapes)))
    g_small["meta_tokens"] = lax.dynamic_slice_in_dim(g_small["meta_tokens"], shard * meta_cols, meta_cols, axis=1)
    g_small["gd_conv_w"] = lax.dynamic_slice_in_dim(g_small["gd_conv_w"], shard * conv_cols, conv_cols, axis=1)

    grad, delta, new_m, new_v = {}, {}, {}, {}
    for n in big:
        grad[n], delta[n], new_m[n], new_v[n] = _adamw(g_big[n], w[n], m[n], v[n], "adamw_" + n)
    local_shapes = [w[n].shape for n in SMALL]
    _, d_, m_, v_ = _adamw(_pack_lanes([g_small[n] for n in SMALL]), _pack_lanes([w[n] for n in SMALL]),
                           _pack_lanes([m[n] for n in SMALL]), _pack_lanes([v[n] for n in SMALL]), "adamw_small")
    for n, gs_, dd, mm, vv in zip(SMALL, [g_small[n] for n in SMALL], _unpack_lanes(d_, local_shapes), _unpack_lanes(m_, local_shapes),
                                  _unpack_lanes(v_, local_shapes)):
        grad[n], delta[n], new_m[n], new_v[n] = gs_.reshape(w[n].shape), dd, mm, vv

    order = ["meta_tokens", "lb_logits", "mix_norm_w", "w_in", "hg_norm_w", "gd_conv_w", "gd_a_log", "gd_dt_bias", "gd_norm_w",
             "w_branch_a", "w_branch_b", "w_out", "ffn_norm_w", "w_ffn_in", "w_ffn_out", "final_norm_w"]
    return (loss, dx[None], *[grad[n] for n in order], *[delta[n] for n in order], *[new_m[n] for n in order],
            *[new_v[n] for n in order])
```

```python
import functools
import math

import jax
import jax.numpy as jnp
from jax import lax
from jax.experimental import pallas as pl
from jax.experimental.pallas import tpu as pltpu

F32, BF16 = jnp.float32, jnp.bfloat16
EPS = 1e-6
D_MODEL = 2048
N_META = 16
FRONT = 256
CH = 64
SUB = 16
SUBH = 16
DH = 128
NH = 8
HW = NH * DH
CONV_K = 4
RT = 256
VMEM_LIMIT = 56 * 1024 * 1024
ADAM_LR, ADAM_B1, ADAM_B2, ADAM_EPS, ADAM_WD, ADAM_STEP = 0.001, 0.9, 0.999, 1e-08, 0.01, 10

NN = (((1,), (0,)), ((), ()))
NT = (((1,), (1,)), ((), ()))
TN = (((0,), (0,)), ((), ()))


def _dot(a, b, dn=NN):
    return lax.dot_general(a.astype(BF16), b.astype(BF16), dn, preferred_element_type=F32)


def _dotx(a, b, dn=NN):
    return lax.dot_general(a, b, dn, precision=lax.Precision.HIGHEST, preferred_element_type=F32)


class _Side:
    def __init__(self, inputs, out_shapes, scratch, start, wait):
        self.inputs, self.out_shapes, self.scratch, self.start, self.wait = inputs, out_shapes, scratch, start, wait


def _call(body, *, name, grid, in_specs, out_specs, out_shape, scratch=(), sem=None, side=None):
    params = pltpu.CompilerParams(dimension_semantics=sem, vmem_limit_bytes=VMEM_LIMIT)
    if side is None:
        return pl.pallas_call(body, name=name, grid=grid, in_specs=in_specs, out_specs=out_specs, out_shape=out_shape,
                              scratch_shapes=list(scratch), compiler_params=params)
    single = not isinstance(out_specs, (list, tuple))
    out_specs, out_shape = ([out_specs], [out_shape]) if single else (list(out_specs), list(out_shape))
    ni, no, ns = len(in_specs), len(out_specs), len(scratch)
    nsi, nso = len(side.inputs), len(side.out_shapes)
    hbm = pl.BlockSpec(memory_space=pltpu.HBM)

    def wrapped(*refs):
        main_in, side_in = refs[:ni], refs[ni:ni + nsi]
        main_out, side_out = refs[ni + nsi:ni + nsi + no], refs[ni + nsi + no:ni + nsi + no + nso]
        main_scr, side_scr = refs[ni + nsi + no + nso:ni + nsi + no + nso + ns], refs[ni + nsi + no + nso + ns:]
        pids = [pl.program_id(d) for d in range(len(grid))]
        first = functools.reduce(lambda a, b: a & b, [p == 0 for p in pids])
        last = functools.reduce(lambda a, b: a & b, [p == g - 1 for p, g in zip(pids, grid)])

        @pl.when(first)
        def _():
            side.start(side_in, side_out, side_scr)

        body(*main_in, *main_out, *main_scr)

        @pl.when(last)
        def _():
            side.wait(side_in, side_out, side_scr)

    call = pl.pallas_call(wrapped, name=name, grid=grid, in_specs=list(in_specs) + [hbm] * nsi,
                          out_specs=out_specs + [hbm] * nso, out_shape=out_shape + list(side.out_shapes),
                          scratch_shapes=list(scratch) + list(side.scratch), compiler_params=params)

    def run(*args):
        outs = call(*args, *side.inputs)
        main = outs[0] if single else list(outs[:no])
        return main, list(outs[no:])

    return run


def _divmod(j, per):
    if per == 1:
        return j, 0
    return lax.div(j, jnp.int32(per)), lax.rem(j, jnp.int32(per))


def _sds(shape, dtype):
    return jax.ShapeDtypeStruct(tuple(shape), dtype)


def _sigmoid(x):
    return 0.5 * jnp.tanh(0.5 * x) + 0.5


def _silu(x):
    return x * _sigmoid(x)


def _dsilu(x):
    s = _sigmoid(x)
    return s * (1.0 + x * (1.0 - s))


def _tri(n, kind):
    r = lax.broadcasted_iota(jnp.int32, (n, n), 0)
    c = lax.broadcasted_iota(jnp.int32, (n, n), 1)
    return {"incl": r >= c, "strict": r > c, "upper": c >= r}[kind]


def _mm(a, b, mode, out_dtype, tm, tn, tk, name, add=None, n_outer=False, out_shards=0, side=None):
    sharded_a = a.ndim == 3
    if sharded_a:
        n_a = a.shape[2]
        a_shape = (a.shape[1], a.shape[0] * n_a)
    else:
        a_shape = a.shape
    sharded_b = b.ndim == 3
    if sharded_b:
        S, R, n = b.shape
        b_rows, b_cols = R, S * n
    else:
        b_rows, b_cols = b.shape
    if mode == "nn":
        (M, K), N, dn = a_shape, b_cols, NN
    elif mode == "nt":
        (M, K), N, dn = a_shape, b_rows, NT
    else:
        (K, M), N, dn = a_shape, b_cols, TN
    tm, tn, tk = min(tm, M), min(tn, N), min(tk, K)
    if sharded_a:
        tk = min(tk, n_a)
    if sharded_b:
        tn, tk = (min(tn, n), tk) if mode != "nt" else (tn, min(tk, n))
    if out_shards:
        tn = min(tn, N // out_shards)
    assert M % tm == 0 and N % tn == 0 and K % tk == 0, (name, M, N, K, tm, tn, tk)
    nk = K // tk
    a_blk, a_idx = ((tm, tk), lambda i, j, k: (i, k)) if mode != "tn" else ((tk, tm), lambda i, j, k: (k, i))
    if sharded_a:
        per_a = n_a // tk
        assert mode != "tn" and n_a % tk == 0
        a_blk, a_idx = (None, tm, tk), lambda i, j, k: (_divmod(k, per_a)[0], i, _divmod(k, per_a)[1])
    if not sharded_b:
        b_blk, b_idx = ((tk, tn), lambda i, j, k: (k, j)) if mode != "nt" else ((tn, tk), lambda i, j, k: (j, k))
    elif mode != "nt":
        per = n // tn
        assert n % tn == 0
        b_blk, b_idx = (None, tk, tn), lambda i, j, k: (_divmod(j, per)[0], k, _divmod(j, per)[1])
    else:
        per = n // tk
        assert n % tk == 0
        b_blk, b_idx = (None, tn, tk), lambda i, j, k: (_divmod(k, per)[0], j, _divmod(k, per)[1])
    if out_shards:
        per_o = N // out_shards // tn
        assert (N // out_shards) % tn == 0
        o_blk, o_idx = (None, tm, tn), lambda i, j, k: (_divmod(j, per_o)[0], i, _divmod(j, per_o)[1])
        o_shape = (out_shards, M, N // out_shards)
    else:
        o_blk, o_idx, o_shape = (tm, tn), (lambda i, j, k: (i, j)), (M, N)
    c_idx = lambda i, j, k: (i, j)
    if n_outer:
        sw = lambda f: (lambda j, i, k: f(i, j, k))
        a_idx, b_idx, o_idx, c_idx = sw(a_idx), sw(b_idx), sw(o_idx), sw(c_idx)
        grid = (N // tn, M // tm, nk)
    else:
        grid = (M // tm, N // tn, nk)
    has_add = add is not None

    def body(*refs):
        if has_add:
            a_ref, b_ref, c_ref, o_ref, acc_ref = refs
        else:
            a_ref, b_ref, o_ref, acc_ref = refs
            c_ref = None
        part = lax.dot_general(a_ref[...].astype(BF16), b_ref[...].astype(BF16), dn, preferred_element_type=F32)

        def fin(val):
            if has_add:
                val = val + c_ref[...]
            o_ref[...] = val.astype(out_dtype)

        if nk == 1:
            fin(part)
        else:
            k = pl.program_id(2)

            @pl.when(k == 0)
            def _():
                acc_ref[...] = part

            @pl.when(k > 0)
            def _():
                acc_ref[...] += part

            @pl.when(k == nk - 1)
            def _():
                fin(acc_ref[...])

    in_specs = [pl.BlockSpec(a_blk, a_idx), pl.BlockSpec(b_blk, b_idx)]
    args = [a, b]
    if has_add:
        in_specs.append(pl.BlockSpec((tm, tn), c_idx))
        args.append(add)
    acc_shape = (tm, tn) if nk > 1 else (8, 128)
    return _call(body, name=name, grid=grid, in_specs=in_specs, out_specs=pl.BlockSpec(o_blk, o_idx),
                 out_shape=_sds(o_shape, out_dtype), scratch=[pltpu.VMEM(acc_shape, F32)],
                 sem=("arbitrary",) * 3 if side is not None else ("parallel", "parallel", "arbitrary"), side=side)(*args)


def _rms1_fwd(x, meta, w, side=None):
    seq, d = x.shape
    nt = (FRONT + seq) // RT

    def body(x_ref, m_ref, w_ref, o_ref):
        i = pl.program_id(0)

        def norm(v):
            r = lax.rsqrt(jnp.mean(v * v, axis=-1, keepdims=True) + EPS)
            return (v * r * w_ref[...]).astype(BF16)

        @pl.when(i == 0)
        def _():
            o_ref[0:RT - N_META, :] = jnp.zeros((RT - N_META, d), BF16)
            o_ref[RT - N_META:RT, :] = norm(m_ref[...])

        @pl.when(i > 0)
        def _():
            o_ref[...] = norm(x_ref[...])

    return _call(body, name="rms1_fwd", grid=(nt,),
                 in_specs=[pl.BlockSpec((RT, d), lambda i: (jnp.maximum(i - 1, 0), 0)),
                           pl.BlockSpec((N_META, d), lambda i: (0, 0)),
                           pl.BlockSpec((1, d), lambda i: (0, 0))],
                 out_specs=pl.BlockSpec((RT, d), lambda i: (i, 0)),
                 out_shape=_sds((FRONT + seq, d), BF16), sem=("arbitrary",) if side is not None else ("parallel",),
                 side=side)(x, meta, w)


def _rms1_bwd(x, meta, w, dxn, dh1):
    seq, d = x.shape
    nt = (FRONT + seq) // RT

    def body(x_ref, m_ref, w_ref, g_ref, r_ref, dx_ref, dm_ref, dw_ref):
        i = pl.program_id(0)

        def bwd(v, g):
            r = lax.rsqrt(jnp.mean(v * v, axis=-1, keepdims=True) + EPS)
            vh = v * r
            gh = g * w_ref[...]
            return r * (gh - vh * jnp.mean(gh * vh, axis=-1, keepdims=True)), jnp.sum(g * vh, axis=0, keepdims=True)

        @pl.when(i == 0)
        def _():
            dm, dw = bwd(m_ref[...], g_ref[RT - N_META:RT, :])
            dm_ref[...] = dm
            dw_ref[...] = dw

        @pl.when(i > 0)
        def _():
            dx, dw = bwd(x_ref[...], g_ref[...])
            dx_ref[...] = dx + r_ref[...]
            dw_ref[...] += dw

    xs = pl.BlockSpec((RT, d), lambda i: (jnp.maximum(i - 1, 0), 0))
    return _call(body, name="rms1_bwd", grid=(nt,),
                 in_specs=[xs, pl.BlockSpec((N_META, d), lambda i: (0, 0)), pl.BlockSpec((1, d), lambda i: (0, 0)),
                           pl.BlockSpec((RT, d), lambda i: (i, 0)), xs],
                 out_specs=[xs, pl.BlockSpec((N_META, d), lambda i: (0, 0)), pl.BlockSpec((1, d), lambda i: (0, 0))],
                 out_shape=[_sds((seq, d), F32), _sds((N_META, d), F32), _sds((1, d), F32)],
                 sem=("arbitrary",))(x, meta, w, dxn, dh1)


def _merge_fwd(proj, za, zb):
    seq, d = za.shape
    off = FRONT // RT
    ca, cb = 8 * HW // d, 8 * HW // d + 1

    def body(ga_ref, gb_ref, za_ref, zb_ref, o_ref):
        o_ref[...] = (_sigmoid(ga_ref[...]) * za_ref[...] + _sigmoid(gb_ref[...]) * zb_ref[...]).astype(BF16)

    zs = pl.BlockSpec((RT, d), lambda i: (i, 0))
    return _call(body, name="merge_fwd", grid=(seq // RT,),
                 in_specs=[pl.BlockSpec((RT, d), lambda i: (i + off, ca)), pl.BlockSpec((RT, d), lambda i: (i + off, cb)), zs, zs],
                 out_specs=zs, out_shape=_sds((seq, d), BF16), sem=("parallel",))(proj, proj, za, zb)


def _merge_bwd(proj, za, zb, dmerged):
    seq, d = za.shape
    off = FRONT // RT
    ca, cb = 8 * HW // d, 8 * HW // d + 1
    nt = (FRONT + seq) // RT

    def body(ga_ref, gb_ref, za_ref, zb_ref, dm_ref, dza_ref, dzb_ref, dg_ref):
        i = pl.program_id(0)

        @pl.when(i < off)
        def _():
            dg_ref[...] = jnp.zeros((RT, 2 * d), BF16)

        @pl.when(i >= off)
        def _():
            sa, sb, dm = _sigmoid(ga_ref[...]), _sigmoid(gb_ref[...]), dm_ref[...]
            dza_ref[...] = (dm * sa).astype(BF16)
            dzb_ref[...] = (dm * sb).astype(BF16)
            dg_ref[:, 0:d] = (dm * za_ref[...] * sa * (1.0 - sa)).astype(BF16)
            dg_ref[:, d:2 * d] = (dm * zb_ref[...] * sb * (1.0 - sb)).astype(BF16)

    rs = pl.BlockSpec((RT, d), lambda i: (jnp.maximum(i - off, 0), 0))
    return _call(body, name="merge_bwd", grid=(nt,),
                 in_specs=[pl.BlockSpec((RT, d), lambda i: (i, ca)), pl.BlockSpec((RT, d), lambda i: (i, cb)), rs, rs, rs],
                 out_specs=[rs, rs, pl.BlockSpec((RT, 2 * d), lambda i: (i, 0))],
                 out_shape=[_sds((seq, d), BF16), _sds((seq, d), BF16), _sds((FRONT + seq, 2 * d), BF16)],
                 sem=("arbitrary",))(proj, proj, za, zb, dmerged)


def _resid_norm_fwd(x, mix, w):
    seq, d = x.shape

    def body(x_ref, m_ref, w_ref, h_ref, n_ref):
        h = x_ref[...] + m_ref[...]
        h_ref[...] = h
        r = lax.rsqrt(jnp.mean(h * h, axis=-1, keepdims=True) + EPS)
        n_ref[...] = (h * r * w_ref[...]).astype(BF16)

    rs = pl.BlockSpec((RT, d), lambda i: (i, 0))
    return _call(body, name="resid_norm_fwd", grid=(seq // RT,),
                 in_specs=[rs, rs, pl.BlockSpec((1, d), lambda i: (0, 0))], out_specs=[rs, rs],
                 out_shape=[_sds((seq, d), F32), _sds((seq, d), BF16)], sem=("parallel",))(x, mix, w)


def _resid_norm_bwd(h1, w, dn2, dh2):
    seq, d = h1.shape

    def body(h_ref, w_ref, g_ref, r_ref, o_ref, ob_ref, dw_ref):
        i = pl.program_id(0)
        h, g = h_ref[...], g_ref[...]
        r = lax.rsqrt(jnp.mean(h * h, axis=-1, keepdims=True) + EPS)
        hh = h * r
        gh = g * w_ref[...]
        dh = r_ref[...] + r * (gh - hh * jnp.mean(gh * hh, axis=-1, keepdims=True))
        o_ref[...] = dh
        ob_ref[...] = dh.astype(BF16)
        dw = jnp.sum(g * hh, axis=0, keepdims=True)

        @pl.when(i == 0)
        def _():
            dw_ref[...] = dw

        @pl.when(i > 0)
        def _():
            dw_ref[...] += dw

    rs = pl.BlockSpec((RT, d), lambda i: (i, 0))
    ws = pl.BlockSpec((1, d), lambda i: (0, 0))
    return _call(body, name="resid_norm_bwd", grid=(seq // RT,), in_specs=[rs, ws, rs, rs], out_specs=[rs, rs, ws],
                 out_shape=[_sds((seq, d), F32), _sds((seq, d), BF16), _sds((1, d), F32)], sem=("arbitrary",))(h1, w, dn2, dh2)


def _swiglu_tiles(seq, ff):
    return min(512, seq), (1408 if ff % 1408 == 0 else 512)


def _swiglu_fwd(gu):
    seq, f2 = gu.shape
    ff = f2 // 2
    rt, tc = _swiglu_tiles(seq, ff)
    nb = ff // tc

    def body(g_ref, u_ref, o_ref):
        o_ref[...] = (_silu(g_ref[...].astype(F32)) * u_ref[...].astype(F32)).astype(BF16)

    return _call(body, name="swiglu_fwd", grid=(seq // rt, nb),
                 in_specs=[pl.BlockSpec((rt, tc), lambda i, j: (i, j)), pl.BlockSpec((rt, tc), lambda i, j: (i, j + nb))],
                 out_specs=pl.BlockSpec((rt, tc), lambda i, j: (i, j)), out_shape=_sds((seq, ff), BF16),
                 sem=("parallel", "parallel"))(gu, gu)


def _swiglu_bwd(gu, dact):
    seq, f2 = gu.shape
    ff = f2 // 2
    rt, tc = _swiglu_tiles(seq, ff)
    nb = ff // tc

    def body(g_ref, u_ref, d_ref, o_ref):
        g, d = g_ref[...].astype(F32), d_ref[...].astype(F32)
        o_ref[0] = (d * u_ref[...].astype(F32) * _dsilu(g)).astype(BF16)
        o_ref[1] = (d * _silu(g)).astype(BF16)

    bs = pl.BlockSpec((rt, tc), lambda i, j: (i, j))
    return _call(body, name="swiglu_bwd", grid=(seq // rt, nb),
                 in_specs=[bs, pl.BlockSpec((rt, tc), lambda i, j: (i, j + nb)), bs],
                 out_specs=pl.BlockSpec((2, rt, tc), lambda i, j: (0, i, j)),
                 out_shape=_sds((2, seq, ff), BF16), sem=("parallel", "parallel"))(gu, gu, dact)


def _loss_head(h1, f, w, tgt):
    seq, d = h1.shape
    nt = seq // RT

    def body(h_ref, f_ref, w_ref, t_ref, l_ref, dh_ref, dhb_ref, dw_ref):
        i = pl.program_id(0)
        h = h_ref[...] + f_ref[...]
        r = lax.rsqrt(jnp.mean(h * h, axis=-1, keepdims=True) + EPS)
        hh = h * r
        err = hh * w_ref[...] - t_ref[...]
        l_ref[...] = jnp.full((8, 128), 0.5 * jnp.sum(jnp.mean(err * err, axis=-1, keepdims=True)), F32)
        dy = err * (1.0 / d)
        gh = dy * w_ref[...]
        dh = r * (gh - hh * jnp.mean(gh * hh, axis=-1, keepdims=True))
        dh_ref[...] = dh
        dhb_ref[...] = dh.astype(BF16)
        dw = jnp.sum(dy * hh, axis=0, keepdims=True)

        @pl.when(i == 0)
        def _():
            dw_ref[...] = dw

        @pl.when(i > 0)
        def _():
            dw_ref[...] += dw

    rs = pl.BlockSpec((RT, d), lambda i: (i, 0))
    ws = pl.BlockSpec((1, d), lambda i: (0, 0))
    return _call(body, name="loss_head", grid=(nt,), in_specs=[rs, rs, ws, rs],
                 out_specs=[pl.BlockSpec((8, 128), lambda i: (i, 0)), rs, rs, ws],
                 out_shape=[_sds((nt * 8, 128), F32), _sds((seq, d), F32), _sds((seq, d), BF16), _sds((1, d), F32)],
                 sem=("arbitrary",))(h1, f, w, tgt)


def _sum_tiles(lt):
    n = lt.shape[0]

    def body(l_ref, o_ref):
        v = l_ref[...]
        r = lax.broadcasted_iota(jnp.int32, v.shape, 0)
        c = lax.broadcasted_iota(jnp.int32, v.shape, 1)
        o_ref[...] = jnp.sum(jnp.where((r % 8 == 0) & (c == 0), v, 0.0), keepdims=True)

    return _call(body, name="loss_sum", grid=(1,), in_specs=[pl.BlockSpec((n, 128), lambda i: (0, 0))],
                 out_specs=pl.BlockSpec((1, 1), lambda i: (0, 0)), out_shape=_sds((1, 1), F32))(lt)


def _gated_norm_fwd(o, g, nw):
    r = lax.rsqrt(jnp.mean(o * o, axis=-1, keepdims=True) + EPS)
    return o * r * nw * _silu(g)


def _gated_norm_bwd(o, g, nw, dout):
    r = lax.rsqrt(jnp.mean(o * o, axis=-1, keepdims=True) + EPS)
    oh = o * r
    don = dout * _silu(g)
    dg = dout * (oh * nw) * _dsilu(g)
    dnw = jnp.sum(don * oh, axis=0, keepdims=True)
    doh = don * nw
    return r * (doh - oh * jnp.mean(doh * oh, axis=-1, keepdims=True)), dg, dnw


def _hg_gates(fs, lbl):
    l0, l1 = lbl[0:1, :], lbl[1:2, :]
    m = jnp.maximum(l0, l1)
    e0, e1 = jnp.exp(l0 - m), jnp.exp(l1 - m)
    lb = e0 / (e0 + e1)
    sig = _sigmoid(fs)
    f = lb + (1.0 - lb) * sig
    return lb, sig, f, jnp.log(f), (1.0 - lb) * _sigmoid(-fs)


def _cumsum(w):
    row = lax.broadcasted_iota(jnp.int32, w.shape, 0) & (CH - 1)
    s = 1
    while s < CH:
        w = w + jnp.where(row >= s, pltpu.roll(w, s, 0), 0.0)
        s *= 2
    return w


def _rcumsum(w):
    row = lax.broadcasted_iota(jnp.int32, w.shape, 0) & (CH - 1)
    s = 1
    while s < CH:
        w = w + jnp.where(row < CH - s, pltpu.roll(w, w.shape[0] - s, 0), 0.0)
        s *= 2
    return w


def _decay_blocks(q, k, b, p_ref):
    p_ref[...] = jnp.zeros((CH, CH), F32)
    m16 = _tri(SUBH, "incl")
    for I in range(CH // SUBH):
        s0 = I * SUBH
        bI, qI, kI = b[s0:s0 + SUBH], q[s0:s0 + SUBH], k[s0:s0 + SUBH]
        dec = jnp.exp(jnp.minimum(bI[:, None, :] - bI[None, :, :], 0.0))
        pii = jnp.sum(qI[:, None, :] * kI[None, :, :] * dec, axis=-1)
        p_ref[s0:s0 + SUBH, s0:s0 + SUBH] = jnp.where(m16, pii, 0.0)
        if I > 0:
            rI = b[s0 - 1:s0]
            qs = qI * jnp.exp(bI - rI)
            ks = k[0:s0] * jnp.exp(rI - b[0:s0])
            p_ref[s0:s0 + SUBH, 0:s0] = _dot(qs, ks, NT)


HPS = 8
HPS_HGRN2_FWD = 4
HB = 4
NHB = NH // HPS
OFF = FRONT // RT


def _head_specs(hps, rev=None):
    row = (lambda i: i) if rev is None else rev
    col = lambda g: pl.BlockSpec((RT, hps * DH), lambda h, i: (row(i), g * (NH // hps) + h))
    full = pl.BlockSpec((RT, hps * DH), lambda h, i: (row(i), h))
    real = pl.BlockSpec((RT, hps * DH), lambda h, i: (jnp.maximum(row(i) - OFF, 0), h))
    state = lambda cpt: pl.BlockSpec((hps, cpt, DH, DH), lambda h, i: (h, row(i), 0, 0))
    scal = pl.BlockSpec((hps, RT, DH), lambda h, i: (h, row(i), 0))
    return col, full, real, state, scal


def _hgrn2_fwd(proj, lb_logits, nw):
    tp = proj.shape[0]
    nt, cpt = tp // RT, RT // CH
    hps = HPS_HGRN2_FWD

    def body(q_ref, f_ref, i_ref, g_ref, lbl_ref, nw_ref, og_ref, or_ref, st_ref, pm_ref, s_ref, p_ref):
        @pl.when(pl.program_id(1) == 0)
        def _():
            s_ref[...] = jnp.zeros((hps, DH, DH), F32)

        def chunk(c, carry):
            rows = pl.ds(pl.multiple_of(c * CH, CH), CH)
            for hh in range(hps):
                cols = slice(hh * DH, (hh + 1) * DH)
                _, _, _, w, k = _hg_gates(f_ref[rows, cols], lbl_ref[:, cols])
                q, v = _silu(q_ref[rows, cols]), i_ref[rows, cols]
                b = _cumsum(w)
                st = s_ref[hh]
                st_ref[hh, c] = st
                _decay_blocks(q, k, b, p_ref.at[hh])
                pm_ref[hh, c] = p_ref[hh]
                o = _dot(q * jnp.exp(b), st, NT) + _dot(p_ref[hh], v)
                bl = b[CH - 1:CH]
                s_ref[hh] = st * jnp.exp(bl) + _dot(v, k * jnp.exp(bl - b), TN)
                or_ref[rows, cols] = o
                og_ref[rows, cols] = _gated_norm_fwd(o, g_ref[rows, cols], nw_ref[...]).astype(BF16)
            return carry

        lax.fori_loop(0, cpt, chunk, 0)

    col, full, real, state, _ = _head_specs(hps)
    return _call(body, name="hgrn2_fwd", grid=(NH // hps, nt),
                 in_specs=[col(0), col(1), col(2), col(3), pl.BlockSpec((2, hps * DH), lambda h, i: (0, h)),
                           pl.BlockSpec((1, DH), lambda h, i: (0, 0))],
                 out_specs=[real, full, state(cpt), pl.BlockSpec((hps, cpt, CH, CH), lambda h, i: (h, i, 0, 0))],
                 out_shape=[_sds((tp - FRONT, HW), BF16), _sds((tp, HW), F32), _sds((NH, tp // CH, DH, DH), F32),
                            _sds((NH, tp // CH, CH, CH), F32)],
                 scratch=[pltpu.VMEM((hps, DH, DH), F32), pltpu.VMEM((hps, CH, CH), F32)],
                 sem=("parallel", "arbitrary"))(proj, proj, proj, proj, lb_logits, nw)


def _hgrn2_bwd(proj, lb_logits, nw, o_raw, states, pmat, dog, side=None):
    tp = proj.shape[0]
    nt, cpt = tp // RT, RT // CH

    def body(q_ref, f_ref, i_ref, g_ref, lbl_ref, nw_ref, or_ref, st_ref, pm_ref, dog_ref,
             dq_ref, df_ref, di_ref, dg_ref, dl_ref, dnw_ref, ds_ref, dk_ref, dqa_ref, do_ref):
        step = pl.program_id(1)

        @pl.when(step == 0)
        def _():
            ds_ref[...] = jnp.zeros((HPS, DH, DH), F32)
            dl_ref[...] = jnp.zeros((2, HPS * DH), F32)

        @pl.when((step == 0) & (pl.program_id(0) == 0))
        def _():
            dnw_ref[...] = jnp.zeros((1, DH), F32)

        front = nt - 1 - step < OFF
        for hh in range(HPS):
            cols = slice(hh * DH, (hh + 1) * DH)
            dog_t = jnp.where(front, 0.0, dog_ref[:, cols])
            do_t, dg_t, dnw = _gated_norm_bwd(or_ref[:, cols], g_ref[:, cols], nw_ref[...], dog_t)
            do_ref[:, cols] = do_t
            dg_ref[:, cols] = dg_t.astype(BF16)
            dnw_ref[...] += dnw
        tril = _tri(CH, "incl")
        m16 = _tri(SUBH, "incl")

        def chunk(cc, carry):
            c = cpt - 1 - cc
            rows = pl.ds(pl.multiple_of(c * CH, CH), CH)
            for hh in range(HPS):
                cols = slice(hh * DH, (hh + 1) * DH)
                fs = f_ref[rows, cols]
                lb, sig, f, w, k = _hg_gates(fs, lbl_ref[:, cols])
                hq = q_ref[rows, cols]
                q, v, do = _silu(hq), i_ref[rows, cols], do_ref[rows, cols]
                b = _cumsum(w)
                bl = b[CH - 1:CH]
                eb = jnp.exp(b)
                qs, kd = q * eb, k * jnp.exp(bl - b)
                st, dst = st_ref[hh, c], ds_ref[hh]
                dv = _dot(pm_ref[hh, c], do, TN) + _dot(kd, dst, NT)
                dp = jnp.where(tril, _dot(do, v, NT), 0.0)
                dqa, dka = dqa_ref.at[hh], dk_ref.at[hh]
                dqa[...] = eb * _dotx(do, st)
                dka[...] = jnp.exp(bl - b) * _dotx(v, dst)
                for I in range(CH // SUBH):
                    s0 = I * SUBH
                    bI, qI, kI = b[s0:s0 + SUBH], q[s0:s0 + SUBH], k[s0:s0 + SUBH]
                    dec = jnp.exp(jnp.minimum(bI[:, None, :] - bI[None, :, :], 0.0))
                    dpii = jnp.where(m16, dp[s0:s0 + SUBH, s0:s0 + SUBH], 0.0)[:, :, None] * dec
                    dqa[s0:s0 + SUBH, :] += jnp.sum(dpii * kI[None, :, :], axis=1)
                    dka[s0:s0 + SUBH, :] += jnp.sum(dpii * qI[:, None, :], axis=0)
                    if I > 0:
                        rI = b[s0 - 1:s0]
                        eq, ek = jnp.exp(bI - rI), jnp.exp(rI - b[0:s0])
                        dpij = dp[s0:s0 + SUBH, 0:s0]
                        dqa[s0:s0 + SUBH, :] += eq * _dotx(dpij, k[0:s0] * ek)
                        dka[0:s0, :] += ek * _dotx(dpij, qI * eq, TN)
                dq, dk = dqa[...], dka[...]
                st_end = st * jnp.exp(bl) + _dotx(v, kd, TN)
                dw = _rcumsum(q * dq - k * dk) + jnp.sum(dst * st_end, axis=0, keepdims=True)
                ds_ref[hh] = dst * jnp.exp(bl) + _dotx(do, qs, TN)
                one_m = 1.0 - sig
                dq_ref[rows, cols] = (dq * _dsilu(hq)).astype(BF16)
                df_ref[rows, cols] = ((dw / f - dk) * (1.0 - lb) * sig * one_m).astype(BF16)
                di_ref[rows, cols] = dv.astype(BF16)
                dl_ref[0:1, cols] += jnp.sum((dw / f - dk) * one_m, axis=0, keepdims=True)
            return carry

        lax.fori_loop(0, cpt, chunk, 0)

        @pl.when(step == nt - 1)
        def _():
            lbl = lbl_ref[...]
            l0, l1 = lbl[0:1, :], lbl[1:2, :]
            m = jnp.maximum(l0, l1)
            e0, e1 = jnp.exp(l0 - m), jnp.exp(l1 - m)
            p0 = e0 / (e0 + e1)
            dl0 = dl_ref[0:1, :] * p0 * (1.0 - p0)
            dl_ref[0:1, :] = dl0
            dl_ref[1:2, :] = -dl0

    col, full, real, state, _ = _head_specs(HPS, lambda i: nt - 1 - i)
    lbs = pl.BlockSpec((2, HPS * DH), lambda h, i: (0, h))
    return _call(body, name="hgrn2_bwd", grid=(NHB, nt),
                 in_specs=[col(0), col(1), col(2), col(3), lbs, pl.BlockSpec((1, DH), lambda h, i: (0, 0)), full,
                           state(cpt), pl.BlockSpec((HPS, cpt, CH, CH), lambda h, i: (h, nt - 1 - i, 0, 0)), real],
                 out_specs=[full, full, full, full, lbs, pl.BlockSpec((1, DH), lambda h, i: (0, 0))],
                 out_shape=[_sds((tp, HW), BF16)] * 4 + [_sds((2, HW), F32), _sds((1, DH), F32)],
                 scratch=[pltpu.VMEM((HPS, DH, DH), F32), pltpu.VMEM((HPS, CH, DH), F32),
                          pltpu.VMEM((HPS, CH, DH), F32), pltpu.VMEM((RT, HPS * DH), F32)],
                 sem=("arbitrary", "arbitrary"), side=side)(proj, proj, proj, proj, lb_logits, nw, o_raw, states, pmat, dog)


GQ0 = 4 * HW
CW = 3 * HW


def _gd_scalars(ab, alog, dtb):
    g = -jnp.exp(alog) * jax.nn.softplus(ab + dtb)
    return g, _sigmoid(ab)


def _conv_ext_specs(row_of):
    main = [pl.BlockSpec((RT, HW), lambda i, g=g: (row_of(i), GQ0 // HW + g)) for g in range(3)]
    prev = [pl.BlockSpec((8, HW), lambda i, g=g: (jnp.maximum(row_of(i) * (RT // 8) - 1, 0), GQ0 // HW + g)) for g in range(3)]
    return main + prev


def _conv_fill(ext_ref, xs, xps, first):
    for g in range(3):
        ext_ref[0:8, g * HW:(g + 1) * HW] = jnp.where(first, 0.0, xps[g][...])
        ext_ref[8:8 + RT, g * HW:(g + 1) * HW] = xs[g][...]


def _conv_apply(ext_ref, cw):
    y = cw[CONV_K - 1:CONV_K, :] * ext_ref[pl.ds(8, RT), :]
    for s in range(1, CONV_K):
        y += cw[CONV_K - 1 - s:CONV_K - s, :] * ext_ref[pl.ds(8 - s, RT), :]
    return y


def _gdn_prep_fwd(proj, pab, conv_w, alog, dtb):
    tp = proj.shape[0]
    nt = tp // RT

    def body(x0, x1, x2, p0, p1, p2, ab_ref, cw_ref, al_ref, dt_ref, q_ref, k_ref, v_ref, g_ref, b_ref, ext_ref):
        _conv_fill(ext_ref, (x0, x1, x2), (p0, p1, p2), pl.program_id(0) == 0)
        a = _silu(_conv_apply(ext_ref, cw_ref[...]))
        for h in range(NH):
            for part, ref, sc in ((0, q_ref, DH ** -0.5), (1, k_ref, 1.0)):
                seg = a[:, part * HW + h * DH:part * HW + (h + 1) * DH]
                ref[:, h * DH:(h + 1) * DH] = seg * (lax.rsqrt(jnp.sum(seg * seg, axis=-1, keepdims=True) + EPS) * sc)
        v_ref[...] = a[:, 2 * HW:3 * HW]
        g, beta = _gd_scalars(ab_ref[...], al_ref[...], dt_ref[...])
        for h in range(NH):
            g_ref[h] = jnp.broadcast_to(g[:, h:h + 1], (RT, DH))
            b_ref[h] = jnp.broadcast_to(beta[:, NH + h:NH + h + 1], (RT, DH))

    hs = pl.BlockSpec((RT, HW), lambda i: (i, 0))
    sc = pl.BlockSpec((NH, RT, DH), lambda i: (0, i, 0))
    one = pl.BlockSpec((1, DH), lambda i: (0, 0))
    return _call(body, name="gdn_prep_fwd", grid=(nt,),
                 in_specs=_conv_ext_specs(lambda i: i) + [pl.BlockSpec((RT, DH), lambda i: (i, 0)),
                                                           pl.BlockSpec((CONV_K, CW), lambda i: (0, 0)), one, one],
                 out_specs=[hs, hs, hs, sc, sc],
                 out_shape=[_sds((tp, HW), F32)] * 3 + [_sds((NH, tp, DH), F32)] * 2,
                 scratch=[pltpu.VMEM((RT + 8, CW), F32)], sem=("parallel",))(*([proj] * 6), pab, conv_w, alog, dtb)


def _gdn_prep_bwd(proj, pab, conv_w, alog, dtb, dq, dk, dv, dgb, dbb):
    tp = proj.shape[0]
    nt = tp // RT

    def body(x0, x1, x2, p0, p1, p2, ab_ref, cw_ref, al_ref, dt_ref, dq_ref, dk_ref, dv_ref, dg_ref, db_ref,
             dx_ref, dab_ref, dcw_ref, dal_ref, ddt_ref, ext_ref, dy_ref):
        step = pl.program_id(0)
        i = nt - 1 - step

        @pl.when(step == 0)
        def _():
            dy_ref[RT:RT + 8, :] = jnp.zeros((8, CW), F32)
            dcw_ref[...] = jnp.zeros((8, CW), F32)
            dal_ref[...] = jnp.zeros((1, DH), F32)
            ddt_ref[...] = jnp.zeros((1, DH), F32)

        _conv_fill(ext_ref, (x0, x1, x2), (p0, p1, p2), i == 0)
        cw = cw_ref[...]
        y = _conv_apply(ext_ref, cw)
        a = _silu(y)
        dsl = _dsilu(y)
        for h in range(NH):
            for part, ref, sc in ((0, dq_ref, DH ** -0.5), (1, dk_ref, 1.0)):
                lo = part * HW + h * DH
                seg = a[:, lo:lo + DH]
                r = lax.rsqrt(jnp.sum(seg * seg, axis=-1, keepdims=True) + EPS)
                xh = seg * r
                dxh = ref[:, h * DH:(h + 1) * DH] * sc
                dy_ref[0:RT, lo:lo + DH] = r * (dxh - xh * jnp.sum(dxh * xh, axis=-1, keepdims=True)) * dsl[:, lo:lo + DH]
        dy_ref[0:RT, 2 * HW:3 * HW] = dv_ref[...] * dsl[:, 2 * HW:3 * HW]
        dy = dy_ref[0:RT, :]
        dx = cw[CONV_K - 1:CONV_K, :] * dy
        dcw_ref[CONV_K - 1:CONV_K, :] += jnp.sum(dy * ext_ref[pl.ds(8, RT), :], axis=0, keepdims=True)
        for s in range(1, CONV_K):
            dx += cw[CONV_K - 1 - s:CONV_K - s, :] * dy_ref[pl.ds(s, RT), :]
            dcw_ref[CONV_K - 1 - s:CONV_K - s, :] += jnp.sum(dy * ext_ref[pl.ds(8 - s, RT), :], axis=0, keepdims=True)
        dx_ref[...] = dx.astype(BF16)
        dy_ref[RT:RT + 8, :] = dy[0:8, :]
        ab = ab_ref[...]
        g, beta = _gd_scalars(ab, al_ref[...], dt_ref[...])
        lane = lax.broadcasted_iota(jnp.int32, (RT, DH), 1)
        dgl = jnp.zeros((RT, DH), F32)
        dbl = jnp.zeros((RT, DH), F32)
        for h in range(NH):
            dgl = jnp.where(lane == h, dg_ref[h], dgl)
            dbl = jnp.where(lane == NH + h, db_ref[h], dbl)
        dsp = dgl * (-jnp.exp(al_ref[...])) * _sigmoid(ab + dt_ref[...])
        dab_ref[...] = (dsp + dbl * beta * (1.0 - beta)).astype(BF16)
        ddt_ref[...] += jnp.sum(dsp, axis=0, keepdims=True)
        dal_ref[...] += jnp.sum(dgl * g, axis=0, keepdims=True)

    hs = pl.BlockSpec((RT, HW), lambda s: (nt - 1 - s, 0))
    sc = pl.BlockSpec((NH, RT, DH), lambda s: (0, nt - 1 - s, 0))
    one = pl.BlockSpec((1, DH), lambda s: (0, 0))
    xs = pl.BlockSpec((RT, CW), lambda s: (nt - 1 - s, 0))
    return _call(body, name="gdn_prep_bwd", grid=(nt,),
                 in_specs=_conv_ext_specs(lambda s: nt - 1 - s) + [
                     pl.BlockSpec((RT, DH), lambda s: (nt - 1 - s, 0)), pl.BlockSpec((CONV_K, CW), lambda s: (0, 0)),
                     one, one, hs, hs, hs, sc, sc],
                 out_specs=[xs, pl.BlockSpec((RT, DH), lambda s: (nt - 1 - s, 0)), pl.BlockSpec((8, CW), lambda s: (0, 0)), one, one],
                 out_shape=[_sds((tp, CW), BF16), _sds((tp, DH), BF16), _sds((8, CW), F32), _sds((1, DH), F32), _sds((1, DH), F32)],
                 scratch=[pltpu.VMEM((RT + 8, CW), F32), pltpu.VMEM((RT + 8, CW), F32)],
                 sem=("arbitrary",))(*([proj] * 6), pab, conv_w, alog, dtb, dq, dk, dv, dgb, dbb)


HS = HB * CH


def _stack_heads(ref, rows, base):
    return jnp.concatenate([ref[rows, (base + hh) * DH:(base + hh + 1) * DH] for hh in range(HB)], axis=0)


def _stack_scal(ref, rows, base):
    return jnp.concatenate([ref[base + hh, rows, :] for hh in range(HB)], axis=0)


def _store_heads(ref, rows, base, val):
    for hh in range(HB):
        ref[rows, (base + hh) * DH:(base + hh + 1) * DH] = val[hh * CH:(hh + 1) * CH].astype(ref.dtype)


def _bd_masks():
    r = lax.broadcasted_iota(jnp.int32, (HS, HS), 0)
    c = lax.broadcasted_iota(jnp.int32, (HS, HS), 1)
    same = lax.shift_right_logical(r, int(math.log2(CH))) == lax.shift_right_logical(c, int(math.log2(CH)))
    return same & (r >= c), same & (r > c)


def _unit_lower_inverse(a):
    n = a.shape[0]
    r = lax.broadcasted_iota(jnp.int32, (n, n), 0)
    c = lax.broadcasted_iota(jnp.int32, (n, n), 1)
    blk_of = lambda t, size: lax.shift_right_logical(t, int(math.log2(size)))
    a16 = jnp.where(blk_of(r, SUB) == blk_of(c, SUB), a, 0.0)
    x = (r == c).astype(F32) - a16
    p = a16
    for _ in range(3):
        p = _dot(p, p)
        x = x + _dot(x, p)
    for blk in (2 * SUB, 4 * SUB):
        off = jnp.where((blk_of(r, blk) == blk_of(c, blk)) & (blk_of(r, blk // 2) != blk_of(c, blk // 2)), a, 0.0)
        x = x - _dot(x, _dot(off, x))
    return x


def _gdn_chunk_common(q, k, v, gl, bt, incl, strict, x=None):
    gc = _cumsum(gl)
    e = jnp.exp(gc)
    rel = jnp.exp(jnp.minimum(gc[:, 0:1] - gc.T[0:1, :], 0.0))
    kb = bt * k
    a = jnp.where(strict, bt[:, 0:1] * _dot(k, k, NT) * rel, 0.0)
    if x is None:
        x = _unit_lower_inverse(a)
    wu = _dot(x, jnp.concatenate([kb * e, bt * v], axis=1))
    attn = jnp.where(incl, _dot(q, k, NT) * rel, 0.0)
    return gc, e, rel, kb, a, x, wu[:, 0:DH], wu[:, DH:2 * DH], attn


def _gdn_fwd(q, k, v, gb, bb, proj, nw):
    tp = q.shape[0]
    nt, cpt = tp // RT, RT // CH

    def body(q_ref, k_ref, v_ref, g_ref, b_ref, z_ref, nw_ref, og_ref, or_ref, st_ref, x_ref, s_ref):
        @pl.when(pl.program_id(1) == 0)
        def _():
            s_ref[...] = jnp.zeros((HPS, DH, DH), F32)

        incl, strict = _bd_masks()

        def chunk(c, carry):
            rows = pl.ds(pl.multiple_of(c * CH, CH), CH)
            for base in range(0, HPS, HB):
                qc, kc, vc = _stack_heads(q_ref, rows, base), _stack_heads(k_ref, rows, base), _stack_heads(v_ref, rows, base)
                gc, e, rel, kb, a, x, w, u, attn = _gdn_chunk_common(qc, kc, vc, _stack_scal(g_ref, rows, base),
                                                                    _stack_scal(b_ref, rows, base), incl, strict)
                x_ref[base // HB, c] = x
                qe = qc * e
                ws, qs = [], []
                for hh in range(HB):
                    blk = slice(hh * CH, (hh + 1) * CH)
                    s = s_ref[base + hh]
                    st_ref[base + hh, c] = s
                    both = _dot(jnp.concatenate([w[blk], qe[blk]], axis=0), s)
                    ws.append(both[0:CH])
                    qs.append(both[CH:2 * CH])
                vn = u - jnp.concatenate(ws, axis=0)
                o = jnp.concatenate(qs, axis=0) + _dot(attn, vn)
                for hh in range(HB):
                    blk = slice(hh * CH, (hh + 1) * CH)
                    gl = gc[(hh + 1) * CH - 1:(hh + 1) * CH]
                    s_ref[base + hh] = s_ref[base + hh] * jnp.exp(gl) + _dot(kc[blk] * jnp.exp(gl - gc[blk]), vn[blk], TN)
                _store_heads(or_ref, rows, base, o)
                for hh in range(HB):
                    cols = slice((base + hh) * DH, (base + hh + 1) * DH)
                    og_ref[rows, cols] = _gated_norm_fwd(o[hh * CH:(hh + 1) * CH], z_ref[rows, cols], nw_ref[...]).astype(BF16)
            return carry

        lax.fori_loop(0, cpt, chunk, 0)

    col, full, real, state, scal = _head_specs(HPS)
    return _call(body, name="gdn_fwd", grid=(NHB, nt),
                 in_specs=[full, full, full, scal, scal, col(7), pl.BlockSpec((1, DH), lambda h, i: (0, 0))],
                 out_specs=[real, full, state(cpt), pl.BlockSpec((HPS // HB, cpt, HS, HS), lambda h, i: (h, i, 0, 0))],
                 out_shape=[_sds((tp - FRONT, HW), BF16), _sds((tp, HW), F32), _sds((NH, tp // CH, DH, DH), F32),
                            _sds((NH // HB, tp // CH, HS, HS), F32)],
                 scratch=[pltpu.VMEM((HPS, DH, DH), F32)], sem=("parallel", "arbitrary"))(q, k, v, gb, bb, proj, nw)


def _gdn_bwd(q, k, v, gb, bb, proj, nw, o_raw, states, xinv, dog):
    tp = q.shape[0]
    nt, cpt = tp // RT, RT // CH

    def body(q_ref, k_ref, v_ref, g_ref, b_ref, z_ref, nw_ref, or_ref, st_ref, x_ref, dog_ref,
             dq_ref, dk_ref, dv_ref, dg_ref, db_ref, dz_ref, dnw_ref, ds_ref, do_ref):
        step = pl.program_id(1)

        @pl.when(step == 0)
        def _():
            ds_ref[...] = jnp.zeros((HPS, DH, DH), F32)

        @pl.when((step == 0) & (pl.program_id(0) == 0))
        def _():
            dnw_ref[...] = jnp.zeros((1, DH), F32)

        front = nt - 1 - step < OFF
        for hh in range(HPS):
            cols = slice(hh * DH, (hh + 1) * DH)
            dog_t = jnp.where(front, 0.0, dog_ref[:, cols])
            do_t, dz_t, dnw = _gated_norm_bwd(or_ref[:, cols], z_ref[:, cols], nw_ref[...], dog_t)
            do_ref[:, cols] = do_t
            dz_ref[:, cols] = dz_t.astype(BF16)
            dnw_ref[...] += dnw
        incl, strict = _bd_masks()
        last_row = (lax.broadcasted_iota(jnp.int32, (CH, 1), 0) == CH - 1)

        def rsum(t):
            return jnp.sum(t, axis=-1, keepdims=True)

        def chunk(cc, carry):
            c = cpt - 1 - cc
            rows = pl.ds(pl.multiple_of(c * CH, CH), CH)
            for base in range(0, HPS, HB):
                qc, kc, vc, do = (_stack_heads(q_ref, rows, base), _stack_heads(k_ref, rows, base), _stack_heads(v_ref, rows, base),
                                  _stack_heads(do_ref, rows, base))
                bt = _stack_scal(b_ref, rows, base)
                gc, e, rel, kb, a, x, w, u, attn = _gdn_chunk_common(qc, kc, vc, _stack_scal(g_ref, rows, base), bt, incl, strict,
                                                                    x=x_ref[base // HB, c])
                qe = qc * e
                heads = [slice(hh * CH, (hh + 1) * CH) for hh in range(HB)]
                gls = [gc[(hh + 1) * CH - 1:(hh + 1) * CH] for hh in range(HB)]
                cdec = jnp.concatenate([jnp.exp(gl - gc[blk]) for gl, blk in zip(gls, heads)], axis=0)
                kcd = kc * cdec
                vn = u - jnp.concatenate([_dot(w[blk], st_ref[base + hh, c]) for hh, blk in enumerate(heads)], axis=0)
                dos = jnp.concatenate([_dot(do[blk], st_ref[base + hh, c], NT) for hh, blk in enumerate(heads)], axis=0)
                kds = jnp.concatenate([_dot(kcd[blk], ds_ref[base + hh]) for hh, blk in enumerate(heads)], axis=0)
                vds = jnp.concatenate([_dot(vn[blk], ds_ref[base + hh], NT) for hh, blk in enumerate(heads)], axis=0)
                dvn = _dot(attn, do, TN) + kds
                dattn = jnp.where(incl, _dot(do, vn, NT), 0.0)
                dar = dattn * rel
                dq = _dot(dar, kc) + e * dos
                dk = _dot(dar, qc, TN) + cdec * vds
                dc = cdec[:, 0:1] * rsum(kc * vds)
                dgc = rsum(qe * dos) - dc
                dw = -jnp.concatenate([_dot(dvn[blk], st_ref[base + hh, c], NT) for hh, blk in enumerate(heads)], axis=0)
                extra = []
                for hh, blk in enumerate(heads):
                    s, dsn = st_ref[base + hh, c], ds_ref[base + hh]
                    el = jnp.exp(gls[hh])
                    dglast = jnp.sum(dc[blk], axis=0, keepdims=True) + el[:, 0:1] * jnp.sum(rsum(dsn * s), axis=0, keepdims=True)
                    extra.append(jnp.where(last_row, dglast, 0.0))
                    ds_ref[base + hh] = dsn * el + _dot(jnp.concatenate([qe[blk], -w[blk]], axis=0),
                                                 jnp.concatenate([do[blk], dvn[blk]], axis=0), TN)
                dr = _dot(x, jnp.concatenate([dw, dvn], axis=1), TN)
                drw, dru = dr[:, 0:DH], dr[:, DH:2 * DH]
                da = -jnp.where(strict, _dot(dr, jnp.concatenate([w, u], axis=1), NT), 0.0)
                dar2 = da * rel
                dkb = _dot(dar2, kc)
                rwk = rsum(drw * kc)
                dk = dk + _dot(dar2, kb, TN) + bt * dkb + (bt * e) * drw
                dbeta = rsum(dkb * kc) + e[:, 0:1] * rwk + rsum(dru * vc)
                z = dattn * attn + da * a
                dgc = dgc + bt[:, 0:1] * e[:, 0:1] * rwk + rsum(z) - rsum(z.T) + jnp.concatenate(extra, axis=0)
                _store_heads(dq_ref, rows, base, dq)
                _store_heads(dk_ref, rows, base, dk)
                _store_heads(dv_ref, rows, base, bt * dru)
                dg = _rcumsum(jnp.broadcast_to(dgc, (HS, DH)))
                dbb = jnp.broadcast_to(dbeta, (HS, DH))
                for hh, blk in enumerate(heads):
                    dg_ref[base + hh, rows, :] = dg[blk]
                    db_ref[base + hh, rows, :] = dbb[blk]
            return carry

        lax.fori_loop(0, cpt, chunk, 0)

    col, full, real, state, scal = _head_specs(HPS, lambda i: nt - 1 - i)
    one = pl.BlockSpec((1, DH), lambda h, i: (0, 0))
    return _call(body, name="gdn_bwd", grid=(NHB, nt),
                 in_specs=[full, full, full, scal, scal, col(7), one, full, state(cpt),
                           pl.BlockSpec((HPS // HB, cpt, HS, HS), lambda h, i: (h, nt - 1 - i, 0, 0)), real],
                 out_specs=[full, full, full, scal, scal, full, one],
                 out_shape=[_sds((tp, HW), F32)] * 3 + [_sds((NH, tp, DH), F32)] * 2 + [_sds((tp, HW), BF16), _sds((1, DH), F32)],
                 scratch=[pltpu.VMEM((HPS, DH, DH), F32), pltpu.VMEM((RT, HPS * DH), F32)],
                 sem=("arbitrary", "arbitrary"))(q, k, v, gb, bb, proj, nw, o_raw, states, xinv, dog)


MAIN_W = 8 * HW
AB_W = 2 * NH


def _split_w_in(w_in):
    main = jnp.concatenate([w_in[:, :MAIN_W], w_in[:, MAIN_W + AB_W:]], axis=1)
    ab = jnp.pad(w_in[:, MAIN_W:MAIN_W + AB_W], ((0, 0), (0, DH - AB_W)))
    return main, ab


def _pad_lanes(v):
    return jnp.pad(v, ((0, 0), (0, DH - v.shape[1])))


class _NoComm:
    def __init__(self, w_in, late):
        self.w_in, self.late = w_in, late

    def first_side(self):
        return None

    def first_weight(self, side_outs):
        return self.w_in

    def proj_side(self):
        return None

    def late_weights(self, side_outs):
        return self.late

    def early_grads_side(self, grads):
        return None

    def early_grads_done(self, side_outs):
        pass

    def last_grad_side(self, dw_in):
        return None

    def last_grad_done(self, side_outs):
        pass


def _local_step(x, tgt, meta, lb_logits, mix_w, hg_nw, conv_w, a_log, dt_bias, gd_nw, ffn_nw, final_w, comm):
    alog, dtb = _pad_lanes(a_log), _pad_lanes(dt_bias)
    final_w = final_w.reshape(1, -1)
    rows4 = lambda t: t.reshape(4, t.shape[0] // 4, t.shape[1])
    side = comm.first_side()
    xn = _rms1_fwd(x, meta, mix_w, side=side)
    xn, landed = xn if side is not None else (xn, None)
    w_main, w_ab = _split_w_in(comm.first_weight(landed))
    side = comm.proj_side()
    proj = _mm(xn, w_main, "nn", F32, 768, 2048, 2048, "proj_main", n_outer=True, side=side)
    proj, landed = proj if side is not None else (proj, None)
    w_a, w_b, w_out, w_ffn_in, w_ffn_out = comm.late_weights(landed)
    pab = _mm(xn, w_ab, "nn", F32, 768, 128, 2048, "proj_ab")
    oa_g, oa_raw, st_a, pm_a = _hgrn2_fwd(proj, lb_logits, hg_nw)
    q, k, v, gb, bb = _gdn_prep_fwd(proj, pab, conv_w, alog, dtb)
    ob_g, ob_raw, st_b, xinv = _gdn_fwd(q, k, v, gb, bb, proj, gd_nw)
    za = _mm(oa_g, w_a, "nn", F32, 1024, 512, 1024, "branch_a", n_outer=True)
    zb = _mm(ob_g, w_b, "nn", F32, 1024, 512, 1024, "branch_b", n_outer=True)
    merged = _merge_fwd(proj, za, zb)
    mix = _mm(merged, w_out, "nn", F32, 1024, 2048, 2048, "mix_out")
    h1, n2 = _resid_norm_fwd(x, mix, ffn_nw)
    gu = _mm(n2, w_ffn_in, "nn", BF16, 1024, 1408, 2048, "ffn_in", n_outer=True)
    act = _swiglu_fwd(gu)
    f = _mm(act, w_ffn_out, "nn", F32, 1024, 2048, 1408, "ffn_out")
    lt, dh2, dh2b, dfinal = _loss_head(h1, f, final_w, tgt)
    loss = _sum_tiles(lt)
    dact = _mm(dh2b, w_ffn_out, "nt", BF16, 1024, 1408, 2048, "d_act", n_outer=True)
    dw_ffn_out = rows4(_mm(act, dh2b, "tn", F32, 512, 2048, 2048, "dw_ffn_out"))
    dgu = _swiglu_bwd(gu, dact)
    dn2 = _mm(dgu, w_ffn_in, "nt", F32, 1024, 2048, 1408, "d_n2")
    dw_ffn_in = _mm(n2, dgu, "tn", F32, 1024, 1408, 2048, "dw_ffn_in", out_shards=4)
    dh1, dh1b, dffn_nw = _resid_norm_bwd(h1, ffn_nw, dn2, dh2)
    dmerged = _mm(dh1b, w_out, "nt", F32, 1024, 2048, 2048, "d_merged")
    dw_out = _mm(merged, dh1b, "tn", F32, 2048, 1024, 1024, "dw_out")
    dza, dzb, dgate = _merge_bwd(proj, za, zb, dmerged)
    doa = _mm(dza, w_a, "nt", F32, 1024, 1024, 512, "d_oa")
    dob = _mm(dzb, w_b, "nt", F32, 1024, 1024, 512, "d_ob")
    dw_a = _mm(oa_g, dza, "tn", F32, 1024, 512, 1024, "dw_branch_a", out_shards=4)
    dw_b = _mm(ob_g, dzb, "tn", F32, 1024, 512, 1024, "dw_branch_b", out_shards=4)
    early = dict(w_ffn_in=dw_ffn_in, w_ffn_out=dw_ffn_out, w_out=rows4(dw_out), w_branch_a=dw_a, w_branch_b=dw_b)
    side = comm.early_grads_side(early)
    hg = _hgrn2_bwd(proj, lb_logits, hg_nw, oa_raw, st_a, pm_a, doa, side=side)
    if side is not None:
        hg, arrived = hg
        comm.early_grads_done(arrived)
    dhq, dhf, dhi, dhg, dlbl, dhg_nw = hg
    dq, dk, dv, dg, dbeta, dz, dgd_nw = _gdn_bwd(q, k, v, gb, bb, proj, gd_nw, ob_raw, st_b, xinv, dob)
    dx3, dab, dconv, dalog, ddtb = _gdn_prep_bwd(proj, pab, conv_w, alog, dtb, dq, dk, dv, dg, dbeta)
    dproj = jnp.concatenate([dhq, dhf, dhi, dhg, dx3, dz, dgate], axis=1)
    dw_main = _mm(xn, dproj, "tn", F32, 1024, 1024, 2816, "dw_in_main")
    dw_ab = _mm(xn, dab, "tn", F32, 2048, 128, 768, "dw_in_ab")
    d_model = dw_main.shape[0]
    dw_in = jnp.concatenate([dw_main[:, :MAIN_W], dw_ab[:, :AB_W], dw_main[:, MAIN_W:]], axis=1)
    dw_in = dw_in.reshape(d_model, 4, -1).transpose(1, 0, 2)
    side = comm.last_grad_side(dw_in)
    dxn = _mm(dproj, w_main, "nt", F32, 768, 2048, 2048, "d_xn", side=side)
    if side is not None:
        dxn, arrived = dxn
        comm.last_grad_done(arrived)
    dxn = _mm(dab, w_ab, "nt", F32, 768, 2048, 128, "d_xn_ab", add=dxn)
    dx, dmeta, dmix_w = _rms1_bwd(x, meta, mix_w, dxn, dh1)
    grads = dict(meta_tokens=dmeta, lb_logits=dlbl, mix_norm_w=dmix_w, w_in=dw_in,
                 hg_norm_w=dhg_nw, gd_conv_w=dconv[:CONV_K], gd_a_log=dalog[:, :NH],
                 gd_dt_bias=ddtb[:, :NH], gd_norm_w=dgd_nw, w_branch_a=dw_a, w_branch_b=dw_b,
                 w_out=rows4(dw_out), ffn_norm_w=dffn_nw, w_ffn_in=dw_ffn_in, w_ffn_out=dw_ffn_out,
                 final_norm_w=dfinal.reshape(-1))
    return loss, dx, grads


def _adamw(g, w, m, v, name):
    rows, cols = g.shape
    tr = rows
    for cand in (128, 64, 32, 16, 8):
        if rows % cand == 0 and rows > cand:
            tr = cand
            break

    def body(g_ref, w_ref, m_ref, v_ref, go_ref, d_ref, nm_ref, nv_ref):
        gg = g_ref[...]
        go_ref[...] = gg
        nm = ADAM_B1 * m_ref[...] + (1.0 - ADAM_B1) * gg
        nv = ADAM_B2 * v_ref[...] + (1.0 - ADAM_B2) * (gg * gg)
        m_hat = nm / (1.0 - ADAM_B1 ** ADAM_STEP)
        v_hat = nv / (1.0 - ADAM_B2 ** ADAM_STEP)
        d_ref[...] = -ADAM_LR * (m_hat / (jnp.sqrt(v_hat) + ADAM_EPS) + ADAM_WD * w_ref[...])
        nm_ref[...] = nm
        nv_ref[...] = nv

    bs = pl.BlockSpec((tr, cols), lambda i: (i, 0))
    return _call(body, name=name, grid=(rows // tr,), in_specs=[bs] * 4, out_specs=[bs] * 4,
                 out_shape=[_sds((rows, cols), F32)] * 4, sem=("parallel",))(g, w, m, v)


HBM = pl.BlockSpec(memory_space=pltpu.HBM)
MESH = pl.DeviceIdType.MESH


def _place():
    x, y, c = lax.axis_index("x"), lax.axis_index("y"), lax.axis_index("c")
    return x, y, c, [(1 - x, y), (x, 1 - y), (1 - x, 1 - y)]


def _comm_call(body, name, out_shape, n_in, scratch):
    return pl.pallas_call(body, name=name, out_shape=out_shape, in_specs=[HBM] * n_in,
                          out_specs=jax.tree.map(lambda _: HBM, out_shape), scratch_shapes=scratch)


def _half_rows(rows, c, tile):
    hh = rows // 2
    assert rows % 2 == 0 and hh % tile == 0, (rows, tile)
    return pl.ds(pl.multiple_of(c * hh, tile), hh)


def _gather_copies(w_refs, out_refs, sems):
    send_sems, recv_sems = sems
    x, y, c, chips = _place()
    s_me = 2 * x + y
    sends, recvs = [], []
    for k, (w_ref, out_ref) in enumerate(zip(w_refs, out_refs)):
        half = _half_rows(w_ref.shape[0], c, 16)
        for j, (cx, cy) in enumerate(chips):
            sem = dict(send_sem=send_sems.at[3 * k + j], recv_sem=recv_sems.at[3 * k + j], device_id=(cx, cy, c), device_id_type=MESH)
            sends.append(pltpu.make_async_remote_copy(src_ref=w_ref.at[half], dst_ref=out_ref.at[s_me, half], **sem))
            recvs.append(pltpu.make_async_remote_copy(src_ref=w_ref.at[half], dst_ref=out_ref.at[2 * cx + cy, half], **sem))
    return sends, recvs


def _gather_sems(n):
    return [pltpu.SemaphoreType.DMA((3 * n,)), pltpu.SemaphoreType.DMA((3 * n,))]


def _gather_start(w_refs, out_refs, sems):
    for cp in _gather_copies(w_refs, out_refs, sems)[0]:
        cp.start()


def _gather_wait(w_refs, out_refs, sems):
    sends, recvs = _gather_copies(w_refs, out_refs, sems)
    for cp in recvs:
        cp.wait_recv()
    for cp in sends:
        cp.wait_send()


def _gather_side(shards):
    return _Side(shards, [_sds((4,) + w.shape, w.dtype) for w in shards], _gather_sems(len(shards)), _gather_start, _gather_wait)


def _forward_halves(outs, name):
    n = len(outs)

    def body(*refs):
        out_refs = refs[n:2 * n]
        send_sems, recv_sems = refs[2 * n:]
        x, y, c, chips = _place()
        cps = []
        for k in range(n):
            rows = out_refs[k].shape[1]
            half, other = _half_rows(rows, c, 16), _half_rows(rows, 1 - c, 16)
            for j, (cx, cy) in enumerate(chips):
                sem = dict(send_sem=send_sems.at[3 * k + j], recv_sem=recv_sems.at[3 * k + j], device_id=(x, y, 1 - c), device_id_type=MESH)
                landed = out_refs[k].at[2 * cx + cy, half]
                cps.append(pltpu.make_async_remote_copy(src_ref=landed, dst_ref=landed, **sem))
                cps[-1].start()
        for k in range(n):
            rows = out_refs[k].shape[1]
            half, other = _half_rows(rows, c, 16), _half_rows(rows, 1 - c, 16)
            for j, (cx, cy) in enumerate(chips):
                sem = dict(send_sem=send_sems.at[3 * k + j], recv_sem=recv_sems.at[3 * k + j], device_id=(x, y, 1 - c), device_id_type=MESH)
                pltpu.make_async_remote_copy(src_ref=out_refs[k].at[2 * cx + cy, half], dst_ref=out_refs[k].at[2 * cx + cy, other], **sem).wait_recv()
        for cp in cps:
            cp.wait_send()

    shapes = [_sds(o.shape, o.dtype) for o in outs]
    return pl.pallas_call(body, name=name, out_shape=shapes, in_specs=[HBM] * n, out_specs=[HBM] * n,
                          input_output_aliases={k: k for k in range(n)},
                          scratch_shapes=[pltpu.SemaphoreType.DMA((3 * n,)), pltpu.SemaphoreType.DMA((3 * n,))])(*outs)


def _swap_halves(gs, name):
    n = len(gs)

    def body(*refs):
        g_refs, out_refs = refs[:n], refs[n:2 * n]
        send_sems, recv_sems = refs[2 * n:]
        x, y, c, _ = _place()
        cps = []
        for k in range(n):
            other = _half_rows(g_refs[k].shape[1], 1 - c, 8)
            cps.append(pltpu.make_async_remote_copy(src_ref=g_refs[k].at[:, other, :], dst_ref=out_refs[k], send_sem=send_sems.at[k],
                                                    recv_sem=recv_sems.at[k], device_id=(x, y, 1 - c), device_id_type=MESH))
            cps[-1].start()
        for cp in cps:
            cp.wait()

    return _comm_call(body, name, [_sds((4, g.shape[1] // 2, g.shape[2]), g.dtype) for g in gs], n,
                      [pltpu.SemaphoreType.DMA((n,)), pltpu.SemaphoreType.DMA((n,))])(*gs)


def _row_tile(rows, row_bytes, budget=3 << 20):
    if rows * row_bytes <= budget:
        return rows
    return max(t for t in range(16, rows, 16) if rows % t == 0 and t * row_bytes <= budget)


def _add_half(g, got, c, name):
    _, rows, cols = g.shape
    hh = rows // 2
    tr = _row_tile(hh, cols * 4)
    nb = hh // tr

    def body(c_ref, a_ref, b_ref, o_ref):
        o_ref[...] = (a_ref[...] + b_ref[...]).astype(BF16)

    gs = pltpu.PrefetchScalarGridSpec(
        num_scalar_prefetch=1, grid=(4, nb),
        in_specs=[pl.BlockSpec((1, tr, cols), lambda s, i, c_ref: (s, c_ref[0] * nb + i, 0)),
                  pl.BlockSpec((1, tr, cols), lambda s, i, c_ref: (s, i, 0))],
        out_specs=pl.BlockSpec((1, tr, cols), lambda s, i, c_ref: (s, i, 0)))
    return pl.pallas_call(body, name=name, grid_spec=gs, out_shape=_sds((4, hh, cols), BF16),
                          compiler_params=pltpu.CompilerParams(dimension_semantics=("parallel", "parallel"),
                                                               vmem_limit_bytes=VMEM_LIMIT))(c, g, got)


def _scatter_copies(p_refs, out_refs, sems):
    send_sems, recv_sems = sems
    x, y, c, chips = _place()
    s_me = 2 * x + y
    cps = []
    for k, (p_ref, out_ref) in enumerate(zip(p_refs, out_refs)):
        for j, (cx, cy) in enumerate(chips):
            cps.append(pltpu.make_async_remote_copy(src_ref=p_ref.at[2 * cx + cy], dst_ref=out_ref.at[s_me],
                                                    send_sem=send_sems.at[3 * k + j], recv_sem=recv_sems.at[3 * k + j],
                                                    device_id=(cx, cy, c), device_id_type=MESH))
    return cps


def _scatter_start(p_refs, out_refs, sems):
    for cp in _scatter_copies(p_refs, out_refs, sems):
        cp.start()


def _scatter_wait(p_refs, out_refs, sems):
    for cp in _scatter_copies(p_refs, out_refs, sems):
        cp.wait()


def _scatter_side(ps):
    return _Side(ps, [_sds(p_.shape, p_.dtype) for p_ in ps], _gather_sems(len(ps)), _scatter_start, _scatter_wait)


def _sum_slabs(b, name):
    n, h, wd = b.shape
    tr = _row_tile(h, n * wd * 4, 6 << 20)

    def body(b_ref, o_ref):
        acc = b_ref[0]
        for s in range(1, n):
            acc = acc + b_ref[s]
        o_ref[...] = acc

    return _call(body, name=name, grid=(h // tr,), in_specs=[pl.BlockSpec((n, tr, wd), lambda i: (0, i, 0))],
                 out_specs=pl.BlockSpec((tr, wd), lambda i: (i, 0)), out_shape=_sds((h, wd), F32), sem=("parallel",))(b)


def _sum_chips(arrived, own, name):
    n, h, wd = arrived.shape
    tr = _row_tile(h, n * wd * 2, 6 << 20)
    nb = h // tr
    my_chip = lambda: 2 * lax.axis_index("x") + lax.axis_index("y")

    def body(b_ref, p_ref, o_ref):
        acc = None
        for s in range(n):
            term = jnp.where(my_chip() == s, p_ref[0], b_ref[s]).astype(F32)
            acc = term if acc is None else acc + term
        o_ref[...] = acc

    return _call(body, name=name, grid=(nb,),
                 in_specs=[pl.BlockSpec((n, tr, wd), lambda i: (0, i, 0)), pl.BlockSpec((1, tr, wd), lambda i: (my_chip(), i, 0))],
                 out_specs=pl.BlockSpec((tr, wd), lambda i: (lax.axis_index("c") * nb + i, 0)),
                 out_shape=_sds((2 * h, wd), F32), sem=("parallel",))(arrived, own)


def _share_halves(gs):
    n = len(gs)

    def body(*refs):
        out_refs = refs[n:2 * n]
        send_sems, recv_sems = refs[2 * n:]
        x, y, c, _ = _place()
        cps = []
        for k in range(n):
            half, other = _half_rows(out_refs[k].shape[0], c, 8), _half_rows(out_refs[k].shape[0], 1 - c, 8)
            sem = dict(send_sem=send_sems.at[k], recv_sem=recv_sems.at[k], device_id=(x, y, 1 - c), device_id_type=MESH)
            cps.append((pltpu.make_async_remote_copy(src_ref=out_refs[k].at[half], dst_ref=out_refs[k].at[half], **sem),
                        pltpu.make_async_remote_copy(src_ref=out_refs[k].at[half], dst_ref=out_refs[k].at[other], **sem)))
            cps[-1][0].start()
        for send, recv in cps:
            recv.wait_recv()
            send.wait_send()

    return pl.pallas_call(body, name="share_halves", out_shape=[_sds(g.shape, g.dtype) for g in gs], in_specs=[HBM] * n,
                          out_specs=[HBM] * n, input_output_aliases={k: k for k in range(n)},
                          scratch_shapes=[pltpu.SemaphoreType.DMA((n,)), pltpu.SemaphoreType.DMA((n,))])(*gs)


def _gather_all(v, name):
    def body(v_ref, out_ref, send_sems, recv_sems, local_sem):
        x, y, c = lax.axis_index("x"), lax.axis_index("y"), lax.axis_index("c")
        me = 4 * x + 2 * y + c
        flip = lambda t, d: 1 - t if d else t
        mine = pltpu.make_async_copy(v_ref, out_ref.at[me], local_sem)
        mine.start()
        cps = []
        for k in range(1, 8):
            to = (flip(x, k & 4), flip(y, k & 2), flip(c, k & 1))
            cps.append(pltpu.make_async_remote_copy(src_ref=v_ref, dst_ref=out_ref.at[me], send_sem=send_sems.at[k - 1],
                                                    recv_sem=recv_sems.at[k - 1], device_id=to, device_id_type=MESH))
        for cp in cps:
            cp.start()
        for cp in cps:
            cp.wait()
        mine.wait()

    return _comm_call(body, name, _sds((8,) + v.shape, v.dtype), 1,
                      [pltpu.SemaphoreType.DMA((7,)), pltpu.SemaphoreType.DMA((7,)), pltpu.SemaphoreType.DMA])(v)


BIG = (("w_in", 1), ("w_branch_a", 1), ("w_branch_b", 1), ("w_out", 0), ("w_ffn_in", 1), ("w_ffn_out", 0))
SMALL = ("meta_tokens", "lb_logits", "mix_norm_w", "hg_norm_w", "gd_conv_w", "gd_a_log", "gd_dt_bias", "gd_norm_w",
         "ffn_norm_w", "final_norm_w")


def _pack_lanes(parts):
    rows = []
    for p in parts:
        f = p.reshape(-1).astype(F32)
        n = -(-f.shape[0] // DH) * DH
        rows.append(jnp.pad(f, (0, n - f.shape[0])).reshape(-1, DH))
    buf = jnp.concatenate(rows, axis=0)
    return jnp.pad(buf, ((0, -buf.shape[0] % 8), (0, 0)))


def _unpack_lanes(buf, shapes):
    out, off = [], 0
    for shp in shapes:
        n = math.prod(shp)
        r = -(-n // DH)
        out.append(buf[off:off + r].reshape(-1)[:n].reshape(shp))
        off += r
    return out


def kernel(x, meta_tokens, lb_logits, mix_norm_w, w_in, hg_norm_w, gd_conv_w, gd_a_log, gd_dt_bias, gd_norm_w, w_branch_a, w_branch_b, w_out, ffn_norm_w, w_ffn_in, w_ffn_out, final_norm_w, loss_target, m_meta_tokens, m_lb_logits, m_mix_norm_w, m_w_in, m_hg_norm_w, m_gd_conv_w, m_gd_a_log, m_gd_dt_bias, m_gd_norm_w, m_w_branch_a, m_w_branch_b, m_w_out, m_ffn_norm_w, m_w_ffn_in, m_w_ffn_out, m_final_norm_w, v_meta_tokens, v_lb_logits, v_mix_norm_w, v_w_in, v_hg_norm_w, v_gd_conv_w, v_gd_a_log, v_gd_dt_bias, v_gd_norm_w, v_w_branch_a, v_w_branch_b, v_w_out, v_ffn_norm_w, v_w_ffn_in, v_w_ffn_out, v_final_norm_w):
    args = dict(locals())
    big = [n for n, _ in BIG]
    w = {n: args[n] for n in SMALL + tuple(big)}
    m = {n: args["m_" + n] for n in w}
    v = {n: args["v_" + n] for n in w}
    xi, yi, ci = lax.axis_index("x"), lax.axis_index("y"), lax.axis_index("c")
    shard = 2 * xi + yi
    big_local = {n: w[n][0] for n in big}

    meta_cols, conv_cols = meta_tokens.shape[1], gd_conv_w.shape[-1]
    sm_all = _gather_all(_pack_lanes([meta_tokens, gd_conv_w[0]]), "gather_meta")
    sm_parts = [_unpack_lanes(sm_all[2 * s], [meta_tokens.shape, gd_conv_w[0].shape]) for s in range(4)]
    meta_full = jnp.concatenate([p[0] for p in sm_parts], axis=1)
    conv_full = jnp.concatenate([p[1] for p in sm_parts], axis=1)
    cvec = ci.reshape(1).astype(jnp.int32)
    late = [n for n in big if n != "w_in"]
    rows_full = lambda t: t.reshape(t.shape[0] * t.shape[1], t.shape[2])

    def pair_sums(names, gs):
        return [_add_half(gk, got, cvec, "add_half_" + n) for n, gk, got in zip(names, gs, _swap_halves(gs, "swap_" + names[0]))]

    def with_own(slabs, n):
        return lax.dynamic_update_index_in_dim(slabs, big_local[n].astype(BF16), shard, 0)

    class MeshComm:
        def first_side(self):
            return _gather_side([big_local["w_in"].astype(BF16)])

        def first_weight(self, landed):
            slabs = with_own(_forward_halves(landed, "forward_w_in")[0], "w_in")
            return slabs.transpose(1, 0, 2).reshape(slabs.shape[1], -1)

        def proj_side(self):
            return _gather_side([big_local[n].astype(BF16) for n in late])

        def late_weights(self, landed):
            wl = {n: with_own(t, n) for n, t in zip(late, _forward_halves(landed, "forward_late"))}
            return (wl["w_branch_a"], wl["w_branch_b"], rows_full(wl["w_out"]), wl["w_ffn_in"], rows_full(wl["w_ffn_out"]))

        def early_grads_side(self, grads):
            self.early = list(grads)
            self.early_parts = pair_sums(self.early, [grads[n] for n in self.early])
            return _scatter_side(self.early_parts)

        def early_grads_done(self, arrived):
            self.early_arrived = arrived

        def last_grad_side(self, dw_in):
            self.last_parts = pair_sums(["w_in"], [dw_in])
            return _scatter_side(self.last_parts)

        def last_grad_done(self, arrived):
            self.last_arrived = arrived

    comm = MeshComm()
    loss, dx, g = _local_step(x[0], loss_target[0], meta_full, lb_logits, mix_norm_w, hg_norm_w, conv_full,
                              gd_a_log, gd_dt_bias, gd_norm_w, ffn_norm_w, final_norm_w, comm)
    loss = lax.psum(loss[0, 0], ("x", "y", "c"))

    parts = dict(zip(comm.early + ["w_in"], comm.early_parts + comm.last_parts))
    arrived = dict(zip(comm.early + ["w_in"], comm.early_arrived + comm.last_arrived))
    g_big = dict(zip(big, _share_halves([_sum_chips(arrived[n], parts[n], "sum_chips_" + n) for n in big])))

    small_shapes = [g[n].shape for n in SMALL]
    g_all = _gather_all(_pack_lanes([g[n] for n in SMALL]), "gather_small")
    g_small = dict(zip(SMALL, _unpack_lanes(_sum_slabs(g_all, "sum_small"), small_shapes)))
    g_small["meta_tokens"] = lax.dynamic_slice_in_dim(g_small["meta_tokens"], shard * meta_cols, meta_cols, axis=1)
    g_small["gd_conv_w"] = lax.dynamic_slice_in_dim(g_small["gd_conv_w"], shard * conv_cols, conv_cols, axis=1)

    grad, delta, new_m, new_v = {}, {}, {}, {}
    for n in big:
        g_, d_, m_, v_ = _adamw(g_big[n], big_local[n], m[n][0], v[n][0], "adamw_" + n)
        grad[n] = g_.reshape(w[n].shape)
        delta[n], new_m[n], new_v[n] = d_.reshape(w[n].shape), m_.reshape(w[n].shape), v_.reshape(w[n].shape)
    local_shapes = [w[n].shape for n in SMALL]
    _, d_, m_, v_ = _adamw(_pack_lanes([g_small[n] for n in SMALL]), _pack_lanes([w[n] for n in SMALL]),
                           _pack_lanes([m[n] for n in SMALL]), _pack_lanes([v[n] for n in SMALL]), "adamw_small")
    for n, gs_, dd, mm, vv in zip(SMALL, [g_small[n] for n in SMALL], _unpack_lanes(d_, local_shapes), _unpack_lanes(m_, local_shapes),
                                  _unpack_lanes(v_, local_shapes)):
        grad[n], delta[n], new_m[n], new_v[n] = gs_.reshape(w[n].shape), dd, mm, vv

    order = ["meta_tokens", "lb_logits", "mix_norm_w", "w_in", "hg_norm_w", "gd_conv_w", "gd_a_log", "gd_dt_bias", "gd_norm_w",
             "w_branch_a", "w_branch_b", "w_out", "ffn_norm_w", "w_ffn_in", "w_ffn_out", "final_norm_w"]
    return (loss, dx[None], *[grad[n] for n in order], *[delta[n] for n in order], *[new_m[n] for n in order],
            *[new_v[n] for n in order])
```

```python
import functools
import math

import jax
import jax.numpy as jnp
from jax import lax
from jax.experimental import pallas as pl
from jax.experimental.pallas import tpu as pltpu

F32, BF16 = jnp.float32, jnp.bfloat16
EPS = 1e-6
D_MODEL = 2048
N_META = 16
FRONT = 256
CH = 64
SUB = 16
SUBH = 16
DH = 128
NH = 8
HW = NH * DH
CONV_K = 4
RT = 256
VMEM_LIMIT = 56 * 1024 * 1024
ADAM_LR, ADAM_B1, ADAM_B2, ADAM_EPS, ADAM_WD, ADAM_STEP = 0.001, 0.9, 0.999, 1e-08, 0.01, 10

NN = (((1,), (0,)), ((), ()))
NT = (((1,), (1,)), ((), ()))
TN = (((0,), (0,)), ((), ()))


def _dot(a, b, dn=NN):
    return lax.dot_general(a.astype(BF16), b.astype(BF16), dn, preferred_element_type=F32)


def _dotx(a, b, dn=NN):
    return lax.dot_general(a, b, dn, precision=lax.Precision.HIGHEST, preferred_element_type=F32)


class _Side:
    def __init__(self, inputs, out_shapes, scratch, start, wait):
        self.inputs, self.out_shapes, self.scratch, self.start, self.wait = inputs, out_shapes, scratch, start, wait


def _call(body, *, name, grid, in_specs, out_specs, out_shape, scratch=(), sem=None, side=None):
    params = pltpu.CompilerParams(dimension_semantics=sem, vmem_limit_bytes=VMEM_LIMIT)
    if side is None:
        return pl.pallas_call(body, name=name, grid=grid, in_specs=in_specs, out_specs=out_specs, out_shape=out_shape,
                              scratch_shapes=list(scratch), compiler_params=params)
    single = not isinstance(out_specs, (list, tuple))
    out_specs, out_shape = ([out_specs], [out_shape]) if single else (list(out_specs), list(out_shape))
    ni, no, ns = len(in_specs), len(out_specs), len(scratch)
    nsi, nso = len(side.inputs), len(side.out_shapes)
    hbm = pl.BlockSpec(memory_space=pltpu.HBM)

    def wrapped(*refs):
        main_in, side_in = refs[:ni], refs[ni:ni + nsi]
        main_out, side_out = refs[ni + nsi:ni + nsi + no], refs[ni + nsi + no:ni + nsi + no + nso]
        main_scr, side_scr = refs[ni + nsi + no + nso:ni + nsi + no + nso + ns], refs[ni + nsi + no + nso + ns:]
        pids = [pl.program_id(d) for d in range(len(grid))]
        first = functools.reduce(lambda a, b: a & b, [p == 0 for p in pids])
        last = functools.reduce(lambda a, b: a & b, [p == g - 1 for p, g in zip(pids, grid)])

        @pl.when(first)
        def _():
            side.start(side_in, side_out, side_scr)

        body(*main_in, *main_out, *main_scr)

        @pl.when(last)
        def _():
            side.wait(side_in, side_out, side_scr)

    call = pl.pallas_call(wrapped, name=name, grid=grid, in_specs=list(in_specs) + [hbm] * nsi,
                          out_specs=out_specs + [hbm] * nso, out_shape=out_shape + list(side.out_shapes),
                          scratch_shapes=list(scratch) + list(side.scratch), compiler_params=params)

    def run(*args):
        outs = call(*args, *side.inputs)
        main = outs[0] if single else list(outs[:no])
        return main, list(outs[no:])

    return run


def _divmod(j, per):
    if per == 1:
        return j, 0
    return lax.div(j, jnp.int32(per)), lax.rem(j, jnp.int32(per))


def _sds(shape, dtype):
    return jax.ShapeDtypeStruct(tuple(shape), dtype)


def _sigmoid(x):
    return 0.5 * jnp.tanh(0.5 * x) + 0.5


def _silu(x):
    return x * _sigmoid(x)


def _dsilu(x):
    s = _sigmoid(x)
    return s * (1.0 + x * (1.0 - s))


def _tri(n, kind):
    r = lax.broadcasted_iota(jnp.int32, (n, n), 0)
    c = lax.broadcasted_iota(jnp.int32, (n, n), 1)
    return {"incl": r >= c, "strict": r > c, "upper": c >= r}[kind]


def _mm(a, b, mode, out_dtype, tm, tn, tk, name, add=None, n_outer=False, out_shards=0, side=None):
    sharded_a = a.ndim == 3
    if sharded_a:
        n_a = a.shape[2]
        a_shape = (a.shape[1], a.shape[0] * n_a)
    else:
        a_shape = a.shape
    sharded_b = b.ndim == 3
    if sharded_b:
        S, R, n = b.shape
        b_rows, b_cols = R, S * n
    else:
        b_rows, b_cols = b.shape
    if mode == "nn":
        (M, K), N, dn = a_shape, b_cols, NN
    elif mode == "nt":
        (M, K), N, dn = a_shape, b_rows, NT
    else:
        (K, M), N, dn = a_shape, b_cols, TN
    tm, tn, tk = min(tm, M), min(tn, N), min(tk, K)
    if sharded_a:
        tk = min(tk, n_a)
    if sharded_b:
        tn, tk = (min(tn, n), tk) if mode != "nt" else (tn, min(tk, n))
    if out_shards:
        tn = min(tn, N // out_shards)
    assert M % tm == 0 and N % tn == 0 and K % tk == 0, (name, M, N, K, tm, tn, tk)
    nk = K // tk
    a_blk, a_idx = ((tm, tk), lambda i, j, k: (i, k)) if mode != "tn" else ((tk, tm), lambda i, j, k: (k, i))
    if sharded_a:
        per_a = n_a // tk
        assert mode != "tn" and n_a % tk == 0
        a_blk, a_idx = (None, tm, tk), lambda i, j, k: (_divmod(k, per_a)[0], i, _divmod(k, per_a)[1])
    if not sharded_b:
        b_blk, b_idx = ((tk, tn), lambda i, j, k: (k, j)) if mode != "nt" else ((tn, tk), lambda i, j, k: (j, k))
    elif mode != "nt":
        per = n // tn
        assert n % tn == 0
        b_blk, b_idx = (None, tk, tn), lambda i, j, k: (_divmod(j, per)[0], k, _divmod(j, per)[1])
    else:
        per = n // tk
        assert n % tk == 0
        b_blk, b_idx = (None, tn, tk), lambda i, j, k: (_divmod(k, per)[0], j, _divmod(k, per)[1])
    if out_shards:
        per_o = N // out_shards // tn
        assert (N // out_shards) % tn == 0
        o_blk, o_idx = (None, tm, tn), lambda i, j, k: (_divmod(j, per_o)[0], i, _divmod(j, per_o)[1])
        o_shape = (out_shards, M, N // out_shards)
    else:
        o_blk, o_idx, o_shape = (tm, tn), (lambda i, j, k: (i, j)), (M, N)
    c_idx = lambda i, j, k: (i, j)
    if n_outer:
        sw = lambda f: (lambda j, i, k: f(i, j, k))
        a_idx, b_idx, o_idx, c_idx = sw(a_idx), sw(b_idx), sw(o_idx), sw(c_idx)
        grid = (N // tn, M // tm, nk)
    else:
        grid = (M // tm, N // tn, nk)
    has_add = add is not None

    def body(*refs):
        if has_add:
            a_ref, b_ref, c_ref, o_ref, acc_ref = refs
        else:
            a_ref, b_ref, o_ref, acc_ref = refs
            c_ref = None
        part = lax.dot_general(a_ref[...].astype(BF16), b_ref[...].astype(BF16), dn, preferred_element_type=F32)

        def fin(val):
            if has_add:
                val = val + c_ref[...]
            o_ref[...] = val.astype(out_dtype)

        if nk == 1:
            fin(part)
        else:
            k = pl.program_id(2)

            @pl.when(k == 0)
            def _():
                acc_ref[...] = part

            @pl.when(k > 0)
            def _():
                acc_ref[...] += part

            @pl.when(k == nk - 1)
            def _():
                fin(acc_ref[...])

    in_specs = [pl.BlockSpec(a_blk, a_idx), pl.BlockSpec(b_blk, b_idx)]
    args = [a, b]
    if has_add:
        in_specs.append(pl.BlockSpec((tm, tn), c_idx))
        args.append(add)
    acc_shape = (tm, tn) if nk > 1 else (8, 128)
    return _call(body, name=name, grid=grid, in_specs=in_specs, out_specs=pl.BlockSpec(o_blk, o_idx),
                 out_shape=_sds(o_shape, out_dtype), scratch=[pltpu.VMEM(acc_shape, F32)],
                 sem=("arbitrary",) * 3 if side is not None else ("parallel", "parallel", "arbitrary"), side=side)(*args)


def _rms1_fwd(x, meta, w, side=None):
    seq, d = x.shape
    nt = (FRONT + seq) // RT

    def body(x_ref, m_ref, w_ref, o_ref):
        i = pl.program_id(0)

        def norm(v):
            r = lax.rsqrt(jnp.mean(v * v, axis=-1, keepdims=True) + EPS)
            return (v * r * w_ref[...]).astype(BF16)

        @pl.when(i == 0)
        def _():
            o_ref[0:RT - N_META, :] = jnp.zeros((RT - N_META, d), BF16)
            o_ref[RT - N_META:RT, :] = norm(m_ref[...])

        @pl.when(i > 0)
        def _():
            o_ref[...] = norm(x_ref[...])

    return _call(body, name="rms1_fwd", grid=(nt,),
                 in_specs=[pl.BlockSpec((RT, d), lambda i: (jnp.maximum(i - 1, 0), 0)),
                           pl.BlockSpec((N_META, d), lambda i: (0, 0)),
                           pl.BlockSpec((1, d), lambda i: (0, 0))],
                 out_specs=pl.BlockSpec((RT, d), lambda i: (i, 0)),
                 out_shape=_sds((FRONT + seq, d), BF16), sem=("arbitrary",) if side is not None else ("parallel",),
                 side=side)(x, meta, w)


def _rms1_bwd(x, meta, w, dxn, dh1):
    seq, d = x.shape
    nt = (FRONT + seq) // RT

    def body(x_ref, m_ref, w_ref, g_ref, r_ref, dx_ref, dm_ref, dw_ref):
        i = pl.program_id(0)

        def bwd(v, g):
            r = lax.rsqrt(jnp.mean(v * v, axis=-1, keepdims=True) + EPS)
            vh = v * r
            gh = g * w_ref[...]
            return r * (gh - vh * jnp.mean(gh * vh, axis=-1, keepdims=True)), jnp.sum(g * vh, axis=0, keepdims=True)

        @pl.when(i == 0)
        def _():
            dm, dw = bwd(m_ref[...], g_ref[RT - N_META:RT, :])
            dm_ref[...] = dm
            dw_ref[...] = dw

        @pl.when(i > 0)
        def _():
            dx, dw = bwd(x_ref[...], g_ref[...])
            dx_ref[...] = dx + r_ref[...]
            dw_ref[...] += dw

    xs = pl.BlockSpec((RT, d), lambda i: (jnp.maximum(i - 1, 0), 0))
    return _call(body, name="rms1_bwd", grid=(nt,),
                 in_specs=[xs, pl.BlockSpec((N_META, d), lambda i: (0, 0)), pl.BlockSpec((1, d), lambda i: (0, 0)),
                           pl.BlockSpec((RT, d), lambda i: (i, 0)), xs],
                 out_specs=[xs, pl.BlockSpec((N_META, d), lambda i: (0, 0)), pl.BlockSpec((1, d), lambda i: (0, 0))],
                 out_shape=[_sds((seq, d), F32), _sds((N_META, d), F32), _sds((1, d), F32)],
                 sem=("arbitrary",))(x, meta, w, dxn, dh1)


def _merge_fwd(proj, za, zb):
    seq, d = za.shape
    off = FRONT // RT
    ca, cb = 8 * HW // d, 8 * HW // d + 1

    def body(ga_ref, gb_ref, za_ref, zb_ref, o_ref):
        o_ref[...] = (_sigmoid(ga_ref[...]) * za_ref[...] + _sigmoid(gb_ref[...]) * zb_ref[...]).astype(BF16)

    zs = pl.BlockSpec((RT, d), lambda i: (i, 0))
    return _call(body, name="merge_fwd", grid=(seq // RT,),
                 in_specs=[pl.BlockSpec((RT, d), lambda i: (i + off, ca)), pl.BlockSpec((RT, d), lambda i: (i + off, cb)), zs, zs],
                 out_specs=zs, out_shape=_sds((seq, d), BF16), sem=("parallel",))(proj, proj, za, zb)


def _merge_bwd(proj, za, zb, dmerged):
    seq, d = za.shape
    off = FRONT // RT
    ca, cb = 8 * HW // d, 8 * HW // d + 1
    nt = (FRONT + seq) // RT

    def body(ga_ref, gb_ref, za_ref, zb_ref, dm_ref, dza_ref, dzb_ref, dg_ref):
        i = pl.program_id(0)

        @pl.when(i < off)
        def _():
            dg_ref[...] = jnp.zeros((RT, 2 * d), BF16)

        @pl.when(i >= off)
        def _():
            sa, sb, dm = _sigmoid(ga_ref[...]), _sigmoid(gb_ref[...]), dm_ref[...]
            dza_ref[...] = (dm * sa).astype(BF16)
            dzb_ref[...] = (dm * sb).astype(BF16)
            dg_ref[:, 0:d] = (dm * za_ref[...] * sa * (1.0 - sa)).astype(BF16)
            dg_ref[:, d:2 * d] = (dm * zb_ref[...] * sb * (1.0 - sb)).astype(BF16)

    rs = pl.BlockSpec((RT, d), lambda i: (jnp.maximum(i - off, 0), 0))
    return _call(body, name="merge_bwd", grid=(nt,),
                 in_specs=[pl.BlockSpec((RT, d), lambda i: (i, ca)), pl.BlockSpec((RT, d), lambda i: (i, cb)), rs, rs, rs],
                 out_specs=[rs, rs, pl.BlockSpec((RT, 2 * d), lambda i: (i, 0))],
                 out_shape=[_sds((seq, d), BF16), _sds((seq, d), BF16), _sds((FRONT + seq, 2 * d), BF16)],
                 sem=("arbitrary",))(proj, proj, za, zb, dmerged)


def _resid_norm_fwd(x, mix, w):
    seq, d = x.shape

    def body(x_ref, m_ref, w_ref, h_ref, n_ref):
        h = x_ref[...] + m_ref[...]
        h_ref[...] = h
        r = lax.rsqrt(jnp.mean(h * h, axis=-1, keepdims=True) + EPS)
        n_ref[...] = (h * r * w_ref[...]).astype(BF16)

    rs = pl.BlockSpec((RT, d), lambda i: (i, 0))
    return _call(body, name="resid_norm_fwd", grid=(seq // RT,),
                 in_specs=[rs, rs, pl.BlockSpec((1, d), lambda i: (0, 0))], out_specs=[rs, rs],
                 out_shape=[_sds((seq, d), F32), _sds((seq, d), BF16)], sem=("parallel",))(x, mix, w)


def _resid_norm_bwd(h1, w, dn2, dh2):
    seq, d = h1.shape

    def body(h_ref, w_ref, g_ref, r_ref, o_ref, ob_ref, dw_ref):
        i = pl.program_id(0)
        h, g = h_ref[...], g_ref[...]
        r = lax.rsqrt(jnp.mean(h * h, axis=-1, keepdims=True) + EPS)
        hh = h * r
        gh = g * w_ref[...]
        dh = r_ref[...] + r * (gh - hh * jnp.mean(gh * hh, axis=-1, keepdims=True))
        o_ref[...] = dh
        ob_ref[...] = dh.astype(BF16)
        dw = jnp.sum(g * hh, axis=0, keepdims=True)

        @pl.when(i == 0)
        def _():
            dw_ref[...] = dw

        @pl.when(i > 0)
        def _():
            dw_ref[...] += dw

    rs = pl.BlockSpec((RT, d), lambda i: (i, 0))
    ws = pl.BlockSpec((1, d), lambda i: (0, 0))
    return _call(body, name="resid_norm_bwd", grid=(seq // RT,), in_specs=[rs, ws, rs, rs], out_specs=[rs, rs, ws],
                 out_shape=[_sds((seq, d), F32), _sds((seq, d), BF16), _sds((1, d), F32)], sem=("arbitrary",))(h1, w, dn2, dh2)


def _swiglu_tiles(seq, ff):
    return min(512, seq), (1408 if ff % 1408 == 0 else 512)


def _swiglu_fwd(gu):
    seq, f2 = gu.shape
    ff = f2 // 2
    rt, tc = _swiglu_tiles(seq, ff)
    nb = ff // tc

    def body(g_ref, u_ref, o_ref):
        o_ref[...] = (_silu(g_ref[...].astype(F32)) * u_ref[...].astype(F32)).astype(BF16)

    return _call(body, name="swiglu_fwd", grid=(seq // rt, nb),
                 in_specs=[pl.BlockSpec((rt, tc), lambda i, j: (i, j)), pl.BlockSpec((rt, tc), lambda i, j: (i, j + nb))],
                 out_specs=pl.BlockSpec((rt, tc), lambda i, j: (i, j)), out_shape=_sds((seq, ff), BF16),
                 sem=("parallel", "parallel"))(gu, gu)


def _swiglu_bwd(gu, dact):
    seq, f2 = gu.shape
    ff = f2 // 2
    rt, tc = _swiglu_tiles(seq, ff)
    nb = ff // tc

    def body(g_ref, u_ref, d_ref, o_ref):
        g, d = g_ref[...].astype(F32), d_ref[...].astype(F32)
        o_ref[0] = (d * u_ref[...].astype(F32) * _dsilu(g)).astype(BF16)
        o_ref[1] = (d * _silu(g)).astype(BF16)

    bs = pl.BlockSpec((rt, tc), lambda i, j: (i, j))
    return _call(body, name="swiglu_bwd", grid=(seq // rt, nb),
                 in_specs=[bs, pl.BlockSpec((rt, tc), lambda i, j: (i, j + nb)), bs],
                 out_specs=pl.BlockSpec((2, rt, tc), lambda i, j: (0, i, j)),
                 out_shape=_sds((2, seq, ff), BF16), sem=("parallel", "parallel"))(gu, gu, dact)


def _loss_head(h1, f, w, tgt):
    seq, d = h1.shape
    nt = seq // RT

    def body(h_ref, f_ref, w_ref, t_ref, l_ref, dh_ref, dhb_ref, dw_ref):
        i = pl.program_id(0)
        h = h_ref[...] + f_ref[...]
        r = lax.rsqrt(jnp.mean(h * h, axis=-1, keepdims=True) + EPS)
        hh = h * r
        err = hh * w_ref[...] - t_ref[...]
        l_ref[...] = jnp.full((8, 128), 0.5 * jnp.sum(jnp.mean(err * err, axis=-1, keepdims=True)), F32)
        dy = err * (1.0 / d)
        gh = dy * w_ref[...]
        dh = r * (gh - hh * jnp.mean(gh * hh, axis=-1, keepdims=True))
        dh_ref[...] = dh
        dhb_ref[...] = dh.astype(BF16)
        dw = jnp.sum(dy * hh, axis=0, keepdims=True)

        @pl.when(i == 0)
        def _():
            dw_ref[...] = dw

        @pl.when(i > 0)
        def _():
            dw_ref[...] += dw

    rs = pl.BlockSpec((RT, d), lambda i: (i, 0))
    ws = pl.BlockSpec((1, d), lambda i: (0, 0))
    return _call(body, name="loss_head", grid=(nt,), in_specs=[rs, rs, ws, rs],
                 out_specs=[pl.BlockSpec((8, 128), lambda i: (i, 0)), rs, rs, ws],
                 out_shape=[_sds((nt * 8, 128), F32), _sds((seq, d), F32), _sds((seq, d), BF16), _sds((1, d), F32)],
                 sem=("arbitrary",))(h1, f, w, tgt)


def _sum_tiles(lt):
    n = lt.shape[0]

    def body(l_ref, o_ref):
        v = l_ref[...]
        r = lax.broadcasted_iota(jnp.int32, v.shape, 0)
        c = lax.broadcasted_iota(jnp.int32, v.shape, 1)
        o_ref[...] = jnp.sum(jnp.where((r % 8 == 0) & (c == 0), v, 0.0), keepdims=True)

    return _call(body, name="loss_sum", grid=(1,), in_specs=[pl.BlockSpec((n, 128), lambda i: (0, 0))],
                 out_specs=pl.BlockSpec((1, 1), lambda i: (0, 0)), out_shape=_sds((1, 1), F32))(lt)


def _gated_norm_fwd(o, g, nw):
    r = lax.rsqrt(jnp.mean(o * o, axis=-1, keepdims=True) + EPS)
    return o * r * nw * _silu(g)


def _gated_norm_bwd(o, g, nw, dout):
    r = lax.rsqrt(jnp.mean(o * o, axis=-1, keepdims=True) + EPS)
    oh = o * r
    don = dout * _silu(g)
    dg = dout * (oh * nw) * _dsilu(g)
    dnw = jnp.sum(don * oh, axis=0, keepdims=True)
    doh = don * nw
    return r * (doh - oh * jnp.mean(doh * oh, axis=-1, keepdims=True)), dg, dnw


def _hg_gates(fs, lbl):
    l0, l1 = lbl[0:1, :], lbl[1:2, :]
    m = jnp.maximum(l0, l1)
    e0, e1 = jnp.exp(l0 - m), jnp.exp(l1 - m)
    lb = e0 / (e0 + e1)
    sig = _sigmoid(fs)
    f = lb + (1.0 - lb) * sig
    return lb, sig, f, jnp.log(f), (1.0 - lb) * (1.0 - sig)


def _cumsum(w):
    row = lax.broadcasted_iota(jnp.int32, w.shape, 0) & (CH - 1)
    s = 1
    while s < CH:
        w = w + jnp.where(row >= s, pltpu.roll(w, s, 0), 0.0)
        s *= 2
    return w


def _rcumsum(w):
    row = lax.broadcasted_iota(jnp.int32, w.shape, 0) & (CH - 1)
    s = 1
    while s < CH:
        w = w + jnp.where(row < CH - s, pltpu.roll(w, w.shape[0] - s, 0), 0.0)
        s *= 2
    return w


def _decay_blocks(q, k, b, p_ref):
    p_ref[...] = jnp.zeros((CH, CH), F32)
    m16 = _tri(SUBH, "incl")
    for I in range(CH // SUBH):
        s0 = I * SUBH
        bI, qI, kI = b[s0:s0 + SUBH], q[s0:s0 + SUBH], k[s0:s0 + SUBH]
        dec = jnp.exp(jnp.minimum(bI[:, None, :] - bI[None, :, :], 0.0))
        pii = jnp.sum(qI[:, None, :] * kI[None, :, :] * dec, axis=-1)
        p_ref[s0:s0 + SUBH, s0:s0 + SUBH] = jnp.where(m16, pii, 0.0)
        if I > 0:
            rI = b[s0 - 1:s0]
            qs = qI * jnp.exp(bI - rI)
            ks = k[0:s0] * jnp.exp(rI - b[0:s0])
            p_ref[s0:s0 + SUBH, 0:s0] = _dot(qs, ks, NT)


HPS = 8
HPS_HGRN2_FWD = 4
HB = 4
NHB = NH // HPS
OFF = FRONT // RT


def _head_specs(hps, rev=None):
    row = (lambda i: i) if rev is None else rev
    col = lambda g: pl.BlockSpec((RT, hps * DH), lambda h, i: (row(i), g * (NH // hps) + h))
    full = pl.BlockSpec((RT, hps * DH), lambda h, i: (row(i), h))
    real = pl.BlockSpec((RT, hps * DH), lambda h, i: (jnp.maximum(row(i) - OFF, 0), h))
    state = lambda cpt: pl.BlockSpec((hps, cpt, DH, DH), lambda h, i: (h, row(i), 0, 0))
    scal = pl.BlockSpec((hps, RT, DH), lambda h, i: (h, row(i), 0))
    return col, full, real, state, scal


def _hgrn2_fwd(proj, lb_logits, nw):
    tp = proj.shape[0]
    nt, cpt = tp // RT, RT // CH
    hps = HPS_HGRN2_FWD

    def body(q_ref, f_ref, i_ref, g_ref, lbl_ref, nw_ref, og_ref, or_ref, st_ref, pm_ref, s_ref, p_ref):
        @pl.when(pl.program_id(1) == 0)
        def _():
            s_ref[...] = jnp.zeros((hps, DH, DH), F32)

        def chunk(c, carry):
            rows = pl.ds(pl.multiple_of(c * CH, CH), CH)
            for hh in range(hps):
                cols = slice(hh * DH, (hh + 1) * DH)
                _, _, _, w, k = _hg_gates(f_ref[rows, cols], lbl_ref[:, cols])
                q, v = _silu(q_ref[rows, cols]), i_ref[rows, cols]
                b = _cumsum(w)
                st = s_ref[hh]
                st_ref[hh, c] = st
                _decay_blocks(q, k, b, p_ref.at[hh])
                pm_ref[hh, c] = p_ref[hh]
                o = _dot(q * jnp.exp(b), st, NT) + _dot(p_ref[hh], v)
                bl = b[CH - 1:CH]
                s_ref[hh] = st * jnp.exp(bl) + _dot(v, k * jnp.exp(bl - b), TN)
                or_ref[rows, cols] = o
                og_ref[rows, cols] = _gated_norm_fwd(o, g_ref[rows, cols], nw_ref[...]).astype(BF16)
            return carry

        lax.fori_loop(0, cpt, chunk, 0)

    col, full, real, state, _ = _head_specs(hps)
    return _call(body, name="hgrn2_fwd", grid=(NH // hps, nt),
                 in_specs=[col(0), col(1), col(2), col(3), pl.BlockSpec((2, hps * DH), lambda h, i: (0, h)),
                           pl.BlockSpec((1, DH), lambda h, i: (0, 0))],
                 out_specs=[real, full, state(cpt), pl.BlockSpec((hps, cpt, CH, CH), lambda h, i: (h, i, 0, 0))],
                 out_shape=[_sds((tp - FRONT, HW), BF16), _sds((tp, HW), F32), _sds((NH, tp // CH, DH, DH), F32),
                            _sds((NH, tp // CH, CH, CH), F32)],
                 scratch=[pltpu.VMEM((hps, DH, DH), F32), pltpu.VMEM((hps, CH, CH), F32)],
                 sem=("parallel", "arbitrary"))(proj, proj, proj, proj, lb_logits, nw)


def _hgrn2_bwd(proj, lb_logits, nw, o_raw, states, pmat, dog, side=None):
    tp = proj.shape[0]
    nt, cpt = tp // RT, RT // CH

    def body(q_ref, f_ref, i_ref, g_ref, lbl_ref, nw_ref, or_ref, st_ref, pm_ref, dog_ref,
             dq_ref, df_ref, di_ref, dg_ref, dl_ref, dnw_ref, ds_ref, dk_ref, dqa_ref, do_ref):
        step = pl.program_id(1)

        @pl.when(step == 0)
        def _():
            ds_ref[...] = jnp.zeros((HPS, DH, DH), F32)
            dl_ref[...] = jnp.zeros((2, HPS * DH), F32)

        @pl.when((step == 0) & (pl.program_id(0) == 0))
        def _():
            dnw_ref[...] = jnp.zeros((1, DH), F32)

        front = nt - 1 - step < OFF
        for hh in range(HPS):
            cols = slice(hh * DH, (hh + 1) * DH)
            dog_t = jnp.where(front, 0.0, dog_ref[:, cols])
            do_t, dg_t, dnw = _gated_norm_bwd(or_ref[:, cols], g_ref[:, cols], nw_ref[...], dog_t)
            do_ref[:, cols] = do_t
            dg_ref[:, cols] = dg_t.astype(BF16)
            dnw_ref[...] += dnw
        tril = _tri(CH, "incl")
        m16 = _tri(SUBH, "incl")

        def chunk(cc, carry):
            c = cpt - 1 - cc
            rows = pl.ds(pl.multiple_of(c * CH, CH), CH)
            for hh in range(HPS):
                cols = slice(hh * DH, (hh + 1) * DH)
                fs = f_ref[rows, cols]
                lb, sig, f, w, k = _hg_gates(fs, lbl_ref[:, cols])
                hq = q_ref[rows, cols]
                q, v, do = _silu(hq), i_ref[rows, cols], do_ref[rows, cols]
                b = _cumsum(w)
                bl = b[CH - 1:CH]
                eb = jnp.exp(b)
                qs, kd = q * eb, k * jnp.exp(bl - b)
                st, dst = st_ref[hh, c], ds_ref[hh]
                dv = _dot(pm_ref[hh, c], do, TN) + _dot(kd, dst, NT)
                dp = jnp.where(tril, _dot(do, v, NT), 0.0)
                dqa, dka = dqa_ref.at[hh], dk_ref.at[hh]
                dqa[...] = eb * _dotx(do, st)
                dka[...] = jnp.exp(bl - b) * _dotx(v, dst)
                for I in range(CH // SUBH):
                    s0 = I * SUBH
                    bI, qI, kI = b[s0:s0 + SUBH], q[s0:s0 + SUBH], k[s0:s0 + SUBH]
                    dec = jnp.exp(jnp.minimum(bI[:, None, :] - bI[None, :, :], 0.0))
                    dpii = jnp.where(m16, dp[s0:s0 + SUBH, s0:s0 + SUBH], 0.0)[:, :, None] * dec
                    dqa[s0:s0 + SUBH, :] += jnp.sum(dpii * kI[None, :, :], axis=1)
                    dka[s0:s0 + SUBH, :] += jnp.sum(dpii * qI[:, None, :], axis=0)
                    if I > 0:
                        rI = b[s0 - 1:s0]
                        eq, ek = jnp.exp(bI - rI), jnp.exp(rI - b[0:s0])
                        dpij = dp[s0:s0 + SUBH, 0:s0]
                        dqa[s0:s0 + SUBH, :] += eq * _dotx(dpij, k[0:s0] * ek)
                        dka[0:s0, :] += ek * _dotx(dpij, qI * eq, TN)
                dq, dk = dqa[...], dka[...]
                st_end = st * jnp.exp(bl) + _dotx(v, kd, TN)
                dw = _rcumsum(q * dq - k * dk) + jnp.sum(dst * st_end, axis=0, keepdims=True)
                ds_ref[hh] = dst * jnp.exp(bl) + _dotx(do, qs, TN)
                one_m = 1.0 - sig
                dq_ref[rows, cols] = (dq * _dsilu(hq)).astype(BF16)
                df_ref[rows, cols] = ((dw / f - dk) * (1.0 - lb) * sig * one_m).astype(BF16)
                di_ref[rows, cols] = dv.astype(BF16)
                dl_ref[0:1, cols] += jnp.sum((dw / f - dk) * one_m, axis=0, keepdims=True)
            return carry

        lax.fori_loop(0, cpt, chunk, 0)

        @pl.when(step == nt - 1)
        def _():
            lbl = lbl_ref[...]
            l0, l1 = lbl[0:1, :], lbl[1:2, :]
            m = jnp.maximum(l0, l1)
            e0, e1 = jnp.exp(l0 - m), jnp.exp(l1 - m)
            p0 = e0 / (e0 + e1)
            dl0 = dl_ref[0:1, :] * p0 * (1.0 - p0)
            dl_ref[0:1, :] = dl0
            dl_ref[1:2, :] = -dl0

    col, full, real, state, _ = _head_specs(HPS, lambda i: nt - 1 - i)
    lbs = pl.BlockSpec((2, HPS * DH), lambda h, i: (0, h))
    return _call(body, name="hgrn2_bwd", grid=(NHB, nt),
                 in_specs=[col(0), col(1), col(2), col(3), lbs, pl.BlockSpec((1, DH), lambda h, i: (0, 0)), full,
                           state(cpt), pl.BlockSpec((HPS, cpt, CH, CH), lambda h, i: (h, nt - 1 - i, 0, 0)), real],
                 out_specs=[full, full, full, full, lbs, pl.BlockSpec((1, DH), lambda h, i: (0, 0))],
                 out_shape=[_sds((tp, HW), BF16)] * 4 + [_sds((2, HW), F32), _sds((1, DH), F32)],
                 scratch=[pltpu.VMEM((HPS, DH, DH), F32), pltpu.VMEM((HPS, CH, DH), F32),
                          pltpu.VMEM((HPS, CH, DH), F32), pltpu.VMEM((RT, HPS * DH), F32)],
                 sem=("arbitrary", "arbitrary"), side=side)(proj, proj, proj, proj, lb_logits, nw, o_raw, states, pmat, dog)


GQ0 = 4 * HW
CW = 3 * HW


def _gd_scalars(ab, alog, dtb):
    g = -jnp.exp(alog) * jax.nn.softplus(ab + dtb)
    return g, _sigmoid(ab)


def _conv_ext_specs(row_of):
    main = [pl.BlockSpec((RT, HW), lambda i, g=g: (row_of(i), GQ0 // HW + g)) for g in range(3)]
    prev = [pl.BlockSpec((8, HW), lambda i, g=g: (jnp.maximum(row_of(i) * (RT // 8) - 1, 0), GQ0 // HW + g)) for g in range(3)]
    return main + prev


def _conv_fill(ext_ref, xs, xps, first):
    for g in range(3):
        ext_ref[0:8, g * HW:(g + 1) * HW] = jnp.where(first, 0.0, xps[g][...])
        ext_ref[8:8 + RT, g * HW:(g + 1) * HW] = xs[g][...]


def _conv_apply(ext_ref, cw):
    y = cw[CONV_K - 1:CONV_K, :] * ext_ref[pl.ds(8, RT), :]
    for s in range(1, CONV_K):
        y += cw[CONV_K - 1 - s:CONV_K - s, :] * ext_ref[pl.ds(8 - s, RT), :]
    return y


def _gdn_prep_fwd(proj, pab, conv_w, alog, dtb):
    tp = proj.shape[0]
    nt = tp // RT

    def body(x0, x1, x2, p0, p1, p2, ab_ref, cw_ref, al_ref, dt_ref, q_ref, k_ref, v_ref, g_ref, b_ref, ext_ref):
        _conv_fill(ext_ref, (x0, x1, x2), (p0, p1, p2), pl.program_id(0) == 0)
        a = _silu(_conv_apply(ext_ref, cw_ref[...]))
        for h in range(NH):
            for part, ref, sc in ((0, q_ref, DH ** -0.5), (1, k_ref, 1.0)):
                seg = a[:, part * HW + h * DH:part * HW + (h + 1) * DH]
                ref[:, h * DH:(h + 1) * DH] = seg * (lax.rsqrt(jnp.sum(seg * seg, axis=-1, keepdims=True) + EPS) * sc)
        v_ref[...] = a[:, 2 * HW:3 * HW]
        g, beta = _gd_scalars(ab_ref[...], al_ref[...], dt_ref[...])
        for h in range(NH):
            g_ref[h] = jnp.broadcast_to(g[:, h:h + 1], (RT, DH))
            b_ref[h] = jnp.broadcast_to(beta[:, NH + h:NH + h + 1], (RT, DH))

    hs = pl.BlockSpec((RT, HW), lambda i: (i, 0))
    sc = pl.BlockSpec((NH, RT, DH), lambda i: (0, i, 0))
    one = pl.BlockSpec((1, DH), lambda i: (0, 0))
    return _call(body, name="gdn_prep_fwd", grid=(nt,),
                 in_specs=_conv_ext_specs(lambda i: i) + [pl.BlockSpec((RT, DH), lambda i: (i, 0)),
                                                           pl.BlockSpec((CONV_K, CW), lambda i: (0, 0)), one, one],
                 out_specs=[hs, hs, hs, sc, sc],
                 out_shape=[_sds((tp, HW), F32)] * 3 + [_sds((NH, tp, DH), F32)] * 2,
                 scratch=[pltpu.VMEM((RT + 8, CW), F32)], sem=("parallel",))(*([proj] * 6), pab, conv_w, alog, dtb)


def _gdn_prep_bwd(proj, pab, conv_w, alog, dtb, dq, dk, dv, dgb, dbb):
    tp = proj.shape[0]
    nt = tp // RT

    def body(x0, x1, x2, p0, p1, p2, ab_ref, cw_ref, al_ref, dt_ref, dq_ref, dk_ref, dv_ref, dg_ref, db_ref,
             dx_ref, dab_ref, dcw_ref, dal_ref, ddt_ref, ext_ref, dy_ref):
        step = pl.program_id(0)
        i = nt - 1 - step

        @pl.when(step == 0)
        def _():
            dy_ref[RT:RT + 8, :] = jnp.zeros((8, CW), F32)
            dcw_ref[...] = jnp.zeros((8, CW), F32)
            dal_ref[...] = jnp.zeros((1, DH), F32)
            ddt_ref[...] = jnp.zeros((1, DH), F32)

        _conv_fill(ext_ref, (x0, x1, x2), (p0, p1, p2), i == 0)
        cw = cw_ref[...]
        y = _conv_apply(ext_ref, cw)
        a = _silu(y)
        dsl = _dsilu(y)
        for h in range(NH):
            for part, ref, sc in ((0, dq_ref, DH ** -0.5), (1, dk_ref, 1.0)):
                lo = part * HW + h * DH
                seg = a[:, lo:lo + DH]
                r = lax.rsqrt(jnp.sum(seg * seg, axis=-1, keepdims=True) + EPS)
                xh = seg * r
                dxh = ref[:, h * DH:(h + 1) * DH] * sc
                dy_ref[0:RT, lo:lo + DH] = r * (dxh - xh * jnp.sum(dxh * xh, axis=-1, keepdims=True)) * dsl[:, lo:lo + DH]
        dy_ref[0:RT, 2 * HW:3 * HW] = dv_ref[...] * dsl[:, 2 * HW:3 * HW]
        dy = dy_ref[0:RT, :]
        dx = cw[CONV_K - 1:CONV_K, :] * dy
        dcw_ref[CONV_K - 1:CONV_K, :] += jnp.sum(dy * ext_ref[pl.ds(8, RT), :], axis=0, keepdims=True)
        for s in range(1, CONV_K):
            dx += cw[CONV_K - 1 - s:CONV_K - s, :] * dy_ref[pl.ds(s, RT), :]
            dcw_ref[CONV_K - 1 - s:CONV_K - s, :] += jnp.sum(dy * ext_ref[pl.ds(8 - s, RT), :], axis=0, keepdims=True)
        dx_ref[...] = dx.astype(BF16)
        dy_ref[RT:RT + 8, :] = dy[0:8, :]
        ab = ab_ref[...]
        g, beta = _gd_scalars(ab, al_ref[...], dt_ref[...])
        lane = lax.broadcasted_iota(jnp.int32, (RT, DH), 1)
        dgl = jnp.zeros((RT, DH), F32)
        dbl = jnp.zeros((RT, DH), F32)
        for h in range(NH):
            dgl = jnp.where(lane == h, dg_ref[h], dgl)
            dbl = jnp.where(lane == NH + h, db_ref[h], dbl)
        dsp = dgl * (-jnp.exp(al_ref[...])) * _sigmoid(ab + dt_ref[...])
        dab_ref[...] = (dsp + dbl * beta * (1.0 - beta)).astype(BF16)
        ddt_ref[...] += jnp.sum(dsp, axis=0, keepdims=True)
        dal_ref[...] += jnp.sum(dgl * g, axis=0, keepdims=True)

    hs = pl.BlockSpec((RT, HW), lambda s: (nt - 1 - s, 0))
    sc = pl.BlockSpec((NH, RT, DH), lambda s: (0, nt - 1 - s, 0))
    one = pl.BlockSpec((1, DH), lambda s: (0, 0))
    xs = pl.BlockSpec((RT, CW), lambda s: (nt - 1 - s, 0))
    return _call(body, name="gdn_prep_bwd", grid=(nt,),
                 in_specs=_conv_ext_specs(lambda s: nt - 1 - s) + [
                     pl.BlockSpec((RT, DH), lambda s: (nt - 1 - s, 0)), pl.BlockSpec((CONV_K, CW), lambda s: (0, 0)),
                     one, one, hs, hs, hs, sc, sc],
                 out_specs=[xs, pl.BlockSpec((RT, DH), lambda s: (nt - 1 - s, 0)), pl.BlockSpec((8, CW), lambda s: (0, 0)), one, one],
                 out_shape=[_sds((tp, CW), BF16), _sds((tp, DH), BF16), _sds((8, CW), F32), _sds((1, DH), F32), _sds((1, DH), F32)],
                 scratch=[pltpu.VMEM((RT + 8, CW), F32), pltpu.VMEM((RT + 8, CW), F32)],
                 sem=("arbitrary",))(*([proj] * 6), pab, conv_w, alog, dtb, dq, dk, dv, dgb, dbb)


HS = HB * CH


def _stack_heads(ref, rows, base):
    return jnp.concatenate([ref[rows, (base + hh) * DH:(base + hh + 1) * DH] for hh in range(HB)], axis=0)


def _stack_scal(ref, rows, base):
    return jnp.concatenate([ref[base + hh, rows, :] for hh in range(HB)], axis=0)


def _store_heads(ref, rows, base, val):
    for hh in range(HB):
        ref[rows, (base + hh) * DH:(base + hh + 1) * DH] = val[hh * CH:(hh + 1) * CH].astype(ref.dtype)


def _bd_masks():
    r = lax.broadcasted_iota(jnp.int32, (HS, HS), 0)
    c = lax.broadcasted_iota(jnp.int32, (HS, HS), 1)
    same = lax.shift_right_logical(r, int(math.log2(CH))) == lax.shift_right_logical(c, int(math.log2(CH)))
    return same & (r >= c), same & (r > c)


def _unit_lower_inverse(a):
    n = a.shape[0]
    r = lax.broadcasted_iota(jnp.int32, (n, n), 0)
    c = lax.broadcasted_iota(jnp.int32, (n, n), 1)
    blk_of = lambda t, size: lax.shift_right_logical(t, int(math.log2(size)))
    a16 = jnp.where(blk_of(r, SUB) == blk_of(c, SUB), a, 0.0)
    x = (r == c).astype(F32) - a16
    p = a16
    for _ in range(3):
        p = _dot(p, p)
        x = x + _dot(x, p)
    for blk in (2 * SUB, 4 * SUB):
        off = jnp.where((blk_of(r, blk) == blk_of(c, blk)) & (blk_of(r, blk // 2) != blk_of(c, blk // 2)), a, 0.0)
        x = x - _dot(x, _dot(off, x))
    return x


def _gdn_chunk_common(q, k, v, gl, bt, incl, strict, x=None):
    gc = _cumsum(gl)
    e = jnp.exp(gc)
    rel = jnp.exp(jnp.minimum(gc[:, 0:1] - gc.T[0:1, :], 0.0))
    kb = bt * k
    a = jnp.where(strict, bt[:, 0:1] * _dot(k, k, NT) * rel, 0.0)
    if x is None:
        x = _unit_lower_inverse(a)
    wu = _dot(x, jnp.concatenate([kb * e, bt * v], axis=1))
    attn = jnp.where(incl, _dot(q, k, NT) * rel, 0.0)
    return gc, e, rel, kb, a, x, wu[:, 0:DH], wu[:, DH:2 * DH], attn


def _gdn_fwd(q, k, v, gb, bb, proj, nw):
    tp = q.shape[0]
    nt, cpt = tp // RT, RT // CH

    def body(q_ref, k_ref, v_ref, g_ref, b_ref, z_ref, nw_ref, og_ref, or_ref, st_ref, x_ref, s_ref):
        @pl.when(pl.program_id(1) == 0)
        def _():
            s_ref[...] = jnp.zeros((HPS, DH, DH), F32)

        incl, strict = _bd_masks()

        def chunk(c, carry):
            rows = pl.ds(pl.multiple_of(c * CH, CH), CH)
            for base in range(0, HPS, HB):
                qc, kc, vc = _stack_heads(q_ref, rows, base), _stack_heads(k_ref, rows, base), _stack_heads(v_ref, rows, base)
                gc, e, rel, kb, a, x, w, u, attn = _gdn_chunk_common(qc, kc, vc, _stack_scal(g_ref, rows, base),
                                                                    _stack_scal(b_ref, rows, base), incl, strict)
                x_ref[base // HB, c] = x
                qe = qc * e
                ws, qs = [], []
                for hh in range(HB):
                    blk = slice(hh * CH, (hh + 1) * CH)
                    s = s_ref[base + hh]
                    st_ref[base + hh, c] = s
                    both = _dot(jnp.concatenate([w[blk], qe[blk]], axis=0), s)
                    ws.append(both[0:CH])
                    qs.append(both[CH:2 * CH])
                vn = u - jnp.concatenate(ws, axis=0)
                o = jnp.concatenate(qs, axis=0) + _dot(attn, vn)
                for hh in range(HB):
                    blk = slice(hh * CH, (hh + 1) * CH)
                    gl = gc[(hh + 1) * CH - 1:(hh + 1) * CH]
                    s_ref[base + hh] = s_ref[base + hh] * jnp.exp(gl) + _dot(kc[blk] * jnp.exp(gl - gc[blk]), vn[blk], TN)
                _store_heads(or_ref, rows, base, o)
                for hh in range(HB):
                    cols = slice((base + hh) * DH, (base + hh + 1) * DH)
                    og_ref[rows, cols] = _gated_norm_fwd(o[hh * CH:(hh + 1) * CH], z_ref[rows, cols], nw_ref[...]).astype(BF16)
            return carry

        lax.fori_loop(0, cpt, chunk, 0)

    col, full, real, state, scal = _head_specs(HPS)
    return _call(body, name="gdn_fwd", grid=(NHB, nt),
                 in_specs=[full, full, full, scal, scal, col(7), pl.BlockSpec((1, DH), lambda h, i: (0, 0))],
                 out_specs=[real, full, state(cpt), pl.BlockSpec((HPS // HB, cpt, HS, HS), lambda h, i: (h, i, 0, 0))],
                 out_shape=[_sds((tp - FRONT, HW), BF16), _sds((tp, HW), F32), _sds((NH, tp // CH, DH, DH), F32),
                            _sds((NH // HB, tp // CH, HS, HS), F32)],
                 scratch=[pltpu.VMEM((HPS, DH, DH), F32)], sem=("parallel", "arbitrary"))(q, k, v, gb, bb, proj, nw)


def _gdn_bwd(q, k, v, gb, bb, proj, nw, o_raw, states, xinv, dog):
    tp = q.shape[0]
    nt, cpt = tp // RT, RT // CH

    def body(q_ref, k_ref, v_ref, g_ref, b_ref, z_ref, nw_ref, or_ref, st_ref, x_ref, dog_ref,
             dq_ref, dk_ref, dv_ref, dg_ref, db_ref, dz_ref, dnw_ref, ds_ref, do_ref):
        step = pl.program_id(1)

        @pl.when(step == 0)
        def _():
            ds_ref[...] = jnp.zeros((HPS, DH, DH), F32)

        @pl.when((step == 0) & (pl.program_id(0) == 0))
        def _():
            dnw_ref[...] = jnp.zeros((1, DH), F32)

        front = nt - 1 - step < OFF
        for hh in range(HPS):
            cols = slice(hh * DH, (hh + 1) * DH)
            dog_t = jnp.where(front, 0.0, dog_ref[:, cols])
            do_t, dz_t, dnw = _gated_norm_bwd(or_ref[:, cols], z_ref[:, cols], nw_ref[...], dog_t)
            do_ref[:, cols] = do_t
            dz_ref[:, cols] = dz_t.astype(BF16)
            dnw_ref[...] += dnw
        incl, strict = _bd_masks()
        last_row = (lax.broadcasted_iota(jnp.int32, (CH, 1), 0) == CH - 1)

        def rsum(t):
            return jnp.sum(t, axis=-1, keepdims=True)

        def chunk(cc, carry):
            c = cpt - 1 - cc
            rows = pl.ds(pl.multiple_of(c * CH, CH), CH)
            for base in range(0, HPS, HB):
                qc, kc, vc, do = (_stack_heads(q_ref, rows, base), _stack_heads(k_ref, rows, base), _stack_heads(v_ref, rows, base),
                                  _stack_heads(do_ref, rows, base))
                bt = _stack_scal(b_ref, rows, base)
                gc, e, rel, kb, a, x, w, u, attn = _gdn_chunk_common(qc, kc, vc, _stack_scal(g_ref, rows, base), bt, incl, strict,
                                                                    x=x_ref[base // HB, c])
                qe = qc * e
                heads = [slice(hh * CH, (hh + 1) * CH) for hh in range(HB)]
                gls = [gc[(hh + 1) * CH - 1:(hh + 1) * CH] for hh in range(HB)]
                cdec = jnp.concatenate([jnp.exp(gl - gc[blk]) for gl, blk in zip(gls, heads)], axis=0)
                kcd = kc * cdec
                vn = u - jnp.concatenate([_dot(w[blk], st_ref[base + hh, c]) for hh, blk in enumerate(heads)], axis=0)
                dos = jnp.concatenate([_dot(do[blk], st_ref[base + hh, c], NT) for hh, blk in enumerate(heads)], axis=0)
                kds = jnp.concatenate([_dot(kcd[blk], ds_ref[base + hh]) for hh, blk in enumerate(heads)], axis=0)
                vds = jnp.concatenate([_dot(vn[blk], ds_ref[base + hh], NT) for hh, blk in enumerate(heads)], axis=0)
                dvn = _dot(attn, do, TN) + kds
                dattn = jnp.where(incl, _dot(do, vn, NT), 0.0)
                dar = dattn * rel
                dq = _dot(dar, kc) + e * dos
                dk = _dot(dar, qc, TN) + cdec * vds
                dc = cdec[:, 0:1] * rsum(kc * vds)
                dgc = rsum(qe * dos) - dc
                dw = -jnp.concatenate([_dot(dvn[blk], st_ref[base + hh, c], NT) for hh, blk in enumerate(heads)], axis=0)
                extra = []
                for hh, blk in enumerate(heads):
                    s, dsn = st_ref[base + hh, c], ds_ref[base + hh]
                    el = jnp.exp(gls[hh])
                    dglast = jnp.sum(dc[blk], axis=0, keepdims=True) + el[:, 0:1] * jnp.sum(rsum(dsn * s), axis=0, keepdims=True)
                    extra.append(jnp.where(last_row, dglast, 0.0))
                    ds_ref[base + hh] = dsn * el + _dot(jnp.concatenate([qe[blk], -w[blk]], axis=0),
                                                 jnp.concatenate([do[blk], dvn[blk]], axis=0), TN)
                dr = _dot(x, jnp.concatenate([dw, dvn], axis=1), TN)
                drw, dru = dr[:, 0:DH], dr[:, DH:2 * DH]
                da = -jnp.where(strict, _dot(dr, jnp.concatenate([w, u], axis=1), NT), 0.0)
                dar2 = da * rel
                dkb = _dot(dar2, kc)
                rwk = rsum(drw * kc)
                dk = dk + _dot(dar2, kb, TN) + bt * dkb + (bt * e) * drw
                dbeta = rsum(dkb * kc) + e[:, 0:1] * rwk + rsum(dru * vc)
                z = dattn * attn + da * a
                dgc = dgc + bt[:, 0:1] * e[:, 0:1] * rwk + rsum(z) - rsum(z.T) + jnp.concatenate(extra, axis=0)
                _store_heads(dq_ref, rows, base, dq)
                _store_heads(dk_ref, rows, base, dk)
                _store_heads(dv_ref, rows, base, bt * dru)
                dg = _rcumsum(jnp.broadcast_to(dgc, (HS, DH)))
                dbb = jnp.broadcast_to(dbeta, (HS, DH))
                for hh, blk in enumerate(heads):
                    dg_ref[base + hh, rows, :] = dg[blk]
                    db_ref[base + hh, rows, :] = dbb[blk]
            return carry

        lax.fori_loop(0, cpt, chunk, 0)

    col, full, real, state, scal = _head_specs(HPS, lambda i: nt - 1 - i)
    one = pl.BlockSpec((1, DH), lambda h, i: (0, 0))
    return _call(body, name="gdn_bwd", grid=(NHB, nt),
                 in_specs=[full, full, full, scal, scal, col(7), one, full, state(cpt),
                           pl.BlockSpec((HPS // HB, cpt, HS, HS), lambda h, i: (h, nt - 1 - i, 0, 0)), real],
                 out_specs=[full, full, full, scal, scal, full, one],
                 out_shape=[_sds((tp, HW), F32)] * 3 + [_sds((NH, tp, DH), F32)] * 2 + [_sds((tp, HW), BF16), _sds((1, DH), F32)],
                 scratch=[pltpu.VMEM((HPS, DH, DH), F32), pltpu.VMEM((RT, HPS * DH), F32)],
                 sem=("arbitrary", "arbitrary"))(q, k, v, gb, bb, proj, nw, o_raw, states, xinv, dog)


MAIN_W = 8 * HW
AB_W = 2 * NH


def _w_in_pieces(lo, hi):
    total = MAIN_W + AB_W
    out = []
    if lo < MAIN_W:
        out.append(("m", lo, min(hi, MAIN_W)))
    if hi > MAIN_W and lo < total:
        out.append(("ab", max(lo, MAIN_W) - MAIN_W, min(hi, total) - MAIN_W))
    if hi > total:
        out.append(("m", max(lo, total) - AB_W, hi - AB_W))
    return out


W_IN_TILE = 128


def _unshard_w_in(slabs):
    ns, d, cs = slabs.shape
    main_w = ns * cs - AB_W

    def body(w_ref, m_ref, ab_ref):
        ab_ref[...] = jnp.zeros((W_IN_TILE, DH), slabs.dtype)
        for s_ in range(ns):
            off = 0
            for src, a, b in _w_in_pieces(s_ * cs, (s_ + 1) * cs):
                piece = w_ref[s_, :, off:off + (b - a)]
                (m_ref if src == "m" else ab_ref)[:, a:b] = piece
                off += b - a

    return _call(body, name="unshard_w_in", grid=(d // W_IN_TILE,),
                 in_specs=[pl.BlockSpec((ns, W_IN_TILE, cs), lambda i: (0, i, 0))],
                 out_specs=[pl.BlockSpec((W_IN_TILE, main_w), lambda i: (i, 0)), pl.BlockSpec((W_IN_TILE, DH), lambda i: (i, 0))],
                 out_shape=[_sds((d, main_w), slabs.dtype), _sds((d, DH), slabs.dtype)], sem=("parallel",))(slabs)


def _shard_w_in_grad(dw_main, dw_ab, ns=4):
    d, main_w = dw_main.shape
    cs = (main_w + AB_W) // ns

    def body(m_ref, ab_ref, o_ref):
        for s_ in range(ns):
            parts = [(m_ref if src == "m" else ab_ref)[:, a:b] for src, a, b in _w_in_pieces(s_ * cs, (s_ + 1) * cs)]
            o_ref[s_] = parts[0] if len(parts) == 1 else jnp.concatenate(parts, axis=1)

    return _call(body, name="shard_w_in_grad", grid=(d // W_IN_TILE,),
                 in_specs=[pl.BlockSpec((W_IN_TILE, main_w), lambda i: (i, 0)), pl.BlockSpec((W_IN_TILE, DH), lambda i: (i, 0))],
                 out_specs=pl.BlockSpec((ns, W_IN_TILE, cs), lambda i: (0, i, 0)),
                 out_shape=_sds((ns, d, cs), dw_main.dtype), sem=("parallel",))(dw_main, dw_ab)


def _pad_lanes(v):
    return jnp.pad(v, ((0, 0), (0, DH - v.shape[1])))


class _NoComm:
    def __init__(self, w_in, late):
        self.w_in, self.late = w_in, late

    def first_side(self):
        return None

    def first_weight(self, side_outs):
        return self.w_in

    def proj_side(self):
        return None

    def late_weights(self, side_outs):
        return self.late

    def early_grads_side(self, grads):
        return None

    def early_grads_done(self, side_outs):
        pass

    def last_grad_side(self, dw_in):
        return None

    def last_grad_done(self, side_outs):
        pass


def _local_step(x, tgt, meta, lb_logits, mix_w, hg_nw, conv_w, a_log, dt_bias, gd_nw, ffn_nw, final_w, comm):
    alog, dtb = _pad_lanes(a_log), _pad_lanes(dt_bias)
    final_w = final_w.reshape(1, -1)
    rows4 = lambda t: t.reshape(4, t.shape[0] // 4, t.shape[1])
    side = comm.first_side()
    xn = _rms1_fwd(x, meta, mix_w, side=side)
    xn, landed = xn if side is not None else (xn, None)
    w_main, w_ab = _unshard_w_in(comm.first_weight(landed))
    side = comm.proj_side()
    proj = _mm(xn, w_main, "nn", F32, 768, 2048, 2048, "proj_main", n_outer=True, side=side)
    proj, landed = proj if side is not None else (proj, None)
    w_a, w_b, w_out, w_ffn_in, w_ffn_out = comm.late_weights(landed)
    pab = _mm(xn, w_ab, "nn", F32, 768, 128, 2048, "proj_ab")
    oa_g, oa_raw, st_a, pm_a = _hgrn2_fwd(proj, lb_logits, hg_nw)
    q, k, v, gb, bb = _gdn_prep_fwd(proj, pab, conv_w, alog, dtb)
    ob_g, ob_raw, st_b, xinv = _gdn_fwd(q, k, v, gb, bb, proj, gd_nw)
    za = _mm(oa_g, w_a, "nn", F32, 1024, 512, 1024, "branch_a", n_outer=True)
    zb = _mm(ob_g, w_b, "nn", F32, 1024, 512, 1024, "branch_b", n_outer=True)
    merged = _merge_fwd(proj, za, zb)
    mix = _mm(merged, w_out, "nn", F32, 1024, 2048, 2048, "mix_out")
    h1, n2 = _resid_norm_fwd(x, mix, ffn_nw)
    gu = _mm(n2, w_ffn_in, "nn", BF16, 1024, 1408, 2048, "ffn_in", n_outer=True)
    act = _swiglu_fwd(gu)
    f = _mm(act, w_ffn_out, "nn", F32, 1024, 2048, 1408, "ffn_out")
    lt, dh2, dh2b, dfinal = _loss_head(h1, f, final_w, tgt)
    loss = _sum_tiles(lt)
    dact = _mm(dh2b, w_ffn_out, "nt", BF16, 1024, 1408, 2048, "d_act", n_outer=True)
    dw_ffn_out = rows4(_mm(act, dh2b, "tn", F32, 512, 2048, 2048, "dw_ffn_out"))
    dgu = _swiglu_bwd(gu, dact)
    dn2 = _mm(dgu, w_ffn_in, "nt", F32, 1024, 2048, 1408, "d_n2")
    dw_ffn_in = _mm(n2, dgu, "tn", F32, 1024, 1408, 2048, "dw_ffn_in", out_shards=4)
    dh1, dh1b, dffn_nw = _resid_norm_bwd(h1, ffn_nw, dn2, dh2)
    dmerged = _mm(dh1b, w_out, "nt", F32, 1024, 2048, 2048, "d_merged")
    dw_out = _mm(merged, dh1b, "tn", F32, 2048, 1024, 1024, "dw_out")
    dza, dzb, dgate = _merge_bwd(proj, za, zb, dmerged)
    doa = _mm(dza, w_a, "nt", F32, 1024, 1024, 512, "d_oa")
    dob = _mm(dzb, w_b, "nt", F32, 1024, 1024, 512, "d_ob")
    dw_a = _mm(oa_g, dza, "tn", F32, 1024, 512, 1024, "dw_branch_a", out_shards=4)
    dw_b = _mm(ob_g, dzb, "tn", F32, 1024, 512, 1024, "dw_branch_b", out_shards=4)
    early = dict(w_ffn_in=dw_ffn_in, w_ffn_out=dw_ffn_out, w_out=rows4(dw_out), w_branch_a=dw_a, w_branch_b=dw_b)
    side = comm.early_grads_side(early)
    hg = _hgrn2_bwd(proj, lb_logits, hg_nw, oa_raw, st_a, pm_a, doa, side=side)
    if side is not None:
        hg, arrived = hg
        comm.early_grads_done(arrived)
    dhq, dhf, dhi, dhg, dlbl, dhg_nw = hg
    dq, dk, dv, dg, dbeta, dz, dgd_nw = _gdn_bwd(q, k, v, gb, bb, proj, gd_nw, ob_raw, st_b, xinv, dob)
    dx3, dab, dconv, dalog, ddtb = _gdn_prep_bwd(proj, pab, conv_w, alog, dtb, dq, dk, dv, dg, dbeta)
    dproj = jnp.concatenate([dhq, dhf, dhi, dhg, dx3, dz, dgate], axis=1)
    dw_main = _mm(xn, dproj, "tn", F32, 1024, 1024, 2816, "dw_in_main")
    dw_ab = _mm(xn, dab, "tn", F32, 2048, 128, 768, "dw_in_ab")
    dw_in = _shard_w_in_grad(dw_main, dw_ab)
    side = comm.last_grad_side(dw_in)
    dxn = _mm(dproj, w_main, "nt", F32, 768, 2048, 2048, "d_xn", side=side)
    if side is not None:
        dxn, arrived = dxn
        comm.last_grad_done(arrived)
    dxn = _mm(dab, w_ab, "nt", F32, 768, 2048, 128, "d_xn_ab", add=dxn)
    dx, dmeta, dmix_w = _rms1_bwd(x, meta, mix_w, dxn, dh1)
    grads = dict(meta_tokens=dmeta, lb_logits=dlbl, mix_norm_w=dmix_w, w_in=dw_in,
                 hg_norm_w=dhg_nw, gd_conv_w=dconv[:CONV_K], gd_a_log=dalog[:, :NH],
                 gd_dt_bias=ddtb[:, :NH], gd_norm_w=dgd_nw, w_branch_a=dw_a, w_branch_b=dw_b,
                 w_out=rows4(dw_out), ffn_norm_w=dffn_nw, w_ffn_in=dw_ffn_in, w_ffn_out=dw_ffn_out,
                 final_norm_w=dfinal.reshape(-1))
    return loss, dx, grads


def _adamw(g, w, m, v, name):
    rows, cols = g.shape
    tr = rows
    for cand in (128, 64, 32, 16, 8):
        if rows % cand == 0 and rows > cand:
            tr = cand
            break

    def body(g_ref, w_ref, m_ref, v_ref, go_ref, d_ref, nm_ref, nv_ref):
        gg = g_ref[...]
        go_ref[...] = gg
        nm = ADAM_B1 * m_ref[...] + (1.0 - ADAM_B1) * gg
        nv = ADAM_B2 * v_ref[...] + (1.0 - ADAM_B2) * (gg * gg)
        m_hat = nm / (1.0 - ADAM_B1 ** ADAM_STEP)
        v_hat = nv / (1.0 - ADAM_B2 ** ADAM_STEP)
        d_ref[...] = -ADAM_LR * (m_hat / (jnp.sqrt(v_hat) + ADAM_EPS) + ADAM_WD * w_ref[...])
        nm_ref[...] = nm
        nv_ref[...] = nv

    bs = pl.BlockSpec((tr, cols), lambda i: (i, 0))
    return _call(body, name=name, grid=(rows // tr,), in_specs=[bs] * 4, out_specs=[bs] * 4,
                 out_shape=[_sds((rows, cols), F32)] * 4, sem=("parallel",))(g, w, m, v)


HBM = pl.BlockSpec(memory_space=pltpu.HBM)
MESH = pl.DeviceIdType.MESH


def _place():
    x, y, c = lax.axis_index("x"), lax.axis_index("y"), lax.axis_index("c")
    return x, y, c, [(1 - x, y), (x, 1 - y), (1 - x, 1 - y)]


def _comm_call(body, name, out_shape, n_in, scratch):
    return pl.pallas_call(body, name=name, out_shape=out_shape, in_specs=[HBM] * n_in,
                          out_specs=jax.tree.map(lambda _: HBM, out_shape), scratch_shapes=scratch)


def _half_rows(rows, c, tile):
    hh = rows // 2
    assert rows % 2 == 0 and hh % tile == 0, (rows, tile)
    return pl.ds(pl.multiple_of(c * hh, tile), hh)


def _gather_copies(w_refs, out_refs, sems):
    send_sems, recv_sems = sems
    x, y, c, chips = _place()
    s_me = 2 * x + y
    sends, recvs = [], []
    for k, (w_ref, out_ref) in enumerate(zip(w_refs, out_refs)):
        half = _half_rows(w_ref.shape[0], c, 16)
        for j, (cx, cy) in enumerate(chips):
            sem = dict(send_sem=send_sems.at[3 * k + j], recv_sem=recv_sems.at[3 * k + j], device_id=(cx, cy, c), device_id_type=MESH)
            sends.append(pltpu.make_async_remote_copy(src_ref=w_ref.at[half], dst_ref=out_ref.at[s_me, half], **sem))
            recvs.append(pltpu.make_async_remote_copy(src_ref=w_ref.at[half], dst_ref=out_ref.at[2 * cx + cy, half], **sem))
    return sends, recvs


def _gather_sems(n):
    return [pltpu.SemaphoreType.DMA((3 * n,)), pltpu.SemaphoreType.DMA((3 * n,))]


def _gather_start(w_refs, out_refs, sems):
    for cp in _gather_copies(w_refs, out_refs, sems)[0]:
        cp.start()


def _gather_wait(w_refs, out_refs, sems):
    sends, recvs = _gather_copies(w_refs, out_refs, sems)
    for cp in recvs:
        cp.wait_recv()
    for cp in sends:
        cp.wait_send()


def _gather_side(shards):
    return _Side(shards, [_sds((4,) + w.shape, w.dtype) for w in shards], _gather_sems(len(shards)), _gather_start, _gather_wait)


def _forward_halves(outs, name):
    n = len(outs)

    def body(*refs):
        out_refs = refs[n:2 * n]
        send_sems, recv_sems = refs[2 * n:]
        x, y, c, chips = _place()
        cps = []
        for k in range(n):
            rows = out_refs[k].shape[1]
            half, other = _half_rows(rows, c, 16), _half_rows(rows, 1 - c, 16)
            for j, (cx, cy) in enumerate(chips):
                sem = dict(send_sem=send_sems.at[3 * k + j], recv_sem=recv_sems.at[3 * k + j], device_id=(x, y, 1 - c), device_id_type=MESH)
                landed = out_refs[k].at[2 * cx + cy, half]
                cps.append(pltpu.make_async_remote_copy(src_ref=landed, dst_ref=landed, **sem))
                cps[-1].start()
        for k in range(n):
            rows = out_refs[k].shape[1]
            half, other = _half_rows(rows, c, 16), _half_rows(rows, 1 - c, 16)
            for j, (cx, cy) in enumerate(chips):
                sem = dict(send_sem=send_sems.at[3 * k + j], recv_sem=recv_sems.at[3 * k + j], device_id=(x, y, 1 - c), device_id_type=MESH)
                pltpu.make_async_remote_copy(src_ref=out_refs[k].at[2 * cx + cy, half], dst_ref=out_refs[k].at[2 * cx + cy, other], **sem).wait_recv()
        for cp in cps:
            cp.wait_send()

    shapes = [_sds(o.shape, o.dtype) for o in outs]
    return pl.pallas_call(body, name=name, out_shape=shapes, in_specs=[HBM] * n, out_specs=[HBM] * n,
                          input_output_aliases={k: k for k in range(n)},
                          scratch_shapes=[pltpu.SemaphoreType.DMA((3 * n,)), pltpu.SemaphoreType.DMA((3 * n,))])(*outs)


def _swap_halves(gs, name):
    n = len(gs)

    def body(*refs):
        g_refs, out_refs = refs[:n], refs[n:2 * n]
        send_sems, recv_sems = refs[2 * n:]
        x, y, c, _ = _place()
        cps = []
        for k in range(n):
            other = _half_rows(g_refs[k].shape[1], 1 - c, 8)
            cps.append(pltpu.make_async_remote_copy(src_ref=g_refs[k].at[:, other, :], dst_ref=out_refs[k], send_sem=send_sems.at[k],
                                                    recv_sem=recv_sems.at[k], device_id=(x, y, 1 - c), device_id_type=MESH))
            cps[-1].start()
        for cp in cps:
            cp.wait()

    return _comm_call(body, name, [_sds((4, g.shape[1] // 2, g.shape[2]), g.dtype) for g in gs], n,
                      [pltpu.SemaphoreType.DMA((n,)), pltpu.SemaphoreType.DMA((n,))])(*gs)


def _row_tile(rows, row_bytes, budget=3 << 20):
    if rows * row_bytes <= budget:
        return rows
    return max(t for t in range(16, rows, 16) if rows % t == 0 and t * row_bytes <= budget)


def _add_half(g, got, c, name):
    _, rows, cols = g.shape
    hh = rows // 2
    tr = _row_tile(hh, cols * 4)
    nb = hh // tr

    def body(c_ref, a_ref, b_ref, o_ref):
        o_ref[...] = (a_ref[...] + b_ref[...]).astype(BF16)

    gs = pltpu.PrefetchScalarGridSpec(
        num_scalar_prefetch=1, grid=(4, nb),
        in_specs=[pl.BlockSpec((1, tr, cols), lambda s, i, c_ref: (s, c_ref[0] * nb + i, 0)),
                  pl.BlockSpec((1, tr, cols), lambda s, i, c_ref: (s, i, 0))],
        out_specs=pl.BlockSpec((1, tr, cols), lambda s, i, c_ref: (s, i, 0)))
    return pl.pallas_call(body, name=name, grid_spec=gs, out_shape=_sds((4, hh, cols), BF16),
                          compiler_params=pltpu.CompilerParams(dimension_semantics=("parallel", "parallel"),
                                                               vmem_limit_bytes=VMEM_LIMIT))(c, g, got)


def _scatter_copies(p_refs, out_refs, sems):
    send_sems, recv_sems = sems
    x, y, c, chips = _place()
    s_me = 2 * x + y
    cps = []
    for k, (p_ref, out_ref) in enumerate(zip(p_refs, out_refs)):
        for j, (cx, cy) in enumerate(chips):
            cps.append(pltpu.make_async_remote_copy(src_ref=p_ref.at[2 * cx + cy], dst_ref=out_ref.at[s_me],
                                                    send_sem=send_sems.at[3 * k + j], recv_sem=recv_sems.at[3 * k + j],
                                                    device_id=(cx, cy, c), device_id_type=MESH))
    return cps


def _scatter_start(p_refs, out_refs, sems):
    for cp in _scatter_copies(p_refs, out_refs, sems):
        cp.start()


def _scatter_wait(p_refs, out_refs, sems):
    for cp in _scatter_copies(p_refs, out_refs, sems):
        cp.wait()


def _scatter_side(ps):
    return _Side(ps, [_sds(p_.shape, p_.dtype) for p_ in ps], _gather_sems(len(ps)), _scatter_start, _scatter_wait)


def _sum_slabs(b, name):
    n, h, wd = b.shape
    tr = _row_tile(h, n * wd * 4, 6 << 20)

    def body(b_ref, o_ref):
        acc = b_ref[0]
        for s in range(1, n):
            acc = acc + b_ref[s]
        o_ref[...] = acc

    return _call(body, name=name, grid=(h // tr,), in_specs=[pl.BlockSpec((n, tr, wd), lambda i: (0, i, 0))],
                 out_specs=pl.BlockSpec((tr, wd), lambda i: (i, 0)), out_shape=_sds((h, wd), F32), sem=("parallel",))(b)


def _sum_chips(arrived, own, name):
    n, h, wd = arrived.shape
    tr = _row_tile(h, n * wd * 2, 6 << 20)
    nb = h // tr
    my_chip = lambda: 2 * lax.axis_index("x") + lax.axis_index("y")

    def body(b_ref, p_ref, o_ref):
        acc = None
        for s in range(n):
            term = jnp.where(my_chip() == s, p_ref[0], b_ref[s]).astype(F32)
            acc = term if acc is None else acc + term
        o_ref[...] = acc

    return _call(body, name=name, grid=(nb,),
                 in_specs=[pl.BlockSpec((n, tr, wd), lambda i: (0, i, 0)), pl.BlockSpec((1, tr, wd), lambda i: (my_chip(), i, 0))],
                 out_specs=pl.BlockSpec((tr, wd), lambda i: (lax.axis_index("c") * nb + i, 0)),
                 out_shape=_sds((2 * h, wd), F32), sem=("parallel",))(arrived, own)


def _share_halves(gs):
    n = len(gs)

    def body(*refs):
        out_refs = refs[n:2 * n]
        send_sems, recv_sems = refs[2 * n:]
        x, y, c, _ = _place()
        cps = []
        for k in range(n):
            half, other = _half_rows(out_refs[k].shape[0], c, 8), _half_rows(out_refs[k].shape[0], 1 - c, 8)
            sem = dict(send_sem=send_sems.at[k], recv_sem=recv_sems.at[k], device_id=(x, y, 1 - c), device_id_type=MESH)
            cps.append((pltpu.make_async_remote_copy(src_ref=out_refs[k].at[half], dst_ref=out_refs[k].at[half], **sem),
                        pltpu.make_async_remote_copy(src_ref=out_refs[k].at[half], dst_ref=out_refs[k].at[other], **sem)))
            cps[-1][0].start()
        for send, recv in cps:
            recv.wait_recv()
            send.wait_send()

    return pl.pallas_call(body, name="share_halves", out_shape=[_sds(g.shape, g.dtype) for g in gs], in_specs=[HBM] * n,
                          out_specs=[HBM] * n, input_output_aliases={k: k for k in range(n)},
                          scratch_shapes=[pltpu.SemaphoreType.DMA((n,)), pltpu.SemaphoreType.DMA((n,))])(*gs)


def _gather_all(v, name):
    def body(v_ref, out_ref, send_sems, recv_sems, local_sem):
        x, y, c = lax.axis_index("x"), lax.axis_index("y"), lax.axis_index("c")
        me = 4 * x + 2 * y + c
        flip = lambda t, d: 1 - t if d else t
        mine = pltpu.make_async_copy(v_ref, out_ref.at[me], local_sem)
        mine.start()
        cps = []
        for k in range(1, 8):
            to = (flip(x, k & 4), flip(y, k & 2), flip(c, k & 1))
            cps.append(pltpu.make_async_remote_copy(src_ref=v_ref, dst_ref=out_ref.at[me], send_sem=send_sems.at[k - 1],
                                                    recv_sem=recv_sems.at[k - 1], device_id=to, device_id_type=MESH))
        for cp in cps:
            cp.start()
        for cp in cps:
            cp.wait()
        mine.wait()

    return _comm_call(body, name, _sds((8,) + v.shape, v.dtype), 1,
                      [pltpu.SemaphoreType.DMA((7,)), pltpu.SemaphoreType.DMA((7,)), pltpu.SemaphoreType.DMA])(v)


BIG = (("w_in", 1), ("w_branch_a", 1), ("w_branch_b", 1), ("w_out", 0), ("w_ffn_in", 1), ("w_ffn_out", 0))
SMALL = ("meta_tokens", "lb_logits", "mix_norm_w", "hg_norm_w", "gd_conv_w", "gd_a_log", "gd_dt_bias", "gd_norm_w",
         "ffn_norm_w", "final_norm_w")


def _pack_lanes(parts):
    rows = []
    for p in parts:
        f = p.reshape(-1).astype(F32)
        n = -(-f.shape[0] // DH) * DH
        rows.append(jnp.pad(f, (0, n - f.shape[0])).reshape(-1, DH))
    buf = jnp.concatenate(rows, axis=0)
    return jnp.pad(buf, ((0, -buf.shape[0] % 8), (0, 0)))


def _unpack_lanes(buf, shapes):
    out, off = [], 0
    for shp in shapes:
        n = math.prod(shp)
        r = -(-n // DH)
        out.append(buf[off:off + r].reshape(-1)[:n].reshape(shp))
        off += r
    return out


def kernel(x, meta_tokens, lb_logits, mix_norm_w, w_in, hg_norm_w, gd_conv_w, gd_a_log, gd_dt_bias, gd_norm_w, w_branch_a, w_branch_b, w_out, ffn_norm_w, w_ffn_in, w_ffn_out, final_norm_w, loss_target, m_meta_tokens, m_lb_logits, m_mix_norm_w, m_w_in, m_hg_norm_w, m_gd_conv_w, m_gd_a_log, m_gd_dt_bias, m_gd_norm_w, m_w_branch_a, m_w_branch_b, m_w_out, m_ffn_norm_w, m_w_ffn_in, m_w_ffn_out, m_final_norm_w, v_meta_tokens, v_lb_logits, v_mix_norm_w, v_w_in, v_hg_norm_w, v_gd_conv_w, v_gd_a_log, v_gd_dt_bias, v_gd_norm_w, v_w_branch_a, v_w_branch_b, v_w_out, v_ffn_norm_w, v_w_ffn_in, v_w_ffn_out, v_final_norm_w):
    args = dict(locals())
    big = [n for n, _ in BIG]
    w = {n: args[n] for n in SMALL + tuple(big)}
    m = {n: args["m_" + n] for n in w}
    v = {n: args["v_" + n] for n in w}
    xi, yi, ci = lax.axis_index("x"), lax.axis_index("y"), lax.axis_index("c")
    shard = 2 * xi + yi
    big_local = {n: w[n][0] for n in big}

    meta_cols, conv_cols = meta_tokens.shape[1], gd_conv_w.shape[-1]
    sm_all = _gather_all(_pack_lanes([meta_tokens, gd_conv_w[0]]), "gather_meta")
    sm_parts = [_unpack_lanes(sm_all[2 * s], [meta_tokens.shape, gd_conv_w[0].shape]) for s in range(4)]
    meta_full = jnp.concatenate([p[0] for p in sm_parts], axis=1)
    conv_full = jnp.concatenate([p[1] for p in sm_parts], axis=1)
    cvec = ci.reshape(1).astype(jnp.int32)
    late = [n for n in big if n != "w_in"]
    rows_full = lambda t: t.reshape(t.shape[0] * t.shape[1], t.shape[2])

    def pair_sums(names, gs):
        return [_add_half(gk, got, cvec, "add_half_" + n) for n, gk, got in zip(names, gs, _swap_halves(gs, "swap_" + names[0]))]

    def with_own(slabs, n):
        return lax.dynamic_update_index_in_dim(slabs, big_local[n].astype(BF16), shard, 0)

    class MeshComm:
        def first_side(self):
            return _gather_side([big_local["w_in"].astype(BF16)])

        def first_weight(self, landed):
            return with_own(_forward_halves(landed, "forward_w_in")[0], "w_in")

        def proj_side(self):
            return _gather_side([big_local[n].astype(BF16) for n in late])

        def late_weights(self, landed):
            wl = {n: with_own(t, n) for n, t in zip(late, _forward_halves(landed, "forward_late"))}
            return (wl["w_branch_a"], wl["w_branch_b"], rows_full(wl["w_out"]), wl["w_ffn_in"], rows_full(wl["w_ffn_out"]))

        def early_grads_side(self, grads):
            self.early = list(grads)
            self.early_parts = pair_sums(self.early, [grads[n] for n in self.early])
            return _scatter_side(self.early_parts)

        def early_grads_done(self, arrived):
            self.early_arrived = arrived

        def last_grad_side(self, dw_in):
            self.last_parts = pair_sums(["w_in"], [dw_in])
            return _scatter_side(self.last_parts)

        def last_grad_done(self, arrived):
            self.last_arrived = arrived

    comm = MeshComm()
    loss, dx, g = _local_step(x[0], loss_target[0], meta_full, lb_logits, mix_norm_w, hg_norm_w, conv_full,
                              gd_a_log, gd_dt_bias, gd_norm_w, ffn_norm_w, final_norm_w, comm)
    loss = lax.psum(loss[0, 0], ("x", "y", "c"))

    parts = dict(zip(comm.early + ["w_in"], comm.early_parts + comm.last_parts))
    arrived = dict(zip(comm.early + ["w_in"], comm.early_arrived + comm.last_arrived))
    g_big = dict(zip(big, _share_halves([_sum_chips(arrived[n], parts[n], "sum_chips_" + n) for n in big])))

    small_shapes = [g[n].shape for n in SMALL]
    g_all = _gather_all(_pack_lanes([g[n] for n in SMALL]), "gather_small")
    g_small = dict(zip(SMALL, _unpack_lanes(_sum_slabs(g_all, "sum_small"), small_shapes)))
    g_small["meta_tokens"] = lax.dynamic_slice_in_dim(g_small["meta_tokens"], shard * meta_cols, meta_cols, axis=1)
    g_small["gd_conv_w"] = lax.dynamic_slice_in_dim(g_small["gd_conv_w"], shard * conv_cols, conv_cols, axis=1)

    grad, delta, new_m, new_v = {}, {}, {}, {}
    for n in big:
        g_, d_, m_, v_ = _adamw(g_big[n], big_local[n], m[n][0], v[n][0], "adamw_" + n)
        grad[n] = g_.reshape(w[n].shape)
        delta[n], new_m[n], new_v[n] = d_.reshape(w[n].shape), m_.reshape(w[n].shape), v_.reshape(w[n].shape)
    local_shapes = [w[n].shape for n in SMALL]
    _, d_, m_, v_ = _adamw(_pack_lanes([g_small[n] for n in SMALL]), _pack_lanes([w[n] for n in SMALL]),
                           _pack_lanes([m[n] for n in SMALL]), _pack_lanes([v[n] for n in SMALL]), "adamw_small")
    for n, gs_, dd, mm, vv in zip(SMALL, [g_small[n] for n in SMALL], _unpack_lanes(d_, local_shapes), _unpack_lanes(m_, local_shapes),
                                  _unpack_lanes(v_, local_shapes)):
        grad[n], delta[n], new_m[n], new_v[n] = gs_.reshape(w[n].shape), dd, mm, vv

    order = ["meta_tokens", "lb_logits", "mix_norm_w", "w_in", "hg_norm_w", "gd_conv_w", "gd_a_log", "gd_dt_bias", "gd_norm_w",
             "w_branch_a", "w_branch_b", "w_out", "ffn_norm_w", "w_ffn_in", "w_ffn_out", "final_norm_w"]
    return (loss, dx[None], *[grad[n] for n in order], *[delta[n] for n in order], *[new_m[n] for n in order],
            *[new_v[n] for n in order])
```

```python
import functools
import math

import jax
import jax.numpy as jnp
from jax import lax
from jax.experimental import pallas as pl
from jax.experimental.pallas import tpu as pltpu

F32, BF16 = jnp.float32, jnp.bfloat16
EPS = 1e-6
D_MODEL = 2048
N_META = 16
FRONT = 256
CH = 64
SUB = 16
SUBH = 16
DH = 128
NH = 8
HW = NH * DH
CONV_K = 4
RT = 256
VMEM_LIMIT = 56 * 1024 * 1024
ADAM_LR, ADAM_B1, ADAM_B2, ADAM_EPS, ADAM_WD, ADAM_STEP = 0.001, 0.9, 0.999, 1e-08, 0.01, 10

NN = (((1,), (0,)), ((), ()))
NT = (((1,), (1,)), ((), ()))
TN = (((0,), (0,)), ((), ()))


def _dot(a, b, dn=NN):
    return lax.dot_general(a.astype(BF16), b.astype(BF16), dn, preferred_element_type=F32)


def _dotx(a, b, dn=NN):
    return lax.dot_general(a, b, dn, precision=lax.Precision.HIGHEST, preferred_element_type=F32)


class _Side:
    def __init__(self, inputs, out_shapes, scratch, start, wait):
        self.inputs, self.out_shapes, self.scratch, self.start, self.wait = inputs, out_shapes, scratch, start, wait


def _call(body, *, name, grid, in_specs, out_specs, out_shape, scratch=(), sem=None, side=None):
    params = pltpu.CompilerParams(dimension_semantics=sem, vmem_limit_bytes=VMEM_LIMIT)
    if side is None:
        return pl.pallas_call(body, name=name, grid=grid, in_specs=in_specs, out_specs=out_specs, out_shape=out_shape,
                              scratch_shapes=list(scratch), compiler_params=params)
    single = not isinstance(out_specs, (list, tuple))
    out_specs, out_shape = ([out_specs], [out_shape]) if single else (list(out_specs), list(out_shape))
    ni, no, ns = len(in_specs), len(out_specs), len(scratch)
    nsi, nso = len(side.inputs), len(side.out_shapes)
    hbm = pl.BlockSpec(memory_space=pltpu.HBM)

    def wrapped(*refs):
        main_in, side_in = refs[:ni], refs[ni:ni + nsi]
        main_out, side_out = refs[ni + nsi:ni + nsi + no], refs[ni + nsi + no:ni + nsi + no + nso]
        main_scr, side_scr = refs[ni + nsi + no + nso:ni + nsi + no + nso + ns], refs[ni + nsi + no + nso + ns:]
        pids = [pl.program_id(d) for d in range(len(grid))]
        first = functools.reduce(lambda a, b: a & b, [p == 0 for p in pids])
        last = functools.reduce(lambda a, b: a & b, [p == g - 1 for p, g in zip(pids, grid)])

        @pl.when(first)
        def _():
            side.start(side_in, side_out, side_scr)

        body(*main_in, *main_out, *main_scr)

        @pl.when(last)
        def _():
            side.wait(side_in, side_out, side_scr)

    call = pl.pallas_call(wrapped, name=name, grid=grid, in_specs=list(in_specs) + [hbm] * nsi,
                          out_specs=out_specs + [hbm] * nso, out_shape=out_shape + list(side.out_shapes),
                          scratch_shapes=list(scratch) + list(side.scratch), compiler_params=params)

    def run(*args):
        outs = call(*args, *side.inputs)
        main = outs[0] if single else list(outs[:no])
        return main, list(outs[no:])

    return run


def _divmod(j, per):
    if per == 1:
        return j, 0
    return lax.div(j, jnp.int32(per)), lax.rem(j, jnp.int32(per))


def _sds(shape, dtype):
    return jax.ShapeDtypeStruct(tuple(shape), dtype)


def _sigmoid(x):
    return 0.5 * jnp.tanh(0.5 * x) + 0.5


def _silu(x):
    return x * _sigmoid(x)


def _dsilu(x):
    s = _sigmoid(x)
    return s * (1.0 + x * (1.0 - s))


def _tri(n, kind):
    r = lax.broadcasted_iota(jnp.int32, (n, n), 0)
    c = lax.broadcasted_iota(jnp.int32, (n, n), 1)
    return {"incl": r >= c, "strict": r > c, "upper": c >= r}[kind]


def _mm(a, b, mode, out_dtype, tm, tn, tk, name, add=None, n_outer=False, out_shards=0, side=None):
    sharded_a = a.ndim == 3
    if sharded_a:
        n_a = a.shape[2]
        a_shape = (a.shape[1], a.shape[0] * n_a)
    else:
        a_shape = a.shape
    sharded_b = b.ndim == 3
    if sharded_b:
        S, R, n = b.shape
        b_rows, b_cols = R, S * n
    else:
        b_rows, b_cols = b.shape
    if mode == "nn":
        (M, K), N, dn = a_shape, b_cols, NN
    elif mode == "nt":
        (M, K), N, dn = a_shape, b_rows, NT
    else:
        (K, M), N, dn = a_shape, b_cols, TN
    tm, tn, tk = min(tm, M), min(tn, N), min(tk, K)
    if sharded_a:
        tk = min(tk, n_a)
    if sharded_b:
        tn, tk = (min(tn, n), tk) if mode != "nt" else (tn, min(tk, n))
    if out_shards:
        tn = min(tn, N // out_shards)
    assert M % tm == 0 and N % tn == 0 and K % tk == 0, (name, M, N, K, tm, tn, tk)
    nk = K // tk
    a_blk, a_idx = ((tm, tk), lambda i, j, k: (i, k)) if mode != "tn" else ((tk, tm), lambda i, j, k: (k, i))
    if sharded_a:
        per_a = n_a // tk
        assert mode != "tn" and n_a % tk == 0
        a_blk, a_idx = (None, tm, tk), lambda i, j, k: (_divmod(k, per_a)[0], i, _divmod(k, per_a)[1])
    if not sharded_b:
        b_blk, b_idx = ((tk, tn), lambda i, j, k: (k, j)) if mode != "nt" else ((tn, tk), lambda i, j, k: (j, k))
    elif mode != "nt":
        per = n // tn
        assert n % tn == 0
        b_blk, b_idx = (None, tk, tn), lambda i, j, k: (_divmod(j, per)[0], k, _divmod(j, per)[1])
    else:
        per = n // tk
        assert n % tk == 0
        b_blk, b_idx = (None, tn, tk), lambda i, j, k: (_divmod(k, per)[0], j, _divmod(k, per)[1])
    if out_shards:
        per_o = N // out_shards // tn
        assert (N // out_shards) % tn == 0
        o_blk, o_idx = (None, tm, tn), lambda i, j, k: (_divmod(j, per_o)[0], i, _divmod(j, per_o)[1])
        o_shape = (out_shards, M, N // out_shards)
    else:
        o_blk, o_idx, o_shape = (tm, tn), (lambda i, j, k: (i, j)), (M, N)
    c_idx = lambda i, j, k: (i, j)
    if n_outer:
        sw = lambda f: (lambda j, i, k: f(i, j, k))
        a_idx, b_idx, o_idx, c_idx = sw(a_idx), sw(b_idx), sw(o_idx), sw(c_idx)
        grid = (N // tn, M // tm, nk)
    else:
        grid = (M // tm, N // tn, nk)
    has_add = add is not None

    def body(*refs):
        if has_add:
            a_ref, b_ref, c_ref, o_ref, acc_ref = refs
        else:
            a_ref, b_ref, o_ref, acc_ref = refs
            c_ref = None
        part = lax.dot_general(a_ref[...].astype(BF16), b_ref[...].astype(BF16), dn, preferred_element_type=F32)

        def fin(val):
            if has_add:
                val = val + c_ref[...]
            o_ref[...] = val.astype(out_dtype)

        if nk == 1:
            fin(part)
        else:
            k = pl.program_id(2)

            @pl.when(k == 0)
            def _():
                acc_ref[...] = part

            @pl.when(k > 0)
            def _():
                acc_ref[...] += part

            @pl.when(k == nk - 1)
            def _():
                fin(acc_ref[...])

    in_specs = [pl.BlockSpec(a_blk, a_idx), pl.BlockSpec(b_blk, b_idx)]
    args = [a, b]
    if has_add:
        in_specs.append(pl.BlockSpec((tm, tn), c_idx))
        args.append(add)
    acc_shape = (tm, tn) if nk > 1 else (8, 128)
    return _call(body, name=name, grid=grid, in_specs=in_specs, out_specs=pl.BlockSpec(o_blk, o_idx),
                 out_shape=_sds(o_shape, out_dtype), scratch=[pltpu.VMEM(acc_shape, F32)],
                 sem=("arbitrary",) * 3 if side is not None else ("parallel", "parallel", "arbitrary"), side=side)(*args)


def _rms1_fwd(x, meta, w, side=None):
    seq, d = x.shape
    nt = (FRONT + seq) // RT

    def body(x_ref, m_ref, w_ref, o_ref):
        i = pl.program_id(0)

        def norm(v):
            r = lax.rsqrt(jnp.mean(v * v, axis=-1, keepdims=True) + EPS)
            return (v * r * w_ref[...]).astype(BF16)

        @pl.when(i == 0)
        def _():
            o_ref[0:RT - N_META, :] = jnp.zeros((RT - N_META, d), BF16)
            o_ref[RT - N_META:RT, :] = norm(m_ref[...])

        @pl.when(i > 0)
        def _():
            o_ref[...] = norm(x_ref[...])

    return _call(body, name="rms1_fwd", grid=(nt,),
                 in_specs=[pl.BlockSpec((RT, d), lambda i: (jnp.maximum(i - 1, 0), 0)),
                           pl.BlockSpec((N_META, d), lambda i: (0, 0)),
                           pl.BlockSpec((1, d), lambda i: (0, 0))],
                 out_specs=pl.BlockSpec((RT, d), lambda i: (i, 0)),
                 out_shape=_sds((FRONT + seq, d), BF16), sem=("arbitrary",) if side is not None else ("parallel",),
                 side=side)(x, meta, w)


def _rms1_bwd(x, meta, w, dxn, dh1):
    seq, d = x.shape
    nt = (FRONT + seq) // RT

    def body(x_ref, m_ref, w_ref, g_ref, r_ref, dx_ref, dm_ref, dw_ref):
        i = pl.program_id(0)

        def bwd(v, g):
            r = lax.rsqrt(jnp.mean(v * v, axis=-1, keepdims=True) + EPS)
            vh = v * r
            gh = g * w_ref[...]
            return r * (gh - vh * jnp.mean(gh * vh, axis=-1, keepdims=True)), jnp.sum(g * vh, axis=0, keepdims=True)

        @pl.when(i == 0)
        def _():
            dm, dw = bwd(m_ref[...], g_ref[RT - N_META:RT, :])
            dm_ref[...] = dm
            dw_ref[...] = dw

        @pl.when(i > 0)
        def _():
            dx, dw = bwd(x_ref[...], g_ref[...])
            dx_ref[...] = dx + r_ref[...]
            dw_ref[...] += dw

    xs = pl.BlockSpec((RT, d), lambda i: (jnp.maximum(i - 1, 0), 0))
    return _call(body, name="rms1_bwd", grid=(nt,),
                 in_specs=[xs, pl.BlockSpec((N_META, d), lambda i: (0, 0)), pl.BlockSpec((1, d), lambda i: (0, 0)),
                           pl.BlockSpec((RT, d), lambda i: (i, 0)), xs],
                 out_specs=[xs, pl.BlockSpec((N_META, d), lambda i: (0, 0)), pl.BlockSpec((1, d), lambda i: (0, 0))],
                 out_shape=[_sds((seq, d), F32), _sds((N_META, d), F32), _sds((1, d), F32)],
                 sem=("arbitrary",))(x, meta, w, dxn, dh1)


def _merge_fwd(proj, za, zb):
    seq, d = za.shape
    off = FRONT // RT
    ca, cb = 8 * HW // d, 8 * HW // d + 1

    def body(ga_ref, gb_ref, za_ref, zb_ref, o_ref):
        o_ref[...] = (_sigmoid(ga_ref[...]) * za_ref[...] + _sigmoid(gb_ref[...]) * zb_ref[...]).astype(BF16)

    zs = pl.BlockSpec((RT, d), lambda i: (i, 0))
    return _call(body, name="merge_fwd", grid=(seq // RT,),
                 in_specs=[pl.BlockSpec((RT, d), lambda i: (i + off, ca)), pl.BlockSpec((RT, d), lambda i: (i + off, cb)), zs, zs],
                 out_specs=zs, out_shape=_sds((seq, d), BF16), sem=("parallel",))(proj, proj, za, zb)


def _merge_bwd(proj, za, zb, dmerged):
    seq, d = za.shape
    off = FRONT // RT
    ca, cb = 8 * HW // d, 8 * HW // d + 1
    nt = (FRONT + seq) // RT

    def body(ga_ref, gb_ref, za_ref, zb_ref, dm_ref, dza_ref, dzb_ref, dg_ref):
        i = pl.program_id(0)

        @pl.when(i < off)
        def _():
            dg_ref[...] = jnp.zeros((RT, 2 * d), BF16)

        @pl.when(i >= off)
        def _():
            sa, sb, dm = _sigmoid(ga_ref[...]), _sigmoid(gb_ref[...]), dm_ref[...]
            dza_ref[...] = (dm * sa).astype(BF16)
            dzb_ref[...] = (dm * sb).astype(BF16)
            dg_ref[:, 0:d] = (dm * za_ref[...] * sa * (1.0 - sa)).astype(BF16)
            dg_ref[:, d:2 * d] = (dm * zb_ref[...] * sb * (1.0 - sb)).astype(BF16)

    rs = pl.BlockSpec((RT, d), lambda i: (jnp.maximum(i - off, 0), 0))
    return _call(body, name="merge_bwd", grid=(nt,),
                 in_specs=[pl.BlockSpec((RT, d), lambda i: (i, ca)), pl.BlockSpec((RT, d), lambda i: (i, cb)), rs, rs, rs],
                 out_specs=[rs, rs, pl.BlockSpec((RT, 2 * d), lambda i: (i, 0))],
                 out_shape=[_sds((seq, d), BF16), _sds((seq, d), BF16), _sds((FRONT + seq, 2 * d), BF16)],
                 sem=("arbitrary",))(proj, proj, za, zb, dmerged)


def _resid_norm_fwd(x, mix, w):
    seq, d = x.shape

    def body(x_ref, m_ref, w_ref, h_ref, n_ref):
        h = x_ref[...] + m_ref[...]
        h_ref[...] = h
        r = lax.rsqrt(jnp.mean(h * h, axis=-1, keepdims=True) + EPS)
        n_ref[...] = (h * r * w_ref[...]).astype(BF16)

    rs = pl.BlockSpec((RT, d), lambda i: (i, 0))
    return _call(body, name="resid_norm_fwd", grid=(seq // RT,),
                 in_specs=[rs, rs, pl.BlockSpec((1, d), lambda i: (0, 0))], out_specs=[rs, rs],
                 out_shape=[_sds((seq, d), F32), _sds((seq, d), BF16)], sem=("parallel",))(x, mix, w)


def _resid_norm_bwd(h1, w, dn2, dh2):
    seq, d = h1.shape

    def body(h_ref, w_ref, g_ref, r_ref, o_ref, ob_ref, dw_ref):
        i = pl.program_id(0)
        h, g = h_ref[...], g_ref[...]
        r = lax.rsqrt(jnp.mean(h * h, axis=-1, keepdims=True) + EPS)
        hh = h * r
        gh = g * w_ref[...]
        dh = r_ref[...] + r * (gh - hh * jnp.mean(gh * hh, axis=-1, keepdims=True))
        o_ref[...] = dh
        ob_ref[...] = dh.astype(BF16)
        dw = jnp.sum(g * hh, axis=0, keepdims=True)

        @pl.when(i == 0)
        def _():
            dw_ref[...] = dw

        @pl.when(i > 0)
        def _():
            dw_ref[...] += dw

    rs = pl.BlockSpec((RT, d), lambda i: (i, 0))
    ws = pl.BlockSpec((1, d), lambda i: (0, 0))
    return _call(body, name="resid_norm_bwd", grid=(seq // RT,), in_specs=[rs, ws, rs, rs], out_specs=[rs, rs, ws],
                 out_shape=[_sds((seq, d), F32), _sds((seq, d), BF16), _sds((1, d), F32)], sem=("arbitrary",))(h1, w, dn2, dh2)


def _swiglu_tiles(seq, ff):
    return min(512, seq), (1408 if ff % 1408 == 0 else 512)


def _swiglu_fwd(gu):
    seq, f2 = gu.shape
    ff = f2 // 2
    rt, tc = _swiglu_tiles(seq, ff)
    nb = ff // tc

    def body(g_ref, u_ref, o_ref):
        o_ref[...] = (_silu(g_ref[...].astype(F32)) * u_ref[...].astype(F32)).astype(BF16)

    return _call(body, name="swiglu_fwd", grid=(seq // rt, nb),
                 in_specs=[pl.BlockSpec((rt, tc), lambda i, j: (i, j)), pl.BlockSpec((rt, tc), lambda i, j: (i, j + nb))],
                 out_specs=pl.BlockSpec((rt, tc), lambda i, j: (i, j)), out_shape=_sds((seq, ff), BF16),
                 sem=("parallel", "parallel"))(gu, gu)


def _swiglu_bwd(gu, dact):
    seq, f2 = gu.shape
    ff = f2 // 2
    rt, tc = _swiglu_tiles(seq, ff)
    nb = ff // tc

    def body(g_ref, u_ref, d_ref, o_ref):
        g, d = g_ref[...].astype(F32), d_ref[...].astype(F32)
        sig = _sigmoid(g)
        o_ref[0] = (d * u_ref[...].astype(F32) * (sig * (1.0 + g * (1.0 - sig)))).astype(BF16)
        o_ref[1] = (d * (g * sig)).astype(BF16)

    bs = pl.BlockSpec((rt, tc), lambda i, j: (i, j))
    return _call(body, name="swiglu_bwd", grid=(seq // rt, nb),
                 in_specs=[bs, pl.BlockSpec((rt, tc), lambda i, j: (i, j + nb)), bs],
                 out_specs=pl.BlockSpec((2, rt, tc), lambda i, j: (0, i, j)),
                 out_shape=_sds((2, seq, ff), BF16), sem=("parallel", "parallel"))(gu, gu, dact)


def _loss_head(h1, f, w, tgt):
    seq, d = h1.shape
    nt = seq // RT

    def body(h_ref, f_ref, w_ref, t_ref, l_ref, dh_ref, dhb_ref, dw_ref):
        i = pl.program_id(0)
        h = h_ref[...] + f_ref[...]
        r = lax.rsqrt(jnp.mean(h * h, axis=-1, keepdims=True) + EPS)
        hh = h * r
        err = hh * w_ref[...] - t_ref[...]
        l_ref[...] = jnp.full((8, 128), 0.5 * jnp.sum(jnp.mean(err * err, axis=-1, keepdims=True)), F32)
        dy = err * (1.0 / d)
        gh = dy * w_ref[...]
        dh = r * (gh - hh * jnp.mean(gh * hh, axis=-1, keepdims=True))
        dh_ref[...] = dh
        dhb_ref[...] = dh.astype(BF16)
        dw = jnp.sum(dy * hh, axis=0, keepdims=True)

        @pl.when(i == 0)
        def _():
            dw_ref[...] = dw

        @pl.when(i > 0)
        def _():
            dw_ref[...] += dw

    rs = pl.BlockSpec((RT, d), lambda i: (i, 0))
    ws = pl.BlockSpec((1, d), lambda i: (0, 0))
    return _call(body, name="loss_head", grid=(nt,), in_specs=[rs, rs, ws, rs],
                 out_specs=[pl.BlockSpec((8, 128), lambda i: (i, 0)), rs, rs, ws],
                 out_shape=[_sds((nt * 8, 128), F32), _sds((seq, d), F32), _sds((seq, d), BF16), _sds((1, d), F32)],
                 sem=("arbitrary",))(h1, f, w, tgt)


def _sum_tiles(lt):
    n = lt.shape[0]

    def body(l_ref, o_ref):
        v = l_ref[...]
        r = lax.broadcasted_iota(jnp.int32, v.shape, 0)
        c = lax.broadcasted_iota(jnp.int32, v.shape, 1)
        o_ref[...] = jnp.sum(jnp.where((r % 8 == 0) & (c == 0), v, 0.0), keepdims=True)

    return _call(body, name="loss_sum", grid=(1,), in_specs=[pl.BlockSpec((n, 128), lambda i: (0, 0))],
                 out_specs=pl.BlockSpec((1, 1), lambda i: (0, 0)), out_shape=_sds((1, 1), F32))(lt)


def _gated_norm_fwd(o, g, nw):
    r = lax.rsqrt(jnp.mean(o * o, axis=-1, keepdims=True) + EPS)
    return o * r * nw * _silu(g)


def _gated_norm_bwd(o, g, nw, dout):
    r = lax.rsqrt(jnp.mean(o * o, axis=-1, keepdims=True) + EPS)
    oh = o * r
    sig = _sigmoid(g)
    don = dout * (g * sig)
    dg = dout * (oh * nw) * (sig * (1.0 + g * (1.0 - sig)))
    dnw = jnp.sum(don * oh, axis=0, keepdims=True)
    doh = don * nw
    return r * (doh - oh * jnp.mean(doh * oh, axis=-1, keepdims=True)), dg, dnw


def _hg_gates(fs, lbl):
    l0, l1 = lbl[0:1, :], lbl[1:2, :]
    m = jnp.maximum(l0, l1)
    e0, e1 = jnp.exp(l0 - m), jnp.exp(l1 - m)
    lb = e0 / (e0 + e1)
    sig = _sigmoid(fs)
    f = lb + (1.0 - lb) * sig
    return lb, sig, f, jnp.log(f), (1.0 - lb) * (1.0 - sig)


def _cumsum(w):
    row = lax.broadcasted_iota(jnp.int32, w.shape, 0) & (CH - 1)
    s = 1
    while s < CH:
        w = w + jnp.where(row >= s, pltpu.roll(w, s, 0), 0.0)
        s *= 2
    return w


def _rcumsum(w):
    row = lax.broadcasted_iota(jnp.int32, w.shape, 0) & (CH - 1)
    s = 1
    while s < CH:
        w = w + jnp.where(row < CH - s, pltpu.roll(w, w.shape[0] - s, 0), 0.0)
        s *= 2
    return w


def _decay_blocks(q, k, b, p_ref):
    p_ref[...] = jnp.zeros((CH, CH), F32)
    m16 = _tri(SUBH, "incl")
    for I in range(CH // SUBH):
        s0 = I * SUBH
        bI, qI, kI = b[s0:s0 + SUBH], q[s0:s0 + SUBH], k[s0:s0 + SUBH]
        dec = jnp.exp(jnp.minimum(bI[:, None, :] - bI[None, :, :], 0.0))
        pii = jnp.sum(qI[:, None, :] * kI[None, :, :] * dec, axis=-1)
        p_ref[s0:s0 + SUBH, s0:s0 + SUBH] = jnp.where(m16, pii, 0.0)
        if I > 0:
            rI = b[s0 - 1:s0]
            qs = qI * jnp.exp(bI - rI)
            ks = k[0:s0] * jnp.exp(rI - b[0:s0])
            p_ref[s0:s0 + SUBH, 0:s0] = _dot(qs, ks, NT)


HPS = 8
HPS_HGRN2_FWD = 4
HB = 4
NHB = NH // HPS
OFF = FRONT // RT


def _head_specs(hps, rev=None):
    row = (lambda i: i) if rev is None else rev
    col = lambda g: pl.BlockSpec((RT, hps * DH), lambda h, i: (row(i), g * (NH // hps) + h))
    full = pl.BlockSpec((RT, hps * DH), lambda h, i: (row(i), h))
    real = pl.BlockSpec((RT, hps * DH), lambda h, i: (jnp.maximum(row(i) - OFF, 0), h))
    state = lambda cpt: pl.BlockSpec((hps, cpt, DH, DH), lambda h, i: (h, row(i), 0, 0))
    scal = pl.BlockSpec((hps, RT, DH), lambda h, i: (h, row(i), 0))
    return col, full, real, state, scal


def _hgrn2_fwd(proj, lb_logits, nw):
    tp = proj.shape[0]
    nt, cpt = tp // RT, RT // CH
    hps = HPS_HGRN2_FWD

    def body(q_ref, f_ref, i_ref, g_ref, lbl_ref, nw_ref, og_ref, or_ref, st_ref, pm_ref, s_ref, p_ref):
        @pl.when(pl.program_id(1) == 0)
        def _():
            s_ref[...] = jnp.zeros((hps, DH, DH), F32)

        def chunk(c, carry):
            rows = pl.ds(pl.multiple_of(c * CH, CH), CH)
            for hh in range(hps):
                cols = slice(hh * DH, (hh + 1) * DH)
                _, _, _, w, k = _hg_gates(f_ref[rows, cols], lbl_ref[:, cols])
                q, v = _silu(q_ref[rows, cols]), i_ref[rows, cols]
                b = _cumsum(w)
                st = s_ref[hh]
                st_ref[hh, c] = st
                _decay_blocks(q, k, b, p_ref.at[hh])
                pm_ref[hh, c] = p_ref[hh]
                o = _dot(q * jnp.exp(b), st, NT) + _dot(p_ref[hh], v)
                bl = b[CH - 1:CH]
                s_ref[hh] = st * jnp.exp(bl) + _dot(v, k * jnp.exp(bl - b), TN)
                or_ref[rows, cols] = o
                og_ref[rows, cols] = _gated_norm_fwd(o, g_ref[rows, cols], nw_ref[...]).astype(BF16)
            return carry

        lax.fori_loop(0, cpt, chunk, 0)

    col, full, real, state, _ = _head_specs(hps)
    return _call(body, name="hgrn2_fwd", grid=(NH // hps, nt),
                 in_specs=[col(0), col(1), col(2), col(3), pl.BlockSpec((2, hps * DH), lambda h, i: (0, h)),
                           pl.BlockSpec((1, DH), lambda h, i: (0, 0))],
                 out_specs=[real, full, state(cpt), pl.BlockSpec((hps, cpt, CH, CH), lambda h, i: (h, i, 0, 0))],
                 out_shape=[_sds((tp - FRONT, HW), BF16), _sds((tp, HW), F32), _sds((NH, tp // CH, DH, DH), F32),
                            _sds((NH, tp // CH, CH, CH), F32)],
                 scratch=[pltpu.VMEM((hps, DH, DH), F32), pltpu.VMEM((hps, CH, CH), F32)],
                 sem=("parallel", "arbitrary"))(proj, proj, proj, proj, lb_logits, nw)


def _hgrn2_bwd(proj, lb_logits, nw, o_raw, states, pmat, dog, side=None):
    tp = proj.shape[0]
    nt, cpt = tp // RT, RT // CH

    def body(q_ref, f_ref, i_ref, g_ref, lbl_ref, nw_ref, or_ref, st_ref, pm_ref, dog_ref,
             dq_ref, df_ref, di_ref, dg_ref, dl_ref, dnw_ref, ds_ref, dk_ref, dqa_ref, do_ref):
        step = pl.program_id(1)

        @pl.when(step == 0)
        def _():
            ds_ref[...] = jnp.zeros((HPS, DH, DH), F32)
            dl_ref[...] = jnp.zeros((2, HPS * DH), F32)

        @pl.when((step == 0) & (pl.program_id(0) == 0))
        def _():
            dnw_ref[...] = jnp.zeros((1, DH), F32)

        front = nt - 1 - step < OFF
        for hh in range(HPS):
            cols = slice(hh * DH, (hh + 1) * DH)
            dog_t = jnp.where(front, 0.0, dog_ref[:, cols])
            do_t, dg_t, dnw = _gated_norm_bwd(or_ref[:, cols], g_ref[:, cols], nw_ref[...], dog_t)
            do_ref[:, cols] = do_t
            dg_ref[:, cols] = dg_t.astype(BF16)
            dnw_ref[...] += dnw
        tril = _tri(CH, "incl")
        m16 = _tri(SUBH, "incl")

        def chunk(cc, carry):
            c = cpt - 1 - cc
            rows = pl.ds(pl.multiple_of(c * CH, CH), CH)
            for hh in range(HPS):
                cols = slice(hh * DH, (hh + 1) * DH)
                fs = f_ref[rows, cols]
                lb, sig, f, w, k = _hg_gates(fs, lbl_ref[:, cols])
                hq = q_ref[rows, cols]
                q, v, do = _silu(hq), i_ref[rows, cols], do_ref[rows, cols]
                b = _cumsum(w)
                bl = b[CH - 1:CH]
                eb = jnp.exp(b)
                qs, kd = q * eb, k * jnp.exp(bl - b)
                st, dst = st_ref[hh, c], ds_ref[hh]
                dv = _dot(pm_ref[hh, c], do, TN) + _dot(kd, dst, NT)
                dp = jnp.where(tril, _dot(do, v, NT), 0.0)
                dqa, dka = dqa_ref.at[hh], dk_ref.at[hh]
                dqa[...] = eb * _dotx(do, st)
                dka[...] = jnp.exp(bl - b) * _dotx(v, dst)
                for I in range(CH // SUBH):
                    s0 = I * SUBH
                    bI, qI, kI = b[s0:s0 + SUBH], q[s0:s0 + SUBH], k[s0:s0 + SUBH]
                    dec = jnp.exp(jnp.minimum(bI[:, None, :] - bI[None, :, :], 0.0))
                    dpii = jnp.where(m16, dp[s0:s0 + SUBH, s0:s0 + SUBH], 0.0)[:, :, None] * dec
                    dqa[s0:s0 + SUBH, :] += jnp.sum(dpii * kI[None, :, :], axis=1)
                    dka[s0:s0 + SUBH, :] += jnp.sum(dpii * qI[:, None, :], axis=0)
                    if I > 0:
                        rI = b[s0 - 1:s0]
                        eq, ek = jnp.exp(bI - rI), jnp.exp(rI - b[0:s0])
                        dpij = dp[s0:s0 + SUBH, 0:s0]
                        dqa[s0:s0 + SUBH, :] += eq * _dotx(dpij, k[0:s0] * ek)
                        dka[0:s0, :] += ek * _dotx(dpij, qI * eq, TN)
                dq, dk = dqa[...], dka[...]
                st_end = st * jnp.exp(bl) + _dotx(v, kd, TN)
                dw = _rcumsum(q * dq - k * dk) + jnp.sum(dst * st_end, axis=0, keepdims=True)
                ds_ref[hh] = dst * jnp.exp(bl) + _dotx(do, qs, TN)
                one_m = 1.0 - sig
                dq_ref[rows, cols] = (dq * _dsilu(hq)).astype(BF16)
                df_ref[rows, cols] = ((dw / f - dk) * (1.0 - lb) * sig * one_m).astype(BF16)
                di_ref[rows, cols] = dv.astype(BF16)
                dl_ref[0:1, cols] += jnp.sum((dw / f - dk) * one_m, axis=0, keepdims=True)
            return carry

        lax.fori_loop(0, cpt, chunk, 0)

        @pl.when(step == nt - 1)
        def _():
            lbl = lbl_ref[...]
            l0, l1 = lbl[0:1, :], lbl[1:2, :]
            m = jnp.maximum(l0, l1)
            e0, e1 = jnp.exp(l0 - m), jnp.exp(l1 - m)
            p0 = e0 / (e0 + e1)
            dl0 = dl_ref[0:1, :] * p0 * (1.0 - p0)
            dl_ref[0:1, :] = dl0
            dl_ref[1:2, :] = -dl0

    col, full, real, state, _ = _head_specs(HPS, lambda i: nt - 1 - i)
    lbs = pl.BlockSpec((2, HPS * DH), lambda h, i: (0, h))
    return _call(body, name="hgrn2_bwd", grid=(NHB, nt),
                 in_specs=[col(0), col(1), col(2), col(3), lbs, pl.BlockSpec((1, DH), lambda h, i: (0, 0)), full,
                           state(cpt), pl.BlockSpec((HPS, cpt, CH, CH), lambda h, i: (h, nt - 1 - i, 0, 0)), real],
                 out_specs=[full, full, full, full, lbs, pl.BlockSpec((1, DH), lambda h, i: (0, 0))],
                 out_shape=[_sds((tp, HW), BF16)] * 4 + [_sds((2, HW), F32), _sds((1, DH), F32)],
                 scratch=[pltpu.VMEM((HPS, DH, DH), F32), pltpu.VMEM((HPS, CH, DH), F32),
                          pltpu.VMEM((HPS, CH, DH), F32), pltpu.VMEM((RT, HPS * DH), F32)],
                 sem=("arbitrary", "arbitrary"), side=side)(proj, proj, proj, proj, lb_logits, nw, o_raw, states, pmat, dog)


GQ0 = 4 * HW
CW = 3 * HW


def _gd_scalars(ab, alog, dtb):
    g = -jnp.exp(alog) * jax.nn.softplus(ab + dtb)
    return g, _sigmoid(ab)


def _conv_ext_specs(row_of):
    main = [pl.BlockSpec((RT, HW), lambda i, g=g: (row_of(i), GQ0 // HW + g)) for g in range(3)]
    prev = [pl.BlockSpec((8, HW), lambda i, g=g: (jnp.maximum(row_of(i) * (RT // 8) - 1, 0), GQ0 // HW + g)) for g in range(3)]
    return main + prev


def _conv_fill(ext_ref, xs, xps, first):
    for g in range(3):
        ext_ref[0:8, g * HW:(g + 1) * HW] = jnp.where(first, 0.0, xps[g][...])
        ext_ref[8:8 + RT, g * HW:(g + 1) * HW] = xs[g][...]


def _conv_apply(ext_ref, cw):
    y = cw[CONV_K - 1:CONV_K, :] * ext_ref[pl.ds(8, RT), :]
    for s in range(1, CONV_K):
        y += cw[CONV_K - 1 - s:CONV_K - s, :] * ext_ref[pl.ds(8 - s, RT), :]
    return y


def _gdn_prep_fwd(proj, pab, conv_w, alog, dtb):
    tp = proj.shape[0]
    nt = tp // RT

    def body(x0, x1, x2, p0, p1, p2, ab_ref, cw_ref, al_ref, dt_ref, q_ref, k_ref, v_ref, g_ref, b_ref, ext_ref):
        _conv_fill(ext_ref, (x0, x1, x2), (p0, p1, p2), pl.program_id(0) == 0)
        a = _silu(_conv_apply(ext_ref, cw_ref[...]))
        for h in range(NH):
            for part, ref, sc in ((0, q_ref, DH ** -0.5), (1, k_ref, 1.0)):
                seg = a[:, part * HW + h * DH:part * HW + (h + 1) * DH]
                ref[:, h * DH:(h + 1) * DH] = seg * (lax.rsqrt(jnp.sum(seg * seg, axis=-1, keepdims=True) + EPS) * sc)
        v_ref[...] = a[:, 2 * HW:3 * HW]
        g, beta = _gd_scalars(ab_ref[...], al_ref[...], dt_ref[...])
        for h in range(NH):
            g_ref[h] = jnp.broadcast_to(g[:, h:h + 1], (RT, DH))
            b_ref[h] = jnp.broadcast_to(beta[:, NH + h:NH + h + 1], (RT, DH))

    hs = pl.BlockSpec((RT, HW), lambda i: (i, 0))
    sc = pl.BlockSpec((NH, RT, DH), lambda i: (0, i, 0))
    one = pl.BlockSpec((1, DH), lambda i: (0, 0))
    return _call(body, name="gdn_prep_fwd", grid=(nt,),
                 in_specs=_conv_ext_specs(lambda i: i) + [pl.BlockSpec((RT, DH), lambda i: (i, 0)),
                                                           pl.BlockSpec((CONV_K, CW), lambda i: (0, 0)), one, one],
                 out_specs=[hs, hs, hs, sc, sc],
                 out_shape=[_sds((tp, HW), F32)] * 3 + [_sds((NH, tp, DH), F32)] * 2,
                 scratch=[pltpu.VMEM((RT + 8, CW), F32)], sem=("parallel",))(*([proj] * 6), pab, conv_w, alog, dtb)


def _gdn_prep_bwd(proj, pab, conv_w, alog, dtb, dq, dk, dv, dgb, dbb):
    tp = proj.shape[0]
    nt = tp // RT

    def body(x0, x1, x2, p0, p1, p2, ab_ref, cw_ref, al_ref, dt_ref, dq_ref, dk_ref, dv_ref, dg_ref, db_ref,
             dx_ref, dab_ref, dcw_ref, dal_ref, ddt_ref, ext_ref, dy_ref):
        step = pl.program_id(0)
        i = nt - 1 - step

        @pl.when(step == 0)
        def _():
            dy_ref[RT:RT + 8, :] = jnp.zeros((8, CW), F32)
            dcw_ref[...] = jnp.zeros((8, CW), F32)
            dal_ref[...] = jnp.zeros((1, DH), F32)
            ddt_ref[...] = jnp.zeros((1, DH), F32)

        _conv_fill(ext_ref, (x0, x1, x2), (p0, p1, p2), i == 0)
        cw = cw_ref[...]
        y = _conv_apply(ext_ref, cw)
        sig = _sigmoid(y)
        a = y * sig
        dsl = sig * (1.0 + y * (1.0 - sig))
        for h in range(NH):
            for part, ref, sc in ((0, dq_ref, DH ** -0.5), (1, dk_ref, 1.0)):
                lo = part * HW + h * DH
                seg = a[:, lo:lo + DH]
                r = lax.rsqrt(jnp.sum(seg * seg, axis=-1, keepdims=True) + EPS)
                xh = seg * r
                dxh = ref[:, h * DH:(h + 1) * DH] * sc
                dy_ref[0:RT, lo:lo + DH] = r * (dxh - xh * jnp.sum(dxh * xh, axis=-1, keepdims=True)) * dsl[:, lo:lo + DH]
        dy_ref[0:RT, 2 * HW:3 * HW] = dv_ref[...] * dsl[:, 2 * HW:3 * HW]
        dy = dy_ref[0:RT, :]
        dx = cw[CONV_K - 1:CONV_K, :] * dy
        dcw_ref[CONV_K - 1:CONV_K, :] += jnp.sum(dy * ext_ref[pl.ds(8, RT), :], axis=0, keepdims=True)
        for s in range(1, CONV_K):
            dx += cw[CONV_K - 1 - s:CONV_K - s, :] * dy_ref[pl.ds(s, RT), :]
            dcw_ref[CONV_K - 1 - s:CONV_K - s, :] += jnp.sum(dy * ext_ref[pl.ds(8 - s, RT), :], axis=0, keepdims=True)
        dx_ref[...] = dx.astype(BF16)
        dy_ref[RT:RT + 8, :] = dy[0:8, :]
        ab = ab_ref[...]
        g, beta = _gd_scalars(ab, al_ref[...], dt_ref[...])
        lane = lax.broadcasted_iota(jnp.int32, (RT, DH), 1)
        dgl = jnp.zeros((RT, DH), F32)
        dbl = jnp.zeros((RT, DH), F32)
        for h in range(NH):
            dgl = jnp.where(lane == h, dg_ref[h], dgl)
            dbl = jnp.where(lane == NH + h, db_ref[h], dbl)
        dsp = dgl * (-jnp.exp(al_ref[...])) * _sigmoid(ab + dt_ref[...])
        dab_ref[...] = (dsp + dbl * beta * (1.0 - beta)).astype(BF16)
        ddt_ref[...] += jnp.sum(dsp, axis=0, keepdims=True)
        dal_ref[...] += jnp.sum(dgl * g, axis=0, keepdims=True)

    hs = pl.BlockSpec((RT, HW), lambda s: (nt - 1 - s, 0))
    sc = pl.BlockSpec((NH, RT, DH), lambda s: (0, nt - 1 - s, 0))
    one = pl.BlockSpec((1, DH), lambda s: (0, 0))
    xs = pl.BlockSpec((RT, CW), lambda s: (nt - 1 - s, 0))
    return _call(body, name="gdn_prep_bwd", grid=(nt,),
                 in_specs=_conv_ext_specs(lambda s: nt - 1 - s) + [
                     pl.BlockSpec((RT, DH), lambda s: (nt - 1 - s, 0)), pl.BlockSpec((CONV_K, CW), lambda s: (0, 0)),
                     one, one, hs, hs, hs, sc, sc],
                 out_specs=[xs, pl.BlockSpec((RT, DH), lambda s: (nt - 1 - s, 0)), pl.BlockSpec((8, CW), lambda s: (0, 0)), one, one],
                 out_shape=[_sds((tp, CW), BF16), _sds((tp, DH), BF16), _sds((8, CW), F32), _sds((1, DH), F32), _sds((1, DH), F32)],
                 scratch=[pltpu.VMEM((RT + 8, CW), F32), pltpu.VMEM((RT + 8, CW), F32)],
                 sem=("arbitrary",))(*([proj] * 6), pab, conv_w, alog, dtb, dq, dk, dv, dgb, dbb)


HS = HB * CH


def _stack_heads(ref, rows, base):
    return jnp.concatenate([ref[rows, (base + hh) * DH:(base + hh + 1) * DH] for hh in range(HB)], axis=0)


def _stack_scal(ref, rows, base):
    return jnp.concatenate([ref[base + hh, rows, :] for hh in range(HB)], axis=0)


def _store_heads(ref, rows, base, val):
    for hh in range(HB):
        ref[rows, (base + hh) * DH:(base + hh + 1) * DH] = val[hh * CH:(hh + 1) * CH].astype(ref.dtype)


def _bd_masks():
    r = lax.broadcasted_iota(jnp.int32, (HS, HS), 0)
    c = lax.broadcasted_iota(jnp.int32, (HS, HS), 1)
    same = lax.shift_right_logical(r, int(math.log2(CH))) == lax.shift_right_logical(c, int(math.log2(CH)))
    return same & (r >= c), same & (r > c)


def _unit_lower_inverse(a):
    n = a.shape[0]
    r = lax.broadcasted_iota(jnp.int32, (n, n), 0)
    c = lax.broadcasted_iota(jnp.int32, (n, n), 1)
    blk_of = lambda t, size: lax.shift_right_logical(t, int(math.log2(size)))
    a16 = jnp.where(blk_of(r, SUB) == blk_of(c, SUB), a, 0.0)
    x = (r == c).astype(F32) - a16
    p = a16
    for _ in range(3):
        p = _dot(p, p)
        x = x + _dot(x, p)
    for blk in (2 * SUB, 4 * SUB):
        off = jnp.where((blk_of(r, blk) == blk_of(c, blk)) & (blk_of(r, blk // 2) != blk_of(c, blk // 2)), a, 0.0)
        x = x - _dot(x, _dot(off, x))
    return x


def _gdn_chunk_common(q, k, v, gl, bt, incl, strict, x=None):
    gc = _cumsum(gl)
    e = jnp.exp(gc)
    rel = jnp.exp(jnp.minimum(gc[:, 0:1] - gc.T[0:1, :], 0.0))
    kb = bt * k
    a = jnp.where(strict, bt[:, 0:1] * _dot(k, k, NT) * rel, 0.0)
    if x is None:
        x = _unit_lower_inverse(a)
    wu = _dot(x, jnp.concatenate([kb * e, bt * v], axis=1))
    attn = jnp.where(incl, _dot(q, k, NT) * rel, 0.0)
    return gc, e, rel, kb, a, x, wu[:, 0:DH], wu[:, DH:2 * DH], attn


def _gdn_fwd(q, k, v, gb, bb, proj, nw):
    tp = q.shape[0]
    nt, cpt = tp // RT, RT // CH

    def body(q_ref, k_ref, v_ref, g_ref, b_ref, z_ref, nw_ref, og_ref, or_ref, st_ref, x_ref, s_ref):
        @pl.when(pl.program_id(1) == 0)
        def _():
            s_ref[...] = jnp.zeros((HPS, DH, DH), F32)

        incl, strict = _bd_masks()

        def chunk(c, carry):
            rows = pl.ds(pl.multiple_of(c * CH, CH), CH)
            for base in range(0, HPS, HB):
                qc, kc, vc = _stack_heads(q_ref, rows, base), _stack_heads(k_ref, rows, base), _stack_heads(v_ref, rows, base)
                gc, e, rel, kb, a, x, w, u, attn = _gdn_chunk_common(qc, kc, vc, _stack_scal(g_ref, rows, base),
                                                                    _stack_scal(b_ref, rows, base), incl, strict)
                x_ref[base // HB, c] = x
                qe = qc * e
                ws, qs = [], []
                for hh in range(HB):
                    blk = slice(hh * CH, (hh + 1) * CH)
                    s = s_ref[base + hh]
                    st_ref[base + hh, c] = s
                    both = _dot(jnp.concatenate([w[blk], qe[blk]], axis=0), s)
                    ws.append(both[0:CH])
                    qs.append(both[CH:2 * CH])
                vn = u - jnp.concatenate(ws, axis=0)
                o = jnp.concatenate(qs, axis=0) + _dot(attn, vn)
                for hh in range(HB):
                    blk = slice(hh * CH, (hh + 1) * CH)
                    gl = gc[(hh + 1) * CH - 1:(hh + 1) * CH]
                    s_ref[base + hh] = s_ref[base + hh] * jnp.exp(gl) + _dot(kc[blk] * jnp.exp(gl - gc[blk]), vn[blk], TN)
                _store_heads(or_ref, rows, base, o)
                for hh in range(HB):
                    cols = slice((base + hh) * DH, (base + hh + 1) * DH)
                    og_ref[rows, cols] = _gated_norm_fwd(o[hh * CH:(hh + 1) * CH], z_ref[rows, cols], nw_ref[...]).astype(BF16)
            return carry

        lax.fori_loop(0, cpt, chunk, 0)

    col, full, real, state, scal = _head_specs(HPS)
    return _call(body, name="gdn_fwd", grid=(NHB, nt),
                 in_specs=[full, full, full, scal, scal, col(7), pl.BlockSpec((1, DH), lambda h, i: (0, 0))],
                 out_specs=[real, full, state(cpt), pl.BlockSpec((HPS // HB, cpt, HS, HS), lambda h, i: (h, i, 0, 0))],
                 out_shape=[_sds((tp - FRONT, HW), BF16), _sds((tp, HW), F32), _sds((NH, tp // CH, DH, DH), F32),
                            _sds((NH // HB, tp // CH, HS, HS), F32)],
                 scratch=[pltpu.VMEM((HPS, DH, DH), F32)], sem=("parallel", "arbitrary"))(q, k, v, gb, bb, proj, nw)


def _gdn_bwd(q, k, v, gb, bb, proj, nw, o_raw, states, xinv, dog):
    tp = q.shape[0]
    nt, cpt = tp // RT, RT // CH

    def body(q_ref, k_ref, v_ref, g_ref, b_ref, z_ref, nw_ref, or_ref, st_ref, x_ref, dog_ref,
             dq_ref, dk_ref, dv_ref, dg_ref, db_ref, dz_ref, dnw_ref, ds_ref, do_ref):
        step = pl.program_id(1)

        @pl.when(step == 0)
        def _():
            ds_ref[...] = jnp.zeros((HPS, DH, DH), F32)

        @pl.when((step == 0) & (pl.program_id(0) == 0))
        def _():
            dnw_ref[...] = jnp.zeros((1, DH), F32)

        front = nt - 1 - step < OFF
        for hh in range(HPS):
            cols = slice(hh * DH, (hh + 1) * DH)
            dog_t = jnp.where(front, 0.0, dog_ref[:, cols])
            do_t, dz_t, dnw = _gated_norm_bwd(or_ref[:, cols], z_ref[:, cols], nw_ref[...], dog_t)
            do_ref[:, cols] = do_t
            dz_ref[:, cols] = dz_t.astype(BF16)
            dnw_ref[...] += dnw
        incl, strict = _bd_masks()
        last_row = (lax.broadcasted_iota(jnp.int32, (CH, 1), 0) == CH - 1)

        def rsum(t):
            return jnp.sum(t, axis=-1, keepdims=True)

        def chunk(cc, carry):
            c = cpt - 1 - cc
            rows = pl.ds(pl.multiple_of(c * CH, CH), CH)
            for base in range(0, HPS, HB):
                qc, kc, vc, do = (_stack_heads(q_ref, rows, base), _stack_heads(k_ref, rows, base), _stack_heads(v_ref, rows, base),
                                  _stack_heads(do_ref, rows, base))
                bt = _stack_scal(b_ref, rows, base)
                gc, e, rel, kb, a, x, w, u, attn = _gdn_chunk_common(qc, kc, vc, _stack_scal(g_ref, rows, base), bt, incl, strict,
                                                                    x=x_ref[base // HB, c])
                qe = qc * e
                heads = [slice(hh * CH, (hh + 1) * CH) for hh in range(HB)]
                gls = [gc[(hh + 1) * CH - 1:(hh + 1) * CH] for hh in range(HB)]
                cdec = jnp.concatenate([jnp.exp(gl - gc[blk]) for gl, blk in zip(gls, heads)], axis=0)
                kcd = kc * cdec
                vn = u - jnp.concatenate([_dot(w[blk], st_ref[base + hh, c]) for hh, blk in enumerate(heads)], axis=0)
                dos = jnp.concatenate([_dot(do[blk], st_ref[base + hh, c], NT) for hh, blk in enumerate(heads)], axis=0)
                kds = jnp.concatenate([_dot(kcd[blk], ds_ref[base + hh]) for hh, blk in enumerate(heads)], axis=0)
                vds = jnp.concatenate([_dot(vn[blk], ds_ref[base + hh], NT) for hh, blk in enumerate(heads)], axis=0)
                dvn = _dot(attn, do, TN) + kds
                dattn = jnp.where(incl, _dot(do, vn, NT), 0.0)
                dar = dattn * rel
                dq = _dot(dar, kc) + e * dos
                dk = _dot(dar, qc, TN) + cdec * vds
                dc = cdec[:, 0:1] * rsum(kc * vds)
                dgc = rsum(qe * dos) - dc
                dw = -jnp.concatenate([_dot(dvn[blk], st_ref[base + hh, c], NT) for hh, blk in enumerate(heads)], axis=0)
                extra = []
                for hh, blk in enumerate(heads):
                    s, dsn = st_ref[base + hh, c], ds_ref[base + hh]
                    el = jnp.exp(gls[hh])
                    dglast = jnp.sum(dc[blk], axis=0, keepdims=True) + el[:, 0:1] * jnp.sum(rsum(dsn * s), axis=0, keepdims=True)
                    extra.append(jnp.where(last_row, dglast, 0.0))
                    ds_ref[base + hh] = dsn * el + _dot(jnp.concatenate([qe[blk], -w[blk]], axis=0),
                                                 jnp.concatenate([do[blk], dvn[blk]], axis=0), TN)
                dr = _dot(x, jnp.concatenate([dw, dvn], axis=1), TN)
                drw, dru = dr[:, 0:DH], dr[:, DH:2 * DH]
                da = -jnp.where(strict, _dot(dr, jnp.concatenate([w, u], axis=1), NT), 0.0)
                dar2 = da * rel
                dkb = _dot(dar2, kc)
                rwk = rsum(drw * kc)
                dk = dk + _dot(dar2, kb, TN) + bt * dkb + (bt * e) * drw
                dbeta = rsum(dkb * kc) + e[:, 0:1] * rwk + rsum(dru * vc)
                z = dattn * attn + da * a
                dgc = dgc + bt[:, 0:1] * e[:, 0:1] * rwk + rsum(z) - rsum(z.T) + jnp.concatenate(extra, axis=0)
                _store_heads(dq_ref, rows, base, dq)
                _store_heads(dk_ref, rows, base, dk)
                _store_heads(dv_ref, rows, base, bt * dru)
                dg = _rcumsum(jnp.broadcast_to(dgc, (HS, DH)))
                dbb = jnp.broadcast_to(dbeta, (HS, DH))
                for hh, blk in enumerate(heads):
                    dg_ref[base + hh, rows, :] = dg[blk]
                    db_ref[base + hh, rows, :] = dbb[blk]
            return carry

        lax.fori_loop(0, cpt, chunk, 0)

    col, full, real, state, scal = _head_specs(HPS, lambda i: nt - 1 - i)
    one = pl.BlockSpec((1, DH), lambda h, i: (0, 0))
    return _call(body, name="gdn_bwd", grid=(NHB, nt),
                 in_specs=[full, full, full, scal, scal, col(7), one, full, state(cpt),
                           pl.BlockSpec((HPS // HB, cpt, HS, HS), lambda h, i: (h, nt - 1 - i, 0, 0)), real],
                 out_specs=[full, full, full, scal, scal, full, one],
                 out_shape=[_sds((tp, HW), F32)] * 3 + [_sds((NH, tp, DH), F32)] * 2 + [_sds((tp, HW), BF16), _sds((1, DH), F32)],
                 scratch=[pltpu.VMEM((HPS, DH, DH), F32), pltpu.VMEM((RT, HPS * DH), F32)],
                 sem=("arbitrary", "arbitrary"))(q, k, v, gb, bb, proj, nw, o_raw, states, xinv, dog)


MAIN_W = 8 * HW
AB_W = 2 * NH


def _w_in_pieces(lo, hi):
    total = MAIN_W + AB_W
    out = []
    if lo < MAIN_W:
        out.append(("m", lo, min(hi, MAIN_W)))
    if hi > MAIN_W and lo < total:
        out.append(("ab", max(lo, MAIN_W) - MAIN_W, min(hi, total) - MAIN_W))
    if hi > total:
        out.append(("m", max(lo, total) - AB_W, hi - AB_W))
    return out


W_IN_TILE = 128


def _unshard_w_in(slabs):
    ns, d, cs = slabs.shape
    main_w = ns * cs - AB_W

    def body(w_ref, m_ref, ab_ref):
        ab_ref[...] = jnp.zeros((W_IN_TILE, DH), slabs.dtype)
        for s_ in range(ns):
            off = 0
            for src, a, b in _w_in_pieces(s_ * cs, (s_ + 1) * cs):
                piece = w_ref[s_, :, off:off + (b - a)]
                (m_ref if src == "m" else ab_ref)[:, a:b] = piece
                off += b - a

    return _call(body, name="unshard_w_in", grid=(d // W_IN_TILE,),
                 in_specs=[pl.BlockSpec((ns, W_IN_TILE, cs), lambda i: (0, i, 0))],
                 out_specs=[pl.BlockSpec((W_IN_TILE, main_w), lambda i: (i, 0)), pl.BlockSpec((W_IN_TILE, DH), lambda i: (i, 0))],
                 out_shape=[_sds((d, main_w), slabs.dtype), _sds((d, DH), slabs.dtype)], sem=("parallel",))(slabs)


def _shard_w_in_grad(dw_main, dw_ab, ns=4):
    d, main_w = dw_main.shape
    cs = (main_w + AB_W) // ns

    def body(m_ref, ab_ref, o_ref):
        for s_ in range(ns):
            parts = [(m_ref if src == "m" else ab_ref)[:, a:b] for src, a, b in _w_in_pieces(s_ * cs, (s_ + 1) * cs)]
            o_ref[s_] = parts[0] if len(parts) == 1 else jnp.concatenate(parts, axis=1)

    return _call(body, name="shard_w_in_grad", grid=(d // W_IN_TILE,),
                 in_specs=[pl.BlockSpec((W_IN_TILE, main_w), lambda i: (i, 0)), pl.BlockSpec((W_IN_TILE, DH), lambda i: (i, 0))],
                 out_specs=pl.BlockSpec((ns, W_IN_TILE, cs), lambda i: (0, i, 0)),
                 out_shape=_sds((ns, d, cs), dw_main.dtype), sem=("parallel",))(dw_main, dw_ab)


def _pad_lanes(v):
    return jnp.pad(v, ((0, 0), (0, DH - v.shape[1])))


class _NoComm:
    def __init__(self, w_in, late):
        self.w_in, self.late = w_in, late

    def first_side(self):
        return None

    def first_weight(self, side_outs):
        return self.w_in

    def proj_side(self):
        return None

    def late_weights(self, side_outs):
        return self.late

    def early_grads_side(self, grads):
        return None

    def early_grads_done(self, side_outs):
        pass

    def last_grad_side(self, dw_in):
        return None

    def last_grad_done(self, side_outs):
        pass


def _local_step(x, tgt, meta, lb_logits, mix_w, hg_nw, conv_w, a_log, dt_bias, gd_nw, ffn_nw, final_w, comm):
    alog, dtb = _pad_lanes(a_log), _pad_lanes(dt_bias)
    final_w = final_w.reshape(1, -1)
    rows4 = lambda t: t.reshape(4, t.shape[0] // 4, t.shape[1])
    side = comm.first_side()
    xn = _rms1_fwd(x, meta, mix_w, side=side)
    xn, landed = xn if side is not None else (xn, None)
    w_main, w_ab = _unshard_w_in(comm.first_weight(landed))
    side = comm.proj_side()
    proj = _mm(xn, w_main, "nn", F32, 768, 2048, 2048, "proj_main", n_outer=True, side=side)
    proj, landed = proj if side is not None else (proj, None)
    w_a, w_b, w_out, w_ffn_in, w_ffn_out = comm.late_weights(landed)
    pab = _mm(xn, w_ab, "nn", F32, 768, 128, 2048, "proj_ab")
    oa_g, oa_raw, st_a, pm_a = _hgrn2_fwd(proj, lb_logits, hg_nw)
    q, k, v, gb, bb = _gdn_prep_fwd(proj, pab, conv_w, alog, dtb)
    ob_g, ob_raw, st_b, xinv = _gdn_fwd(q, k, v, gb, bb, proj, gd_nw)
    za = _mm(oa_g, w_a, "nn", F32, 1024, 512, 1024, "branch_a", n_outer=True)
    zb = _mm(ob_g, w_b, "nn", F32, 1024, 512, 1024, "branch_b", n_outer=True)
    merged = _merge_fwd(proj, za, zb)
    mix = _mm(merged, w_out, "nn", F32, 1024, 2048, 2048, "mix_out")
    h1, n2 = _resid_norm_fwd(x, mix, ffn_nw)
    gu = _mm(n2, w_ffn_in, "nn", BF16, 1024, 1408, 2048, "ffn_in", n_outer=True)
    act = _swiglu_fwd(gu)
    f = _mm(act, w_ffn_out, "nn", F32, 1024, 2048, 1408, "ffn_out")
    lt, dh2, dh2b, dfinal = _loss_head(h1, f, final_w, tgt)
    loss = _sum_tiles(lt)
    dact = _mm(dh2b, w_ffn_out, "nt", BF16, 1024, 1408, 2048, "d_act", n_outer=True)
    dw_ffn_out = rows4(_mm(act, dh2b, "tn", F32, 512, 2048, 2048, "dw_ffn_out"))
    dgu = _swiglu_bwd(gu, dact)
    dn2 = _mm(dgu, w_ffn_in, "nt", F32, 1024, 2048, 1408, "d_n2")
    dw_ffn_in = _mm(n2, dgu, "tn", F32, 1024, 1408, 2048, "dw_ffn_in", out_shards=4)
    dh1, dh1b, dffn_nw = _resid_norm_bwd(h1, ffn_nw, dn2, dh2)
    dmerged = _mm(dh1b, w_out, "nt", F32, 1024, 2048, 2048, "d_merged")
    dw_out = _mm(merged, dh1b, "tn", F32, 2048, 1024, 1024, "dw_out")
    dza, dzb, dgate = _merge_bwd(proj, za, zb, dmerged)
    doa = _mm(dza, w_a, "nt", F32, 1024, 1024, 512, "d_oa")
    dob = _mm(dzb, w_b, "nt", F32, 1024, 1024, 512, "d_ob")
    dw_a = _mm(oa_g, dza, "tn", F32, 1024, 512, 1024, "dw_branch_a", out_shards=4)
    dw_b = _mm(ob_g, dzb, "tn", F32, 1024, 512, 1024, "dw_branch_b", out_shards=4)
    early = dict(w_ffn_in=dw_ffn_in, w_ffn_out=dw_ffn_out, w_out=rows4(dw_out), w_branch_a=dw_a, w_branch_b=dw_b)
    side = comm.early_grads_side(early)
    hg = _hgrn2_bwd(proj, lb_logits, hg_nw, oa_raw, st_a, pm_a, doa, side=side)
    if side is not None:
        hg, arrived = hg
        comm.early_grads_done(arrived)
    dhq, dhf, dhi, dhg, dlbl, dhg_nw = hg
    dq, dk, dv, dg, dbeta, dz, dgd_nw = _gdn_bwd(q, k, v, gb, bb, proj, gd_nw, ob_raw, st_b, xinv, dob)
    dx3, dab, dconv, dalog, ddtb = _gdn_prep_bwd(proj, pab, conv_w, alog, dtb, dq, dk, dv, dg, dbeta)
    dproj = jnp.concatenate([dhq, dhf, dhi, dhg, dx3, dz, dgate], axis=1)
    dw_main = _mm(xn, dproj, "tn", F32, 1024, 1024, 2816, "dw_in_main")
    dw_ab = _mm(xn, dab, "tn", F32, 2048, 128, 768, "dw_in_ab")
    dw_in = _shard_w_in_grad(dw_main, dw_ab)
    side = comm.last_grad_side(dw_in)
    dxn = _mm(dproj, w_main, "nt", F32, 768, 2048, 2048, "d_xn", side=side)
    if side is not None:
        dxn, arrived = dxn
        comm.last_grad_done(arrived)
    dxn = _mm(dab, w_ab, "nt", F32, 768, 2048, 128, "d_xn_ab", add=dxn)
    dx, dmeta, dmix_w = _rms1_bwd(x, meta, mix_w, dxn, dh1)
    grads = dict(meta_tokens=dmeta, lb_logits=dlbl, mix_norm_w=dmix_w, w_in=dw_in,
                 hg_norm_w=dhg_nw, gd_conv_w=dconv[:CONV_K], gd_a_log=dalog[:, :NH],
                 gd_dt_bias=ddtb[:, :NH], gd_norm_w=dgd_nw, w_branch_a=dw_a, w_branch_b=dw_b,
                 w_out=rows4(dw_out), ffn_norm_w=dffn_nw, w_ffn_in=dw_ffn_in, w_ffn_out=dw_ffn_out,
                 final_norm_w=dfinal.reshape(-1))
    return loss, dx, grads


def _adamw(g, w, m, v, name):
    rows, cols = g.shape
    tr = rows
    for cand in (128, 64, 32, 16, 8):
        if rows % cand == 0 and rows > cand:
            tr = cand
            break

    def body(g_ref, w_ref, m_ref, v_ref, go_ref, d_ref, nm_ref, nv_ref):
        gg = g_ref[...]
        go_ref[...] = gg
        nm = ADAM_B1 * m_ref[...] + (1.0 - ADAM_B1) * gg
        nv = ADAM_B2 * v_ref[...] + (1.0 - ADAM_B2) * (gg * gg)
        m_hat = nm / (1.0 - ADAM_B1 ** ADAM_STEP)
        v_hat = nv / (1.0 - ADAM_B2 ** ADAM_STEP)
        d_ref[...] = -ADAM_LR * (m_hat / (jnp.sqrt(v_hat) + ADAM_EPS) + ADAM_WD * w_ref[...])
        nm_ref[...] = nm
        nv_ref[...] = nv

    bs = pl.BlockSpec((tr, cols), lambda i: (i, 0))
    return _call(body, name=name, grid=(rows // tr,), in_specs=[bs] * 4, out_specs=[bs] * 4,
                 out_shape=[_sds((rows, cols), F32)] * 4, sem=("parallel",))(g, w, m, v)


HBM = pl.BlockSpec(memory_space=pltpu.HBM)
MESH = pl.DeviceIdType.MESH


def _place():
    x, y, c = lax.axis_index("x"), lax.axis_index("y"), lax.axis_index("c")
    return x, y, c, [(1 - x, y), (x, 1 - y), (1 - x, 1 - y)]


def _comm_call(body, name, out_shape, n_in, scratch):
    return pl.pallas_call(body, name=name, out_shape=out_shape, in_specs=[HBM] * n_in,
                          out_specs=jax.tree.map(lambda _: HBM, out_shape), scratch_shapes=scratch)


def _half_rows(rows, c, tile):
    hh = rows // 2
    assert rows % 2 == 0 and hh % tile == 0, (rows, tile)
    return pl.ds(pl.multiple_of(c * hh, tile), hh)


def _gather_copies(w_refs, out_refs, sems):
    send_sems, recv_sems = sems
    x, y, c, chips = _place()
    s_me = 2 * x + y
    sends, recvs = [], []
    for k, (w_ref, out_ref) in enumerate(zip(w_refs, out_refs)):
        half = _half_rows(w_ref.shape[0], c, 16)
        for j, (cx, cy) in enumerate(chips):
            sem = dict(send_sem=send_sems.at[3 * k + j], recv_sem=recv_sems.at[3 * k + j], device_id=(cx, cy, c), device_id_type=MESH)
            sends.append(pltpu.make_async_remote_copy(src_ref=w_ref.at[half], dst_ref=out_ref.at[s_me, half], **sem))
            recvs.append(pltpu.make_async_remote_copy(src_ref=w_ref.at[half], dst_ref=out_ref.at[2 * cx + cy, half], **sem))
    return sends, recvs


def _gather_sems(n):
    return [pltpu.SemaphoreType.DMA((3 * n,)), pltpu.SemaphoreType.DMA((3 * n,))]


def _gather_start(w_refs, out_refs, sems):
    for cp in _gather_copies(w_refs, out_refs, sems)[0]:
        cp.start()


def _gather_wait(w_refs, out_refs, sems):
    sends, recvs = _gather_copies(w_refs, out_refs, sems)
    for cp in recvs:
        cp.wait_recv()
    for cp in sends:
        cp.wait_send()


def _gather_side(shards):
    return _Side(shards, [_sds((4,) + w.shape, w.dtype) for w in shards], _gather_sems(len(shards)), _gather_start, _gather_wait)


def _forward_halves(outs, name):
    n = len(outs)

    def body(*refs):
        out_refs = refs[n:2 * n]
        send_sems, recv_sems = refs[2 * n:]
        x, y, c, chips = _place()
        cps = []
        for k in range(n):
            rows = out_refs[k].shape[1]
            half, other = _half_rows(rows, c, 16), _half_rows(rows, 1 - c, 16)
            for j, (cx, cy) in enumerate(chips):
                sem = dict(send_sem=send_sems.at[3 * k + j], recv_sem=recv_sems.at[3 * k + j], device_id=(x, y, 1 - c), device_id_type=MESH)
                landed = out_refs[k].at[2 * cx + cy, half]
                cps.append(pltpu.make_async_remote_copy(src_ref=landed, dst_ref=landed, **sem))
                cps[-1].start()
        for k in range(n):
            rows = out_refs[k].shape[1]
            half, other = _half_rows(rows, c, 16), _half_rows(rows, 1 - c, 16)
            for j, (cx, cy) in enumerate(chips):
                sem = dict(send_sem=send_sems.at[3 * k + j], recv_sem=recv_sems.at[3 * k + j], device_id=(x, y, 1 - c), device_id_type=MESH)
                pltpu.make_async_remote_copy(src_ref=out_refs[k].at[2 * cx + cy, half], dst_ref=out_refs[k].at[2 * cx + cy, other], **sem).wait_recv()
        for cp in cps:
            cp.wait_send()

    shapes = [_sds(o.shape, o.dtype) for o in outs]
    return pl.pallas_call(body, name=name, out_shape=shapes, in_specs=[HBM] * n, out_specs=[HBM] * n,
                          input_output_aliases={k: k for k in range(n)},
                          scratch_shapes=[pltpu.SemaphoreType.DMA((3 * n,)), pltpu.SemaphoreType.DMA((3 * n,))])(*outs)


def _swap_halves(gs, name):
    n = len(gs)

    def body(*refs):
        g_refs, out_refs = refs[:n], refs[n:2 * n]
        send_sems, recv_sems = refs[2 * n:]
        x, y, c, _ = _place()
        cps = []
        for k in range(n):
            other = _half_rows(g_refs[k].shape[1], 1 - c, 8)
            cps.append(pltpu.make_async_remote_copy(src_ref=g_refs[k].at[:, other, :], dst_ref=out_refs[k], send_sem=send_sems.at[k],
                                                    recv_sem=recv_sems.at[k], device_id=(x, y, 1 - c), device_id_type=MESH))
            cps[-1].start()
        for cp in cps:
            cp.wait()

    return _comm_call(body, name, [_sds((4, g.shape[1] // 2, g.shape[2]), g.dtype) for g in gs], n,
                      [pltpu.SemaphoreType.DMA((n,)), pltpu.SemaphoreType.DMA((n,))])(*gs)


def _row_tile(rows, row_bytes, budget=3 << 20):
    if rows * row_bytes <= budget:
        return rows
    return max(t for t in range(16, rows, 16) if rows % t == 0 and t * row_bytes <= budget)


def _add_half(g, got, c, name):
    _, rows, cols = g.shape
    hh = rows // 2
    tr = _row_tile(hh, cols * 4)
    nb = hh // tr

    def body(c_ref, a_ref, b_ref, o_ref):
        o_ref[...] = (a_ref[...] + b_ref[...]).astype(BF16)

    gs = pltpu.PrefetchScalarGridSpec(
        num_scalar_prefetch=1, grid=(4, nb),
        in_specs=[pl.BlockSpec((1, tr, cols), lambda s, i, c_ref: (s, c_ref[0] * nb + i, 0)),
                  pl.BlockSpec((1, tr, cols), lambda s, i, c_ref: (s, i, 0))],
        out_specs=pl.BlockSpec((1, tr, cols), lambda s, i, c_ref: (s, i, 0)))
    return pl.pallas_call(body, name=name, grid_spec=gs, out_shape=_sds((4, hh, cols), BF16),
                          compiler_params=pltpu.CompilerParams(dimension_semantics=("parallel", "parallel"),
                                                               vmem_limit_bytes=VMEM_LIMIT))(c, g, got)


def _scatter_copies(p_refs, out_refs, sems):
    send_sems, recv_sems = sems
    x, y, c, chips = _place()
    s_me = 2 * x + y
    cps = []
    for k, (p_ref, out_ref) in enumerate(zip(p_refs, out_refs)):
        for j, (cx, cy) in enumerate(chips):
            cps.append(pltpu.make_async_remote_copy(src_ref=p_ref.at[2 * cx + cy], dst_ref=out_ref.at[s_me],
                                                    send_sem=send_sems.at[3 * k + j], recv_sem=recv_sems.at[3 * k + j],
                                                    device_id=(cx, cy, c), device_id_type=MESH))
    return cps


def _scatter_start(p_refs, out_refs, sems):
    for cp in _scatter_copies(p_refs, out_refs, sems):
        cp.start()


def _scatter_wait(p_refs, out_refs, sems):
    for cp in _scatter_copies(p_refs, out_refs, sems):
        cp.wait()


def _scatter_side(ps):
    return _Side(ps, [_sds(p_.shape, p_.dtype) for p_ in ps], _gather_sems(len(ps)), _scatter_start, _scatter_wait)


def _sum_slabs(b, name):
    n, h, wd = b.shape
    tr = _row_tile(h, n * wd * 4, 6 << 20)

    def body(b_ref, o_ref):
        acc = b_ref[0]
        for s in range(1, n):
            acc = acc + b_ref[s]
        o_ref[...] = acc

    return _call(body, name=name, grid=(h // tr,), in_specs=[pl.BlockSpec((n, tr, wd), lambda i: (0, i, 0))],
                 out_specs=pl.BlockSpec((tr, wd), lambda i: (i, 0)), out_shape=_sds((h, wd), F32), sem=("parallel",))(b)


def _sum_chips(arrived, own, name):
    n, h, wd = arrived.shape
    tr = _row_tile(h, n * wd * 2, 6 << 20)
    nb = h // tr
    my_chip = lambda: 2 * lax.axis_index("x") + lax.axis_index("y")

    def body(b_ref, p_ref, o_ref):
        acc = None
        for s in range(n):
            term = jnp.where(my_chip() == s, p_ref[0], b_ref[s]).astype(F32)
            acc = term if acc is None else acc + term
        o_ref[...] = acc

    return _call(body, name=name, grid=(nb,),
                 in_specs=[pl.BlockSpec((n, tr, wd), lambda i: (0, i, 0)), pl.BlockSpec((1, tr, wd), lambda i: (my_chip(), i, 0))],
                 out_specs=pl.BlockSpec((tr, wd), lambda i: (lax.axis_index("c") * nb + i, 0)),
                 out_shape=_sds((2 * h, wd), F32), sem=("parallel",))(arrived, own)


def _share_halves(gs):
    n = len(gs)

    def body(*refs):
        out_refs = refs[n:2 * n]
        send_sems, recv_sems = refs[2 * n:]
        x, y, c, _ = _place()
        cps = []
        for k in range(n):
            half, other = _half_rows(out_refs[k].shape[0], c, 8), _half_rows(out_refs[k].shape[0], 1 - c, 8)
            sem = dict(send_sem=send_sems.at[k], recv_sem=recv_sems.at[k], device_id=(x, y, 1 - c), device_id_type=MESH)
            cps.append((pltpu.make_async_remote_copy(src_ref=out_refs[k].at[half], dst_ref=out_refs[k].at[half], **sem),
                        pltpu.make_async_remote_copy(src_ref=out_refs[k].at[half], dst_ref=out_refs[k].at[other], **sem)))
            cps[-1][0].start()
        for send, recv in cps:
            recv.wait_recv()
            send.wait_send()

    return pl.pallas_call(body, name="share_halves", out_shape=[_sds(g.shape, g.dtype) for g in gs], in_specs=[HBM] * n,
                          out_specs=[HBM] * n, input_output_aliases={k: k for k in range(n)},
                          scratch_shapes=[pltpu.SemaphoreType.DMA((n,)), pltpu.SemaphoreType.DMA((n,))])(*gs)


def _gather_all(v, name):
    def body(v_ref, out_ref, send_sems, recv_sems, local_sem):
        x, y, c = lax.axis_index("x"), lax.axis_index("y"), lax.axis_index("c")
        me = 4 * x + 2 * y + c
        flip = lambda t, d: 1 - t if d else t
        mine = pltpu.make_async_copy(v_ref, out_ref.at[me], local_sem)
        mine.start()
        cps = []
        for k in range(1, 8):
            to = (flip(x, k & 4), flip(y, k & 2), flip(c, k & 1))
            cps.append(pltpu.make_async_remote_copy(src_ref=v_ref, dst_ref=out_ref.at[me], send_sem=send_sems.at[k - 1],
                                                    recv_sem=recv_sems.at[k - 1], device_id=to, device_id_type=MESH))
        for cp in cps:
            cp.start()
        for cp in cps:
            cp.wait()
        mine.wait()

    return _comm_call(body, name, _sds((8,) + v.shape, v.dtype), 1,
                      [pltpu.SemaphoreType.DMA((7,)), pltpu.SemaphoreType.DMA((7,)), pltpu.SemaphoreType.DMA])(v)


BIG = (("w_in", 1), ("w_branch_a", 1), ("w_branch_b", 1), ("w_out", 0), ("w_ffn_in", 1), ("w_ffn_out", 0))
SMALL = ("meta_tokens", "lb_logits", "mix_norm_w", "hg_norm_w", "gd_conv_w", "gd_a_log", "gd_dt_bias", "gd_norm_w",
         "ffn_norm_w", "final_norm_w")


def _pack_lanes(parts):
    rows = []
    for p in parts:
        f = p.reshape(-1).astype(F32)
        n = -(-f.shape[0] // DH) * DH
        rows.append(jnp.pad(f, (0, n - f.shape[0])).reshape(-1, DH))
    buf = jnp.concatenate(rows, axis=0)
    return jnp.pad(buf, ((0, -buf.shape[0] % 8), (0, 0)))


def _unpack_lanes(buf, shapes):
    out, off = [], 0
    for shp in shapes:
        n = math.prod(shp)
        r = -(-n // DH)
        out.append(buf[off:off + r].reshape(-1)[:n].reshape(shp))
        off += r
    return out


def kernel(x, meta_tokens, lb_logits, mix_norm_w, w_in, hg_norm_w, gd_conv_w, gd_a_log, gd_dt_bias, gd_norm_w, w_branch_a, w_branch_b, w_out, ffn_norm_w, w_ffn_in, w_ffn_out, final_norm_w, loss_target, m_meta_tokens, m_lb_logits, m_mix_norm_w, m_w_in, m_hg_norm_w, m_gd_conv_w, m_gd_a_log, m_gd_dt_bias, m_gd_norm_w, m_w_branch_a, m_w_branch_b, m_w_out, m_ffn_norm_w, m_w_ffn_in, m_w_ffn_out, m_final_norm_w, v_meta_tokens, v_lb_logits, v_mix_norm_w, v_w_in, v_hg_norm_w, v_gd_conv_w, v_gd_a_log, v_gd_dt_bias, v_gd_norm_w, v_w_branch_a, v_w_branch_b, v_w_out, v_ffn_norm_w, v_w_ffn_in, v_w_ffn_out, v_final_norm_w):
    args = dict(locals())
    big = [n for n, _ in BIG]
    w = {n: args[n] for n in SMALL + tuple(big)}
    m = {n: args["m_" + n] for n in w}
    v = {n: args["v_" + n] for n in w}
    xi, yi, ci = lax.axis_index("x"), lax.axis_index("y"), lax.axis_index("c")
    shard = 2 * xi + yi
    big_local = {n: w[n][0] for n in big}

    meta_cols, conv_cols = meta_tokens.shape[1], gd_conv_w.shape[-1]
    sm_all = _gather_all(_pack_lanes([meta_tokens, gd_conv_w[0]]), "gather_meta")
    sm_parts = [_unpack_lanes(sm_all[2 * s], [meta_tokens.shape, gd_conv_w[0].shape]) for s in range(4)]
    meta_full = jnp.concatenate([p[0] for p in sm_parts], axis=1)
    conv_full = jnp.concatenate([p[1] for p in sm_parts], axis=1)
    cvec = ci.reshape(1).astype(jnp.int32)
    late = [n for n in big if n != "w_in"]
    rows_full = lambda t: t.reshape(t.shape[0] * t.shape[1], t.shape[2])

    def pair_sums(names, gs):
        return [_add_half(gk, got, cvec, "add_half_" + n) for n, gk, got in zip(names, gs, _swap_halves(gs, "swap_" + names[0]))]

    def with_own(slabs, n):
        return lax.dynamic_update_index_in_dim(slabs, big_local[n].astype(BF16), shard, 0)

    class MeshComm:
        def first_side(self):
            return _gather_side([big_local["w_in"].astype(BF16)])

        def first_weight(self, landed):
            return with_own(_forward_halves(landed, "forward_w_in")[0], "w_in")

        def proj_side(self):
            return _gather_side([big_local[n].astype(BF16) for n in late])

        def late_weights(self, landed):
            wl = {n: with_own(t, n) for n, t in zip(late, _forward_halves(landed, "forward_late"))}
            return (wl["w_branch_a"], wl["w_branch_b"], rows_full(wl["w_out"]), wl["w_ffn_in"], rows_full(wl["w_ffn_out"]))

        def early_grads_side(self, grads):
            self.early = list(grads)
            self.early_parts = pair_sums(self.early, [grads[n] for n in self.early])
            return _scatter_side(self.early_parts)

        def early_grads_done(self, arrived):
            self.early_arrived = arrived

        def last_grad_side(self, dw_in):
            self.last_parts = pair_sums(["w_in"], [dw_in])
            return _scatter_side(self.last_parts)

        def last_grad_done(self, arrived):
            self.last_arrived = arrived

    comm = MeshComm()
    loss, dx, g = _local_step(x[0], loss_target[0], meta_full, lb_logits, mix_norm_w, hg_norm_w, conv_full,
                              gd_a_log, gd_dt_bias, gd_norm_w, ffn_norm_w, final_norm_w, comm)
    loss = lax.psum(loss[0, 0], ("x", "y", "c"))

    parts = dict(zip(comm.early + ["w_in"], comm.early_parts + comm.last_parts))
    arrived = dict(zip(comm.early + ["w_in"], comm.early_arrived + comm.last_arrived))
    g_big = dict(zip(big, _share_halves([_sum_chips(arrived[n], parts[n], "sum_chips_" + n) for n in big])))

    small_shapes = [g[n].shape for n in SMALL]
    g_all = _gather_all(_pack_lanes([g[n] for n in SMALL]), "gather_small")
    g_small = dict(zip(SMALL, _unpack_lanes(_sum_slabs(g_all, "sum_small"), small_shapes)))
    g_small["meta_tokens"] = lax.dynamic_slice_in_dim(g_small["meta_tokens"], shard * meta_cols, meta_cols, axis=1)
    g_small["gd_conv_w"] = lax.dynamic_slice_in_dim(g_small["gd_conv_w"], shard * conv_cols, conv_cols, axis=1)

    grad, delta, new_m, new_v = {}, {}, {}, {}
    for n in big:
        g_, d_, m_, v_ = _adamw(g_big[n], big_local[n], m[n][0], v[n][0], "adamw_" + n)
        grad[n] = g_.reshape(w[n].shape)
        delta[n], new_m[n], new_v[n] = d_.reshape(w[n].shape), m_.reshape(w[n].shape), v_.reshape(w[n].shape)
    local_shapes = [w[n].shape for n in SMALL]
    _, d_, m_, v_ = _adamw(_pack_lanes([g_small[n] for n in SMALL]), _pack_lanes([w[n] for n in SMALL]),
                           _pack_lanes([m[n] for n in SMALL]), _pack_lanes([v[n] for n in SMALL]), "adamw_small")
    for n, gs_, dd, mm, vv in zip(SMALL, [g_small[n] for n in SMALL], _unpack_lanes(d_, local_shapes), _unpack_lanes(m_, local_shapes),
                                  _unpack_lanes(v_, local_shapes)):
        grad[n], delta[n], new_m[n], new_v[n] = gs_.reshape(w[n].shape), dd, mm, vv

    order = ["meta_tokens", "lb_logits", "mix_norm_w", "w_in", "hg_norm_w", "gd_conv_w", "gd_a_log", "gd_dt_bias", "gd_norm_w",
             "w_branch_a", "w_branch_b", "w_out", "ffn_norm_w", "w_ffn_in", "w_ffn_out", "final_norm_w"]
    return (loss, dx[None], *[grad[n] for n in order], *[delta[n] for n in order], *[new_m[n] for n in order],
            *[new_v[n] for n in order])
```

```python
import functools
import math

import jax
import jax.numpy as jnp
from jax import lax
from jax.experimental import pallas as pl
from jax.experimental.pallas import tpu as pltpu

F32, BF16 = jnp.float32, jnp.bfloat16
EPS = 1e-6
D_MODEL = 2048
N_META = 16
FRONT = 256
CH = 64
SUB = 16
SUBH = 16
DH = 128
NH = 8
HW = NH * DH
CONV_K = 4
RT = 256
VMEM_LIMIT = 56 * 1024 * 1024
ADAM_LR, ADAM_B1, ADAM_B2, ADAM_EPS, ADAM_WD, ADAM_STEP = 0.001, 0.9, 0.999, 1e-08, 0.01, 10

NN = (((1,), (0,)), ((), ()))
NT = (((1,), (1,)), ((), ()))
TN = (((0,), (0,)), ((), ()))


def _dot(a, b, dn=NN):
    return lax.dot_general(a.astype(BF16), b.astype(BF16), dn, preferred_element_type=F32)


def _dotx(a, b, dn=NN):
    return lax.dot_general(a, b, dn, precision=lax.Precision.HIGHEST, preferred_element_type=F32)


class _Side:
    def __init__(self, inputs, out_shapes, scratch, start, wait):
        self.inputs, self.out_shapes, self.scratch, self.start, self.wait = inputs, out_shapes, scratch, start, wait


def _call(body, *, name, grid, in_specs, out_specs, out_shape, scratch=(), sem=None, side=None):
    params = pltpu.CompilerParams(dimension_semantics=sem, vmem_limit_bytes=VMEM_LIMIT)
    if side is None:
        return pl.pallas_call(body, name=name, grid=grid, in_specs=in_specs, out_specs=out_specs, out_shape=out_shape,
                              scratch_shapes=list(scratch), compiler_params=params)
    single = not isinstance(out_specs, (list, tuple))
    out_specs, out_shape = ([out_specs], [out_shape]) if single else (list(out_specs), list(out_shape))
    ni, no, ns = len(in_specs), len(out_specs), len(scratch)
    nsi, nso = len(side.inputs), len(side.out_shapes)
    hbm = pl.BlockSpec(memory_space=pltpu.HBM)

    def wrapped(*refs):
        main_in, side_in = refs[:ni], refs[ni:ni + nsi]
        main_out, side_out = refs[ni + nsi:ni + nsi + no], refs[ni + nsi + no:ni + nsi + no + nso]
        main_scr, side_scr = refs[ni + nsi + no + nso:ni + nsi + no + nso + ns], refs[ni + nsi + no + nso + ns:]
        pids = [pl.program_id(d) for d in range(len(grid))]
        first = functools.reduce(lambda a, b: a & b, [p == 0 for p in pids])
        last = functools.reduce(lambda a, b: a & b, [p == g - 1 for p, g in zip(pids, grid)])

        @pl.when(first)
        def _():
            side.start(side_in, side_out, side_scr)

        body(*main_in, *main_out, *main_scr)

        @pl.when(last)
        def _():
            side.wait(side_in, side_out, side_scr)

    call = pl.pallas_call(wrapped, name=name, grid=grid, in_specs=list(in_specs) + [hbm] * nsi,
                          out_specs=out_specs + [hbm] * nso, out_shape=out_shape + list(side.out_shapes),
                          scratch_shapes=list(scratch) + list(side.scratch), compiler_params=params)

    def run(*args):
        outs = call(*args, *side.inputs)
        main = outs[0] if single else list(outs[:no])
        return main, list(outs[no:])

    return run


def _divmod(j, per):
    if per == 1:
        return j, 0
    return lax.div(j, jnp.int32(per)), lax.rem(j, jnp.int32(per))


def _sds(shape, dtype):
    return jax.ShapeDtypeStruct(tuple(shape), dtype)


def _sigmoid(x):
    return 0.5 * jnp.tanh(0.5 * x) + 0.5


def _silu(x):
    return x * _sigmoid(x)


def _dsilu(x):
    s = _sigmoid(x)
    return s * (1.0 + x * (1.0 - s))


def _tri(n, kind):
    r = lax.broadcasted_iota(jnp.int32, (n, n), 0)
    c = lax.broadcasted_iota(jnp.int32, (n, n), 1)
    return {"incl": r >= c, "strict": r > c, "upper": c >= r}[kind]


def _mm(a, b, mode, out_dtype, tm, tn, tk, name, add=None, n_outer=False, out_shards=0, side=None):
    sharded_a = a.ndim == 3
    if sharded_a:
        n_a = a.shape[2]
        a_shape = (a.shape[1], a.shape[0] * n_a)
    else:
        a_shape = a.shape
    sharded_b = b.ndim == 3
    if sharded_b:
        S, R, n = b.shape
        b_rows, b_cols = R, S * n
    else:
        b_rows, b_cols = b.shape
    if mode == "nn":
        (M, K), N, dn = a_shape, b_cols, NN
    elif mode == "nt":
        (M, K), N, dn = a_shape, b_rows, NT
    else:
        (K, M), N, dn = a_shape, b_cols, TN
    tm, tn, tk = min(tm, M), min(tn, N), min(tk, K)
    if sharded_a:
        tk = min(tk, n_a)
    if sharded_b:
        tn, tk = (min(tn, n), tk) if mode != "nt" else (tn, min(tk, n))
    if out_shards:
        tn = min(tn, N // out_shards)
    assert M % tm == 0 and N % tn == 0 and K % tk == 0, (name, M, N, K, tm, tn, tk)
    nk = K // tk
    a_blk, a_idx = ((tm, tk), lambda i, j, k: (i, k)) if mode != "tn" else ((tk, tm), lambda i, j, k: (k, i))
    if sharded_a:
        per_a = n_a // tk
        assert mode != "tn" and n_a % tk == 0
        a_blk, a_idx = (None, tm, tk), lambda i, j, k: (_divmod(k, per_a)[0], i, _divmod(k, per_a)[1])
    if not sharded_b:
        b_blk, b_idx = ((tk, tn), lambda i, j, k: (k, j)) if mode != "nt" else ((tn, tk), lambda i, j, k: (j, k))
    elif mode != "nt":
        per = n // tn
        assert n % tn == 0
        b_blk, b_idx = (None, tk, tn), lambda i, j, k: (_divmod(j, per)[0], k, _divmod(j, per)[1])
    else:
        per = n // tk
        assert n % tk == 0
        b_blk, b_idx = (None, tn, tk), lambda i, j, k: (_divmod(k, per)[0], j, _divmod(k, per)[1])
    if out_shards:
        per_o = N // out_shards // tn
        assert (N // out_shards) % tn == 0
        o_blk, o_idx = (None, tm, tn), lambda i, j, k: (_divmod(j, per_o)[0], i, _divmod(j, per_o)[1])
        o_shape = (out_shards, M, N // out_shards)
    else:
        o_blk, o_idx, o_shape = (tm, tn), (lambda i, j, k: (i, j)), (M, N)
    c_idx = lambda i, j, k: (i, j)
    if n_outer:
        sw = lambda f: (lambda j, i, k: f(i, j, k))
        a_idx, b_idx, o_idx, c_idx = sw(a_idx), sw(b_idx), sw(o_idx), sw(c_idx)
        grid = (N // tn, M // tm, nk)
    else:
        grid = (M // tm, N // tn, nk)
    has_add = add is not None

    def body(*refs):
        if has_add:
            a_ref, b_ref, c_ref, o_ref, acc_ref = refs
        else:
            a_ref, b_ref, o_ref, acc_ref = refs
            c_ref = None
        part = lax.dot_general(a_ref[...].astype(BF16), b_ref[...].astype(BF16), dn, preferred_element_type=F32)

        def fin(val):
            if has_add:
                val = val + c_ref[...]
            o_ref[...] = val.astype(out_dtype)

        if nk == 1:
            fin(part)
        else:
            k = pl.program_id(2)

            @pl.when(k == 0)
            def _():
                acc_ref[...] = part

            @pl.when(k > 0)
            def _():
                acc_ref[...] += part

            @pl.when(k == nk - 1)
            def _():
                fin(acc_ref[...])

    in_specs = [pl.BlockSpec(a_blk, a_idx), pl.BlockSpec(b_blk, b_idx)]
    args = [a, b]
    if has_add:
        in_specs.append(pl.BlockSpec((tm, tn), c_idx))
        args.append(add)
    acc_shape = (tm, tn) if nk > 1 else (8, 128)
    return _call(body, name=name, grid=grid, in_specs=in_specs, out_specs=pl.BlockSpec(o_blk, o_idx),
                 out_shape=_sds(o_shape, out_dtype), scratch=[pltpu.VMEM(acc_shape, F32)],
                 sem=("arbitrary",) * 3 if side is not None else ("parallel", "parallel", "arbitrary"), side=side)(*args)


def _rms1_fwd(x, meta, w, side=None):
    seq, d = x.shape
    nt = (FRONT + seq) // RT

    def body(x_ref, m_ref, w_ref, o_ref):
        i = pl.program_id(0)

        def norm(v):
            r = lax.rsqrt(jnp.mean(v * v, axis=-1, keepdims=True) + EPS)
            return (v * r * w_ref[...]).astype(BF16)

        @pl.when(i == 0)
        def _():
            o_ref[0:RT - N_META, :] = jnp.zeros((RT - N_META, d), BF16)
            o_ref[RT - N_META:RT, :] = norm(m_ref[...])

        @pl.when(i > 0)
        def _():
            o_ref[...] = norm(x_ref[...])

    return _call(body, name="rms1_fwd", grid=(nt,),
                 in_specs=[pl.BlockSpec((RT, d), lambda i: (jnp.maximum(i - 1, 0), 0)),
                           pl.BlockSpec((N_META, d), lambda i: (0, 0)),
                           pl.BlockSpec((1, d), lambda i: (0, 0))],
                 out_specs=pl.BlockSpec((RT, d), lambda i: (i, 0)),
                 out_shape=_sds((FRONT + seq, d), BF16), sem=("arbitrary",) if side is not None else ("parallel",),
                 side=side)(x, meta, w)


def _rms1_bwd(x, meta, w, dxn, dh1):
    seq, d = x.shape
    nt = (FRONT + seq) // RT

    def body(x_ref, m_ref, w_ref, g_ref, r_ref, dx_ref, dm_ref, dw_ref):
        i = pl.program_id(0)

        def bwd(v, g):
            r = lax.rsqrt(jnp.mean(v * v, axis=-1, keepdims=True) + EPS)
            vh = v * r
            gh = g * w_ref[...]
            return r * (gh - vh * jnp.mean(gh * vh, axis=-1, keepdims=True)), jnp.sum(g * vh, axis=0, keepdims=True)

        @pl.when(i == 0)
        def _():
            dm, dw = bwd(m_ref[...], g_ref[RT - N_META:RT, :])
            dm_ref[...] = dm
            dw_ref[...] = dw

        @pl.when(i > 0)
        def _():
            dx, dw = bwd(x_ref[...], g_ref[...])
            dx_ref[...] = dx + r_ref[...]
            dw_ref[...] += dw

    xs = pl.BlockSpec((RT, d), lambda i: (jnp.maximum(i - 1, 0), 0))
    return _call(body, name="rms1_bwd", grid=(nt,),
                 in_specs=[xs, pl.BlockSpec((N_META, d), lambda i: (0, 0)), pl.BlockSpec((1, d), lambda i: (0, 0)),
                           pl.BlockSpec((RT, d), lambda i: (i, 0)), xs],
                 out_specs=[xs, pl.BlockSpec((N_META, d), lambda i: (0, 0)), pl.BlockSpec((1, d), lambda i: (0, 0))],
                 out_shape=[_sds((seq, d), F32), _sds((N_META, d), F32), _sds((1, d), F32)],
                 sem=("arbitrary",))(x, meta, w, dxn, dh1)


def _merge_fwd(proj, za, zb):
    seq, d = za.shape
    off = FRONT // RT
    ca, cb = 8 * HW // d, 8 * HW // d + 1

    def body(ga_ref, gb_ref, za_ref, zb_ref, o_ref):
        o_ref[...] = (_sigmoid(ga_ref[...]) * za_ref[...] + _sigmoid(gb_ref[...]) * zb_ref[...]).astype(BF16)

    zs = pl.BlockSpec((RT, d), lambda i: (i, 0))
    return _call(body, name="merge_fwd", grid=(seq // RT,),
                 in_specs=[pl.BlockSpec((RT, d), lambda i: (i + off, ca)), pl.BlockSpec((RT, d), lambda i: (i + off, cb)), zs, zs],
                 out_specs=zs, out_shape=_sds((seq, d), BF16), sem=("parallel",))(proj, proj, za, zb)


def _merge_bwd(proj, za, zb, dmerged, side=None):
    seq, d = za.shape
    off = FRONT // RT
    ca, cb = 8 * HW // d, 8 * HW // d + 1
    nt = (FRONT + seq) // RT

    def body(ga_ref, gb_ref, za_ref, zb_ref, dm_ref, dza_ref, dzb_ref, dg_ref):
        i = pl.program_id(0)

        @pl.when(i < off)
        def _():
            dg_ref[...] = jnp.zeros((RT, 2 * d), BF16)

        @pl.when(i >= off)
        def _():
            sa, sb, dm = _sigmoid(ga_ref[...]), _sigmoid(gb_ref[...]), dm_ref[...]
            dza_ref[...] = (dm * sa).astype(BF16)
            dzb_ref[...] = (dm * sb).astype(BF16)
            dg_ref[:, 0:d] = (dm * za_ref[...] * sa * (1.0 - sa)).astype(BF16)
            dg_ref[:, d:2 * d] = (dm * zb_ref[...] * sb * (1.0 - sb)).astype(BF16)

    rs = pl.BlockSpec((RT, d), lambda i: (jnp.maximum(i - off, 0), 0))
    return _call(body, name="merge_bwd", grid=(nt,),
                 in_specs=[pl.BlockSpec((RT, d), lambda i: (i, ca)), pl.BlockSpec((RT, d), lambda i: (i, cb)), rs, rs, rs],
                 out_specs=[rs, rs, pl.BlockSpec((RT, 2 * d), lambda i: (i, 0))],
                 out_shape=[_sds((seq, d), BF16), _sds((seq, d), BF16), _sds((FRONT + seq, 2 * d), BF16)],
                 sem=("arbitrary",), side=side)(proj, proj, za, zb, dmerged)


def _resid_norm_fwd(x, mix, w):
    seq, d = x.shape

    def body(x_ref, m_ref, w_ref, h_ref, n_ref):
        h = x_ref[...] + m_ref[...]
        h_ref[...] = h
        r = lax.rsqrt(jnp.mean(h * h, axis=-1, keepdims=True) + EPS)
        n_ref[...] = (h * r * w_ref[...]).astype(BF16)

    rs = pl.BlockSpec((RT, d), lambda i: (i, 0))
    return _call(body, name="resid_norm_fwd", grid=(seq // RT,),
                 in_specs=[rs, rs, pl.BlockSpec((1, d), lambda i: (0, 0))], out_specs=[rs, rs],
                 out_shape=[_sds((seq, d), F32), _sds((seq, d), BF16)], sem=("parallel",))(x, mix, w)


def _resid_norm_bwd(h1, w, dn2, dh2):
    seq, d = h1.shape

    def body(h_ref, w_ref, g_ref, r_ref, o_ref, ob_ref, dw_ref):
        i = pl.program_id(0)
        h, g = h_ref[...], g_ref[...]
        r = lax.rsqrt(jnp.mean(h * h, axis=-1, keepdims=True) + EPS)
        hh = h * r
        gh = g * w_ref[...]
        dh = r_ref[...] + r * (gh - hh * jnp.mean(gh * hh, axis=-1, keepdims=True))
        o_ref[...] = dh
        ob_ref[...] = dh.astype(BF16)
        dw = jnp.sum(g * hh, axis=0, keepdims=True)

        @pl.when(i == 0)
        def _():
            dw_ref[...] = dw

        @pl.when(i > 0)
        def _():
            dw_ref[...] += dw

    rs = pl.BlockSpec((RT, d), lambda i: (i, 0))
    ws = pl.BlockSpec((1, d), lambda i: (0, 0))
    return _call(body, name="resid_norm_bwd", grid=(seq // RT,), in_specs=[rs, ws, rs, rs], out_specs=[rs, rs, ws],
                 out_shape=[_sds((seq, d), F32), _sds((seq, d), BF16), _sds((1, d), F32)], sem=("arbitrary",))(h1, w, dn2, dh2)


def _swiglu_tiles(seq, ff):
    return min(512, seq), (1408 if ff % 1408 == 0 else 512)


def _swiglu_fwd(gu):
    seq, f2 = gu.shape
    ff = f2 // 2
    rt, tc = _swiglu_tiles(seq, ff)
    nb = ff // tc

    def body(g_ref, u_ref, o_ref):
        o_ref[...] = (_silu(g_ref[...].astype(F32)) * u_ref[...].astype(F32)).astype(BF16)

    return _call(body, name="swiglu_fwd", grid=(seq // rt, nb),
                 in_specs=[pl.BlockSpec((rt, tc), lambda i, j: (i, j)), pl.BlockSpec((rt, tc), lambda i, j: (i, j + nb))],
                 out_specs=pl.BlockSpec((rt, tc), lambda i, j: (i, j)), out_shape=_sds((seq, ff), BF16),
                 sem=("parallel", "parallel"))(gu, gu)


def _swiglu_bwd(gu, dact):
    seq, f2 = gu.shape
    ff = f2 // 2
    rt, tc = _swiglu_tiles(seq, ff)
    nb = ff // tc

    def body(g_ref, u_ref, d_ref, o_ref):
        g, d = g_ref[...].astype(F32), d_ref[...].astype(F32)
        sig = _sigmoid(g)
        o_ref[0] = (d * u_ref[...].astype(F32) * (sig * (1.0 + g * (1.0 - sig)))).astype(BF16)
        o_ref[1] = (d * (g * sig)).astype(BF16)

    bs = pl.BlockSpec((rt, tc), lambda i, j: (i, j))
    return _call(body, name="swiglu_bwd", grid=(seq // rt, nb),
                 in_specs=[bs, pl.BlockSpec((rt, tc), lambda i, j: (i, j + nb)), bs],
                 out_specs=pl.BlockSpec((2, rt, tc), lambda i, j: (0, i, j)),
                 out_shape=_sds((2, seq, ff), BF16), sem=("parallel", "parallel"))(gu, gu, dact)


def _loss_head(h1, f, w, tgt):
    seq, d = h1.shape
    nt = seq // RT

    def body(h_ref, f_ref, w_ref, t_ref, l_ref, dh_ref, dhb_ref, dw_ref):
        i = pl.program_id(0)
        h = h_ref[...] + f_ref[...]
        r = lax.rsqrt(jnp.mean(h * h, axis=-1, keepdims=True) + EPS)
        hh = h * r
        err = hh * w_ref[...] - t_ref[...]
        l_ref[...] = jnp.full((8, 128), 0.5 * jnp.sum(jnp.mean(err * err, axis=-1, keepdims=True)), F32)
        dy = err * (1.0 / d)
        gh = dy * w_ref[...]
        dh = r * (gh - hh * jnp.mean(gh * hh, axis=-1, keepdims=True))
        dh_ref[...] = dh
        dhb_ref[...] = dh.astype(BF16)
        dw = jnp.sum(dy * hh, axis=0, keepdims=True)

        @pl.when(i == 0)
        def _():
            dw_ref[...] = dw

        @pl.when(i > 0)
        def _():
            dw_ref[...] += dw

    rs = pl.BlockSpec((RT, d), lambda i: (i, 0))
    ws = pl.BlockSpec((1, d), lambda i: (0, 0))
    return _call(body, name="loss_head", grid=(nt,), in_specs=[rs, rs, ws, rs],
                 out_specs=[pl.BlockSpec((8, 128), lambda i: (i, 0)), rs, rs, ws],
                 out_shape=[_sds((nt * 8, 128), F32), _sds((seq, d), F32), _sds((seq, d), BF16), _sds((1, d), F32)],
                 sem=("arbitrary",))(h1, f, w, tgt)


def _sum_tiles(lt):
    n = lt.shape[0]

    def body(l_ref, o_ref):
        v = l_ref[...]
        r = lax.broadcasted_iota(jnp.int32, v.shape, 0)
        c = lax.broadcasted_iota(jnp.int32, v.shape, 1)
        o_ref[...] = jnp.sum(jnp.where((r % 8 == 0) & (c == 0), v, 0.0), keepdims=True)

    return _call(body, name="loss_sum", grid=(1,), in_specs=[pl.BlockSpec((n, 128), lambda i: (0, 0))],
                 out_specs=pl.BlockSpec((1, 1), lambda i: (0, 0)), out_shape=_sds((1, 1), F32))(lt)


def _gated_norm_fwd(o, g, nw):
    r = lax.rsqrt(jnp.mean(o * o, axis=-1, keepdims=True) + EPS)
    return o * r * nw * _silu(g)


def _gated_norm_bwd(o, g, nw, dout):
    r = lax.rsqrt(jnp.mean(o * o, axis=-1, keepdims=True) + EPS)
    oh = o * r
    sig = _sigmoid(g)
    don = dout * (g * sig)
    dg = dout * (oh * nw) * (sig * (1.0 + g * (1.0 - sig)))
    dnw = jnp.sum(don * oh, axis=0, keepdims=True)
    doh = don * nw
    return r * (doh - oh * jnp.mean(doh * oh, axis=-1, keepdims=True)), dg, dnw


def _hg_gates(fs, lbl):
    l0, l1 = lbl[0:1, :], lbl[1:2, :]
    m = jnp.maximum(l0, l1)
    e0, e1 = jnp.exp(l0 - m), jnp.exp(l1 - m)
    lb = e0 / (e0 + e1)
    sig = _sigmoid(fs)
    f = lb + (1.0 - lb) * sig
    return lb, sig, f, jnp.log(f), (1.0 - lb) * (1.0 - sig)


def _cumsum(w):
    row = lax.broadcasted_iota(jnp.int32, w.shape, 0) & (CH - 1)
    s = 1
    while s < CH:
        w = w + jnp.where(row >= s, pltpu.roll(w, s, 0), 0.0)
        s *= 2
    return w


def _rcumsum(w):
    row = lax.broadcasted_iota(jnp.int32, w.shape, 0) & (CH - 1)
    s = 1
    while s < CH:
        w = w + jnp.where(row < CH - s, pltpu.roll(w, w.shape[0] - s, 0), 0.0)
        s *= 2
    return w


def _decay_blocks(q, k, b, p_ref):
    p_ref[...] = jnp.zeros((CH, CH), F32)
    m16 = _tri(SUBH, "incl")
    for I in range(CH // SUBH):
        s0 = I * SUBH
        bI, qI, kI = b[s0:s0 + SUBH], q[s0:s0 + SUBH], k[s0:s0 + SUBH]
        dec = jnp.exp(jnp.minimum(bI[:, None, :] - bI[None, :, :], 0.0))
        pii = jnp.sum(qI[:, None, :] * kI[None, :, :] * dec, axis=-1)
        p_ref[s0:s0 + SUBH, s0:s0 + SUBH] = jnp.where(m16, pii, 0.0)
        if I > 0:
            rI = b[s0 - 1:s0]
            qs = qI * jnp.exp(bI - rI)
            ks = k[0:s0] * jnp.exp(rI - b[0:s0])
            p_ref[s0:s0 + SUBH, 0:s0] = _dot(qs, ks, NT)


HPS = 8
HPS_HGRN2_FWD = 4
HB = 4
NHB = NH // HPS
OFF = FRONT // RT


def _head_specs(hps, rev=None):
    row = (lambda i: i) if rev is None else rev
    col = lambda g: pl.BlockSpec((RT, hps * DH), lambda h, i: (row(i), g * (NH // hps) + h))
    full = pl.BlockSpec((RT, hps * DH), lambda h, i: (row(i), h))
    real = pl.BlockSpec((RT, hps * DH), lambda h, i: (jnp.maximum(row(i) - OFF, 0), h))
    state = lambda cpt: pl.BlockSpec((hps, cpt, DH, DH), lambda h, i: (h, row(i), 0, 0))
    scal = pl.BlockSpec((hps, RT, DH), lambda h, i: (h, row(i), 0))
    return col, full, real, state, scal


def _hgrn2_fwd(proj, lb_logits, nw):
    tp = proj.shape[0]
    nt, cpt = tp // RT, RT // CH
    hps = HPS_HGRN2_FWD

    def body(q_ref, f_ref, i_ref, g_ref, lbl_ref, nw_ref, og_ref, or_ref, st_ref, pm_ref, s_ref, p_ref):
        @pl.when(pl.program_id(1) == 0)
        def _():
            s_ref[...] = jnp.zeros((hps, DH, DH), F32)

        def chunk(c, carry):
            rows = pl.ds(pl.multiple_of(c * CH, CH), CH)
            for hh in range(hps):
                cols = slice(hh * DH, (hh + 1) * DH)
                _, _, _, w, k = _hg_gates(f_ref[rows, cols], lbl_ref[:, cols])
                q, v = _silu(q_ref[rows, cols]), i_ref[rows, cols]
                b = _cumsum(w)
                st = s_ref[hh]
                st_ref[hh, c] = st
                _decay_blocks(q, k, b, p_ref.at[hh])
                pm_ref[hh, c] = p_ref[hh]
                o = _dot(q * jnp.exp(b), st, NT) + _dot(p_ref[hh], v)
                bl = b[CH - 1:CH]
                s_ref[hh] = st * jnp.exp(bl) + _dot(v, k * jnp.exp(bl - b), TN)
                or_ref[rows, cols] = o
                og_ref[rows, cols] = _gated_norm_fwd(o, g_ref[rows, cols], nw_ref[...]).astype(BF16)
            return carry

        lax.fori_loop(0, cpt, chunk, 0)

    col, full, real, state, _ = _head_specs(hps)
    return _call(body, name="hgrn2_fwd", grid=(NH // hps, nt),
                 in_specs=[col(0), col(1), col(2), col(3), pl.BlockSpec((2, hps * DH), lambda h, i: (0, h)),
                           pl.BlockSpec((1, DH), lambda h, i: (0, 0))],
                 out_specs=[real, full, state(cpt), pl.BlockSpec((hps, cpt, CH, CH), lambda h, i: (h, i, 0, 0))],
                 out_shape=[_sds((tp - FRONT, HW), BF16), _sds((tp, HW), F32), _sds((NH, tp // CH, DH, DH), F32),
                            _sds((NH, tp // CH, CH, CH), F32)],
                 scratch=[pltpu.VMEM((hps, DH, DH), F32), pltpu.VMEM((hps, CH, CH), F32)],
                 sem=("parallel", "arbitrary"))(proj, proj, proj, proj, lb_logits, nw)


def _hgrn2_bwd(proj, lb_logits, nw, o_raw, states, pmat, dog, side=None):
    tp = proj.shape[0]
    nt, cpt = tp // RT, RT // CH

    def body(q_ref, f_ref, i_ref, g_ref, lbl_ref, nw_ref, or_ref, st_ref, pm_ref, dog_ref,
             dq_ref, df_ref, di_ref, dg_ref, dl_ref, dnw_ref, ds_ref, dk_ref, dqa_ref, do_ref):
        step = pl.program_id(1)

        @pl.when(step == 0)
        def _():
            ds_ref[...] = jnp.zeros((HPS, DH, DH), F32)
            dl_ref[...] = jnp.zeros((2, HPS * DH), F32)

        @pl.when((step == 0) & (pl.program_id(0) == 0))
        def _():
            dnw_ref[...] = jnp.zeros((1, DH), F32)

        front = nt - 1 - step < OFF
        for hh in range(HPS):
            cols = slice(hh * DH, (hh + 1) * DH)
            dog_t = jnp.where(front, 0.0, dog_ref[:, cols])
            do_t, dg_t, dnw = _gated_norm_bwd(or_ref[:, cols], g_ref[:, cols], nw_ref[...], dog_t)
            do_ref[:, cols] = do_t
            dg_ref[:, cols] = dg_t.astype(BF16)
            dnw_ref[...] += dnw
        tril = _tri(CH, "incl")
        m16 = _tri(SUBH, "incl")

        def chunk(cc, carry):
            c = cpt - 1 - cc
            rows = pl.ds(pl.multiple_of(c * CH, CH), CH)
            for hh in range(HPS):
                cols = slice(hh * DH, (hh + 1) * DH)
                fs = f_ref[rows, cols]
                lb, sig, f, w, k = _hg_gates(fs, lbl_ref[:, cols])
                hq = q_ref[rows, cols]
                q, v, do = _silu(hq), i_ref[rows, cols], do_ref[rows, cols]
                b = _cumsum(w)
                bl = b[CH - 1:CH]
                eb = jnp.exp(b)
                qs, kd = q * eb, k * jnp.exp(bl - b)
                st, dst = st_ref[hh, c], ds_ref[hh]
                dv = _dot(pm_ref[hh, c], do, TN) + _dot(kd, dst, NT)
                dp = jnp.where(tril, _dot(do, v, NT), 0.0)
                dqa, dka = dqa_ref.at[hh], dk_ref.at[hh]
                dqa[...] = eb * _dotx(do, st)
                dka[...] = jnp.exp(bl - b) * _dotx(v, dst)
                for I in range(CH // SUBH):
                    s0 = I * SUBH
                    bI, qI, kI = b[s0:s0 + SUBH], q[s0:s0 + SUBH], k[s0:s0 + SUBH]
                    dec = jnp.exp(jnp.minimum(bI[:, None, :] - bI[None, :, :], 0.0))
                    dpii = jnp.where(m16, dp[s0:s0 + SUBH, s0:s0 + SUBH], 0.0)[:, :, None] * dec
                    dqa[s0:s0 + SUBH, :] += jnp.sum(dpii * kI[None, :, :], axis=1)
                    dka[s0:s0 + SUBH, :] += jnp.sum(dpii * qI[:, None, :], axis=0)
                    if I > 0:
                        rI = b[s0 - 1:s0]
                        eq, ek = jnp.exp(bI - rI), jnp.exp(rI - b[0:s0])
                        dpij = dp[s0:s0 + SUBH, 0:s0]
                        dqa[s0:s0 + SUBH, :] += eq * _dotx(dpij, k[0:s0] * ek)
                        dka[0:s0, :] += ek * _dotx(dpij, qI * eq, TN)
                dq, dk = dqa[...], dka[...]
                st_end = st * jnp.exp(bl) + _dotx(v, kd, TN)
                dw = _rcumsum(q * dq - k * dk) + jnp.sum(dst * st_end, axis=0, keepdims=True)
                ds_ref[hh] = dst * jnp.exp(bl) + _dotx(do, qs, TN)
                one_m = 1.0 - sig
                dq_ref[rows, cols] = (dq * _dsilu(hq)).astype(BF16)
                df_ref[rows, cols] = ((dw / f - dk) * (1.0 - lb) * sig * one_m).astype(BF16)
                di_ref[rows, cols] = dv.astype(BF16)
                dl_ref[0:1, cols] += jnp.sum((dw / f - dk) * one_m, axis=0, keepdims=True)
            return carry

        lax.fori_loop(0, cpt, chunk, 0)

        @pl.when(step == nt - 1)
        def _():
            lbl = lbl_ref[...]
            l0, l1 = lbl[0:1, :], lbl[1:2, :]
            m = jnp.maximum(l0, l1)
            e0, e1 = jnp.exp(l0 - m), jnp.exp(l1 - m)
            p0 = e0 / (e0 + e1)
            dl0 = dl_ref[0:1, :] * p0 * (1.0 - p0)
            dl_ref[0:1, :] = dl0
            dl_ref[1:2, :] = -dl0

    col, full, real, state, _ = _head_specs(HPS, lambda i: nt - 1 - i)
    lbs = pl.BlockSpec((2, HPS * DH), lambda h, i: (0, h))
    return _call(body, name="hgrn2_bwd", grid=(NHB, nt),
                 in_specs=[col(0), col(1), col(2), col(3), lbs, pl.BlockSpec((1, DH), lambda h, i: (0, 0)), full,
                           state(cpt), pl.BlockSpec((HPS, cpt, CH, CH), lambda h, i: (h, nt - 1 - i, 0, 0)), real],
                 out_specs=[full, full, full, full, lbs, pl.BlockSpec((1, DH), lambda h, i: (0, 0))],
                 out_shape=[_sds((tp, HW), BF16)] * 4 + [_sds((2, HW), F32), _sds((1, DH), F32)],
                 scratch=[pltpu.VMEM((HPS, DH, DH), F32), pltpu.VMEM((HPS, CH, DH), F32),
                          pltpu.VMEM((HPS, CH, DH), F32), pltpu.VMEM((RT, HPS * DH), F32)],
                 sem=("arbitrary", "arbitrary"), side=side)(proj, proj, proj, proj, lb_logits, nw, o_raw, states, pmat, dog)


GQ0 = 4 * HW
CW = 3 * HW


def _gd_scalars(ab, alog, dtb):
    g = -jnp.exp(alog) * jax.nn.softplus(ab + dtb)
    return g, _sigmoid(ab)


def _conv_ext_specs(row_of):
    main = [pl.BlockSpec((RT, HW), lambda i, g=g: (row_of(i), GQ0 // HW + g)) for g in range(3)]
    prev = [pl.BlockSpec((8, HW), lambda i, g=g: (jnp.maximum(row_of(i) * (RT // 8) - 1, 0), GQ0 // HW + g)) for g in range(3)]
    return main + prev


def _conv_fill(ext_ref, xs, xps, first):
    for g in range(3):
        ext_ref[0:8, g * HW:(g + 1) * HW] = jnp.where(first, 0.0, xps[g][...])
        ext_ref[8:8 + RT, g * HW:(g + 1) * HW] = xs[g][...]


def _conv_apply(ext_ref, cw):
    y = cw[CONV_K - 1:CONV_K, :] * ext_ref[pl.ds(8, RT), :]
    for s in range(1, CONV_K):
        y += cw[CONV_K - 1 - s:CONV_K - s, :] * ext_ref[pl.ds(8 - s, RT), :]
    return y


def _gdn_prep_fwd(proj, pab, conv_w, alog, dtb):
    tp = proj.shape[0]
    nt = tp // RT

    def body(x0, x1, x2, p0, p1, p2, ab_ref, cw_ref, al_ref, dt_ref, q_ref, k_ref, v_ref, g_ref, b_ref, ext_ref):
        _conv_fill(ext_ref, (x0, x1, x2), (p0, p1, p2), pl.program_id(0) == 0)
        a = _silu(_conv_apply(ext_ref, cw_ref[...]))
        for h in range(NH):
            for part, ref, sc in ((0, q_ref, DH ** -0.5), (1, k_ref, 1.0)):
                seg = a[:, part * HW + h * DH:part * HW + (h + 1) * DH]
                ref[:, h * DH:(h + 1) * DH] = seg * (lax.rsqrt(jnp.sum(seg * seg, axis=-1, keepdims=True) + EPS) * sc)
        v_ref[...] = a[:, 2 * HW:3 * HW]
        g, beta = _gd_scalars(ab_ref[...], al_ref[...], dt_ref[...])
        for h in range(NH):
            g_ref[h] = jnp.broadcast_to(g[:, h:h + 1], (RT, DH))
            b_ref[h] = jnp.broadcast_to(beta[:, NH + h:NH + h + 1], (RT, DH))

    hs = pl.BlockSpec((RT, HW), lambda i: (i, 0))
    sc = pl.BlockSpec((NH, RT, DH), lambda i: (0, i, 0))
    one = pl.BlockSpec((1, DH), lambda i: (0, 0))
    return _call(body, name="gdn_prep_fwd", grid=(nt,),
                 in_specs=_conv_ext_specs(lambda i: i) + [pl.BlockSpec((RT, DH), lambda i: (i, 0)),
                                                           pl.BlockSpec((CONV_K, CW), lambda i: (0, 0)), one, one],
                 out_specs=[hs, hs, hs, sc, sc],
                 out_shape=[_sds((tp, HW), F32)] * 3 + [_sds((NH, tp, DH), F32)] * 2,
                 scratch=[pltpu.VMEM((RT + 8, CW), F32)], sem=("parallel",))(*([proj] * 6), pab, conv_w, alog, dtb)


def _gdn_prep_bwd(proj, pab, conv_w, alog, dtb, dq, dk, dv, dgb, dbb):
    tp = proj.shape[0]
    nt = tp // RT

    def body(x0, x1, x2, p0, p1, p2, ab_ref, cw_ref, al_ref, dt_ref, dq_ref, dk_ref, dv_ref, dg_ref, db_ref,
             dx_ref, dab_ref, dcw_ref, dal_ref, ddt_ref, ext_ref, dy_ref):
        step = pl.program_id(0)
        i = nt - 1 - step

        @pl.when(step == 0)
        def _():
            dy_ref[RT:RT + 8, :] = jnp.zeros((8, CW), F32)
            dcw_ref[...] = jnp.zeros((8, CW), F32)
            dal_ref[...] = jnp.zeros((1, DH), F32)
            ddt_ref[...] = jnp.zeros((1, DH), F32)

        _conv_fill(ext_ref, (x0, x1, x2), (p0, p1, p2), i == 0)
        cw = cw_ref[...]
        y = _conv_apply(ext_ref, cw)
        sig = _sigmoid(y)
        a = y * sig
        dsl = sig * (1.0 + y * (1.0 - sig))
        for h in range(NH):
            for part, ref, sc in ((0, dq_ref, DH ** -0.5), (1, dk_ref, 1.0)):
                lo = part * HW + h * DH
                seg = a[:, lo:lo + DH]
                r = lax.rsqrt(jnp.sum(seg * seg, axis=-1, keepdims=True) + EPS)
                xh = seg * r
                dxh = ref[:, h * DH:(h + 1) * DH] * sc
                dy_ref[0:RT, lo:lo + DH] = r * (dxh - xh * jnp.sum(dxh * xh, axis=-1, keepdims=True)) * dsl[:, lo:lo + DH]
        dy_ref[0:RT, 2 * HW:3 * HW] = dv_ref[...] * dsl[:, 2 * HW:3 * HW]
        dy = dy_ref[0:RT, :]
        dx = cw[CONV_K - 1:CONV_K, :] * dy
        dcw_ref[CONV_K - 1:CONV_K, :] += jnp.sum(dy * ext_ref[pl.ds(8, RT), :], axis=0, keepdims=True)
        for s in range(1, CONV_K):
            dx += cw[CONV_K - 1 - s:CONV_K - s, :] * dy_ref[pl.ds(s, RT), :]
            dcw_ref[CONV_K - 1 - s:CONV_K - s, :] += jnp.sum(dy * ext_ref[pl.ds(8 - s, RT), :], axis=0, keepdims=True)
        dx_ref[...] = dx.astype(BF16)
        dy_ref[RT:RT + 8, :] = dy[0:8, :]
        ab = ab_ref[...]
        g, beta = _gd_scalars(ab, al_ref[...], dt_ref[...])
        lane = lax.broadcasted_iota(jnp.int32, (RT, DH), 1)
        dgl = jnp.zeros((RT, DH), F32)
        dbl = jnp.zeros((RT, DH), F32)
        for h in range(NH):
            dgl = jnp.where(lane == h, dg_ref[h], dgl)
            dbl = jnp.where(lane == NH + h, db_ref[h], dbl)
        dsp = dgl * (-jnp.exp(al_ref[...])) * _sigmoid(ab + dt_ref[...])
        dab_ref[...] = (dsp + dbl * beta * (1.0 - beta)).astype(BF16)
        ddt_ref[...] += jnp.sum(dsp, axis=0, keepdims=True)
        dal_ref[...] += jnp.sum(dgl * g, axis=0, keepdims=True)

    hs = pl.BlockSpec((RT, HW), lambda s: (nt - 1 - s, 0))
    sc = pl.BlockSpec((NH, RT, DH), lambda s: (0, nt - 1 - s, 0))
    one = pl.BlockSpec((1, DH), lambda s: (0, 0))
    xs = pl.BlockSpec((RT, CW), lambda s: (nt - 1 - s, 0))
    return _call(body, name="gdn_prep_bwd", grid=(nt,),
                 in_specs=_conv_ext_specs(lambda s: nt - 1 - s) + [
                     pl.BlockSpec((RT, DH), lambda s: (nt - 1 - s, 0)), pl.BlockSpec((CONV_K, CW), lambda s: (0, 0)),
                     one, one, hs, hs, hs, sc, sc],
                 out_specs=[xs, pl.BlockSpec((RT, DH), lambda s: (nt - 1 - s, 0)), pl.BlockSpec((8, CW), lambda s: (0, 0)), one, one],
                 out_shape=[_sds((tp, CW), BF16), _sds((tp, DH), BF16), _sds((8, CW), F32), _sds((1, DH), F32), _sds((1, DH), F32)],
                 scratch=[pltpu.VMEM((RT + 8, CW), F32), pltpu.VMEM((RT + 8, CW), F32)],
                 sem=("arbitrary",))(*([proj] * 6), pab, conv_w, alog, dtb, dq, dk, dv, dgb, dbb)


HS = HB * CH


def _stack_heads(ref, rows, base):
    return jnp.concatenate([ref[rows, (base + hh) * DH:(base + hh + 1) * DH] for hh in range(HB)], axis=0)


def _stack_scal(ref, rows, base):
    return jnp.concatenate([ref[base + hh, rows, :] for hh in range(HB)], axis=0)


def _store_heads(ref, rows, base, val):
    for hh in range(HB):
        ref[rows, (base + hh) * DH:(base + hh + 1) * DH] = val[hh * CH:(hh + 1) * CH].astype(ref.dtype)


def _bd_masks():
    r = lax.broadcasted_iota(jnp.int32, (HS, HS), 0)
    c = lax.broadcasted_iota(jnp.int32, (HS, HS), 1)
    same = lax.shift_right_logical(r, int(math.log2(CH))) == lax.shift_right_logical(c, int(math.log2(CH)))
    return same & (r >= c), same & (r > c)


def _unit_lower_inverse(a):
    n = a.shape[0]
    r = lax.broadcasted_iota(jnp.int32, (n, n), 0)
    c = lax.broadcasted_iota(jnp.int32, (n, n), 1)
    blk_of = lambda t, size: lax.shift_right_logical(t, int(math.log2(size)))
    a16 = jnp.where(blk_of(r, SUB) == blk_of(c, SUB), a, 0.0)
    x = (r == c).astype(F32) - a16
    p = a16
    for _ in range(3):
        p = _dot(p, p)
        x = x + _dot(x, p)
    for blk in (2 * SUB, 4 * SUB):
        off = jnp.where((blk_of(r, blk) == blk_of(c, blk)) & (blk_of(r, blk // 2) != blk_of(c, blk // 2)), a, 0.0)
        x = x - _dot(x, _dot(off, x))
    return x


def _gdn_chunk_common(q, k, v, gl, bt, incl, strict, x=None):
    gc = _cumsum(gl)
    e = jnp.exp(gc)
    rel = jnp.exp(jnp.minimum(gc[:, 0:1] - gc.T[0:1, :], 0.0))
    kb = bt * k
    a = jnp.where(strict, bt[:, 0:1] * _dot(k, k, NT) * rel, 0.0)
    if x is None:
        x = _unit_lower_inverse(a)
    wu = _dot(x, jnp.concatenate([kb * e, bt * v], axis=1))
    attn = jnp.where(incl, _dot(q, k, NT) * rel, 0.0)
    return gc, e, rel, kb, a, x, wu[:, 0:DH], wu[:, DH:2 * DH], attn


def _gdn_fwd(q, k, v, gb, bb, proj, nw):
    tp = q.shape[0]
    nt, cpt = tp // RT, RT // CH

    def body(q_ref, k_ref, v_ref, g_ref, b_ref, z_ref, nw_ref, og_ref, or_ref, st_ref, x_ref, s_ref):
        @pl.when(pl.program_id(1) == 0)
        def _():
            s_ref[...] = jnp.zeros((HPS, DH, DH), F32)

        incl, strict = _bd_masks()

        def chunk(c, carry):
            rows = pl.ds(pl.multiple_of(c * CH, CH), CH)
            for base in range(0, HPS, HB):
                qc, kc, vc = _stack_heads(q_ref, rows, base), _stack_heads(k_ref, rows, base), _stack_heads(v_ref, rows, base)
                gc, e, rel, kb, a, x, w, u, attn = _gdn_chunk_common(qc, kc, vc, _stack_scal(g_ref, rows, base),
                                                                    _stack_scal(b_ref, rows, base), incl, strict)
                x_ref[base // HB, c] = x
                qe = qc * e
                ws, qs = [], []
                for hh in range(HB):
                    blk = slice(hh * CH, (hh + 1) * CH)
                    s = s_ref[base + hh]
                    st_ref[base + hh, c] = s
                    both = _dot(jnp.concatenate([w[blk], qe[blk]], axis=0), s)
                    ws.append(both[0:CH])
                    qs.append(both[CH:2 * CH])
                vn = u - jnp.concatenate(ws, axis=0)
                o = jnp.concatenate(qs, axis=0) + _dot(attn, vn)
                for hh in range(HB):
                    blk = slice(hh * CH, (hh + 1) * CH)
                    gl = gc[(hh + 1) * CH - 1:(hh + 1) * CH]
                    s_ref[base + hh] = s_ref[base + hh] * jnp.exp(gl) + _dot(kc[blk] * jnp.exp(gl - gc[blk]), vn[blk], TN)
                _store_heads(or_ref, rows, base, o)
                for hh in range(HB):
                    cols = slice((base + hh) * DH, (base + hh + 1) * DH)
                    og_ref[rows, cols] = _gated_norm_fwd(o[hh * CH:(hh + 1) * CH], z_ref[rows, cols], nw_ref[...]).astype(BF16)
            return carry

        lax.fori_loop(0, cpt, chunk, 0)

    col, full, real, state, scal = _head_specs(HPS)
    return _call(body, name="gdn_fwd", grid=(NHB, nt),
                 in_specs=[full, full, full, scal, scal, col(7), pl.BlockSpec((1, DH), lambda h, i: (0, 0))],
                 out_specs=[real, full, state(cpt), pl.BlockSpec((HPS // HB, cpt, HS, HS), lambda h, i: (h, i, 0, 0))],
                 out_shape=[_sds((tp - FRONT, HW), BF16), _sds((tp, HW), F32), _sds((NH, tp // CH, DH, DH), F32),
                            _sds((NH // HB, tp // CH, HS, HS), F32)],
                 scratch=[pltpu.VMEM((HPS, DH, DH), F32)], sem=("parallel", "arbitrary"))(q, k, v, gb, bb, proj, nw)


def _gdn_bwd(q, k, v, gb, bb, proj, nw, o_raw, states, xinv, dog):
    tp = q.shape[0]
    nt, cpt = tp // RT, RT // CH

    def body(q_ref, k_ref, v_ref, g_ref, b_ref, z_ref, nw_ref, or_ref, st_ref, x_ref, dog_ref,
             dq_ref, dk_ref, dv_ref, dg_ref, db_ref, dz_ref, dnw_ref, ds_ref, do_ref):
        step = pl.program_id(1)

        @pl.when(step == 0)
        def _():
            ds_ref[...] = jnp.zeros((HPS, DH, DH), F32)

        @pl.when((step == 0) & (pl.program_id(0) == 0))
        def _():
            dnw_ref[...] = jnp.zeros((1, DH), F32)

        front = nt - 1 - step < OFF
        for hh in range(HPS):
            cols = slice(hh * DH, (hh + 1) * DH)
            dog_t = jnp.where(front, 0.0, dog_ref[:, cols])
            do_t, dz_t, dnw = _gated_norm_bwd(or_ref[:, cols], z_ref[:, cols], nw_ref[...], dog_t)
            do_ref[:, cols] = do_t
            dz_ref[:, cols] = dz_t.astype(BF16)
            dnw_ref[...] += dnw
        incl, strict = _bd_masks()
        last_row = (lax.broadcasted_iota(jnp.int32, (CH, 1), 0) == CH - 1)

        def rsum(t):
            return jnp.sum(t, axis=-1, keepdims=True)

        def chunk(cc, carry):
            c = cpt - 1 - cc
            rows = pl.ds(pl.multiple_of(c * CH, CH), CH)
            for base in range(0, HPS, HB):
                qc, kc, vc, do = (_stack_heads(q_ref, rows, base), _stack_heads(k_ref, rows, base), _stack_heads(v_ref, rows, base),
                                  _stack_heads(do_ref, rows, base))
                bt = _stack_scal(b_ref, rows, base)
                gc, e, rel, kb, a, x, w, u, attn = _gdn_chunk_common(qc, kc, vc, _stack_scal(g_ref, rows, base), bt, incl, strict,
                                                                    x=x_ref[base // HB, c])
                qe = qc * e
                heads = [slice(hh * CH, (hh + 1) * CH) for hh in range(HB)]
                gls = [gc[(hh + 1) * CH - 1:(hh + 1) * CH] for hh in range(HB)]
                cdec = jnp.concatenate([jnp.exp(gl - gc[blk]) for gl, blk in zip(gls, heads)], axis=0)
                kcd = kc * cdec
                vn = u - jnp.concatenate([_dot(w[blk], st_ref[base + hh, c]) for hh, blk in enumerate(heads)], axis=0)
                dos = jnp.concatenate([_dot(do[blk], st_ref[base + hh, c], NT) for hh, blk in enumerate(heads)], axis=0)
                kds = jnp.concatenate([_dot(kcd[blk], ds_ref[base + hh]) for hh, blk in enumerate(heads)], axis=0)
                vds = jnp.concatenate([_dot(vn[blk], ds_ref[base + hh], NT) for hh, blk in enumerate(heads)], axis=0)
                dvn = _dot(attn, do, TN) + kds
                dattn = jnp.where(incl, _dot(do, vn, NT), 0.0)
                dar = dattn * rel
                dq = _dot(dar, kc) + e * dos
                dk = _dot(dar, qc, TN) + cdec * vds
                dc = cdec[:, 0:1] * rsum(kc * vds)
                dgc = rsum(qe * dos) - dc
                dw = -jnp.concatenate([_dot(dvn[blk], st_ref[base + hh, c], NT) for hh, blk in enumerate(heads)], axis=0)
                extra = []
                for hh, blk in enumerate(heads):
                    s, dsn = st_ref[base + hh, c], ds_ref[base + hh]
                    el = jnp.exp(gls[hh])
                    dglast = jnp.sum(dc[blk], axis=0, keepdims=True) + el[:, 0:1] * jnp.sum(rsum(dsn * s), axis=0, keepdims=True)
                    extra.append(jnp.where(last_row, dglast, 0.0))
                    ds_ref[base + hh] = dsn * el + _dot(jnp.concatenate([qe[blk], -w[blk]], axis=0),
                                                 jnp.concatenate([do[blk], dvn[blk]], axis=0), TN)
                dr = _dot(x, jnp.concatenate([dw, dvn], axis=1), TN)
                drw, dru = dr[:, 0:DH], dr[:, DH:2 * DH]
                da = -jnp.where(strict, _dot(dr, jnp.concatenate([w, u], axis=1), NT), 0.0)
                dar2 = da * rel
                dkb = _dot(dar2, kc)
                rwk = rsum(drw * kc)
                dk = dk + _dot(dar2, kb, TN) + bt * dkb + (bt * e) * drw
                dbeta = rsum(dkb * kc) + e[:, 0:1] * rwk + rsum(dru * vc)
                z = dattn * attn + da * a
                dgc = dgc + bt[:, 0:1] * e[:, 0:1] * rwk + rsum(z) - rsum(z.T) + jnp.concatenate(extra, axis=0)
                _store_heads(dq_ref, rows, base, dq)
                _store_heads(dk_ref, rows, base, dk)
                _store_heads(dv_ref, rows, base, bt * dru)
                dg = _rcumsum(jnp.broadcast_to(dgc, (HS, DH)))
                dbb = jnp.broadcast_to(dbeta, (HS, DH))
                for hh, blk in enumerate(heads):
                    dg_ref[base + hh, rows, :] = dg[blk]
                    db_ref[base + hh, rows, :] = dbb[blk]
            return carry

        lax.fori_loop(0, cpt, chunk, 0)

    col, full, real, state, scal = _head_specs(HPS, lambda i: nt - 1 - i)
    one = pl.BlockSpec((1, DH), lambda h, i: (0, 0))
    return _call(body, name="gdn_bwd", grid=(NHB, nt),
                 in_specs=[full, full, full, scal, scal, col(7), one, full, state(cpt),
                           pl.BlockSpec((HPS // HB, cpt, HS, HS), lambda h, i: (h, nt - 1 - i, 0, 0)), real],
                 out_specs=[full, full, full, scal, scal, full, one],
                 out_shape=[_sds((tp, HW), F32)] * 3 + [_sds((NH, tp, DH), F32)] * 2 + [_sds((tp, HW), BF16), _sds((1, DH), F32)],
                 scratch=[pltpu.VMEM((HPS, DH, DH), F32), pltpu.VMEM((RT, HPS * DH), F32)],
                 sem=("arbitrary", "arbitrary"))(q, k, v, gb, bb, proj, nw, o_raw, states, xinv, dog)


MAIN_W = 8 * HW
AB_W = 2 * NH


def _w_in_pieces(lo, hi):
    total = MAIN_W + AB_W
    out = []
    if lo < MAIN_W:
        out.append(("m", lo, min(hi, MAIN_W)))
    if hi > MAIN_W and lo < total:
        out.append(("ab", max(lo, MAIN_W) - MAIN_W, min(hi, total) - MAIN_W))
    if hi > total:
        out.append(("m", max(lo, total) - AB_W, hi - AB_W))
    return out


W_IN_TILE = 128


def _unshard_w_in(slabs):
    ns, d, cs = slabs.shape
    main_w = ns * cs - AB_W

    def body(w_ref, m_ref, ab_ref):
        ab_ref[...] = jnp.zeros((W_IN_TILE, DH), slabs.dtype)
        for s_ in range(ns):
            off = 0
            for src, a, b in _w_in_pieces(s_ * cs, (s_ + 1) * cs):
                piece = w_ref[s_, :, off:off + (b - a)]
                (m_ref if src == "m" else ab_ref)[:, a:b] = piece
                off += b - a

    return _call(body, name="unshard_w_in", grid=(d // W_IN_TILE,),
                 in_specs=[pl.BlockSpec((ns, W_IN_TILE, cs), lambda i: (0, i, 0))],
                 out_specs=[pl.BlockSpec((W_IN_TILE, main_w), lambda i: (i, 0)), pl.BlockSpec((W_IN_TILE, DH), lambda i: (i, 0))],
                 out_shape=[_sds((d, main_w), slabs.dtype), _sds((d, DH), slabs.dtype)], sem=("parallel",))(slabs)


def _shard_w_in_grad(dw_main, dw_ab, ns=4):
    d, main_w = dw_main.shape
    cs = (main_w + AB_W) // ns

    def body(m_ref, ab_ref, o_ref):
        for s_ in range(ns):
            parts = [(m_ref if src == "m" else ab_ref)[:, a:b] for src, a, b in _w_in_pieces(s_ * cs, (s_ + 1) * cs)]
            o_ref[s_] = parts[0] if len(parts) == 1 else jnp.concatenate(parts, axis=1)

    return _call(body, name="shard_w_in_grad", grid=(d // W_IN_TILE,),
                 in_specs=[pl.BlockSpec((W_IN_TILE, main_w), lambda i: (i, 0)), pl.BlockSpec((W_IN_TILE, DH), lambda i: (i, 0))],
                 out_specs=pl.BlockSpec((ns, W_IN_TILE, cs), lambda i: (0, i, 0)),
                 out_shape=_sds((ns, d, cs), dw_main.dtype), sem=("parallel",))(dw_main, dw_ab)


def _pad_lanes(v):
    return jnp.pad(v, ((0, 0), (0, DH - v.shape[1])))


class _NoComm:
    def __init__(self, w_in, late):
        self.w_in, self.late = w_in, late

    def first_side(self):
        return None

    def first_weight(self, side_outs):
        return self.w_in

    def proj_side(self):
        return None

    def late_weights(self, side_outs):
        return self.late

    def ffn_swap_side(self, dw_ffn_in, dw_ffn_out):
        return None

    def ffn_swap_done(self, side_outs):
        pass

    def early_grads_side(self, grads):
        return None

    def early_grads_done(self, side_outs):
        pass

    def last_grad_side(self, dw_in):
        return None

    def last_grad_done(self, side_outs):
        pass


def _local_step(x, tgt, meta, lb_logits, mix_w, hg_nw, conv_w, a_log, dt_bias, gd_nw, ffn_nw, final_w, comm):
    alog, dtb = _pad_lanes(a_log), _pad_lanes(dt_bias)
    final_w = final_w.reshape(1, -1)
    rows4 = lambda t: t.reshape(4, t.shape[0] // 4, t.shape[1])
    side = comm.first_side()
    xn = _rms1_fwd(x, meta, mix_w, side=side)
    xn, landed = xn if side is not None else (xn, None)
    w_main, w_ab = _unshard_w_in(comm.first_weight(landed))
    side = comm.proj_side()
    proj = _mm(xn, w_main, "nn", F32, 768, 2048, 2048, "proj_main", n_outer=True, side=side)
    proj, landed = proj if side is not None else (proj, None)
    w_a, w_b, w_out, w_ffn_in, w_ffn_out = comm.late_weights(landed)
    pab = _mm(xn, w_ab, "nn", F32, 768, 128, 2048, "proj_ab")
    oa_g, oa_raw, st_a, pm_a = _hgrn2_fwd(proj, lb_logits, hg_nw)
    q, k, v, gb, bb = _gdn_prep_fwd(proj, pab, conv_w, alog, dtb)
    ob_g, ob_raw, st_b, xinv = _gdn_fwd(q, k, v, gb, bb, proj, gd_nw)
    za = _mm(oa_g, w_a, "nn", F32, 1024, 512, 1024, "branch_a", n_outer=True)
    zb = _mm(ob_g, w_b, "nn", F32, 1024, 512, 1024, "branch_b", n_outer=True)
    merged = _merge_fwd(proj, za, zb)
    mix = _mm(merged, w_out, "nn", F32, 1024, 2048, 2048, "mix_out")
    h1, n2 = _resid_norm_fwd(x, mix, ffn_nw)
    gu = _mm(n2, w_ffn_in, "nn", BF16, 1024, 1408, 2048, "ffn_in", n_outer=True)
    act = _swiglu_fwd(gu)
    f = _mm(act, w_ffn_out, "nn", F32, 1024, 2048, 1408, "ffn_out")
    lt, dh2, dh2b, dfinal = _loss_head(h1, f, final_w, tgt)
    loss = _sum_tiles(lt)
    dact = _mm(dh2b, w_ffn_out, "nt", BF16, 1024, 1408, 2048, "d_act", n_outer=True)
    dw_ffn_out = rows4(_mm(act, dh2b, "tn", F32, 512, 2048, 2048, "dw_ffn_out"))
    dgu = _swiglu_bwd(gu, dact)
    dn2 = _mm(dgu, w_ffn_in, "nt", F32, 1024, 2048, 1408, "d_n2")
    dw_ffn_in = _mm(n2, dgu, "tn", F32, 1024, 1408, 2048, "dw_ffn_in", out_shards=4)
    dh1, dh1b, dffn_nw = _resid_norm_bwd(h1, ffn_nw, dn2, dh2)
    dmerged = _mm(dh1b, w_out, "nt", F32, 1024, 2048, 2048, "d_merged")
    dw_out = _mm(merged, dh1b, "tn", F32, 2048, 1024, 1024, "dw_out")
    side = comm.ffn_swap_side(dw_ffn_in, dw_ffn_out)
    mb = _merge_bwd(proj, za, zb, dmerged, side=side)
    if side is not None:
        mb, swapped = mb
        comm.ffn_swap_done(swapped)
    dza, dzb, dgate = mb
    doa = _mm(dza, w_a, "nt", F32, 1024, 1024, 512, "d_oa")
    dob = _mm(dzb, w_b, "nt", F32, 1024, 1024, 512, "d_ob")
    dw_a = _mm(oa_g, dza, "tn", F32, 1024, 512, 1024, "dw_branch_a", out_shards=4)
    dw_b = _mm(ob_g, dzb, "tn", F32, 1024, 512, 1024, "dw_branch_b", out_shards=4)
    early = dict(w_ffn_in=dw_ffn_in, w_ffn_out=dw_ffn_out, w_out=rows4(dw_out), w_branch_a=dw_a, w_branch_b=dw_b)
    side = comm.early_grads_side(early)
    hg = _hgrn2_bwd(proj, lb_logits, hg_nw, oa_raw, st_a, pm_a, doa, side=side)
    if side is not None:
        hg, arrived = hg
        comm.early_grads_done(arrived)
    dhq, dhf, dhi, dhg, dlbl, dhg_nw = hg
    dq, dk, dv, dg, dbeta, dz, dgd_nw = _gdn_bwd(q, k, v, gb, bb, proj, gd_nw, ob_raw, st_b, xinv, dob)
    dx3, dab, dconv, dalog, ddtb = _gdn_prep_bwd(proj, pab, conv_w, alog, dtb, dq, dk, dv, dg, dbeta)
    dproj = jnp.concatenate([dhq, dhf, dhi, dhg, dx3, dz, dgate], axis=1)
    dw_main = _mm(xn, dproj, "tn", F32, 1024, 1024, 2816, "dw_in_main")
    dw_ab = _mm(xn, dab, "tn", F32, 2048, 128, 768, "dw_in_ab")
    dw_in = _shard_w_in_grad(dw_main, dw_ab)
    side = comm.last_grad_side(dw_in)
    dxn = _mm(dproj, w_main, "nt", F32, 768, 2048, 2048, "d_xn", side=side)
    if side is not None:
        dxn, arrived = dxn
        comm.last_grad_done(arrived)
    dxn = _mm(dab, w_ab, "nt", F32, 768, 2048, 128, "d_xn_ab", add=dxn)
    dx, dmeta, dmix_w = _rms1_bwd(x, meta, mix_w, dxn, dh1)
    grads = dict(meta_tokens=dmeta, lb_logits=dlbl, mix_norm_w=dmix_w, w_in=dw_in,
                 hg_norm_w=dhg_nw, gd_conv_w=dconv[:CONV_K], gd_a_log=dalog[:, :NH],
                 gd_dt_bias=ddtb[:, :NH], gd_norm_w=dgd_nw, w_branch_a=dw_a, w_branch_b=dw_b,
                 w_out=rows4(dw_out), ffn_norm_w=dffn_nw, w_ffn_in=dw_ffn_in, w_ffn_out=dw_ffn_out,
                 final_norm_w=dfinal.reshape(-1))
    return loss, dx, grads


def _adamw(g, w, m, v, name):
    rows, cols = g.shape
    tr = rows
    for cand in (128, 64, 32, 16, 8):
        if rows % cand == 0 and rows > cand:
            tr = cand
            break

    def body(g_ref, w_ref, m_ref, v_ref, go_ref, d_ref, nm_ref, nv_ref):
        gg = g_ref[...]
        go_ref[...] = gg
        nm = ADAM_B1 * m_ref[...] + (1.0 - ADAM_B1) * gg
        nv = ADAM_B2 * v_ref[...] + (1.0 - ADAM_B2) * (gg * gg)
        m_hat = nm / (1.0 - ADAM_B1 ** ADAM_STEP)
        v_hat = nv / (1.0 - ADAM_B2 ** ADAM_STEP)
        d_ref[...] = -ADAM_LR * (m_hat / (jnp.sqrt(v_hat) + ADAM_EPS) + ADAM_WD * w_ref[...])
        nm_ref[...] = nm
        nv_ref[...] = nv

    bs = pl.BlockSpec((tr, cols), lambda i: (i, 0))
    return _call(body, name=name, grid=(rows // tr,), in_specs=[bs] * 4, out_specs=[bs] * 4,
                 out_shape=[_sds((rows, cols), F32)] * 4, sem=("parallel",))(g, w, m, v)


HBM = pl.BlockSpec(memory_space=pltpu.HBM)
MESH = pl.DeviceIdType.MESH


def _place():
    x, y, c = lax.axis_index("x"), lax.axis_index("y"), lax.axis_index("c")
    return x, y, c, [(1 - x, y), (x, 1 - y), (1 - x, 1 - y)]


def _comm_call(body, name, out_shape, n_in, scratch):
    return pl.pallas_call(body, name=name, out_shape=out_shape, in_specs=[HBM] * n_in,
                          out_specs=jax.tree.map(lambda _: HBM, out_shape), scratch_shapes=scratch)


def _half_rows(rows, c, tile):
    hh = rows // 2
    assert rows % 2 == 0 and hh % tile == 0, (rows, tile)
    return pl.ds(pl.multiple_of(c * hh, tile), hh)


def _gather_copies(w_refs, out_refs, sems):
    send_sems, recv_sems = sems
    x, y, c, chips = _place()
    s_me = 2 * x + y
    sends, recvs = [], []
    for k, (w_ref, out_ref) in enumerate(zip(w_refs, out_refs)):
        half = _half_rows(w_ref.shape[0], c, 16)
        for j, (cx, cy) in enumerate(chips):
            sem = dict(send_sem=send_sems.at[3 * k + j], recv_sem=recv_sems.at[3 * k + j], device_id=(cx, cy, c), device_id_type=MESH)
            sends.append(pltpu.make_async_remote_copy(src_ref=w_ref.at[half], dst_ref=out_ref.at[s_me, half], **sem))
            recvs.append(pltpu.make_async_remote_copy(src_ref=w_ref.at[half], dst_ref=out_ref.at[2 * cx + cy, half], **sem))
    return sends, recvs


def _gather_sems(n):
    return [pltpu.SemaphoreType.DMA((3 * n,)), pltpu.SemaphoreType.DMA((3 * n,))]


def _gather_start(w_refs, out_refs, sems):
    for cp in _gather_copies(w_refs, out_refs, sems)[0]:
        cp.start()


def _gather_wait(w_refs, out_refs, sems):
    sends, recvs = _gather_copies(w_refs, out_refs, sems)
    for cp in recvs:
        cp.wait_recv()
    for cp in sends:
        cp.wait_send()


def _gather_side(shards):
    return _Side(shards, [_sds((4,) + w.shape, w.dtype) for w in shards], _gather_sems(len(shards)), _gather_start, _gather_wait)


def _forward_halves(outs, name):
    n = len(outs)

    def body(*refs):
        out_refs = refs[n:2 * n]
        send_sems, recv_sems = refs[2 * n:]
        x, y, c, chips = _place()
        cps = []
        for k in range(n):
            rows = out_refs[k].shape[1]
            half, other = _half_rows(rows, c, 16), _half_rows(rows, 1 - c, 16)
            for j, (cx, cy) in enumerate(chips):
                sem = dict(send_sem=send_sems.at[3 * k + j], recv_sem=recv_sems.at[3 * k + j], device_id=(x, y, 1 - c), device_id_type=MESH)
                landed = out_refs[k].at[2 * cx + cy, half]
                cps.append(pltpu.make_async_remote_copy(src_ref=landed, dst_ref=landed, **sem))
                cps[-1].start()
        for k in range(n):
            rows = out_refs[k].shape[1]
            half, other = _half_rows(rows, c, 16), _half_rows(rows, 1 - c, 16)
            for j, (cx, cy) in enumerate(chips):
                sem = dict(send_sem=send_sems.at[3 * k + j], recv_sem=recv_sems.at[3 * k + j], device_id=(x, y, 1 - c), device_id_type=MESH)
                pltpu.make_async_remote_copy(src_ref=out_refs[k].at[2 * cx + cy, half], dst_ref=out_refs[k].at[2 * cx + cy, other], **sem).wait_recv()
        for cp in cps:
            cp.wait_send()

    shapes = [_sds(o.shape, o.dtype) for o in outs]
    return pl.pallas_call(body, name=name, out_shape=shapes, in_specs=[HBM] * n, out_specs=[HBM] * n,
                          input_output_aliases={k: k for k in range(n)},
                          scratch_shapes=[pltpu.SemaphoreType.DMA((3 * n,)), pltpu.SemaphoreType.DMA((3 * n,))])(*outs)


def _swap_copies(g_refs, out_refs, sems):
    send_sems, recv_sems = sems
    x, y, c, _ = _place()
    cps = []
    for k, (g_ref, out_ref) in enumerate(zip(g_refs, out_refs)):
        other = _half_rows(g_ref.shape[1], 1 - c, 8)
        cps.append(pltpu.make_async_remote_copy(src_ref=g_ref.at[:, other, :], dst_ref=out_ref, send_sem=send_sems.at[k],
                                                recv_sem=recv_sems.at[k], device_id=(x, y, 1 - c), device_id_type=MESH))
    return cps


def _swap_start(g_refs, out_refs, sems):
    for cp in _swap_copies(g_refs, out_refs, sems):
        cp.start()


def _swap_wait(g_refs, out_refs, sems):
    for cp in _swap_copies(g_refs, out_refs, sems):
        cp.wait()


def _swap_shapes(gs):
    return [_sds((4, g.shape[1] // 2, g.shape[2]), g.dtype) for g in gs]


def _swap_sems(n):
    return [pltpu.SemaphoreType.DMA((n,)), pltpu.SemaphoreType.DMA((n,))]


def _swap_halves(gs, name):
    n = len(gs)

    def body(*refs):
        _swap_start(refs[:n], refs[n:2 * n], refs[2 * n:])
        _swap_wait(refs[:n], refs[n:2 * n], refs[2 * n:])

    return _comm_call(body, name, _swap_shapes(gs), n, _swap_sems(n))(*gs)


def _swap_side(gs):
    return _Side(gs, _swap_shapes(gs), _swap_sems(len(gs)), _swap_start, _swap_wait)


def _row_tile(rows, row_bytes, budget=3 << 20):
    if rows * row_bytes <= budget:
        return rows
    return max(t for t in range(16, rows, 16) if rows % t == 0 and t * row_bytes <= budget)


def _add_half(g, got, c, name):
    _, rows, cols = g.shape
    hh = rows // 2
    tr = _row_tile(hh, cols * 4)
    nb = hh // tr

    def body(c_ref, a_ref, b_ref, o_ref):
        o_ref[...] = (a_ref[...] + b_ref[...]).astype(BF16)

    gs = pltpu.PrefetchScalarGridSpec(
        num_scalar_prefetch=1, grid=(4, nb),
        in_specs=[pl.BlockSpec((1, tr, cols), lambda s, i, c_ref: (s, c_ref[0] * nb + i, 0)),
                  pl.BlockSpec((1, tr, cols), lambda s, i, c_ref: (s, i, 0))],
        out_specs=pl.BlockSpec((1, tr, cols), lambda s, i, c_ref: (s, i, 0)))
    return pl.pallas_call(body, name=name, grid_spec=gs, out_shape=_sds((4, hh, cols), BF16),
                          compiler_params=pltpu.CompilerParams(dimension_semantics=("parallel", "parallel"),
                                                               vmem_limit_bytes=VMEM_LIMIT))(c, g, got)


def _scatter_copies(p_refs, out_refs, sems):
    send_sems, recv_sems = sems
    x, y, c, chips = _place()
    s_me = 2 * x + y
    cps = []
    for k, (p_ref, out_ref) in enumerate(zip(p_refs, out_refs)):
        for j, (cx, cy) in enumerate(chips):
            cps.append(pltpu.make_async_remote_copy(src_ref=p_ref.at[2 * cx + cy], dst_ref=out_ref.at[s_me],
                                                    send_sem=send_sems.at[3 * k + j], recv_sem=recv_sems.at[3 * k + j],
                                                    device_id=(cx, cy, c), device_id_type=MESH))
    return cps


def _scatter_start(p_refs, out_refs, sems):
    for cp in _scatter_copies(p_refs, out_refs, sems):
        cp.start()


def _scatter_wait(p_refs, out_refs, sems):
    for cp in _scatter_copies(p_refs, out_refs, sems):
        cp.wait()


def _scatter_side(ps):
    return _Side(ps, [_sds(p_.shape, p_.dtype) for p_ in ps], _gather_sems(len(ps)), _scatter_start, _scatter_wait)


def _sum_slabs(b, name):
    n, h, wd = b.shape
    tr = _row_tile(h, n * wd * 4, 6 << 20)

    def body(b_ref, o_ref):
        acc = b_ref[0]
        for s in range(1, n):
            acc = acc + b_ref[s]
        o_ref[...] = acc

    return _call(body, name=name, grid=(h // tr,), in_specs=[pl.BlockSpec((n, tr, wd), lambda i: (0, i, 0))],
                 out_specs=pl.BlockSpec((tr, wd), lambda i: (i, 0)), out_shape=_sds((h, wd), F32), sem=("parallel",))(b)


def _sum_chips(arrived, own, name):
    n, h, wd = arrived.shape
    tr = _row_tile(h, n * wd * 2, 6 << 20)
    nb = h // tr
    my_chip = lambda: 2 * lax.axis_index("x") + lax.axis_index("y")

    def body(b_ref, p_ref, o_ref):
        acc = None
        for s in range(n):
            term = jnp.where(my_chip() == s, p_ref[0], b_ref[s]).astype(F32)
            acc = term if acc is None else acc + term
        o_ref[...] = acc

    return _call(body, name=name, grid=(nb,),
                 in_specs=[pl.BlockSpec((n, tr, wd), lambda i: (0, i, 0)), pl.BlockSpec((1, tr, wd), lambda i: (my_chip(), i, 0))],
                 out_specs=pl.BlockSpec((tr, wd), lambda i: (lax.axis_index("c") * nb + i, 0)),
                 out_shape=_sds((2 * h, wd), F32), sem=("parallel",))(arrived, own)


def _share_halves(gs):
    n = len(gs)

    def body(*refs):
        out_refs = refs[n:2 * n]
        send_sems, recv_sems = refs[2 * n:]
        x, y, c, _ = _place()
        cps = []
        for k in range(n):
            half, other = _half_rows(out_refs[k].shape[0], c, 8), _half_rows(out_refs[k].shape[0], 1 - c, 8)
            sem = dict(send_sem=send_sems.at[k], recv_sem=recv_sems.at[k], device_id=(x, y, 1 - c), device_id_type=MESH)
            cps.append((pltpu.make_async_remote_copy(src_ref=out_refs[k].at[half], dst_ref=out_refs[k].at[half], **sem),
                        pltpu.make_async_remote_copy(src_ref=out_refs[k].at[half], dst_ref=out_refs[k].at[other], **sem)))
            cps[-1][0].start()
        for send, recv in cps:
            recv.wait_recv()
            send.wait_send()

    return pl.pallas_call(body, name="share_halves", out_shape=[_sds(g.shape, g.dtype) for g in gs], in_specs=[HBM] * n,
                          out_specs=[HBM] * n, input_output_aliases={k: k for k in range(n)},
                          scratch_shapes=[pltpu.SemaphoreType.DMA((n,)), pltpu.SemaphoreType.DMA((n,))])(*gs)


def _gather_all(v, name):
    def body(v_ref, out_ref, send_sems, recv_sems, local_sem):
        x, y, c = lax.axis_index("x"), lax.axis_index("y"), lax.axis_index("c")
        me = 4 * x + 2 * y + c
        flip = lambda t, d: 1 - t if d else t
        mine = pltpu.make_async_copy(v_ref, out_ref.at[me], local_sem)
        mine.start()
        cps = []
        for k in range(1, 8):
            to = (flip(x, k & 4), flip(y, k & 2), flip(c, k & 1))
            cps.append(pltpu.make_async_remote_copy(src_ref=v_ref, dst_ref=out_ref.at[me], send_sem=send_sems.at[k - 1],
                                                    recv_sem=recv_sems.at[k - 1], device_id=to, device_id_type=MESH))
        for cp in cps:
            cp.start()
        for cp in cps:
            cp.wait()
        mine.wait()

    return _comm_call(body, name, _sds((8,) + v.shape, v.dtype), 1,
                      [pltpu.SemaphoreType.DMA((7,)), pltpu.SemaphoreType.DMA((7,)), pltpu.SemaphoreType.DMA])(v)


BIG = (("w_in", 1), ("w_branch_a", 1), ("w_branch_b", 1), ("w_out", 0), ("w_ffn_in", 1), ("w_ffn_out", 0))
SMALL = ("meta_tokens", "lb_logits", "mix_norm_w", "hg_norm_w", "gd_conv_w", "gd_a_log", "gd_dt_bias", "gd_norm_w",
         "ffn_norm_w", "final_norm_w")


def _pack_lanes(parts):
    rows = []
    for p in parts:
        f = p.reshape(-1).astype(F32)
        n = -(-f.shape[0] // DH) * DH
        rows.append(jnp.pad(f, (0, n - f.shape[0])).reshape(-1, DH))
    buf = jnp.concatenate(rows, axis=0)
    return jnp.pad(buf, ((0, -buf.shape[0] % 8), (0, 0)))


def _unpack_lanes(buf, shapes):
    out, off = [], 0
    for shp in shapes:
        n = math.prod(shp)
        r = -(-n // DH)
        out.append(buf[off:off + r].reshape(-1)[:n].reshape(shp))
        off += r
    return out


def kernel(x, meta_tokens, lb_logits, mix_norm_w, w_in, hg_norm_w, gd_conv_w, gd_a_log, gd_dt_bias, gd_norm_w, w_branch_a, w_branch_b, w_out, ffn_norm_w, w_ffn_in, w_ffn_out, final_norm_w, loss_target, m_meta_tokens, m_lb_logits, m_mix_norm_w, m_w_in, m_hg_norm_w, m_gd_conv_w, m_gd_a_log, m_gd_dt_bias, m_gd_norm_w, m_w_branch_a, m_w_branch_b, m_w_out, m_ffn_norm_w, m_w_ffn_in, m_w_ffn_out, m_final_norm_w, v_meta_tokens, v_lb_logits, v_mix_norm_w, v_w_in, v_hg_norm_w, v_gd_conv_w, v_gd_a_log, v_gd_dt_bias, v_gd_norm_w, v_w_branch_a, v_w_branch_b, v_w_out, v_ffn_norm_w, v_w_ffn_in, v_w_ffn_out, v_final_norm_w):
    args = dict(locals())
    big = [n for n, _ in BIG]
    w = {n: args[n] for n in SMALL + tuple(big)}
    m = {n: args["m_" + n] for n in w}
    v = {n: args["v_" + n] for n in w}
    xi, yi, ci = lax.axis_index("x"), lax.axis_index("y"), lax.axis_index("c")
    shard = 2 * xi + yi
    big_local = {n: w[n][0] for n in big}

    meta_cols, conv_cols = meta_tokens.shape[1], gd_conv_w.shape[-1]
    sm_all = _gather_all(_pack_lanes([meta_tokens, gd_conv_w[0]]), "gather_meta")
    sm_parts = [_unpack_lanes(sm_all[2 * s], [meta_tokens.shape, gd_conv_w[0].shape]) for s in range(4)]
    meta_full = jnp.concatenate([p[0] for p in sm_parts], axis=1)
    conv_full = jnp.concatenate([p[1] for p in sm_parts], axis=1)
    cvec = ci.reshape(1).astype(jnp.int32)
    late = [n for n in big if n != "w_in"]
    rows_full = lambda t: t.reshape(t.shape[0] * t.shape[1], t.shape[2])

    def pair_sums(names, gs):
        return [_add_half(gk, got, cvec, "add_half_" + n) for n, gk, got in zip(names, gs, _swap_halves(gs, "swap_" + names[0]))]

    def with_own(slabs, n):
        return lax.dynamic_update_index_in_dim(slabs, big_local[n].astype(BF16), shard, 0)

    class MeshComm:
        def first_side(self):
            return _gather_side([big_local["w_in"].astype(BF16)])

        def first_weight(self, landed):
            return with_own(_forward_halves(landed, "forward_w_in")[0], "w_in")

        def proj_side(self):
            return _gather_side([big_local[n].astype(BF16) for n in late])

        def late_weights(self, landed):
            wl = {n: with_own(t, n) for n, t in zip(late, _forward_halves(landed, "forward_late"))}
            return (wl["w_branch_a"], wl["w_branch_b"], rows_full(wl["w_out"]), wl["w_ffn_in"], rows_full(wl["w_ffn_out"]))

        def ffn_swap_side(self, dw_ffn_in, dw_ffn_out):
            return _swap_side([dw_ffn_in, dw_ffn_out])

        def ffn_swap_done(self, swapped):
            self.ffn_swapped = swapped

        def early_grads_side(self, grads):
            self.early = list(grads)
            ffn = ["w_ffn_in", "w_ffn_out"]
            others = [n for n in self.early if n not in ffn]
            sums = {n: _add_half(grads[n], got, cvec, "add_half_" + n) for n, got in zip(ffn, self.ffn_swapped)}
            sums.update(zip(others, pair_sums(others, [grads[n] for n in others])))
            self.early_parts = [sums[n] for n in self.early]
            return _scatter_side(self.early_parts)

        def early_grads_done(self, arrived):
            self.early_arrived = arrived

        def last_grad_side(self, dw_in):
            self.last_parts = pair_sums(["w_in"], [dw_in])
            return _scatter_side(self.last_parts)

        def last_grad_done(self, arrived):
            self.last_arrived = arrived

    comm = MeshComm()
    loss, dx, g = _local_step(x[0], loss_target[0], meta_full, lb_logits, mix_norm_w, hg_norm_w, conv_full,
                              gd_a_log, gd_dt_bias, gd_norm_w, ffn_norm_w, final_norm_w, comm)
    loss = lax.psum(loss[0, 0], ("x", "y", "c"))

    parts = dict(zip(comm.early + ["w_in"], comm.early_parts + comm.last_parts))
    arrived = dict(zip(comm.early + ["w_in"], comm.early_arrived + comm.last_arrived))
    g_big = dict(zip(big, _share_halves([_sum_chips(arrived[n], parts[n], "sum_chips_" + n) for n in big])))

    small_shapes = [g[n].shape for n in SMALL]
    g_all = _gather_all(_pack_lanes([g[n] for n in SMALL]), "gather_small")
    g_small = dict(zip(SMALL, _unpack_lanes(_sum_slabs(g_all, "sum_small"), small_shapes)))
    g_small["meta_tokens"] = lax.dynamic_slice_in_dim(g_small["meta_tokens"], shard * meta_cols, meta_cols, axis=1)
    g_small["gd_conv_w"] = lax.dynamic_slice_in_dim(g_small["gd_conv_w"], shard * conv_cols, conv_cols, axis=1)

    grad, delta, new_m, new_v = {}, {}, {}, {}
    for n in big:
        g_, d_, m_, v_ = _adamw(g_big[n], big_local[n], m[n][0], v[n][0], "adamw_" + n)
        grad[n] = g_.reshape(w[n].shape)
        delta[n], new_m[n], new_v[n] = d_.reshape(w[n].shape), m_.reshape(w[n].shape), v_.reshape(w[n].shape)
    local_shapes = [w[n].shape for n in SMALL]
    _, d_, m_, v_ = _adamw(_pack_lanes([g_small[n] for n in SMALL]), _pack_lanes([w[n] for n in SMALL]),
                           _pack_lanes([m[n] for n in SMALL]), _pack_lanes([v[n] for n in SMALL]), "adamw_small")
    for n, gs_, dd, mm, vv in zip(SMALL, [g_small[n] for n in SMALL], _unpack_lanes(d_, local_shapes), _unpack_lanes(m_, local_shapes),
                                  _unpack_lanes(v_, local_shapes)):
        grad[n], delta[n], new_m[n], new_v[n] = gs_.reshape(w[n].shape), dd, mm, vv

    order = ["meta_tokens", "lb_logits", "mix_norm_w", "w_in", "hg_norm_w", "gd_conv_w", "gd_a_log", "gd_dt_bias", "gd_norm_w",
             "w_branch_a", "w_branch_b", "w_out", "ffn_norm_w", "w_ffn_in", "w_ffn_out", "final_norm_w"]
    return (loss, dx[None], *[grad[n] for n in order], *[delta[n] for n in order], *[new_m[n] for n in order],
            *[new_v[n] for n in order])
```

```python
import functools
import math

import jax
import jax.numpy as jnp
from jax import lax
from jax.experimental import pallas as pl
from jax.experimental.pallas import tpu as pltpu

F32, BF16 = jnp.float32, jnp.bfloat16
EPS = 1e-6
D_MODEL = 2048
N_META = 16
FRONT = 256
CH = 64
SUB = 16
SUBH = 16
DH = 128
NH = 8
HW = NH * DH
CONV_K = 4
RT = 256
VMEM_LIMIT = 56 * 1024 * 1024
ADAM_LR, ADAM_B1, ADAM_B2, ADAM_EPS, ADAM_WD, ADAM_STEP = 0.001, 0.9, 0.999, 1e-08, 0.01, 10

NN = (((1,), (0,)), ((), ()))
NT = (((1,), (1,)), ((), ()))
TN = (((0,), (0,)), ((), ()))


def _dot(a, b, dn=NN):
    return lax.dot_general(a.astype(BF16), b.astype(BF16), dn, preferred_element_type=F32)


def _dotx(a, b, dn=NN):
    return lax.dot_general(a, b, dn, precision=lax.Precision.HIGHEST, preferred_element_type=F32)


class _Side:
    def __init__(self, inputs, out_shapes, scratch, start, wait):
        self.inputs, self.out_shapes, self.scratch, self.start, self.wait = inputs, out_shapes, scratch, start, wait


def _call(body, *, name, grid, in_specs, out_specs, out_shape, scratch=(), sem=None, side=None):
    params = pltpu.CompilerParams(dimension_semantics=sem, vmem_limit_bytes=VMEM_LIMIT)
    if side is None:
        return pl.pallas_call(body, name=name, grid=grid, in_specs=in_specs, out_specs=out_specs, out_shape=out_shape,
                              scratch_shapes=list(scratch), compiler_params=params)
    single = not isinstance(out_specs, (list, tuple))
    out_specs, out_shape = ([out_specs], [out_shape]) if single else (list(out_specs), list(out_shape))
    ni, no, ns = len(in_specs), len(out_specs), len(scratch)
    nsi, nso = len(side.inputs), len(side.out_shapes)
    hbm = pl.BlockSpec(memory_space=pltpu.HBM)

    def wrapped(*refs):
        main_in, side_in = refs[:ni], refs[ni:ni + nsi]
        main_out, side_out = refs[ni + nsi:ni + nsi + no], refs[ni + nsi + no:ni + nsi + no + nso]
        main_scr, side_scr = refs[ni + nsi + no + nso:ni + nsi + no + nso + ns], refs[ni + nsi + no + nso + ns:]
        pids = [pl.program_id(d) for d in range(len(grid))]
        first = functools.reduce(lambda a, b: a & b, [p == 0 for p in pids])
        last = functools.reduce(lambda a, b: a & b, [p == g - 1 for p, g in zip(pids, grid)])

        @pl.when(first)
        def _():
            side.start(side_in, side_out, side_scr)

        body(*main_in, *main_out, *main_scr)

        @pl.when(last)
        def _():
            side.wait(side_in, side_out, side_scr)

    call = pl.pallas_call(wrapped, name=name, grid=grid, in_specs=list(in_specs) + [hbm] * nsi,
                          out_specs=out_specs + [hbm] * nso, out_shape=out_shape + list(side.out_shapes),
                          scratch_shapes=list(scratch) + list(side.scratch), compiler_params=params)

    def run(*args):
        outs = call(*args, *side.inputs)
        main = outs[0] if single else list(outs[:no])
        return main, list(outs[no:])

    return run


def _divmod(j, per):
    if per == 1:
        return j, 0
    return lax.div(j, jnp.int32(per)), lax.rem(j, jnp.int32(per))


def _sds(shape, dtype):
    return jax.ShapeDtypeStruct(tuple(shape), dtype)


def _sigmoid(x):
    return 0.5 * jnp.tanh(0.5 * x) + 0.5


def _silu(x):
    return x * _sigmoid(x)


def _dsilu(x):
    s = _sigmoid(x)
    return s * (1.0 + x * (1.0 - s))


def _tri(n, kind):
    r = lax.broadcasted_iota(jnp.int32, (n, n), 0)
    c = lax.broadcasted_iota(jnp.int32, (n, n), 1)
    return {"incl": r >= c, "strict": r > c, "upper": c >= r}[kind]


def _mm(a, b, mode, out_dtype, tm, tn, tk, name, add=None, n_outer=False, out_shards=0, side=None, epilogue=None):
    sharded_a = a.ndim == 3
    if sharded_a:
        n_a = a.shape[2]
        a_shape = (a.shape[1], a.shape[0] * n_a)
    else:
        a_shape = a.shape
    sharded_b = b.ndim == 3
    if sharded_b:
        S, R, n = b.shape
        b_rows, b_cols = R, S * n
    else:
        b_rows, b_cols = b.shape
    if mode == "nn":
        (M, K), N, dn = a_shape, b_cols, NN
    elif mode == "nt":
        (M, K), N, dn = a_shape, b_rows, NT
    else:
        (K, M), N, dn = a_shape, b_cols, TN
    tm, tn, tk = min(tm, M), min(tn, N), min(tk, K)
    if sharded_a:
        tk = min(tk, n_a)
    if sharded_b:
        tn, tk = (min(tn, n), tk) if mode != "nt" else (tn, min(tk, n))
    if out_shards:
        tn = min(tn, N // out_shards)
    assert M % tm == 0 and N % tn == 0 and K % tk == 0, (name, M, N, K, tm, tn, tk)
    nk = K // tk
    a_blk, a_idx = ((tm, tk), lambda i, j, k: (i, k)) if mode != "tn" else ((tk, tm), lambda i, j, k: (k, i))
    if sharded_a:
        per_a = n_a // tk
        assert mode != "tn" and n_a % tk == 0
        a_blk, a_idx = (None, tm, tk), lambda i, j, k: (_divmod(k, per_a)[0], i, _divmod(k, per_a)[1])
    if not sharded_b:
        b_blk, b_idx = ((tk, tn), lambda i, j, k: (k, j)) if mode != "nt" else ((tn, tk), lambda i, j, k: (j, k))
    elif mode != "nt":
        per = n // tn
        assert n % tn == 0
        b_blk, b_idx = (None, tk, tn), lambda i, j, k: (_divmod(j, per)[0], k, _divmod(j, per)[1])
    else:
        per = n // tk
        assert n % tk == 0
        b_blk, b_idx = (None, tn, tk), lambda i, j, k: (_divmod(k, per)[0], j, _divmod(k, per)[1])
    if out_shards:
        per_o = N // out_shards // tn
        assert (N // out_shards) % tn == 0
        o_blk, o_idx = (None, tm, tn), lambda i, j, k: (_divmod(j, per_o)[0], i, _divmod(j, per_o)[1])
        o_shape = (out_shards, M, N // out_shards)
    else:
        o_blk, o_idx, o_shape = (tm, tn), (lambda i, j, k: (i, j)), (M, N)
    c_idx = lambda i, j, k: (i, j)
    if n_outer:
        sw = lambda f: (lambda j, i, k: f(i, j, k))
        a_idx, b_idx, o_idx, c_idx = sw(a_idx), sw(b_idx), sw(o_idx), sw(c_idx)
        grid = (N // tn, M // tm, nk)
    else:
        grid = (M // tm, N // tn, nk)
    has_add = add is not None
    epi_fn, epi_ins, epi_outs = epilogue if epilogue is not None else (None, [], [])
    n_epi = len(epi_ins)
    assert epilogue is None or (tn == N and not has_add and not out_shards and side is None)

    def body(*refs):
        if has_add:
            a_ref, b_ref, c_ref, o_ref, acc_ref = refs
        elif epilogue is not None:
            a_ref, b_ref = refs[:2]
            e_refs, o_refs, acc_ref = refs[2:2 + n_epi], refs[2 + n_epi:-1], refs[-1]
            c_ref = None
        else:
            a_ref, b_ref, o_ref, acc_ref = refs
            c_ref = None
        part = lax.dot_general(a_ref[...].astype(BF16), b_ref[...].astype(BF16), dn, preferred_element_type=F32)

        def fin(val):
            if epilogue is not None:
                for ref, res in zip(o_refs, epi_fn(val, *[r[...] for r in e_refs])):
                    ref[...] = res.astype(ref.dtype)
                return
            if has_add:
                val = val + c_ref[...]
            o_ref[...] = val.astype(out_dtype)

        if nk == 1:
            fin(part)
        else:
            k = pl.program_id(2)

            @pl.when(k == 0)
            def _():
                acc_ref[...] = part

            @pl.when(k > 0)
            def _():
                acc_ref[...] += part

            @pl.when(k == nk - 1)
            def _():
                fin(acc_ref[...])

    in_specs = [pl.BlockSpec(a_blk, a_idx), pl.BlockSpec(b_blk, b_idx)]
    args = [a, b]
    if has_add:
        in_specs.append(pl.BlockSpec((tm, tn), c_idx))
        args.append(add)
    acc_shape = (tm, tn) if nk > 1 else (8, 128)
    if epilogue is not None:
        row_idx = (lambda j, i, k: (0, j)) if n_outer else (lambda i, j, k: (0, j))
        in_specs += [pl.BlockSpec((tm, tn), c_idx) if e.shape[0] == M else pl.BlockSpec((1, tn), row_idx) for e in epi_ins]
        return _call(body, name=name, grid=grid, in_specs=in_specs, out_specs=[pl.BlockSpec(o_blk, o_idx)] * len(epi_outs),
                     out_shape=[_sds(o_shape, dt) for dt in epi_outs], scratch=[pltpu.VMEM(acc_shape, F32)],
                     sem=("parallel", "parallel", "arbitrary"))(*args, *epi_ins)
    return _call(body, name=name, grid=grid, in_specs=in_specs, out_specs=pl.BlockSpec(o_blk, o_idx),
                 out_shape=_sds(o_shape, out_dtype), scratch=[pltpu.VMEM(acc_shape, F32)],
                 sem=("arbitrary",) * 3 if side is not None else ("parallel", "parallel", "arbitrary"), side=side)(*args)


def _rms1_fwd(x, meta, w, side=None):
    seq, d = x.shape
    nt = (FRONT + seq) // RT

    def body(x_ref, m_ref, w_ref, o_ref):
        i = pl.program_id(0)

        def norm(v):
            r = lax.rsqrt(jnp.mean(v * v, axis=-1, keepdims=True) + EPS)
            return (v * r * w_ref[...]).astype(BF16)

        @pl.when(i == 0)
        def _():
            o_ref[0:RT - N_META, :] = jnp.zeros((RT - N_META, d), BF16)
            o_ref[RT - N_META:RT, :] = norm(m_ref[...])

        @pl.when(i > 0)
        def _():
            o_ref[...] = norm(x_ref[...])

    return _call(body, name="rms1_fwd", grid=(nt,),
                 in_specs=[pl.BlockSpec((RT, d), lambda i: (jnp.maximum(i - 1, 0), 0)),
                           pl.BlockSpec((N_META, d), lambda i: (0, 0)),
                           pl.BlockSpec((1, d), lambda i: (0, 0))],
                 out_specs=pl.BlockSpec((RT, d), lambda i: (i, 0)),
                 out_shape=_sds((FRONT + seq, d), BF16), sem=("arbitrary",) if side is not None else ("parallel",),
                 side=side)(x, meta, w)


def _rms1_bwd(x, meta, w, dxn, dh1):
    seq, d = x.shape
    nt = (FRONT + seq) // RT

    def body(x_ref, m_ref, w_ref, g_ref, r_ref, dx_ref, dm_ref, dw_ref):
        i = pl.program_id(0)

        def bwd(v, g):
            r = lax.rsqrt(jnp.mean(v * v, axis=-1, keepdims=True) + EPS)
            vh = v * r
            gh = g * w_ref[...]
            return r * (gh - vh * jnp.mean(gh * vh, axis=-1, keepdims=True)), jnp.sum(g * vh, axis=0, keepdims=True)

        @pl.when(i == 0)
        def _():
            dm, dw = bwd(m_ref[...], g_ref[RT - N_META:RT, :])
            dm_ref[...] = dm
            dw_ref[...] = dw

        @pl.when(i > 0)
        def _():
            dx, dw = bwd(x_ref[...], g_ref[...])
            dx_ref[...] = dx + r_ref[...]
            dw_ref[...] += dw

    xs = pl.BlockSpec((RT, d), lambda i: (jnp.maximum(i - 1, 0), 0))
    return _call(body, name="rms1_bwd", grid=(nt,),
                 in_specs=[xs, pl.BlockSpec((N_META, d), lambda i: (0, 0)), pl.BlockSpec((1, d), lambda i: (0, 0)),
                           pl.BlockSpec((RT, d), lambda i: (i, 0)), xs],
                 out_specs=[xs, pl.BlockSpec((N_META, d), lambda i: (0, 0)), pl.BlockSpec((1, d), lambda i: (0, 0))],
                 out_shape=[_sds((seq, d), F32), _sds((N_META, d), F32), _sds((1, d), F32)],
                 sem=("arbitrary",))(x, meta, w, dxn, dh1)


def _merge_fwd(proj, za, zb):
    seq, d = za.shape
    off = FRONT // RT
    ca, cb = 8 * HW // d, 8 * HW // d + 1

    def body(ga_ref, gb_ref, za_ref, zb_ref, o_ref):
        o_ref[...] = (_sigmoid(ga_ref[...]) * za_ref[...] + _sigmoid(gb_ref[...]) * zb_ref[...]).astype(BF16)

    zs = pl.BlockSpec((RT, d), lambda i: (i, 0))
    return _call(body, name="merge_fwd", grid=(seq // RT,),
                 in_specs=[pl.BlockSpec((RT, d), lambda i: (i + off, ca)), pl.BlockSpec((RT, d), lambda i: (i + off, cb)), zs, zs],
                 out_specs=zs, out_shape=_sds((seq, d), BF16), sem=("parallel",))(proj, proj, za, zb)


def _merge_bwd(proj, za, zb, dmerged):
    seq, d = za.shape
    off = FRONT // RT
    ca, cb = 8 * HW // d, 8 * HW // d + 1
    nt = (FRONT + seq) // RT

    def body(ga_ref, gb_ref, za_ref, zb_ref, dm_ref, dza_ref, dzb_ref, dg_ref):
        i = pl.program_id(0)

        @pl.when(i < off)
        def _():
            dg_ref[...] = jnp.zeros((RT, 2 * d), BF16)

        @pl.when(i >= off)
        def _():
            sa, sb, dm = _sigmoid(ga_ref[...]), _sigmoid(gb_ref[...]), dm_ref[...]
            dza_ref[...] = (dm * sa).astype(BF16)
            dzb_ref[...] = (dm * sb).astype(BF16)
            dg_ref[:, 0:d] = (dm * za_ref[...] * sa * (1.0 - sa)).astype(BF16)
            dg_ref[:, d:2 * d] = (dm * zb_ref[...] * sb * (1.0 - sb)).astype(BF16)

    rs = pl.BlockSpec((RT, d), lambda i: (jnp.maximum(i - off, 0), 0))
    return _call(body, name="merge_bwd", grid=(nt,),
                 in_specs=[pl.BlockSpec((RT, d), lambda i: (i, ca)), pl.BlockSpec((RT, d), lambda i: (i, cb)), rs, rs, rs],
                 out_specs=[rs, rs, pl.BlockSpec((RT, 2 * d), lambda i: (i, 0))],
                 out_shape=[_sds((seq, d), BF16), _sds((seq, d), BF16), _sds((FRONT + seq, 2 * d), BF16)],
                 sem=("arbitrary",))(proj, proj, za, zb, dmerged)


def _resid_norm_bwd(h1, w, dn2, dh2):
    seq, d = h1.shape

    def body(h_ref, w_ref, g_ref, r_ref, o_ref, ob_ref, dw_ref):
        i = pl.program_id(0)
        h, g = h_ref[...], g_ref[...]
        r = lax.rsqrt(jnp.mean(h * h, axis=-1, keepdims=True) + EPS)
        hh = h * r
        gh = g * w_ref[...]
        dh = r_ref[...] + r * (gh - hh * jnp.mean(gh * hh, axis=-1, keepdims=True))
        o_ref[...] = dh
        ob_ref[...] = dh.astype(BF16)
        dw = jnp.sum(g * hh, axis=0, keepdims=True)

        @pl.when(i == 0)
        def _():
            dw_ref[...] = dw

        @pl.when(i > 0)
        def _():
            dw_ref[...] += dw

    rs = pl.BlockSpec((RT, d), lambda i: (i, 0))
    ws = pl.BlockSpec((1, d), lambda i: (0, 0))
    return _call(body, name="resid_norm_bwd", grid=(seq // RT,), in_specs=[rs, ws, rs, rs], out_specs=[rs, rs, ws],
                 out_shape=[_sds((seq, d), F32), _sds((seq, d), BF16), _sds((1, d), F32)], sem=("arbitrary",))(h1, w, dn2, dh2)


def _swiglu_tiles(seq, ff):
    return min(512, seq), (1408 if ff % 1408 == 0 else 512)


def _swiglu_fwd(gu):
    seq, f2 = gu.shape
    ff = f2 // 2
    rt, tc = _swiglu_tiles(seq, ff)
    nb = ff // tc

    def body(g_ref, u_ref, o_ref):
        o_ref[...] = (_silu(g_ref[...].astype(F32)) * u_ref[...].astype(F32)).astype(BF16)

    return _call(body, name="swiglu_fwd", grid=(seq // rt, nb),
                 in_specs=[pl.BlockSpec((rt, tc), lambda i, j: (i, j)), pl.BlockSpec((rt, tc), lambda i, j: (i, j + nb))],
                 out_specs=pl.BlockSpec((rt, tc), lambda i, j: (i, j)), out_shape=_sds((seq, ff), BF16),
                 sem=("parallel", "parallel"))(gu, gu)


def _swiglu_bwd(gu, dact):
    seq, f2 = gu.shape
    ff = f2 // 2
    rt, tc = _swiglu_tiles(seq, ff)
    nb = ff // tc

    def body(g_ref, u_ref, d_ref, o_ref):
        g, d = g_ref[...].astype(F32), d_ref[...].astype(F32)
        sig = _sigmoid(g)
        o_ref[0] = (d * u_ref[...].astype(F32) * (sig * (1.0 + g * (1.0 - sig)))).astype(BF16)
        o_ref[1] = (d * (g * sig)).astype(BF16)

    bs = pl.BlockSpec((rt, tc), lambda i, j: (i, j))
    return _call(body, name="swiglu_bwd", grid=(seq // rt, nb),
                 in_specs=[bs, pl.BlockSpec((rt, tc), lambda i, j: (i, j + nb)), bs],
                 out_specs=pl.BlockSpec((2, rt, tc), lambda i, j: (0, i, j)),
                 out_shape=_sds((2, seq, ff), BF16), sem=("parallel", "parallel"))(gu, gu, dact)


def _loss_head(h1, f, w, tgt):
    seq, d = h1.shape
    nt = seq // RT

    def body(h_ref, f_ref, w_ref, t_ref, l_ref, dh_ref, dhb_ref, dw_ref):
        i = pl.program_id(0)
        h = h_ref[...] + f_ref[...]
        r = lax.rsqrt(jnp.mean(h * h, axis=-1, keepdims=True) + EPS)
        hh = h * r
        err = hh * w_ref[...] - t_ref[...]
        l_ref[...] = jnp.full((8, 128), 0.5 * jnp.sum(jnp.mean(err * err, axis=-1, keepdims=True)), F32)
        dy = err * (1.0 / d)
        gh = dy * w_ref[...]
        dh = r * (gh - hh * jnp.mean(gh * hh, axis=-1, keepdims=True))
        dh_ref[...] = dh
        dhb_ref[...] = dh.astype(BF16)
        dw = jnp.sum(dy * hh, axis=0, keepdims=True)

        @pl.when(i == 0)
        def _():
            dw_ref[...] = dw

        @pl.when(i > 0)
        def _():
            dw_ref[...] += dw

    rs = pl.BlockSpec((RT, d), lambda i: (i, 0))
    ws = pl.BlockSpec((1, d), lambda i: (0, 0))
    return _call(body, name="loss_head", grid=(nt,), in_specs=[rs, rs, ws, rs],
                 out_specs=[pl.BlockSpec((8, 128), lambda i: (i, 0)), rs, rs, ws],
                 out_shape=[_sds((nt * 8, 128), F32), _sds((seq, d), F32), _sds((seq, d), BF16), _sds((1, d), F32)],
                 sem=("arbitrary",))(h1, f, w, tgt)


def _sum_tiles(lt):
    n = lt.shape[0]

    def body(l_ref, o_ref):
        v = l_ref[...]
        r = lax.broadcasted_iota(jnp.int32, v.shape, 0)
        c = lax.broadcasted_iota(jnp.int32, v.shape, 1)
        o_ref[...] = jnp.sum(jnp.where((r % 8 == 0) & (c == 0), v, 0.0), keepdims=True)

    return _call(body, name="loss_sum", grid=(1,), in_specs=[pl.BlockSpec((n, 128), lambda i: (0, 0))],
                 out_specs=pl.BlockSpec((1, 1), lambda i: (0, 0)), out_shape=_sds((1, 1), F32))(lt)


def _gated_norm_fwd(o, g, nw):
    r = lax.rsqrt(jnp.mean(o * o, axis=-1, keepdims=True) + EPS)
    return o * r * nw * _silu(g)


def _gated_norm_bwd(o, g, nw, dout):
    r = lax.rsqrt(jnp.mean(o * o, axis=-1, keepdims=True) + EPS)
    oh = o * r
    sig = _sigmoid(g)
    don = dout * (g * sig)
    dg = dout * (oh * nw) * (sig * (1.0 + g * (1.0 - sig)))
    dnw = jnp.sum(don * oh, axis=0, keepdims=True)
    doh = don * nw
    return r * (doh - oh * jnp.mean(doh * oh, axis=-1, keepdims=True)), dg, dnw


def _hg_gates(fs, lbl):
    l0, l1 = lbl[0:1, :], lbl[1:2, :]
    m = jnp.maximum(l0, l1)
    e0, e1 = jnp.exp(l0 - m), jnp.exp(l1 - m)
    lb = e0 / (e0 + e1)
    sig = _sigmoid(fs)
    f = lb + (1.0 - lb) * sig
    return lb, sig, f, jnp.log(f), (1.0 - lb) * (1.0 - sig)


def _cumsum(w):
    row = lax.broadcasted_iota(jnp.int32, w.shape, 0) & (CH - 1)
    s = 1
    while s < CH:
        w = w + jnp.where(row >= s, pltpu.roll(w, s, 0), 0.0)
        s *= 2
    return w


def _rcumsum(w):
    row = lax.broadcasted_iota(jnp.int32, w.shape, 0) & (CH - 1)
    s = 1
    while s < CH:
        w = w + jnp.where(row < CH - s, pltpu.roll(w, w.shape[0] - s, 0), 0.0)
        s *= 2
    return w


def _decay_blocks(q, k, b, p_ref):
    p_ref[...] = jnp.zeros((CH, CH), F32)
    m16 = _tri(SUBH, "incl")
    for I in range(CH // SUBH):
        s0 = I * SUBH
        bI, qI, kI = b[s0:s0 + SUBH], q[s0:s0 + SUBH], k[s0:s0 + SUBH]
        dec = jnp.exp(jnp.minimum(bI[:, None, :] - bI[None, :, :], 0.0))
        pii = jnp.sum(qI[:, None, :] * kI[None, :, :] * dec, axis=-1)
        p_ref[s0:s0 + SUBH, s0:s0 + SUBH] = jnp.where(m16, pii, 0.0)
        if I > 0:
            rI = b[s0 - 1:s0]
            qs = qI * jnp.exp(bI - rI)
            ks = k[0:s0] * jnp.exp(rI - b[0:s0])
            p_ref[s0:s0 + SUBH, 0:s0] = _dot(qs, ks, NT)


HPS = 8
HPS_HGRN2_FWD = 4
HB = 4
NHB = NH // HPS
OFF = FRONT // RT


def _head_specs(hps, rev=None):
    row = (lambda i: i) if rev is None else rev
    col = lambda g: pl.BlockSpec((RT, hps * DH), lambda h, i: (row(i), g * (NH // hps) + h))
    full = pl.BlockSpec((RT, hps * DH), lambda h, i: (row(i), h))
    real = pl.BlockSpec((RT, hps * DH), lambda h, i: (jnp.maximum(row(i) - OFF, 0), h))
    state = lambda cpt: pl.BlockSpec((hps, cpt, DH, DH), lambda h, i: (h, row(i), 0, 0))
    scal = pl.BlockSpec((hps, RT, DH), lambda h, i: (h, row(i), 0))
    return col, full, real, state, scal


def _hgrn2_fwd(proj, lb_logits, nw):
    tp = proj.shape[0]
    nt, cpt = tp // RT, RT // CH
    hps = HPS_HGRN2_FWD

    def body(q_ref, f_ref, i_ref, g_ref, lbl_ref, nw_ref, og_ref, or_ref, st_ref, pm_ref, s_ref, p_ref):
        @pl.when(pl.program_id(1) == 0)
        def _():
            s_ref[...] = jnp.zeros((hps, DH, DH), F32)

        def chunk(c, carry):
            rows = pl.ds(pl.multiple_of(c * CH, CH), CH)
            for hh in range(hps):
                cols = slice(hh * DH, (hh + 1) * DH)
                _, _, _, w, k = _hg_gates(f_ref[rows, cols], lbl_ref[:, cols])
                q, v = _silu(q_ref[rows, cols]), i_ref[rows, cols]
                b = _cumsum(w)
                st = s_ref[hh]
                st_ref[hh, c] = st
                _decay_blocks(q, k, b, p_ref.at[hh])
                pm_ref[hh, c] = p_ref[hh]
                o = _dot(q * jnp.exp(b), st, NT) + _dot(p_ref[hh], v)
                bl = b[CH - 1:CH]
                s_ref[hh] = st * jnp.exp(bl) + _dot(v, k * jnp.exp(bl - b), TN)
                or_ref[rows, cols] = o
                og_ref[rows, cols] = _gated_norm_fwd(o, g_ref[rows, cols], nw_ref[...]).astype(BF16)
            return carry

        lax.fori_loop(0, cpt, chunk, 0)

    col, full, real, state, _ = _head_specs(hps)
    return _call(body, name="hgrn2_fwd", grid=(NH // hps, nt),
                 in_specs=[col(0), col(1), col(2), col(3), pl.BlockSpec((2, hps * DH), lambda h, i: (0, h)),
                           pl.BlockSpec((1, DH), lambda h, i: (0, 0))],
                 out_specs=[real, full, state(cpt), pl.BlockSpec((hps, cpt, CH, CH), lambda h, i: (h, i, 0, 0))],
                 out_shape=[_sds((tp - FRONT, HW), BF16), _sds((tp, HW), F32), _sds((NH, tp // CH, DH, DH), F32),
                            _sds((NH, tp // CH, CH, CH), F32)],
                 scratch=[pltpu.VMEM((hps, DH, DH), F32), pltpu.VMEM((hps, CH, CH), F32)],
                 sem=("parallel", "arbitrary"))(proj, proj, proj, proj, lb_logits, nw)


def _hgrn2_bwd(proj, lb_logits, nw, o_raw, states, pmat, dog, side=None):
    tp = proj.shape[0]
    nt, cpt = tp // RT, RT // CH

    def body(q_ref, f_ref, i_ref, g_ref, lbl_ref, nw_ref, or_ref, st_ref, pm_ref, dog_ref,
             dq_ref, df_ref, di_ref, dg_ref, dl_ref, dnw_ref, ds_ref, dk_ref, dqa_ref, do_ref):
        step = pl.program_id(1)

        @pl.when(step == 0)
        def _():
            ds_ref[...] = jnp.zeros((HPS, DH, DH), F32)
            dl_ref[...] = jnp.zeros((2, HPS * DH), F32)

        @pl.when((step == 0) & (pl.program_id(0) == 0))
        def _():
            dnw_ref[...] = jnp.zeros((1, DH), F32)

        front = nt - 1 - step < OFF
        for hh in range(HPS):
            cols = slice(hh * DH, (hh + 1) * DH)
            dog_t = jnp.where(front, 0.0, dog_ref[:, cols])
            do_t, dg_t, dnw = _gated_norm_bwd(or_ref[:, cols], g_ref[:, cols], nw_ref[...], dog_t)
            do_ref[:, cols] = do_t
            dg_ref[:, cols] = dg_t.astype(BF16)
            dnw_ref[...] += dnw
        tril = _tri(CH, "incl")
        m16 = _tri(SUBH, "incl")

        def chunk(cc, carry):
            c = cpt - 1 - cc
            rows = pl.ds(pl.multiple_of(c * CH, CH), CH)
            for hh in range(HPS):
                cols = slice(hh * DH, (hh + 1) * DH)
                fs = f_ref[rows, cols]
                lb, sig, f, w, k = _hg_gates(fs, lbl_ref[:, cols])
                hq = q_ref[rows, cols]
                q, v, do = _silu(hq), i_ref[rows, cols], do_ref[rows, cols]
                b = _cumsum(w)
                bl = b[CH - 1:CH]
                eb = jnp.exp(b)
                qs, kd = q * eb, k * jnp.exp(bl - b)
                st, dst = st_ref[hh, c], ds_ref[hh]
                dv = _dot(pm_ref[hh, c], do, TN) + _dot(kd, dst, NT)
                dp = jnp.where(tril, _dot(do, v, NT), 0.0)
                dqa, dka = dqa_ref.at[hh], dk_ref.at[hh]
                dqa[...] = eb * _dotx(do, st)
                dka[...] = jnp.exp(bl - b) * _dotx(v, dst)
                for I in range(CH // SUBH):
                    s0 = I * SUBH
                    bI, qI, kI = b[s0:s0 + SUBH], q[s0:s0 + SUBH], k[s0:s0 + SUBH]
                    dec = jnp.exp(jnp.minimum(bI[:, None, :] - bI[None, :, :], 0.0))
                    dpii = jnp.where(m16, dp[s0:s0 + SUBH, s0:s0 + SUBH], 0.0)[:, :, None] * dec
                    dqa[s0:s0 + SUBH, :] += jnp.sum(dpii * kI[None, :, :], axis=1)
                    dka[s0:s0 + SUBH, :] += jnp.sum(dpii * qI[:, None, :], axis=0)
                    if I > 0:
                        rI = b[s0 - 1:s0]
                        eq, ek = jnp.exp(bI - rI), jnp.exp(rI - b[0:s0])
                        dpij = dp[s0:s0 + SUBH, 0:s0]
                        dqa[s0:s0 + SUBH, :] += eq * _dotx(dpij, k[0:s0] * ek)
                        dka[0:s0, :] += ek * _dotx(dpij, qI * eq, TN)
                dq, dk = dqa[...], dka[...]
                st_end = st * jnp.exp(bl) + _dotx(v, kd, TN)
                dw = _rcumsum(q * dq - k * dk) + jnp.sum(dst * st_end, axis=0, keepdims=True)
                ds_ref[hh] = dst * jnp.exp(bl) + _dotx(do, qs, TN)
                one_m = 1.0 - sig
                dq_ref[rows, cols] = (dq * _dsilu(hq)).astype(BF16)
                df_ref[rows, cols] = ((dw / f - dk) * (1.0 - lb) * sig * one_m).astype(BF16)
                di_ref[rows, cols] = dv.astype(BF16)
                dl_ref[0:1, cols] += jnp.sum((dw / f - dk) * one_m, axis=0, keepdims=True)
            return carry

        lax.fori_loop(0, cpt, chunk, 0)

        @pl.when(step == nt - 1)
        def _():
            lbl = lbl_ref[...]
            l0, l1 = lbl[0:1, :], lbl[1:2, :]
            m = jnp.maximum(l0, l1)
            e0, e1 = jnp.exp(l0 - m), jnp.exp(l1 - m)
            p0 = e0 / (e0 + e1)
            dl0 = dl_ref[0:1, :] * p0 * (1.0 - p0)
            dl_ref[0:1, :] = dl0
            dl_ref[1:2, :] = -dl0

    col, full, real, state, _ = _head_specs(HPS, lambda i: nt - 1 - i)
    lbs = pl.BlockSpec((2, HPS * DH), lambda h, i: (0, h))
    return _call(body, name="hgrn2_bwd", grid=(NHB, nt),
                 in_specs=[col(0), col(1), col(2), col(3), lbs, pl.BlockSpec((1, DH), lambda h, i: (0, 0)), full,
                           state(cpt), pl.BlockSpec((HPS, cpt, CH, CH), lambda h, i: (h, nt - 1 - i, 0, 0)), real],
                 out_specs=[full, full, full, full, lbs, pl.BlockSpec((1, DH), lambda h, i: (0, 0))],
                 out_shape=[_sds((tp, HW), BF16)] * 4 + [_sds((2, HW), F32), _sds((1, DH), F32)],
                 scratch=[pltpu.VMEM((HPS, DH, DH), F32), pltpu.VMEM((HPS, CH, DH), F32),
                          pltpu.VMEM((HPS, CH, DH), F32), pltpu.VMEM((RT, HPS * DH), F32)],
                 sem=("arbitrary", "arbitrary"), side=side)(proj, proj, proj, proj, lb_logits, nw, o_raw, states, pmat, dog)


GQ0 = 4 * HW
CW = 3 * HW


def _gd_scalars(ab, alog, dtb):
    g = -jnp.exp(alog) * jax.nn.softplus(ab + dtb)
    return g, _sigmoid(ab)


def _conv_ext_specs(row_of):
    main = [pl.BlockSpec((RT, HW), lambda i, g=g: (row_of(i), GQ0 // HW + g)) for g in range(3)]
    prev = [pl.BlockSpec((8, HW), lambda i, g=g: (jnp.maximum(row_of(i) * (RT // 8) - 1, 0), GQ0 // HW + g)) for g in range(3)]
    return main + prev


def _conv_fill(ext_ref, xs, xps, first):
    for g in range(3):
        ext_ref[0:8, g * HW:(g + 1) * HW] = jnp.where(first, 0.0, xps[g][...])
        ext_ref[8:8 + RT, g * HW:(g + 1) * HW] = xs[g][...]


def _conv_apply(ext_ref, cw):
    y = cw[CONV_K - 1:CONV_K, :] * ext_ref[pl.ds(8, RT), :]
    for s in range(1, CONV_K):
        y += cw[CONV_K - 1 - s:CONV_K - s, :] * ext_ref[pl.ds(8 - s, RT), :]
    return y


def _gdn_prep_fwd(proj, pab, conv_w, alog, dtb):
    tp = proj.shape[0]
    nt = tp // RT

    def body(x0, x1, x2, p0, p1, p2, ab_ref, cw_ref, al_ref, dt_ref, q_ref, k_ref, v_ref, g_ref, b_ref, ext_ref):
        _conv_fill(ext_ref, (x0, x1, x2), (p0, p1, p2), pl.program_id(0) == 0)
        a = _silu(_conv_apply(ext_ref, cw_ref[...]))
        for h in range(NH):
            for part, ref, sc in ((0, q_ref, DH ** -0.5), (1, k_ref, 1.0)):
                seg = a[:, part * HW + h * DH:part * HW + (h + 1) * DH]
                ref[:, h * DH:(h + 1) * DH] = seg * (lax.rsqrt(jnp.sum(seg * seg, axis=-1, keepdims=True) + EPS) * sc)
        v_ref[...] = a[:, 2 * HW:3 * HW]
        g, beta = _gd_scalars(ab_ref[...], al_ref[...], dt_ref[...])
        for h in range(NH):
            g_ref[h] = jnp.broadcast_to(g[:, h:h + 1], (RT, DH))
            b_ref[h] = jnp.broadcast_to(beta[:, NH + h:NH + h + 1], (RT, DH))

    hs = pl.BlockSpec((RT, HW), lambda i: (i, 0))
    sc = pl.BlockSpec((NH, RT, DH), lambda i: (0, i, 0))
    one = pl.BlockSpec((1, DH), lambda i: (0, 0))
    return _call(body, name="gdn_prep_fwd", grid=(nt,),
                 in_specs=_conv_ext_specs(lambda i: i) + [pl.BlockSpec((RT, DH), lambda i: (i, 0)),
                                                           pl.BlockSpec((CONV_K, CW), lambda i: (0, 0)), one, one],
                 out_specs=[hs, hs, hs, sc, sc],
                 out_shape=[_sds((tp, HW), F32)] * 3 + [_sds((NH, tp, DH), F32)] * 2,
                 scratch=[pltpu.VMEM((RT + 8, CW), F32)], sem=("parallel",))(*([proj] * 6), pab, conv_w, alog, dtb)


def _gdn_prep_bwd(proj, pab, conv_w, alog, dtb, dq, dk, dv, dgb, dbb):
    tp = proj.shape[0]
    nt = tp // RT

    def body(x0, x1, x2, p0, p1, p2, ab_ref, cw_ref, al_ref, dt_ref, dq_ref, dk_ref, dv_ref, dg_ref, db_ref,
             dx_ref, dab_ref, dcw_ref, dal_ref, ddt_ref, ext_ref, dy_ref):
        step = pl.program_id(0)
        i = nt - 1 - step

        @pl.when(step == 0)
        def _():
            dy_ref[RT:RT + 8, :] = jnp.zeros((8, CW), F32)
            dcw_ref[...] = jnp.zeros((8, CW), F32)
            dal_ref[...] = jnp.zeros((1, DH), F32)
            ddt_ref[...] = jnp.zeros((1, DH), F32)

        _conv_fill(ext_ref, (x0, x1, x2), (p0, p1, p2), i == 0)
        cw = cw_ref[...]
        y = _conv_apply(ext_ref, cw)
        sig = _sigmoid(y)
        a = y * sig
        dsl = sig * (1.0 + y * (1.0 - sig))
        for h in range(NH):
            for part, ref, sc in ((0, dq_ref, DH ** -0.5), (1, dk_ref, 1.0)):
                lo = part * HW + h * DH
                seg = a[:, lo:lo + DH]
                r = lax.rsqrt(jnp.sum(seg * seg, axis=-1, keepdims=True) + EPS)
                xh = seg * r
                dxh = ref[:, h * DH:(h + 1) * DH] * sc
                dy_ref[0:RT, lo:lo + DH] = r * (dxh - xh * jnp.sum(dxh * xh, axis=-1, keepdims=True)) * dsl[:, lo:lo + DH]
        dy_ref[0:RT, 2 * HW:3 * HW] = dv_ref[...] * dsl[:, 2 * HW:3 * HW]
        dy = dy_ref[0:RT, :]
        dx = cw[CONV_K - 1:CONV_K, :] * dy
        dcw_ref[CONV_K - 1:CONV_K, :] += jnp.sum(dy * ext_ref[pl.ds(8, RT), :], axis=0, keepdims=True)
        for s in range(1, CONV_K):
            dx += cw[CONV_K - 1 - s:CONV_K - s, :] * dy_ref[pl.ds(s, RT), :]
            dcw_ref[CONV_K - 1 - s:CONV_K - s, :] += jnp.sum(dy * ext_ref[pl.ds(8 - s, RT), :], axis=0, keepdims=True)
        dx_ref[...] = dx.astype(BF16)
        dy_ref[RT:RT + 8, :] = dy[0:8, :]
        ab = ab_ref[...]
        g, beta = _gd_scalars(ab, al_ref[...], dt_ref[...])
        lane = lax.broadcasted_iota(jnp.int32, (RT, DH), 1)
        dgl = jnp.zeros((RT, DH), F32)
        dbl = jnp.zeros((RT, DH), F32)
        for h in range(NH):
            dgl = jnp.where(lane == h, dg_ref[h], dgl)
            dbl = jnp.where(lane == NH + h, db_ref[h], dbl)
        dsp = dgl * (-jnp.exp(al_ref[...])) * _sigmoid(ab + dt_ref[...])
        dab_ref[...] = (dsp + dbl * beta * (1.0 - beta)).astype(BF16)
        ddt_ref[...] += jnp.sum(dsp, axis=0, keepdims=True)
        dal_ref[...] += jnp.sum(dgl * g, axis=0, keepdims=True)

    hs = pl.BlockSpec((RT, HW), lambda s: (nt - 1 - s, 0))
    sc = pl.BlockSpec((NH, RT, DH), lambda s: (0, nt - 1 - s, 0))
    one = pl.BlockSpec((1, DH), lambda s: (0, 0))
    xs = pl.BlockSpec((RT, CW), lambda s: (nt - 1 - s, 0))
    return _call(body, name="gdn_prep_bwd", grid=(nt,),
                 in_specs=_conv_ext_specs(lambda s: nt - 1 - s) + [
                     pl.BlockSpec((RT, DH), lambda s: (nt - 1 - s, 0)), pl.BlockSpec((CONV_K, CW), lambda s: (0, 0)),
                     one, one, hs, hs, hs, sc, sc],
                 out_specs=[xs, pl.BlockSpec((RT, DH), lambda s: (nt - 1 - s, 0)), pl.BlockSpec((8, CW), lambda s: (0, 0)), one, one],
                 out_shape=[_sds((tp, CW), BF16), _sds((tp, DH), BF16), _sds((8, CW), F32), _sds((1, DH), F32), _sds((1, DH), F32)],
                 scratch=[pltpu.VMEM((RT + 8, CW), F32), pltpu.VMEM((RT + 8, CW), F32)],
                 sem=("arbitrary",))(*([proj] * 6), pab, conv_w, alog, dtb, dq, dk, dv, dgb, dbb)


HS = HB * CH


def _stack_heads(ref, rows, base):
    return jnp.concatenate([ref[rows, (base + hh) * DH:(base + hh + 1) * DH] for hh in range(HB)], axis=0)


def _stack_scal(ref, rows, base):
    return jnp.concatenate([ref[base + hh, rows, :] for hh in range(HB)], axis=0)


def _store_heads(ref, rows, base, val):
    for hh in range(HB):
        ref[rows, (base + hh) * DH:(base + hh + 1) * DH] = val[hh * CH:(hh + 1) * CH].astype(ref.dtype)


def _bd_masks():
    r = lax.broadcasted_iota(jnp.int32, (HS, HS), 0)
    c = lax.broadcasted_iota(jnp.int32, (HS, HS), 1)
    same = lax.shift_right_logical(r, int(math.log2(CH))) == lax.shift_right_logical(c, int(math.log2(CH)))
    return same & (r >= c), same & (r > c)


def _unit_lower_inverse(a):
    n = a.shape[0]
    r = lax.broadcasted_iota(jnp.int32, (n, n), 0)
    c = lax.broadcasted_iota(jnp.int32, (n, n), 1)
    blk_of = lambda t, size: lax.shift_right_logical(t, int(math.log2(size)))
    a16 = jnp.where(blk_of(r, SUB) == blk_of(c, SUB), a, 0.0)
    x = (r == c).astype(F32) - a16
    p = a16
    for _ in range(3):
        p = _dot(p, p)
        x = x + _dot(x, p)
    for blk in (2 * SUB, 4 * SUB):
        off = jnp.where((blk_of(r, blk) == blk_of(c, blk)) & (blk_of(r, blk // 2) != blk_of(c, blk // 2)), a, 0.0)
        x = x - _dot(x, _dot(off, x))
    return x


def _gdn_chunk_common(q, k, v, gl, bt, incl, strict, x=None):
    gc = _cumsum(gl)
    e = jnp.exp(gc)
    rel = jnp.exp(jnp.minimum(gc[:, 0:1] - gc.T[0:1, :], 0.0))
    kb = bt * k
    a = jnp.where(strict, bt[:, 0:1] * _dot(k, k, NT) * rel, 0.0)
    if x is None:
        x = _unit_lower_inverse(a)
    wu = _dot(x, jnp.concatenate([kb * e, bt * v], axis=1))
    attn = jnp.where(incl, _dot(q, k, NT) * rel, 0.0)
    return gc, e, rel, kb, a, x, wu[:, 0:DH], wu[:, DH:2 * DH], attn


def _gdn_fwd(q, k, v, gb, bb, proj, nw):
    tp = q.shape[0]
    nt, cpt = tp // RT, RT // CH

    def body(q_ref, k_ref, v_ref, g_ref, b_ref, z_ref, nw_ref, og_ref, or_ref, st_ref, x_ref, s_ref):
        @pl.when(pl.program_id(1) == 0)
        def _():
            s_ref[...] = jnp.zeros((HPS, DH, DH), F32)

        incl, strict = _bd_masks()

        def chunk(c, carry):
            rows = pl.ds(pl.multiple_of(c * CH, CH), CH)
            for base in range(0, HPS, HB):
                qc, kc, vc = _stack_heads(q_ref, rows, base), _stack_heads(k_ref, rows, base), _stack_heads(v_ref, rows, base)
                gc, e, rel, kb, a, x, w, u, attn = _gdn_chunk_common(qc, kc, vc, _stack_scal(g_ref, rows, base),
                                                                    _stack_scal(b_ref, rows, base), incl, strict)
                x_ref[base // HB, c] = x
                qe = qc * e
                ws, qs = [], []
                for hh in range(HB):
                    blk = slice(hh * CH, (hh + 1) * CH)
                    s = s_ref[base + hh]
                    st_ref[base + hh, c] = s
                    both = _dot(jnp.concatenate([w[blk], qe[blk]], axis=0), s)
                    ws.append(both[0:CH])
                    qs.append(both[CH:2 * CH])
                vn = u - jnp.concatenate(ws, axis=0)
                o = jnp.concatenate(qs, axis=0) + _dot(attn, vn)
                for hh in range(HB):
                    blk = slice(hh * CH, (hh + 1) * CH)
                    gl = gc[(hh + 1) * CH - 1:(hh + 1) * CH]
                    s_ref[base + hh] = s_ref[base + hh] * jnp.exp(gl) + _dot(kc[blk] * jnp.exp(gl - gc[blk]), vn[blk], TN)
                _store_heads(or_ref, rows, base, o)
                for hh in range(HB):
                    cols = slice((base + hh) * DH, (base + hh + 1) * DH)
                    og_ref[rows, cols] = _gated_norm_fwd(o[hh * CH:(hh + 1) * CH], z_ref[rows, cols], nw_ref[...]).astype(BF16)
            return carry

        lax.fori_loop(0, cpt, chunk, 0)

    col, full, real, state, scal = _head_specs(HPS)
    return _call(body, name="gdn_fwd", grid=(NHB, nt),
                 in_specs=[full, full, full, scal, scal, col(7), pl.BlockSpec((1, DH), lambda h, i: (0, 0))],
                 out_specs=[real, full, state(cpt), pl.BlockSpec((HPS // HB, cpt, HS, HS), lambda h, i: (h, i, 0, 0))],
                 out_shape=[_sds((tp - FRONT, HW), BF16), _sds((tp, HW), F32), _sds((NH, tp // CH, DH, DH), F32),
                            _sds((NH // HB, tp // CH, HS, HS), F32)],
                 scratch=[pltpu.VMEM((HPS, DH, DH), F32)], sem=("parallel", "arbitrary"))(q, k, v, gb, bb, proj, nw)


def _gdn_bwd(q, k, v, gb, bb, proj, nw, o_raw, states, xinv, dog):
    tp = q.shape[0]
    nt, cpt = tp // RT, RT // CH

    def body(q_ref, k_ref, v_ref, g_ref, b_ref, z_ref, nw_ref, or_ref, st_ref, x_ref, dog_ref,
             dq_ref, dk_ref, dv_ref, dg_ref, db_ref, dz_ref, dnw_ref, ds_ref, do_ref):
        step = pl.program_id(1)

        @pl.when(step == 0)
        def _():
            ds_ref[...] = jnp.zeros((HPS, DH, DH), F32)

        @pl.when((step == 0) & (pl.program_id(0) == 0))
        def _():
            dnw_ref[...] = jnp.zeros((1, DH), F32)

        front = nt - 1 - step < OFF
        for hh in range(HPS):
            cols = slice(hh * DH, (hh + 1) * DH)
            dog_t = jnp.where(front, 0.0, dog_ref[:, cols])
            do_t, dz_t, dnw = _gated_norm_bwd(or_ref[:, cols], z_ref[:, cols], nw_ref[...], dog_t)
            do_ref[:, cols] = do_t
            dz_ref[:, cols] = dz_t.astype(BF16)
            dnw_ref[...] += dnw
        incl, strict = _bd_masks()
        last_row = (lax.broadcasted_iota(jnp.int32, (CH, 1), 0) == CH - 1)

        def rsum(t):
            return jnp.sum(t, axis=-1, keepdims=True)

        def chunk(cc, carry):
            c = cpt - 1 - cc
            rows = pl.ds(pl.multiple_of(c * CH, CH), CH)
            for base in range(0, HPS, HB):
                qc, kc, vc, do = (_stack_heads(q_ref, rows, base), _stack_heads(k_ref, rows, base), _stack_heads(v_ref, rows, base),
                                  _stack_heads(do_ref, rows, base))
                bt = _stack_scal(b_ref, rows, base)
                gc, e, rel, kb, a, x, w, u, attn = _gdn_chunk_common(qc, kc, vc, _stack_scal(g_ref, rows, base), bt, incl, strict,
                                                                    x=x_ref[base // HB, c])
                qe = qc * e
                heads = [slice(hh * CH, (hh + 1) * CH) for hh in range(HB)]
                gls = [gc[(hh + 1) * CH - 1:(hh + 1) * CH] for hh in range(HB)]
                cdec = jnp.concatenate([jnp.exp(gl - gc[blk]) for gl, blk in zip(gls, heads)], axis=0)
                kcd = kc * cdec
                vn = u - jnp.concatenate([_dot(w[blk], st_ref[base + hh, c]) for hh, blk in enumerate(heads)], axis=0)
                dos = jnp.concatenate([_dot(do[blk], st_ref[base + hh, c], NT) for hh, blk in enumerate(heads)], axis=0)
                kds = jnp.concatenate([_dot(kcd[blk], ds_ref[base + hh]) for hh, blk in enumerate(heads)], axis=0)
                vds = jnp.concatenate([_dot(vn[blk], ds_ref[base + hh], NT) for hh, blk in enumerate(heads)], axis=0)
                dvn = _dot(attn, do, TN) + kds
                dattn = jnp.where(incl, _dot(do, vn, NT), 0.0)
                dar = dattn * rel
                dq = _dot(dar, kc) + e * dos
                dk = _dot(dar, qc, TN) + cdec * vds
                dc = cdec[:, 0:1] * rsum(kc * vds)
                dgc = rsum(qe * dos) - dc
                dw = -jnp.concatenate([_dot(dvn[blk], st_ref[base + hh, c], NT) for hh, blk in enumerate(heads)], axis=0)
                extra = []
                for hh, blk in enumerate(heads):
                    s, dsn = st_ref[base + hh, c], ds_ref[base + hh]
                    el = jnp.exp(gls[hh])
                    dglast = jnp.sum(dc[blk], axis=0, keepdims=True) + el[:, 0:1] * jnp.sum(rsum(dsn * s), axis=0, keepdims=True)
                    extra.append(jnp.where(last_row, dglast, 0.0))
                    ds_ref[base + hh] = dsn * el + _dot(jnp.concatenate([qe[blk], -w[blk]], axis=0),
                                                 jnp.concatenate([do[blk], dvn[blk]], axis=0), TN)
                dr = _dot(x, jnp.concatenate([dw, dvn], axis=1), TN)
                drw, dru = dr[:, 0:DH], dr[:, DH:2 * DH]
                da = -jnp.where(strict, _dot(dr, jnp.concatenate([w, u], axis=1), NT), 0.0)
                dar2 = da * rel
                dkb = _dot(dar2, kc)
                rwk = rsum(drw * kc)
                dk = dk + _dot(dar2, kb, TN) + bt * dkb + (bt * e) * drw
                dbeta = rsum(dkb * kc) + e[:, 0:1] * rwk + rsum(dru * vc)
                z = dattn * attn + da * a
                dgc = dgc + bt[:, 0:1] * e[:, 0:1] * rwk + rsum(z) - rsum(z.T) + jnp.concatenate(extra, axis=0)
                _store_heads(dq_ref, rows, base, dq)
                _store_heads(dk_ref, rows, base, dk)
                _store_heads(dv_ref, rows, base, bt * dru)
                dg = _rcumsum(jnp.broadcast_to(dgc, (HS, DH)))
                dbb = jnp.broadcast_to(dbeta, (HS, DH))
                for hh, blk in enumerate(heads):
                    dg_ref[base + hh, rows, :] = dg[blk]
                    db_ref[base + hh, rows, :] = dbb[blk]
            return carry

        lax.fori_loop(0, cpt, chunk, 0)

    col, full, real, state, scal = _head_specs(HPS, lambda i: nt - 1 - i)
    one = pl.BlockSpec((1, DH), lambda h, i: (0, 0))
    return _call(body, name="gdn_bwd", grid=(NHB, nt),
                 in_specs=[full, full, full, scal, scal, col(7), one, full, state(cpt),
                           pl.BlockSpec((HPS // HB, cpt, HS, HS), lambda h, i: (h, nt - 1 - i, 0, 0)), real],
                 out_specs=[full, full, full, scal, scal, full, one],
                 out_shape=[_sds((tp, HW), F32)] * 3 + [_sds((NH, tp, DH), F32)] * 2 + [_sds((tp, HW), BF16), _sds((1, DH), F32)],
                 scratch=[pltpu.VMEM((HPS, DH, DH), F32), pltpu.VMEM((RT, HPS * DH), F32)],
                 sem=("arbitrary", "arbitrary"))(q, k, v, gb, bb, proj, nw, o_raw, states, xinv, dog)


MAIN_W = 8 * HW
AB_W = 2 * NH


def _w_in_pieces(lo, hi):
    total = MAIN_W + AB_W
    out = []
    if lo < MAIN_W:
        out.append(("m", lo, min(hi, MAIN_W)))
    if hi > MAIN_W and lo < total:
        out.append(("ab", max(lo, MAIN_W) - MAIN_W, min(hi, total) - MAIN_W))
    if hi > total:
        out.append(("m", max(lo, total) - AB_W, hi - AB_W))
    return out


W_IN_TILE = 128


def _unshard_w_in(slabs):
    ns, d, cs = slabs.shape
    main_w = ns * cs - AB_W

    def body(w_ref, m_ref, ab_ref):
        ab_ref[...] = jnp.zeros((W_IN_TILE, DH), slabs.dtype)
        for s_ in range(ns):
            off = 0
            for src, a, b in _w_in_pieces(s_ * cs, (s_ + 1) * cs):
                piece = w_ref[s_, :, off:off + (b - a)]
                (m_ref if src == "m" else ab_ref)[:, a:b] = piece
                off += b - a

    return _call(body, name="unshard_w_in", grid=(d // W_IN_TILE,),
                 in_specs=[pl.BlockSpec((ns, W_IN_TILE, cs), lambda i: (0, i, 0))],
                 out_specs=[pl.BlockSpec((W_IN_TILE, main_w), lambda i: (i, 0)), pl.BlockSpec((W_IN_TILE, DH), lambda i: (i, 0))],
                 out_shape=[_sds((d, main_w), slabs.dtype), _sds((d, DH), slabs.dtype)], sem=("parallel",))(slabs)


def _shard_w_in_grad(dw_main, dw_ab, ns=4):
    d, main_w = dw_main.shape
    cs = (main_w + AB_W) // ns

    def body(m_ref, ab_ref, o_ref):
        for s_ in range(ns):
            parts = [(m_ref if src == "m" else ab_ref)[:, a:b] for src, a, b in _w_in_pieces(s_ * cs, (s_ + 1) * cs)]
            o_ref[s_] = parts[0] if len(parts) == 1 else jnp.concatenate(parts, axis=1)

    return _call(body, name="shard_w_in_grad", grid=(d // W_IN_TILE,),
                 in_specs=[pl.BlockSpec((W_IN_TILE, main_w), lambda i: (i, 0)), pl.BlockSpec((W_IN_TILE, DH), lambda i: (i, 0))],
                 out_specs=pl.BlockSpec((ns, W_IN_TILE, cs), lambda i: (0, i, 0)),
                 out_shape=_sds((ns, d, cs), dw_main.dtype), sem=("parallel",))(dw_main, dw_ab)


def _pad_lanes(v):
    return jnp.pad(v, ((0, 0), (0, DH - v.shape[1])))


class _NoComm:
    def __init__(self, w_in, late):
        self.w_in, self.late = w_in, late

    def first_side(self):
        return None

    def first_weight(self, side_outs):
        return self.w_in

    def proj_side(self):
        return None

    def late_weights(self, side_outs):
        return self.late

    def early_grads_side(self, grads):
        return None

    def early_grads_done(self, side_outs):
        pass

    def last_grad_side(self, dw_in):
        return None

    def last_grad_done(self, side_outs):
        pass


def _local_step(x, tgt, meta, lb_logits, mix_w, hg_nw, conv_w, a_log, dt_bias, gd_nw, ffn_nw, final_w, comm):
    alog, dtb = _pad_lanes(a_log), _pad_lanes(dt_bias)
    final_w = final_w.reshape(1, -1)
    rows4 = lambda t: t.reshape(4, t.shape[0] // 4, t.shape[1])
    side = comm.first_side()
    xn = _rms1_fwd(x, meta, mix_w, side=side)
    xn, landed = xn if side is not None else (xn, None)
    w_main, w_ab = _unshard_w_in(comm.first_weight(landed))
    side = comm.proj_side()
    proj = _mm(xn, w_main, "nn", F32, 768, 2048, 2048, "proj_main", n_outer=True, side=side)
    proj, landed = proj if side is not None else (proj, None)
    w_a, w_b, w_out, w_ffn_in, w_ffn_out = comm.late_weights(landed)
    pab = _mm(xn, w_ab, "nn", F32, 768, 128, 2048, "proj_ab")
    oa_g, oa_raw, st_a, pm_a = _hgrn2_fwd(proj, lb_logits, hg_nw)
    q, k, v, gb, bb = _gdn_prep_fwd(proj, pab, conv_w, alog, dtb)
    ob_g, ob_raw, st_b, xinv = _gdn_fwd(q, k, v, gb, bb, proj, gd_nw)
    za = _mm(oa_g, w_a, "nn", F32, 1024, 512, 1024, "branch_a", n_outer=True)
    zb = _mm(ob_g, w_b, "nn", F32, 1024, 512, 1024, "branch_b", n_outer=True)
    merged = _merge_fwd(proj, za, zb)
    def resid_norm(mix, xr, wr):
        h = xr + mix
        return h, h * lax.rsqrt(jnp.mean(h * h, axis=-1, keepdims=True) + EPS) * wr

    h1, n2 = _mm(merged, w_out, "nn", F32, 512, 2048, 2048, "mix_out_norm", epilogue=(resid_norm, [x, ffn_nw], [F32, BF16]))
    gu = _mm(n2, w_ffn_in, "nn", BF16, 1024, 1408, 2048, "ffn_in", n_outer=True)
    act = _swiglu_fwd(gu)
    f = _mm(act, w_ffn_out, "nn", F32, 1024, 2048, 1408, "ffn_out")
    lt, dh2, dh2b, dfinal = _loss_head(h1, f, final_w, tgt)
    loss = _sum_tiles(lt)
    dact = _mm(dh2b, w_ffn_out, "nt", BF16, 1024, 1408, 2048, "d_act", n_outer=True)
    dw_ffn_out = rows4(_mm(act, dh2b, "tn", F32, 512, 2048, 2048, "dw_ffn_out"))
    dgu = _swiglu_bwd(gu, dact)
    dn2 = _mm(dgu, w_ffn_in, "nt", F32, 1024, 2048, 1408, "d_n2")
    dw_ffn_in = _mm(n2, dgu, "tn", F32, 1024, 1408, 2048, "dw_ffn_in", out_shards=4)
    dh1, dh1b, dffn_nw = _resid_norm_bwd(h1, ffn_nw, dn2, dh2)
    dmerged = _mm(dh1b, w_out, "nt", F32, 1024, 2048, 2048, "d_merged")
    dw_out = _mm(merged, dh1b, "tn", F32, 2048, 1024, 1024, "dw_out")
    dza, dzb, dgate = _merge_bwd(proj, za, zb, dmerged)
    doa = _mm(dza, w_a, "nt", F32, 1024, 1024, 512, "d_oa")
    dob = _mm(dzb, w_b, "nt", F32, 1024, 1024, 512, "d_ob")
    dw_a = _mm(oa_g, dza, "tn", F32, 1024, 512, 1024, "dw_branch_a", out_shards=4)
    dw_b = _mm(ob_g, dzb, "tn", F32, 1024, 512, 1024, "dw_branch_b", out_shards=4)
    early = dict(w_ffn_in=dw_ffn_in, w_ffn_out=dw_ffn_out, w_out=rows4(dw_out), w_branch_a=dw_a, w_branch_b=dw_b)
    side = comm.early_grads_side(early)
    hg = _hgrn2_bwd(proj, lb_logits, hg_nw, oa_raw, st_a, pm_a, doa, side=side)
    if side is not None:
        hg, arrived = hg
        comm.early_grads_done(arrived)
    dhq, dhf, dhi, dhg, dlbl, dhg_nw = hg
    dq, dk, dv, dg, dbeta, dz, dgd_nw = _gdn_bwd(q, k, v, gb, bb, proj, gd_nw, ob_raw, st_b, xinv, dob)
    dx3, dab, dconv, dalog, ddtb = _gdn_prep_bwd(proj, pab, conv_w, alog, dtb, dq, dk, dv, dg, dbeta)
    dproj = jnp.concatenate([dhq, dhf, dhi, dhg, dx3, dz, dgate], axis=1)
    dw_main = _mm(xn, dproj, "tn", F32, 1024, 1024, 2816, "dw_in_main")
    dw_ab = _mm(xn, dab, "tn", F32, 2048, 128, 768, "dw_in_ab")
    dw_in = _shard_w_in_grad(dw_main, dw_ab)
    side = comm.last_grad_side(dw_in)
    dxn = _mm(dproj, w_main, "nt", F32, 768, 2048, 2048, "d_xn", side=side)
    if side is not None:
        dxn, arrived = dxn
        comm.last_grad_done(arrived)
    dxn = _mm(dab, w_ab, "nt", F32, 768, 2048, 128, "d_xn_ab", add=dxn)
    dx, dmeta, dmix_w = _rms1_bwd(x, meta, mix_w, dxn, dh1)
    grads = dict(meta_tokens=dmeta, lb_logits=dlbl, mix_norm_w=dmix_w, w_in=dw_in,
                 hg_norm_w=dhg_nw, gd_conv_w=dconv[:CONV_K], gd_a_log=dalog[:, :NH],
                 gd_dt_bias=ddtb[:, :NH], gd_norm_w=dgd_nw, w_branch_a=dw_a, w_branch_b=dw_b,
                 w_out=rows4(dw_out), ffn_norm_w=dffn_nw, w_ffn_in=dw_ffn_in, w_ffn_out=dw_ffn_out,
                 final_norm_w=dfinal.reshape(-1))
    return loss, dx, grads


def _adamw(g, w, m, v, name):
    rows, cols = g.shape
    tr = rows
    for cand in (128, 64, 32, 16, 8):
        if rows % cand == 0 and rows > cand:
            tr = cand
            break

    def body(g_ref, w_ref, m_ref, v_ref, go_ref, d_ref, nm_ref, nv_ref):
        gg = g_ref[...]
        go_ref[...] = gg
        nm = ADAM_B1 * m_ref[...] + (1.0 - ADAM_B1) * gg
        nv = ADAM_B2 * v_ref[...] + (1.0 - ADAM_B2) * (gg * gg)
        m_hat = nm / (1.0 - ADAM_B1 ** ADAM_STEP)
        v_hat = nv / (1.0 - ADAM_B2 ** ADAM_STEP)
        d_ref[...] = -ADAM_LR * (m_hat / (jnp.sqrt(v_hat) + ADAM_EPS) + ADAM_WD * w_ref[...])
        nm_ref[...] = nm
        nv_ref[...] = nv

    bs = pl.BlockSpec((tr, cols), lambda i: (i, 0))
    return _call(body, name=name, grid=(rows // tr,), in_specs=[bs] * 4, out_specs=[bs] * 4,
                 out_shape=[_sds((rows, cols), F32)] * 4, sem=("parallel",))(g, w, m, v)


HBM = pl.BlockSpec(memory_space=pltpu.HBM)
MESH = pl.DeviceIdType.MESH


def _place():
    x, y, c = lax.axis_index("x"), lax.axis_index("y"), lax.axis_index("c")
    return x, y, c, [(1 - x, y), (x, 1 - y), (1 - x, 1 - y)]


def _comm_call(body, name, out_shape, n_in, scratch):
    return pl.pallas_call(body, name=name, out_shape=out_shape, in_specs=[HBM] * n_in,
                          out_specs=jax.tree.map(lambda _: HBM, out_shape), scratch_shapes=scratch)


def _half_rows(rows, c, tile):
    hh = rows // 2
    assert rows % 2 == 0 and hh % tile == 0, (rows, tile)
    return pl.ds(pl.multiple_of(c * hh, tile), hh)


def _gather_copies(w_refs, out_refs, sems):
    send_sems, recv_sems = sems
    x, y, c, chips = _place()
    s_me = 2 * x + y
    sends, recvs = [], []
    for k, (w_ref, out_ref) in enumerate(zip(w_refs, out_refs)):
        half = _half_rows(w_ref.shape[0], c, 16)
        for j, (cx, cy) in enumerate(chips):
            sem = dict(send_sem=send_sems.at[3 * k + j], recv_sem=recv_sems.at[3 * k + j], device_id=(cx, cy, c), device_id_type=MESH)
            sends.append(pltpu.make_async_remote_copy(src_ref=w_ref.at[half], dst_ref=out_ref.at[s_me, half], **sem))
            recvs.append(pltpu.make_async_remote_copy(src_ref=w_ref.at[half], dst_ref=out_ref.at[2 * cx + cy, half], **sem))
    return sends, recvs


def _gather_sems(n):
    return [pltpu.SemaphoreType.DMA((3 * n,)), pltpu.SemaphoreType.DMA((3 * n,))]


def _gather_start(w_refs, out_refs, sems):
    for cp in _gather_copies(w_refs, out_refs, sems)[0]:
        cp.start()


def _gather_wait(w_refs, out_refs, sems):
    sends, recvs = _gather_copies(w_refs, out_refs, sems)
    for cp in recvs:
        cp.wait_recv()
    for cp in sends:
        cp.wait_send()


def _gather_side(shards):
    return _Side(shards, [_sds((4,) + w.shape, w.dtype) for w in shards], _gather_sems(len(shards)), _gather_start, _gather_wait)


def _forward_halves(outs, name):
    n = len(outs)

    def body(*refs):
        out_refs = refs[n:2 * n]
        send_sems, recv_sems = refs[2 * n:]
        x, y, c, chips = _place()
        cps = []
        for k in range(n):
            rows = out_refs[k].shape[1]
            half, other = _half_rows(rows, c, 16), _half_rows(rows, 1 - c, 16)
            for j, (cx, cy) in enumerate(chips):
                sem = dict(send_sem=send_sems.at[3 * k + j], recv_sem=recv_sems.at[3 * k + j], device_id=(x, y, 1 - c), device_id_type=MESH)
                landed = out_refs[k].at[2 * cx + cy, half]
                cps.append(pltpu.make_async_remote_copy(src_ref=landed, dst_ref=landed, **sem))
                cps[-1].start()
        for k in range(n):
            rows = out_refs[k].shape[1]
            half, other = _half_rows(rows, c, 16), _half_rows(rows, 1 - c, 16)
            for j, (cx, cy) in enumerate(chips):
                sem = dict(send_sem=send_sems.at[3 * k + j], recv_sem=recv_sems.at[3 * k + j], device_id=(x, y, 1 - c), device_id_type=MESH)
                pltpu.make_async_remote_copy(src_ref=out_refs[k].at[2 * cx + cy, half], dst_ref=out_refs[k].at[2 * cx + cy, other], **sem).wait_recv()
        for cp in cps:
            cp.wait_send()

    shapes = [_sds(o.shape, o.dtype) for o in outs]
    return pl.pallas_call(body, name=name, out_shape=shapes, in_specs=[HBM] * n, out_specs=[HBM] * n,
                          input_output_aliases={k: k for k in range(n)},
                          scratch_shapes=[pltpu.SemaphoreType.DMA((3 * n,)), pltpu.SemaphoreType.DMA((3 * n,))])(*outs)


def _swap_halves(gs, name):
    n = len(gs)

    def body(*refs):
        g_refs, out_refs = refs[:n], refs[n:2 * n]
        send_sems, recv_sems = refs[2 * n:]
        x, y, c, _ = _place()
        cps = []
        for k in range(n):
            other = _half_rows(g_refs[k].shape[1], 1 - c, 8)
            cps.append(pltpu.make_async_remote_copy(src_ref=g_refs[k].at[:, other, :], dst_ref=out_refs[k], send_sem=send_sems.at[k],
                                                    recv_sem=recv_sems.at[k], device_id=(x, y, 1 - c), device_id_type=MESH))
            cps[-1].start()
        for cp in cps:
            cp.wait()

    return _comm_call(body, name, [_sds((4, g.shape[1] // 2, g.shape[2]), g.dtype) for g in gs], n,
                      [pltpu.SemaphoreType.DMA((n,)), pltpu.SemaphoreType.DMA((n,))])(*gs)


def _row_tile(rows, row_bytes, budget=3 << 20):
    if rows * row_bytes <= budget:
        return rows
    return max(t for t in range(16, rows, 16) if rows % t == 0 and t * row_bytes <= budget)


def _add_half(g, got, c, name):
    _, rows, cols = g.shape
    hh = rows // 2
    tr = _row_tile(hh, cols * 4)
    nb = hh // tr

    def body(c_ref, a_ref, b_ref, o_ref):
        o_ref[...] = (a_ref[...] + b_ref[...]).astype(BF16)

    gs = pltpu.PrefetchScalarGridSpec(
        num_scalar_prefetch=1, grid=(4, nb),
        in_specs=[pl.BlockSpec((1, tr, cols), lambda s, i, c_ref: (s, c_ref[0] * nb + i, 0)),
                  pl.BlockSpec((1, tr, cols), lambda s, i, c_ref: (s, i, 0))],
        out_specs=pl.BlockSpec((1, tr, cols), lambda s, i, c_ref: (s, i, 0)))
    return pl.pallas_call(body, name=name, grid_spec=gs, out_shape=_sds((4, hh, cols), BF16),
                          compiler_params=pltpu.CompilerParams(dimension_semantics=("parallel", "parallel"),
                                                               vmem_limit_bytes=VMEM_LIMIT))(c, g, got)


def _scatter_copies(p_refs, out_refs, sems):
    send_sems, recv_sems = sems
    x, y, c, chips = _place()
    s_me = 2 * x + y
    cps = []
    for k, (p_ref, out_ref) in enumerate(zip(p_refs, out_refs)):
        for j, (cx, cy) in enumerate(chips):
            cps.append(pltpu.make_async_remote_copy(src_ref=p_ref.at[2 * cx + cy], dst_ref=out_ref.at[s_me],
                                                    send_sem=send_sems.at[3 * k + j], recv_sem=recv_sems.at[3 * k + j],
                                                    device_id=(cx, cy, c), device_id_type=MESH))
    return cps


def _scatter_start(p_refs, out_refs, sems):
    for cp in _scatter_copies(p_refs, out_refs, sems):
        cp.start()


def _scatter_wait(p_refs, out_refs, sems):
    for cp in _scatter_copies(p_refs, out_refs, sems):
        cp.wait()


def _scatter_side(ps):
    return _Side(ps, [_sds(p_.shape, p_.dtype) for p_ in ps], _gather_sems(len(ps)), _scatter_start, _scatter_wait)


def _sum_slabs(b, name):
    n, h, wd = b.shape
    tr = _row_tile(h, n * wd * 4, 6 << 20)

    def body(b_ref, o_ref):
        acc = b_ref[0]
        for s in range(1, n):
            acc = acc + b_ref[s]
        o_ref[...] = acc

    return _call(body, name=name, grid=(h // tr,), in_specs=[pl.BlockSpec((n, tr, wd), lambda i: (0, i, 0))],
                 out_specs=pl.BlockSpec((tr, wd), lambda i: (i, 0)), out_shape=_sds((h, wd), F32), sem=("parallel",))(b)


def _sum_chips(arrived, own, name):
    n, h, wd = arrived.shape
    tr = _row_tile(h, n * wd * 2, 6 << 20)
    nb = h // tr
    my_chip = lambda: 2 * lax.axis_index("x") + lax.axis_index("y")

    def body(b_ref, p_ref, o_ref):
        acc = None
        for s in range(n):
            term = jnp.where(my_chip() == s, p_ref[0], b_ref[s]).astype(F32)
            acc = term if acc is None else acc + term
        o_ref[...] = acc

    return _call(body, name=name, grid=(nb,),
                 in_specs=[pl.BlockSpec((n, tr, wd), lambda i: (0, i, 0)), pl.BlockSpec((1, tr, wd), lambda i: (my_chip(), i, 0))],
                 out_specs=pl.BlockSpec((tr, wd), lambda i: (lax.axis_index("c") * nb + i, 0)),
                 out_shape=_sds((2 * h, wd), F32), sem=("parallel",))(arrived, own)


def _share_halves(gs):
    n = len(gs)

    def body(*refs):
        out_refs = refs[n:2 * n]
        send_sems, recv_sems = refs[2 * n:]
        x, y, c, _ = _place()
        cps = []
        for k in range(n):
            half, other = _half_rows(out_refs[k].shape[0], c, 8), _half_rows(out_refs[k].shape[0], 1 - c, 8)
            sem = dict(send_sem=send_sems.at[k], recv_sem=recv_sems.at[k], device_id=(x, y, 1 - c), device_id_type=MESH)
            cps.append((pltpu.make_async_remote_copy(src_ref=out_refs[k].at[half], dst_ref=out_refs[k].at[half], **sem),
                        pltpu.make_async_remote_copy(src_ref=out_refs[k].at[half], dst_ref=out_refs[k].at[other], **sem)))
            cps[-1][0].start()
        for send, recv in cps:
            recv.wait_recv()
            send.wait_send()

    return pl.pallas_call(body, name="share_halves", out_shape=[_sds(g.shape, g.dtype) for g in gs], in_specs=[HBM] * n,
                          out_specs=[HBM] * n, input_output_aliases={k: k for k in range(n)},
                          scratch_shapes=[pltpu.SemaphoreType.DMA((n,)), pltpu.SemaphoreType.DMA((n,))])(*gs)


def _gather_all(v, name):
    def body(v_ref, out_ref, send_sems, recv_sems, local_sem):
        x, y, c = lax.axis_index("x"), lax.axis_index("y"), lax.axis_index("c")
        me = 4 * x + 2 * y + c
        flip = lambda t, d: 1 - t if d else t
        mine = pltpu.make_async_copy(v_ref, out_ref.at[me], local_sem)
        mine.start()
        cps = []
        for k in range(1, 8):
            to = (flip(x, k & 4), flip(y, k & 2), flip(c, k & 1))
            cps.append(pltpu.make_async_remote_copy(src_ref=v_ref, dst_ref=out_ref.at[me], send_sem=send_sems.at[k - 1],
                                                    recv_sem=recv_sems.at[k - 1], device_id=to, device_id_type=MESH))
        for cp in cps:
            cp.start()
        for cp in cps:
            cp.wait()
        mine.wait()

    return _comm_call(body, name, _sds((8,) + v.shape, v.dtype), 1,
                      [pltpu.SemaphoreType.DMA((7,)), pltpu.SemaphoreType.DMA((7,)), pltpu.SemaphoreType.DMA])(v)


BIG = (("w_in", 1), ("w_branch_a", 1), ("w_branch_b", 1), ("w_out", 0), ("w_ffn_in", 1), ("w_ffn_out", 0))
SMALL = ("meta_tokens", "lb_logits", "mix_norm_w", "hg_norm_w", "gd_conv_w", "gd_a_log", "gd_dt_bias", "gd_norm_w",
         "ffn_norm_w", "final_norm_w")


def _pack_lanes(parts):
    rows = []
    for p in parts:
        f = p.reshape(-1).astype(F32)
        n = -(-f.shape[0] // DH) * DH
        rows.append(jnp.pad(f, (0, n - f.shape[0])).reshape(-1, DH))
    buf = jnp.concatenate(rows, axis=0)
    return jnp.pad(buf, ((0, -buf.shape[0] % 8), (0, 0)))


def _unpack_lanes(buf, shapes):
    out, off = [], 0
    for shp in shapes:
        n = math.prod(shp)
        r = -(-n // DH)
        out.append(buf[off:off + r].reshape(-1)[:n].reshape(shp))
        off += r
    return out


def kernel(x, meta_tokens, lb_logits, mix_norm_w, w_in, hg_norm_w, gd_conv_w, gd_a_log, gd_dt_bias, gd_norm_w, w_branch_a, w_branch_b, w_out, ffn_norm_w, w_ffn_in, w_ffn_out, final_norm_w, loss_target, m_meta_tokens, m_lb_logits, m_mix_norm_w, m_w_in, m_hg_norm_w, m_gd_conv_w, m_gd_a_log, m_gd_dt_bias, m_gd_norm_w, m_w_branch_a, m_w_branch_b, m_w_out, m_ffn_norm_w, m_w_ffn_in, m_w_ffn_out, m_final_norm_w, v_meta_tokens, v_lb_logits, v_mix_norm_w, v_w_in, v_hg_norm_w, v_gd_conv_w, v_gd_a_log, v_gd_dt_bias, v_gd_norm_w, v_w_branch_a, v_w_branch_b, v_w_out, v_ffn_norm_w, v_w_ffn_in, v_w_ffn_out, v_final_norm_w):
    args = dict(locals())
    big = [n for n, _ in BIG]
    w = {n: args[n] for n in SMALL + tuple(big)}
    m = {n: args["m_" + n] for n in w}
    v = {n: args["v_" + n] for n in w}
    xi, yi, ci = lax.axis_index("x"), lax.axis_index("y"), lax.axis_index("c")
    shard = 2 * xi + yi
    big_local = {n: w[n][0] for n in big}

    meta_cols, conv_cols = meta_tokens.shape[1], gd_conv_w.shape[-1]
    sm_all = _gather_all(_pack_lanes([meta_tokens, gd_conv_w[0]]), "gather_meta")
    sm_parts = [_unpack_lanes(sm_all[2 * s], [meta_tokens.shape, gd_conv_w[0].shape]) for s in range(4)]
    meta_full = jnp.concatenate([p[0] for p in sm_parts], axis=1)
    conv_full = jnp.concatenate([p[1] for p in sm_parts], axis=1)
    cvec = ci.reshape(1).astype(jnp.int32)
    late = [n for n in big if n != "w_in"]
    rows_full = lambda t: t.reshape(t.shape[0] * t.shape[1], t.shape[2])

    def pair_sums(names, gs):
        return [_add_half(gk, got, cvec, "add_half_" + n) for n, gk, got in zip(names, gs, _swap_halves(gs, "swap_" + names[0]))]

    def with_own(slabs, n):
        return lax.dynamic_update_index_in_dim(slabs, big_local[n].astype(BF16), shard, 0)

    class MeshComm:
        def first_side(self):
            return _gather_side([big_local["w_in"].astype(BF16)])

        def first_weight(self, landed):
            return with_own(_forward_halves(landed, "forward_w_in")[0], "w_in")

        def proj_side(self):
            return _gather_side([big_local[n].astype(BF16) for n in late])

        def late_weights(self, landed):
            wl = {n: with_own(t, n) for n, t in zip(late, _forward_halves(landed, "forward_late"))}
            return (wl["w_branch_a"], wl["w_branch_b"], rows_full(wl["w_out"]), wl["w_ffn_in"], rows_full(wl["w_ffn_out"]))

        def early_grads_side(self, grads):
            self.early = list(grads)
            self.early_parts = pair_sums(self.early, [grads[n] for n in self.early])
            return _scatter_side(self.early_parts)

        def early_grads_done(self, arrived):
            self.early_arrived = arrived

        def last_grad_side(self, dw_in):
            self.last_parts = pair_sums(["w_in"], [dw_in])
            return _scatter_side(self.last_parts)

        def last_grad_done(self, arrived):
            self.last_arrived = arrived

    comm = MeshComm()
    loss, dx, g = _local_step(x[0], loss_target[0], meta_full, lb_logits, mix_norm_w, hg_norm_w, conv_full,
                              gd_a_log, gd_dt_bias, gd_norm_w, ffn_norm_w, final_norm_w, comm)
    loss = lax.psum(loss[0, 0], ("x", "y", "c"))

    parts = dict(zip(comm.early + ["w_in"], comm.early_parts + comm.last_parts))
    arrived = dict(zip(comm.early + ["w_in"], comm.early_arrived + comm.last_arrived))
    g_big = dict(zip(big, _share_halves([_sum_chips(arrived[n], parts[n], "sum_chips_" + n) for n in big])))

    small_shapes = [g[n].shape for n in SMALL]
    g_all = _gather_all(_pack_lanes([g[n] for n in SMALL]), "gather_small")
    g_small = dict(zip(SMALL, _unpack_lanes(_sum_slabs(g_all, "sum_small"), small_shapes)))
    g_small["meta_tokens"] = lax.dynamic_slice_in_dim(g_small["meta_tokens"], shard * meta_cols, meta_cols, axis=1)
    g_small["gd_conv_w"] = lax.dynamic_slice_in_dim(g_small["gd_conv_w"], shard * conv_cols, conv_cols, axis=1)

    grad, delta, new_m, new_v = {}, {}, {}, {}
    for n in big:
        g_, d_, m_, v_ = _adamw(g_big[n], big_local[n], m[n][0], v[n][0], "adamw_" + n)
        grad[n] = g_.reshape(w[n].shape)
        delta[n], new_m[n], new_v[n] = d_.reshape(w[n].shape), m_.reshape(w[n].shape), v_.reshape(w[n].shape)
    local_shapes = [w[n].shape for n in SMALL]
    _, d_, m_, v_ = _adamw(_pack_lanes([g_small[n] for n in SMALL]), _pack_lanes([w[n] for n in SMALL]),
                           _pack_lanes([m[n] for n in SMALL]), _pack_lanes([v[n] for n in SMALL]), "adamw_small")
    for n, gs_, dd, mm, vv in zip(SMALL, [g_small[n] for n in SMALL], _unpack_lanes(d_, local_shapes), _unpack_lanes(m_, local_shapes),
                                  _unpack_lanes(v_, local_shapes)):
        grad[n], delta[n], new_m[n], new_v[n] = gs_.reshape(w[n].shape), dd, mm, vv

    order = ["meta_tokens", "lb_logits", "mix_norm_w", "w_in", "hg_norm_w", "gd_conv_w", "gd_a_log", "gd_dt_bias", "gd_norm_w",
             "w_branch_a", "w_branch_b", "w_out", "ffn_norm_w", "w_ffn_in", "w_ffn_out", "final_norm_w"]
    return (loss, dx[None], *[grad[n] for n in order], *[delta[n] for n in order], *[new_m[n] for n in order],
            *[new_v[n] for n in order])
```
